```python
import math
import jax, jax.numpy as jnp
from jax import lax
import numpy as np

D_MODEL = 2048
BATCH = 8
SEQ = 8192
DEPTH = 1

CTX_LEN = 256
GRID_W = 64
MIX_WIDTH = D_MODEL
POOL_WIDTH = MIX_WIDTH // 2
SSM_WIDTH = MIX_WIDTH - POOL_WIDTH
POOL_WINDOWS = (2, 4, 8, 16)
POOL_GROUPS = len(POOL_WINDOWS)
POOL_GROUP_W = POOL_WIDTH // POOL_GROUPS
SSM_GROUP_CH = 16
SSM_GROUPS = SSM_WIDTH // SSM_GROUP_CH
SSM_STATE = 64
DT_MIN = 1e-3
DT_MAX = 1e-1
EPS = 1e-6

kernel_name = "hybrid_pool_s5_prefix_dit_block"


def _rmsnorm(v, g):
    v32 = v.astype(jnp.float32)
    out = v32 * lax.rsqrt(jnp.mean(v32 * v32, axis=-1, keepdims=True) + EPS)
    return (out * g.astype(jnp.float32)).astype(v.dtype)


def _centred_window_mean(v, window, axis):
    n = v.shape[axis]
    pad = [(0, 0)] * v.ndim
    pad[axis] = (1, 0)
    cs = jnp.pad(jnp.cumsum(v.astype(jnp.float32), axis=axis), pad)
    t = jnp.arange(n)
    lo = jnp.clip(t - window // 2, 0, n)
    hi = jnp.clip(t + window - window // 2, 0, n)
    s = jnp.take(cs, hi, axis=axis) - jnp.take(cs, lo, axis=axis)
    shape = [1] * v.ndim
    shape[axis] = n
    cnt = (hi - lo).astype(jnp.float32).reshape(shape)
    return (s / cnt).astype(v.dtype)


def _pool_mix(u, pool_w, pool_scale, axes):
    outs = []
    for g, w in enumerate(POOL_WINDOWS):
        ug = u[..., g * POOL_GROUP_W:(g + 1) * POOL_GROUP_W]
        m = ug
        for ax in axes:
            m = _centred_window_mean(m, w, ax)
        outs.append(jnp.einsum('...c,cd->...d', m - ug, pool_w[g]))
    return jnp.concatenate(outs, axis=-1) * pool_scale


def _linear_combine(left, right):
    a_l, b_l = left
    a_r, b_r = right
    return a_l * a_r, a_r * b_l + b_r


def _s5_scan(u, lam_re, lam_im, log_dt, b_re, b_im, h0, reverse):
    bsz, n = u.shape[0], u.shape[1]
    lam = lax.complex(lam_re.astype(jnp.float32), lam_im.astype(jnp.float32))
    dt = jnp.exp(log_dt.astype(jnp.float32))[:, None]
    a_bar = jnp.exp(lam * dt)
    b = lax.complex(b_re.astype(jnp.float32), b_im.astype(jnp.float32))
    b_bar = ((a_bar - 1.0) / lam)[..., None] * b
    ug = u.astype(jnp.float32).reshape(bsz, n, SSM_GROUPS, SSM_GROUP_CH)
    bu = lax.complex(jnp.einsum('blgh,gnh->blgn', ug, b_bar.real),
                     jnp.einsum('blgh,gnh->blgn', ug, b_bar.imag))
    first = n - 1 if reverse else 0
    bu = bu.at[:, first].add(a_bar * h0)
    a = jnp.broadcast_to(a_bar, bu.shape)
    _, hs = lax.associative_scan(_linear_combine, (a, bu), reverse=reverse, axis=1)
    return hs


def _s5_readout(hs, c_re, c_im):
    y = (jnp.einsum('blgn,ghn->blgh', hs.real, c_re.astype(jnp.float32))
         - jnp.einsum('blgn,ghn->blgh', hs.imag, c_im.astype(jnp.float32)))
    return y.reshape(hs.shape[0], hs.shape[1], SSM_WIDTH)


def _merge_branches(pool_out, u_ssm, hs_f, hs_b, z, c_re, c_im, d_skip, glu_w, glu_b, out_w):
    y = (_s5_readout(hs_f, c_re[0], c_im[0]) + _s5_readout(hs_b, c_re[1], c_im[1])).astype(u_ssm.dtype)
    y = jax.nn.gelu(y + d_skip * u_ssm)
    y1, y2 = jnp.split(y @ glu_w + glu_b, 2, axis=-1)
    ssm_out = y1 * jax.nn.sigmoid(y2)
    branch = jnp.concatenate([pool_out, ssm_out], axis=-1) * jax.nn.silu(z)
    return branch @ out_w


def _layer(x, ctx, c, c_ctx, ada_w, ada_b, norm_g, in_w, pool_w, pool_scale,
           lam_re, lam_im, log_dt, b_re, b_im, c_re, c_im, d_skip, glu_w, glu_b, out_w, update_ctx):
    bsz, seq = x.shape[0], x.shape[1]
    rows = seq // GRID_W
    shift, scale, gate = jnp.split(jax.nn.silu(c) @ ada_w + ada_b, 3, axis=-1)
    shift_c, scale_c, gate_c = jnp.split(jax.nn.silu(c_ctx) @ ada_w + ada_b, 3, axis=-1)
    h = _rmsnorm(x, norm_g) * (1.0 + scale[:, None]) + shift[:, None]
    hc = _rmsnorm(ctx, norm_g) * (1.0 + scale_c) + shift_c

    uc_ssm = hc @ in_w[:, POOL_WIDTH:MIX_WIDTH]
    zeros_h = jnp.zeros((bsz, SSM_GROUPS, SSM_STATE), jnp.complex64)
    hs_cf = _s5_scan(uc_ssm, lam_re[0], lam_im[0], log_dt[0], b_re[0], b_im[0], zeros_h, False)
    hs_cb = _s5_scan(uc_ssm, lam_re[1], lam_im[1], log_dt[1], b_re[1], b_im[1], zeros_h, True)
    if update_ctx:
        uc_pool = hc @ in_w[:, :POOL_WIDTH]
        zc = hc @ in_w[:, MIX_WIDTH:]
        pool_c = _pool_mix(uc_pool, pool_w, pool_scale, (1,))
        mix_c = _merge_branches(pool_c, uc_ssm, hs_cf, hs_cb, zc, c_re, c_im, d_skip, glu_w, glu_b, out_w)
        ctx = ctx + gate_c * mix_c

    proj = h @ in_w
    u_pool = proj[..., :POOL_WIDTH]
    u_ssm = proj[..., POOL_WIDTH:MIX_WIDTH]
    z = proj[..., MIX_WIDTH:]
    grid = u_pool.reshape(bsz, rows, GRID_W, POOL_WIDTH)
    pool_out = _pool_mix(grid, pool_w, pool_scale, (1, 2)).reshape(bsz, seq, POOL_WIDTH)
    hs_f = _s5_scan(u_ssm, lam_re[0], lam_im[0], log_dt[0], b_re[0], b_im[0], hs_cf[:, -1], False)
    hs_b = _s5_scan(u_ssm, lam_re[1], lam_im[1], log_dt[1], b_re[1], b_im[1], hs_cb[:, 0], True)
    mix = _merge_branches(pool_out, u_ssm, hs_f, hs_b, z, c_re, c_im, d_skip, glu_w, glu_b, out_w)
    x = x + gate[:, None] * mix
    return x, ctx


def _fwd_setup_inputs(seed: int = 0) -> dict:
    key = jax.random.key(seed)
    k = jax.random.split(key, 24)
    nrm = jax.random.normal
    D, G, N, H = D_MODEL, SSM_GROUPS, SSM_STATE, SSM_GROUP_CH
    n_idx = jnp.arange(N, dtype=jnp.float32)
    return {
        "x": nrm(k[0], (BATCH, SEQ, D), jnp.float32),
        "c": nrm(k[1], (BATCH, D), jnp.float32),
        "ctx": nrm(k[2], (BATCH, CTX_LEN, D), jnp.float32),
        "c_ctx": nrm(k[3], (D,), jnp.float32),
        "ada_w": nrm(k[4], (DEPTH, D, 3 * D), jnp.float32) * (0.5 * D ** -0.5),
        "ada_b": 0.01 * nrm(k[5], (DEPTH, 3 * D), jnp.float32),
        "norm_g": 1.0 + 0.01 * nrm(k[6], (DEPTH, D), jnp.float32),
        "in_w": nrm(k[7], (DEPTH, D, 2 * MIX_WIDTH), jnp.float32) * D ** -0.5,
        "pool_w": nrm(k[8], (DEPTH, POOL_GROUPS, POOL_GROUP_W, POOL_GROUP_W), jnp.float32) * POOL_GROUP_W ** -0.5,
        "pool_scale": 1.0 + 0.1 * nrm(k[9], (DEPTH, POOL_WIDTH), jnp.float32),
        "s5_lam_re": -0.5 + 0.01 * nrm(k[10], (DEPTH, 2, G, N), jnp.float32),
        "s5_lam_im": jnp.pi * n_idx + 0.01 * nrm(k[11], (DEPTH, 2, G, N), jnp.float32),
        "s5_log_dt": jax.random.uniform(k[12], (DEPTH, 2, G), jnp.float32,
                                        minval=math.log(DT_MIN), maxval=math.log(DT_MAX)),
        "s5_b_re": nrm(k[13], (DEPTH, 2, G, N, H), jnp.float32) * (2 * H) ** -0.5,
        "s5_b_im": nrm(k[14], (DEPTH, 2, G, N, H), jnp.float32) * (2 * H) ** -0.5,
        "s5_c_re": nrm(k[15], (DEPTH, 2, G, H, N), jnp.float32) * N ** -0.5,
        "s5_c_im": nrm(k[16], (DEPTH, 2, G, H, N), jnp.float32) * N ** -0.5,
        "s5_d": nrm(k[17], (DEPTH, SSM_WIDTH), jnp.float32),
        "glu_w": nrm(k[18], (DEPTH, SSM_WIDTH, 2 * SSM_WIDTH), jnp.float32) * SSM_WIDTH ** -0.5,
        "glu_b": 0.01 * nrm(k[19], (DEPTH, 2 * SSM_WIDTH), jnp.float32),
        "out_w": nrm(k[20], (DEPTH, MIX_WIDTH, D), jnp.float32) * MIX_WIDTH ** -0.5,
        "final_g": 1.0 + 0.01 * nrm(k[21], (D,), jnp.float32),
    }


def _fwd_reference(x, c, ctx, c_ctx, ada_w, ada_b, norm_g, in_w, pool_w, pool_scale,
              s5_lam_re, s5_lam_im, s5_log_dt, s5_b_re, s5_b_im, s5_c_re, s5_c_im, s5_d,
              glu_w, glu_b, out_w, final_g):
    for i in range(DEPTH):
        x, ctx = _layer(x, ctx, c, c_ctx, ada_w[i], ada_b[i], norm_g[i], in_w[i], pool_w[i], pool_scale[i],
                        s5_lam_re[i], s5_lam_im[i], s5_log_dt[i], s5_b_re[i], s5_b_im[i],
                        s5_c_re[i], s5_c_im[i], s5_d[i], glu_w[i], glu_b[i], out_w[i],
                        update_ctx=(i < DEPTH - 1))
    return _rmsnorm(x, final_g)


import jax as _jax
import jax.numpy as _jnp

TWIN_FORMAT = 'train_step'
FWD_PARAMS = ['x', 'c', 'ctx', 'c_ctx', 'ada_w', 'ada_b', 'norm_g', 'in_w', 'pool_w', 'pool_scale', 's5_lam_re', 's5_lam_im', 's5_log_dt', 's5_b_re', 's5_b_im', 's5_c_re', 's5_c_im', 's5_d', 'glu_w', 'glu_b', 'out_w', 'final_g']
TWIN_WEIGHTS = ['c_ctx', 'ada_w', 'ada_b', 'norm_g', 'in_w', 'pool_w', 'pool_scale', 's5_lam_re', 's5_lam_im', 's5_log_dt', 's5_b_re', 's5_b_im', 's5_c_re', 's5_c_im', 's5_d', 'glu_w', 'glu_b', 'out_w', 'final_g']
TWIN_DIFF_INPUT = 'x'
TWIN_INPUTS = ['x', 'c', 'ctx', 'c_ctx', 'ada_w', 'ada_b', 'norm_g', 'in_w', 'pool_w', 'pool_scale', 's5_lam_re', 's5_lam_im', 's5_log_dt', 's5_b_re', 's5_b_im', 's5_c_re', 's5_c_im', 's5_d', 'glu_w', 'glu_b', 'out_w', 'final_g', 'loss_target', 'm_c_ctx', 'm_ada_w', 'm_ada_b', 'm_norm_g', 'm_in_w', 'm_pool_w', 'm_pool_scale', 'm_s5_lam_re', 'm_s5_lam_im', 'm_s5_log_dt', 'm_s5_b_re', 'm_s5_b_im', 'm_s5_c_re', 'm_s5_c_im', 'm_s5_d', 'm_glu_w', 'm_glu_b', 'm_out_w', 'm_final_g', 'v_c_ctx', 'v_ada_w', 'v_ada_b', 'v_norm_g', 'v_in_w', 'v_pool_w', 'v_pool_scale', 'v_s5_lam_re', 'v_s5_lam_im', 'v_s5_log_dt', 'v_s5_b_re', 'v_s5_b_im', 'v_s5_c_re', 'v_s5_c_im', 'v_s5_d', 'v_glu_w', 'v_glu_b', 'v_out_w', 'v_final_g']
TWIN_OUTPUTS = ['loss', 'grad_x', 'grad_c_ctx', 'grad_ada_w', 'grad_ada_b', 'grad_norm_g', 'grad_in_w', 'grad_pool_w', 'grad_pool_scale', 'grad_s5_lam_re', 'grad_s5_lam_im', 'grad_s5_log_dt', 'grad_s5_b_re', 'grad_s5_b_im', 'grad_s5_c_re', 'grad_s5_c_im', 'grad_s5_d', 'grad_glu_w', 'grad_glu_b', 'grad_out_w', 'grad_final_g', 'delta_c_ctx', 'delta_ada_w', 'delta_ada_b', 'delta_norm_g', 'delta_in_w', 'delta_pool_w', 'delta_pool_scale', 'delta_s5_lam_re', 'delta_s5_lam_im', 'delta_s5_log_dt', 'delta_s5_b_re', 'delta_s5_b_im', 'delta_s5_c_re', 'delta_s5_c_im', 'delta_s5_d', 'delta_glu_w', 'delta_glu_b', 'delta_out_w', 'delta_final_g', 'new_m_c_ctx', 'new_m_ada_w', 'new_m_ada_b', 'new_m_norm_g', 'new_m_in_w', 'new_m_pool_w', 'new_m_pool_scale', 'new_m_s5_lam_re', 'new_m_s5_lam_im', 'new_m_s5_log_dt', 'new_m_s5_b_re', 'new_m_s5_b_im', 'new_m_s5_c_re', 'new_m_s5_c_im', 'new_m_s5_d', 'new_m_glu_w', 'new_m_glu_b', 'new_m_out_w', 'new_m_final_g', 'new_v_c_ctx', 'new_v_ada_w', 'new_v_ada_b', 'new_v_norm_g', 'new_v_in_w', 'new_v_pool_w', 'new_v_pool_scale', 'new_v_s5_lam_re', 'new_v_s5_lam_im', 'new_v_s5_log_dt', 'new_v_s5_b_re', 'new_v_s5_b_im', 'new_v_s5_c_re', 'new_v_s5_c_im', 'new_v_s5_d', 'new_v_glu_w', 'new_v_glu_b', 'new_v_out_w', 'new_v_final_g']
TWIN_LEAF_KINDS = {'loss': 'loss', 'grad_x': 'grad_x', 'grad_c_ctx': 'grad_w', 'grad_ada_w': 'grad_w', 'grad_ada_b': 'grad_w', 'grad_norm_g': 'grad_w', 'grad_in_w': 'grad_w', 'grad_pool_w': 'grad_w', 'grad_pool_scale': 'grad_w', 'grad_s5_lam_re': 'grad_w', 'grad_s5_lam_im': 'grad_w', 'grad_s5_log_dt': 'grad_w', 'grad_s5_b_re': 'grad_w', 'grad_s5_b_im': 'grad_w', 'grad_s5_c_re': 'grad_w', 'grad_s5_c_im': 'grad_w', 'grad_s5_d': 'grad_w', 'grad_glu_w': 'grad_w', 'grad_glu_b': 'grad_w', 'grad_out_w': 'grad_w', 'grad_final_g': 'grad_w', 'delta_c_ctx': 'delta_w', 'delta_ada_w': 'delta_w', 'delta_ada_b': 'delta_w', 'delta_norm_g': 'delta_w', 'delta_in_w': 'delta_w', 'delta_pool_w': 'delta_w', 'delta_pool_scale': 'delta_w', 'delta_s5_lam_re': 'delta_w', 'delta_s5_lam_im': 'delta_w', 'delta_s5_log_dt': 'delta_w', 'delta_s5_b_re': 'delta_w', 'delta_s5_b_im': 'delta_w', 'delta_s5_c_re': 'delta_w', 'delta_s5_c_im': 'delta_w', 'delta_s5_d': 'delta_w', 'delta_glu_w': 'delta_w', 'delta_glu_b': 'delta_w', 'delta_out_w': 'delta_w', 'delta_final_g': 'delta_w', 'new_m_c_ctx': 'new_m', 'new_m_ada_w': 'new_m', 'new_m_ada_b': 'new_m', 'new_m_norm_g': 'new_m', 'new_m_in_w': 'new_m', 'new_m_pool_w': 'new_m', 'new_m_pool_scale': 'new_m', 'new_m_s5_lam_re': 'new_m', 'new_m_s5_lam_im': 'new_m', 'new_m_s5_log_dt': 'new_m', 'new_m_s5_b_re': 'new_m', 'new_m_s5_b_im': 'new_m', 'new_m_s5_c_re': 'new_m', 'new_m_s5_c_im': 'new_m', 'new_m_s5_d': 'new_m', 'new_m_glu_w': 'new_m', 'new_m_glu_b': 'new_m', 'new_m_out_w': 'new_m', 'new_m_final_g': 'new_m', 'new_v_c_ctx': 'new_v', 'new_v_ada_w': 'new_v', 'new_v_ada_b': 'new_v', 'new_v_norm_g': 'new_v', 'new_v_in_w': 'new_v', 'new_v_pool_w': 'new_v', 'new_v_pool_scale': 'new_v', 'new_v_s5_lam_re': 'new_v', 'new_v_s5_lam_im': 'new_v', 'new_v_s5_log_dt': 'new_v', 'new_v_s5_b_re': 'new_v', 'new_v_s5_b_im': 'new_v', 'new_v_s5_c_re': 'new_v', 'new_v_s5_c_im': 'new_v', 'new_v_s5_d': 'new_v', 'new_v_glu_w': 'new_v', 'new_v_glu_b': 'new_v', 'new_v_out_w': 'new_v', 'new_v_final_g': 'new_v'}


def _forward(args):
    return _fwd_reference(*[args[k] for k in FWD_PARAMS])


def _output_shape():
    def fwd():
        inp = _fwd_setup_inputs(0)
        return _fwd_reference(*[inp[k] for k in FWD_PARAMS])
    out = _jax.eval_shape(fwd)
    return out.shape, out.dtype

N_MICROBATCH = 1
ADAM_LR = 0.001
ADAM_B1 = 0.9
ADAM_B2 = 0.999
ADAM_EPS = 1e-08
ADAM_WD = 0.01
ADAM_STEP = 10
PER_EXAMPLE_BATCH_AXIS = {'x': 0, 'c': 0, 'ctx': 0, 'loss_target': 0}
SHARED_INPUTS = []
_WEIGHT_DTYPES = {'c_ctx': _jnp.float32, 'ada_w': _jnp.float32, 'ada_b': _jnp.float32, 'norm_g': _jnp.float32, 'in_w': _jnp.float32, 'pool_w': _jnp.float32, 'pool_scale': _jnp.float32, 's5_lam_re': _jnp.float32, 's5_lam_im': _jnp.float32, 's5_log_dt': _jnp.float32, 's5_b_re': _jnp.float32, 's5_b_im': _jnp.float32, 's5_c_re': _jnp.float32, 's5_c_im': _jnp.float32, 's5_d': _jnp.float32, 'glu_w': _jnp.float32, 'glu_b': _jnp.float32, 'out_w': _jnp.float32, 'final_g': _jnp.float32}
MOMENT_SCALE = {'c_ctx': 2.400772e-04, 'ada_w': 2.510062e-02, 'ada_b': 4.091442e-02, 'norm_g': 2.666139e-02, 'in_w': 1.925608e-02, 'pool_w': 2.508080e-02, 'pool_scale': 2.611878e-02, 's5_lam_re': 8.033294e-04, 's5_lam_im': 1.016619e-03, 's5_log_dt': 5.156716e-01, 's5_b_re': 5.431181e-04, 's5_b_im': 5.364229e-04, 's5_c_re': 8.096129e-04, 's5_c_im': 7.890513e-04, 's5_d': 1.051013e-02, 'glu_w': 7.026297e-03, 'glu_b': 9.946897e-03, 'out_w': 1.888389e-02, 'final_g': 3.197613e+01}


def _to_microbatches(a, axis):
    t = _jnp.moveaxis(a, axis, 0)
    t = t.reshape((N_MICROBATCH, t.shape[0] // N_MICROBATCH) + t.shape[1:])
    return _jnp.moveaxis(t, 1, axis + 1)


def setup_inputs(seed: int = 0) -> dict:
    inp = _fwd_setup_inputs(seed)
    key = _jax.random.fold_in(_jax.random.key(seed), 7919)
    shape, _ = _output_shape()
    out = dict(inp)
    out["loss_target"] = _jax.random.normal(_jax.random.fold_in(key, 0), shape, _jnp.float32)
    for i, name in enumerate(TWIN_WEIGHTS):
        w = inp[name].astype(_jnp.float32)
        if MOMENT_SCALE is None:
            s = _jnp.sqrt(_jnp.mean(_jnp.square(w)) + 1e-30)
        else:
            s = MOMENT_SCALE[name]
        km, kv = _jax.random.split(_jax.random.fold_in(key, i + 1))
        out[name] = w
        out["m_" + name] = s * _jax.random.normal(km, w.shape, _jnp.float32)
        out["v_" + name] = (s * s) * _jax.random.uniform(kv, w.shape, _jnp.float32, 0.5, 1.5)
    if N_MICROBATCH > 1:
        for name, axis in PER_EXAMPLE_BATCH_AXIS.items():
            out[name] = _to_microbatches(out[name], axis)
    return {'x': out['x'], 'c': out['c'], 'ctx': out['ctx'], 'c_ctx': out['c_ctx'], 'ada_w': out['ada_w'], 'ada_b': out['ada_b'], 'norm_g': out['norm_g'], 'in_w': out['in_w'], 'pool_w': out['pool_w'], 'pool_scale': out['pool_scale'], 's5_lam_re': out['s5_lam_re'], 's5_lam_im': out['s5_lam_im'], 's5_log_dt': out['s5_log_dt'], 's5_b_re': out['s5_b_re'], 's5_b_im': out['s5_b_im'], 's5_c_re': out['s5_c_re'], 's5_c_im': out['s5_c_im'], 's5_d': out['s5_d'], 'glu_w': out['glu_w'], 'glu_b': out['glu_b'], 'out_w': out['out_w'], 'final_g': out['final_g'], 'loss_target': out['loss_target'], 'm_c_ctx': out['m_c_ctx'], 'm_ada_w': out['m_ada_w'], 'm_ada_b': out['m_ada_b'], 'm_norm_g': out['m_norm_g'], 'm_in_w': out['m_in_w'], 'm_pool_w': out['m_pool_w'], 'm_pool_scale': out['m_pool_scale'], 'm_s5_lam_re': out['m_s5_lam_re'], 'm_s5_lam_im': out['m_s5_lam_im'], 'm_s5_log_dt': out['m_s5_log_dt'], 'm_s5_b_re': out['m_s5_b_re'], 'm_s5_b_im': out['m_s5_b_im'], 'm_s5_c_re': out['m_s5_c_re'], 'm_s5_c_im': out['m_s5_c_im'], 'm_s5_d': out['m_s5_d'], 'm_glu_w': out['m_glu_w'], 'm_glu_b': out['m_glu_b'], 'm_out_w': out['m_out_w'], 'm_final_g': out['m_final_g'], 'v_c_ctx': out['v_c_ctx'], 'v_ada_w': out['v_ada_w'], 'v_ada_b': out['v_ada_b'], 'v_norm_g': out['v_norm_g'], 'v_in_w': out['v_in_w'], 'v_pool_w': out['v_pool_w'], 'v_pool_scale': out['v_pool_scale'], 'v_s5_lam_re': out['v_s5_lam_re'], 'v_s5_lam_im': out['v_s5_lam_im'], 'v_s5_log_dt': out['v_s5_log_dt'], 'v_s5_b_re': out['v_s5_b_re'], 'v_s5_b_im': out['v_s5_b_im'], 'v_s5_c_re': out['v_s5_c_re'], 'v_s5_c_im': out['v_s5_c_im'], 'v_s5_d': out['v_s5_d'], 'v_glu_w': out['v_glu_w'], 'v_glu_b': out['v_glu_b'], 'v_out_w': out['v_out_w'], 'v_final_g': out['v_final_g']}


def _loss(weights, diff, rest, loss_target):
    with _jax.named_scope("forward"):
        args = {**rest, TWIN_DIFF_INPUT: diff, **{k: w.astype(_WEIGHT_DTYPES[k]) for k, w in weights.items()}}
        y = _forward(args)
    with _jax.named_scope("loss_head"):
        err = _jnp.square(y.astype(_jnp.float32) - loss_target)
        return 0.5 * _jnp.sum(_jnp.mean(err, axis=-1)) if err.ndim else 0.5 * err


def _adamw(w, g, m, v):
    m = ADAM_B1 * m + (1.0 - ADAM_B1) * g
    v = ADAM_B2 * v + (1.0 - ADAM_B2) * _jnp.square(g)
    m_hat = m / (1.0 - ADAM_B1 ** ADAM_STEP)
    v_hat = v / (1.0 - ADAM_B2 ** ADAM_STEP)
    delta = -ADAM_LR * (m_hat / (_jnp.sqrt(v_hat) + ADAM_EPS) + ADAM_WD * w)
    return delta, m, v


def reference(x, c, ctx, c_ctx, ada_w, ada_b, norm_g, in_w, pool_w, pool_scale, s5_lam_re, s5_lam_im, s5_log_dt, s5_b_re, s5_b_im, s5_c_re, s5_c_im, s5_d, glu_w, glu_b, out_w, final_g, loss_target, m_c_ctx, m_ada_w, m_ada_b, m_norm_g, m_in_w, m_pool_w, m_pool_scale, m_s5_lam_re, m_s5_lam_im, m_s5_log_dt, m_s5_b_re, m_s5_b_im, m_s5_c_re, m_s5_c_im, m_s5_d, m_glu_w, m_glu_b, m_out_w, m_final_g, v_c_ctx, v_ada_w, v_ada_b, v_norm_g, v_in_w, v_pool_w, v_pool_scale, v_s5_lam_re, v_s5_lam_im, v_s5_log_dt, v_s5_b_re, v_s5_b_im, v_s5_c_re, v_s5_c_im, v_s5_d, v_glu_w, v_glu_b, v_out_w, v_final_g):
    given = dict(x=x, c=c, ctx=ctx, c_ctx=c_ctx, ada_w=ada_w, ada_b=ada_b, norm_g=norm_g, in_w=in_w, pool_w=pool_w, pool_scale=pool_scale, s5_lam_re=s5_lam_re, s5_lam_im=s5_lam_im, s5_log_dt=s5_log_dt, s5_b_re=s5_b_re, s5_b_im=s5_b_im, s5_c_re=s5_c_re, s5_c_im=s5_c_im, s5_d=s5_d, glu_w=glu_w, glu_b=glu_b, out_w=out_w, final_g=final_g, loss_target=loss_target, m_c_ctx=m_c_ctx, m_ada_w=m_ada_w, m_ada_b=m_ada_b, m_norm_g=m_norm_g, m_in_w=m_in_w, m_pool_w=m_pool_w, m_pool_scale=m_pool_scale, m_s5_lam_re=m_s5_lam_re, m_s5_lam_im=m_s5_lam_im, m_s5_log_dt=m_s5_log_dt, m_s5_b_re=m_s5_b_re, m_s5_b_im=m_s5_b_im, m_s5_c_re=m_s5_c_re, m_s5_c_im=m_s5_c_im, m_s5_d=m_s5_d, m_glu_w=m_glu_w, m_glu_b=m_glu_b, m_out_w=m_out_w, m_final_g=m_final_g, v_c_ctx=v_c_ctx, v_ada_w=v_ada_w, v_ada_b=v_ada_b, v_norm_g=v_norm_g, v_in_w=v_in_w, v_pool_w=v_pool_w, v_pool_scale=v_pool_scale, v_s5_lam_re=v_s5_lam_re, v_s5_lam_im=v_s5_lam_im, v_s5_log_dt=v_s5_log_dt, v_s5_b_re=v_s5_b_re, v_s5_b_im=v_s5_b_im, v_s5_c_re=v_s5_c_re, v_s5_c_im=v_s5_c_im, v_s5_d=v_s5_d, v_glu_w=v_glu_w, v_glu_b=v_glu_b, v_out_w=v_out_w, v_final_g=v_final_g)
    weights = {n: given[n] for n in TWIN_WEIGHTS}
    shared = {n: given[n] for n in SHARED_INPUTS}
    per_example = {n: given[n] for n in ['x', 'c', 'ctx']}
    grad_fn = _jax.value_and_grad(_loss, argnums=(0, 1))

    def one_microbatch(ex, loss_target):
        ex = dict(ex)
        diff = ex.pop(TWIN_DIFF_INPUT)
        return grad_fn(weights, diff, {**shared, **ex}, loss_target)

    if N_MICROBATCH == 1:
        loss, (grad_w, grad_x) = one_microbatch(per_example, given["loss_target"])
    else:
        def body(carry, xs):
            loss_sum, grad_sum = carry
            l_k, (gw_k, gx_k) = one_microbatch(xs[0], xs[1])
            with _jax.named_scope("update"):
                return (loss_sum + l_k, _jax.tree.map(_jnp.add, grad_sum, gw_k)), gx_k

        init = (_jnp.zeros((), _jnp.float32), _jax.tree.map(_jnp.zeros_like, weights))
        (loss, grad_w), grad_x = _jax.lax.scan(body, init, (per_example, given["loss_target"]))
    with _jax.named_scope("update"):
        delta_w, new_m, new_v = {}, {}, {}
        for n in TWIN_WEIGHTS:
            delta_w[n], new_m[n], new_v[n] = _adamw(weights[n], grad_w[n], given["m_" + n], given["v_" + n])
    return (loss, grad_x, *[grad_w[n] for n in TWIN_WEIGHTS], *[delta_w[n] for n in TWIN_WEIGHTS],
            *[new_m[n] for n in TWIN_WEIGHTS], *[new_v[n] for n in TWIN_WEIGHTS])
```

```python
import numpy as np

import jax
import jax.numpy as jnp
from jax import lax
from jax.experimental import pallas as pl
from jax.experimental.pallas import tpu as pltpu

F32 = jnp.float32
BF16 = jnp.bfloat16
NDEV = 8
EPS = 1e-6
GRID_W = 64
POOL_WINDOWS = (2, 4, 8, 16)
SSM_H = 16
CHUNK_T = 16
LANES = 128
ADAM_LR, ADAM_B1, ADAM_B2, ADAM_EPS, ADAM_WD, ADAM_STEP = 0.001, 0.9, 0.999, 1e-08, 0.01, 10
VMEM_BIG = 56 << 20
MESH_ID = pl.DeviceIdType.MESH

_HBM = pl.BlockSpec(memory_space=pltpu.HBM)
_ANY = pl.BlockSpec(memory_space=pl.ANY)
_SMEM = pl.BlockSpec(memory_space=pltpu.SMEM)


def _sds(shape, dtype=F32):
    return jax.ShapeDtypeStruct(tuple(shape), dtype)


def _cparams(ngrid=0, vmem=None):
    return pltpu.CompilerParams(
        dimension_semantics=("arbitrary",) * ngrid if ngrid else None, vmem_limit_bytes=vmem
    )


def _rows(tm, c, col=0):
    return pl.BlockSpec((tm, c), lambda i: (i, col))


def _whole(shape):
    nd = len(shape)
    return pl.BlockSpec(tuple(shape), lambda *_: (0,) * nd, pipeline_mode=pl.Buffered(1))


def _acc(shape):
    nd = len(shape)
    return pl.BlockSpec(tuple(shape), lambda *_: (0,) * nd)


def _mm(a, b):
    return jnp.dot(a, b, preferred_element_type=F32)


def _mm_nt(a, b):
    return lax.dot_general(a, b, (((1,), (1,)), ((), ())), preferred_element_type=F32)


def _mm_tn(a, b):
    return lax.dot_general(a, b, (((0,), (0,)), ((), ())), preferred_element_type=F32)


def _mm_split(k01, s):
    hi = s.astype(BF16)
    lo = (s - hi.astype(F32)).astype(BF16)
    return _mm(k01, hi) + _mm(k01, lo)


def _sigmoid(v):
    return 0.5 * (jnp.tanh(0.5 * v) + 1.0)


def _silu(v):
    return v * _sigmoid(v)


_GELU_K = 0.7978845608028654
_GELU_C = 0.044715


def _gelu(v):
    return 0.5 * v * (1.0 + jnp.tanh(_GELU_K * (v + _GELU_C * v * v * v)))


def _gelu_grad(v):
    th = jnp.tanh(_GELU_K * (v + _GELU_C * v * v * v))
    return 0.5 * (1.0 + th) + 0.5 * v * (1.0 - th * th) * (_GELU_K * (1.0 + 3.0 * _GELU_C * v * v))


def _colsum(v):
    return jnp.sum(v, axis=0, keepdims=True)


def _rowmean(v):
    return jnp.mean(v, axis=-1, keepdims=True)


def _my_pos():
    return lax.axis_index("x"), lax.axis_index("y"), lax.axis_index("c")


def _peer(k):
    x, y, c = _my_pos()
    px = 1 - x if (k >> 2) & 1 else x
    py = 1 - y if (k >> 1) & 1 else y
    pc = 1 - c if k & 1 else c
    return (px, py, pc), 4 * px + 2 * py + pc


def _exchange(arrs, name, scatter):
    n = len(arrs)

    def body(*refs):
        ins, outs = refs[:n], refs[n : 2 * n]
        send, recv, loc = refs[2 * n :]
        x, y, c = _my_pos()
        me = 4 * x + 2 * y + c

        def src(a, to):
            return ins[a].at[to] if scatter else ins[a]

        local = [pltpu.make_async_copy(src(a, me), outs[a].at[me], loc.at[a]) for a in range(n)]
        for cp in local:
            cp.start()
        sent = []
        for a in range(n):
            for k in range(1, NDEV):
                peer, pidx = _peer(k)
                cp = pltpu.make_async_remote_copy(
                    src(a, pidx), outs[a].at[me], send.at[a, k - 1], recv.at[a, k - 1], device_id=peer, device_id_type=MESH_ID
                )
                cp.start()
                sent.append(cp)
        for a in range(n):
            for k in range(1, NDEV):
                peer, pidx = _peer(k)
                pltpu.make_async_remote_copy(
                    src(a, pidx), outs[a].at[pidx], send.at[a, k - 1], recv.at[a, k - 1], device_id=peer, device_id_type=MESH_ID
                ).wait_recv()
        for cp in sent:
            cp.wait_send()
        for cp in local:
            cp.wait()

    shapes = [_sds(a.shape if scatter else (NDEV,) + a.shape, a.dtype) for a in arrs]
    return pl.pallas_call(
        body,
        name=name,
        out_shape=shapes,
        in_specs=[_HBM] * n,
        out_specs=[_HBM] * n,
        scratch_shapes=[pltpu.SemaphoreType.DMA((n, NDEV - 1)), pltpu.SemaphoreType.DMA((n, NDEV - 1)), pltpu.SemaphoreType.DMA((n,))],
    )(*arrs)


def _all_gather(arrs, name):
    return _exchange(arrs, name, scatter=False)


def _all_to_all(arrs, name):
    return _exchange(arrs, name, scatter=True)


def _row_tile(r, c, budget=1 << 20):
    best = r
    for t in range(8, r, 8):
        if r % t == 0 and t * c * 4 <= budget:
            best = t
    if r * c * 4 <= budget:
        best = r
    return best


def _sum8(land):
    return land[0] + land[1] + land[2] + land[3] + land[4] + land[5] + land[6] + land[7]


def _adam_math(w, g, m, v):
    m2 = ADAM_B1 * m + (1.0 - ADAM_B1) * g
    v2 = ADAM_B2 * v + (1.0 - ADAM_B2) * (g * g)
    mh = m2 / (1.0 - ADAM_B1**ADAM_STEP)
    vh = v2 / (1.0 - ADAM_B2**ADAM_STEP)
    delta = -ADAM_LR * (mh / (jnp.sqrt(vh) + ADAM_EPS) + ADAM_WD * w)
    return delta, m2, v2


def _adam(g, w, m, v, name, land=None):
    r, c = w.shape
    tr = _row_tile(r, c, budget=(1 << 19) if land is not None else (1 << 20))

    def body(*refs):
        if land is not None:
            l_ref, w_ref, m_ref, v_ref, g_ref, d_ref, m2_ref, v2_ref = refs
            gv = _sum8(l_ref)
            g_ref[...] = gv
        else:
            g_in, w_ref, m_ref, v_ref, d_ref, m2_ref, v2_ref = refs
            gv = g_in[...]
        d, m2, v2 = _adam_math(w_ref[...], gv, m_ref[...], v_ref[...])
        d_ref[...] = d
        m2_ref[...] = m2
        v2_ref[...] = v2

    blk = _rows(tr, c)
    if land is not None:
        in_specs = [pl.BlockSpec((NDEV, tr, c), lambda i: (0, i, 0)), blk, blk, blk]
        out = pl.pallas_call(
            body, name=name, grid=(r // tr,), in_specs=in_specs, out_specs=[blk] * 4, out_shape=[_sds((r, c))] * 4, compiler_params=_cparams(1)
        )(land, w, m, v)
        return out
    out = pl.pallas_call(
        body, name=name, grid=(r // tr,), in_specs=[blk] * 4, out_specs=[blk] * 3, out_shape=[_sds((r, c))] * 3, compiler_params=_cparams(1)
    )(g, w, m, v)
    return (g,) + tuple(out)


def _sum_slots(land, name):
    _, r, c = land.shape
    tr = _row_tile(r, c, budget=1 << 19)

    def body(l_ref, o_ref):
        o_ref[...] = _sum8(l_ref)

    return pl.pallas_call(
        body,
        name=name,
        grid=(r // tr,),
        in_specs=[pl.BlockSpec((NDEV, tr, c), lambda i: (0, i, 0))],
        out_specs=_rows(tr, c),
        out_shape=_sds((r, c)),
        compiler_params=_cparams(1),
    )(land)


def _mod_fwd(call, ada_w, ada_b_loc):
    def body(c_ref, w_ref, b_ref, o_ref):
        s = _silu(c_ref[...]).astype(BF16)
        o_ref[...] = _mm(s, w_ref[...].astype(BF16)) + b_ref[...]

    return pl.pallas_call(
        body, name="mod_fwd", out_shape=_sds((call.shape[0], ada_w.shape[1])), compiler_params=_cparams(0, 32 << 20)
    )(call, ada_w, ada_b_loc)


def _mod_bwd(call, dm_loc, dmc_loc, ada_w):
    d, n = ada_w.shape
    pad = call.shape[0] - NDEV - 1

    def body(c_ref, dm_ref, dmc_ref, w_ref, gw_ref, gb_ref, cp_ref):
        cv = c_ref[...]
        sg = _sigmoid(cv)
        dmc = _colsum(dmc_ref[...])
        dm = dm_ref[...]
        rows = jnp.concatenate([dm, dmc, jnp.zeros((pad, n), F32)], axis=0)
        gw_ref[...] = _mm_tn((cv * sg).astype(BF16), rows.astype(BF16))
        gb_ref[...] = _colsum(dm) + dmc
        back = _mm_nt(rows[NDEV:].astype(BF16), w_ref[...].astype(BF16))
        c8, s8 = cv[NDEV : NDEV + 1], sg[NDEV : NDEV + 1]
        cp_ref[...] = back[0:1] * (s8 * (1.0 + c8 * (1.0 - s8)))

    return pl.pallas_call(
        body, name="mod_bwd", out_shape=[_sds((d, n)), _sds((1, n)), _sds((1, d))], compiler_params=_cparams(0, 40 << 20)
    )(call, dm_loc, dmc_loc, ada_w)


def _in_proj(xr, norm_g, scale, shift, wg, jsel, tm, name):
    lx, d = xr.shape
    cb = wg.shape[2]

    def body(x_ref, g_ref, sc_ref, sh_ref, w_ref, p_ref, h_ref):
        xv = x_ref[...]
        r = lax.rsqrt(_rowmean(xv * xv) + EPS)
        hb = ((xv * r) * g_ref[...] * (1.0 + sc_ref[...]) + sh_ref[...]).astype(BF16)
        h_ref[...] = hb
        for q, j in enumerate(jsel):
            p_ref[:, q * cb : (q + 1) * cb] = _mm(hb, w_ref[j])

    vec = _whole((1, d))
    return pl.pallas_call(
        body,
        name=name,
        grid=(lx // tm,),
        in_specs=[_rows(tm, d), vec, vec, vec, _whole(wg.shape)],
        out_specs=[_rows(tm, len(jsel) * cb), _rows(tm, d)],
        out_shape=[_sds((lx, len(jsel) * cb)), _sds((lx, d), BF16)],
        compiler_params=_cparams(1, VMEM_BIG),
    )(xr, norm_g, scale, shift, wg)


def _in_bwd(xr, dxo, dus, du5, dypre, dz, wg, norm_g, scale, shift, d_skip, dg_init, tm):
    lx, d = xr.shape
    cb = wg.shape[2]
    pgw = dus[0].shape[1]
    sw = du5.shape[1]
    mix = dz.shape[1]
    ncol = NDEV * cb

    def body(x_ref, dxo_ref, u0, u1, u2, u3, d5_ref, dy_ref, dz_ref, w_ref, g_ref, sc_ref, sh_ref, dk_ref, gi_ref,
             gx_ref, dp_ref, dsc_ref, dsh_ref, dg_ref):
        i = pl.program_id(0)

        @pl.when(i == 0)
        def _():
            dsc_ref[...] = jnp.zeros_like(dsc_ref)
            dsh_ref[...] = jnp.zeros_like(dsh_ref)
            dg_ref[...] = gi_ref[...]

        dp = jnp.concatenate(
            [u0[...], u1[...], u2[...], u3[...], d5_ref[...] + dk_ref[...] * dy_ref[...], dz_ref[...]], axis=1
        ).astype(BF16)
        dp_ref[...] = dp
        dh = _mm_nt(dp[:, 0:cb], w_ref[0])
        for j in range(1, NDEV):
            dh = dh + _mm_nt(dp[:, j * cb : (j + 1) * cb], w_ref[j])
        xv = x_ref[...]
        r = lax.rsqrt(_rowmean(xv * xv) + EPS)
        xh = xv * r
        g = g_ref[...]
        one_sc = 1.0 + sc_ref[...]
        dsh_ref[...] += _colsum(dh)
        dsc_ref[...] += _colsum(dh * (xh * g))
        dg_ref[...] += _colsum(dh * one_sc * xh)
        dxh = dh * one_sc * g
        gx_ref[...] = r * (dxh - xh * _rowmean(dxh * xh)) + dxo_ref[...]

    vec = _whole((1, d))
    return pl.pallas_call(
        body,
        name="in_bwd",
        grid=(lx // tm,),
        in_specs=[_rows(tm, d), _rows(tm, d)] + [_rows(tm, pgw)] * 4 + [_rows(tm, sw), _rows(tm, sw), _rows(tm, mix), _whole(wg.shape), vec, vec, vec, _whole((1, sw)), vec],
        out_specs=[_rows(tm, d), _rows(tm, ncol), _acc((1, d)), _acc((1, d)), _acc((1, d))],
        out_shape=[_sds((lx, d)), _sds((lx, ncol), BF16), _sds((1, d)), _sds((1, d)), _sds((1, d))],
        compiler_params=_cparams(1, VMEM_BIG),
    )(xr, dxo, *dus, du5, dypre, dz, wg, norm_g, scale, shift, d_skip, dg_init)


def _in_bwd_ctx(xc, duc, wg, jsel, norm_g, scale, shift):
    lc, d = xc.shape
    cb = wg.shape[2]

    def body(x_ref, du_ref, w_ref, g_ref, sc_ref, sh_ref, dp_ref, dsc_ref, dsh_ref, dg_ref):
        dp = du_ref[...].astype(BF16)
        dp_ref[...] = dp
        dh = _mm_nt(dp[:, 0:cb], w_ref[jsel[0]])
        for q in range(1, len(jsel)):
            dh = dh + _mm_nt(dp[:, q * cb : (q + 1) * cb], w_ref[jsel[q]])
        xv = x_ref[...]
        xh = xv * lax.rsqrt(_rowmean(xv * xv) + EPS)
        dsh_ref[...] = _colsum(dh)
        dsc_ref[...] = _colsum(dh * (xh * g_ref[...]))
        dg_ref[...] = _colsum(dh * (1.0 + sc_ref[...]) * xh)

    return pl.pallas_call(
        body,
        name="in_bwd_ctx",
        out_shape=[_sds(duc.shape, BF16), _sds((1, d)), _sds((1, d)), _sds((1, d))],
        compiler_params=_cparams(0, VMEM_BIG),
    )(xc, duc, wg, norm_g, scale, shift)


def _in_w_grad(hb, dpb, hcb, dpcb, cb, tm):
    lx, d = hb.shape
    lc = hcb.shape[0]
    nb = lx // tm
    cg = 2 * cb

    def body(h_ref, dp_ref, hc_ref, dpc_ref, o_ref):
        j, i = pl.program_id(0), pl.program_id(1)

        @pl.when(i == 0)
        def _():
            o_ref[...] = jnp.zeros_like(o_ref)

        @pl.when(i < nb)
        def _():
            h = h_ref[...]
            o_ref[0] += _mm_tn(h, dp_ref[:, 0:cb])
            o_ref[1] += _mm_tn(h, dp_ref[:, cb:cg])

        @pl.when(jnp.logical_and(i == nb, j == 1))
        def _():
            h = hc_ref[...]
            o_ref[0] += _mm_tn(h, dpc_ref[:, 0:cb])
            o_ref[1] += _mm_tn(h, dpc_ref[:, cb:cg])

    return pl.pallas_call(
        body,
        name="in_w_grad",
        grid=(NDEV // 2, nb + 1),
        in_specs=[
            pl.BlockSpec((tm, d), lambda j, i: (jnp.minimum(i, nb - 1), 0)),
            pl.BlockSpec((tm, cg), lambda j, i: (jnp.minimum(i, nb - 1), j)),
            pl.BlockSpec((lc, d), lambda j, i: (0, 0)),
            pl.BlockSpec((lc, cg), lambda j, i: (0, 0)),
        ],
        out_specs=pl.BlockSpec((2, d, cb), lambda j, i: (j, 0, 0)),
        out_shape=_sds((NDEV, d, cb)),
        compiler_params=_cparams(2, VMEM_BIG),
    )(hb, dpb, hcb, dpcb)


def _pool_tables(w, rows, rb, pgw, transpose):
    t = np.arange(GRID_W)
    lo, hi = np.clip(t - w // 2, 0, GRID_W), np.clip(t + w - w // 2, 0, GRID_W)
    band = ((t[None, :] >= lo[:, None]) & (t[None, :] < hi[:, None])).astype(np.float32)
    if transpose:
        band = band.T
    kc = np.kron(np.eye(rb, dtype=np.float32), band)
    inv_c = np.tile((1.0 / (hi - lo).astype(np.float32))[:, None], (rb, pgw)).astype(np.float32)
    r = np.arange(rows)
    cnt_r = np.clip(r + w - w // 2, 0, rows) - np.clip(r - w // 2, 0, rows)
    inv_r = (1.0 / cnt_r.astype(np.float32)).astype(np.float32)
    return jnp.asarray(kc, BF16), jnp.asarray(inv_r), jnp.asarray(inv_c)


def _pool_fwd(proj, col0, w, pw_g, name):
    lx = proj.shape[0]
    pgw = pw_g.shape[0]
    rows = lx // GRID_W
    rb = min(4, rows)
    tok = rb * GRID_W
    lo = w // 2
    kc, inv_r, inv_c = _pool_tables(w, rows, rb, pgw, False)

    def body(p_hbm, w_ref, kc_ref, ir_ref, ic_ref, lin_ref, dm_ref, xp, sem):
        xp[pl.ds(0, lo * GRID_W), :] = jnp.zeros((lo * GRID_W, pgw), F32)
        xp[pl.ds((lo + rows) * GRID_W, (w - lo) * GRID_W), :] = jnp.zeros(((w - lo) * GRID_W, pgw), F32)
        cp = pltpu.make_async_copy(p_hbm.at[:, pl.ds(col0, pgw)], xp.at[pl.ds(lo * GRID_W, lx), :], sem)
        cp.start()
        cp.wait()

        def blk(b, carry):
            r0 = b * rb
            parts = []
            for rr in range(rb):
                base = pl.multiple_of((r0 + rr) * GRID_W, GRID_W)
                s = xp[pl.ds(base, GRID_W), :]
                for k in range(1, w):
                    s = s + xp[pl.ds(base + k * GRID_W, GRID_W), :]
                parts.append(s * ir_ref[r0 + rr])
            m = _mm_split(kc_ref[...], jnp.concatenate(parts, axis=0)) * ic_ref[...]
            u = xp[pl.ds(pl.multiple_of((r0 + lo) * GRID_W, GRID_W), tok), :]
            db = (m - u).astype(BF16)
            o0 = pl.multiple_of(r0 * GRID_W, GRID_W)
            dm_ref[pl.ds(o0, tok), :] = db
            lin_ref[pl.ds(o0, tok), :] = _mm(db, w_ref[...])
            return carry

        lax.fori_loop(0, rows // rb, blk, 0)

    return pl.pallas_call(
        body,
        name=name,
        in_specs=[_ANY, pl.BlockSpec(memory_space=pltpu.VMEM), pl.BlockSpec(memory_space=pltpu.VMEM), _SMEM, pl.BlockSpec(memory_space=pltpu.VMEM)],
        out_shape=[_sds((lx, pgw)), _sds((lx, pgw), BF16)],
        scratch_shapes=[pltpu.VMEM(((rows + w) * GRID_W, pgw), F32), pltpu.SemaphoreType.DMA],
        compiler_params=_cparams(0, VMEM_BIG),
    )(proj, pw_g, kc, inv_r, inv_c)


def _pool_bwd(dlin, col0, w, pw_g, dmat_g, name):
    lx = dlin.shape[0]
    pgw = pw_g.shape[0]
    rows = lx // GRID_W
    rb = min(4, rows)
    tok = rb * GRID_W
    front = w - w // 2 - 1
    kct, inv_r, inv_c = _pool_tables(w, rows, rb, pgw, True)

    def body(dl_hbm, w_ref, kc_ref, ir_ref, ic_ref, dm_ref, du_ref, dw_ref, tp, dl, sem):
        if front:
            tp[pl.ds(0, front * GRID_W), :] = jnp.zeros((front * GRID_W, pgw), F32)
        tp[pl.ds((front + rows) * GRID_W, (w - front) * GRID_W), :] = jnp.zeros(((w - front) * GRID_W, pgw), F32)
        cp = pltpu.make_async_copy(dl_hbm.at[:, pl.ds(col0, pgw)], dl, sem)
        cp.start()
        cp.wait()
        dw_ref[...] = jnp.zeros_like(dw_ref)

        def blk(b, carry):
            o0 = pl.multiple_of(b * tok, GRID_W)
            dlb = dl[pl.ds(o0, tok), :].astype(BF16)
            dw_ref[...] += _mm_tn(dm_ref[pl.ds(o0, tok), :], dlb)
            dd = _mm_nt(dlb, w_ref[...])
            du_ref[pl.ds(o0, tok), :] = -dd
            t = _mm_split(kc_ref[...], dd * ic_ref[...])
            for rr in range(rb):
                dst = pl.multiple_of((b * rb + rr + front) * GRID_W, GRID_W)
                tp[pl.ds(dst, GRID_W), :] = t[rr * GRID_W : (rr + 1) * GRID_W] * ir_ref[b * rb + rr]
            return carry

        lax.fori_loop(0, rows // rb, blk, 0)

        def rowl(r, carry):
            base = pl.multiple_of(r * GRID_W, GRID_W)
            s = tp[pl.ds(base, GRID_W), :]
            for k in range(1, w):
                s = s + tp[pl.ds(base + k * GRID_W, GRID_W), :]
            du_ref[pl.ds(base, GRID_W), :] += s
            return carry

        lax.fori_loop(0, rows, rowl, 0)

    vm = pl.BlockSpec(memory_space=pltpu.VMEM)
    return pl.pallas_call(
        body,
        name=name,
        in_specs=[_ANY, vm, vm, _SMEM, vm, vm],
        out_shape=[_sds((lx, pgw)), _sds((pgw, pgw))],
        scratch_shapes=[pltpu.VMEM(((rows + w) * GRID_W, pgw), F32), pltpu.VMEM((lx, pgw), F32), pltpu.SemaphoreType.DMA],
        compiler_params=_cparams(0, VMEM_BIG),
    )(dlin, pw_g, kct, inv_r, inv_c, dmat_g)


def _s5_ops(lam_re, lam_im, log_dt, b_re, b_im, c_re, c_im):
    hp = lax.Precision.HIGHEST
    t = CHUNK_T
    g, n = lam_re.shape[1], lam_re.shape[2]
    h = b_re.shape[3]
    dt = jnp.exp(log_dt)[..., None]
    zr, zi = lam_re * dt, lam_im * dt
    tau = jnp.arange(t + 1, dtype=F32)[:, None, None, None]
    mag = jnp.exp(tau * zr)
    pr, pi = mag * jnp.cos(tau * zi), mag * jnp.sin(tau * zi)
    ar, ai = pr[1], pi[1]
    den = lam_re * lam_re + lam_im * lam_im
    qr = ((ar - 1.0) * lam_re + ai * lam_im) / den
    qi = (ai * lam_re - (ar - 1.0) * lam_im) / den
    bbr = qr[..., None] * b_re - qi[..., None] * b_im
    bbi = qr[..., None] * b_im + qi[..., None] * b_re
    car = c_re[None] * pr[:, :, :, None, :] - c_im[None] * pi[:, :, :, None, :]
    cai = c_re[None] * pi[:, :, :, None, :] + c_im[None] * pr[:, :, :, None, :]
    kern = jnp.einsum("tdgon,dgni->dgtoi", car[:t], bbr, precision=hp) - jnp.einsum("tdgon,dgni->dgtoi", cai[:t], bbi, precision=hp)
    idx = np.arange(t)
    lag = idx[None, :] - idx[:, None]
    ef = (lag[None] == idx[:, None, None]).astype(np.float32)
    eb = ((-lag)[None] == idx[:, None, None]).astype(np.float32)
    tsum = jnp.einsum("zst,gzoi->gsito", ef, kern[0], precision=hp) + jnp.einsum("zst,gzoi->gsito", eb, kern[1], precision=hp)
    tsum = tsum.reshape(g, t * h, t * h)
    pfr, pfi = pr[:t][::-1, 0], pi[:t][::-1, 0]
    pbr, pbi = pr[:t, 1], pi[:t, 1]

    def st(pr_, pi_, d):
        re = pr_[..., None] * bbr[d][None] - pi_[..., None] * bbi[d][None]
        im = pr_[..., None] * bbi[d][None] + pi_[..., None] * bbr[d][None]
        lay = lambda v: jnp.transpose(v, (1, 0, 3, 2)).reshape(g, t * h, n)
        return lay(re), lay(im)

    wfr, wfi = st(pfr, pfi, 0)
    wbr, wbi = st(pbr, pbi, 1)
    wst = jnp.concatenate([wfr, wbr, wfi, wbi], axis=2)
    lay = lambda v: jnp.transpose(v, (1, 3, 0, 2)).reshape(g, n, t * h)
    cfr, cfi = lay(car[1:, 0]), lay(cai[1:, 0])
    cbr, cbi = lay(car[1:][::-1, 1]), lay(cai[1:][::-1, 1])
    ccat = jnp.concatenate([cfr, cbr, -cfi, -cbi], axis=1)
    a16 = jnp.concatenate([pr[t, 0], pr[t, 1], pi[t, 0], pi[t, 1]], axis=1)
    return tsum, wst, ccat, a16


def _to_chunks(u):
    lx, sw = u.shape
    g = sw // SSM_H
    return u.reshape(lx // CHUNK_T, CHUNK_T, g, SSM_H).transpose(2, 0, 1, 3).reshape(g, lx // CHUNK_T, CHUNK_T * SSM_H)


def _from_chunks(v):
    g, nc, _ = v.shape
    return v.reshape(g, nc, CHUNK_T, SSM_H).transpose(1, 2, 0, 3).reshape(nc * CHUNK_T, g * SSM_H)


def _gspec(gb, a, b):
    return pl.BlockSpec((gb, a, b), lambda i: (i, 0, 0))


def _s5_state(u_all, wst, gb):
    g, nch, cw = u_all.shape
    ns = wst.shape[2]

    def body(u_ref, w_ref, s_ref):
        for gi in range(gb):
            s_ref[gi] = _mm(u_ref[gi].astype(BF16), w_ref[gi].astype(BF16))

    return pl.pallas_call(
        body,
        name="s5_state",
        grid=(g // gb,),
        in_specs=[_gspec(gb, nch, cw), _gspec(gb, cw, ns)],
        out_specs=_gspec(gb, nch, ns),
        out_shape=_sds((g, nch, ns)),
        compiler_params=_cparams(1, VMEM_BIG),
    )(u_all, wst)


def _lane_masks(gb, half):
    lane = lax.broadcasted_iota(jnp.int32, (gb, half), 1)
    return lane < (half // 2)


def _s5_scan_fwd(s2, a16, ncc, nch, gb):
    g, ns = a16.shape
    half = ns // 2
    st = 2 * nch

    def body(s_ref, a_ref, hf_ref, hb_ref):
        ar, ai = a_ref[:, :half], a_ref[:, half:]
        mf = _lane_masks(gb, half)
        zero = jnp.zeros((gb, half), F32)

        def step(t, c):
            fr, fi, br, bi = c
            rb = jnp.where(t < ncc, ncc - 1 - t, nch - 1 - (t - ncc))
            hf_ref[pl.ds(2 * t, gb, stride=st), :] = jnp.where(mf, fr, 0.0)
            hf_ref[pl.ds(2 * t + 1, gb, stride=st), :] = jnp.where(mf, fi, 0.0)
            hb_ref[pl.ds(2 * rb, gb, stride=st), :] = jnp.where(mf, 0.0, br)
            hb_ref[pl.ds(2 * rb + 1, gb, stride=st), :] = jnp.where(mf, 0.0, bi)
            return (
                ar * fr - ai * fi + s_ref[pl.ds(2 * t, gb, stride=st), :],
                ar * fi + ai * fr + s_ref[pl.ds(2 * t + 1, gb, stride=st), :],
                ar * br - ai * bi + s_ref[pl.ds(2 * rb, gb, stride=st), :],
                ar * bi + ai * br + s_ref[pl.ds(2 * rb + 1, gb, stride=st), :],
            )

        lax.fori_loop(0, nch, step, (zero, zero, zero, zero))

    blk = pl.BlockSpec((gb * st, half), lambda i: (i, 0))
    return pl.pallas_call(
        body,
        name="s5_scan_fwd",
        grid=(g // gb,),
        in_specs=[blk, pl.BlockSpec((gb, ns), lambda i: (i, 0))],
        out_specs=[blk, blk],
        out_shape=[_sds((g * st, half))] * 2,
        compiler_params=_cparams(1, VMEM_BIG),
    )(s2, a16)


def _s5_out(u_all, hf, hb, tsum, ccat, ncc, gb):
    g, nch, cw = u_all.shape
    ns = ccat.shape[1]
    ncl = nch - ncc

    def body(u_ref, hf_ref, hb_ref, t_ref, c_ref, y_ref):
        for gi in range(gb):
            u = u_ref[gi, ncc:, :].astype(BF16)
            hs = (hf_ref[gi, ncc:, :] + hb_ref[gi, ncc:, :]).astype(BF16)
            y_ref[gi] = _mm(u, t_ref[gi].astype(BF16)) + _mm(hs, c_ref[gi].astype(BF16))

    return pl.pallas_call(
        body,
        name="s5_out",
        grid=(g // gb,),
        in_specs=[_gspec(gb, nch, cw), _gspec(gb, nch, ns), _gspec(gb, nch, ns), _gspec(gb, cw, cw), _gspec(gb, ns, cw)],
        out_specs=_gspec(gb, ncl, cw),
        out_shape=_sds((g, ncl, cw)),
        compiler_params=_cparams(1, VMEM_BIG),
    )(u_all, hf, hb, tsum, ccat)


def _s5_dstate(dy, ccat, ncc, gb):
    g, ncl, cw = dy.shape
    ns = ccat.shape[1]
    nch = ncl + ncc

    def body(dy_ref, c_ref, dh_ref):
        for gi in range(gb):
            dh_ref[gi, :ncc, :] = jnp.zeros((ncc, ns), F32)
            dh_ref[gi, ncc:, :] = _mm_nt(dy_ref[gi].astype(BF16), c_ref[gi].astype(BF16))

    return pl.pallas_call(
        body,
        name="s5_dstate",
        grid=(g // gb,),
        in_specs=[_gspec(gb, ncl, cw), _gspec(gb, ns, cw)],
        out_specs=_gspec(gb, nch, ns),
        out_shape=_sds((g, nch, ns)),
        compiler_params=_cparams(1, VMEM_BIG),
    )(dy, ccat)


def _s5_scan_bwd(dh2, hf2, hb2, a16, ncc, nch, gb):
    g, ns = a16.shape
    half = ns // 2
    ncl = nch - ncc
    st = 2 * nch

    def body(dh_ref, hf_ref, hb_ref, a_ref, dsf_ref, dsb_ref, da_ref):
        ar, ai = a_ref[:, :half], a_ref[:, half:]
        mf = _lane_masks(gb, half)
        zero = jnp.zeros((gb, half), F32)

        def one(row, lr, li, accr, acci, h_ref):
            hr = h_ref[pl.ds(2 * row, gb, stride=st), :]
            hi = h_ref[pl.ds(2 * row + 1, gb, stride=st), :]
            return (
                dh_ref[pl.ds(2 * row, gb, stride=st), :] + ar * lr + ai * li,
                dh_ref[pl.ds(2 * row + 1, gb, stride=st), :] + ar * li - ai * lr,
                accr + hr * lr + hi * li,
                acci + hr * li - hi * lr,
            )

        def step(t, c):
            lfr, lfi, afr, afi, lbr, lbi, abr, abi = c
            pf = nch - 1 - t
            pb = jnp.where(t < ncl, ncc + t, t - ncl)
            dsf_ref[pl.ds(2 * pf, gb, stride=st), :] = jnp.where(mf, lfr, 0.0)
            dsf_ref[pl.ds(2 * pf + 1, gb, stride=st), :] = jnp.where(mf, lfi, 0.0)
            dsb_ref[pl.ds(2 * pb, gb, stride=st), :] = jnp.where(mf, 0.0, lbr)
            dsb_ref[pl.ds(2 * pb + 1, gb, stride=st), :] = jnp.where(mf, 0.0, lbi)
            return one(pf, lfr, lfi, afr, afi, hf_ref) + one(pb, lbr, lbi, abr, abi, hb_ref)

        out = lax.fori_loop(0, nch, step, (zero,) * 8)
        da_ref[...] = jnp.concatenate([jnp.where(mf, out[2], out[6]), jnp.where(mf, out[3], out[7])], axis=1)

    blk = pl.BlockSpec((gb * st, half), lambda i: (i, 0))
    ablk = pl.BlockSpec((gb, ns), lambda i: (i, 0))
    return pl.pallas_call(
        body,
        name="s5_scan_bwd",
        grid=(g // gb,),
        in_specs=[blk, blk, blk, ablk],
        out_specs=[blk, blk, ablk],
        out_shape=[_sds((g * st, half)), _sds((g * st, half)), _sds((g, ns))],
        compiler_params=_cparams(1, VMEM_BIG),
    )(dh2, hf2, hb2, a16)


def _s5_grads(u_all, dy, dsf, dsb, hf, hb, tsum, wst, ncc, gb):
    g, nch, cw = u_all.shape
    ns = wst.shape[2]
    ncl = nch - ncc

    def body(u_ref, dy_ref, dsf_ref, dsb_ref, hf_ref, hb_ref, t_ref, w_ref, du_ref, dw_ref, dt_ref, dc_ref):
        for gi in range(gb):
            ds = (dsf_ref[gi] + dsb_ref[gi]).astype(BF16)
            ub = u_ref[gi].astype(BF16)
            dyb = dy_ref[gi].astype(BF16)
            du_ref[gi] = _mm_nt(ds, w_ref[gi].astype(BF16))
            du_ref[gi, ncc:, :] += _mm_nt(dyb, t_ref[gi].astype(BF16))
            dw_ref[gi] = _mm_tn(ub, ds)
            dt_ref[gi] = _mm_tn(u_ref[gi, ncc:, :].astype(BF16), dyb)
            hs = (hf_ref[gi, ncc:, :] + hb_ref[gi, ncc:, :]).astype(BF16)
            dc_ref[gi] = _mm_tn(hs, dyb)

    return pl.pallas_call(
        body,
        name="s5_grads",
        grid=(g // gb,),
        in_specs=[_gspec(gb, nch, cw), _gspec(gb, ncl, cw), _gspec(gb, nch, ns), _gspec(gb, nch, ns), _gspec(gb, nch, ns), _gspec(gb, nch, ns), _gspec(gb, cw, cw), _gspec(gb, cw, ns)],
        out_specs=[_gspec(gb, nch, cw), _gspec(gb, cw, ns), _gspec(gb, cw, cw), _gspec(gb, ns, cw)],
        out_shape=[_sds((g, nch, cw)), _sds((g, cw, ns)), _sds((g, cw, cw)), _sds((g, ns, cw))],
        compiler_params=_cparams(1, VMEM_BIG),
    )(u_all, dy, dsf, dsb, hf, hb, tsum, wst)


def _merge_fwd(y_ssm, proj, lins, x, tgt, glu_wg, out_w, d_skip, glu_b, pscale, gate, fin_g, tm):
    lx, d = x.shape
    sw = y_ssm.shape[1]
    mix = out_w.shape[0]
    pw = mix - sw
    pgw = lins[0].shape[1]
    cb2 = glu_wg.shape[2]
    nh = NDEV // 2

    def body(y_ref, u_ref, z_ref, l0, l1, l2, l3, x_ref, t_ref, gw_ref, ow_ref, dk_ref, gb_ref, ps_ref, gt_ref, fg_ref,
             ypre_ref, y12_ref, br_ref, dxo_ref, loss_ref, dfg_ref, dgt_ref):
        @pl.when(pl.program_id(0) == 0)
        def _():
            loss_ref[...] = jnp.zeros_like(loss_ref)
            dfg_ref[...] = jnp.zeros_like(dfg_ref)
            dgt_ref[...] = jnp.zeros_like(dgt_ref)

        ypre = y_ref[...] + dk_ref[...] * u_ref[...]
        ypre_ref[...] = ypre
        yg = _gelu(ypre).astype(BF16)
        outs = []
        for j in range(nh):
            y1 = _mm(yg, gw_ref[j]) + gb_ref[:, j * cb2 : (j + 1) * cb2]
            y2 = _mm(yg, gw_ref[nh + j]) + gb_ref[:, (nh + j) * cb2 : (nh + j + 1) * cb2]
            y12_ref[:, j * cb2 : (j + 1) * cb2] = y1
            y12_ref[:, (nh + j) * cb2 : (nh + j + 1) * cb2] = y2
            outs.append(y1 * _sigmoid(y2))
        lin = jnp.concatenate([l0[...], l1[...], l2[...], l3[...]], axis=1) * ps_ref[...]
        brb = (jnp.concatenate([lin] + outs, axis=1) * _silu(z_ref[...])).astype(BF16)
        br_ref[...] = brb
        mixv = _mm(brb, ow_ref[...])
        xo = x_ref[...] + gt_ref[...] * mixv
        r2 = lax.rsqrt(_rowmean(xo * xo) + EPS)
        xh = xo * r2
        fg = fg_ref[...]
        e = xh * fg - t_ref[...]
        loss_ref[...] += 0.5 * _colsum(_rowmean(e * e))
        dy = e * (1.0 / d)
        dfg_ref[...] += _colsum(dy * xh)
        gy = dy * fg
        dxo = r2 * (gy - xh * _rowmean(gy * xh))
        dxo_ref[...] = dxo
        dgt_ref[...] += _colsum(dxo * mixv)

    vec = _whole((1, d))
    return pl.pallas_call(
        body,
        name="merge_fwd",
        grid=(lx // tm,),
        in_specs=[_rows(tm, sw), _rows(tm, sw, 1), _rows(tm, mix, 1)] + [_rows(tm, pgw)] * 4 + [_rows(tm, d), _rows(tm, d), _whole(glu_wg.shape), _whole(out_w.shape), _whole((1, sw)), _whole((1, 2 * sw)), _whole((1, pw)), vec, vec],
        out_specs=[_rows(tm, sw), _rows(tm, 2 * sw), _rows(tm, mix), _rows(tm, d), _acc((1, 1)), _acc((1, d)), _acc((1, d))],
        out_shape=[_sds((lx, sw)), _sds((lx, 2 * sw)), _sds((lx, mix), BF16), _sds((lx, d)), _sds((1, 1)), _sds((1, d)), _sds((1, d))],
        compiler_params=_cparams(1, VMEM_BIG),
    )(y_ssm, proj, proj, *lins, x, tgt, glu_wg, out_w, d_skip, glu_b, pscale, gate, fin_g)


def _out_bwd(dxo, gate, branch, out_w, lins, y12, proj, pscale, tm):
    lx, d = dxo.shape
    mix = out_w.shape[0]
    pgw = lins[0].shape[1]
    pw = 4 * pgw
    sw = mix - pw
    nb = lx // tm

    def body(dxo_ref, gt_ref, br_ref, ow_ref, l0, l1, l2, l3, y_ref, z_ref, ps_ref, dz_ref, dl_ref, dg_ref, dps_ref, dow_hbm, acc, sem):
        i = pl.program_id(0)

        @pl.when(i == 0)
        def _():
            acc[...] = jnp.zeros_like(acc)
            dps_ref[...] = jnp.zeros_like(dps_ref)

        dmix = (dxo_ref[...] * gt_ref[...]).astype(BF16)
        dbr = _mm_nt(dmix, ow_ref[...])
        acc[...] += _mm_tn(br_ref[...], dmix)
        z = z_ref[...]
        sg = _sigmoid(z)
        dbp = dbr * (z * sg)
        y1, y2 = y_ref[:, :sw], y_ref[:, sw:]
        s2 = _sigmoid(y2)
        ps = ps_ref[...]
        lin = jnp.concatenate([l0[...], l1[...], l2[...], l3[...]], axis=1)
        bp = jnp.concatenate([lin * ps, y1 * s2], axis=1)
        dz_ref[...] = dbr * bp * (sg * (1.0 + z * (1.0 - sg)))
        dlp = dbp[:, :pw]
        dl_ref[...] = dlp * ps
        dps_ref[...] += _colsum(dlp * lin)
        dss = dbp[:, pw:]
        dg_ref[...] = jnp.concatenate([dss * s2, dss * y1 * s2 * (1.0 - s2)], axis=1).astype(BF16)

        @pl.when(i == nb - 1)
        def _():
            cp = pltpu.make_async_copy(acc, dow_hbm, sem)
            cp.start()
            cp.wait()

    return pl.pallas_call(
        body,
        name="out_bwd",
        grid=(nb,),
        in_specs=[_rows(tm, d), _whole((1, d)), _rows(tm, mix), _whole(out_w.shape)] + [_rows(tm, pgw)] * 4 + [_rows(tm, 2 * sw), _rows(tm, mix, 1), _whole((1, pw))],
        out_specs=[_rows(tm, mix), _rows(tm, pw), _rows(tm, 2 * sw), _acc((1, pw)), _ANY],
        out_shape=[_sds((lx, mix)), _sds((lx, pw)), _sds((lx, 2 * sw), BF16), _sds((1, pw)), _sds((mix, d))],
        scratch_shapes=[pltpu.VMEM((mix, d), F32), pltpu.SemaphoreType.DMA],
        compiler_params=_cparams(1, VMEM_BIG),
    )(dxo, gate, branch, out_w, *lins, y12, proj, pscale)


def _glu_bwd(dg12, ypre, proj, glu_wg, d_skip, tm):
    lx, sw = ypre.shape
    cb2 = glu_wg.shape[2]
    nb = lx // tm

    def body(dg_ref, yp_ref, u_ref, gw_ref, dk_ref, dyp_ref, dgb_ref, dd_ref, dgw_hbm, acc, sem):
        i = pl.program_id(0)

        @pl.when(i == 0)
        def _():
            acc[...] = jnp.zeros_like(acc)
            dgb_ref[...] = jnp.zeros_like(dgb_ref)
            dd_ref[...] = jnp.zeros_like(dd_ref)

        ypre = yp_ref[...]
        ygb = _gelu(ypre).astype(BF16)
        dg = dg_ref[...]
        dyg = jnp.zeros((tm, sw), F32)
        for j in range(NDEV):
            dgj = dg[:, j * cb2 : (j + 1) * cb2]
            dyg = dyg + _mm_nt(dgj, gw_ref[j])
            acc[j] += _mm_tn(ygb, dgj)
        dgb_ref[...] += _colsum(dg.astype(F32))
        dyp = dyg * _gelu_grad(ypre)
        dyp_ref[...] = dyp
        dd_ref[...] += _colsum(dyp * u_ref[...])

        @pl.when(i == nb - 1)
        def _():
            cp = pltpu.make_async_copy(acc, dgw_hbm, sem)
            cp.start()
            cp.wait()

    return pl.pallas_call(
        body,
        name="glu_bwd",
        grid=(nb,),
        in_specs=[_rows(tm, 2 * sw), _rows(tm, sw), _rows(tm, sw, 1), _whole(glu_wg.shape), _whole((1, sw))],
        out_specs=[_rows(tm, sw), _acc((1, 2 * sw)), _acc((1, sw)), _ANY],
        out_shape=[_sds((lx, sw)), _sds((1, 2 * sw)), _sds((1, sw)), _sds(glu_wg.shape)],
        scratch_shapes=[pltpu.VMEM(glu_wg.shape, F32), pltpu.SemaphoreType.DMA],
        compiler_params=_cparams(1, VMEM_BIG),
    )(dg12, ypre, proj, glu_wg, d_skip)


def _pack(parts, total):
    flat = jnp.concatenate([p.reshape(-1).astype(F32) for p in parts])
    return jnp.pad(flat, (0, total - flat.shape[0]))


def _unpack(flat, shapes):
    out, off = [], 0
    for s in shapes:
        n = int(np.prod(s)) if len(s) else 1
        out.append(flat[off : off + n].reshape(s))
        off += n
    return out


def kernel(x, c, ctx, c_ctx, ada_w, ada_b, norm_g, in_w, pool_w, pool_scale, s5_lam_re, s5_lam_im, s5_log_dt, s5_b_re, s5_b_im, s5_c_re, s5_c_im, s5_d, glu_w, glu_b, out_w, final_g, loss_target, m_c_ctx, m_ada_w, m_ada_b, m_norm_g, m_in_w, m_pool_w, m_pool_scale, m_s5_lam_re, m_s5_lam_im, m_s5_log_dt, m_s5_b_re, m_s5_b_im, m_s5_c_re, m_s5_c_im, m_s5_d, m_glu_w, m_glu_b, m_out_w, m_final_g, v_c_ctx, v_ada_w, v_ada_b, v_norm_g, v_in_w, v_pool_w, v_pool_scale, v_s5_lam_re, v_s5_lam_im, v_s5_log_dt, v_s5_b_re, v_s5_b_im, v_s5_c_re, v_s5_c_im, v_s5_d, v_glu_w, v_glu_b, v_out_w, v_final_g):
    xr, tgt, xc = x[0], loss_target[0], ctx[0]
    lx, d = xr.shape
    lc = xc.shape[0]
    mix = out_w.shape[1] * NDEV
    sw = glu_w.shape[1]
    pw = mix - sw
    pgw = pw // len(POOL_WINDOWS)
    ngrp = sw // SSM_H
    cb = in_w.shape[2]
    cb2 = glu_w.shape[2]
    nmod = ada_w.shape[2]
    ncc, ncl = lc // CHUNK_T, lx // CHUNK_T
    nch = ncc + ncl
    tm = min(256, lx)
    assert mix == d and pw == sw and 2 * cb == pw and NDEV * cb2 == 2 * sw and lx % GRID_W == 0
    mx, my, mc = _my_pos()
    me = 4 * mx + 2 * my + mc

    (c_all,) = _all_gather([c], "gather_c")
    call = jnp.concatenate([c_all.reshape(NDEV, d), c_ctx.reshape(1, d), jnp.zeros((NDEV - 1, d), F32)], axis=0)
    ada_w_l = ada_w[0]
    ada_b_l = lax.dynamic_slice_in_dim(ada_b, me * nmod, nmod, axis=1)
    m_loc = _mod_fwd(call, ada_w_l, ada_b_l)
    (m_all,) = _all_gather([m_loc], "gather_mod")
    mod = lax.dynamic_index_in_dim(m_all, me, axis=1, keepdims=False).reshape(1, NDEV * nmod)
    mod_c = m_all[:, NDEV, :].reshape(1, NDEV * nmod)
    shift, scale, gate = mod[:, :d], mod[:, d : 2 * d], mod[:, 2 * d :]
    shift_c, scale_c = mod_c[:, :d], mod_c[:, d : 2 * d]

    in_wg, glu_wg, out_wg, pool_wg = _all_gather(
        [in_w[0].astype(BF16), glu_w[0].astype(BF16), out_w[0].astype(BF16), pool_w[0].reshape(-1, pgw).astype(BF16)], "gather_weights"
    )
    out_wf = out_wg.reshape(mix, d)
    pool_wf = pool_wg.reshape(NDEV, len(POOL_WINDOWS), pgw // NDEV, pgw).transpose(1, 0, 2, 3).reshape(len(POOL_WINDOWS), pgw, pgw)

    ssm_blocks = (2, 3)
    proj, hb = _in_proj(xr, norm_g, scale, shift, in_wg, tuple(range(NDEV)), tm, "in_proj")
    uc, hcb = _in_proj(xc, norm_g, scale_c, shift_c, in_wg, ssm_blocks, min(tm, lc), "in_proj_ctx")
    lins, dmats = [], []
    for gi, w in enumerate(POOL_WINDOWS):
        lin_g, dmat_g = _pool_fwd(proj, gi * pgw, w, pool_wf[gi], f"pool_fwd{gi}")
        lins.append(lin_g)
        dmats.append(dmat_g)

    s5_params = tuple(p[0] for p in (s5_lam_re, s5_lam_im, s5_log_dt, s5_b_re, s5_b_im, s5_c_re, s5_c_im))
    (tsum, wst, ccat, a16), ops_vjp = jax.vjp(_s5_ops, *s5_params)
    u_all = jnp.concatenate([_to_chunks(uc), _to_chunks(proj[:, pw : pw + sw])], axis=1)
    gb_scan = min(8, ngrp)
    gb_mm = min(4, ngrp)
    ns = wst.shape[2]
    s_all = _s5_state(u_all, wst, gb_mm)
    hf2, hb2 = _s5_scan_fwd(s_all.reshape(ngrp * nch * 2, ns // 2), a16, ncc, nch, gb_scan)
    hf3, hb3 = hf2.reshape(ngrp, nch, ns), hb2.reshape(ngrp, nch, ns)
    y_ssm = _from_chunks(_s5_out(u_all, hf3, hb3, tsum, ccat, ncc, gb_mm))

    ypre, y12, branch, dxo, loss_l, dfg, dgate = _merge_fwd(
        y_ssm, proj, lins, xr, tgt, glu_wg, out_wf, s5_d, glu_b, pool_scale, gate, final_g.reshape(1, d), tm
    )

    dz, dlin, dg12, dps, d_out_w = _out_bwd(dxo, gate, branch, out_wf, lins, y12, proj, pool_scale, tm)
    dypre, dglu_b, dd_skip, d_glu_w = _glu_bwd(dg12, ypre, proj, glu_wg, s5_d, tm)

    dy_r = _to_chunks(dypre)
    dh3 = _s5_dstate(dy_r, ccat, ncc, gb_mm)
    dsf2, dsb2, da16 = _s5_scan_bwd(dh3.reshape(ngrp * nch * 2, ns // 2), hf2, hb2, a16, ncc, nch, gb_scan)
    gb_g = min(2, ngrp)
    du_all, dwst, dtsum, dccat = _s5_grads(
        u_all, dy_r, dsf2.reshape(ngrp, nch, ns), dsb2.reshape(ngrp, nch, ns), hf3, hb3, tsum, wst, ncc, gb_g
    )
    ds5 = ops_vjp((dtsum, dwst, dccat, da16))
    duc = _from_chunks(du_all[:, :ncc])
    du5 = _from_chunks(du_all[:, ncc:])

    dus, dpool_w = [], []
    for gi, w in enumerate(POOL_WINDOWS):
        du_g, dw_g = _pool_bwd(dlin, gi * pgw, w, pool_wf[gi], dmats[gi], f"pool_bwd{gi}")
        dus.append(du_g)
        dpool_w.append(dw_g)

    dpcb, dsc_c, dsh_c, dg_c = _in_bwd_ctx(xc, duc, in_wg, ssm_blocks, norm_g, scale_c, shift_c)
    grad_x, dpb, dsc, dsh, dnorm_g = _in_bwd(xr, dxo, dus, du5, dypre, dz, in_wg, norm_g, scale, shift, s5_d, dg_c, tm)
    d_in_w = _in_w_grad(hb, dpb, hcb, dpcb, cb, tm)

    dmod = jnp.concatenate(
        [jnp.concatenate([dsh, dsc, dgate], axis=1), jnp.concatenate([dsh_c, dsc_c, jnp.zeros((1, d), F32)], axis=1)], axis=0
    )
    (dmod_all,) = _all_gather([dmod], "gather_dmod")
    dmod_l = lax.dynamic_slice_in_dim(dmod_all, me * nmod, nmod, axis=2)
    g_ada_w, g_ada_b_l, cctx_part = _mod_bwd(call, dmod_l[:, 0, :], dmod_l[:, 1, :], ada_w_l)

    npw = len(POOL_WINDOWS)
    dpool_s = jnp.stack(dpool_w).reshape(npw, NDEV, pgw // NDEV, pgw).transpose(1, 0, 2, 3).reshape(NDEV, npw * (pgw // NDEV), pgw)
    land_in, land_glu, land_out, land_pool = _all_to_all(
        [d_in_w, d_glu_w, d_out_w.reshape(NDEV, mix // NDEV, d), dpool_s], "scatter_grads"
    )

    small_names = ["norm_g", "pool_scale", "s5_lam_re", "s5_lam_im", "s5_log_dt", "s5_b_re", "s5_b_im", "s5_c_re", "s5_c_im", "s5_d", "glu_b", "final_g", "c_ctx"]
    small_w = dict(norm_g=norm_g, pool_scale=pool_scale, s5_lam_re=s5_lam_re, s5_lam_im=s5_lam_im, s5_log_dt=s5_log_dt, s5_b_re=s5_b_re, s5_b_im=s5_b_im, s5_c_re=s5_c_re, s5_c_im=s5_c_im, s5_d=s5_d, glu_b=glu_b, final_g=final_g, c_ctx=c_ctx)
    small_m = dict(norm_g=m_norm_g, pool_scale=m_pool_scale, s5_lam_re=m_s5_lam_re, s5_lam_im=m_s5_lam_im, s5_log_dt=m_s5_log_dt, s5_b_re=m_s5_b_re, s5_b_im=m_s5_b_im, s5_c_re=m_s5_c_re, s5_c_im=m_s5_c_im, s5_d=m_s5_d, glu_b=m_glu_b, final_g=m_final_g, c_ctx=m_c_ctx)
    small_v = dict(norm_g=v_norm_g, pool_scale=v_pool_scale, s5_lam_re=v_s5_lam_re, s5_lam_im=v_s5_lam_im, s5_log_dt=v_s5_log_dt, s5_b_re=v_s5_b_re, s5_b_im=v_s5_b_im, s5_c_re=v_s5_c_re, s5_c_im=v_s5_c_im, s5_d=v_s5_d, glu_b=v_glu_b, final_g=v_final_g, c_ctx=v_c_ctx)
    small_g = dict(norm_g=dnorm_g, pool_scale=dps, s5_lam_re=ds5[0], s5_lam_im=ds5[1], s5_log_dt=ds5[2], s5_b_re=ds5[3], s5_b_im=ds5[4], s5_c_re=ds5[5], s5_c_im=ds5[6], s5_d=dd_skip, glu_b=dglu_b, final_g=dfg, c_ctx=cctx_part)
    shapes = [small_w[k].shape for k in small_names]
    nsmall = sum(int(np.prod(s)) for s in shapes) + 1
    unit = NDEV * 8 * LANES
    tot = -(-nsmall // unit) * unit
    rows = tot // LANES
    gpack = _pack([small_g[k] for k in small_names] + [loss_l], tot).reshape(NDEV, rows // NDEV, LANES)
    (land_small,) = _all_to_all([gpack], "scatter_small")
    gsum = _sum_slots(land_small, "sum_small")
    (gall,) = _all_gather([gsum], "gather_small")
    (g_ada_b_all,) = _all_gather([g_ada_b_l], "gather_ada_b")
    gflat = gall.reshape(tot)
    loss = gflat[nsmall - 1]

    nab = NDEV * nmod
    tot2 = tot + -(-nab // (8 * LANES)) * (8 * LANES)
    g2 = jnp.concatenate([gflat, jnp.pad(g_ada_b_all.reshape(nab), (0, tot2 - tot - nab))]).reshape(tot2 // LANES, LANES)

    def pack2(dct, ab):
        return jnp.concatenate(
            [_pack([dct[k] for k in small_names] + [jnp.zeros((1,), F32)], tot), jnp.pad(ab.reshape(nab), (0, tot2 - tot - nab))]
        ).reshape(tot2 // LANES, LANES)

    _, d2, m2, v2 = _adam(g2, pack2(small_w, ada_b), pack2(small_m, m_ada_b), pack2(small_v, v_ada_b), "adam_small")

    def unpack2(flat2):
        flat2 = flat2.reshape(tot2)
        vals = dict(zip(small_names, _unpack(flat2, shapes)))
        vals["ada_b"] = flat2[tot : tot + nab].reshape(ada_b.shape)
        return vals

    res = {k: unpack2(a) for k, a in (("g", g2), ("d", d2), ("m", m2), ("v", v2))}

    def shard(name, g, land, w, m, v):
        shp = w.shape
        w2, m2_, v2_ = (a.reshape(-1, shp[-1]) for a in (w, m, v))
        out = _adam(g, w2, m2_, v2_, "adam_" + name, land=land)
        for k, a in zip(("g", "d", "m", "v"), out):
            res[k][name] = a.reshape(shp)

    shard("ada_w", g_ada_w, None, ada_w, m_ada_w, v_ada_w)
    shard("in_w", None, land_in, in_w, m_in_w, v_in_w)
    shard("pool_w", None, land_pool, pool_w, m_pool_w, v_pool_w)
    shard("glu_w", None, land_glu, glu_w, m_glu_w, v_glu_w)
    shard("out_w", None, land_out, out_w, m_out_w, v_out_w)

    names = ["c_ctx", "ada_w", "ada_b", "norm_g", "in_w", "pool_w", "pool_scale", "s5_lam_re", "s5_lam_im", "s5_log_dt", "s5_b_re", "s5_b_im", "s5_c_re", "s5_c_im", "s5_d", "glu_w", "glu_b", "out_w", "final_g"]
    return (loss, grad_x[None], *[res["g"][n] for n in names], *[res["d"][n] for n in names], *[res["m"][n] for n in names], *[res["v"][n] for n in names])
```

```python
import numpy as np

import jax
import jax.numpy as jnp
from jax import lax
from jax.experimental import pallas as pl
from jax.experimental.pallas import tpu as pltpu

F32 = jnp.float32
BF16 = jnp.bfloat16
NDEV = 8
EPS = 1e-6
GRID_W = 64
POOL_WINDOWS = (2, 4, 8, 16)
SSM_H = 16
CHUNK_T = 16
LANES = 128
ADAM_LR, ADAM_B1, ADAM_B2, ADAM_EPS, ADAM_WD, ADAM_STEP = 0.001, 0.9, 0.999, 1e-08, 0.01, 10
VMEM_BIG = 56 << 20
MESH_ID = pl.DeviceIdType.MESH

_HBM = pl.BlockSpec(memory_space=pltpu.HBM)
_ANY = pl.BlockSpec(memory_space=pl.ANY)
_SMEM = pl.BlockSpec(memory_space=pltpu.SMEM)


def _sds(shape, dtype=F32):
    return jax.ShapeDtypeStruct(tuple(shape), dtype)


def _cparams(ngrid=0, vmem=None):
    return pltpu.CompilerParams(
        dimension_semantics=("arbitrary",) * ngrid if ngrid else None, vmem_limit_bytes=vmem
    )


def _rows(tm, c, col=0):
    return pl.BlockSpec((tm, c), lambda i: (i, col))


def _whole(shape):
    nd = len(shape)
    return pl.BlockSpec(tuple(shape), lambda *_: (0,) * nd, pipeline_mode=pl.Buffered(1))


def _acc(shape):
    nd = len(shape)
    return pl.BlockSpec(tuple(shape), lambda *_: (0,) * nd)


def _mm(a, b):
    return jnp.dot(a, b, preferred_element_type=F32)


def _mm_nt(a, b):
    return lax.dot_general(a, b, (((1,), (1,)), ((), ())), preferred_element_type=F32)


def _mm_tn(a, b):
    return lax.dot_general(a, b, (((0,), (0,)), ((), ())), preferred_element_type=F32)


def _mm_split(k01, s):
    hi = s.astype(BF16)
    lo = (s - hi.astype(F32)).astype(BF16)
    return _mm(k01, hi) + _mm(k01, lo)


def _sigmoid(v):
    return 0.5 * (jnp.tanh(0.5 * v) + 1.0)


def _silu(v):
    return v * _sigmoid(v)


_GELU_K = 0.7978845608028654
_GELU_C = 0.044715


def _gelu(v):
    return 0.5 * v * (1.0 + jnp.tanh(_GELU_K * (v + _GELU_C * v * v * v)))


def _gelu_grad(v):
    th = jnp.tanh(_GELU_K * (v + _GELU_C * v * v * v))
    return 0.5 * (1.0 + th) + 0.5 * v * (1.0 - th * th) * (_GELU_K * (1.0 + 3.0 * _GELU_C * v * v))


def _colsum(v):
    return jnp.sum(v, axis=0, keepdims=True)


def _rowmean(v):
    return jnp.mean(v, axis=-1, keepdims=True)


NCHIP = NDEV // 2


def _my_pos():
    return lax.axis_index("x"), lax.axis_index("y"), lax.axis_index("c")


def _other_chips():
    x, y, _ = _my_pos()
    return [(1 - x, y), (x, 1 - y), (1 - x, 1 - y)]


def _remote(src, dst, send, recv, to):
    return pltpu.make_async_remote_copy(src, dst, send, recv, device_id=to, device_id_type=MESH_ID)


def _all_gather(arrs, name):
    n = len(arrs)

    def body(*refs):
        ins, outs = refs[:n], refs[n : 2 * n]
        send, recv, loc = refs[2 * n :]
        x, y, c = _my_pos()
        me, sib = (x, y, c), (x, y, 1 - c)
        chips = _other_chips()

        def slot(a, p):
            return outs[a].at[4 * p[0] + 2 * p[1] + p[2]]

        def copy(a, k, block, to, own=False):
            return _remote(ins[a] if own else slot(a, block), slot(a, block), send.at[a, k], recv.at[a, k], to)

        local = [pltpu.make_async_copy(ins[a], slot(a, me), loc.at[a]) for a in range(n)]
        for cp in local:
            cp.start()
        sent = []
        for a in range(n):
            sent.append(copy(a, 0, me, sib, own=True))
            sent += [copy(a, 1 + j, me, (*chip, c), own=True) for j, chip in enumerate(chips)]
        for cp in sent:
            cp.start()
        for j, chip in enumerate(chips):
            for a in range(n):
                copy(a, 1 + j, (*chip, c), me).wait_recv()
                fwd = copy(a, 4 + j, (*chip, c), sib)
                fwd.start()
                sent.append(fwd)
        for a in range(n):
            copy(a, 0, sib, me).wait_recv()
            for j, chip in enumerate(chips):
                copy(a, 4 + j, (*chip, 1 - c), me).wait_recv()
        for cp in sent:
            cp.wait_send()
        for cp in local:
            cp.wait()

    return pl.pallas_call(
        body,
        name=name,
        out_shape=[_sds((NDEV,) + a.shape, a.dtype) for a in arrs],
        in_specs=[_HBM] * n,
        out_specs=[_HBM] * n,
        scratch_shapes=[pltpu.SemaphoreType.DMA((n, NDEV - 1)), pltpu.SemaphoreType.DMA((n, NDEV - 1)), pltpu.SemaphoreType.DMA((n,))],
    )(*arrs)


def _sibling_swap(arrs, name):
    n = len(arrs)

    def body(*refs):
        ins, outs = refs[:n], refs[n : 2 * n]
        send, recv = refs[2 * n :]
        x, y, c = _my_pos()
        cps = [_remote(ins[a].at[:, 1 - c], outs[a], send.at[a], recv.at[a], (x, y, 1 - c)) for a in range(n)]
        for cp in cps:
            cp.start()
        for cp in cps:
            cp.wait()

    return pl.pallas_call(
        body,
        name=name,
        out_shape=[_sds((a.shape[0],) + a.shape[2:], a.dtype) for a in arrs],
        in_specs=[_HBM] * n,
        out_specs=[_HBM] * n,
        scratch_shapes=[pltpu.SemaphoreType.DMA((n,)), pltpu.SemaphoreType.DMA((n,))],
    )(*arrs)


def _pair_sum(arr, got, wire, name):
    _, _, r, c = arr.shape
    tr = _row_tile(r, c, budget=1 << 20)

    def body(a_ref, g_ref, o_ref):
        o_ref[0] = (a_ref[0, lax.axis_index("c")] + g_ref[0]).astype(wire)

    return pl.pallas_call(
        body,
        name=name,
        grid=(NCHIP, r // tr),
        in_specs=[pl.BlockSpec((1, 2, tr, c), lambda q, i: (q, 0, i, 0)), pl.BlockSpec((1, tr, c), lambda q, i: (q, i, 0))],
        out_specs=pl.BlockSpec((1, tr, c), lambda q, i: (q, i, 0)),
        out_shape=_sds((NCHIP, r, c), wire),
        compiler_params=_cparams(2),
    )(arr, got)


def _chip_exchange(arrs, name):
    n = len(arrs)

    def body(*refs):
        ins, outs = refs[:n], refs[n : 2 * n]
        send, recv, loc = refs[2 * n :]
        x, y, c = _my_pos()
        mine = 2 * x + y
        chips = _other_chips()
        local = [pltpu.make_async_copy(ins[a].at[mine], outs[a].at[mine], loc.at[a]) for a in range(n)]
        for cp in local:
            cp.start()
        sent = []
        for a in range(n):
            for j, (px, py) in enumerate(chips):
                cp = _remote(ins[a].at[2 * px + py], outs[a].at[mine], send.at[a, j], recv.at[a, j], (px, py, c))
                cp.start()
                sent.append(cp)
        for a in range(n):
            for j, (px, py) in enumerate(chips):
                _remote(ins[a].at[mine], outs[a].at[2 * px + py], send.at[a, j], recv.at[a, j], (px, py, c)).wait_recv()
        for cp in sent:
            cp.wait_send()
        for cp in local:
            cp.wait()

    return pl.pallas_call(
        body,
        name=name,
        out_shape=[_sds(a.shape, a.dtype) for a in arrs],
        in_specs=[_HBM] * n,
        out_specs=[_HBM] * n,
        scratch_shapes=[pltpu.SemaphoreType.DMA((n, NCHIP - 1)), pltpu.SemaphoreType.DMA((n, NCHIP - 1)), pltpu.SemaphoreType.DMA((n,))],
    )(*arrs)


def _reduce_scatter(arrs, wires, tag):
    four = [a.reshape((NCHIP, 2) + a.shape[1:]) for a in arrs]
    got = _sibling_swap(four, "swap_" + tag)
    part = [_pair_sum(a, g, w, f"pair_sum_{tag}{i}") for i, (a, g, w) in enumerate(zip(four, got, wires))]
    return _chip_exchange(part, "exchange_" + tag)


def _row_tile(r, c, budget=1 << 20):
    best = r
    for t in range(8, r, 8):
        if r % t == 0 and t * c * 4 <= budget:
            best = t
    if r * c * 4 <= budget:
        best = r
    return best


def _sum_chips(land):
    tot = land[0].astype(F32)
    for q in range(1, NCHIP):
        tot = tot + land[q].astype(F32)
    return tot


def _adam_math(w, g, m, v):
    m2 = ADAM_B1 * m + (1.0 - ADAM_B1) * g
    v2 = ADAM_B2 * v + (1.0 - ADAM_B2) * (g * g)
    mh = m2 / (1.0 - ADAM_B1**ADAM_STEP)
    vh = v2 / (1.0 - ADAM_B2**ADAM_STEP)
    delta = -ADAM_LR * (mh / (jnp.sqrt(vh) + ADAM_EPS) + ADAM_WD * w)
    return delta, m2, v2


def _adam(g, w, m, v, name, land=None):
    r, c = w.shape
    tr = _row_tile(r, c, budget=1 << 20)

    def body(*refs):
        if land is not None:
            l_ref, w_ref, m_ref, v_ref, g_ref, d_ref, m2_ref, v2_ref = refs
            gv = _sum_chips(l_ref)
            g_ref[...] = gv
        else:
            g_in, w_ref, m_ref, v_ref, d_ref, m2_ref, v2_ref = refs
            gv = g_in[...]
        d, m2, v2 = _adam_math(w_ref[...], gv, m_ref[...], v_ref[...])
        d_ref[...] = d
        m2_ref[...] = m2
        v2_ref[...] = v2

    blk = _rows(tr, c)
    if land is not None:
        in_specs = [pl.BlockSpec((NCHIP, tr, c), lambda i: (0, i, 0)), blk, blk, blk]
        out = pl.pallas_call(
            body, name=name, grid=(r // tr,), in_specs=in_specs, out_specs=[blk] * 4, out_shape=[_sds((r, c))] * 4, compiler_params=_cparams(1)
        )(land, w, m, v)
        return out
    out = pl.pallas_call(
        body, name=name, grid=(r // tr,), in_specs=[blk] * 4, out_specs=[blk] * 3, out_shape=[_sds((r, c))] * 3, compiler_params=_cparams(1)
    )(g, w, m, v)
    return (g,) + tuple(out)


def _sum_slots(land, name):
    _, r, c = land.shape
    tr = _row_tile(r, c, budget=1 << 20)

    def body(l_ref, o_ref):
        o_ref[...] = _sum_chips(l_ref)

    return pl.pallas_call(
        body,
        name=name,
        grid=(r // tr,),
        in_specs=[pl.BlockSpec((NCHIP, tr, c), lambda i: (0, i, 0))],
        out_specs=_rows(tr, c),
        out_shape=_sds((r, c)),
        compiler_params=_cparams(1),
    )(land)


def _mod_fwd(call, ada_w, ada_b_loc):
    def body(c_ref, w_ref, b_ref, o_ref):
        s = _silu(c_ref[...]).astype(BF16)
        o_ref[...] = _mm(s, w_ref[...].astype(BF16)) + b_ref[...]

    return pl.pallas_call(
        body, name="mod_fwd", out_shape=_sds((call.shape[0], ada_w.shape[1])), compiler_params=_cparams(0, 32 << 20)
    )(call, ada_w, ada_b_loc)


def _mod_bwd(call, dm_loc, dmc_loc, ada_w):
    d, n = ada_w.shape
    pad = call.shape[0] - NDEV - 1

    def body(c_ref, dm_ref, dmc_ref, w_ref, gw_ref, gb_ref, cp_ref):
        cv = c_ref[...]
        sg = _sigmoid(cv)
        dmc = _colsum(dmc_ref[...])
        dm = dm_ref[...]
        rows = jnp.concatenate([dm, dmc, jnp.zeros((pad, n), F32)], axis=0)
        gw_ref[...] = _mm_tn((cv * sg).astype(BF16), rows.astype(BF16))
        gb_ref[...] = _colsum(dm) + dmc
        back = _mm_nt(rows[NDEV:].astype(BF16), w_ref[...].astype(BF16))
        c8, s8 = cv[NDEV : NDEV + 1], sg[NDEV : NDEV + 1]
        cp_ref[...] = back[0:1] * (s8 * (1.0 + c8 * (1.0 - s8)))

    return pl.pallas_call(
        body, name="mod_bwd", out_shape=[_sds((d, n)), _sds((1, n)), _sds((1, d))], compiler_params=_cparams(0, 40 << 20)
    )(call, dm_loc, dmc_loc, ada_w)


def _in_proj(xr, norm_g, scale, shift, wg, jsel, tm, name):
    lx, d = xr.shape
    cb = wg.shape[2]

    def body(x_ref, g_ref, sc_ref, sh_ref, w_ref, p_ref, h_ref):
        xv = x_ref[...]
        r = lax.rsqrt(_rowmean(xv * xv) + EPS)
        hb = ((xv * r) * g_ref[...] * (1.0 + sc_ref[...]) + sh_ref[...]).astype(BF16)
        h_ref[...] = hb
        for q, j in enumerate(jsel):
            p_ref[:, q * cb : (q + 1) * cb] = _mm(hb, w_ref[j])

    vec = _whole((1, d))
    return pl.pallas_call(
        body,
        name=name,
        grid=(lx // tm,),
        in_specs=[_rows(tm, d), vec, vec, vec, _whole(wg.shape)],
        out_specs=[_rows(tm, len(jsel) * cb), _rows(tm, d)],
        out_shape=[_sds((lx, len(jsel) * cb)), _sds((lx, d), BF16)],
        compiler_params=_cparams(1, VMEM_BIG),
    )(xr, norm_g, scale, shift, wg)


def _in_bwd(xr, dxo, dus, du5, dypre, dz, wg, norm_g, scale, shift, d_skip, dg_init, tm):
    lx, d = xr.shape
    cb = wg.shape[2]
    pgw = dus[0].shape[1]
    sw = du5.shape[1]
    mix = dz.shape[1]
    ncol = NDEV * cb

    def body(x_ref, dxo_ref, u0, u1, u2, u3, d5_ref, dy_ref, dz_ref, w_ref, g_ref, sc_ref, sh_ref, dk_ref, gi_ref,
             gx_ref, dp_ref, dsc_ref, dsh_ref, dg_ref):
        i = pl.program_id(0)

        @pl.when(i == 0)
        def _():
            dsc_ref[...] = jnp.zeros_like(dsc_ref)
            dsh_ref[...] = jnp.zeros_like(dsh_ref)
            dg_ref[...] = gi_ref[...]

        dp = jnp.concatenate(
            [u0[...], u1[...], u2[...], u3[...], d5_ref[...] + dk_ref[...] * dy_ref[...], dz_ref[...]], axis=1
        ).astype(BF16)
        dp_ref[...] = dp
        dh = _mm_nt(dp[:, 0:cb], w_ref[0])
        for j in range(1, NDEV):
            dh = dh + _mm_nt(dp[:, j * cb : (j + 1) * cb], w_ref[j])
        xv = x_ref[...]
        r = lax.rsqrt(_rowmean(xv * xv) + EPS)
        xh = xv * r
        g = g_ref[...]
        one_sc = 1.0 + sc_ref[...]
        dsh_ref[...] += _colsum(dh)
        dsc_ref[...] += _colsum(dh * (xh * g))
        dg_ref[...] += _colsum(dh * one_sc * xh)
        dxh = dh * one_sc * g
        gx_ref[...] = r * (dxh - xh * _rowmean(dxh * xh)) + dxo_ref[...]

    vec = _whole((1, d))
    return pl.pallas_call(
        body,
        name="in_bwd",
        grid=(lx // tm,),
        in_specs=[_rows(tm, d), _rows(tm, d)] + [_rows(tm, pgw)] * 4 + [_rows(tm, sw), _rows(tm, sw), _rows(tm, mix), _whole(wg.shape), vec, vec, vec, _whole((1, sw)), vec],
        out_specs=[_rows(tm, d), _rows(tm, ncol), _acc((1, d)), _acc((1, d)), _acc((1, d))],
        out_shape=[_sds((lx, d)), _sds((lx, ncol), BF16), _sds((1, d)), _sds((1, d)), _sds((1, d))],
        compiler_params=_cparams(1, VMEM_BIG),
    )(xr, dxo, *dus, du5, dypre, dz, wg, norm_g, scale, shift, d_skip, dg_init)


def _in_bwd_ctx(xc, duc, wg, jsel, norm_g, scale, shift):
    lc, d = xc.shape
    cb = wg.shape[2]

    def body(x_ref, du_ref, w_ref, g_ref, sc_ref, sh_ref, dp_ref, dsc_ref, dsh_ref, dg_ref):
        dp = du_ref[...].astype(BF16)
        dp_ref[...] = dp
        dh = _mm_nt(dp[:, 0:cb], w_ref[jsel[0]])
        for q in range(1, len(jsel)):
            dh = dh + _mm_nt(dp[:, q * cb : (q + 1) * cb], w_ref[jsel[q]])
        xv = x_ref[...]
        xh = xv * lax.rsqrt(_rowmean(xv * xv) + EPS)
        dsh_ref[...] = _colsum(dh)
        dsc_ref[...] = _colsum(dh * (xh * g_ref[...]))
        dg_ref[...] = _colsum(dh * (1.0 + sc_ref[...]) * xh)

    return pl.pallas_call(
        body,
        name="in_bwd_ctx",
        out_shape=[_sds(duc.shape, BF16), _sds((1, d)), _sds((1, d)), _sds((1, d))],
        compiler_params=_cparams(0, VMEM_BIG),
    )(xc, duc, wg, norm_g, scale, shift)


def _in_w_grad(hb, dpb, hcb, dpcb, cb, tm):
    lx, d = hb.shape
    lc = hcb.shape[0]
    nb = lx // tm
    cg = 2 * cb

    def body(h_ref, dp_ref, hc_ref, dpc_ref, o_ref):
        j, i = pl.program_id(0), pl.program_id(1)

        @pl.when(i == 0)
        def _():
            o_ref[...] = jnp.zeros_like(o_ref)

        @pl.when(i < nb)
        def _():
            h = h_ref[...]
            o_ref[0] += _mm_tn(h, dp_ref[:, 0:cb])
            o_ref[1] += _mm_tn(h, dp_ref[:, cb:cg])

        @pl.when(jnp.logical_and(i == nb, j == 1))
        def _():
            h = hc_ref[...]
            o_ref[0] += _mm_tn(h, dpc_ref[:, 0:cb])
            o_ref[1] += _mm_tn(h, dpc_ref[:, cb:cg])

    return pl.pallas_call(
        body,
        name="in_w_grad",
        grid=(NDEV // 2, nb + 1),
        in_specs=[
            pl.BlockSpec((tm, d), lambda j, i: (jnp.minimum(i, nb - 1), 0)),
            pl.BlockSpec((tm, cg), lambda j, i: (jnp.minimum(i, nb - 1), j)),
            pl.BlockSpec((lc, d), lambda j, i: (0, 0)),
            pl.BlockSpec((lc, cg), lambda j, i: (0, 0)),
        ],
        out_specs=pl.BlockSpec((2, d, cb), lambda j, i: (j, 0, 0)),
        out_shape=_sds((NDEV, d, cb)),
        compiler_params=_cparams(2, VMEM_BIG),
    )(hb, dpb, hcb, dpcb)


def _pool_tables(w, rows, rb, pgw, transpose):
    t = np.arange(GRID_W)
    lo, hi = np.clip(t - w // 2, 0, GRID_W), np.clip(t + w - w // 2, 0, GRID_W)
    band = ((t[None, :] >= lo[:, None]) & (t[None, :] < hi[:, None])).astype(np.float32)
    if transpose:
        band = band.T
    kc = np.kron(np.eye(rb, dtype=np.float32), band)
    inv_c = np.tile((1.0 / (hi - lo).astype(np.float32))[:, None], (rb, pgw)).astype(np.float32)
    r = np.arange(rows)
    cnt_r = np.clip(r + w - w // 2, 0, rows) - np.clip(r - w // 2, 0, rows)
    inv_r = (1.0 / cnt_r.astype(np.float32)).astype(np.float32)
    return jnp.asarray(kc, BF16), jnp.asarray(inv_r), jnp.asarray(inv_c)


def _pool_fwd(proj, col0, w, pw_g, name):
    lx = proj.shape[0]
    pgw = pw_g.shape[0]
    rows = lx // GRID_W
    rb = min(4, rows)
    tok = rb * GRID_W
    lo = w // 2
    kc, inv_r, inv_c = _pool_tables(w, rows, rb, pgw, False)

    def body(p_hbm, w_ref, kc_ref, ir_ref, ic_ref, lin_ref, dm_ref, xp, sem):
        xp[pl.ds(0, lo * GRID_W), :] = jnp.zeros((lo * GRID_W, pgw), F32)
        xp[pl.ds((lo + rows) * GRID_W, (w - lo) * GRID_W), :] = jnp.zeros(((w - lo) * GRID_W, pgw), F32)
        cp = pltpu.make_async_copy(p_hbm.at[:, pl.ds(col0, pgw)], xp.at[pl.ds(lo * GRID_W, lx), :], sem)
        cp.start()
        cp.wait()

        def blk(b, carry):
            r0 = b * rb
            parts = []
            for rr in range(rb):
                base = pl.multiple_of((r0 + rr) * GRID_W, GRID_W)
                s = xp[pl.ds(base, GRID_W), :]
                for k in range(1, w):
                    s = s + xp[pl.ds(base + k * GRID_W, GRID_W), :]
                parts.append(s * ir_ref[r0 + rr])
            m = _mm_split(kc_ref[...], jnp.concatenate(parts, axis=0)) * ic_ref[...]
            u = xp[pl.ds(pl.multiple_of((r0 + lo) * GRID_W, GRID_W), tok), :]
            db = (m - u).astype(BF16)
            o0 = pl.multiple_of(r0 * GRID_W, GRID_W)
            dm_ref[pl.ds(o0, tok), :] = db
            lin_ref[pl.ds(o0, tok), :] = _mm(db, w_ref[...])
            return carry

        lax.fori_loop(0, rows // rb, blk, 0)

    return pl.pallas_call(
        body,
        name=name,
        in_specs=[_ANY, pl.BlockSpec(memory_space=pltpu.VMEM), pl.BlockSpec(memory_space=pltpu.VMEM), _SMEM, pl.BlockSpec(memory_space=pltpu.VMEM)],
        out_shape=[_sds((lx, pgw)), _sds((lx, pgw), BF16)],
        scratch_shapes=[pltpu.VMEM(((rows + w) * GRID_W, pgw), F32), pltpu.SemaphoreType.DMA],
        compiler_params=_cparams(0, VMEM_BIG),
    )(proj, pw_g, kc, inv_r, inv_c)


def _pool_bwd(dlin, col0, w, pw_g, dmat_g, name):
    lx = dlin.shape[0]
    pgw = pw_g.shape[0]
    rows = lx // GRID_W
    rb = min(4, rows)
    tok = rb * GRID_W
    front = w - w // 2 - 1
    kct, inv_r, inv_c = _pool_tables(w, rows, rb, pgw, True)

    def body(dl_hbm, w_ref, kc_ref, ir_ref, ic_ref, dm_ref, du_ref, dw_ref, tp, dl, sem):
        if front:
            tp[pl.ds(0, front * GRID_W), :] = jnp.zeros((front * GRID_W, pgw), F32)
        tp[pl.ds((front + rows) * GRID_W, (w - front) * GRID_W), :] = jnp.zeros(((w - front) * GRID_W, pgw), F32)
        cp = pltpu.make_async_copy(dl_hbm.at[:, pl.ds(col0, pgw)], dl, sem)
        cp.start()
        cp.wait()
        dw_ref[...] = jnp.zeros_like(dw_ref)

        def blk(b, carry):
            o0 = pl.multiple_of(b * tok, GRID_W)
            dlb = dl[pl.ds(o0, tok), :].astype(BF16)
            dw_ref[...] += _mm_tn(dm_ref[pl.ds(o0, tok), :], dlb)
            dd = _mm_nt(dlb, w_ref[...])
            du_ref[pl.ds(o0, tok), :] = -dd
            t = _mm_split(kc_ref[...], dd * ic_ref[...])
            for rr in range(rb):
                dst = pl.multiple_of((b * rb + rr + front) * GRID_W, GRID_W)
                tp[pl.ds(dst, GRID_W), :] = t[rr * GRID_W : (rr + 1) * GRID_W] * ir_ref[b * rb + rr]
            return carry

        lax.fori_loop(0, rows // rb, blk, 0)

        def rowl(r, carry):
            base = pl.multiple_of(r * GRID_W, GRID_W)
            s = tp[pl.ds(base, GRID_W), :]
            for k in range(1, w):
                s = s + tp[pl.ds(base + k * GRID_W, GRID_W), :]
            du_ref[pl.ds(base, GRID_W), :] += s
            return carry

        lax.fori_loop(0, rows, rowl, 0)

    vm = pl.BlockSpec(memory_space=pltpu.VMEM)
    return pl.pallas_call(
        body,
        name=name,
        in_specs=[_ANY, vm, vm, _SMEM, vm, vm],
        out_shape=[_sds((lx, pgw)), _sds((pgw, pgw))],
        scratch_shapes=[pltpu.VMEM(((rows + w) * GRID_W, pgw), F32), pltpu.VMEM((lx, pgw), F32), pltpu.SemaphoreType.DMA],
        compiler_params=_cparams(0, VMEM_BIG),
    )(dlin, pw_g, kct, inv_r, inv_c, dmat_g)


def _s5_ops(lam_re, lam_im, log_dt, b_re, b_im, c_re, c_im):
    hp = lax.Precision.HIGHEST
    t = CHUNK_T
    g, n = lam_re.shape[1], lam_re.shape[2]
    h = b_re.shape[3]
    dt = jnp.exp(log_dt)[..., None]
    zr, zi = lam_re * dt, lam_im * dt
    tau = jnp.arange(t + 1, dtype=F32)[:, None, None, None]
    mag = jnp.exp(tau * zr)
    pr, pi = mag * jnp.cos(tau * zi), mag * jnp.sin(tau * zi)
    ar, ai = pr[1], pi[1]
    den = lam_re * lam_re + lam_im * lam_im
    qr = ((ar - 1.0) * lam_re + ai * lam_im) / den
    qi = (ai * lam_re - (ar - 1.0) * lam_im) / den
    bbr = qr[..., None] * b_re - qi[..., None] * b_im
    bbi = qr[..., None] * b_im + qi[..., None] * b_re
    car = c_re[None] * pr[:, :, :, None, :] - c_im[None] * pi[:, :, :, None, :]
    cai = c_re[None] * pi[:, :, :, None, :] + c_im[None] * pr[:, :, :, None, :]
    kern = jnp.einsum("tdgon,dgni->dgtoi", car[:t], bbr, precision=hp) - jnp.einsum("tdgon,dgni->dgtoi", cai[:t], bbi, precision=hp)
    idx = np.arange(t)
    lag = idx[None, :] - idx[:, None]
    ef = (lag[None] == idx[:, None, None]).astype(np.float32)
    eb = ((-lag)[None] == idx[:, None, None]).astype(np.float32)
    tsum = jnp.einsum("zst,gzoi->gsito", ef, kern[0], precision=hp) + jnp.einsum("zst,gzoi->gsito", eb, kern[1], precision=hp)
    tsum = tsum.reshape(g, t * h, t * h)
    pfr, pfi = pr[:t][::-1, 0], pi[:t][::-1, 0]
    pbr, pbi = pr[:t, 1], pi[:t, 1]

    def st(pr_, pi_, d):
        re = pr_[..., None] * bbr[d][None] - pi_[..., None] * bbi[d][None]
        im = pr_[..., None] * bbi[d][None] + pi_[..., None] * bbr[d][None]
        lay = lambda v: jnp.transpose(v, (1, 0, 3, 2)).reshape(g, t * h, n)
        return lay(re), lay(im)

    wfr, wfi = st(pfr, pfi, 0)
    wbr, wbi = st(pbr, pbi, 1)
    wst = jnp.concatenate([wfr, wbr, wfi, wbi], axis=2)
    lay = lambda v: jnp.transpose(v, (1, 3, 0, 2)).reshape(g, n, t * h)
    cfr, cfi = lay(car[1:, 0]), lay(cai[1:, 0])
    cbr, cbi = lay(car[1:][::-1, 1]), lay(cai[1:][::-1, 1])
    ccat = jnp.concatenate([cfr, cbr, -cfi, -cbi], axis=1)
    a16 = jnp.concatenate([pr[t, 0], pr[t, 1], pi[t, 0], pi[t, 1]], axis=1)
    return tsum, wst, ccat, a16


GPL = LANES // SSM_H
CW = CHUNK_T * SSM_H
BW = CHUNK_T * LANES


def _chunk_perm():
    o = np.arange(BW)
    src = ((o % CW) // SSM_H) * LANES + (o // CW) * SSM_H + o % SSM_H
    return jnp.asarray(np.arange(BW)[:, None] == src[None, :], BF16)


def _gspec(gb, a, b):
    return pl.BlockSpec((gb, a, b), lambda i: (i, 0, 0))


def _chunk_rows(ref, n):
    return jnp.concatenate([ref[pl.ds(s, n, stride=CHUNK_T), :] for s in range(CHUNK_T)], axis=1)


def _s5_state(proj, col0, uc, perm, wst, ncc, ncl):
    nlb = uc.shape[1] // LANES
    g, _, ns = wst.shape
    nch = ncc + ncl
    half = ns // 2

    def body(p_ref, c_ref, pm_ref, w_ref, u_ref, sr_ref, si_ref):
        xrows = jnp.concatenate([_chunk_rows(c_ref, ncc), _chunk_rows(p_ref, ncl)], axis=0).astype(BF16)
        u = _mm(xrows, pm_ref[...]).astype(BF16)
        u_ref[0] = u
        for gi in range(GPL):
            s = _mm(u[:, gi * CW : (gi + 1) * CW], w_ref[gi].astype(BF16))
            sr_ref[gi] = s[:, :half]
            si_ref[gi] = s[:, half:]

    return pl.pallas_call(
        body,
        name="s5_state",
        grid=(nlb,),
        in_specs=[
            pl.BlockSpec((ncl * CHUNK_T, LANES), lambda i: (0, col0 + i)),
            pl.BlockSpec((ncc * CHUNK_T, LANES), lambda i: (0, i)),
            _whole(perm.shape),
            _gspec(GPL, CW, ns),
        ],
        out_specs=[pl.BlockSpec((1, nch, BW), lambda i: (i, 0, 0)), _gspec(GPL, nch, half), _gspec(GPL, nch, half)],
        out_shape=[_sds((nlb, nch, BW), BF16), _sds((g, nch, half)), _sds((g, nch, half))],
        compiler_params=_cparams(1, VMEM_BIG),
    )(proj, uc, perm, wst)


def _lane_masks(gb, half):
    lane = lax.broadcasted_iota(jnp.int32, (gb, half), 1)
    return lane < (half // 2)


def _s5_scan_fwd(s_re, s_im, a16, ncc, nch, gb):
    g, ns = a16.shape
    half = ns // 2

    def body(sr_ref, si_ref, a_ref, hr_ref, hi_ref):
        ar, ai = a_ref[:, :half], a_ref[:, half:]
        mf = _lane_masks(gb, half)
        zero = jnp.zeros((gb, half), F32)
        hr_ref[...] = jnp.zeros_like(hr_ref)
        hi_ref[...] = jnp.zeros_like(hi_ref)

        def step(t, c):
            fr, fi, br, bi = c
            rf = pl.ds(t, gb, stride=nch)
            rb = pl.ds(jnp.where(t < ncc, ncc - 1 - t, nch - 1 - (t - ncc)), gb, stride=nch)
            hr_ref[rf, :] += jnp.where(mf, fr, 0.0)
            hi_ref[rf, :] += jnp.where(mf, fi, 0.0)
            hr_ref[rb, :] += jnp.where(mf, 0.0, br)
            hi_ref[rb, :] += jnp.where(mf, 0.0, bi)
            return (
                ar * fr - ai * fi + sr_ref[rf, :],
                ar * fi + ai * fr + si_ref[rf, :],
                ar * br - ai * bi + sr_ref[rb, :],
                ar * bi + ai * br + si_ref[rb, :],
            )

        lax.fori_loop(0, nch, step, (zero, zero, zero, zero))

    blk = pl.BlockSpec((gb * nch, half), lambda i: (i, 0))
    return pl.pallas_call(
        body,
        name="s5_scan_fwd",
        grid=(g // gb,),
        in_specs=[blk, blk, pl.BlockSpec((gb, ns), lambda i: (i, 0))],
        out_specs=[blk, blk],
        out_shape=[_sds((g * nch, half))] * 2,
        compiler_params=_cparams(1, VMEM_BIG),
    )(s_re, s_im, a16)


def _token_rows_store(ref, val, n):
    for s in range(CHUNK_T):
        ref[pl.ds(s, n, stride=CHUNK_T), :] = val[:, s * LANES : (s + 1) * LANES]


def _s5_out(u_all, h_re, h_im, tsum, ccat, perm, ncc, lx):
    nlb, nch, _ = u_all.shape
    g, ns, _ = ccat.shape
    ncl = nch - ncc
    half = ns // 2

    def body(u_ref, hr_ref, hi_ref, t_ref, c_ref, pm_ref, y_ref):
        parts = []
        for gi in range(GPL):
            u = u_ref[0, ncc:, gi * CW : (gi + 1) * CW]
            hs = jnp.concatenate([hr_ref[gi, ncc:, :], hi_ref[gi, ncc:, :]], axis=1).astype(BF16)
            parts.append(_mm(u, t_ref[gi].astype(BF16)) + _mm(hs, c_ref[gi].astype(BF16)))
        y = jnp.concatenate(parts, axis=1)
        hi = y.astype(BF16)
        lo = (y - hi.astype(F32)).astype(BF16)
        _token_rows_store(y_ref, _mm_nt(hi, pm_ref[...]) + _mm_nt(lo, pm_ref[...]), ncl)

    return pl.pallas_call(
        body,
        name="s5_out",
        grid=(nlb,),
        in_specs=[pl.BlockSpec((1, nch, BW), lambda i: (i, 0, 0)), _gspec(GPL, nch, half), _gspec(GPL, nch, half), _gspec(GPL, CW, CW), _gspec(GPL, ns, CW), _whole(perm.shape)],
        out_specs=pl.BlockSpec((lx, LANES), lambda i: (0, i)),
        out_shape=_sds((lx, nlb * LANES)),
        compiler_params=_cparams(1, VMEM_BIG),
    )(u_all, h_re, h_im, tsum, ccat, perm)


def _s5_dstate(dypre, perm, ccat, ncc):
    lx, sw = dypre.shape
    nlb = sw // LANES
    g, ns, _ = ccat.shape
    ncl = lx // CHUNK_T
    nch = ncl + ncc
    half = ns // 2

    def body(dy_ref, pm_ref, c_ref, dyr_ref, dr_ref, di_ref):
        dy = _mm(_chunk_rows(dy_ref, ncl).astype(BF16), pm_ref[...]).astype(BF16)
        dyr_ref[0] = dy
        for gi in range(GPL):
            dh = _mm_nt(dy[:, gi * CW : (gi + 1) * CW], c_ref[gi].astype(BF16))
            dr_ref[gi, :ncc, :] = jnp.zeros((ncc, half), F32)
            di_ref[gi, :ncc, :] = jnp.zeros((ncc, half), F32)
            dr_ref[gi, ncc:, :] = dh[:, :half]
            di_ref[gi, ncc:, :] = dh[:, half:]

    return pl.pallas_call(
        body,
        name="s5_dstate",
        grid=(nlb,),
        in_specs=[pl.BlockSpec((lx, LANES), lambda i: (0, i)), _whole(perm.shape), _gspec(GPL, ns, CW)],
        out_specs=[pl.BlockSpec((1, ncl, BW), lambda i: (i, 0, 0)), _gspec(GPL, nch, half), _gspec(GPL, nch, half)],
        out_shape=[_sds((nlb, ncl, BW), BF16), _sds((g, nch, half)), _sds((g, nch, half))],
        compiler_params=_cparams(1, VMEM_BIG),
    )(dypre, perm, ccat)


def _s5_scan_bwd(dh_re, dh_im, a16, ncc, nch, gb):
    g, ns = a16.shape
    half = ns // 2
    ncl = nch - ncc

    def body(dr_ref, di_ref, a_ref, sr_ref, si_ref):
        ar, ai = a_ref[:, :half], a_ref[:, half:]
        mf = _lane_masks(gb, half)
        zero = jnp.zeros((gb, half), F32)
        sr_ref[...] = jnp.zeros_like(sr_ref)
        si_ref[...] = jnp.zeros_like(si_ref)

        def step(t, c):
            lfr, lfi, lbr, lbi = c
            pf = pl.ds(nch - 1 - t, gb, stride=nch)
            pb = pl.ds(jnp.where(t < ncl, ncc + t, t - ncl), gb, stride=nch)
            sr_ref[pf, :] += jnp.where(mf, lfr, 0.0)
            si_ref[pf, :] += jnp.where(mf, lfi, 0.0)
            sr_ref[pb, :] += jnp.where(mf, 0.0, lbr)
            si_ref[pb, :] += jnp.where(mf, 0.0, lbi)
            return (
                dr_ref[pf, :] + ar * lfr + ai * lfi,
                di_ref[pf, :] + ar * lfi - ai * lfr,
                dr_ref[pb, :] + ar * lbr + ai * lbi,
                di_ref[pb, :] + ar * lbi - ai * lbr,
            )

        lax.fori_loop(0, nch, step, (zero,) * 4)

    blk = pl.BlockSpec((gb * nch, half), lambda i: (i, 0))
    return pl.pallas_call(
        body,
        name="s5_scan_bwd",
        grid=(g // gb,),
        in_specs=[blk, blk, pl.BlockSpec((gb, ns), lambda i: (i, 0))],
        out_specs=[blk, blk],
        out_shape=[_sds((g * nch, half))] * 2,
        compiler_params=_cparams(1, VMEM_BIG),
    )(dh_re, dh_im, a16)


def _s5_du(dy_all, ds_re, ds_im, tsum, wst, perm, ncc, lx, lc):
    nlb, ncl, _ = dy_all.shape
    g, _, ns = wst.shape
    nch = ncc + ncl
    half = ns // 2

    def body(dy_ref, sr_ref, si_ref, t_ref, w_ref, pm_ref, du_ref, dc_ref):
        parts = []
        for gi in range(GPL):
            ds = jnp.concatenate([sr_ref[gi], si_ref[gi]], axis=1).astype(BF16)
            d_all = _mm_nt(ds, w_ref[gi].astype(BF16))
            d_lat = d_all[ncc:] + _mm_nt(dy_ref[0, :, gi * CW : (gi + 1) * CW], t_ref[gi].astype(BF16))
            parts.append(jnp.concatenate([d_all[:ncc], d_lat], axis=0))
        du = _mm_nt(jnp.concatenate(parts, axis=1).astype(BF16), pm_ref[...])
        _token_rows_store(dc_ref, du[:ncc], ncc)
        _token_rows_store(du_ref, du[ncc:], ncl)

    return pl.pallas_call(
        body,
        name="s5_du",
        grid=(nlb,),
        in_specs=[pl.BlockSpec((1, ncl, BW), lambda i: (i, 0, 0)), _gspec(GPL, nch, half), _gspec(GPL, nch, half), _gspec(GPL, CW, CW), _gspec(GPL, CW, ns), _whole(perm.shape)],
        out_specs=[pl.BlockSpec((lx, LANES), lambda i: (0, i)), pl.BlockSpec((lc, LANES), lambda i: (0, i))],
        out_shape=[_sds((lx, nlb * LANES)), _sds((lc, nlb * LANES))],
        compiler_params=_cparams(1, VMEM_BIG),
    )(dy_all, ds_re, ds_im, tsum, wst, perm)


def _s5_op_grads(u_all, dy_all, ds_re, ds_im, h_re, h_im, ncc):
    nlb, nch, _ = u_all.shape
    g, _, half = ds_re.shape
    ns = 2 * half

    def body(u_ref, dy_ref, sr_ref, si_ref, hr_ref, hi_ref, dw_ref, dt_ref, dc_ref, da_ref):
        das = []
        for gi in range(GPL):
            cols = slice(gi * CW, (gi + 1) * CW)
            sr, si, hr, hi = sr_ref[gi], si_ref[gi], hr_ref[gi], hi_ref[gi]
            ds = jnp.concatenate([sr, si], axis=1).astype(BF16)
            dy = dy_ref[0, :, cols]
            dw_ref[gi] = _mm_tn(u_ref[0, :, cols], ds)
            dt_ref[gi] = _mm_tn(u_ref[0, ncc:, cols], dy)
            hs = jnp.concatenate([hr[ncc:], hi[ncc:]], axis=1).astype(BF16)
            dc_ref[gi] = _mm_tn(hs, dy)
            das.append(jnp.concatenate([_colsum(hr * sr + hi * si), _colsum(hr * si - hi * sr)], axis=1))
        da_ref[...] = jnp.concatenate(das, axis=0)

    return pl.pallas_call(
        body,
        name="s5_op_grads",
        grid=(nlb,),
        in_specs=[pl.BlockSpec((1, nch, BW), lambda i: (i, 0, 0)), pl.BlockSpec((1, nch - ncc, BW), lambda i: (i, 0, 0))] + [_gspec(GPL, nch, half)] * 4,
        out_specs=[_gspec(GPL, CW, ns), _gspec(GPL, CW, CW), _gspec(GPL, ns, CW), pl.BlockSpec((GPL, ns), lambda i: (i, 0))],
        out_shape=[_sds((g, CW, ns)), _sds((g, CW, CW)), _sds((g, ns, CW)), _sds((g, ns))],
        compiler_params=_cparams(1, VMEM_BIG),
    )(u_all, dy_all, ds_re, ds_im, h_re, h_im)


def _merge_fwd(y_ssm, proj, lins, x, tgt, glu_wg, out_w, d_skip, glu_b, pscale, gate, fin_g, tm):
    lx, d = x.shape
    sw = y_ssm.shape[1]
    mix = out_w.shape[0]
    pw = mix - sw
    pgw = lins[0].shape[1]
    cb2 = glu_wg.shape[2]
    nh = NDEV // 2

    def body(y_ref, u_ref, z_ref, l0, l1, l2, l3, x_ref, t_ref, gw_ref, ow_ref, dk_ref, gb_ref, ps_ref, gt_ref, fg_ref,
             ypre_ref, y12_ref, br_ref, dxo_ref, loss_ref, dfg_ref, dgt_ref):
        @pl.when(pl.program_id(0) == 0)
        def _():
            loss_ref[...] = jnp.zeros_like(loss_ref)
            dfg_ref[...] = jnp.zeros_like(dfg_ref)
            dgt_ref[...] = jnp.zeros_like(dgt_ref)

        ypre = y_ref[...] + dk_ref[...] * u_ref[...]
        ypre_ref[...] = ypre
        yg = _gelu(ypre).astype(BF16)
        outs = []
        for j in range(nh):
            y1 = _mm(yg, gw_ref[j]) + gb_ref[:, j * cb2 : (j + 1) * cb2]
            y2 = _mm(yg, gw_ref[nh + j]) + gb_ref[:, (nh + j) * cb2 : (nh + j + 1) * cb2]
            y12_ref[:, j * cb2 : (j + 1) * cb2] = y1
            y12_ref[:, (nh + j) * cb2 : (nh + j + 1) * cb2] = y2
            outs.append(y1 * _sigmoid(y2))
        lin = jnp.concatenate([l0[...], l1[...], l2[...], l3[...]], axis=1) * ps_ref[...]
        brb = (jnp.concatenate([lin] + outs, axis=1) * _silu(z_ref[...])).astype(BF16)
        br_ref[...] = brb
        mixv = _mm(brb, ow_ref[...])
        xo = x_ref[...] + gt_ref[...] * mixv
        r2 = lax.rsqrt(_rowmean(xo * xo) + EPS)
        xh = xo * r2
        fg = fg_ref[...]
        e = xh * fg - t_ref[...]
        loss_ref[...] += 0.5 * _colsum(_rowmean(e * e))
        dy = e * (1.0 / d)
        dfg_ref[...] += _colsum(dy * xh)
        gy = dy * fg
        dxo = r2 * (gy - xh * _rowmean(gy * xh))
        dxo_ref[...] = dxo
        dgt_ref[...] += _colsum(dxo * mixv)

    vec = _whole((1, d))
    return pl.pallas_call(
        body,
        name="merge_fwd",
        grid=(lx // tm,),
        in_specs=[_rows(tm, sw), _rows(tm, sw, 1), _rows(tm, mix, 1)] + [_rows(tm, pgw)] * 4 + [_rows(tm, d), _rows(tm, d), _whole(glu_wg.shape), _whole(out_w.shape), _whole((1, sw)), _whole((1, 2 * sw)), _whole((1, pw)), vec, vec],
        out_specs=[_rows(tm, sw), _rows(tm, 2 * sw), _rows(tm, mix), _rows(tm, d), _acc((1, 1)), _acc((1, d)), _acc((1, d))],
        out_shape=[_sds((lx, sw)), _sds((lx, 2 * sw)), _sds((lx, mix), BF16), _sds((lx, d)), _sds((1, 1)), _sds((1, d)), _sds((1, d))],
        compiler_params=_cparams(1, VMEM_BIG),
    )(y_ssm, proj, proj, *lins, x, tgt, glu_wg, out_w, d_skip, glu_b, pscale, gate, fin_g)


def _out_bwd(dxo, gate, branch, out_w, lins, y12, proj, pscale, tm):
    lx, d = dxo.shape
    mix = out_w.shape[0]
    pgw = lins[0].shape[1]
    pw = 4 * pgw
    sw = mix - pw
    nb = lx // tm

    def body(dxo_ref, gt_ref, br_ref, ow_ref, l0, l1, l2, l3, y_ref, z_ref, ps_ref, dz_ref, dl_ref, dg_ref, dps_ref, dow_hbm, acc, sem):
        i = pl.program_id(0)

        @pl.when(i == 0)
        def _():
            acc[...] = jnp.zeros_like(acc)
            dps_ref[...] = jnp.zeros_like(dps_ref)

        dmix = (dxo_ref[...] * gt_ref[...]).astype(BF16)
        dbr = _mm_nt(dmix, ow_ref[...])
        acc[...] += _mm_tn(br_ref[...], dmix)
        z = z_ref[...]
        sg = _sigmoid(z)
        dbp = dbr * (z * sg)
        y1, y2 = y_ref[:, :sw], y_ref[:, sw:]
        s2 = _sigmoid(y2)
        ps = ps_ref[...]
        lin = jnp.concatenate([l0[...], l1[...], l2[...], l3[...]], axis=1)
        bp = jnp.concatenate([lin * ps, y1 * s2], axis=1)
        dz_ref[...] = dbr * bp * (sg * (1.0 + z * (1.0 - sg)))
        dlp = dbp[:, :pw]
        dl_ref[...] = dlp * ps
        dps_ref[...] += _colsum(dlp * lin)
        dss = dbp[:, pw:]
        dg_ref[...] = jnp.concatenate([dss * s2, dss * y1 * s2 * (1.0 - s2)], axis=1).astype(BF16)

        @pl.when(i == nb - 1)
        def _():
            cp = pltpu.make_async_copy(acc, dow_hbm, sem)
            cp.start()
            cp.wait()

    return pl.pallas_call(
        body,
        name="out_bwd",
        grid=(nb,),
        in_specs=[_rows(tm, d), _whole((1, d)), _rows(tm, mix), _whole(out_w.shape)] + [_rows(tm, pgw)] * 4 + [_rows(tm, 2 * sw), _rows(tm, mix, 1), _whole((1, pw))],
        out_specs=[_rows(tm, mix), _rows(tm, pw), _rows(tm, 2 * sw), _acc((1, pw)), _ANY],
        out_shape=[_sds((lx, mix)), _sds((lx, pw)), _sds((lx, 2 * sw), BF16), _sds((1, pw)), _sds((mix, d))],
        scratch_shapes=[pltpu.VMEM((mix, d), F32), pltpu.SemaphoreType.DMA],
        compiler_params=_cparams(1, VMEM_BIG),
    )(dxo, gate, branch, out_w, *lins, y12, proj, pscale)


def _glu_bwd(dg12, ypre, proj, glu_wg, d_skip, tm):
    lx, sw = ypre.shape
    cb2 = glu_wg.shape[2]
    nb = lx // tm

    def body(dg_ref, yp_ref, u_ref, gw_ref, dk_ref, dyp_ref, dgb_ref, dd_ref, dgw_hbm, acc, sem):
        i = pl.program_id(0)

        @pl.when(i == 0)
        def _():
            acc[...] = jnp.zeros_like(acc)
            dgb_ref[...] = jnp.zeros_like(dgb_ref)
            dd_ref[...] = jnp.zeros_like(dd_ref)

        ypre = yp_ref[...]
        ygb = _gelu(ypre).astype(BF16)
        dg = dg_ref[...]
        dyg = jnp.zeros((tm, sw), F32)
        for j in range(NDEV):
            dgj = dg[:, j * cb2 : (j + 1) * cb2]
            dyg = dyg + _mm_nt(dgj, gw_ref[j])
            acc[j] += _mm_tn(ygb, dgj)
        dgb_ref[...] += _colsum(dg.astype(F32))
        dyp = dyg * _gelu_grad(ypre)
        dyp_ref[...] = dyp
        dd_ref[...] += _colsum(dyp * u_ref[...])

        @pl.when(i == nb - 1)
        def _():
            cp = pltpu.make_async_copy(acc, dgw_hbm, sem)
            cp.start()
            cp.wait()

    return pl.pallas_call(
        body,
        name="glu_bwd",
        grid=(nb,),
        in_specs=[_rows(tm, 2 * sw), _rows(tm, sw), _rows(tm, sw, 1), _whole(glu_wg.shape), _whole((1, sw))],
        out_specs=[_rows(tm, sw), _acc((1, 2 * sw)), _acc((1, sw)), _ANY],
        out_shape=[_sds((lx, sw)), _sds((1, 2 * sw)), _sds((1, sw)), _sds(glu_wg.shape)],
        scratch_shapes=[pltpu.VMEM(glu_wg.shape, F32), pltpu.SemaphoreType.DMA],
        compiler_params=_cparams(1, VMEM_BIG),
    )(dg12, ypre, proj, glu_wg, d_skip)


def _pack(parts, total):
    flat = jnp.concatenate([p.reshape(-1).astype(F32) for p in parts])
    return jnp.pad(flat, (0, total - flat.shape[0]))


def _unpack(flat, shapes):
    out, off = [], 0
    for s in shapes:
        n = int(np.prod(s)) if len(s) else 1
        out.append(flat[off : off + n].reshape(s))
        off += n
    return out


def kernel(x, c, ctx, c_ctx, ada_w, ada_b, norm_g, in_w, pool_w, pool_scale, s5_lam_re, s5_lam_im, s5_log_dt, s5_b_re, s5_b_im, s5_c_re, s5_c_im, s5_d, glu_w, glu_b, out_w, final_g, loss_target, m_c_ctx, m_ada_w, m_ada_b, m_norm_g, m_in_w, m_pool_w, m_pool_scale, m_s5_lam_re, m_s5_lam_im, m_s5_log_dt, m_s5_b_re, m_s5_b_im, m_s5_c_re, m_s5_c_im, m_s5_d, m_glu_w, m_glu_b, m_out_w, m_final_g, v_c_ctx, v_ada_w, v_ada_b, v_norm_g, v_in_w, v_pool_w, v_pool_scale, v_s5_lam_re, v_s5_lam_im, v_s5_log_dt, v_s5_b_re, v_s5_b_im, v_s5_c_re, v_s5_c_im, v_s5_d, v_glu_w, v_glu_b, v_out_w, v_final_g):
    xr, tgt, xc = x[0], loss_target[0], ctx[0]
    lx, d = xr.shape
    lc = xc.shape[0]
    mix = out_w.shape[1] * NDEV
    sw = glu_w.shape[1]
    pw = mix - sw
    pgw = pw // len(POOL_WINDOWS)
    ngrp = sw // SSM_H
    cb = in_w.shape[2]
    cb2 = glu_w.shape[2]
    nmod = ada_w.shape[2]
    ncc, ncl = lc // CHUNK_T, lx // CHUNK_T
    nch = ncc + ncl
    tm = min(256, lx)
    assert mix == d and pw == sw and 2 * cb == pw and NDEV * cb2 == 2 * sw and lx % GRID_W == 0
    mx, my, mc = _my_pos()
    me = 4 * mx + 2 * my + mc

    (c_all,) = _all_gather([c], "gather_c")
    call = jnp.concatenate([c_all.reshape(NDEV, d), c_ctx.reshape(1, d), jnp.zeros((NDEV - 1, d), F32)], axis=0)
    ada_w_l = ada_w[0]
    ada_b_l = lax.dynamic_slice_in_dim(ada_b, me * nmod, nmod, axis=1)
    m_loc = _mod_fwd(call, ada_w_l, ada_b_l)
    (m_all,) = _all_gather([m_loc], "gather_mod")
    mod = lax.dynamic_index_in_dim(m_all, me, axis=1, keepdims=False).reshape(1, NDEV * nmod)
    mod_c = m_all[:, NDEV, :].reshape(1, NDEV * nmod)
    shift, scale, gate = mod[:, :d], mod[:, d : 2 * d], mod[:, 2 * d :]
    shift_c, scale_c = mod_c[:, :d], mod_c[:, d : 2 * d]

    in_wg, glu_wg, out_wg, pool_wg = _all_gather(
        [in_w[0].astype(BF16), glu_w[0].astype(BF16), out_w[0].astype(BF16), pool_w[0].reshape(-1, pgw).astype(BF16)], "gather_weights"
    )
    out_wf = out_wg.reshape(mix, d)
    pool_wf = pool_wg.reshape(NDEV, len(POOL_WINDOWS), pgw // NDEV, pgw).transpose(1, 0, 2, 3).reshape(len(POOL_WINDOWS), pgw, pgw)

    ssm_blocks = (2, 3)
    proj, hb = _in_proj(xr, norm_g, scale, shift, in_wg, tuple(range(NDEV)), tm, "in_proj")
    uc, hcb = _in_proj(xc, norm_g, scale_c, shift_c, in_wg, ssm_blocks, min(tm, lc), "in_proj_ctx")
    lins, dmats = [], []
    for gi, w in enumerate(POOL_WINDOWS):
        lin_g, dmat_g = _pool_fwd(proj, gi * pgw, w, pool_wf[gi], f"pool_fwd{gi}")
        lins.append(lin_g)
        dmats.append(dmat_g)

    s5_params = tuple(p[0] for p in (s5_lam_re, s5_lam_im, s5_log_dt, s5_b_re, s5_b_im, s5_c_re, s5_c_im))
    (tsum, wst, ccat, a16), ops_vjp = jax.vjp(_s5_ops, *s5_params)
    gb_scan = min(16, ngrp)
    half = wst.shape[2] // 2
    perm = _chunk_perm()
    flat = lambda a: a.reshape(ngrp * nch, half)
    grp = lambda a: a.reshape(ngrp, nch, half)
    u_all, s_re, s_im = _s5_state(proj, pw // LANES, uc, perm, wst, ncc, ncl)
    h_re, h_im = _s5_scan_fwd(flat(s_re), flat(s_im), a16, ncc, nch, gb_scan)
    y_ssm = _s5_out(u_all, grp(h_re), grp(h_im), tsum, ccat, perm, ncc, lx)

    ypre, y12, branch, dxo, loss_l, dfg, dgate = _merge_fwd(
        y_ssm, proj, lins, xr, tgt, glu_wg, out_wf, s5_d, glu_b, pool_scale, gate, final_g.reshape(1, d), tm
    )

    dz, dlin, dg12, dps, d_out_w = _out_bwd(dxo, gate, branch, out_wf, lins, y12, proj, pool_scale, tm)
    dypre, dglu_b, dd_skip, d_glu_w = _glu_bwd(dg12, ypre, proj, glu_wg, s5_d, tm)

    dy_all, dh_re, dh_im = _s5_dstate(dypre, perm, ccat, ncc)
    ds_re, ds_im = _s5_scan_bwd(flat(dh_re), flat(dh_im), a16, ncc, nch, gb_scan)
    du5, duc = _s5_du(dy_all, grp(ds_re), grp(ds_im), tsum, wst, perm, ncc, lx, lc)
    dwst, dtsum, dccat, da16 = _s5_op_grads(u_all, dy_all, grp(ds_re), grp(ds_im), grp(h_re), grp(h_im), ncc)
    ds5 = ops_vjp((dtsum, dwst, dccat, da16))

    dus, dpool_w = [], []
    for gi, w in enumerate(POOL_WINDOWS):
        du_g, dw_g = _pool_bwd(dlin, gi * pgw, w, pool_wf[gi], dmats[gi], f"pool_bwd{gi}")
        dus.append(du_g)
        dpool_w.append(dw_g)

    dpcb, dsc_c, dsh_c, dg_c = _in_bwd_ctx(xc, duc, in_wg, ssm_blocks, norm_g, scale_c, shift_c)
    grad_x, dpb, dsc, dsh, dnorm_g = _in_bwd(xr, dxo, dus, du5, dypre, dz, in_wg, norm_g, scale, shift, s5_d, dg_c, tm)
    d_in_w = _in_w_grad(hb, dpb, hcb, dpcb, cb, tm)

    dmod = jnp.concatenate(
        [jnp.concatenate([dsh, dsc, dgate], axis=1), jnp.concatenate([dsh_c, dsc_c, jnp.zeros((1, d), F32)], axis=1)], axis=0
    )
    (dmod_all,) = _all_gather([dmod], "gather_dmod")
    dmod_l = lax.dynamic_slice_in_dim(dmod_all, me * nmod, nmod, axis=2)
    g_ada_w, g_ada_b_l, cctx_part = _mod_bwd(call, dmod_l[:, 0, :], dmod_l[:, 1, :], ada_w_l)

    npw = len(POOL_WINDOWS)
    dpool_s = jnp.stack(dpool_w).reshape(npw, NDEV, pgw // NDEV, pgw).transpose(1, 0, 2, 3).reshape(NDEV, npw * (pgw // NDEV), pgw)
    land_in, land_glu, land_out, land_pool = _reduce_scatter(
        [d_in_w, d_glu_w, d_out_w.reshape(NDEV, mix // NDEV, d), dpool_s], [BF16] * 4, "grads"
    )

    small_names = ["norm_g", "pool_scale", "s5_lam_re", "s5_lam_im", "s5_log_dt", "s5_b_re", "s5_b_im", "s5_c_re", "s5_c_im", "s5_d", "glu_b", "final_g", "c_ctx"]
    small_w = dict(norm_g=norm_g, pool_scale=pool_scale, s5_lam_re=s5_lam_re, s5_lam_im=s5_lam_im, s5_log_dt=s5_log_dt, s5_b_re=s5_b_re, s5_b_im=s5_b_im, s5_c_re=s5_c_re, s5_c_im=s5_c_im, s5_d=s5_d, glu_b=glu_b, final_g=final_g, c_ctx=c_ctx)
    small_m = dict(norm_g=m_norm_g, pool_scale=m_pool_scale, s5_lam_re=m_s5_lam_re, s5_lam_im=m_s5_lam_im, s5_log_dt=m_s5_log_dt, s5_b_re=m_s5_b_re, s5_b_im=m_s5_b_im, s5_c_re=m_s5_c_re, s5_c_im=m_s5_c_im, s5_d=m_s5_d, glu_b=m_glu_b, final_g=m_final_g, c_ctx=m_c_ctx)
    small_v = dict(norm_g=v_norm_g, pool_scale=v_pool_scale, s5_lam_re=v_s5_lam_re, s5_lam_im=v_s5_lam_im, s5_log_dt=v_s5_log_dt, s5_b_re=v_s5_b_re, s5_b_im=v_s5_b_im, s5_c_re=v_s5_c_re, s5_c_im=v_s5_c_im, s5_d=v_s5_d, glu_b=v_glu_b, final_g=v_final_g, c_ctx=v_c_ctx)
    small_g = dict(norm_g=dnorm_g, pool_scale=dps, s5_lam_re=ds5[0], s5_lam_im=ds5[1], s5_log_dt=ds5[2], s5_b_re=ds5[3], s5_b_im=ds5[4], s5_c_re=ds5[5], s5_c_im=ds5[6], s5_d=dd_skip, glu_b=dglu_b, final_g=dfg, c_ctx=cctx_part)
    shapes = [small_w[k].shape for k in small_names]
    nsmall = sum(int(np.prod(s)) for s in shapes) + 1
    unit = NDEV * 8 * LANES
    tot = -(-nsmall // unit) * unit
    rows = tot // LANES
    gpack = _pack([small_g[k] for k in small_names] + [loss_l], tot).reshape(NDEV, rows // NDEV, LANES)
    (land_small,) = _reduce_scatter([gpack], [F32], "small")
    gsum = _sum_slots(land_small, "sum_small")
    (gall,) = _all_gather([gsum], "gather_small")
    (g_ada_b_all,) = _all_gather([g_ada_b_l], "gather_ada_b")
    gflat = gall.reshape(tot)
    loss = gflat[nsmall - 1]

    nab = NDEV * nmod
    tot2 = tot + -(-nab // (8 * LANES)) * (8 * LANES)
    g2 = jnp.concatenate([gflat, jnp.pad(g_ada_b_all.reshape(nab), (0, tot2 - tot - nab))]).reshape(tot2 // LANES, LANES)

    def pack2(dct, ab):
        return jnp.concatenate(
            [_pack([dct[k] for k in small_names] + [jnp.zeros((1,), F32)], tot), jnp.pad(ab.reshape(nab), (0, tot2 - tot - nab))]
        ).reshape(tot2 // LANES, LANES)

    _, d2, m2, v2 = _adam(g2, pack2(small_w, ada_b), pack2(small_m, m_ada_b), pack2(small_v, v_ada_b), "adam_small")

    def unpack2(flat2):
        flat2 = flat2.reshape(tot2)
        vals = dict(zip(small_names, _unpack(flat2, shapes)))
        vals["ada_b"] = flat2[tot : tot + nab].reshape(ada_b.shape)
        return vals

    res = {k: unpack2(a) for k, a in (("g", g2), ("d", d2), ("m", m2), ("v", v2))}

    def shard(name, g, land, w, m, v):
        shp = w.shape
        w2, m2_, v2_ = (a.reshape(-1, shp[-1]) for a in (w, m, v))
        out = _adam(g, w2, m2_, v2_, "adam_" + name, land=land)
        for k, a in zip(("g", "d", "m", "v"), out):
            res[k][name] = a.reshape(shp)

    shard("ada_w", g_ada_w, None, ada_w, m_ada_w, v_ada_w)
    shard("in_w", None, land_in, in_w, m_in_w, v_in_w)
    shard("pool_w", None, land_pool, pool_w, m_pool_w, v_pool_w)
    shard("glu_w", None, land_glu, glu_w, m_glu_w, v_glu_w)
    shard("out_w", None, land_out, out_w, m_out_w, v_out_w)

    names = ["c_ctx", "ada_w", "ada_b", "norm_g", "in_w", "pool_w", "pool_scale", "s5_lam_re", "s5_lam_im", "s5_log_dt", "s5_b_re", "s5_b_im", "s5_c_re", "s5_c_im", "s5_d", "glu_w", "glu_b", "out_w", "final_g"]
    return (loss, grad_x[None], *[res["g"][n] for n in names], *[res["d"][n] for n in names], *[res["m"][n] for n in names], *[res["v"][n] for n in names])
```

```python
import numpy as np

import jax
import jax.numpy as jnp
from jax import lax
from jax.experimental import pallas as pl
from jax.experimental.pallas import tpu as pltpu

F32 = jnp.float32
BF16 = jnp.bfloat16
NDEV = 8
EPS = 1e-6
GRID_W = 64
POOL_WINDOWS = (2, 4, 8, 16)
SSM_H = 16
CHUNK_T = 16
LANES = 128
ADAM_LR, ADAM_B1, ADAM_B2, ADAM_EPS, ADAM_WD, ADAM_STEP = 0.001, 0.9, 0.999, 1e-08, 0.01, 10
VMEM_BIG = 56 << 20
MESH_ID = pl.DeviceIdType.MESH

_HBM = pl.BlockSpec(memory_space=pltpu.HBM)
_ANY = pl.BlockSpec(memory_space=pl.ANY)
_SMEM = pl.BlockSpec(memory_space=pltpu.SMEM)


def _sds(shape, dtype=F32):
    return jax.ShapeDtypeStruct(tuple(shape), dtype)


def _cparams(ngrid=0, vmem=None):
    return pltpu.CompilerParams(
        dimension_semantics=("arbitrary",) * ngrid if ngrid else None, vmem_limit_bytes=vmem
    )


def _rows(tm, c, col=0):
    return pl.BlockSpec((tm, c), lambda i: (i, col))


def _whole(shape):
    nd = len(shape)
    return pl.BlockSpec(tuple(shape), lambda *_: (0,) * nd, pipeline_mode=pl.Buffered(1))


def _acc(shape):
    nd = len(shape)
    return pl.BlockSpec(tuple(shape), lambda *_: (0,) * nd)


def _mm(a, b):
    return jnp.dot(a, b, preferred_element_type=F32)


def _mm_nt(a, b):
    return lax.dot_general(a, b, (((1,), (1,)), ((), ())), preferred_element_type=F32)


def _mm_tn(a, b):
    return lax.dot_general(a, b, (((0,), (0,)), ((), ())), preferred_element_type=F32)


def _mm_split(k01, s):
    hi = s.astype(BF16)
    lo = (s - hi.astype(F32)).astype(BF16)
    return _mm(k01, hi) + _mm(k01, lo)


def _sigmoid(v):
    return 0.5 * (jnp.tanh(0.5 * v) + 1.0)


def _silu(v):
    return v * _sigmoid(v)


_GELU_K = 0.7978845608028654
_GELU_C = 0.044715


def _gelu(v):
    return 0.5 * v * (1.0 + jnp.tanh(_GELU_K * (v + _GELU_C * v * v * v)))


def _gelu_grad(v):
    th = jnp.tanh(_GELU_K * (v + _GELU_C * v * v * v))
    return 0.5 * (1.0 + th) + 0.5 * v * (1.0 - th * th) * (_GELU_K * (1.0 + 3.0 * _GELU_C * v * v))


def _colsum(v):
    return jnp.sum(v, axis=0, keepdims=True)


def _rowmean(v):
    return jnp.mean(v, axis=-1, keepdims=True)


NCHIP = NDEV // 2


def _my_pos():
    return lax.axis_index("x"), lax.axis_index("y"), lax.axis_index("c")


def _other_chips():
    x, y, _ = _my_pos()
    return [(1 - x, y), (x, 1 - y), (1 - x, 1 - y)]


def _remote(src, dst, send, recv, to):
    return pltpu.make_async_remote_copy(src, dst, send, recv, device_id=to, device_id_type=MESH_ID)


def _all_gather(arrs, name):
    n = len(arrs)

    def body(*refs):
        ins, outs = refs[:n], refs[n : 2 * n]
        send, recv, loc = refs[2 * n :]
        x, y, c = _my_pos()
        me, sib = (x, y, c), (x, y, 1 - c)
        chips = _other_chips()

        def slot(a, p):
            return outs[a].at[4 * p[0] + 2 * p[1] + p[2]]

        def copy(a, k, block, to, own=False):
            return _remote(ins[a] if own else slot(a, block), slot(a, block), send.at[a, k], recv.at[a, k], to)

        local = [pltpu.make_async_copy(ins[a], slot(a, me), loc.at[a]) for a in range(n)]
        for cp in local:
            cp.start()
        sent = []
        for a in range(n):
            sent.append(copy(a, 0, me, sib, own=True))
            sent += [copy(a, 1 + j, me, (*chip, c), own=True) for j, chip in enumerate(chips)]
        for cp in sent:
            cp.start()
        for j, chip in enumerate(chips):
            for a in range(n):
                copy(a, 1 + j, (*chip, c), me).wait_recv()
                fwd = copy(a, 4 + j, (*chip, c), sib)
                fwd.start()
                sent.append(fwd)
        for a in range(n):
            copy(a, 0, sib, me).wait_recv()
            for j, chip in enumerate(chips):
                copy(a, 4 + j, (*chip, 1 - c), me).wait_recv()
        for cp in sent:
            cp.wait_send()
        for cp in local:
            cp.wait()

    return pl.pallas_call(
        body,
        name=name,
        out_shape=[_sds((NDEV,) + a.shape, a.dtype) for a in arrs],
        in_specs=[_HBM] * n,
        out_specs=[_HBM] * n,
        scratch_shapes=[pltpu.SemaphoreType.DMA((n, NDEV - 1)), pltpu.SemaphoreType.DMA((n, NDEV - 1)), pltpu.SemaphoreType.DMA((n,))],
    )(*arrs)


def _sibling_swap(arrs, name):
    n = len(arrs)

    def body(*refs):
        ins, outs = refs[:n], refs[n : 2 * n]
        send, recv = refs[2 * n :]
        x, y, c = _my_pos()
        cps = [_remote(ins[a].at[:, 1 - c], outs[a], send.at[a], recv.at[a], (x, y, 1 - c)) for a in range(n)]
        for cp in cps:
            cp.start()
        for cp in cps:
            cp.wait()

    return pl.pallas_call(
        body,
        name=name,
        out_shape=[_sds((a.shape[0],) + a.shape[2:], a.dtype) for a in arrs],
        in_specs=[_HBM] * n,
        out_specs=[_HBM] * n,
        scratch_shapes=[pltpu.SemaphoreType.DMA((n,)), pltpu.SemaphoreType.DMA((n,))],
    )(*arrs)


def _pair_sum(arr, got, wire, name):
    _, _, r, c = arr.shape
    tr = _row_tile(r, c, budget=1 << 20)

    def body(a_ref, g_ref, o_ref):
        o_ref[0] = (a_ref[0, lax.axis_index("c")] + g_ref[0]).astype(wire)

    return pl.pallas_call(
        body,
        name=name,
        grid=(NCHIP, r // tr),
        in_specs=[pl.BlockSpec((1, 2, tr, c), lambda q, i: (q, 0, i, 0)), pl.BlockSpec((1, tr, c), lambda q, i: (q, i, 0))],
        out_specs=pl.BlockSpec((1, tr, c), lambda q, i: (q, i, 0)),
        out_shape=_sds((NCHIP, r, c), wire),
        compiler_params=_cparams(2),
    )(arr, got)


def _chip_exchange(arrs, name):
    n = len(arrs)

    def body(*refs):
        ins, outs = refs[:n], refs[n : 2 * n]
        send, recv, loc = refs[2 * n :]
        x, y, c = _my_pos()
        mine = 2 * x + y
        chips = _other_chips()
        local = [pltpu.make_async_copy(ins[a].at[mine], outs[a].at[mine], loc.at[a]) for a in range(n)]
        for cp in local:
            cp.start()
        sent = []
        for a in range(n):
            for j, (px, py) in enumerate(chips):
                cp = _remote(ins[a].at[2 * px + py], outs[a].at[mine], send.at[a, j], recv.at[a, j], (px, py, c))
                cp.start()
                sent.append(cp)
        for a in range(n):
            for j, (px, py) in enumerate(chips):
                _remote(ins[a].at[mine], outs[a].at[2 * px + py], send.at[a, j], recv.at[a, j], (px, py, c)).wait_recv()
        for cp in sent:
            cp.wait_send()
        for cp in local:
            cp.wait()

    return pl.pallas_call(
        body,
        name=name,
        out_shape=[_sds(a.shape, a.dtype) for a in arrs],
        in_specs=[_HBM] * n,
        out_specs=[_HBM] * n,
        scratch_shapes=[pltpu.SemaphoreType.DMA((n, NCHIP - 1)), pltpu.SemaphoreType.DMA((n, NCHIP - 1)), pltpu.SemaphoreType.DMA((n,))],
    )(*arrs)


def _reduce_scatter(arrs, wires, tag):
    four = [a.reshape((NCHIP, 2) + a.shape[1:]) for a in arrs]
    got = _sibling_swap(four, "swap_" + tag)
    part = [_pair_sum(a, g, w, f"pair_sum_{tag}{i}") for i, (a, g, w) in enumerate(zip(four, got, wires))]
    return _chip_exchange(part, "exchange_" + tag)


def _row_tile(r, c, budget=1 << 20):
    best = r
    for t in range(8, r, 8):
        if r % t == 0 and t * c * 4 <= budget:
            best = t
    if r * c * 4 <= budget:
        best = r
    return best


def _sum_chips(land):
    tot = land[0].astype(F32)
    for q in range(1, NCHIP):
        tot = tot + land[q].astype(F32)
    return tot


def _adam_math(w, g, m, v):
    m2 = ADAM_B1 * m + (1.0 - ADAM_B1) * g
    v2 = ADAM_B2 * v + (1.0 - ADAM_B2) * (g * g)
    mh = m2 / (1.0 - ADAM_B1**ADAM_STEP)
    vh = v2 / (1.0 - ADAM_B2**ADAM_STEP)
    delta = -ADAM_LR * (mh / (jnp.sqrt(vh) + ADAM_EPS) + ADAM_WD * w)
    return delta, m2, v2


def _adam(g, w, m, v, name, land=None):
    r, c = w.shape
    tr = _row_tile(r, c, budget=1 << 20)

    def body(*refs):
        if land is not None:
            l_ref, w_ref, m_ref, v_ref, g_ref, d_ref, m2_ref, v2_ref = refs
            gv = _sum_chips(l_ref)
            g_ref[...] = gv
        else:
            g_in, w_ref, m_ref, v_ref, d_ref, m2_ref, v2_ref = refs
            gv = g_in[...]
        d, m2, v2 = _adam_math(w_ref[...], gv, m_ref[...], v_ref[...])
        d_ref[...] = d
        m2_ref[...] = m2
        v2_ref[...] = v2

    blk = _rows(tr, c)
    if land is not None:
        in_specs = [pl.BlockSpec((NCHIP, tr, c), lambda i: (0, i, 0)), blk, blk, blk]
        out = pl.pallas_call(
            body, name=name, grid=(r // tr,), in_specs=in_specs, out_specs=[blk] * 4, out_shape=[_sds((r, c))] * 4, compiler_params=_cparams(1)
        )(land, w, m, v)
        return out
    out = pl.pallas_call(
        body, name=name, grid=(r // tr,), in_specs=[blk] * 4, out_specs=[blk] * 3, out_shape=[_sds((r, c))] * 3, compiler_params=_cparams(1)
    )(g, w, m, v)
    return (g,) + tuple(out)


def _sum_slots(land, name):
    _, r, c = land.shape
    tr = _row_tile(r, c, budget=1 << 20)

    def body(l_ref, o_ref):
        o_ref[...] = _sum_chips(l_ref)

    return pl.pallas_call(
        body,
        name=name,
        grid=(r // tr,),
        in_specs=[pl.BlockSpec((NCHIP, tr, c), lambda i: (0, i, 0))],
        out_specs=_rows(tr, c),
        out_shape=_sds((r, c)),
        compiler_params=_cparams(1),
    )(land)


def _mod_fwd(call, ada_w, ada_b_loc):
    def body(c_ref, w_ref, b_ref, o_ref):
        s = _silu(c_ref[...]).astype(BF16)
        o_ref[...] = _mm(s, w_ref[...].astype(BF16)) + b_ref[...]

    return pl.pallas_call(
        body, name="mod_fwd", out_shape=_sds((call.shape[0], ada_w.shape[1])), compiler_params=_cparams(0, 32 << 20)
    )(call, ada_w, ada_b_loc)


def _mod_bwd(call, dm_loc, dmc_loc, ada_w):
    d, n = ada_w.shape
    pad = call.shape[0] - NDEV - 1

    def body(c_ref, dm_ref, dmc_ref, w_ref, gw_ref, gb_ref, cp_ref):
        cv = c_ref[...]
        sg = _sigmoid(cv)
        dmc = _colsum(dmc_ref[...])
        dm = dm_ref[...]
        rows = jnp.concatenate([dm, dmc, jnp.zeros((pad, n), F32)], axis=0)
        gw_ref[...] = _mm_tn((cv * sg).astype(BF16), rows.astype(BF16))
        gb_ref[...] = _colsum(dm) + dmc
        back = _mm_nt(rows[NDEV:].astype(BF16), w_ref[...].astype(BF16))
        c8, s8 = cv[NDEV : NDEV + 1], sg[NDEV : NDEV + 1]
        cp_ref[...] = back[0:1] * (s8 * (1.0 + c8 * (1.0 - s8)))

    return pl.pallas_call(
        body, name="mod_bwd", out_shape=[_sds((d, n)), _sds((1, n)), _sds((1, d))], compiler_params=_cparams(0, 40 << 20)
    )(call, dm_loc, dmc_loc, ada_w)


def _in_proj(xr, norm_g, scale, shift, wg, jsel, tm, name):
    lx, d = xr.shape
    cb = wg.shape[2]

    def body(x_ref, g_ref, sc_ref, sh_ref, w_ref, p_ref, h_ref):
        xv = x_ref[...]
        r = lax.rsqrt(_rowmean(xv * xv) + EPS)
        hb = ((xv * r) * g_ref[...] * (1.0 + sc_ref[...]) + sh_ref[...]).astype(BF16)
        h_ref[...] = hb
        for q, j in enumerate(jsel):
            p_ref[:, q * cb : (q + 1) * cb] = _mm(hb, w_ref[j])

    vec = _whole((1, d))
    return pl.pallas_call(
        body,
        name=name,
        grid=(lx // tm,),
        in_specs=[_rows(tm, d), vec, vec, vec, _whole(wg.shape)],
        out_specs=[_rows(tm, len(jsel) * cb), _rows(tm, d)],
        out_shape=[_sds((lx, len(jsel) * cb)), _sds((lx, d), BF16)],
        compiler_params=_cparams(1, VMEM_BIG),
    )(xr, norm_g, scale, shift, wg)


def _in_bwd(xr, dxo, dus, du5, dypre, dz, wg, norm_g, scale, shift, d_skip, dg_init, tm):
    lx, d = xr.shape
    cb = wg.shape[2]
    pgw = dus[0].shape[1]
    sw = du5.shape[1]
    mix = dz.shape[1]
    ncol = NDEV * cb

    def body(x_ref, dxo_ref, u0, u1, u2, u3, d5_ref, dy_ref, dz_ref, w_ref, g_ref, sc_ref, sh_ref, dk_ref, gi_ref,
             gx_ref, dp_ref, dsc_ref, dsh_ref, dg_ref):
        i = pl.program_id(0)

        @pl.when(i == 0)
        def _():
            dsc_ref[...] = jnp.zeros_like(dsc_ref)
            dsh_ref[...] = jnp.zeros_like(dsh_ref)
            dg_ref[...] = gi_ref[...]

        dp = jnp.concatenate(
            [u0[...], u1[...], u2[...], u3[...], d5_ref[...] + dk_ref[...] * dy_ref[...], dz_ref[...]], axis=1
        ).astype(BF16)
        dp_ref[...] = dp
        dh = _mm_nt(dp[:, 0:cb], w_ref[0])
        for j in range(1, NDEV):
            dh = dh + _mm_nt(dp[:, j * cb : (j + 1) * cb], w_ref[j])
        xv = x_ref[...]
        r = lax.rsqrt(_rowmean(xv * xv) + EPS)
        xh = xv * r
        g = g_ref[...]
        one_sc = 1.0 + sc_ref[...]
        dsh_ref[...] += _colsum(dh)
        dsc_ref[...] += _colsum(dh * (xh * g))
        dg_ref[...] += _colsum(dh * one_sc * xh)
        dxh = dh * one_sc * g
        gx_ref[...] = r * (dxh - xh * _rowmean(dxh * xh)) + dxo_ref[...]

    vec = _whole((1, d))
    return pl.pallas_call(
        body,
        name="in_bwd",
        grid=(lx // tm,),
        in_specs=[_rows(tm, d), _rows(tm, d)] + [_rows(tm, pgw)] * 4 + [_rows(tm, sw), _rows(tm, sw), _rows(tm, mix), _whole(wg.shape), vec, vec, vec, _whole((1, sw)), vec],
        out_specs=[_rows(tm, d), _rows(tm, ncol), _acc((1, d)), _acc((1, d)), _acc((1, d))],
        out_shape=[_sds((lx, d)), _sds((lx, ncol), BF16), _sds((1, d)), _sds((1, d)), _sds((1, d))],
        compiler_params=_cparams(1, VMEM_BIG),
    )(xr, dxo, *dus, du5, dypre, dz, wg, norm_g, scale, shift, d_skip, dg_init)


def _in_bwd_ctx(xc, duc, wg, jsel, norm_g, scale, shift):
    lc, d = xc.shape
    cb = wg.shape[2]

    def body(x_ref, du_ref, w_ref, g_ref, sc_ref, sh_ref, dp_ref, dsc_ref, dsh_ref, dg_ref):
        dp = du_ref[...].astype(BF16)
        dp_ref[...] = dp
        dh = _mm_nt(dp[:, 0:cb], w_ref[jsel[0]])
        for q in range(1, len(jsel)):
            dh = dh + _mm_nt(dp[:, q * cb : (q + 1) * cb], w_ref[jsel[q]])
        xv = x_ref[...]
        xh = xv * lax.rsqrt(_rowmean(xv * xv) + EPS)
        dsh_ref[...] = _colsum(dh)
        dsc_ref[...] = _colsum(dh * (xh * g_ref[...]))
        dg_ref[...] = _colsum(dh * (1.0 + sc_ref[...]) * xh)

    return pl.pallas_call(
        body,
        name="in_bwd_ctx",
        out_shape=[_sds(duc.shape, BF16), _sds((1, d)), _sds((1, d)), _sds((1, d))],
        compiler_params=_cparams(0, VMEM_BIG),
    )(xc, duc, wg, norm_g, scale, shift)


def _in_w_grad(hb, dpb, hcb, dpcb, cb, tm):
    lx, d = hb.shape
    lc = hcb.shape[0]
    nb = lx // tm
    cg = 2 * cb

    def body(h_ref, dp_ref, hc_ref, dpc_ref, o_ref):
        j, i = pl.program_id(0), pl.program_id(1)

        @pl.when(i == 0)
        def _():
            o_ref[...] = jnp.zeros_like(o_ref)

        @pl.when(i < nb)
        def _():
            h = h_ref[...]
            o_ref[0] += _mm_tn(h, dp_ref[:, 0:cb])
            o_ref[1] += _mm_tn(h, dp_ref[:, cb:cg])

        @pl.when(jnp.logical_and(i == nb, j == 1))
        def _():
            h = hc_ref[...]
            o_ref[0] += _mm_tn(h, dpc_ref[:, 0:cb])
            o_ref[1] += _mm_tn(h, dpc_ref[:, cb:cg])

    return pl.pallas_call(
        body,
        name="in_w_grad",
        grid=(NDEV // 2, nb + 1),
        in_specs=[
            pl.BlockSpec((tm, d), lambda j, i: (jnp.minimum(i, nb - 1), 0)),
            pl.BlockSpec((tm, cg), lambda j, i: (jnp.minimum(i, nb - 1), j)),
            pl.BlockSpec((lc, d), lambda j, i: (0, 0)),
            pl.BlockSpec((lc, cg), lambda j, i: (0, 0)),
        ],
        out_specs=pl.BlockSpec((2, d, cb), lambda j, i: (j, 0, 0)),
        out_shape=_sds((NDEV, d, cb)),
        compiler_params=_cparams(2, VMEM_BIG),
    )(hb, dpb, hcb, dpcb)


def _pool_tables(w, rows, rb, pgw, transpose):
    t = np.arange(GRID_W)
    lo, hi = np.clip(t - w // 2, 0, GRID_W), np.clip(t + w - w // 2, 0, GRID_W)
    band = ((t[None, :] >= lo[:, None]) & (t[None, :] < hi[:, None])).astype(np.float32)
    if transpose:
        band = band.T
    kc = np.kron(np.eye(rb, dtype=np.float32), band)
    inv_c = np.tile((1.0 / (hi - lo).astype(np.float32))[:, None], (rb, pgw)).astype(np.float32)
    r = np.arange(rows)
    cnt_r = np.clip(r + w - w // 2, 0, rows) - np.clip(r - w // 2, 0, rows)
    inv_r = (1.0 / cnt_r.astype(np.float32)).astype(np.float32)
    return jnp.asarray(kc, BF16), jnp.asarray(inv_r), jnp.asarray(inv_c)


def _pool_fwd(proj, col0, w, pw_g, name):
    lx = proj.shape[0]
    pgw = pw_g.shape[0]
    rows = lx // GRID_W
    rb = min(4, rows)
    tok = rb * GRID_W
    lo = w // 2
    kc, inv_r, inv_c = _pool_tables(w, rows, rb, pgw, False)

    def body(p_hbm, w_ref, kc_ref, ir_ref, ic_ref, lin_ref, dm_ref, xp, sem):
        xp[pl.ds(0, lo * GRID_W), :] = jnp.zeros((lo * GRID_W, pgw), F32)
        xp[pl.ds((lo + rows) * GRID_W, (w - lo) * GRID_W), :] = jnp.zeros(((w - lo) * GRID_W, pgw), F32)
        cp = pltpu.make_async_copy(p_hbm.at[:, pl.ds(col0, pgw)], xp.at[pl.ds(lo * GRID_W, lx), :], sem)
        cp.start()
        cp.wait()

        def blk(b, carry):
            r0 = b * rb
            parts = []
            for rr in range(rb):
                base = pl.multiple_of((r0 + rr) * GRID_W, GRID_W)
                s = xp[pl.ds(base, GRID_W), :]
                for k in range(1, w):
                    s = s + xp[pl.ds(base + k * GRID_W, GRID_W), :]
                parts.append(s * ir_ref[r0 + rr])
            m = _mm_split(kc_ref[...], jnp.concatenate(parts, axis=0)) * ic_ref[...]
            u = xp[pl.ds(pl.multiple_of((r0 + lo) * GRID_W, GRID_W), tok), :]
            db = (m - u).astype(BF16)
            o0 = pl.multiple_of(r0 * GRID_W, GRID_W)
            dm_ref[pl.ds(o0, tok), :] = db
            lin_ref[pl.ds(o0, tok), :] = _mm(db, w_ref[...])
            return carry

        lax.fori_loop(0, rows // rb, blk, 0)

    return pl.pallas_call(
        body,
        name=name,
        in_specs=[_ANY, pl.BlockSpec(memory_space=pltpu.VMEM), pl.BlockSpec(memory_space=pltpu.VMEM), _SMEM, pl.BlockSpec(memory_space=pltpu.VMEM)],
        out_shape=[_sds((lx, pgw)), _sds((lx, pgw), BF16)],
        scratch_shapes=[pltpu.VMEM(((rows + w) * GRID_W, pgw), F32), pltpu.SemaphoreType.DMA],
        compiler_params=_cparams(0, VMEM_BIG),
    )(proj, pw_g, kc, inv_r, inv_c)


def _pool_bwd(dlin, col0, w, pw_g, dmat_g, name):
    lx = dlin.shape[0]
    pgw = pw_g.shape[0]
    rows = lx // GRID_W
    rb = min(4, rows)
    tok = rb * GRID_W
    front = w - w // 2 - 1
    kct, inv_r, inv_c = _pool_tables(w, rows, rb, pgw, True)

    def body(dl_hbm, w_ref, kc_ref, ir_ref, ic_ref, dm_ref, du_ref, dw_ref, tp, dl, sem):
        if front:
            tp[pl.ds(0, front * GRID_W), :] = jnp.zeros((front * GRID_W, pgw), F32)
        tp[pl.ds((front + rows) * GRID_W, (w - front) * GRID_W), :] = jnp.zeros(((w - front) * GRID_W, pgw), F32)
        cp = pltpu.make_async_copy(dl_hbm.at[:, pl.ds(col0, pgw)], dl, sem)
        cp.start()
        cp.wait()
        dw_ref[...] = jnp.zeros_like(dw_ref)

        def blk(b, carry):
            o0 = pl.multiple_of(b * tok, GRID_W)
            dlb = dl[pl.ds(o0, tok), :].astype(BF16)
            dw_ref[...] += _mm_tn(dm_ref[pl.ds(o0, tok), :], dlb)
            dd = _mm_nt(dlb, w_ref[...])
            du_ref[pl.ds(o0, tok), :] = -dd
            t = _mm_split(kc_ref[...], dd * ic_ref[...])
            for rr in range(rb):
                dst = pl.multiple_of((b * rb + rr + front) * GRID_W, GRID_W)
                tp[pl.ds(dst, GRID_W), :] = t[rr * GRID_W : (rr + 1) * GRID_W] * ir_ref[b * rb + rr]
            return carry

        lax.fori_loop(0, rows // rb, blk, 0)

        def rowl(r, carry):
            base = pl.multiple_of(r * GRID_W, GRID_W)
            s = tp[pl.ds(base, GRID_W), :]
            for k in range(1, w):
                s = s + tp[pl.ds(base + k * GRID_W, GRID_W), :]
            du_ref[pl.ds(base, GRID_W), :] += s
            return carry

        lax.fori_loop(0, rows, rowl, 0)

    vm = pl.BlockSpec(memory_space=pltpu.VMEM)
    return pl.pallas_call(
        body,
        name=name,
        in_specs=[_ANY, vm, vm, _SMEM, vm, vm],
        out_shape=[_sds((lx, pgw)), _sds((pgw, pgw))],
        scratch_shapes=[pltpu.VMEM(((rows + w) * GRID_W, pgw), F32), pltpu.VMEM((lx, pgw), F32), pltpu.SemaphoreType.DMA],
        compiler_params=_cparams(0, VMEM_BIG),
    )(dlin, pw_g, kct, inv_r, inv_c, dmat_g)


def _s5_ops(lam_re, lam_im, log_dt, b_re, b_im, c_re, c_im):
    hp = lax.Precision.HIGHEST
    t = CHUNK_T
    g, n = lam_re.shape[1], lam_re.shape[2]
    h = b_re.shape[3]
    dt = jnp.exp(log_dt)[..., None]
    zr, zi = lam_re * dt, lam_im * dt
    tau = jnp.arange(t + 1, dtype=F32)
    mag = jnp.exp(zr[..., None] * tau)
    pr, pi = mag * jnp.cos(zi[..., None] * tau), mag * jnp.sin(zi[..., None] * tau)
    ar, ai = pr[..., 1], pi[..., 1]
    den = lam_re * lam_re + lam_im * lam_im
    qr = ((ar - 1.0) * lam_re + ai * lam_im) / den
    qi = (ai * lam_re - (ar - 1.0) * lam_im) / den
    bbr = qr[..., None] * b_re - qi[..., None] * b_im
    bbi = qr[..., None] * b_im + qi[..., None] * b_re
    ctr, cti = jnp.swapaxes(c_re, 2, 3), jnp.swapaxes(c_im, 2, 3)
    rep = lambda p: jnp.repeat(p, h, axis=-1)
    til = lambda v: jnp.tile(v, (1,) * (v.ndim - 1) + (t,))
    cr16, ci16 = til(ctr), til(cti)

    def cpow(p_r, p_i):
        p_r, p_i = rep(p_r), rep(p_i)
        return cr16 * p_r - ci16 * p_i, cr16 * p_i + ci16 * p_r

    flip = lambda v: v[..., ::-1]
    p16r = jnp.stack([pr[0, ..., :t], flip(pr[1, ..., :t])])
    p16i = jnp.stack([pi[0, ..., :t], flip(pi[1, ..., :t])])
    car, cai = cpow(p16r, p16i)
    row = jnp.einsum("dgni,dgnk->dgik", bbr, car, precision=hp) - jnp.einsum("dgni,dgnk->dgik", bbi, cai, precision=hp)
    span = (t - 1) * h
    padf = jnp.pad(row[0], ((0, 0), (0, 0), (span, 0)))
    padb = jnp.pad(row[1], ((0, 0), (0, 0), (0, span)))
    tsum = jnp.stack(
        [padf[:, :, span - h * s : span - h * s + t * h] + padb[:, :, h * (t - 1 - s) : h * (t - 1 - s) + t * h] for s in range(t)], axis=1
    ).reshape(g, t * h, t * h)
    btr, bti = jnp.swapaxes(bbr, 2, 3), jnp.swapaxes(bbi, 2, 3)
    sw = lambda v: jnp.swapaxes(v, 1, 2)[:, :, None, :]

    def st(p_r, p_i, d):
        p_r, p_i = sw(p_r), sw(p_i)
        b_r, b_i = btr[d][:, None], bti[d][:, None]
        return (p_r * b_r - p_i * b_i).reshape(g, t * h, n), (p_r * b_i + p_i * b_r).reshape(g, t * h, n)

    wfr, wfi = st(flip(pr[0, ..., :t]), flip(pi[0, ..., :t]), 0)
    wbr, wbi = st(pr[1, ..., :t], pi[1, ..., :t], 1)
    wst = jnp.concatenate([wfr, wbr, wfi, wbi], axis=2)
    c1r, c1i = cpow(jnp.stack([pr[0, ..., 1:], flip(pr[1, ..., 1:])]), jnp.stack([pi[0, ..., 1:], flip(pi[1, ..., 1:])]))
    ccat = jnp.concatenate([c1r[0], c1r[1], -c1i[0], -c1i[1]], axis=1)
    a16 = jnp.concatenate([pr[0, ..., t], pr[1, ..., t], pi[0, ..., t], pi[1, ..., t]], axis=1)
    return tsum, wst, ccat, a16


SCAN_UNROLL = 4
GPL = LANES // SSM_H
CW = CHUNK_T * SSM_H
BW = CHUNK_T * LANES


def _chunk_perm():
    o = np.arange(BW)
    src = ((o % CW) // SSM_H) * LANES + (o // CW) * SSM_H + o % SSM_H
    return jnp.asarray(np.arange(BW)[:, None] == src[None, :], BF16)


def _gspec(gb, a, b):
    return pl.BlockSpec((gb, a, b), lambda i: (i, 0, 0))


def _chunk_rows(ref, n):
    return jnp.concatenate([ref[pl.ds(s, n, stride=CHUNK_T), :] for s in range(CHUNK_T)], axis=1)


def _s5_state(proj, col0, uc, perm, wst, ncc, ncl):
    nlb = uc.shape[1] // LANES
    g, _, ns = wst.shape
    nch = ncc + ncl
    half = ns // 2

    def body(p_ref, c_ref, pm_ref, w_ref, u_ref, sr_ref, si_ref):
        xrows = jnp.concatenate([_chunk_rows(c_ref, ncc), _chunk_rows(p_ref, ncl)], axis=0).astype(BF16)
        u = _mm(xrows, pm_ref[...]).astype(BF16)
        u_ref[0] = u
        for gi in range(GPL):
            s = _mm(u[:, gi * CW : (gi + 1) * CW], w_ref[gi].astype(BF16))
            sr_ref[gi] = s[:, :half]
            si_ref[gi] = s[:, half:]

    return pl.pallas_call(
        body,
        name="s5_state",
        grid=(nlb,),
        in_specs=[
            pl.BlockSpec((ncl * CHUNK_T, LANES), lambda i: (0, col0 + i)),
            pl.BlockSpec((ncc * CHUNK_T, LANES), lambda i: (0, i)),
            _whole(perm.shape),
            _gspec(GPL, CW, ns),
        ],
        out_specs=[pl.BlockSpec((1, nch, BW), lambda i: (i, 0, 0)), _gspec(GPL, nch, half), _gspec(GPL, nch, half)],
        out_shape=[_sds((nlb, nch, BW), BF16), _sds((g, nch, half)), _sds((g, nch, half))],
        compiler_params=_cparams(1, VMEM_BIG),
    )(proj, uc, perm, wst)


def _unrolled_loop(n, step, init):
    u = SCAN_UNROLL if n % SCAN_UNROLL == 0 else 1

    def trip(i, c):
        for k in range(u):
            c = step(i * u + k, c)
        return c

    return lax.fori_loop(0, n // u, trip, init)


def _lane_masks(gb, half):
    lane = lax.broadcasted_iota(jnp.int32, (gb, half), 1)
    return lane < (half // 2)


def _s5_scan_fwd(s_re, s_im, a16, ncc, nch, gb):
    g, ns = a16.shape
    half = ns // 2

    def body(sr_ref, si_ref, a_ref, hr_ref, hi_ref):
        ar, ai = a_ref[:, :half], a_ref[:, half:]
        mf = _lane_masks(gb, half)
        zero = jnp.zeros((gb, half), F32)
        hr_ref[...] = jnp.zeros_like(hr_ref)
        hi_ref[...] = jnp.zeros_like(hi_ref)

        def step(t, c):
            fr, fi, br, bi = c
            rf = pl.ds(t, gb, stride=nch)
            rb = pl.ds(jnp.where(t < ncc, ncc - 1 - t, nch - 1 - (t - ncc)), gb, stride=nch)
            hr_ref[rf, :] += jnp.where(mf, fr, 0.0)
            hi_ref[rf, :] += jnp.where(mf, fi, 0.0)
            hr_ref[rb, :] += jnp.where(mf, 0.0, br)
            hi_ref[rb, :] += jnp.where(mf, 0.0, bi)
            return (
                ar * fr - ai * fi + sr_ref[rf, :],
                ar * fi + ai * fr + si_ref[rf, :],
                ar * br - ai * bi + sr_ref[rb, :],
                ar * bi + ai * br + si_ref[rb, :],
            )

        _unrolled_loop(nch, step, (zero, zero, zero, zero))

    blk = pl.BlockSpec((gb * nch, half), lambda i: (i, 0))
    return pl.pallas_call(
        body,
        name="s5_scan_fwd",
        grid=(g // gb,),
        in_specs=[blk, blk, pl.BlockSpec((gb, ns), lambda i: (i, 0))],
        out_specs=[blk, blk],
        out_shape=[_sds((g * nch, half))] * 2,
        compiler_params=_cparams(1, VMEM_BIG),
    )(s_re, s_im, a16)


def _token_rows_store(ref, val, n):
    for s in range(CHUNK_T):
        ref[pl.ds(s, n, stride=CHUNK_T), :] = val[:, s * LANES : (s + 1) * LANES]


def _s5_out(u_all, h_re, h_im, tsum, ccat, perm, ncc, lx):
    nlb, nch, _ = u_all.shape
    g, ns, _ = ccat.shape
    ncl = nch - ncc
    half = ns // 2

    def body(u_ref, hr_ref, hi_ref, t_ref, c_ref, pm_ref, y_ref):
        parts = []
        for gi in range(GPL):
            u = u_ref[0, ncc:, gi * CW : (gi + 1) * CW]
            hs = jnp.concatenate([hr_ref[gi, ncc:, :], hi_ref[gi, ncc:, :]], axis=1).astype(BF16)
            parts.append(_mm(u, t_ref[gi].astype(BF16)) + _mm(hs, c_ref[gi].astype(BF16)))
        y = jnp.concatenate(parts, axis=1)
        hi = y.astype(BF16)
        lo = (y - hi.astype(F32)).astype(BF16)
        _token_rows_store(y_ref, _mm_nt(hi, pm_ref[...]) + _mm_nt(lo, pm_ref[...]), ncl)

    return pl.pallas_call(
        body,
        name="s5_out",
        grid=(nlb,),
        in_specs=[pl.BlockSpec((1, nch, BW), lambda i: (i, 0, 0)), _gspec(GPL, nch, half), _gspec(GPL, nch, half), _gspec(GPL, CW, CW), _gspec(GPL, ns, CW), _whole(perm.shape)],
        out_specs=pl.BlockSpec((lx, LANES), lambda i: (0, i)),
        out_shape=_sds((lx, nlb * LANES)),
        compiler_params=_cparams(1, VMEM_BIG),
    )(u_all, h_re, h_im, tsum, ccat, perm)


def _s5_dstate(dypre, perm, ccat, ncc):
    lx, sw = dypre.shape
    nlb = sw // LANES
    g, ns, _ = ccat.shape
    ncl = lx // CHUNK_T
    nch = ncl + ncc
    half = ns // 2

    def body(dy_ref, pm_ref, c_ref, dyr_ref, dr_ref, di_ref):
        dy = _mm(_chunk_rows(dy_ref, ncl).astype(BF16), pm_ref[...]).astype(BF16)
        dyr_ref[0] = dy
        for gi in range(GPL):
            dh = _mm_nt(dy[:, gi * CW : (gi + 1) * CW], c_ref[gi].astype(BF16))
            dr_ref[gi, :ncc, :] = jnp.zeros((ncc, half), F32)
            di_ref[gi, :ncc, :] = jnp.zeros((ncc, half), F32)
            dr_ref[gi, ncc:, :] = dh[:, :half]
            di_ref[gi, ncc:, :] = dh[:, half:]

    return pl.pallas_call(
        body,
        name="s5_dstate",
        grid=(nlb,),
        in_specs=[pl.BlockSpec((lx, LANES), lambda i: (0, i)), _whole(perm.shape), _gspec(GPL, ns, CW)],
        out_specs=[pl.BlockSpec((1, ncl, BW), lambda i: (i, 0, 0)), _gspec(GPL, nch, half), _gspec(GPL, nch, half)],
        out_shape=[_sds((nlb, ncl, BW), BF16), _sds((g, nch, half)), _sds((g, nch, half))],
        compiler_params=_cparams(1, VMEM_BIG),
    )(dypre, perm, ccat)


def _s5_scan_bwd(dh_re, dh_im, a16, ncc, nch, gb):
    g, ns = a16.shape
    half = ns // 2
    ncl = nch - ncc

    def body(dr_ref, di_ref, a_ref, sr_ref, si_ref):
        ar, ai = a_ref[:, :half], a_ref[:, half:]
        mf = _lane_masks(gb, half)
        zero = jnp.zeros((gb, half), F32)
        sr_ref[...] = jnp.zeros_like(sr_ref)
        si_ref[...] = jnp.zeros_like(si_ref)

        def step(t, c):
            lfr, lfi, lbr, lbi = c
            pf = pl.ds(nch - 1 - t, gb, stride=nch)
            pb = pl.ds(jnp.where(t < ncl, ncc + t, t - ncl), gb, stride=nch)
            sr_ref[pf, :] += jnp.where(mf, lfr, 0.0)
            si_ref[pf, :] += jnp.where(mf, lfi, 0.0)
            sr_ref[pb, :] += jnp.where(mf, 0.0, lbr)
            si_ref[pb, :] += jnp.where(mf, 0.0, lbi)
            return (
                dr_ref[pf, :] + ar * lfr + ai * lfi,
                di_ref[pf, :] + ar * lfi - ai * lfr,
                dr_ref[pb, :] + ar * lbr + ai * lbi,
                di_ref[pb, :] + ar * lbi - ai * lbr,
            )

        _unrolled_loop(nch, step, (zero,) * 4)

    blk = pl.BlockSpec((gb * nch, half), lambda i: (i, 0))
    return pl.pallas_call(
        body,
        name="s5_scan_bwd",
        grid=(g // gb,),
        in_specs=[blk, blk, pl.BlockSpec((gb, ns), lambda i: (i, 0))],
        out_specs=[blk, blk],
        out_shape=[_sds((g * nch, half))] * 2,
        compiler_params=_cparams(1, VMEM_BIG),
    )(dh_re, dh_im, a16)


def _s5_du(dy_all, ds_re, ds_im, tsum, wst, perm, ncc, lx, lc):
    nlb, ncl, _ = dy_all.shape
    g, _, ns = wst.shape
    nch = ncc + ncl
    half = ns // 2

    def body(dy_ref, sr_ref, si_ref, t_ref, w_ref, pm_ref, du_ref, dc_ref):
        parts = []
        for gi in range(GPL):
            ds = jnp.concatenate([sr_ref[gi], si_ref[gi]], axis=1).astype(BF16)
            d_all = _mm_nt(ds, w_ref[gi].astype(BF16))
            d_lat = d_all[ncc:] + _mm_nt(dy_ref[0, :, gi * CW : (gi + 1) * CW], t_ref[gi].astype(BF16))
            parts.append(jnp.concatenate([d_all[:ncc], d_lat], axis=0))
        du = _mm_nt(jnp.concatenate(parts, axis=1).astype(BF16), pm_ref[...])
        _token_rows_store(dc_ref, du[:ncc], ncc)
        _token_rows_store(du_ref, du[ncc:], ncl)

    return pl.pallas_call(
        body,
        name="s5_du",
        grid=(nlb,),
        in_specs=[pl.BlockSpec((1, ncl, BW), lambda i: (i, 0, 0)), _gspec(GPL, nch, half), _gspec(GPL, nch, half), _gspec(GPL, CW, CW), _gspec(GPL, CW, ns), _whole(perm.shape)],
        out_specs=[pl.BlockSpec((lx, LANES), lambda i: (0, i)), pl.BlockSpec((lc, LANES), lambda i: (0, i))],
        out_shape=[_sds((lx, nlb * LANES)), _sds((lc, nlb * LANES))],
        compiler_params=_cparams(1, VMEM_BIG),
    )(dy_all, ds_re, ds_im, tsum, wst, perm)


def _s5_op_grads(u_all, dy_all, ds_re, ds_im, h_re, h_im, ncc):
    nlb, nch, _ = u_all.shape
    g, _, half = ds_re.shape
    ns = 2 * half

    def body(u_ref, dy_ref, sr_ref, si_ref, hr_ref, hi_ref, dw_ref, dt_ref, dc_ref, da_ref):
        das = []
        for gi in range(GPL):
            cols = slice(gi * CW, (gi + 1) * CW)
            sr, si, hr, hi = sr_ref[gi], si_ref[gi], hr_ref[gi], hi_ref[gi]
            ds = jnp.concatenate([sr, si], axis=1).astype(BF16)
            dy = dy_ref[0, :, cols]
            dw_ref[gi] = _mm_tn(u_ref[0, :, cols], ds)
            dt_ref[gi] = _mm_tn(u_ref[0, ncc:, cols], dy)
            hs = jnp.concatenate([hr[ncc:], hi[ncc:]], axis=1).astype(BF16)
            dc_ref[gi] = _mm_tn(hs, dy)
            das.append(jnp.concatenate([_colsum(hr * sr + hi * si), _colsum(hr * si - hi * sr)], axis=1))
        da_ref[...] = jnp.concatenate(das, axis=0)

    return pl.pallas_call(
        body,
        name="s5_op_grads",
        grid=(nlb,),
        in_specs=[pl.BlockSpec((1, nch, BW), lambda i: (i, 0, 0)), pl.BlockSpec((1, nch - ncc, BW), lambda i: (i, 0, 0))] + [_gspec(GPL, nch, half)] * 4,
        out_specs=[_gspec(GPL, CW, ns), _gspec(GPL, CW, CW), _gspec(GPL, ns, CW), pl.BlockSpec((GPL, ns), lambda i: (i, 0))],
        out_shape=[_sds((g, CW, ns)), _sds((g, CW, CW)), _sds((g, ns, CW)), _sds((g, ns))],
        compiler_params=_cparams(1, VMEM_BIG),
    )(u_all, dy_all, ds_re, ds_im, h_re, h_im)


def _merge_fwd(y_ssm, proj, lins, x, tgt, glu_wg, out_w, d_skip, glu_b, pscale, gate, fin_g, tm):
    lx, d = x.shape
    sw = y_ssm.shape[1]
    mix = out_w.shape[0]
    pw = mix - sw
    pgw = lins[0].shape[1]
    cb2 = glu_wg.shape[2]
    nh = NDEV // 2

    def body(y_ref, u_ref, z_ref, l0, l1, l2, l3, x_ref, t_ref, gw_ref, ow_ref, dk_ref, gb_ref, ps_ref, gt_ref, fg_ref,
             ypre_ref, y12_ref, br_ref, dxo_ref, loss_ref, dfg_ref, dgt_ref):
        @pl.when(pl.program_id(0) == 0)
        def _():
            loss_ref[...] = jnp.zeros_like(loss_ref)
            dfg_ref[...] = jnp.zeros_like(dfg_ref)
            dgt_ref[...] = jnp.zeros_like(dgt_ref)

        ypre = y_ref[...] + dk_ref[...] * u_ref[...]
        ypre_ref[...] = ypre
        yg = _gelu(ypre).astype(BF16)
        outs = []
        for j in range(nh):
            y1 = _mm(yg, gw_ref[j]) + gb_ref[:, j * cb2 : (j + 1) * cb2]
            y2 = _mm(yg, gw_ref[nh + j]) + gb_ref[:, (nh + j) * cb2 : (nh + j + 1) * cb2]
            y12_ref[:, j * cb2 : (j + 1) * cb2] = y1
            y12_ref[:, (nh + j) * cb2 : (nh + j + 1) * cb2] = y2
            outs.append(y1 * _sigmoid(y2))
        lin = jnp.concatenate([l0[...], l1[...], l2[...], l3[...]], axis=1) * ps_ref[...]
        brb = (jnp.concatenate([lin] + outs, axis=1) * _silu(z_ref[...])).astype(BF16)
        br_ref[...] = brb
        mixv = _mm(brb, ow_ref[...])
        xo = x_ref[...] + gt_ref[...] * mixv
        r2 = lax.rsqrt(_rowmean(xo * xo) + EPS)
        xh = xo * r2
        fg = fg_ref[...]
        e = xh * fg - t_ref[...]
        loss_ref[...] += 0.5 * _colsum(_rowmean(e * e))
        dy = e * (1.0 / d)
        dfg_ref[...] += _colsum(dy * xh)
        gy = dy * fg
        dxo = r2 * (gy - xh * _rowmean(gy * xh))
        dxo_ref[...] = dxo
        dgt_ref[...] += _colsum(dxo * mixv)

    vec = _whole((1, d))
    return pl.pallas_call(
        body,
        name="merge_fwd",
        grid=(lx // tm,),
        in_specs=[_rows(tm, sw), _rows(tm, sw, 1), _rows(tm, mix, 1)] + [_rows(tm, pgw)] * 4 + [_rows(tm, d), _rows(tm, d), _whole(glu_wg.shape), _whole(out_w.shape), _whole((1, sw)), _whole((1, 2 * sw)), _whole((1, pw)), vec, vec],
        out_specs=[_rows(tm, sw), _rows(tm, 2 * sw), _rows(tm, mix), _rows(tm, d), _acc((1, 1)), _acc((1, d)), _acc((1, d))],
        out_shape=[_sds((lx, sw)), _sds((lx, 2 * sw)), _sds((lx, mix), BF16), _sds((lx, d)), _sds((1, 1)), _sds((1, d)), _sds((1, d))],
        compiler_params=_cparams(1, VMEM_BIG),
    )(y_ssm, proj, proj, *lins, x, tgt, glu_wg, out_w, d_skip, glu_b, pscale, gate, fin_g)


def _out_bwd(dxo, gate, branch, out_w, lins, y12, proj, pscale, tm):
    lx, d = dxo.shape
    mix = out_w.shape[0]
    pgw = lins[0].shape[1]
    pw = 4 * pgw
    sw = mix - pw
    nb = lx // tm

    def body(dxo_ref, gt_ref, br_ref, ow_ref, l0, l1, l2, l3, y_ref, z_ref, ps_ref, dz_ref, dl_ref, dg_ref, dps_ref, dow_hbm, acc, sem):
        i = pl.program_id(0)

        @pl.when(i == 0)
        def _():
            acc[...] = jnp.zeros_like(acc)
            dps_ref[...] = jnp.zeros_like(dps_ref)

        dmix = (dxo_ref[...] * gt_ref[...]).astype(BF16)
        dbr = _mm_nt(dmix, ow_ref[...])
        acc[...] += _mm_tn(br_ref[...], dmix)
        z = z_ref[...]
        sg = _sigmoid(z)
        dbp = dbr * (z * sg)
        y1, y2 = y_ref[:, :sw], y_ref[:, sw:]
        s2 = _sigmoid(y2)
        ps = ps_ref[...]
        lin = jnp.concatenate([l0[...], l1[...], l2[...], l3[...]], axis=1)
        bp = jnp.concatenate([lin * ps, y1 * s2], axis=1)
        dz_ref[...] = dbr * bp * (sg * (1.0 + z * (1.0 - sg)))
        dlp = dbp[:, :pw]
        dl_ref[...] = dlp * ps
        dps_ref[...] += _colsum(dlp * lin)
        dss = dbp[:, pw:]
        dg_ref[...] = jnp.concatenate([dss * s2, dss * y1 * s2 * (1.0 - s2)], axis=1).astype(BF16)

        @pl.when(i == nb - 1)
        def _():
            cp = pltpu.make_async_copy(acc, dow_hbm, sem)
            cp.start()
            cp.wait()

    return pl.pallas_call(
        body,
        name="out_bwd",
        grid=(nb,),
        in_specs=[_rows(tm, d), _whole((1, d)), _rows(tm, mix), _whole(out_w.shape)] + [_rows(tm, pgw)] * 4 + [_rows(tm, 2 * sw), _rows(tm, mix, 1), _whole((1, pw))],
        out_specs=[_rows(tm, mix), _rows(tm, pw), _rows(tm, 2 * sw), _acc((1, pw)), _ANY],
        out_shape=[_sds((lx, mix)), _sds((lx, pw)), _sds((lx, 2 * sw), BF16), _sds((1, pw)), _sds((mix, d))],
        scratch_shapes=[pltpu.VMEM((mix, d), F32), pltpu.SemaphoreType.DMA],
        compiler_params=_cparams(1, VMEM_BIG),
    )(dxo, gate, branch, out_w, *lins, y12, proj, pscale)


def _glu_bwd(dg12, ypre, proj, glu_wg, d_skip, tm):
    lx, sw = ypre.shape
    cb2 = glu_wg.shape[2]
    nb = lx // tm

    def body(dg_ref, yp_ref, u_ref, gw_ref, dk_ref, dyp_ref, dgb_ref, dd_ref, dgw_hbm, acc, sem):
        i = pl.program_id(0)

        @pl.when(i == 0)
        def _():
            acc[...] = jnp.zeros_like(acc)
            dgb_ref[...] = jnp.zeros_like(dgb_ref)
            dd_ref[...] = jnp.zeros_like(dd_ref)

        ypre = yp_ref[...]
        ygb = _gelu(ypre).astype(BF16)
        dg = dg_ref[...]
        dyg = jnp.zeros((tm, sw), F32)
        for j in range(NDEV):
            dgj = dg[:, j * cb2 : (j + 1) * cb2]
            dyg = dyg + _mm_nt(dgj, gw_ref[j])
            acc[j] += _mm_tn(ygb, dgj)
        dgb_ref[...] += _colsum(dg.astype(F32))
        dyp = dyg * _gelu_grad(ypre)
        dyp_ref[...] = dyp
        dd_ref[...] += _colsum(dyp * u_ref[...])

        @pl.when(i == nb - 1)
        def _():
            cp = pltpu.make_async_copy(acc, dgw_hbm, sem)
            cp.start()
            cp.wait()

    return pl.pallas_call(
        body,
        name="glu_bwd",
        grid=(nb,),
        in_specs=[_rows(tm, 2 * sw), _rows(tm, sw), _rows(tm, sw, 1), _whole(glu_wg.shape), _whole((1, sw))],
        out_specs=[_rows(tm, sw), _acc((1, 2 * sw)), _acc((1, sw)), _ANY],
        out_shape=[_sds((lx, sw)), _sds((1, 2 * sw)), _sds((1, sw)), _sds(glu_wg.shape)],
        scratch_shapes=[pltpu.VMEM(glu_wg.shape, F32), pltpu.SemaphoreType.DMA],
        compiler_params=_cparams(1, VMEM_BIG),
    )(dg12, ypre, proj, glu_wg, d_skip)


def _pack(parts, total):
    flat = jnp.concatenate([p.reshape(-1).astype(F32) for p in parts])
    return jnp.pad(flat, (0, total - flat.shape[0]))


def _unpack(flat, shapes):
    out, off = [], 0
    for s in shapes:
        n = int(np.prod(s)) if len(s) else 1
        out.append(flat[off : off + n].reshape(s))
        off += n
    return out


def kernel(x, c, ctx, c_ctx, ada_w, ada_b, norm_g, in_w, pool_w, pool_scale, s5_lam_re, s5_lam_im, s5_log_dt, s5_b_re, s5_b_im, s5_c_re, s5_c_im, s5_d, glu_w, glu_b, out_w, final_g, loss_target, m_c_ctx, m_ada_w, m_ada_b, m_norm_g, m_in_w, m_pool_w, m_pool_scale, m_s5_lam_re, m_s5_lam_im, m_s5_log_dt, m_s5_b_re, m_s5_b_im, m_s5_c_re, m_s5_c_im, m_s5_d, m_glu_w, m_glu_b, m_out_w, m_final_g, v_c_ctx, v_ada_w, v_ada_b, v_norm_g, v_in_w, v_pool_w, v_pool_scale, v_s5_lam_re, v_s5_lam_im, v_s5_log_dt, v_s5_b_re, v_s5_b_im, v_s5_c_re, v_s5_c_im, v_s5_d, v_glu_w, v_glu_b, v_out_w, v_final_g):
    xr, tgt, xc = x[0], loss_target[0], ctx[0]
    lx, d = xr.shape
    lc = xc.shape[0]
    mix = out_w.shape[1] * NDEV
    sw = glu_w.shape[1]
    pw = mix - sw
    pgw = pw // len(POOL_WINDOWS)
    ngrp = sw // SSM_H
    cb = in_w.shape[2]
    cb2 = glu_w.shape[2]
    nmod = ada_w.shape[2]
    ncc, ncl = lc // CHUNK_T, lx // CHUNK_T
    nch = ncc + ncl
    tm = min(256, lx)
    tm_acc = min(512, lx)
    assert mix == d and pw == sw and 2 * cb == pw and NDEV * cb2 == 2 * sw and lx % GRID_W == 0
    mx, my, mc = _my_pos()
    me = 4 * mx + 2 * my + mc

    (c_all,) = _all_gather([c], "gather_c")
    call = jnp.concatenate([c_all.reshape(NDEV, d), c_ctx.reshape(1, d), jnp.zeros((NDEV - 1, d), F32)], axis=0)
    ada_w_l = ada_w[0]
    ada_b_l = lax.dynamic_slice_in_dim(ada_b, me * nmod, nmod, axis=1)
    m_loc = _mod_fwd(call, ada_w_l, ada_b_l)
    (m_all,) = _all_gather([m_loc], "gather_mod")
    mod = lax.dynamic_index_in_dim(m_all, me, axis=1, keepdims=False).reshape(1, NDEV * nmod)
    mod_c = m_all[:, NDEV, :].reshape(1, NDEV * nmod)
    shift, scale, gate = mod[:, :d], mod[:, d : 2 * d], mod[:, 2 * d :]
    shift_c, scale_c = mod_c[:, :d], mod_c[:, d : 2 * d]

    in_wg, glu_wg, out_wg, pool_wg = _all_gather(
        [in_w[0].astype(BF16), glu_w[0].astype(BF16), out_w[0].astype(BF16), pool_w[0].reshape(-1, pgw).astype(BF16)], "gather_weights"
    )
    out_wf = out_wg.reshape(mix, d)
    pool_wf = pool_wg.reshape(NDEV, len(POOL_WINDOWS), pgw // NDEV, pgw).transpose(1, 0, 2, 3).reshape(len(POOL_WINDOWS), pgw, pgw)

    ssm_blocks = (2, 3)
    proj, hb = _in_proj(xr, norm_g, scale, shift, in_wg, tuple(range(NDEV)), tm, "in_proj")
    uc, hcb = _in_proj(xc, norm_g, scale_c, shift_c, in_wg, ssm_blocks, min(tm, lc), "in_proj_ctx")
    lins, dmats = [], []
    for gi, w in enumerate(POOL_WINDOWS):
        lin_g, dmat_g = _pool_fwd(proj, gi * pgw, w, pool_wf[gi], f"pool_fwd{gi}")
        lins.append(lin_g)
        dmats.append(dmat_g)

    s5_params = tuple(p[0] for p in (s5_lam_re, s5_lam_im, s5_log_dt, s5_b_re, s5_b_im, s5_c_re, s5_c_im))
    (tsum, wst, ccat, a16), ops_vjp = jax.vjp(_s5_ops, *s5_params)
    gb_scan = min(16, ngrp)
    half = wst.shape[2] // 2
    perm = _chunk_perm()
    flat = lambda a: a.reshape(ngrp * nch, half)
    grp = lambda a: a.reshape(ngrp, nch, half)
    u_all, s_re, s_im = _s5_state(proj, pw // LANES, uc, perm, wst, ncc, ncl)
    h_re, h_im = _s5_scan_fwd(flat(s_re), flat(s_im), a16, ncc, nch, gb_scan)
    y_ssm = _s5_out(u_all, grp(h_re), grp(h_im), tsum, ccat, perm, ncc, lx)

    ypre, y12, branch, dxo, loss_l, dfg, dgate = _merge_fwd(
        y_ssm, proj, lins, xr, tgt, glu_wg, out_wf, s5_d, glu_b, pool_scale, gate, final_g.reshape(1, d), tm
    )

    dz, dlin, dg12, dps, d_out_w = _out_bwd(dxo, gate, branch, out_wf, lins, y12, proj, pool_scale, tm)
    dypre, dglu_b, dd_skip, d_glu_w = _glu_bwd(dg12, ypre, proj, glu_wg, s5_d, tm_acc)

    dy_all, dh_re, dh_im = _s5_dstate(dypre, perm, ccat, ncc)
    ds_re, ds_im = _s5_scan_bwd(flat(dh_re), flat(dh_im), a16, ncc, nch, gb_scan)
    du5, duc = _s5_du(dy_all, grp(ds_re), grp(ds_im), tsum, wst, perm, ncc, lx, lc)
    dwst, dtsum, dccat, da16 = _s5_op_grads(u_all, dy_all, grp(ds_re), grp(ds_im), grp(h_re), grp(h_im), ncc)
    ds5 = ops_vjp((dtsum, dwst, dccat, da16))

    dus, dpool_w = [], []
    for gi, w in enumerate(POOL_WINDOWS):
        du_g, dw_g = _pool_bwd(dlin, gi * pgw, w, pool_wf[gi], dmats[gi], f"pool_bwd{gi}")
        dus.append(du_g)
        dpool_w.append(dw_g)

    dpcb, dsc_c, dsh_c, dg_c = _in_bwd_ctx(xc, duc, in_wg, ssm_blocks, norm_g, scale_c, shift_c)
    grad_x, dpb, dsc, dsh, dnorm_g = _in_bwd(xr, dxo, dus, du5, dypre, dz, in_wg, norm_g, scale, shift, s5_d, dg_c, tm)
    d_in_w = _in_w_grad(hb, dpb, hcb, dpcb, cb, tm_acc)

    dmod = jnp.concatenate(
        [jnp.concatenate([dsh, dsc, dgate], axis=1), jnp.concatenate([dsh_c, dsc_c, jnp.zeros((1, d), F32)], axis=1)], axis=0
    )
    (dmod_all,) = _all_gather([dmod], "gather_dmod")
    dmod_l = lax.dynamic_slice_in_dim(dmod_all, me * nmod, nmod, axis=2)
    g_ada_w, g_ada_b_l, cctx_part = _mod_bwd(call, dmod_l[:, 0, :], dmod_l[:, 1, :], ada_w_l)

    npw = len(POOL_WINDOWS)
    dpool_s = jnp.stack(dpool_w).reshape(npw, NDEV, pgw // NDEV, pgw).transpose(1, 0, 2, 3).reshape(NDEV, npw * (pgw // NDEV), pgw)
    land_in, land_glu, land_out, land_pool = _reduce_scatter(
        [d_in_w, d_glu_w, d_out_w.reshape(NDEV, mix // NDEV, d), dpool_s], [BF16] * 4, "grads"
    )

    small_names = ["norm_g", "pool_scale", "s5_lam_re", "s5_lam_im", "s5_log_dt", "s5_b_re", "s5_b_im", "s5_c_re", "s5_c_im", "s5_d", "glu_b", "final_g", "c_ctx"]
    small_w = dict(norm_g=norm_g, pool_scale=pool_scale, s5_lam_re=s5_lam_re, s5_lam_im=s5_lam_im, s5_log_dt=s5_log_dt, s5_b_re=s5_b_re, s5_b_im=s5_b_im, s5_c_re=s5_c_re, s5_c_im=s5_c_im, s5_d=s5_d, glu_b=glu_b, final_g=final_g, c_ctx=c_ctx)
    small_m = dict(norm_g=m_norm_g, pool_scale=m_pool_scale, s5_lam_re=m_s5_lam_re, s5_lam_im=m_s5_lam_im, s5_log_dt=m_s5_log_dt, s5_b_re=m_s5_b_re, s5_b_im=m_s5_b_im, s5_c_re=m_s5_c_re, s5_c_im=m_s5_c_im, s5_d=m_s5_d, glu_b=m_glu_b, final_g=m_final_g, c_ctx=m_c_ctx)
    small_v = dict(norm_g=v_norm_g, pool_scale=v_pool_scale, s5_lam_re=v_s5_lam_re, s5_lam_im=v_s5_lam_im, s5_log_dt=v_s5_log_dt, s5_b_re=v_s5_b_re, s5_b_im=v_s5_b_im, s5_c_re=v_s5_c_re, s5_c_im=v_s5_c_im, s5_d=v_s5_d, glu_b=v_glu_b, final_g=v_final_g, c_ctx=v_c_ctx)
    small_g = dict(norm_g=dnorm_g, pool_scale=dps, s5_lam_re=ds5[0], s5_lam_im=ds5[1], s5_log_dt=ds5[2], s5_b_re=ds5[3], s5_b_im=ds5[4], s5_c_re=ds5[5], s5_c_im=ds5[6], s5_d=dd_skip, glu_b=dglu_b, final_g=dfg, c_ctx=cctx_part)
    shapes = [small_w[k].shape for k in small_names]
    nsmall = sum(int(np.prod(s)) for s in shapes) + 1
    unit = NDEV * 8 * LANES
    tot = -(-nsmall // unit) * unit
    rows = tot // LANES
    gpack = _pack([small_g[k] for k in small_names] + [loss_l], tot).reshape(NDEV, rows // NDEV, LANES)
    (land_small,) = _reduce_scatter([gpack], [F32], "small")
    gsum = _sum_slots(land_small, "sum_small")
    (gall,) = _all_gather([gsum], "gather_small")
    (g_ada_b_all,) = _all_gather([g_ada_b_l], "gather_ada_b")
    gflat = gall.reshape(tot)
    loss = gflat[nsmall - 1]

    nab = NDEV * nmod
    tot2 = tot + -(-nab // (8 * LANES)) * (8 * LANES)
    g2 = jnp.concatenate([gflat, jnp.pad(g_ada_b_all.reshape(nab), (0, tot2 - tot - nab))]).reshape(tot2 // LANES, LANES)

    def pack2(dct, ab):
        return jnp.concatenate(
            [_pack([dct[k] for k in small_names] + [jnp.zeros((1,), F32)], tot), jnp.pad(ab.reshape(nab), (0, tot2 - tot - nab))]
        ).reshape(tot2 // LANES, LANES)

    _, d2, m2, v2 = _adam(g2, pack2(small_w, ada_b), pack2(small_m, m_ada_b), pack2(small_v, v_ada_b), "adam_small")

    def unpack2(flat2):
        flat2 = flat2.reshape(tot2)
        vals = dict(zip(small_names, _unpack(flat2, shapes)))
        vals["ada_b"] = flat2[tot : tot + nab].reshape(ada_b.shape)
        return vals

    res = {k: unpack2(a) for k, a in (("g", g2), ("d", d2), ("m", m2), ("v", v2))}

    def shard(name, g, land, w, m, v):
        shp = w.shape
        w2, m2_, v2_ = (a.reshape(-1, shp[-1]) for a in (w, m, v))
        out = _adam(g, w2, m2_, v2_, "adam_" + name, land=land)
        for k, a in zip(("g", "d", "m", "v"), out):
            res[k][name] = a.reshape(shp)

    shard("ada_w", g_ada_w, None, ada_w, m_ada_w, v_ada_w)
    shard("in_w", None, land_in, in_w, m_in_w, v_in_w)
    shard("pool_w", None, land_pool, pool_w, m_pool_w, v_pool_w)
    shard("glu_w", None, land_glu, glu_w, m_glu_w, v_glu_w)
    shard("out_w", None, land_out, out_w, m_out_w, v_out_w)

    names = ["c_ctx", "ada_w", "ada_b", "norm_g", "in_w", "pool_w", "pool_scale", "s5_lam_re", "s5_lam_im", "s5_log_dt", "s5_b_re", "s5_b_im", "s5_c_re", "s5_c_im", "s5_d", "glu_w", "glu_b", "out_w", "final_g"]
    return (loss, grad_x[None], *[res["g"][n] for n in names], *[res["d"][n] for n in names], *[res["m"][n] for n in names], *[res["v"][n] for n in names])
```

```python
import numpy as np

import jax
import jax.numpy as jnp
from jax import lax
from jax.experimental import pallas as pl
from jax.experimental.pallas import tpu as pltpu

F32 = jnp.float32
BF16 = jnp.bfloat16
NDEV = 8
EPS = 1e-6
GRID_W = 64
POOL_WINDOWS = (2, 4, 8, 16)
SSM_H = 16
CHUNK_T = 16
LANES = 128
ADAM_LR, ADAM_B1, ADAM_B2, ADAM_EPS, ADAM_WD, ADAM_STEP = 0.001, 0.9, 0.999, 1e-08, 0.01, 10
VMEM_BIG = 56 << 20
MESH_ID = pl.DeviceIdType.MESH

_HBM = pl.BlockSpec(memory_space=pltpu.HBM)
_ANY = pl.BlockSpec(memory_space=pl.ANY)
_SMEM = pl.BlockSpec(memory_space=pltpu.SMEM)


def _sds(shape, dtype=F32):
    return jax.ShapeDtypeStruct(tuple(shape), dtype)


def _cparams(ngrid=0, vmem=None):
    return pltpu.CompilerParams(
        dimension_semantics=("arbitrary",) * ngrid if ngrid else None, vmem_limit_bytes=vmem
    )


def _rows(tm, c, col=0):
    return pl.BlockSpec((tm, c), lambda i: (i, col))


def _whole(shape):
    nd = len(shape)
    return pl.BlockSpec(tuple(shape), lambda *_: (0,) * nd, pipeline_mode=pl.Buffered(1))


def _acc(shape):
    nd = len(shape)
    return pl.BlockSpec(tuple(shape), lambda *_: (0,) * nd)


def _mm(a, b):
    return jnp.dot(a, b, preferred_element_type=F32)


def _mm_nt(a, b):
    return lax.dot_general(a, b, (((1,), (1,)), ((), ())), preferred_element_type=F32)


def _mm_tn(a, b):
    return lax.dot_general(a, b, (((0,), (0,)), ((), ())), preferred_element_type=F32)


def _mm_split(k01, s):
    hi = s.astype(BF16)
    lo = (s - hi.astype(F32)).astype(BF16)
    return _mm(k01, hi) + _mm(k01, lo)


def _sigmoid(v):
    return 0.5 * (jnp.tanh(0.5 * v) + 1.0)


def _silu(v):
    return v * _sigmoid(v)


_GELU_K = 0.7978845608028654
_GELU_C = 0.044715


def _gelu(v):
    return 0.5 * v * (1.0 + jnp.tanh(_GELU_K * (v + _GELU_C * v * v * v)))


def _gelu_grad(v):
    th = jnp.tanh(_GELU_K * (v + _GELU_C * v * v * v))
    return 0.5 * (1.0 + th) + 0.5 * v * (1.0 - th * th) * (_GELU_K * (1.0 + 3.0 * _GELU_C * v * v))


def _colsum(v):
    return jnp.sum(v, axis=0, keepdims=True)


def _rowmean(v):
    return jnp.mean(v, axis=-1, keepdims=True)


NCHIP = NDEV // 2


def _my_pos():
    return lax.axis_index("x"), lax.axis_index("y"), lax.axis_index("c")


def _other_chips():
    x, y, _ = _my_pos()
    return [(1 - x, y), (x, 1 - y), (1 - x, 1 - y)]


def _remote(src, dst, send, recv, to):
    return pltpu.make_async_remote_copy(src, dst, send, recv, device_id=to, device_id_type=MESH_ID)


def _all_gather(arrs, name):
    n = len(arrs)

    def body(*refs):
        ins, outs = refs[:n], refs[n : 2 * n]
        send, recv, loc = refs[2 * n :]
        x, y, c = _my_pos()
        me, sib = (x, y, c), (x, y, 1 - c)
        chips = _other_chips()

        def slot(a, p):
            return outs[a].at[4 * p[0] + 2 * p[1] + p[2]]

        def copy(a, k, block, to, own=False):
            return _remote(ins[a] if own else slot(a, block), slot(a, block), send.at[a, k], recv.at[a, k], to)

        local = [pltpu.make_async_copy(ins[a], slot(a, me), loc.at[a]) for a in range(n)]
        for cp in local:
            cp.start()
        sent = []
        for a in range(n):
            sent.append(copy(a, 0, me, sib, own=True))
            sent += [copy(a, 1 + j, me, (*chip, c), own=True) for j, chip in enumerate(chips)]
        for cp in sent:
            cp.start()
        for j, chip in enumerate(chips):
            for a in range(n):
                copy(a, 1 + j, (*chip, c), me).wait_recv()
                fwd = copy(a, 4 + j, (*chip, c), sib)
                fwd.start()
                sent.append(fwd)
        for a in range(n):
            copy(a, 0, sib, me).wait_recv()
            for j, chip in enumerate(chips):
                copy(a, 4 + j, (*chip, 1 - c), me).wait_recv()
        for cp in sent:
            cp.wait_send()
        for cp in local:
            cp.wait()

    return pl.pallas_call(
        body,
        name=name,
        out_shape=[_sds((NDEV,) + a.shape, a.dtype) for a in arrs],
        in_specs=[_HBM] * n,
        out_specs=[_HBM] * n,
        scratch_shapes=[pltpu.SemaphoreType.DMA((n, NDEV - 1)), pltpu.SemaphoreType.DMA((n, NDEV - 1)), pltpu.SemaphoreType.DMA((n,))],
    )(*arrs)


def _sibling_swap(arrs, name):
    n = len(arrs)

    def body(*refs):
        ins, outs = refs[:n], refs[n : 2 * n]
        send, recv = refs[2 * n :]
        x, y, c = _my_pos()
        cps = [_remote(ins[a].at[:, 1 - c], outs[a], send.at[a], recv.at[a], (x, y, 1 - c)) for a in range(n)]
        for cp in cps:
            cp.start()
        for cp in cps:
            cp.wait()

    return pl.pallas_call(
        body,
        name=name,
        out_shape=[_sds((a.shape[0],) + a.shape[2:], a.dtype) for a in arrs],
        in_specs=[_HBM] * n,
        out_specs=[_HBM] * n,
        scratch_shapes=[pltpu.SemaphoreType.DMA((n,)), pltpu.SemaphoreType.DMA((n,))],
    )(*arrs)


def _pair_sum(arr, got, wire, name):
    _, _, r, c = arr.shape
    tr = _row_tile(r, c, budget=1 << 20)

    def body(a_ref, g_ref, o_ref):
        o_ref[0] = (a_ref[0, lax.axis_index("c")] + g_ref[0]).astype(wire)

    return pl.pallas_call(
        body,
        name=name,
        grid=(NCHIP, r // tr),
        in_specs=[pl.BlockSpec((1, 2, tr, c), lambda q, i: (q, 0, i, 0)), pl.BlockSpec((1, tr, c), lambda q, i: (q, i, 0))],
        out_specs=pl.BlockSpec((1, tr, c), lambda q, i: (q, i, 0)),
        out_shape=_sds((NCHIP, r, c), wire),
        compiler_params=_cparams(2),
    )(arr, got)


def _chip_exchange(arrs, name):
    n = len(arrs)

    def body(*refs):
        ins, outs = refs[:n], refs[n : 2 * n]
        send, recv, loc = refs[2 * n :]
        x, y, c = _my_pos()
        mine = 2 * x + y
        chips = _other_chips()
        local = [pltpu.make_async_copy(ins[a].at[mine], outs[a].at[mine], loc.at[a]) for a in range(n)]
        for cp in local:
            cp.start()
        sent = []
        for a in range(n):
            for j, (px, py) in enumerate(chips):
                cp = _remote(ins[a].at[2 * px + py], outs[a].at[mine], send.at[a, j], recv.at[a, j], (px, py, c))
                cp.start()
                sent.append(cp)
        for a in range(n):
            for j, (px, py) in enumerate(chips):
                _remote(ins[a].at[mine], outs[a].at[2 * px + py], send.at[a, j], recv.at[a, j], (px, py, c)).wait_recv()
        for cp in sent:
            cp.wait_send()
        for cp in local:
            cp.wait()

    return pl.pallas_call(
        body,
        name=name,
        out_shape=[_sds(a.shape, a.dtype) for a in arrs],
        in_specs=[_HBM] * n,
        out_specs=[_HBM] * n,
        scratch_shapes=[pltpu.SemaphoreType.DMA((n, NCHIP - 1)), pltpu.SemaphoreType.DMA((n, NCHIP - 1)), pltpu.SemaphoreType.DMA((n,))],
    )(*arrs)


def _reduce_scatter(arrs, wires, tag):
    four = [a.reshape((NCHIP, 2) + a.shape[1:]) for a in arrs]
    got = _sibling_swap(four, "swap_" + tag)
    part = [_pair_sum(a, g, w, f"pair_sum_{tag}{i}") for i, (a, g, w) in enumerate(zip(four, got, wires))]
    return _chip_exchange(part, "exchange_" + tag)


def _row_tile(r, c, budget=1 << 20):
    best = r
    for t in range(8, r, 8):
        if r % t == 0 and t * c * 4 <= budget:
            best = t
    if r * c * 4 <= budget:
        best = r
    return best


def _sum_chips(land):
    tot = land[0].astype(F32)
    for q in range(1, NCHIP):
        tot = tot + land[q].astype(F32)
    return tot


def _adam_math(w, g, m, v):
    m2 = ADAM_B1 * m + (1.0 - ADAM_B1) * g
    v2 = ADAM_B2 * v + (1.0 - ADAM_B2) * (g * g)
    mh = m2 / (1.0 - ADAM_B1**ADAM_STEP)
    vh = v2 / (1.0 - ADAM_B2**ADAM_STEP)
    delta = -ADAM_LR * (mh / (jnp.sqrt(vh) + ADAM_EPS) + ADAM_WD * w)
    return delta, m2, v2


def _adam(g, w, m, v, name, land=None):
    r, c = w.shape
    tr = _row_tile(r, c, budget=1 << 20)

    def body(*refs):
        if land is not None:
            l_ref, w_ref, m_ref, v_ref, g_ref, d_ref, m2_ref, v2_ref = refs
            gv = _sum_chips(l_ref)
            g_ref[...] = gv
        else:
            g_in, w_ref, m_ref, v_ref, d_ref, m2_ref, v2_ref = refs
            gv = g_in[...]
        d, m2, v2 = _adam_math(w_ref[...], gv, m_ref[...], v_ref[...])
        d_ref[...] = d
        m2_ref[...] = m2
        v2_ref[...] = v2

    blk = _rows(tr, c)
    if land is not None:
        in_specs = [pl.BlockSpec((NCHIP, tr, c), lambda i: (0, i, 0)), blk, blk, blk]
        out = pl.pallas_call(
            body, name=name, grid=(r // tr,), in_specs=in_specs, out_specs=[blk] * 4, out_shape=[_sds((r, c))] * 4, compiler_params=_cparams(1)
        )(land, w, m, v)
        return out
    out = pl.pallas_call(
        body, name=name, grid=(r // tr,), in_specs=[blk] * 4, out_specs=[blk] * 3, out_shape=[_sds((r, c))] * 3, compiler_params=_cparams(1)
    )(g, w, m, v)
    return (g,) + tuple(out)


def _sum_slots(land, name):
    _, r, c = land.shape
    tr = _row_tile(r, c, budget=1 << 20)

    def body(l_ref, o_ref):
        o_ref[...] = _sum_chips(l_ref)

    return pl.pallas_call(
        body,
        name=name,
        grid=(r // tr,),
        in_specs=[pl.BlockSpec((NCHIP, tr, c), lambda i: (0, i, 0))],
        out_specs=_rows(tr, c),
        out_shape=_sds((r, c)),
        compiler_params=_cparams(1),
    )(land)


def _mod_fwd(call, ada_w, ada_b_loc):
    def body(c_ref, w_ref, b_ref, o_ref):
        s = _silu(c_ref[...]).astype(BF16)
        o_ref[...] = _mm(s, w_ref[...].astype(BF16)) + b_ref[...]

    return pl.pallas_call(
        body, name="mod_fwd", out_shape=_sds((call.shape[0], ada_w.shape[1])), compiler_params=_cparams(0, 32 << 20)
    )(call, ada_w, ada_b_loc)


def _mod_bwd(call, dm_loc, dmc_loc, ada_w):
    d, n = ada_w.shape
    pad = call.shape[0] - NDEV - 1

    def body(c_ref, dm_ref, dmc_ref, w_ref, gw_ref, gb_ref, cp_ref):
        cv = c_ref[...]
        sg = _sigmoid(cv)
        dmc = _colsum(dmc_ref[...])
        dm = dm_ref[...]
        rows = jnp.concatenate([dm, dmc, jnp.zeros((pad, n), F32)], axis=0)
        gw_ref[...] = _mm_tn((cv * sg).astype(BF16), rows.astype(BF16))
        gb_ref[...] = _colsum(dm) + dmc
        back = _mm_nt(rows[NDEV:].astype(BF16), w_ref[...].astype(BF16))
        c8, s8 = cv[NDEV : NDEV + 1], sg[NDEV : NDEV + 1]
        cp_ref[...] = back[0:1] * (s8 * (1.0 + c8 * (1.0 - s8)))

    return pl.pallas_call(
        body, name="mod_bwd", out_shape=[_sds((d, n)), _sds((1, n)), _sds((1, d))], compiler_params=_cparams(0, 40 << 20)
    )(call, dm_loc, dmc_loc, ada_w)


def _in_proj(xr, norm_g, scale, shift, wg, jsel, tm, name):
    lx, d = xr.shape
    cb = wg.shape[2]

    def body(x_ref, g_ref, sc_ref, sh_ref, w_ref, p_ref, h_ref):
        xv = x_ref[...]
        r = lax.rsqrt(_rowmean(xv * xv) + EPS)
        hb = ((xv * r) * g_ref[...] * (1.0 + sc_ref[...]) + sh_ref[...]).astype(BF16)
        h_ref[...] = hb
        for q, j in enumerate(jsel):
            p_ref[:, q * cb : (q + 1) * cb] = _mm(hb, w_ref[j])

    vec = _whole((1, d))
    return pl.pallas_call(
        body,
        name=name,
        grid=(lx // tm,),
        in_specs=[_rows(tm, d), vec, vec, vec, _whole(wg.shape)],
        out_specs=[_rows(tm, len(jsel) * cb), _rows(tm, d)],
        out_shape=[_sds((lx, len(jsel) * cb)), _sds((lx, d), BF16)],
        compiler_params=_cparams(1, VMEM_BIG),
    )(xr, norm_g, scale, shift, wg)


def _in_bwd(xr, dxo, dus, du5, dypre, dz, wg, norm_g, scale, shift, d_skip, dg_init, tm):
    lx, d = xr.shape
    cb = wg.shape[2]
    pgw = dus[0].shape[1]
    sw = du5.shape[1]
    mix = dz.shape[1]
    ncol = NDEV * cb

    def body(x_ref, dxo_ref, u0, u1, u2, u3, d5_ref, dy_ref, dz_ref, w_ref, g_ref, sc_ref, sh_ref, dk_ref, gi_ref,
             gx_ref, dp_ref, dsc_ref, dsh_ref, dg_ref):
        i = pl.program_id(0)

        @pl.when(i == 0)
        def _():
            dsc_ref[...] = jnp.zeros_like(dsc_ref)
            dsh_ref[...] = jnp.zeros_like(dsh_ref)
            dg_ref[...] = gi_ref[...]

        dp = jnp.concatenate(
            [u0[...], u1[...], u2[...], u3[...], d5_ref[...] + dk_ref[...] * dy_ref[...], dz_ref[...]], axis=1
        ).astype(BF16)
        dp_ref[...] = dp
        dh = _mm_nt(dp[:, 0:cb], w_ref[0])
        for j in range(1, NDEV):
            dh = dh + _mm_nt(dp[:, j * cb : (j + 1) * cb], w_ref[j])
        xv = x_ref[...]
        r = lax.rsqrt(_rowmean(xv * xv) + EPS)
        xh = xv * r
        g = g_ref[...]
        one_sc = 1.0 + sc_ref[...]
        dsh_ref[...] += _colsum(dh)
        dsc_ref[...] += _colsum(dh * (xh * g))
        dg_ref[...] += _colsum(dh * one_sc * xh)
        dxh = dh * one_sc * g
        gx_ref[...] = r * (dxh - xh * _rowmean(dxh * xh)) + dxo_ref[...]

    vec = _whole((1, d))
    return pl.pallas_call(
        body,
        name="in_bwd",
        grid=(lx // tm,),
        in_specs=[_rows(tm, d), _rows(tm, d)] + [_rows(tm, pgw)] * 4 + [_rows(tm, sw), _rows(tm, sw), _rows(tm, mix), _whole(wg.shape), vec, vec, vec, _whole((1, sw)), vec],
        out_specs=[_rows(tm, d), _rows(tm, ncol), _acc((1, d)), _acc((1, d)), _acc((1, d))],
        out_shape=[_sds((lx, d)), _sds((lx, ncol), BF16), _sds((1, d)), _sds((1, d)), _sds((1, d))],
        compiler_params=_cparams(1, VMEM_BIG),
    )(xr, dxo, *dus, du5, dypre, dz, wg, norm_g, scale, shift, d_skip, dg_init)


def _in_bwd_ctx(xc, duc, wg, jsel, norm_g, scale, shift):
    lc, d = xc.shape
    cb = wg.shape[2]

    def body(x_ref, du_ref, w_ref, g_ref, sc_ref, sh_ref, dp_ref, dsc_ref, dsh_ref, dg_ref):
        dp = du_ref[...].astype(BF16)
        dp_ref[...] = dp
        dh = _mm_nt(dp[:, 0:cb], w_ref[jsel[0]])
        for q in range(1, len(jsel)):
            dh = dh + _mm_nt(dp[:, q * cb : (q + 1) * cb], w_ref[jsel[q]])
        xv = x_ref[...]
        xh = xv * lax.rsqrt(_rowmean(xv * xv) + EPS)
        dsh_ref[...] = _colsum(dh)
        dsc_ref[...] = _colsum(dh * (xh * g_ref[...]))
        dg_ref[...] = _colsum(dh * (1.0 + sc_ref[...]) * xh)

    return pl.pallas_call(
        body,
        name="in_bwd_ctx",
        out_shape=[_sds(duc.shape, BF16), _sds((1, d)), _sds((1, d)), _sds((1, d))],
        compiler_params=_cparams(0, VMEM_BIG),
    )(xc, duc, wg, norm_g, scale, shift)


def _in_w_grad(hb, dpb, hcb, dpcb, cb, tm):
    lx, d = hb.shape
    lc = hcb.shape[0]
    nb = lx // tm
    cg = 2 * cb

    def body(h_ref, dp_ref, hc_ref, dpc_ref, o_ref):
        j, i = pl.program_id(0), pl.program_id(1)

        @pl.when(i == 0)
        def _():
            o_ref[...] = jnp.zeros_like(o_ref)

        @pl.when(i < nb)
        def _():
            h = h_ref[...]
            o_ref[0] += _mm_tn(h, dp_ref[:, 0:cb])
            o_ref[1] += _mm_tn(h, dp_ref[:, cb:cg])

        @pl.when(jnp.logical_and(i == nb, j == 1))
        def _():
            h = hc_ref[...]
            o_ref[0] += _mm_tn(h, dpc_ref[:, 0:cb])
            o_ref[1] += _mm_tn(h, dpc_ref[:, cb:cg])

    return pl.pallas_call(
        body,
        name="in_w_grad",
        grid=(NDEV // 2, nb + 1),
        in_specs=[
            pl.BlockSpec((tm, d), lambda j, i: (jnp.minimum(i, nb - 1), 0)),
            pl.BlockSpec((tm, cg), lambda j, i: (jnp.minimum(i, nb - 1), j)),
            pl.BlockSpec((lc, d), lambda j, i: (0, 0)),
            pl.BlockSpec((lc, cg), lambda j, i: (0, 0)),
        ],
        out_specs=pl.BlockSpec((2, d, cb), lambda j, i: (j, 0, 0)),
        out_shape=_sds((NDEV, d, cb)),
        compiler_params=_cparams(2, VMEM_BIG),
    )(hb, dpb, hcb, dpcb)


def _pool_tables(w, rows, rb, pgw, transpose):
    t = np.arange(GRID_W)
    lo, hi = np.clip(t - w // 2, 0, GRID_W), np.clip(t + w - w // 2, 0, GRID_W)
    band = ((t[None, :] >= lo[:, None]) & (t[None, :] < hi[:, None])).astype(np.float32)
    if transpose:
        band = band.T
    kc = np.kron(np.eye(rb, dtype=np.float32), band)
    inv_c = np.tile((1.0 / (hi - lo).astype(np.float32))[:, None], (rb, pgw)).astype(np.float32)
    r = np.arange(rows)
    cnt_r = np.clip(r + w - w // 2, 0, rows) - np.clip(r - w // 2, 0, rows)
    inv_r = (1.0 / cnt_r.astype(np.float32)).astype(np.float32)
    return jnp.asarray(kc, BF16), jnp.asarray(inv_r), jnp.asarray(inv_c)


def _pool_fwd(proj, col0, w, pw_g, name):
    lx = proj.shape[0]
    pgw = pw_g.shape[0]
    rows = lx // GRID_W
    rb = min(4, rows)
    tok = rb * GRID_W
    lo = w // 2
    kc, inv_r, inv_c = _pool_tables(w, rows, rb, pgw, False)

    def body(p_hbm, w_ref, kc_ref, ir_ref, ic_ref, lin_ref, dm_ref, xp, sem):
        xp[pl.ds(0, lo * GRID_W), :] = jnp.zeros((lo * GRID_W, pgw), F32)
        xp[pl.ds((lo + rows) * GRID_W, (w - lo) * GRID_W), :] = jnp.zeros(((w - lo) * GRID_W, pgw), F32)
        cp = pltpu.make_async_copy(p_hbm.at[:, pl.ds(col0, pgw)], xp.at[pl.ds(lo * GRID_W, lx), :], sem)
        cp.start()
        cp.wait()

        def blk(b, carry):
            r0 = b * rb
            parts = []
            for rr in range(rb):
                base = pl.multiple_of((r0 + rr) * GRID_W, GRID_W)
                s = xp[pl.ds(base, GRID_W), :]
                for k in range(1, w):
                    s = s + xp[pl.ds(base + k * GRID_W, GRID_W), :]
                parts.append(s * ir_ref[r0 + rr])
            m = _mm_split(kc_ref[...], jnp.concatenate(parts, axis=0)) * ic_ref[...]
            u = xp[pl.ds(pl.multiple_of((r0 + lo) * GRID_W, GRID_W), tok), :]
            db = (m - u).astype(BF16)
            o0 = pl.multiple_of(r0 * GRID_W, GRID_W)
            dm_ref[pl.ds(o0, tok), :] = db
            lin_ref[pl.ds(o0, tok), :] = _mm(db, w_ref[...])
            return carry

        lax.fori_loop(0, rows // rb, blk, 0)

    return pl.pallas_call(
        body,
        name=name,
        in_specs=[_ANY, pl.BlockSpec(memory_space=pltpu.VMEM), pl.BlockSpec(memory_space=pltpu.VMEM), _SMEM, pl.BlockSpec(memory_space=pltpu.VMEM)],
        out_shape=[_sds((lx, pgw)), _sds((lx, pgw), BF16)],
        scratch_shapes=[pltpu.VMEM(((rows + w) * GRID_W, pgw), F32), pltpu.SemaphoreType.DMA],
        compiler_params=_cparams(0, VMEM_BIG),
    )(proj, pw_g, kc, inv_r, inv_c)


def _pool_bwd(dlin, col0, w, pw_g, dmat_g, name):
    lx = dlin.shape[0]
    pgw = pw_g.shape[0]
    rows = lx // GRID_W
    rb = min(4, rows)
    tok = rb * GRID_W
    front = w - w // 2 - 1
    kct, inv_r, inv_c = _pool_tables(w, rows, rb, pgw, True)

    def body(dl_hbm, w_ref, kc_ref, ir_ref, ic_ref, dm_ref, du_ref, dw_ref, tp, dl, sem):
        if front:
            tp[pl.ds(0, front * GRID_W), :] = jnp.zeros((front * GRID_W, pgw), F32)
        tp[pl.ds((front + rows) * GRID_W, (w - front) * GRID_W), :] = jnp.zeros(((w - front) * GRID_W, pgw), F32)
        cp = pltpu.make_async_copy(dl_hbm.at[:, pl.ds(col0, pgw)], dl, sem)
        cp.start()
        cp.wait()
        dw_ref[...] = jnp.zeros_like(dw_ref)

        def blk(b, carry):
            o0 = pl.multiple_of(b * tok, GRID_W)
            dlb = dl[pl.ds(o0, tok), :].astype(BF16)
            dw_ref[...] += _mm_tn(dm_ref[pl.ds(o0, tok), :], dlb)
            dd = _mm_nt(dlb, w_ref[...])
            du_ref[pl.ds(o0, tok), :] = -dd
            t = _mm_split(kc_ref[...], dd * ic_ref[...])
            for rr in range(rb):
                dst = pl.multiple_of((b * rb + rr + front) * GRID_W, GRID_W)
                tp[pl.ds(dst, GRID_W), :] = t[rr * GRID_W : (rr + 1) * GRID_W] * ir_ref[b * rb + rr]
            return carry

        lax.fori_loop(0, rows // rb, blk, 0)

        def rowl(r, carry):
            base = pl.multiple_of(r * GRID_W, GRID_W)
            s = tp[pl.ds(base, GRID_W), :]
            for k in range(1, w):
                s = s + tp[pl.ds(base + k * GRID_W, GRID_W), :]
            du_ref[pl.ds(base, GRID_W), :] += s
            return carry

        lax.fori_loop(0, rows, rowl, 0)

    vm = pl.BlockSpec(memory_space=pltpu.VMEM)
    return pl.pallas_call(
        body,
        name=name,
        in_specs=[_ANY, vm, vm, _SMEM, vm, vm],
        out_shape=[_sds((lx, pgw)), _sds((pgw, pgw))],
        scratch_shapes=[pltpu.VMEM(((rows + w) * GRID_W, pgw), F32), pltpu.VMEM((lx, pgw), F32), pltpu.SemaphoreType.DMA],
        compiler_params=_cparams(0, VMEM_BIG),
    )(dlin, pw_g, kct, inv_r, inv_c, dmat_g)


def _s5_small(lam_re, lam_im, log_dt, b_re, b_im, c_re, c_im):
    t = CHUNK_T
    dt = jnp.exp(log_dt)[..., None]
    zr, zi = lam_re * dt, lam_im * dt
    tau = jnp.arange(t + 1, dtype=F32)
    mag = jnp.exp(zr[..., None] * tau)
    pr, pi = mag * jnp.cos(zi[..., None] * tau), mag * jnp.sin(zi[..., None] * tau)
    ar, ai = pr[..., 1], pi[..., 1]
    den = lam_re * lam_re + lam_im * lam_im
    qr = ((ar - 1.0) * lam_re + ai * lam_im) / den
    qi = (ai * lam_re - (ar - 1.0) * lam_im) / den
    bbr = qr[..., None] * b_re - qi[..., None] * b_im
    bbi = qr[..., None] * b_im + qi[..., None] * b_re
    ctr, cti = jnp.swapaxes(c_re, 2, 3), jnp.swapaxes(c_im, 2, 3)
    lane_pad = lambda v: jnp.pad(jnp.swapaxes(v, 0, 1), ((0, 0), (0, 0), (0, 0), (0, LANES - v.shape[-1])))
    tp = lambda v: jnp.swapaxes(v, 1, 2)
    flip = lambda v: v[..., ::-1]
    pfr, pfi = tp(flip(pr[0, ..., :t])), tp(flip(pi[0, ..., :t]))
    pbr, pbi = tp(pr[1, ..., :t]), tp(pi[1, ..., :t])
    bfr, bfi, bbr_t, bbi_t = tp(bbr[0]), tp(bbi[0]), tp(bbr[1]), tp(bbi[1])
    a1 = jnp.concatenate([pfr, pbr, pfr, pbr], axis=-1)
    a2 = jnp.concatenate([-pfi, -pbi, pfi, pbi], axis=-1)
    b1 = jnp.concatenate([bfr, bbr_t, bfi, bbi_t], axis=-1)
    b2 = jnp.concatenate([bfi, bbi_t, bfr, bbr_t], axis=-1)
    a16 = jnp.concatenate([pr[0, ..., t], pr[1, ..., t], pi[0, ..., t], pi[1, ..., t]], axis=1)
    return (lane_pad(pr), lane_pad(pi), lane_pad(bbr), lane_pad(bbi), lane_pad(ctr), lane_pad(cti), a1, a2, b1, b2), a16


def _split2(v):
    hi = v.astype(BF16)
    return hi, (v - hi.astype(F32)).astype(BF16)


def _dotx(a, b, mm=_mm):
    ah, al = (a, None) if a.dtype == BF16 else _split2(a)
    bh, bl = (b, None) if b.dtype == BF16 else _split2(b)
    out = mm(ah, bh)
    if bl is not None:
        out = out + mm(ah, bl)
    if al is not None:
        out = out + mm(al, bh)
    return out


def _shift_lanes(v, k):
    if k == 0:
        return v
    n = v.shape[-1]
    lane = lax.broadcasted_iota(jnp.int32, v.shape, v.ndim - 1)
    r = pltpu.roll(v, k % n, axis=v.ndim - 1)
    return jnp.where(lane >= k, r, 0.0) if k > 0 else jnp.where(lane < n + k, r, 0.0)


def _op_consts():
    t, h = CHUNK_T, SSM_H
    j, o = np.divmod(np.arange(t * h), h)
    p = np.arange(LANES)[:, None]
    sel = lambda col: (p == col[None, :]).astype(np.float32)
    pw = np.stack([sel(j), sel(t - 1 - j), sel(j + 1), sel(t - j)])
    return jnp.asarray(pw, BF16), jnp.asarray(sel(o), BF16), jnp.asarray(sel(j).T, BF16), jnp.asarray(sel(o).T, BF16)


def _op_pieces(d, pr_ref, pi_ref, cr_ref, ci_ref, pw_ref, ch_ref, gi):
    p_r, p_i = pr_ref[gi, d], pi_ref[gi, d]
    e16 = _dotx(p_r, pw_ref[d]), _dotx(p_i, pw_ref[d])
    e1 = _dotx(p_r, pw_ref[2 + d]), _dotx(p_i, pw_ref[2 + d])
    c16 = _dotx(cr_ref[gi, d], ch_ref[...]), _dotx(ci_ref[gi, d], ch_ref[...])
    return e16, e1, c16


def _cmul(x, y):
    return x[0] * y[0] - x[1] * y[1], x[0] * y[1] + x[1] * y[0]


def _cmul_conj(g, y):
    return g[0] * y[0] + g[1] * y[1], g[1] * y[0] - g[0] * y[1]


def _rows_pad(v):
    return jnp.concatenate([v, jnp.zeros((LANES - v.shape[0], v.shape[1]), F32)], axis=0)


def _s5_build(small, gb):
    pr, pi, br, bi, cr, ci, a1, a2, b1, b2 = small
    g, _, n, _ = pr.shape
    t, h = CHUNK_T, SSM_H
    pw, ch, rep_s, til_s = _op_consts()

    def body(pr_ref, pi_ref, br_ref, bi_ref, cr_ref, ci_ref, a1_ref, a2_ref, b1_ref, b2_ref, pw_ref, ch_ref, rs_ref, ts_ref, t_ref, w_ref, c_ref):
        def group(gi, carry):
            rows, outs = [], []
            for d in range(2):
                e16, e1, c16 = _op_pieces(d, pr_ref, pi_ref, cr_ref, ci_ref, pw_ref, ch_ref, gi)
                ca = _cmul(c16, e16)
                rows.append((_dotx(br_ref[gi, d], ca[0], _mm_tn) - _dotx(bi_ref[gi, d], ca[1], _mm_tn))[:h])
                outs.append(_cmul(c16, e1))
            c_ref[gi] = jnp.concatenate([outs[0][0], outs[1][0], -outs[0][1], -outs[1][1]], axis=0).astype(BF16)
            for s in range(t):
                t_ref[gi, s * h : (s + 1) * h, :] = (_shift_lanes(rows[0], h * s) + _shift_lanes(rows[1], -h * (t - 1 - s))).astype(BF16)
            spread = lambda m_ref, v_ref: _dotx(m_ref[...], _rows_pad(v_ref[gi]))
            w_ref[gi] = (spread(rs_ref, a1_ref) * spread(ts_ref, b1_ref) + spread(rs_ref, a2_ref) * spread(ts_ref, b2_ref)).astype(BF16)
            return carry

        lax.fori_loop(0, gb, group, 0)

    sm = pl.BlockSpec((gb, 2, n, LANES), lambda i: (i, 0, 0, 0))
    ab = lambda v: pl.BlockSpec((gb,) + v.shape[1:], lambda i: (i, 0, 0))
    return pl.pallas_call(
        body,
        name="s5_build",
        grid=(g // gb,),
        in_specs=[sm] * 6 + [ab(a1), ab(a2), ab(b1), ab(b2), _whole(pw.shape), _whole(ch.shape), _whole(rep_s.shape), _whole(til_s.shape)],
        out_specs=[_gspec(gb, CW, CW), _gspec(gb, CW, 4 * n), _gspec(gb, 4 * n, CW)],
        out_shape=[_sds((g, CW, CW), BF16), _sds((g, CW, 4 * n), BF16), _sds((g, 4 * n, CW), BF16)],
        compiler_params=_cparams(1, VMEM_BIG),
    )(pr, pi, br, bi, cr, ci, a1, a2, b1, b2, pw, ch, rep_s, til_s)


def _s5_build_bwd(small, dtsum, dwst, dccat, gb):
    pr, pi, br, bi, cr, ci, a1, a2, b1, b2 = small
    g, _, n, _ = pr.shape
    t, h = CHUNK_T, SSM_H
    pw, ch, rep_s, til_s = _op_consts()

    def body(pr_ref, pi_ref, br_ref, bi_ref, cr_ref, ci_ref, a1_ref, a2_ref, b1_ref, b2_ref, pw_ref, ch_ref, rs_ref, ts_ref, dt_ref, dw_ref, dc_ref,
             dpr_ref, dpi_ref, dbr_ref, dbi_ref, dcr_ref, dci_ref, da1_ref, da2_ref, db1_ref, db2_ref):
        def group(gi, carry):
            drow = [jnp.zeros((h, CW), F32), jnp.zeros((h, CW), F32)]
            for s in range(t):
                blk = dt_ref[gi, s * h : (s + 1) * h, :]
                drow[0] = drow[0] + _shift_lanes(blk, -h * s)
                drow[1] = drow[1] + _shift_lanes(blk, h * (t - 1 - s))
            dcc = dc_ref[gi]
            for d in range(2):
                e16, e1, c16 = _op_pieces(d, pr_ref, pi_ref, cr_ref, ci_ref, pw_ref, ch_ref, gi)
                ca = _cmul(c16, e16)
                dr = _rows_pad(drow[d])
                dbr_ref[gi, d] = _dotx(ca[0], dr, _mm_nt)
                dbi_ref[gi, d] = -_dotx(ca[1], dr, _mm_nt)
                dca = _dotx(br_ref[gi, d], dr), -_dotx(bi_ref[gi, d], dr)
                gx = dcc[d * n : (d + 1) * n], -dcc[(2 + d) * n : (3 + d) * n]
                dc16 = _cmul_conj(dca, e16)
                dc16b = _cmul_conj(gx, e1)
                de16 = _cmul_conj(dca, c16)
                de1 = _cmul_conj(gx, c16)
                dcr_ref[gi, d] = _dotx(dc16[0] + dc16b[0], ch_ref[...], _mm_nt)
                dci_ref[gi, d] = _dotx(dc16[1] + dc16b[1], ch_ref[...], _mm_nt)
                dpr_ref[gi, d] = _dotx(de16[0], pw_ref[d], _mm_nt) + _dotx(de1[0], pw_ref[2 + d], _mm_nt)
                dpi_ref[gi, d] = _dotx(de16[1], pw_ref[d], _mm_nt) + _dotx(de1[1], pw_ref[2 + d], _mm_nt)
            dw = dw_ref[gi]
            spread = lambda m_ref, v_ref: _dotx(m_ref[...], _rows_pad(v_ref[gi]))
            gather = lambda m_ref, v: _dotx(m_ref[...], v, _mm_tn)
            da1_ref[gi] = gather(rs_ref, dw * spread(ts_ref, b1_ref))[:t]
            da2_ref[gi] = gather(rs_ref, dw * spread(ts_ref, b2_ref))[:t]
            db1_ref[gi] = gather(ts_ref, dw * spread(rs_ref, a1_ref))[:h]
            db2_ref[gi] = gather(ts_ref, dw * spread(rs_ref, a2_ref))[:h]
            return carry

        lax.fori_loop(0, gb, group, 0)

    sm = pl.BlockSpec((gb, 2, n, LANES), lambda i: (i, 0, 0, 0))
    ab = lambda v: pl.BlockSpec((gb,) + v.shape[1:], lambda i: (i, 0, 0))
    ops = [_gspec(gb, CW, CW), _gspec(gb, CW, 4 * n), _gspec(gb, 4 * n, CW)]
    return pl.pallas_call(
        body,
        name="s5_build_bwd",
        grid=(g // gb,),
        in_specs=[sm] * 6 + [ab(a1), ab(a2), ab(b1), ab(b2), _whole(pw.shape), _whole(ch.shape), _whole(rep_s.shape), _whole(til_s.shape)] + ops,
        out_specs=[sm] * 6 + [ab(a1), ab(a2), ab(b1), ab(b2)],
        out_shape=[_sds(v.shape) for v in small],
        compiler_params=_cparams(1, VMEM_BIG),
    )(pr, pi, br, bi, cr, ci, a1, a2, b1, b2, pw, ch, rep_s, til_s, dtsum, dwst, dccat)


SCAN_UNROLL = 4
GPL = LANES // SSM_H
CW = CHUNK_T * SSM_H
BW = CHUNK_T * LANES


def _chunk_perm():
    o = np.arange(BW)
    src = ((o % CW) // SSM_H) * LANES + (o // CW) * SSM_H + o % SSM_H
    return jnp.asarray(np.arange(BW)[:, None] == src[None, :], BF16)


def _gspec(gb, a, b):
    return pl.BlockSpec((gb, a, b), lambda i: (i, 0, 0))


def _chunk_rows(ref, n):
    return jnp.concatenate([ref[pl.ds(s, n, stride=CHUNK_T), :] for s in range(CHUNK_T)], axis=1)


def _s5_state(proj, col0, uc, perm, wst, ncc, ncl):
    nlb = uc.shape[1] // LANES
    g, _, ns = wst.shape
    nch = ncc + ncl
    half = ns // 2

    def body(p_ref, c_ref, pm_ref, w_ref, u_ref, sr_ref, si_ref):
        xrows = jnp.concatenate([_chunk_rows(c_ref, ncc), _chunk_rows(p_ref, ncl)], axis=0).astype(BF16)
        u = _mm(xrows, pm_ref[...]).astype(BF16)
        u_ref[0] = u
        for gi in range(GPL):
            s = _mm(u[:, gi * CW : (gi + 1) * CW], w_ref[gi].astype(BF16))
            sr_ref[gi] = s[:, :half]
            si_ref[gi] = s[:, half:]

    return pl.pallas_call(
        body,
        name="s5_state",
        grid=(nlb,),
        in_specs=[
            pl.BlockSpec((ncl * CHUNK_T, LANES), lambda i: (0, col0 + i)),
            pl.BlockSpec((ncc * CHUNK_T, LANES), lambda i: (0, i)),
            _whole(perm.shape),
            _gspec(GPL, CW, ns),
        ],
        out_specs=[pl.BlockSpec((1, nch, BW), lambda i: (i, 0, 0)), _gspec(GPL, nch, half), _gspec(GPL, nch, half)],
        out_shape=[_sds((nlb, nch, BW), BF16), _sds((g, nch, half)), _sds((g, nch, half))],
        compiler_params=_cparams(1, VMEM_BIG),
    )(proj, uc, perm, wst)


def _unrolled_loop(n, step, init):
    u = SCAN_UNROLL if n % SCAN_UNROLL == 0 else 1

    def trip(i, c):
        for k in range(u):
            c = step(i * u + k, c)
        return c

    return lax.fori_loop(0, n // u, trip, init)


def _lane_masks(gb, half):
    lane = lax.broadcasted_iota(jnp.int32, (gb, half), 1)
    return lane < (half // 2)


def _s5_scan_fwd(s_re, s_im, a16, ncc, nch, gb):
    g, ns = a16.shape
    half = ns // 2

    def body(sr_ref, si_ref, a_ref, hr_ref, hi_ref):
        ar, ai = a_ref[:, :half], a_ref[:, half:]
        mf = _lane_masks(gb, half)
        zero = jnp.zeros((gb, half), F32)
        hr_ref[...] = jnp.zeros_like(hr_ref)
        hi_ref[...] = jnp.zeros_like(hi_ref)

        def step(t, c):
            fr, fi, br, bi = c
            rf = pl.ds(t, gb, stride=nch)
            rb = pl.ds(jnp.where(t < ncc, ncc - 1 - t, nch - 1 - (t - ncc)), gb, stride=nch)
            hr_ref[rf, :] += jnp.where(mf, fr, 0.0)
            hi_ref[rf, :] += jnp.where(mf, fi, 0.0)
            hr_ref[rb, :] += jnp.where(mf, 0.0, br)
            hi_ref[rb, :] += jnp.where(mf, 0.0, bi)
            return (
                ar * fr - ai * fi + sr_ref[rf, :],
                ar * fi + ai * fr + si_ref[rf, :],
                ar * br - ai * bi + sr_ref[rb, :],
                ar * bi + ai * br + si_ref[rb, :],
            )

        _unrolled_loop(nch, step, (zero, zero, zero, zero))

    blk = pl.BlockSpec((gb * nch, half), lambda i: (i, 0))
    return pl.pallas_call(
        body,
        name="s5_scan_fwd",
        grid=(g // gb,),
        in_specs=[blk, blk, pl.BlockSpec((gb, ns), lambda i: (i, 0))],
        out_specs=[blk, blk],
        out_shape=[_sds((g * nch, half))] * 2,
        compiler_params=_cparams(1, VMEM_BIG),
    )(s_re, s_im, a16)


def _token_rows_store(ref, val, n):
    for s in range(CHUNK_T):
        ref[pl.ds(s, n, stride=CHUNK_T), :] = val[:, s * LANES : (s + 1) * LANES]


def _s5_out(u_all, h_re, h_im, tsum, ccat, perm, ncc, lx):
    nlb, nch, _ = u_all.shape
    g, ns, _ = ccat.shape
    ncl = nch - ncc
    half = ns // 2

    def body(u_ref, hr_ref, hi_ref, t_ref, c_ref, pm_ref, y_ref):
        parts = []
        for gi in range(GPL):
            u = u_ref[0, ncc:, gi * CW : (gi + 1) * CW]
            hs = jnp.concatenate([hr_ref[gi, ncc:, :], hi_ref[gi, ncc:, :]], axis=1).astype(BF16)
            parts.append(_mm(u, t_ref[gi].astype(BF16)) + _mm(hs, c_ref[gi].astype(BF16)))
        y = jnp.concatenate(parts, axis=1)
        hi = y.astype(BF16)
        lo = (y - hi.astype(F32)).astype(BF16)
        _token_rows_store(y_ref, _mm_nt(hi, pm_ref[...]) + _mm_nt(lo, pm_ref[...]), ncl)

    return pl.pallas_call(
        body,
        name="s5_out",
        grid=(nlb,),
        in_specs=[pl.BlockSpec((1, nch, BW), lambda i: (i, 0, 0)), _gspec(GPL, nch, half), _gspec(GPL, nch, half), _gspec(GPL, CW, CW), _gspec(GPL, ns, CW), _whole(perm.shape)],
        out_specs=pl.BlockSpec((lx, LANES), lambda i: (0, i)),
        out_shape=_sds((lx, nlb * LANES)),
        compiler_params=_cparams(1, VMEM_BIG),
    )(u_all, h_re, h_im, tsum, ccat, perm)


def _s5_dstate(dypre, perm, ccat, ncc):
    lx, sw = dypre.shape
    nlb = sw // LANES
    g, ns, _ = ccat.shape
    ncl = lx // CHUNK_T
    nch = ncl + ncc
    half = ns // 2

    def body(dy_ref, pm_ref, c_ref, dyr_ref, dr_ref, di_ref):
        dy = _mm(_chunk_rows(dy_ref, ncl).astype(BF16), pm_ref[...]).astype(BF16)
        dyr_ref[0] = dy
        for gi in range(GPL):
            dh = _mm_nt(dy[:, gi * CW : (gi + 1) * CW], c_ref[gi].astype(BF16))
            dr_ref[gi, :ncc, :] = jnp.zeros((ncc, half), F32)
            di_ref[gi, :ncc, :] = jnp.zeros((ncc, half), F32)
            dr_ref[gi, ncc:, :] = dh[:, :half]
            di_ref[gi, ncc:, :] = dh[:, half:]

    return pl.pallas_call(
        body,
        name="s5_dstate",
        grid=(nlb,),
        in_specs=[pl.BlockSpec((lx, LANES), lambda i: (0, i)), _whole(perm.shape), _gspec(GPL, ns, CW)],
        out_specs=[pl.BlockSpec((1, ncl, BW), lambda i: (i, 0, 0)), _gspec(GPL, nch, half), _gspec(GPL, nch, half)],
        out_shape=[_sds((nlb, ncl, BW), BF16), _sds((g, nch, half)), _sds((g, nch, half))],
        compiler_params=_cparams(1, VMEM_BIG),
    )(dypre, perm, ccat)


def _s5_scan_bwd(dh_re, dh_im, a16, ncc, nch, gb):
    g, ns = a16.shape
    half = ns // 2
    ncl = nch - ncc

    def body(dr_ref, di_ref, a_ref, sr_ref, si_ref):
        ar, ai = a_ref[:, :half], a_ref[:, half:]
        mf = _lane_masks(gb, half)
        zero = jnp.zeros((gb, half), F32)
        sr_ref[...] = jnp.zeros_like(sr_ref)
        si_ref[...] = jnp.zeros_like(si_ref)

        def step(t, c):
            lfr, lfi, lbr, lbi = c
            pf = pl.ds(nch - 1 - t, gb, stride=nch)
            pb = pl.ds(jnp.where(t < ncl, ncc + t, t - ncl), gb, stride=nch)
            sr_ref[pf, :] += jnp.where(mf, lfr, 0.0)
            si_ref[pf, :] += jnp.where(mf, lfi, 0.0)
            sr_ref[pb, :] += jnp.where(mf, 0.0, lbr)
            si_ref[pb, :] += jnp.where(mf, 0.0, lbi)
            return (
                dr_ref[pf, :] + ar * lfr + ai * lfi,
                di_ref[pf, :] + ar * lfi - ai * lfr,
                dr_ref[pb, :] + ar * lbr + ai * lbi,
                di_ref[pb, :] + ar * lbi - ai * lbr,
            )

        _unrolled_loop(nch, step, (zero,) * 4)

    blk = pl.BlockSpec((gb * nch, half), lambda i: (i, 0))
    return pl.pallas_call(
        body,
        name="s5_scan_bwd",
        grid=(g // gb,),
        in_specs=[blk, blk, pl.BlockSpec((gb, ns), lambda i: (i, 0))],
        out_specs=[blk, blk],
        out_shape=[_sds((g * nch, half))] * 2,
        compiler_params=_cparams(1, VMEM_BIG),
    )(dh_re, dh_im, a16)


def _s5_du(dy_all, ds_re, ds_im, tsum, wst, perm, ncc, lx, lc):
    nlb, ncl, _ = dy_all.shape
    g, _, ns = wst.shape
    nch = ncc + ncl
    half = ns // 2

    def body(dy_ref, sr_ref, si_ref, t_ref, w_ref, pm_ref, du_ref, dc_ref):
        parts = []
        for gi in range(GPL):
            ds = jnp.concatenate([sr_ref[gi], si_ref[gi]], axis=1).astype(BF16)
            d_all = _mm_nt(ds, w_ref[gi].astype(BF16))
            d_lat = d_all[ncc:] + _mm_nt(dy_ref[0, :, gi * CW : (gi + 1) * CW], t_ref[gi].astype(BF16))
            parts.append(jnp.concatenate([d_all[:ncc], d_lat], axis=0))
        du = _mm_nt(jnp.concatenate(parts, axis=1).astype(BF16), pm_ref[...])
        _token_rows_store(dc_ref, du[:ncc], ncc)
        _token_rows_store(du_ref, du[ncc:], ncl)

    return pl.pallas_call(
        body,
        name="s5_du",
        grid=(nlb,),
        in_specs=[pl.BlockSpec((1, ncl, BW), lambda i: (i, 0, 0)), _gspec(GPL, nch, half), _gspec(GPL, nch, half), _gspec(GPL, CW, CW), _gspec(GPL, CW, ns), _whole(perm.shape)],
        out_specs=[pl.BlockSpec((lx, LANES), lambda i: (0, i)), pl.BlockSpec((lc, LANES), lambda i: (0, i))],
        out_shape=[_sds((lx, nlb * LANES)), _sds((lc, nlb * LANES))],
        compiler_params=_cparams(1, VMEM_BIG),
    )(dy_all, ds_re, ds_im, tsum, wst, perm)


def _s5_op_grads(u_all, dy_all, ds_re, ds_im, h_re, h_im, ncc):
    nlb, nch, _ = u_all.shape
    g, _, half = ds_re.shape
    ns = 2 * half

    def body(u_ref, dy_ref, sr_ref, si_ref, hr_ref, hi_ref, dw_ref, dt_ref, dc_ref, da_ref):
        das = []
        for gi in range(GPL):
            cols = slice(gi * CW, (gi + 1) * CW)
            sr, si, hr, hi = sr_ref[gi], si_ref[gi], hr_ref[gi], hi_ref[gi]
            ds = jnp.concatenate([sr, si], axis=1).astype(BF16)
            dy = dy_ref[0, :, cols]
            dw_ref[gi] = _mm_tn(u_ref[0, :, cols], ds)
            dt_ref[gi] = _mm_tn(u_ref[0, ncc:, cols], dy)
            hs = jnp.concatenate([hr[ncc:], hi[ncc:]], axis=1).astype(BF16)
            dc_ref[gi] = _mm_tn(hs, dy)
            das.append(jnp.concatenate([_colsum(hr * sr + hi * si), _colsum(hr * si - hi * sr)], axis=1))
        da_ref[...] = jnp.concatenate(das, axis=0)

    return pl.pallas_call(
        body,
        name="s5_op_grads",
        grid=(nlb,),
        in_specs=[pl.BlockSpec((1, nch, BW), lambda i: (i, 0, 0)), pl.BlockSpec((1, nch - ncc, BW), lambda i: (i, 0, 0))] + [_gspec(GPL, nch, half)] * 4,
        out_specs=[_gspec(GPL, CW, ns), _gspec(GPL, CW, CW), _gspec(GPL, ns, CW), pl.BlockSpec((GPL, ns), lambda i: (i, 0))],
        out_shape=[_sds((g, CW, ns)), _sds((g, CW, CW)), _sds((g, ns, CW)), _sds((g, ns))],
        compiler_params=_cparams(1, VMEM_BIG),
    )(u_all, dy_all, ds_re, ds_im, h_re, h_im)


def _merge_fwd(y_ssm, proj, lins, x, tgt, glu_wg, out_w, d_skip, glu_b, pscale, gate, fin_g, tm):
    lx, d = x.shape
    sw = y_ssm.shape[1]
    mix = out_w.shape[0]
    pw = mix - sw
    pgw = lins[0].shape[1]
    cb2 = glu_wg.shape[2]
    nh = NDEV // 2

    def body(y_ref, u_ref, z_ref, l0, l1, l2, l3, x_ref, t_ref, gw_ref, ow_ref, dk_ref, gb_ref, ps_ref, gt_ref, fg_ref,
             ypre_ref, y12_ref, br_ref, dxo_ref, loss_ref, dfg_ref, dgt_ref):
        @pl.when(pl.program_id(0) == 0)
        def _():
            loss_ref[...] = jnp.zeros_like(loss_ref)
            dfg_ref[...] = jnp.zeros_like(dfg_ref)
            dgt_ref[...] = jnp.zeros_like(dgt_ref)

        ypre = y_ref[...] + dk_ref[...] * u_ref[...]
        ypre_ref[...] = ypre
        yg = _gelu(ypre).astype(BF16)
        outs = []
        for j in range(nh):
            y1 = _mm(yg, gw_ref[j]) + gb_ref[:, j * cb2 : (j + 1) * cb2]
            y2 = _mm(yg, gw_ref[nh + j]) + gb_ref[:, (nh + j) * cb2 : (nh + j + 1) * cb2]
            y12_ref[:, j * cb2 : (j + 1) * cb2] = y1
            y12_ref[:, (nh + j) * cb2 : (nh + j + 1) * cb2] = y2
            outs.append(y1 * _sigmoid(y2))
        lin = jnp.concatenate([l0[...], l1[...], l2[...], l3[...]], axis=1) * ps_ref[...]
        brb = (jnp.concatenate([lin] + outs, axis=1) * _silu(z_ref[...])).astype(BF16)
        br_ref[...] = brb
        mixv = _mm(brb, ow_ref[...])
        xo = x_ref[...] + gt_ref[...] * mixv
        r2 = lax.rsqrt(_rowmean(xo * xo) + EPS)
        xh = xo * r2
        fg = fg_ref[...]
        e = xh * fg - t_ref[...]
        loss_ref[...] += 0.5 * _colsum(_rowmean(e * e))
        dy = e * (1.0 / d)
        dfg_ref[...] += _colsum(dy * xh)
        gy = dy * fg
        dxo = r2 * (gy - xh * _rowmean(gy * xh))
        dxo_ref[...] = dxo
        dgt_ref[...] += _colsum(dxo * mixv)

    vec = _whole((1, d))
    return pl.pallas_call(
        body,
        name="merge_fwd",
        grid=(lx // tm,),
        in_specs=[_rows(tm, sw), _rows(tm, sw, 1), _rows(tm, mix, 1)] + [_rows(tm, pgw)] * 4 + [_rows(tm, d), _rows(tm, d), _whole(glu_wg.shape), _whole(out_w.shape), _whole((1, sw)), _whole((1, 2 * sw)), _whole((1, pw)), vec, vec],
        out_specs=[_rows(tm, sw), _rows(tm, 2 * sw), _rows(tm, mix), _rows(tm, d), _acc((1, 1)), _acc((1, d)), _acc((1, d))],
        out_shape=[_sds((lx, sw)), _sds((lx, 2 * sw)), _sds((lx, mix), BF16), _sds((lx, d)), _sds((1, 1)), _sds((1, d)), _sds((1, d))],
        compiler_params=_cparams(1, VMEM_BIG),
    )(y_ssm, proj, proj, *lins, x, tgt, glu_wg, out_w, d_skip, glu_b, pscale, gate, fin_g)


def _out_bwd(dxo, gate, branch, out_w, lins, y12, proj, pscale, tm):
    lx, d = dxo.shape
    mix = out_w.shape[0]
    pgw = lins[0].shape[1]
    pw = 4 * pgw
    sw = mix - pw
    nb = lx // tm

    def body(dxo_ref, gt_ref, br_ref, ow_ref, l0, l1, l2, l3, y_ref, z_ref, ps_ref, dz_ref, dl_ref, dg_ref, dps_ref, dow_hbm, acc, sem):
        i = pl.program_id(0)

        @pl.when(i == 0)
        def _():
            acc[...] = jnp.zeros_like(acc)
            dps_ref[...] = jnp.zeros_like(dps_ref)

        dmix = (dxo_ref[...] * gt_ref[...]).astype(BF16)
        dbr = _mm_nt(dmix, ow_ref[...])
        acc[...] += _mm_tn(br_ref[...], dmix)
        z = z_ref[...]
        sg = _sigmoid(z)
        dbp = dbr * (z * sg)
        y1, y2 = y_ref[:, :sw], y_ref[:, sw:]
        s2 = _sigmoid(y2)
        ps = ps_ref[...]
        lin = jnp.concatenate([l0[...], l1[...], l2[...], l3[...]], axis=1)
        bp = jnp.concatenate([lin * ps, y1 * s2], axis=1)
        dz_ref[...] = dbr * bp * (sg * (1.0 + z * (1.0 - sg)))
        dlp = dbp[:, :pw]
        dl_ref[...] = dlp * ps
        dps_ref[...] += _colsum(dlp * lin)
        dss = dbp[:, pw:]
        dg_ref[...] = jnp.concatenate([dss * s2, dss * y1 * s2 * (1.0 - s2)], axis=1).astype(BF16)

        @pl.when(i == nb - 1)
        def _():
            cp = pltpu.make_async_copy(acc, dow_hbm, sem)
            cp.start()
            cp.wait()

    return pl.pallas_call(
        body,
        name="out_bwd",
        grid=(nb,),
        in_specs=[_rows(tm, d), _whole((1, d)), _rows(tm, mix), _whole(out_w.shape)] + [_rows(tm, pgw)] * 4 + [_rows(tm, 2 * sw), _rows(tm, mix, 1), _whole((1, pw))],
        out_specs=[_rows(tm, mix), _rows(tm, pw), _rows(tm, 2 * sw), _acc((1, pw)), _ANY],
        out_shape=[_sds((lx, mix)), _sds((lx, pw)), _sds((lx, 2 * sw), BF16), _sds((1, pw)), _sds((mix, d))],
        scratch_shapes=[pltpu.VMEM((mix, d), F32), pltpu.SemaphoreType.DMA],
        compiler_params=_cparams(1, VMEM_BIG),
    )(dxo, gate, branch, out_w, *lins, y12, proj, pscale)


def _glu_bwd(dg12, ypre, proj, glu_wg, d_skip, tm):
    lx, sw = ypre.shape
    cb2 = glu_wg.shape[2]
    nb = lx // tm

    def body(dg_ref, yp_ref, u_ref, gw_ref, dk_ref, dyp_ref, dgb_ref, dd_ref, dgw_hbm, acc, sem):
        i = pl.program_id(0)

        @pl.when(i == 0)
        def _():
            acc[...] = jnp.zeros_like(acc)
            dgb_ref[...] = jnp.zeros_like(dgb_ref)
            dd_ref[...] = jnp.zeros_like(dd_ref)

        ypre = yp_ref[...]
        ygb = _gelu(ypre).astype(BF16)
        dg = dg_ref[...]
        dyg = jnp.zeros((tm, sw), F32)
        for j in range(NDEV):
            dgj = dg[:, j * cb2 : (j + 1) * cb2]
            dyg = dyg + _mm_nt(dgj, gw_ref[j])
            acc[j] += _mm_tn(ygb, dgj)
        dgb_ref[...] += _colsum(dg.astype(F32))
        dyp = dyg * _gelu_grad(ypre)
        dyp_ref[...] = dyp
        dd_ref[...] += _colsum(dyp * u_ref[...])

        @pl.when(i == nb - 1)
        def _():
            cp = pltpu.make_async_copy(acc, dgw_hbm, sem)
            cp.start()
            cp.wait()

    return pl.pallas_call(
        body,
        name="glu_bwd",
        grid=(nb,),
        in_specs=[_rows(tm, 2 * sw), _rows(tm, sw), _rows(tm, sw, 1), _whole(glu_wg.shape), _whole((1, sw))],
        out_specs=[_rows(tm, sw), _acc((1, 2 * sw)), _acc((1, sw)), _ANY],
        out_shape=[_sds((lx, sw)), _sds((1, 2 * sw)), _sds((1, sw)), _sds(glu_wg.shape)],
        scratch_shapes=[pltpu.VMEM(glu_wg.shape, F32), pltpu.SemaphoreType.DMA],
        compiler_params=_cparams(1, VMEM_BIG),
    )(dg12, ypre, proj, glu_wg, d_skip)


def _pack(parts, total):
    flat = jnp.concatenate([p.reshape(-1).astype(F32) for p in parts])
    return jnp.pad(flat, (0, total - flat.shape[0]))


def _unpack(flat, shapes):
    out, off = [], 0
    for s in shapes:
        n = int(np.prod(s)) if len(s) else 1
        out.append(flat[off : off + n].reshape(s))
        off += n
    return out


def kernel(x, c, ctx, c_ctx, ada_w, ada_b, norm_g, in_w, pool_w, pool_scale, s5_lam_re, s5_lam_im, s5_log_dt, s5_b_re, s5_b_im, s5_c_re, s5_c_im, s5_d, glu_w, glu_b, out_w, final_g, loss_target, m_c_ctx, m_ada_w, m_ada_b, m_norm_g, m_in_w, m_pool_w, m_pool_scale, m_s5_lam_re, m_s5_lam_im, m_s5_log_dt, m_s5_b_re, m_s5_b_im, m_s5_c_re, m_s5_c_im, m_s5_d, m_glu_w, m_glu_b, m_out_w, m_final_g, v_c_ctx, v_ada_w, v_ada_b, v_norm_g, v_in_w, v_pool_w, v_pool_scale, v_s5_lam_re, v_s5_lam_im, v_s5_log_dt, v_s5_b_re, v_s5_b_im, v_s5_c_re, v_s5_c_im, v_s5_d, v_glu_w, v_glu_b, v_out_w, v_final_g):
    xr, tgt, xc = x[0], loss_target[0], ctx[0]
    lx, d = xr.shape
    lc = xc.shape[0]
    mix = out_w.shape[1] * NDEV
    sw = glu_w.shape[1]
    pw = mix - sw
    pgw = pw // len(POOL_WINDOWS)
    ngrp = sw // SSM_H
    cb = in_w.shape[2]
    cb2 = glu_w.shape[2]
    nmod = ada_w.shape[2]
    ncc, ncl = lc // CHUNK_T, lx // CHUNK_T
    nch = ncc + ncl
    tm = min(256, lx)
    tm_acc = min(512, lx)
    assert mix == d and pw == sw and 2 * cb == pw and NDEV * cb2 == 2 * sw and lx % GRID_W == 0
    mx, my, mc = _my_pos()
    me = 4 * mx + 2 * my + mc

    (c_all,) = _all_gather([c], "gather_c")
    call = jnp.concatenate([c_all.reshape(NDEV, d), c_ctx.reshape(1, d), jnp.zeros((NDEV - 1, d), F32)], axis=0)
    ada_w_l = ada_w[0]
    ada_b_l = lax.dynamic_slice_in_dim(ada_b, me * nmod, nmod, axis=1)
    m_loc = _mod_fwd(call, ada_w_l, ada_b_l)
    (m_all,) = _all_gather([m_loc], "gather_mod")
    mod = lax.dynamic_index_in_dim(m_all, me, axis=1, keepdims=False).reshape(1, NDEV * nmod)
    mod_c = m_all[:, NDEV, :].reshape(1, NDEV * nmod)
    shift, scale, gate = mod[:, :d], mod[:, d : 2 * d], mod[:, 2 * d :]
    shift_c, scale_c = mod_c[:, :d], mod_c[:, d : 2 * d]

    in_wg, glu_wg, out_wg, pool_wg = _all_gather(
        [in_w[0].astype(BF16), glu_w[0].astype(BF16), out_w[0].astype(BF16), pool_w[0].reshape(-1, pgw).astype(BF16)], "gather_weights"
    )
    out_wf = out_wg.reshape(mix, d)
    pool_wf = pool_wg.reshape(NDEV, len(POOL_WINDOWS), pgw // NDEV, pgw).transpose(1, 0, 2, 3).reshape(len(POOL_WINDOWS), pgw, pgw)

    ssm_blocks = (2, 3)
    proj, hb = _in_proj(xr, norm_g, scale, shift, in_wg, tuple(range(NDEV)), tm, "in_proj")
    uc, hcb = _in_proj(xc, norm_g, scale_c, shift_c, in_wg, ssm_blocks, min(tm, lc), "in_proj_ctx")
    lins, dmats = [], []
    for gi, w in enumerate(POOL_WINDOWS):
        lin_g, dmat_g = _pool_fwd(proj, gi * pgw, w, pool_wf[gi], f"pool_fwd{gi}")
        lins.append(lin_g)
        dmats.append(dmat_g)

    s5_params = tuple(p[0] for p in (s5_lam_re, s5_lam_im, s5_log_dt, s5_b_re, s5_b_im, s5_c_re, s5_c_im))
    (small, a16), small_vjp = jax.vjp(_s5_small, *s5_params)
    gb_ops = min(8, ngrp)
    tsum, wst, ccat = _s5_build(small, gb_ops)
    gb_scan = min(16, ngrp)
    half = wst.shape[2] // 2
    perm = _chunk_perm()
    flat = lambda a: a.reshape(ngrp * nch, half)
    grp = lambda a: a.reshape(ngrp, nch, half)
    u_all, s_re, s_im = _s5_state(proj, pw // LANES, uc, perm, wst, ncc, ncl)
    h_re, h_im = _s5_scan_fwd(flat(s_re), flat(s_im), a16, ncc, nch, gb_scan)
    y_ssm = _s5_out(u_all, grp(h_re), grp(h_im), tsum, ccat, perm, ncc, lx)

    ypre, y12, branch, dxo, loss_l, dfg, dgate = _merge_fwd(
        y_ssm, proj, lins, xr, tgt, glu_wg, out_wf, s5_d, glu_b, pool_scale, gate, final_g.reshape(1, d), tm
    )

    dz, dlin, dg12, dps, d_out_w = _out_bwd(dxo, gate, branch, out_wf, lins, y12, proj, pool_scale, tm)
    dypre, dglu_b, dd_skip, d_glu_w = _glu_bwd(dg12, ypre, proj, glu_wg, s5_d, tm_acc)

    dy_all, dh_re, dh_im = _s5_dstate(dypre, perm, ccat, ncc)
    ds_re, ds_im = _s5_scan_bwd(flat(dh_re), flat(dh_im), a16, ncc, nch, gb_scan)
    du5, duc = _s5_du(dy_all, grp(ds_re), grp(ds_im), tsum, wst, perm, ncc, lx, lc)
    dwst, dtsum, dccat, da16 = _s5_op_grads(u_all, dy_all, grp(ds_re), grp(ds_im), grp(h_re), grp(h_im), ncc)
    ds5 = small_vjp((tuple(_s5_build_bwd(small, dtsum, dwst, dccat, gb_ops)), da16))

    dus, dpool_w = [], []
    for gi, w in enumerate(POOL_WINDOWS):
        du_g, dw_g = _pool_bwd(dlin, gi * pgw, w, pool_wf[gi], dmats[gi], f"pool_bwd{gi}")
        dus.append(du_g)
        dpool_w.append(dw_g)

    dpcb, dsc_c, dsh_c, dg_c = _in_bwd_ctx(xc, duc, in_wg, ssm_blocks, norm_g, scale_c, shift_c)
    grad_x, dpb, dsc, dsh, dnorm_g = _in_bwd(xr, dxo, dus, du5, dypre, dz, in_wg, norm_g, scale, shift, s5_d, dg_c, tm)
    d_in_w = _in_w_grad(hb, dpb, hcb, dpcb, cb, tm_acc)

    dmod = jnp.concatenate(
        [jnp.concatenate([dsh, dsc, dgate], axis=1), jnp.concatenate([dsh_c, dsc_c, jnp.zeros((1, d), F32)], axis=1)], axis=0
    )
    (dmod_all,) = _all_gather([dmod], "gather_dmod")
    dmod_l = lax.dynamic_slice_in_dim(dmod_all, me * nmod, nmod, axis=2)
    g_ada_w, g_ada_b_l, cctx_part = _mod_bwd(call, dmod_l[:, 0, :], dmod_l[:, 1, :], ada_w_l)

    npw = len(POOL_WINDOWS)
    dpool_s = jnp.stack(dpool_w).reshape(npw, NDEV, pgw // NDEV, pgw).transpose(1, 0, 2, 3).reshape(NDEV, npw * (pgw // NDEV), pgw)
    land_in, land_glu, land_out, land_pool = _reduce_scatter(
        [d_in_w, d_glu_w, d_out_w.reshape(NDEV, mix // NDEV, d), dpool_s], [BF16] * 4, "grads"
    )

    small_names = ["norm_g", "pool_scale", "s5_lam_re", "s5_lam_im", "s5_log_dt", "s5_b_re", "s5_b_im", "s5_c_re", "s5_c_im", "s5_d", "glu_b", "final_g", "c_ctx"]
    small_w = dict(norm_g=norm_g, pool_scale=pool_scale, s5_lam_re=s5_lam_re, s5_lam_im=s5_lam_im, s5_log_dt=s5_log_dt, s5_b_re=s5_b_re, s5_b_im=s5_b_im, s5_c_re=s5_c_re, s5_c_im=s5_c_im, s5_d=s5_d, glu_b=glu_b, final_g=final_g, c_ctx=c_ctx)
    small_m = dict(norm_g=m_norm_g, pool_scale=m_pool_scale, s5_lam_re=m_s5_lam_re, s5_lam_im=m_s5_lam_im, s5_log_dt=m_s5_log_dt, s5_b_re=m_s5_b_re, s5_b_im=m_s5_b_im, s5_c_re=m_s5_c_re, s5_c_im=m_s5_c_im, s5_d=m_s5_d, glu_b=m_glu_b, final_g=m_final_g, c_ctx=m_c_ctx)
    small_v = dict(norm_g=v_norm_g, pool_scale=v_pool_scale, s5_lam_re=v_s5_lam_re, s5_lam_im=v_s5_lam_im, s5_log_dt=v_s5_log_dt, s5_b_re=v_s5_b_re, s5_b_im=v_s5_b_im, s5_c_re=v_s5_c_re, s5_c_im=v_s5_c_im, s5_d=v_s5_d, glu_b=v_glu_b, final_g=v_final_g, c_ctx=v_c_ctx)
    small_g = dict(norm_g=dnorm_g, pool_scale=dps, s5_lam_re=ds5[0], s5_lam_im=ds5[1], s5_log_dt=ds5[2], s5_b_re=ds5[3], s5_b_im=ds5[4], s5_c_re=ds5[5], s5_c_im=ds5[6], s5_d=dd_skip, glu_b=dglu_b, final_g=dfg, c_ctx=cctx_part)
    shapes = [small_w[k].shape for k in small_names]
    nsmall = sum(int(np.prod(s)) for s in shapes) + 1
    unit = NDEV * 8 * LANES
    tot = -(-nsmall // unit) * unit
    rows = tot // LANES
    gpack = _pack([small_g[k] for k in small_names] + [loss_l], tot).reshape(NDEV, rows // NDEV, LANES)
    (land_small,) = _reduce_scatter([gpack], [F32], "small")
    gsum = _sum_slots(land_small, "sum_small")
    (gall,) = _all_gather([gsum], "gather_small")
    (g_ada_b_all,) = _all_gather([g_ada_b_l], "gather_ada_b")
    gflat = gall.reshape(tot)
    loss = gflat[nsmall - 1]

    nab = NDEV * nmod
    tot2 = tot + -(-nab // (8 * LANES)) * (8 * LANES)
    g2 = jnp.concatenate([gflat, jnp.pad(g_ada_b_all.reshape(nab), (0, tot2 - tot - nab))]).reshape(tot2 // LANES, LANES)

    def pack2(dct, ab):
        return jnp.concatenate(
            [_pack([dct[k] for k in small_names] + [jnp.zeros((1,), F32)], tot), jnp.pad(ab.reshape(nab), (0, tot2 - tot - nab))]
        ).reshape(tot2 // LANES, LANES)

    _, d2, m2, v2 = _adam(g2, pack2(small_w, ada_b), pack2(small_m, m_ada_b), pack2(small_v, v_ada_b), "adam_small")

    def unpack2(flat2):
        flat2 = flat2.reshape(tot2)
        vals = dict(zip(small_names, _unpack(flat2, shapes)))
        vals["ada_b"] = flat2[tot : tot + nab].reshape(ada_b.shape)
        return vals

    res = {k: unpack2(a) for k, a in (("g", g2), ("d", d2), ("m", m2), ("v", v2))}

    def shard(name, g, land, w, m, v):
        shp = w.shape
        w2, m2_, v2_ = (a.reshape(-1, shp[-1]) for a in (w, m, v))
        out = _adam(g, w2, m2_, v2_, "adam_" + name, land=land)
        for k, a in zip(("g", "d", "m", "v"), out):
            res[k][name] = a.reshape(shp)

    shard("ada_w", g_ada_w, None, ada_w, m_ada_w, v_ada_w)
    shard("in_w", None, land_in, in_w, m_in_w, v_in_w)
    shard("pool_w", None, land_pool, pool_w, m_pool_w, v_pool_w)
    shard("glu_w", None, land_glu, glu_w, m_glu_w, v_glu_w)
    shard("out_w", None, land_out, out_w, m_out_w, v_out_w)

    names = ["c_ctx", "ada_w", "ada_b", "norm_g", "in_w", "pool_w", "pool_scale", "s5_lam_re", "s5_lam_im", "s5_log_dt", "s5_b_re", "s5_b_im", "s5_c_re", "s5_c_im", "s5_d", "glu_w", "glu_b", "out_w", "final_g"]
    return (loss, grad_x[None], *[res["g"][n] for n in names], *[res["d"][n] for n in names], *[res["m"][n] for n in names], *[res["v"][n] for n in names])
```

```python
import numpy as np

import jax
import jax.numpy as jnp
from jax import lax
from jax.experimental import pallas as pl
from jax.experimental.pallas import tpu as pltpu

F32 = jnp.float32
BF16 = jnp.bfloat16
NDEV = 8
EPS = 1e-6
GRID_W = 64
POOL_WINDOWS = (2, 4, 8, 16)
SSM_H = 16
CHUNK_T = 16
LANES = 128
ADAM_LR, ADAM_B1, ADAM_B2, ADAM_EPS, ADAM_WD, ADAM_STEP = 0.001, 0.9, 0.999, 1e-08, 0.01, 10
VMEM_BIG = 56 << 20
MESH_ID = pl.DeviceIdType.MESH

_HBM = pl.BlockSpec(memory_space=pltpu.HBM)
_ANY = pl.BlockSpec(memory_space=pl.ANY)
_SMEM = pl.BlockSpec(memory_space=pltpu.SMEM)


def _sds(shape, dtype=F32):
    return jax.ShapeDtypeStruct(tuple(shape), dtype)


def _cparams(ngrid=0, vmem=None):
    return pltpu.CompilerParams(
        dimension_semantics=("arbitrary",) * ngrid if ngrid else None, vmem_limit_bytes=vmem
    )


def _rows(tm, c, col=0):
    return pl.BlockSpec((tm, c), lambda i: (i, col))


def _whole(shape):
    nd = len(shape)
    return pl.BlockSpec(tuple(shape), lambda *_: (0,) * nd, pipeline_mode=pl.Buffered(1))


def _acc(shape):
    nd = len(shape)
    return pl.BlockSpec(tuple(shape), lambda *_: (0,) * nd)


def _mm(a, b):
    return jnp.dot(a, b, preferred_element_type=F32)


def _mm_nt(a, b):
    return lax.dot_general(a, b, (((1,), (1,)), ((), ())), preferred_element_type=F32)


def _mm_tn(a, b):
    return lax.dot_general(a, b, (((0,), (0,)), ((), ())), preferred_element_type=F32)


def _mm_split(k01, s):
    hi = s.astype(BF16)
    lo = (s - hi.astype(F32)).astype(BF16)
    return _mm(k01, hi) + _mm(k01, lo)


def _sigmoid(v):
    return 0.5 * (jnp.tanh(0.5 * v) + 1.0)


def _silu(v):
    return v * _sigmoid(v)


_GELU_K = 0.7978845608028654
_GELU_C = 0.044715


def _gelu(v):
    return 0.5 * v * (1.0 + jnp.tanh(_GELU_K * (v + _GELU_C * v * v * v)))


def _gelu_grad(v):
    th = jnp.tanh(_GELU_K * (v + _GELU_C * v * v * v))
    return 0.5 * (1.0 + th) + 0.5 * v * (1.0 - th * th) * (_GELU_K * (1.0 + 3.0 * _GELU_C * v * v))


def _colsum(v):
    return jnp.sum(v, axis=0, keepdims=True)


def _rowmean(v):
    return jnp.mean(v, axis=-1, keepdims=True)


NCHIP = NDEV // 2


def _my_pos():
    return lax.axis_index("x"), lax.axis_index("y"), lax.axis_index("c")


def _other_chips():
    x, y, _ = _my_pos()
    return [(1 - x, y), (x, 1 - y), (1 - x, 1 - y)]


def _remote(src, dst, send, recv, to):
    return pltpu.make_async_remote_copy(src, dst, send, recv, device_id=to, device_id_type=MESH_ID)


def _all_gather(arrs, name):
    n = len(arrs)

    def body(*refs):
        start, finish = _gather_copies(refs[:n], refs[n : 2 * n], *refs[2 * n :])
        start()
        finish()

    return pl.pallas_call(
        body, name=name, out_shape=_gather_shapes(arrs), in_specs=[_HBM] * n, out_specs=[_HBM] * n, scratch_shapes=_gather_sems(n)
    )(*arrs)


def _gather_shapes(arrs):
    return [_sds((NDEV,) + a.shape, a.dtype) for a in arrs]


def _gather_sems(n):
    return [pltpu.SemaphoreType.DMA((n, NDEV - 1)), pltpu.SemaphoreType.DMA((n, NDEV - 1)), pltpu.SemaphoreType.DMA((n,))]


def _gather_copies(ins, outs, send, recv, loc):
    n = len(ins)
    x, y, c = _my_pos()
    me, sib = (x, y, c), (x, y, 1 - c)
    chips = _other_chips()

    def slot(a, p):
        return outs[a].at[4 * p[0] + 2 * p[1] + p[2]]

    def copy(a, k, block, to, own=False):
        return _remote(ins[a] if own else slot(a, block), slot(a, block), send.at[a, k], recv.at[a, k], to)

    def mine():
        local = [pltpu.make_async_copy(ins[a], slot(a, me), loc.at[a]) for a in range(n)]
        sent = []
        for a in range(n):
            sent.append(copy(a, 0, me, sib, own=True))
            sent += [copy(a, 1 + j, me, (*chip, c), own=True) for j, chip in enumerate(chips)]
        return local, sent

    def start():
        local, sent = mine()
        for cp in local + sent:
            cp.start()

    def finish():
        local, sent = mine()
        for j, chip in enumerate(chips):
            for a in range(n):
                copy(a, 1 + j, (*chip, c), me).wait_recv()
                fwd = copy(a, 4 + j, (*chip, c), sib)
                fwd.start()
                sent.append(fwd)
        for a in range(n):
            copy(a, 0, sib, me).wait_recv()
            for j, chip in enumerate(chips):
                copy(a, 4 + j, (*chip, 1 - c), me).wait_recv()
        for cp in sent:
            cp.wait_send()
        for cp in local:
            cp.wait()

    return start, finish


def _sibling_swap(arrs, name):
    n = len(arrs)

    def body(*refs):
        ins, outs = refs[:n], refs[n : 2 * n]
        send, recv = refs[2 * n :]
        x, y, c = _my_pos()
        cps = [_remote(ins[a].at[:, 1 - c], outs[a], send.at[a], recv.at[a], (x, y, 1 - c)) for a in range(n)]
        for cp in cps:
            cp.start()
        for cp in cps:
            cp.wait()

    return pl.pallas_call(
        body,
        name=name,
        out_shape=[_sds((a.shape[0],) + a.shape[2:], a.dtype) for a in arrs],
        in_specs=[_HBM] * n,
        out_specs=[_HBM] * n,
        scratch_shapes=[pltpu.SemaphoreType.DMA((n,)), pltpu.SemaphoreType.DMA((n,))],
    )(*arrs)


def _pair_sum(arr, got, wire, name):
    _, _, r, c = arr.shape
    tr = _row_tile(r, c, budget=1 << 20)

    def body(a_ref, g_ref, o_ref):
        o_ref[0] = (a_ref[0, lax.axis_index("c")] + g_ref[0]).astype(wire)

    return pl.pallas_call(
        body,
        name=name,
        grid=(NCHIP, r // tr),
        in_specs=[pl.BlockSpec((1, 2, tr, c), lambda q, i: (q, 0, i, 0)), pl.BlockSpec((1, tr, c), lambda q, i: (q, i, 0))],
        out_specs=pl.BlockSpec((1, tr, c), lambda q, i: (q, i, 0)),
        out_shape=_sds((NCHIP, r, c), wire),
        compiler_params=_cparams(2),
    )(arr, got)


def _chip_exchange(arrs, name):
    n = len(arrs)

    def body(*refs):
        start, finish = _exchange_copies(refs[:n], refs[n : 2 * n], *refs[2 * n :])
        start()
        finish()

    return pl.pallas_call(
        body, name=name, out_shape=[_sds(a.shape, a.dtype) for a in arrs], in_specs=[_HBM] * n, out_specs=[_HBM] * n, scratch_shapes=_exchange_sems(n)
    )(*arrs)


def _exchange_sems(n):
    return [pltpu.SemaphoreType.DMA((n, NCHIP - 1)), pltpu.SemaphoreType.DMA((n, NCHIP - 1)), pltpu.SemaphoreType.DMA((n,))]


def _exchange_copies(ins, outs, send, recv, loc):
    n = len(ins)
    x, y, c = _my_pos()
    mine = 2 * x + y
    chips = _other_chips()

    def copies():
        local = [pltpu.make_async_copy(ins[a].at[mine], outs[a].at[mine], loc.at[a]) for a in range(n)]
        sent = [
            _remote(ins[a].at[2 * px + py], outs[a].at[mine], send.at[a, j], recv.at[a, j], (px, py, c))
            for a in range(n)
            for j, (px, py) in enumerate(chips)
        ]
        return local, sent

    def start():
        local, sent = copies()
        for cp in local + sent:
            cp.start()

    def finish():
        local, sent = copies()
        for a in range(n):
            for j, (px, py) in enumerate(chips):
                _remote(ins[a].at[mine], outs[a].at[2 * px + py], send.at[a, j], recv.at[a, j], (px, py, c)).wait_recv()
        for cp in sent:
            cp.wait_send()
        for cp in local:
            cp.wait()

    return start, finish


def _pair_partials(arrs, wires, tag):
    four = [a.reshape((NCHIP, 2) + a.shape[1:]) for a in arrs]
    got = _sibling_swap(four, "swap_" + tag)
    return [_pair_sum(a, g, w, f"pair_sum_{tag}{i}") for i, (a, g, w) in enumerate(zip(four, got, wires))]


def _reduce_scatter(arrs, wires, tag):
    return _chip_exchange(_pair_partials(arrs, wires, tag), "exchange_" + tag)


def _row_tile(r, c, budget=1 << 20):
    best = r
    for t in range(8, r, 8):
        if r % t == 0 and t * c * 4 <= budget:
            best = t
    if r * c * 4 <= budget:
        best = r
    return best


def _sum_chips(land):
    tot = land[0].astype(F32)
    for q in range(1, NCHIP):
        tot = tot + land[q].astype(F32)
    return tot


def _adam_math(w, g, m, v):
    m2 = ADAM_B1 * m + (1.0 - ADAM_B1) * g
    v2 = ADAM_B2 * v + (1.0 - ADAM_B2) * (g * g)
    mh = m2 / (1.0 - ADAM_B1**ADAM_STEP)
    vh = v2 / (1.0 - ADAM_B2**ADAM_STEP)
    delta = -ADAM_LR * (mh / (jnp.sqrt(vh) + ADAM_EPS) + ADAM_WD * w)
    return delta, m2, v2


def _adam(g, w, m, v, name, land=None):
    r, c = w.shape
    tr = _row_tile(r, c, budget=1 << 20)

    def body(*refs):
        if land is not None:
            l_ref, w_ref, m_ref, v_ref, g_ref, d_ref, m2_ref, v2_ref = refs
            gv = _sum_chips(l_ref)
            g_ref[...] = gv
        else:
            g_in, w_ref, m_ref, v_ref, d_ref, m2_ref, v2_ref = refs
            gv = g_in[...]
        d, m2, v2 = _adam_math(w_ref[...], gv, m_ref[...], v_ref[...])
        d_ref[...] = d
        m2_ref[...] = m2
        v2_ref[...] = v2

    blk = _rows(tr, c)
    if land is not None:
        in_specs = [pl.BlockSpec((NCHIP, tr, c), lambda i: (0, i, 0)), blk, blk, blk]
        out = pl.pallas_call(
            body, name=name, grid=(r // tr,), in_specs=in_specs, out_specs=[blk] * 4, out_shape=[_sds((r, c))] * 4, compiler_params=_cparams(1)
        )(land, w, m, v)
        return out
    out = pl.pallas_call(
        body, name=name, grid=(r // tr,), in_specs=[blk] * 4, out_specs=[blk] * 3, out_shape=[_sds((r, c))] * 3, compiler_params=_cparams(1)
    )(g, w, m, v)
    return (g,) + tuple(out)


def _sum_slots(land, name):
    _, r, c = land.shape
    tr = _row_tile(r, c, budget=1 << 20)

    def body(l_ref, o_ref):
        o_ref[...] = _sum_chips(l_ref)

    return pl.pallas_call(
        body,
        name=name,
        grid=(r // tr,),
        in_specs=[pl.BlockSpec((NCHIP, tr, c), lambda i: (0, i, 0))],
        out_specs=_rows(tr, c),
        out_shape=_sds((r, c)),
        compiler_params=_cparams(1),
    )(land)


def _mod_fwd(call, ada_w, ada_b_loc):
    def body(c_ref, w_ref, b_ref, o_ref):
        s = _silu(c_ref[...]).astype(BF16)
        o_ref[...] = _mm(s, w_ref[...].astype(BF16)) + b_ref[...]

    return pl.pallas_call(
        body, name="mod_fwd", out_shape=_sds((call.shape[0], ada_w.shape[1])), compiler_params=_cparams(0, 32 << 20)
    )(call, ada_w, ada_b_loc)


def _mod_bwd(call, dm_loc, dmc_loc, ada_w):
    d, n = ada_w.shape
    pad = call.shape[0] - NDEV - 1

    def body(c_ref, dm_ref, dmc_ref, w_ref, gw_ref, gb_ref, cp_ref):
        cv = c_ref[...]
        sg = _sigmoid(cv)
        dmc = _colsum(dmc_ref[...])
        dm = dm_ref[...]
        rows = jnp.concatenate([dm, dmc, jnp.zeros((pad, n), F32)], axis=0)
        gw_ref[...] = _mm_tn((cv * sg).astype(BF16), rows.astype(BF16))
        gb_ref[...] = _colsum(dm) + dmc
        back = _mm_nt(rows[NDEV:].astype(BF16), w_ref[...].astype(BF16))
        c8, s8 = cv[NDEV : NDEV + 1], sg[NDEV : NDEV + 1]
        cp_ref[...] = back[0:1] * (s8 * (1.0 + c8 * (1.0 - s8)))

    return pl.pallas_call(
        body, name="mod_bwd", out_shape=[_sds((d, n)), _sds((1, n)), _sds((1, d))], compiler_params=_cparams(0, 40 << 20)
    )(call, dm_loc, dmc_loc, ada_w)


def _in_proj(xr, norm_g, scale, shift, wg, jsel, tm, name):
    lx, d = xr.shape
    cb = wg.shape[2]

    def body(x_ref, g_ref, sc_ref, sh_ref, w_ref, p_ref, h_ref):
        xv = x_ref[...]
        r = lax.rsqrt(_rowmean(xv * xv) + EPS)
        hb = ((xv * r) * g_ref[...] * (1.0 + sc_ref[...]) + sh_ref[...]).astype(BF16)
        h_ref[...] = hb
        for q, j in enumerate(jsel):
            p_ref[:, q * cb : (q + 1) * cb] = _mm(hb, w_ref[j])

    vec = _whole((1, d))
    return pl.pallas_call(
        body,
        name=name,
        grid=(lx // tm,),
        in_specs=[_rows(tm, d), vec, vec, vec, _whole(wg.shape)],
        out_specs=[_rows(tm, len(jsel) * cb), _rows(tm, d)],
        out_shape=[_sds((lx, len(jsel) * cb)), _sds((lx, d), BF16)],
        compiler_params=_cparams(1, VMEM_BIG),
    )(xr, norm_g, scale, shift, wg)


def _in_bwd(xr, dxo, dus, du5, dypre, dz, wg, norm_g, scale, shift, d_skip, dg_init, tm):
    lx, d = xr.shape
    cb = wg.shape[2]
    pgw = dus[0].shape[1]
    sw = du5.shape[1]
    mix = dz.shape[1]
    ncol = NDEV * cb

    def body(x_ref, dxo_ref, u0, u1, u2, u3, d5_ref, dy_ref, dz_ref, w_ref, g_ref, sc_ref, sh_ref, dk_ref, gi_ref,
             gx_ref, dp_ref, dsc_ref, dsh_ref, dg_ref):
        i = pl.program_id(0)

        @pl.when(i == 0)
        def _():
            dsc_ref[...] = jnp.zeros_like(dsc_ref)
            dsh_ref[...] = jnp.zeros_like(dsh_ref)
            dg_ref[...] = gi_ref[...]

        dp = jnp.concatenate(
            [u0[...], u1[...], u2[...], u3[...], d5_ref[...] + dk_ref[...] * dy_ref[...], dz_ref[...]], axis=1
        ).astype(BF16)
        dp_ref[...] = dp
        dh = _mm_nt(dp[:, 0:cb], w_ref[0])
        for j in range(1, NDEV):
            dh = dh + _mm_nt(dp[:, j * cb : (j + 1) * cb], w_ref[j])
        xv = x_ref[...]
        r = lax.rsqrt(_rowmean(xv * xv) + EPS)
        xh = xv * r
        g = g_ref[...]
        one_sc = 1.0 + sc_ref[...]
        dsh_ref[...] += _colsum(dh)
        dsc_ref[...] += _colsum(dh * (xh * g))
        dg_ref[...] += _colsum(dh * one_sc * xh)
        dxh = dh * one_sc * g
        gx_ref[...] = r * (dxh - xh * _rowmean(dxh * xh)) + dxo_ref[...]

    vec = _whole((1, d))
    return pl.pallas_call(
        body,
        name="in_bwd",
        grid=(lx // tm,),
        in_specs=[_rows(tm, d), _rows(tm, d)] + [_rows(tm, pgw)] * 4 + [_rows(tm, sw), _rows(tm, sw), _rows(tm, mix), _whole(wg.shape), vec, vec, vec, _whole((1, sw)), vec],
        out_specs=[_rows(tm, d), _rows(tm, ncol), _acc((1, d)), _acc((1, d)), _acc((1, d))],
        out_shape=[_sds((lx, d)), _sds((lx, ncol), BF16), _sds((1, d)), _sds((1, d)), _sds((1, d))],
        compiler_params=_cparams(1, VMEM_BIG),
    )(xr, dxo, *dus, du5, dypre, dz, wg, norm_g, scale, shift, d_skip, dg_init)


def _in_bwd_ctx(xc, duc, wg, jsel, norm_g, scale, shift):
    lc, d = xc.shape
    cb = wg.shape[2]

    def body(x_ref, du_ref, w_ref, g_ref, sc_ref, sh_ref, dp_ref, dsc_ref, dsh_ref, dg_ref):
        dp = du_ref[...].astype(BF16)
        dp_ref[...] = dp
        dh = _mm_nt(dp[:, 0:cb], w_ref[jsel[0]])
        for q in range(1, len(jsel)):
            dh = dh + _mm_nt(dp[:, q * cb : (q + 1) * cb], w_ref[jsel[q]])
        xv = x_ref[...]
        xh = xv * lax.rsqrt(_rowmean(xv * xv) + EPS)
        dsh_ref[...] = _colsum(dh)
        dsc_ref[...] = _colsum(dh * (xh * g_ref[...]))
        dg_ref[...] = _colsum(dh * (1.0 + sc_ref[...]) * xh)

    return pl.pallas_call(
        body,
        name="in_bwd_ctx",
        out_shape=[_sds(duc.shape, BF16), _sds((1, d)), _sds((1, d)), _sds((1, d))],
        compiler_params=_cparams(0, VMEM_BIG),
    )(xc, duc, wg, norm_g, scale, shift)


def _in_w_grad(hb, dpb, hcb, dpcb, cb, tm):
    lx, d = hb.shape
    lc = hcb.shape[0]
    nb = lx // tm
    cg = 2 * cb

    def body(h_ref, dp_ref, hc_ref, dpc_ref, o_ref):
        j, i = pl.program_id(0), pl.program_id(1)

        @pl.when(i == 0)
        def _():
            o_ref[...] = jnp.zeros_like(o_ref)

        @pl.when(i < nb)
        def _():
            h = h_ref[...]
            o_ref[0] += _mm_tn(h, dp_ref[:, 0:cb])
            o_ref[1] += _mm_tn(h, dp_ref[:, cb:cg])

        @pl.when(jnp.logical_and(i == nb, j == 1))
        def _():
            h = hc_ref[...]
            o_ref[0] += _mm_tn(h, dpc_ref[:, 0:cb])
            o_ref[1] += _mm_tn(h, dpc_ref[:, cb:cg])

    return pl.pallas_call(
        body,
        name="in_w_grad",
        grid=(NDEV // 2, nb + 1),
        in_specs=[
            pl.BlockSpec((tm, d), lambda j, i: (jnp.minimum(i, nb - 1), 0)),
            pl.BlockSpec((tm, cg), lambda j, i: (jnp.minimum(i, nb - 1), j)),
            pl.BlockSpec((lc, d), lambda j, i: (0, 0)),
            pl.BlockSpec((lc, cg), lambda j, i: (0, 0)),
        ],
        out_specs=pl.BlockSpec((2, d, cb), lambda j, i: (j, 0, 0)),
        out_shape=_sds((NDEV, d, cb)),
        compiler_params=_cparams(2, VMEM_BIG),
    )(hb, dpb, hcb, dpcb)


def _pool_tables(w, rows, rb, pgw, transpose):
    t = np.arange(GRID_W)
    lo, hi = np.clip(t - w // 2, 0, GRID_W), np.clip(t + w - w // 2, 0, GRID_W)
    band = ((t[None, :] >= lo[:, None]) & (t[None, :] < hi[:, None])).astype(np.float32)
    if transpose:
        band = band.T
    kc = np.kron(np.eye(rb, dtype=np.float32), band)
    inv_c = np.tile((1.0 / (hi - lo).astype(np.float32))[:, None], (rb, pgw)).astype(np.float32)
    r = np.arange(rows)
    cnt_r = np.clip(r + w - w // 2, 0, rows) - np.clip(r - w // 2, 0, rows)
    inv_r = (1.0 / cnt_r.astype(np.float32)).astype(np.float32)
    return jnp.asarray(kc, BF16), jnp.asarray(inv_r), jnp.asarray(inv_c)


def _pool_fwd(proj, col0, w, pw_g, name):
    lx = proj.shape[0]
    pgw = pw_g.shape[0]
    rows = lx // GRID_W
    rb = min(4, rows)
    tok = rb * GRID_W
    lo = w // 2
    kc, inv_r, inv_c = _pool_tables(w, rows, rb, pgw, False)

    def body(p_hbm, w_ref, kc_ref, ir_ref, ic_ref, lin_ref, dm_ref, xp, sem):
        xp[pl.ds(0, lo * GRID_W), :] = jnp.zeros((lo * GRID_W, pgw), F32)
        xp[pl.ds((lo + rows) * GRID_W, (w - lo) * GRID_W), :] = jnp.zeros(((w - lo) * GRID_W, pgw), F32)
        cp = pltpu.make_async_copy(p_hbm.at[:, pl.ds(col0, pgw)], xp.at[pl.ds(lo * GRID_W, lx), :], sem)
        cp.start()
        cp.wait()

        def blk(b, carry):
            r0 = b * rb
            parts = []
            for rr in range(rb):
                base = pl.multiple_of((r0 + rr) * GRID_W, GRID_W)
                s = xp[pl.ds(base, GRID_W), :]
                for k in range(1, w):
                    s = s + xp[pl.ds(base + k * GRID_W, GRID_W), :]
                parts.append(s * ir_ref[r0 + rr])
            m = _mm_split(kc_ref[...], jnp.concatenate(parts, axis=0)) * ic_ref[...]
            u = xp[pl.ds(pl.multiple_of((r0 + lo) * GRID_W, GRID_W), tok), :]
            db = (m - u).astype(BF16)
            o0 = pl.multiple_of(r0 * GRID_W, GRID_W)
            dm_ref[pl.ds(o0, tok), :] = db
            lin_ref[pl.ds(o0, tok), :] = _mm(db, w_ref[...])
            return carry

        lax.fori_loop(0, rows // rb, blk, 0)

    return pl.pallas_call(
        body,
        name=name,
        in_specs=[_ANY, pl.BlockSpec(memory_space=pltpu.VMEM), pl.BlockSpec(memory_space=pltpu.VMEM), _SMEM, pl.BlockSpec(memory_space=pltpu.VMEM)],
        out_shape=[_sds((lx, pgw)), _sds((lx, pgw), BF16)],
        scratch_shapes=[pltpu.VMEM(((rows + w) * GRID_W, pgw), F32), pltpu.SemaphoreType.DMA],
        compiler_params=_cparams(0, VMEM_BIG),
    )(proj, pw_g, kc, inv_r, inv_c)


def _pool_bwd(dlin, col0, w, pw_g, dmat_g, name):
    lx = dlin.shape[0]
    pgw = pw_g.shape[0]
    rows = lx // GRID_W
    rb = min(4, rows)
    tok = rb * GRID_W
    front = w - w // 2 - 1
    kct, inv_r, inv_c = _pool_tables(w, rows, rb, pgw, True)

    def body(dl_hbm, w_ref, kc_ref, ir_ref, ic_ref, dm_ref, du_ref, dw_ref, tp, dl, sem):
        if front:
            tp[pl.ds(0, front * GRID_W), :] = jnp.zeros((front * GRID_W, pgw), F32)
        tp[pl.ds((front + rows) * GRID_W, (w - front) * GRID_W), :] = jnp.zeros(((w - front) * GRID_W, pgw), F32)
        cp = pltpu.make_async_copy(dl_hbm.at[:, pl.ds(col0, pgw)], dl, sem)
        cp.start()
        cp.wait()
        dw_ref[...] = jnp.zeros_like(dw_ref)

        def blk(b, carry):
            o0 = pl.multiple_of(b * tok, GRID_W)
            dlb = dl[pl.ds(o0, tok), :].astype(BF16)
            dw_ref[...] += _mm_tn(dm_ref[pl.ds(o0, tok), :], dlb)
            dd = _mm_nt(dlb, w_ref[...])
            du_ref[pl.ds(o0, tok), :] = -dd
            t = _mm_split(kc_ref[...], dd * ic_ref[...])
            for rr in range(rb):
                dst = pl.multiple_of((b * rb + rr + front) * GRID_W, GRID_W)
                tp[pl.ds(dst, GRID_W), :] = t[rr * GRID_W : (rr + 1) * GRID_W] * ir_ref[b * rb + rr]
            return carry

        lax.fori_loop(0, rows // rb, blk, 0)

        def rowl(r, carry):
            base = pl.multiple_of(r * GRID_W, GRID_W)
            s = tp[pl.ds(base, GRID_W), :]
            for k in range(1, w):
                s = s + tp[pl.ds(base + k * GRID_W, GRID_W), :]
            du_ref[pl.ds(base, GRID_W), :] += s
            return carry

        lax.fori_loop(0, rows, rowl, 0)

    vm = pl.BlockSpec(memory_space=pltpu.VMEM)
    return pl.pallas_call(
        body,
        name=name,
        in_specs=[_ANY, vm, vm, _SMEM, vm, vm],
        out_shape=[_sds((lx, pgw)), _sds((pgw, pgw))],
        scratch_shapes=[pltpu.VMEM(((rows + w) * GRID_W, pgw), F32), pltpu.VMEM((lx, pgw), F32), pltpu.SemaphoreType.DMA],
        compiler_params=_cparams(0, VMEM_BIG),
    )(dlin, pw_g, kct, inv_r, inv_c, dmat_g)


def _s5_small(lam_re, lam_im, log_dt, b_re, b_im, c_re, c_im):
    t = CHUNK_T
    dt = jnp.exp(log_dt)[..., None]
    zr, zi = lam_re * dt, lam_im * dt
    tau = jnp.arange(t + 1, dtype=F32)
    mag = jnp.exp(zr[..., None] * tau)
    pr, pi = mag * jnp.cos(zi[..., None] * tau), mag * jnp.sin(zi[..., None] * tau)
    ar, ai = pr[..., 1], pi[..., 1]
    den = lam_re * lam_re + lam_im * lam_im
    qr = ((ar - 1.0) * lam_re + ai * lam_im) / den
    qi = (ai * lam_re - (ar - 1.0) * lam_im) / den
    bbr = qr[..., None] * b_re - qi[..., None] * b_im
    bbi = qr[..., None] * b_im + qi[..., None] * b_re
    ctr, cti = jnp.swapaxes(c_re, 2, 3), jnp.swapaxes(c_im, 2, 3)
    lane_pad = lambda v: jnp.pad(jnp.swapaxes(v, 0, 1), ((0, 0), (0, 0), (0, 0), (0, LANES - v.shape[-1])))
    tp = lambda v: jnp.swapaxes(v, 1, 2)
    flip = lambda v: v[..., ::-1]
    pfr, pfi = tp(flip(pr[0, ..., :t])), tp(flip(pi[0, ..., :t]))
    pbr, pbi = tp(pr[1, ..., :t]), tp(pi[1, ..., :t])
    bfr, bfi, bbr_t, bbi_t = tp(bbr[0]), tp(bbi[0]), tp(bbr[1]), tp(bbi[1])
    a1 = jnp.concatenate([pfr, pbr, pfr, pbr], axis=-1)
    a2 = jnp.concatenate([-pfi, -pbi, pfi, pbi], axis=-1)
    b1 = jnp.concatenate([bfr, bbr_t, bfi, bbi_t], axis=-1)
    b2 = jnp.concatenate([bfi, bbi_t, bfr, bbr_t], axis=-1)
    a16 = jnp.concatenate([pr[0, ..., t], pr[1, ..., t], pi[0, ..., t], pi[1, ..., t]], axis=1)
    return (lane_pad(pr), lane_pad(pi), lane_pad(bbr), lane_pad(bbi), lane_pad(ctr), lane_pad(cti), a1, a2, b1, b2), a16


def _split2(v):
    hi = v.astype(BF16)
    return hi, (v - hi.astype(F32)).astype(BF16)


def _dotx(a, b, mm=_mm):
    ah, al = (a, None) if a.dtype == BF16 else _split2(a)
    bh, bl = (b, None) if b.dtype == BF16 else _split2(b)
    out = mm(ah, bh)
    if bl is not None:
        out = out + mm(ah, bl)
    if al is not None:
        out = out + mm(al, bh)
    return out


def _shift_lanes(v, k):
    if k == 0:
        return v
    n = v.shape[-1]
    lane = lax.broadcasted_iota(jnp.int32, v.shape, v.ndim - 1)
    r = pltpu.roll(v, k % n, axis=v.ndim - 1)
    return jnp.where(lane >= k, r, 0.0) if k > 0 else jnp.where(lane < n + k, r, 0.0)


def _op_consts():
    t, h = CHUNK_T, SSM_H
    j, o = np.divmod(np.arange(t * h), h)
    p = np.arange(LANES)[:, None]
    sel = lambda col: (p == col[None, :]).astype(np.float32)
    pw = np.stack([np.concatenate([sel(j), sel(j + 1)], axis=1), np.concatenate([sel(t - 1 - j), sel(t - j)], axis=1)])
    return jnp.asarray(pw, BF16), jnp.asarray(sel(o), BF16), jnp.asarray(sel(j).T, BF16), jnp.asarray(sel(o).T, BF16)


def _op_factors(gi, pr_ref, pi_ref, br_ref, bi_ref, cr_ref, ci_ref, pw_ref, ch_ref):
    n = pr_ref.shape[2]
    c16 = _dotx(jnp.concatenate([cr_ref[gi, 0], ci_ref[gi, 0], cr_ref[gi, 1], ci_ref[gi, 1]], axis=0), ch_ref[...])
    out = []
    for d in range(2):
        e = _dotx(jnp.concatenate([pr_ref[gi, d], pi_ref[gi, d]], axis=0), pw_ref[d])
        c = c16[2 * d * n : (2 * d + 1) * n], c16[(2 * d + 1) * n : (2 * d + 2) * n]
        bst = jnp.concatenate([br_ref[gi, d], -bi_ref[gi, d]], axis=0)
        out.append(((e[:n, :CW], e[n:, :CW]), (e[:n, CW:], e[n:, CW:]), c, bst))
    return out


def _op_spread(m_ref, u_ref, v_ref, gi):
    return _dotx(m_ref[...], jnp.concatenate([_rows_pad(u_ref[gi]), _rows_pad(v_ref[gi])], axis=1))


def _cmul(x, y):
    return x[0] * y[0] - x[1] * y[1], x[0] * y[1] + x[1] * y[0]


def _cmul_conj(g, y):
    return g[0] * y[0] + g[1] * y[1], g[1] * y[0] - g[0] * y[1]


def _rows_pad(v):
    return jnp.concatenate([v, jnp.zeros((LANES - v.shape[0], v.shape[1]), F32)], axis=0)


def _s5_build(small, gb, shards):
    pr, pi, br, bi, cr, ci, a1, a2, b1, b2 = small
    g, _, n, _ = pr.shape
    t, h, ns = CHUNK_T, SSM_H, 4 * n
    pw, ch, rep_s, til_s = _op_consts()
    nr, nin, steps = len(shards), 14, g // gb

    def body(*refs):
        pr_ref, pi_ref, br_ref, bi_ref, cr_ref, ci_ref, a1_ref, a2_ref, b1_ref, b2_ref, pw_ref, ch_ref, rs_ref, ts_ref = refs[:nin]
        t_ref, w_ref, c_ref = refs[nin + nr : nin + nr + 3]
        start, finish = _gather_copies(refs[nin : nin + nr], refs[nin + nr + 3 : nin + 2 * nr + 3], *refs[nin + 2 * nr + 3 :])
        pl.when(pl.program_id(0) == 0)(start)

        def group(gi, carry):
            rows, outs = [], []
            for e16, e1, c16, bst in _op_factors(gi, pr_ref, pi_ref, br_ref, bi_ref, cr_ref, ci_ref, pw_ref, ch_ref):
                rows.append(_dotx(bst, jnp.concatenate(_cmul(c16, e16), axis=0), _mm_tn)[:h])
                outs.append(_cmul(c16, e1))
            c_ref[gi] = jnp.concatenate([outs[0][0], outs[1][0], -outs[0][1], -outs[1][1]], axis=0).astype(BF16)
            for s in range(t):
                t_ref[gi, s * h : (s + 1) * h, :] = (_shift_lanes(rows[0], h * s) + _shift_lanes(rows[1], -h * (t - 1 - s))).astype(BF16)
            ae, be = _op_spread(rs_ref, a1_ref, a2_ref, gi), _op_spread(ts_ref, b1_ref, b2_ref, gi)
            w_ref[gi] = (ae[:, :ns] * be[:, :ns] + ae[:, ns:] * be[:, ns:]).astype(BF16)
            return carry

        lax.fori_loop(0, gb, group, 0)
        pl.when(pl.program_id(0) == steps - 1)(finish)

    sm = pl.BlockSpec((gb, 2, n, LANES), lambda i: (i, 0, 0, 0))
    ab = lambda v: pl.BlockSpec((gb,) + v.shape[1:], lambda i: (i, 0, 0))
    out = pl.pallas_call(
        body,
        name="s5_build",
        grid=(steps,),
        in_specs=[sm] * 6 + [ab(a1), ab(a2), ab(b1), ab(b2), _whole(pw.shape), _whole(ch.shape), _whole(rep_s.shape), _whole(til_s.shape)] + [_HBM] * nr,
        out_specs=[_gspec(gb, CW, CW), _gspec(gb, CW, 4 * n), _gspec(gb, 4 * n, CW)] + [_HBM] * nr,
        out_shape=[_sds((g, CW, CW), BF16), _sds((g, CW, 4 * n), BF16), _sds((g, 4 * n, CW), BF16)] + _gather_shapes(shards),
        scratch_shapes=_gather_sems(nr),
        compiler_params=_cparams(1, VMEM_BIG),
    )(pr, pi, br, bi, cr, ci, a1, a2, b1, b2, pw, ch, rep_s, til_s, *shards)
    return out[:3], out[3:]


def _s5_build_bwd(small, dtsum, dwst, dccat, gb, parts):
    pr, pi, br, bi, cr, ci, a1, a2, b1, b2 = small
    g, _, n, _ = pr.shape
    t, h, ns = CHUNK_T, SSM_H, 4 * n
    pw, ch, rep_s, til_s = _op_consts()
    nr, nin, nout, steps = len(parts), 17, 10, g // gb

    def body(*refs):
        pr_ref, pi_ref, br_ref, bi_ref, cr_ref, ci_ref, a1_ref, a2_ref, b1_ref, b2_ref, pw_ref, ch_ref, rs_ref, ts_ref, dt_ref, dw_ref, dc_ref = refs[:nin]
        dpr_ref, dpi_ref, dbr_ref, dbi_ref, dcr_ref, dci_ref, da1_ref, da2_ref, db1_ref, db2_ref = refs[nin + nr : nin + nr + nout]
        start, finish = _exchange_copies(refs[nin : nin + nr], refs[nin + nr + nout : nin + 2 * nr + nout], *refs[nin + 2 * nr + nout :])
        pl.when(pl.program_id(0) == 0)(start)

        def group(gi, carry):
            drow = [jnp.zeros((h, CW), F32), jnp.zeros((h, CW), F32)]
            for s in range(t):
                blk = dt_ref[gi, s * h : (s + 1) * h, :]
                drow[0] = drow[0] + _shift_lanes(blk, -h * s)
                drow[1] = drow[1] + _shift_lanes(blk, h * (t - 1 - s))
            dcc = dc_ref[gi]
            dcs = []
            for d, (e16, e1, c16, bst) in enumerate(_op_factors(gi, pr_ref, pi_ref, br_ref, bi_ref, cr_ref, ci_ref, pw_ref, ch_ref)):
                dr = _rows_pad(drow[d])
                db = _dotx(jnp.concatenate(_cmul(c16, e16), axis=0), dr, _mm_nt)
                dbr_ref[gi, d] = db[:n]
                dbi_ref[gi, d] = -db[n:]
                dst = _dotx(bst, dr)
                dca = dst[:n], dst[n:]
                gx = dcc[d * n : (d + 1) * n], -dcc[(2 + d) * n : (3 + d) * n]
                dc16, dc16b = _cmul_conj(dca, e16), _cmul_conj(gx, e1)
                dcs += [dc16[0] + dc16b[0], dc16[1] + dc16b[1]]
                de = jnp.concatenate([jnp.concatenate(_cmul_conj(dca, c16), axis=0), jnp.concatenate(_cmul_conj(gx, c16), axis=0)], axis=1)
                dp = _dotx(de, pw_ref[d], _mm_nt)
                dpr_ref[gi, d] = dp[:n]
                dpi_ref[gi, d] = dp[n:]
            dct = _dotx(jnp.concatenate(dcs, axis=0), ch_ref[...], _mm_nt)
            for d in range(2):
                dcr_ref[gi, d] = dct[2 * d * n : (2 * d + 1) * n]
                dci_ref[gi, d] = dct[(2 * d + 1) * n : (2 * d + 2) * n]
            dw = dw_ref[gi]
            ae, be = _op_spread(rs_ref, a1_ref, a2_ref, gi), _op_spread(ts_ref, b1_ref, b2_ref, gi)
            da = _dotx(rs_ref[...], jnp.concatenate([dw * be[:, :ns], dw * be[:, ns:]], axis=1), _mm_tn)
            db = _dotx(ts_ref[...], jnp.concatenate([dw * ae[:, :ns], dw * ae[:, ns:]], axis=1), _mm_tn)
            da1_ref[gi] = da[:t, :ns]
            da2_ref[gi] = da[:t, ns:]
            db1_ref[gi] = db[:h, :ns]
            db2_ref[gi] = db[:h, ns:]
            return carry

        lax.fori_loop(0, gb, group, 0)
        pl.when(pl.program_id(0) == steps - 1)(finish)

    sm = pl.BlockSpec((gb, 2, n, LANES), lambda i: (i, 0, 0, 0))
    ab = lambda v: pl.BlockSpec((gb,) + v.shape[1:], lambda i: (i, 0, 0))
    ops = [_gspec(gb, CW, CW), _gspec(gb, CW, 4 * n), _gspec(gb, 4 * n, CW)]
    out = pl.pallas_call(
        body,
        name="s5_build_bwd",
        grid=(steps,),
        in_specs=[sm] * 6 + [ab(a1), ab(a2), ab(b1), ab(b2), _whole(pw.shape), _whole(ch.shape), _whole(rep_s.shape), _whole(til_s.shape)] + ops + [_HBM] * nr,
        out_specs=[sm] * 6 + [ab(a1), ab(a2), ab(b1), ab(b2)] + [_HBM] * nr,
        out_shape=[_sds(v.shape) for v in small] + [_sds(p.shape, p.dtype) for p in parts],
        scratch_shapes=_exchange_sems(nr),
        compiler_params=_cparams(1, VMEM_BIG),
    )(pr, pi, br, bi, cr, ci, a1, a2, b1, b2, pw, ch, rep_s, til_s, dtsum, dwst, dccat, *parts)
    return out[:nout], out[nout:]


SCAN_UNROLL = 4
GPL = LANES // SSM_H
CW = CHUNK_T * SSM_H
BW = CHUNK_T * LANES


def _chunk_perm():
    o = np.arange(BW)
    src = ((o % CW) // SSM_H) * LANES + (o // CW) * SSM_H + o % SSM_H
    return jnp.asarray(np.arange(BW)[:, None] == src[None, :], BF16)


def _gspec(gb, a, b):
    return pl.BlockSpec((gb, a, b), lambda i: (i, 0, 0))


def _chunk_rows(ref, n):
    return jnp.concatenate([ref[pl.ds(s, n, stride=CHUNK_T), :] for s in range(CHUNK_T)], axis=1)


def _s5_state(proj, col0, uc, perm, wst, ncc, ncl):
    nlb = uc.shape[1] // LANES
    g, _, ns = wst.shape
    nch = ncc + ncl
    half = ns // 2

    def body(p_ref, c_ref, pm_ref, w_ref, u_ref, sr_ref, si_ref):
        xrows = jnp.concatenate([_chunk_rows(c_ref, ncc), _chunk_rows(p_ref, ncl)], axis=0).astype(BF16)
        u = _mm(xrows, pm_ref[...]).astype(BF16)
        u_ref[0] = u
        for gi in range(GPL):
            s = _mm(u[:, gi * CW : (gi + 1) * CW], w_ref[gi].astype(BF16))
            sr_ref[gi] = s[:, :half]
            si_ref[gi] = s[:, half:]

    return pl.pallas_call(
        body,
        name="s5_state",
        grid=(nlb,),
        in_specs=[
            pl.BlockSpec((ncl * CHUNK_T, LANES), lambda i: (0, col0 + i)),
            pl.BlockSpec((ncc * CHUNK_T, LANES), lambda i: (0, i)),
            _whole(perm.shape),
            _gspec(GPL, CW, ns),
        ],
        out_specs=[pl.BlockSpec((1, nch, BW), lambda i: (i, 0, 0)), _gspec(GPL, nch, half), _gspec(GPL, nch, half)],
        out_shape=[_sds((nlb, nch, BW), BF16), _sds((g, nch, half)), _sds((g, nch, half))],
        compiler_params=_cparams(1, VMEM_BIG),
    )(proj, uc, perm, wst)


def _unrolled_loop(n, step, init):
    u = SCAN_UNROLL if n % SCAN_UNROLL == 0 else 1

    def trip(i, c):
        for k in range(u):
            c = step(i * u + k, c)
        return c

    return lax.fori_loop(0, n // u, trip, init)


def _lane_masks(gb, half):
    lane = lax.broadcasted_iota(jnp.int32, (gb, half), 1)
    return lane < (half // 2)


def _s5_scan_fwd(s_re, s_im, a16, ncc, nch, gb):
    g, ns = a16.shape
    half = ns // 2

    def body(sr_ref, si_ref, a_ref, hr_ref, hi_ref):
        ar, ai = a_ref[:, :half], a_ref[:, half:]
        mf = _lane_masks(gb, half)
        zero = jnp.zeros((gb, half), F32)
        hr_ref[...] = jnp.zeros_like(hr_ref)
        hi_ref[...] = jnp.zeros_like(hi_ref)

        def step(t, c):
            fr, fi, br, bi = c
            rf = pl.ds(t, gb, stride=nch)
            rb = pl.ds(jnp.where(t < ncc, ncc - 1 - t, nch - 1 - (t - ncc)), gb, stride=nch)
            hr_ref[rf, :] += jnp.where(mf, fr, 0.0)
            hi_ref[rf, :] += jnp.where(mf, fi, 0.0)
            hr_ref[rb, :] += jnp.where(mf, 0.0, br)
            hi_ref[rb, :] += jnp.where(mf, 0.0, bi)
            return (
                ar * fr - ai * fi + sr_ref[rf, :],
                ar * fi + ai * fr + si_ref[rf, :],
                ar * br - ai * bi + sr_ref[rb, :],
                ar * bi + ai * br + si_ref[rb, :],
            )

        _unrolled_loop(nch, step, (zero, zero, zero, zero))

    blk = pl.BlockSpec((gb * nch, half), lambda i: (i, 0))
    return pl.pallas_call(
        body,
        name="s5_scan_fwd",
        grid=(g // gb,),
        in_specs=[blk, blk, pl.BlockSpec((gb, ns), lambda i: (i, 0))],
        out_specs=[blk, blk],
        out_shape=[_sds((g * nch, half))] * 2,
        compiler_params=_cparams(1, VMEM_BIG),
    )(s_re, s_im, a16)


def _token_rows_store(ref, val, n):
    for s in range(CHUNK_T):
        ref[pl.ds(s, n, stride=CHUNK_T), :] = val[:, s * LANES : (s + 1) * LANES]


def _s5_out(u_all, h_re, h_im, tsum, ccat, perm, ncc, lx):
    nlb, nch, _ = u_all.shape
    g, ns, _ = ccat.shape
    ncl = nch - ncc
    half = ns // 2

    def body(u_ref, hr_ref, hi_ref, t_ref, c_ref, pm_ref, y_ref):
        parts = []
        for gi in range(GPL):
            u = u_ref[0, ncc:, gi * CW : (gi + 1) * CW]
            hs = jnp.concatenate([hr_ref[gi, ncc:, :], hi_ref[gi, ncc:, :]], axis=1).astype(BF16)
            parts.append(_mm(u, t_ref[gi].astype(BF16)) + _mm(hs, c_ref[gi].astype(BF16)))
        y = jnp.concatenate(parts, axis=1)
        hi = y.astype(BF16)
        lo = (y - hi.astype(F32)).astype(BF16)
        _token_rows_store(y_ref, _mm_nt(hi, pm_ref[...]) + _mm_nt(lo, pm_ref[...]), ncl)

    return pl.pallas_call(
        body,
        name="s5_out",
        grid=(nlb,),
        in_specs=[pl.BlockSpec((1, nch, BW), lambda i: (i, 0, 0)), _gspec(GPL, nch, half), _gspec(GPL, nch, half), _gspec(GPL, CW, CW), _gspec(GPL, ns, CW), _whole(perm.shape)],
        out_specs=pl.BlockSpec((lx, LANES), lambda i: (0, i)),
        out_shape=_sds((lx, nlb * LANES)),
        compiler_params=_cparams(1, VMEM_BIG),
    )(u_all, h_re, h_im, tsum, ccat, perm)


def _s5_dstate(dypre, perm, ccat, ncc):
    lx, sw = dypre.shape
    nlb = sw // LANES
    g, ns, _ = ccat.shape
    ncl = lx // CHUNK_T
    nch = ncl + ncc
    half = ns // 2

    def body(dy_ref, pm_ref, c_ref, dyr_ref, dr_ref, di_ref):
        dy = _mm(_chunk_rows(dy_ref, ncl).astype(BF16), pm_ref[...]).astype(BF16)
        dyr_ref[0] = dy
        for gi in range(GPL):
            dh = _mm_nt(dy[:, gi * CW : (gi + 1) * CW], c_ref[gi].astype(BF16))
            dr_ref[gi, :ncc, :] = jnp.zeros((ncc, half), F32)
            di_ref[gi, :ncc, :] = jnp.zeros((ncc, half), F32)
            dr_ref[gi, ncc:, :] = dh[:, :half]
            di_ref[gi, ncc:, :] = dh[:, half:]

    return pl.pallas_call(
        body,
        name="s5_dstate",
        grid=(nlb,),
        in_specs=[pl.BlockSpec((lx, LANES), lambda i: (0, i)), _whole(perm.shape), _gspec(GPL, ns, CW)],
        out_specs=[pl.BlockSpec((1, ncl, BW), lambda i: (i, 0, 0)), _gspec(GPL, nch, half), _gspec(GPL, nch, half)],
        out_shape=[_sds((nlb, ncl, BW), BF16), _sds((g, nch, half)), _sds((g, nch, half))],
        compiler_params=_cparams(1, VMEM_BIG),
    )(dypre, perm, ccat)


def _s5_scan_bwd(dh_re, dh_im, a16, ncc, nch, gb):
    g, ns = a16.shape
    half = ns // 2
    ncl = nch - ncc

    def body(dr_ref, di_ref, a_ref, sr_ref, si_ref):
        ar, ai = a_ref[:, :half], a_ref[:, half:]
        mf = _lane_masks(gb, half)
        zero = jnp.zeros((gb, half), F32)
        sr_ref[...] = jnp.zeros_like(sr_ref)
        si_ref[...] = jnp.zeros_like(si_ref)

        def step(t, c):
            lfr, lfi, lbr, lbi = c
            pf = pl.ds(nch - 1 - t, gb, stride=nch)
            pb = pl.ds(jnp.where(t < ncl, ncc + t, t - ncl), gb, stride=nch)
            sr_ref[pf, :] += jnp.where(mf, lfr, 0.0)
            si_ref[pf, :] += jnp.where(mf, lfi, 0.0)
            sr_ref[pb, :] += jnp.where(mf, 0.0, lbr)
            si_ref[pb, :] += jnp.where(mf, 0.0, lbi)
            return (
                dr_ref[pf, :] + ar * lfr + ai * lfi,
                di_ref[pf, :] + ar * lfi - ai * lfr,
                dr_ref[pb, :] + ar * lbr + ai * lbi,
                di_ref[pb, :] + ar * lbi - ai * lbr,
            )

        _unrolled_loop(nch, step, (zero,) * 4)

    blk = pl.BlockSpec((gb * nch, half), lambda i: (i, 0))
    return pl.pallas_call(
        body,
        name="s5_scan_bwd",
        grid=(g // gb,),
        in_specs=[blk, blk, pl.BlockSpec((gb, ns), lambda i: (i, 0))],
        out_specs=[blk, blk],
        out_shape=[_sds((g * nch, half))] * 2,
        compiler_params=_cparams(1, VMEM_BIG),
    )(dh_re, dh_im, a16)


def _s5_du(dy_all, ds_re, ds_im, tsum, wst, perm, ncc, lx, lc):
    nlb, ncl, _ = dy_all.shape
    g, _, ns = wst.shape
    nch = ncc + ncl
    half = ns // 2

    def body(dy_ref, sr_ref, si_ref, t_ref, w_ref, pm_ref, du_ref, dc_ref):
        parts = []
        for gi in range(GPL):
            ds = jnp.concatenate([sr_ref[gi], si_ref[gi]], axis=1).astype(BF16)
            d_all = _mm_nt(ds, w_ref[gi].astype(BF16))
            d_lat = d_all[ncc:] + _mm_nt(dy_ref[0, :, gi * CW : (gi + 1) * CW], t_ref[gi].astype(BF16))
            parts.append(jnp.concatenate([d_all[:ncc], d_lat], axis=0))
        du = _mm_nt(jnp.concatenate(parts, axis=1).astype(BF16), pm_ref[...])
        _token_rows_store(dc_ref, du[:ncc], ncc)
        _token_rows_store(du_ref, du[ncc:], ncl)

    return pl.pallas_call(
        body,
        name="s5_du",
        grid=(nlb,),
        in_specs=[pl.BlockSpec((1, ncl, BW), lambda i: (i, 0, 0)), _gspec(GPL, nch, half), _gspec(GPL, nch, half), _gspec(GPL, CW, CW), _gspec(GPL, CW, ns), _whole(perm.shape)],
        out_specs=[pl.BlockSpec((lx, LANES), lambda i: (0, i)), pl.BlockSpec((lc, LANES), lambda i: (0, i))],
        out_shape=[_sds((lx, nlb * LANES)), _sds((lc, nlb * LANES))],
        compiler_params=_cparams(1, VMEM_BIG),
    )(dy_all, ds_re, ds_im, tsum, wst, perm)


def _s5_op_grads(u_all, dy_all, ds_re, ds_im, h_re, h_im, ncc):
    nlb, nch, _ = u_all.shape
    g, _, half = ds_re.shape
    ns = 2 * half

    def body(u_ref, dy_ref, sr_ref, si_ref, hr_ref, hi_ref, dw_ref, dt_ref, dc_ref, da_ref):
        das = []
        for gi in range(GPL):
            cols = slice(gi * CW, (gi + 1) * CW)
            sr, si, hr, hi = sr_ref[gi], si_ref[gi], hr_ref[gi], hi_ref[gi]
            ds = jnp.concatenate([sr, si], axis=1).astype(BF16)
            dy = dy_ref[0, :, cols]
            dw_ref[gi] = _mm_tn(u_ref[0, :, cols], ds)
            dt_ref[gi] = _mm_tn(u_ref[0, ncc:, cols], dy)
            hs = jnp.concatenate([hr[ncc:], hi[ncc:]], axis=1).astype(BF16)
            dc_ref[gi] = _mm_tn(hs, dy)
            das.append(jnp.concatenate([_colsum(hr * sr + hi * si), _colsum(hr * si - hi * sr)], axis=1))
        da_ref[...] = jnp.concatenate(das, axis=0)

    return pl.pallas_call(
        body,
        name="s5_op_grads",
        grid=(nlb,),
        in_specs=[pl.BlockSpec((1, nch, BW), lambda i: (i, 0, 0)), pl.BlockSpec((1, nch - ncc, BW), lambda i: (i, 0, 0))] + [_gspec(GPL, nch, half)] * 4,
        out_specs=[_gspec(GPL, CW, ns), _gspec(GPL, CW, CW), _gspec(GPL, ns, CW), pl.BlockSpec((GPL, ns), lambda i: (i, 0))],
        out_shape=[_sds((g, CW, ns)), _sds((g, CW, CW)), _sds((g, ns, CW)), _sds((g, ns))],
        compiler_params=_cparams(1, VMEM_BIG),
    )(u_all, dy_all, ds_re, ds_im, h_re, h_im)


def _merge_fwd(y_ssm, proj, lins, x, tgt, glu_wg, out_w, d_skip, glu_b, pscale, gate, fin_g, tm):
    lx, d = x.shape
    sw = y_ssm.shape[1]
    mix = out_w.shape[0]
    pw = mix - sw
    pgw = lins[0].shape[1]
    cb2 = glu_wg.shape[2]
    nh = NDEV // 2

    def body(y_ref, u_ref, z_ref, l0, l1, l2, l3, x_ref, t_ref, gw_ref, ow_ref, dk_ref, gb_ref, ps_ref, gt_ref, fg_ref,
             ypre_ref, y12_ref, br_ref, dxo_ref, loss_ref, dfg_ref, dgt_ref):
        @pl.when(pl.program_id(0) == 0)
        def _():
            loss_ref[...] = jnp.zeros_like(loss_ref)
            dfg_ref[...] = jnp.zeros_like(dfg_ref)
            dgt_ref[...] = jnp.zeros_like(dgt_ref)

        ypre = y_ref[...] + dk_ref[...] * u_ref[...]
        ypre_ref[...] = ypre
        yg = _gelu(ypre).astype(BF16)
        outs = []
        for j in range(nh):
            y1 = _mm(yg, gw_ref[j]) + gb_ref[:, j * cb2 : (j + 1) * cb2]
            y2 = _mm(yg, gw_ref[nh + j]) + gb_ref[:, (nh + j) * cb2 : (nh + j + 1) * cb2]
            y12_ref[:, j * cb2 : (j + 1) * cb2] = y1
            y12_ref[:, (nh + j) * cb2 : (nh + j + 1) * cb2] = y2
            outs.append(y1 * _sigmoid(y2))
        lin = jnp.concatenate([l0[...], l1[...], l2[...], l3[...]], axis=1) * ps_ref[...]
        brb = (jnp.concatenate([lin] + outs, axis=1) * _silu(z_ref[...])).astype(BF16)
        br_ref[...] = brb
        mixv = _mm(brb, ow_ref[...])
        xo = x_ref[...] + gt_ref[...] * mixv
        r2 = lax.rsqrt(_rowmean(xo * xo) + EPS)
        xh = xo * r2
        fg = fg_ref[...]
        e = xh * fg - t_ref[...]
        loss_ref[...] += 0.5 * _colsum(_rowmean(e * e))
        dy = e * (1.0 / d)
        dfg_ref[...] += _colsum(dy * xh)
        gy = dy * fg
        dxo = r2 * (gy - xh * _rowmean(gy * xh))
        dxo_ref[...] = dxo
        dgt_ref[...] += _colsum(dxo * mixv)

    vec = _whole((1, d))
    return pl.pallas_call(
        body,
        name="merge_fwd",
        grid=(lx // tm,),
        in_specs=[_rows(tm, sw), _rows(tm, sw, 1), _rows(tm, mix, 1)] + [_rows(tm, pgw)] * 4 + [_rows(tm, d), _rows(tm, d), _whole(glu_wg.shape), _whole(out_w.shape), _whole((1, sw)), _whole((1, 2 * sw)), _whole((1, pw)), vec, vec],
        out_specs=[_rows(tm, sw), _rows(tm, 2 * sw), _rows(tm, mix), _rows(tm, d), _acc((1, 1)), _acc((1, d)), _acc((1, d))],
        out_shape=[_sds((lx, sw)), _sds((lx, 2 * sw)), _sds((lx, mix), BF16), _sds((lx, d)), _sds((1, 1)), _sds((1, d)), _sds((1, d))],
        compiler_params=_cparams(1, VMEM_BIG),
    )(y_ssm, proj, proj, *lins, x, tgt, glu_wg, out_w, d_skip, glu_b, pscale, gate, fin_g)


def _out_bwd(dxo, gate, branch, out_w, lins, y12, proj, pscale, tm):
    lx, d = dxo.shape
    mix = out_w.shape[0]
    pgw = lins[0].shape[1]
    pw = 4 * pgw
    sw = mix - pw
    nb = lx // tm

    def body(dxo_ref, gt_ref, br_ref, ow_ref, l0, l1, l2, l3, y_ref, z_ref, ps_ref, dz_ref, dl_ref, dg_ref, dps_ref, dow_hbm, acc, sem):
        i = pl.program_id(0)

        @pl.when(i == 0)
        def _():
            acc[...] = jnp.zeros_like(acc)
            dps_ref[...] = jnp.zeros_like(dps_ref)

        dmix = (dxo_ref[...] * gt_ref[...]).astype(BF16)
        dbr = _mm_nt(dmix, ow_ref[...])
        acc[...] += _mm_tn(br_ref[...], dmix)
        z = z_ref[...]
        sg = _sigmoid(z)
        dbp = dbr * (z * sg)
        y1, y2 = y_ref[:, :sw], y_ref[:, sw:]
        s2 = _sigmoid(y2)
        ps = ps_ref[...]
        lin = jnp.concatenate([l0[...], l1[...], l2[...], l3[...]], axis=1)
        bp = jnp.concatenate([lin * ps, y1 * s2], axis=1)
        dz_ref[...] = dbr * bp * (sg * (1.0 + z * (1.0 - sg)))
        dlp = dbp[:, :pw]
        dl_ref[...] = dlp * ps
        dps_ref[...] += _colsum(dlp * lin)
        dss = dbp[:, pw:]
        dg_ref[...] = jnp.concatenate([dss * s2, dss * y1 * s2 * (1.0 - s2)], axis=1).astype(BF16)

        @pl.when(i == nb - 1)
        def _():
            cp = pltpu.make_async_copy(acc, dow_hbm, sem)
            cp.start()
            cp.wait()

    return pl.pallas_call(
        body,
        name="out_bwd",
        grid=(nb,),
        in_specs=[_rows(tm, d), _whole((1, d)), _rows(tm, mix), _whole(out_w.shape)] + [_rows(tm, pgw)] * 4 + [_rows(tm, 2 * sw), _rows(tm, mix, 1), _whole((1, pw))],
        out_specs=[_rows(tm, mix), _rows(tm, pw), _rows(tm, 2 * sw), _acc((1, pw)), _ANY],
        out_shape=[_sds((lx, mix)), _sds((lx, pw)), _sds((lx, 2 * sw), BF16), _sds((1, pw)), _sds((mix, d))],
        scratch_shapes=[pltpu.VMEM((mix, d), F32), pltpu.SemaphoreType.DMA],
        compiler_params=_cparams(1, VMEM_BIG),
    )(dxo, gate, branch, out_w, *lins, y12, proj, pscale)


def _glu_bwd(dg12, ypre, proj, glu_wg, d_skip, tm):
    lx, sw = ypre.shape
    cb2 = glu_wg.shape[2]
    nb = lx // tm

    def body(dg_ref, yp_ref, u_ref, gw_ref, dk_ref, dyp_ref, dgb_ref, dd_ref, dgw_hbm, acc, sem):
        i = pl.program_id(0)

        @pl.when(i == 0)
        def _():
            acc[...] = jnp.zeros_like(acc)
            dgb_ref[...] = jnp.zeros_like(dgb_ref)
            dd_ref[...] = jnp.zeros_like(dd_ref)

        ypre = yp_ref[...]
        ygb = _gelu(ypre).astype(BF16)
        dg = dg_ref[...]
        dyg = jnp.zeros((tm, sw), F32)
        for j in range(NDEV):
            dgj = dg[:, j * cb2 : (j + 1) * cb2]
            dyg = dyg + _mm_nt(dgj, gw_ref[j])
            acc[j] += _mm_tn(ygb, dgj)
        dgb_ref[...] += _colsum(dg.astype(F32))
        dyp = dyg * _gelu_grad(ypre)
        dyp_ref[...] = dyp
        dd_ref[...] += _colsum(dyp * u_ref[...])

        @pl.when(i == nb - 1)
        def _():
            cp = pltpu.make_async_copy(acc, dgw_hbm, sem)
            cp.start()
            cp.wait()

    return pl.pallas_call(
        body,
        name="glu_bwd",
        grid=(nb,),
        in_specs=[_rows(tm, 2 * sw), _rows(tm, sw), _rows(tm, sw, 1), _whole(glu_wg.shape), _whole((1, sw))],
        out_specs=[_rows(tm, sw), _acc((1, 2 * sw)), _acc((1, sw)), _ANY],
        out_shape=[_sds((lx, sw)), _sds((1, 2 * sw)), _sds((1, sw)), _sds(glu_wg.shape)],
        scratch_shapes=[pltpu.VMEM(glu_wg.shape, F32), pltpu.SemaphoreType.DMA],
        compiler_params=_cparams(1, VMEM_BIG),
    )(dg12, ypre, proj, glu_wg, d_skip)


def _pack(parts, total):
    flat = jnp.concatenate([p.reshape(-1).astype(F32) for p in parts])
    return jnp.pad(flat, (0, total - flat.shape[0]))


def _unpack(flat, shapes):
    out, off = [], 0
    for s in shapes:
        n = int(np.prod(s)) if len(s) else 1
        out.append(flat[off : off + n].reshape(s))
        off += n
    return out


def kernel(x, c, ctx, c_ctx, ada_w, ada_b, norm_g, in_w, pool_w, pool_scale, s5_lam_re, s5_lam_im, s5_log_dt, s5_b_re, s5_b_im, s5_c_re, s5_c_im, s5_d, glu_w, glu_b, out_w, final_g, loss_target, m_c_ctx, m_ada_w, m_ada_b, m_norm_g, m_in_w, m_pool_w, m_pool_scale, m_s5_lam_re, m_s5_lam_im, m_s5_log_dt, m_s5_b_re, m_s5_b_im, m_s5_c_re, m_s5_c_im, m_s5_d, m_glu_w, m_glu_b, m_out_w, m_final_g, v_c_ctx, v_ada_w, v_ada_b, v_norm_g, v_in_w, v_pool_w, v_pool_scale, v_s5_lam_re, v_s5_lam_im, v_s5_log_dt, v_s5_b_re, v_s5_b_im, v_s5_c_re, v_s5_c_im, v_s5_d, v_glu_w, v_glu_b, v_out_w, v_final_g):
    xr, tgt, xc = x[0], loss_target[0], ctx[0]
    lx, d = xr.shape
    lc = xc.shape[0]
    mix = out_w.shape[1] * NDEV
    sw = glu_w.shape[1]
    pw = mix - sw
    pgw = pw // len(POOL_WINDOWS)
    ngrp = sw // SSM_H
    cb = in_w.shape[2]
    cb2 = glu_w.shape[2]
    nmod = ada_w.shape[2]
    ncc, ncl = lc // CHUNK_T, lx // CHUNK_T
    nch = ncc + ncl
    tm = min(256, lx)
    tm_acc = min(512, lx)
    assert mix == d and pw == sw and 2 * cb == pw and NDEV * cb2 == 2 * sw and lx % GRID_W == 0
    mx, my, mc = _my_pos()
    me = 4 * mx + 2 * my + mc

    (c_all,) = _all_gather([c], "gather_c")
    call = jnp.concatenate([c_all.reshape(NDEV, d), c_ctx.reshape(1, d), jnp.zeros((NDEV - 1, d), F32)], axis=0)
    ada_w_l = ada_w[0]
    ada_b_l = lax.dynamic_slice_in_dim(ada_b, me * nmod, nmod, axis=1)
    m_loc = _mod_fwd(call, ada_w_l, ada_b_l)
    (m_all,) = _all_gather([m_loc], "gather_mod")
    mod = lax.dynamic_index_in_dim(m_all, me, axis=1, keepdims=False).reshape(1, NDEV * nmod)
    mod_c = m_all[:, NDEV, :].reshape(1, NDEV * nmod)
    shift, scale, gate = mod[:, :d], mod[:, d : 2 * d], mod[:, 2 * d :]
    shift_c, scale_c = mod_c[:, :d], mod_c[:, d : 2 * d]

    s5_params = tuple(p[0] for p in (s5_lam_re, s5_lam_im, s5_log_dt, s5_b_re, s5_b_im, s5_c_re, s5_c_im))
    (small, a16), small_vjp = jax.vjp(_s5_small, *s5_params)
    gb_ops = min(8, ngrp)
    (tsum, wst, ccat), (in_wg, glu_wg, out_wg, pool_wg) = _s5_build(
        small, gb_ops, [in_w[0].astype(BF16), glu_w[0].astype(BF16), out_w[0].astype(BF16), pool_w[0].reshape(-1, pgw).astype(BF16)]
    )
    out_wf = out_wg.reshape(mix, d)
    pool_wf = pool_wg.reshape(NDEV, len(POOL_WINDOWS), pgw // NDEV, pgw).transpose(1, 0, 2, 3).reshape(len(POOL_WINDOWS), pgw, pgw)

    ssm_blocks = (2, 3)
    proj, hb = _in_proj(xr, norm_g, scale, shift, in_wg, tuple(range(NDEV)), tm, "in_proj")
    uc, hcb = _in_proj(xc, norm_g, scale_c, shift_c, in_wg, ssm_blocks, min(tm, lc), "in_proj_ctx")
    lins, dmats = [], []
    for gi, w in enumerate(POOL_WINDOWS):
        lin_g, dmat_g = _pool_fwd(proj, gi * pgw, w, pool_wf[gi], f"pool_fwd{gi}")
        lins.append(lin_g)
        dmats.append(dmat_g)

    gb_scan = min(16, ngrp)
    half = wst.shape[2] // 2
    perm = _chunk_perm()
    flat = lambda a: a.reshape(ngrp * nch, half)
    grp = lambda a: a.reshape(ngrp, nch, half)
    u_all, s_re, s_im = _s5_state(proj, pw // LANES, uc, perm, wst, ncc, ncl)
    h_re, h_im = _s5_scan_fwd(flat(s_re), flat(s_im), a16, ncc, nch, gb_scan)
    y_ssm = _s5_out(u_all, grp(h_re), grp(h_im), tsum, ccat, perm, ncc, lx)

    ypre, y12, branch, dxo, loss_l, dfg, dgate = _merge_fwd(
        y_ssm, proj, lins, xr, tgt, glu_wg, out_wf, s5_d, glu_b, pool_scale, gate, final_g.reshape(1, d), tm
    )

    dz, dlin, dg12, dps, d_out_w = _out_bwd(dxo, gate, branch, out_wf, lins, y12, proj, pool_scale, tm)
    dypre, dglu_b, dd_skip, d_glu_w = _glu_bwd(dg12, ypre, proj, glu_wg, s5_d, tm_acc)

    dy_all, dh_re, dh_im = _s5_dstate(dypre, perm, ccat, ncc)
    ds_re, ds_im = _s5_scan_bwd(flat(dh_re), flat(dh_im), a16, ncc, nch, gb_scan)
    du5, duc = _s5_du(dy_all, grp(ds_re), grp(ds_im), tsum, wst, perm, ncc, lx, lc)
    dwst, dtsum, dccat, da16 = _s5_op_grads(u_all, dy_all, grp(ds_re), grp(ds_im), grp(h_re), grp(h_im), ncc)
    dus, dpool_w = [], []
    for gi, w in enumerate(POOL_WINDOWS):
        du_g, dw_g = _pool_bwd(dlin, gi * pgw, w, pool_wf[gi], dmats[gi], f"pool_bwd{gi}")
        dus.append(du_g)
        dpool_w.append(dw_g)

    dpcb, dsc_c, dsh_c, dg_c = _in_bwd_ctx(xc, duc, in_wg, ssm_blocks, norm_g, scale_c, shift_c)
    grad_x, dpb, dsc, dsh, dnorm_g = _in_bwd(xr, dxo, dus, du5, dypre, dz, in_wg, norm_g, scale, shift, s5_d, dg_c, tm)
    d_in_w = _in_w_grad(hb, dpb, hcb, dpcb, cb, min(2 * tm_acc, lx))

    dmod = jnp.concatenate(
        [jnp.concatenate([dsh, dsc, dgate], axis=1), jnp.concatenate([dsh_c, dsc_c, jnp.zeros((1, d), F32)], axis=1)], axis=0
    )
    (dmod_all,) = _all_gather([dmod], "gather_dmod")
    dmod_l = lax.dynamic_slice_in_dim(dmod_all, me * nmod, nmod, axis=2)
    g_ada_w, g_ada_b_l, cctx_part = _mod_bwd(call, dmod_l[:, 0, :], dmod_l[:, 1, :], ada_w_l)

    npw = len(POOL_WINDOWS)
    dpool_s = jnp.stack(dpool_w).reshape(npw, NDEV, pgw // NDEV, pgw).transpose(1, 0, 2, 3).reshape(NDEV, npw * (pgw // NDEV), pgw)
    parts = _pair_partials([d_in_w, d_glu_w, d_out_w.reshape(NDEV, mix // NDEV, d), dpool_s], [BF16] * 4, "grads")
    dsmall, (land_in, land_glu, land_out, land_pool) = _s5_build_bwd(small, dtsum, dwst, dccat, gb_ops, parts)
    ds5 = small_vjp((tuple(dsmall), da16))

    small_names = ["norm_g", "pool_scale", "s5_lam_re", "s5_lam_im", "s5_log_dt", "s5_b_re", "s5_b_im", "s5_c_re", "s5_c_im", "s5_d", "glu_b", "final_g", "c_ctx"]
    small_w = dict(norm_g=norm_g, pool_scale=pool_scale, s5_lam_re=s5_lam_re, s5_lam_im=s5_lam_im, s5_log_dt=s5_log_dt, s5_b_re=s5_b_re, s5_b_im=s5_b_im, s5_c_re=s5_c_re, s5_c_im=s5_c_im, s5_d=s5_d, glu_b=glu_b, final_g=final_g, c_ctx=c_ctx)
    small_m = dict(norm_g=m_norm_g, pool_scale=m_pool_scale, s5_lam_re=m_s5_lam_re, s5_lam_im=m_s5_lam_im, s5_log_dt=m_s5_log_dt, s5_b_re=m_s5_b_re, s5_b_im=m_s5_b_im, s5_c_re=m_s5_c_re, s5_c_im=m_s5_c_im, s5_d=m_s5_d, glu_b=m_glu_b, final_g=m_final_g, c_ctx=m_c_ctx)
    small_v = dict(norm_g=v_norm_g, pool_scale=v_pool_scale, s5_lam_re=v_s5_lam_re, s5_lam_im=v_s5_lam_im, s5_log_dt=v_s5_log_dt, s5_b_re=v_s5_b_re, s5_b_im=v_s5_b_im, s5_c_re=v_s5_c_re, s5_c_im=v_s5_c_im, s5_d=v_s5_d, glu_b=v_glu_b, final_g=v_final_g, c_ctx=v_c_ctx)
    small_g = dict(norm_g=dnorm_g, pool_scale=dps, s5_lam_re=ds5[0], s5_lam_im=ds5[1], s5_log_dt=ds5[2], s5_b_re=ds5[3], s5_b_im=ds5[4], s5_c_re=ds5[5], s5_c_im=ds5[6], s5_d=dd_skip, glu_b=dglu_b, final_g=dfg, c_ctx=cctx_part)
    shapes = [small_w[k].shape for k in small_names]
    nsmall = sum(int(np.prod(s)) for s in shapes) + 1
    unit = NDEV * 8 * LANES
    tot = -(-nsmall // unit) * unit
    rows = tot // LANES
    gpack = _pack([small_g[k] for k in small_names] + [loss_l], tot).reshape(NDEV, rows // NDEV, LANES)
    (land_small,) = _reduce_scatter([gpack], [F32], "small")
    gsum = _sum_slots(land_small, "sum_small")
    (gall,) = _all_gather([gsum], "gather_small")
    (g_ada_b_all,) = _all_gather([g_ada_b_l], "gather_ada_b")
    gflat = gall.reshape(tot)
    loss = gflat[nsmall - 1]

    nab = NDEV * nmod
    tot2 = tot + -(-nab // (8 * LANES)) * (8 * LANES)
    g2 = jnp.concatenate([gflat, jnp.pad(g_ada_b_all.reshape(nab), (0, tot2 - tot - nab))]).reshape(tot2 // LANES, LANES)

    def pack2(dct, ab):
        return jnp.concatenate(
            [_pack([dct[k] for k in small_names] + [jnp.zeros((1,), F32)], tot), jnp.pad(ab.reshape(nab), (0, tot2 - tot - nab))]
        ).reshape(tot2 // LANES, LANES)

    _, d2, m2, v2 = _adam(g2, pack2(small_w, ada_b), pack2(small_m, m_ada_b), pack2(small_v, v_ada_b), "adam_small")

    def unpack2(flat2):
        flat2 = flat2.reshape(tot2)
        vals = dict(zip(small_names, _unpack(flat2, shapes)))
        vals["ada_b"] = flat2[tot : tot + nab].reshape(ada_b.shape)
        return vals

    res = {k: unpack2(a) for k, a in (("g", g2), ("d", d2), ("m", m2), ("v", v2))}

    def shard(name, g, land, w, m, v):
        shp = w.shape
        w2, m2_, v2_ = (a.reshape(-1, shp[-1]) for a in (w, m, v))
        out = _adam(g, w2, m2_, v2_, "adam_" + name, land=land)
        for k, a in zip(("g", "d", "m", "v"), out):
            res[k][name] = a.reshape(shp)

    shard("ada_w", g_ada_w, None, ada_w, m_ada_w, v_ada_w)
    shard("in_w", None, land_in, in_w, m_in_w, v_in_w)
    shard("pool_w", None, land_pool, pool_w, m_pool_w, v_pool_w)
    shard("glu_w", None, land_glu, glu_w, m_glu_w, v_glu_w)
    shard("out_w", None, land_out, out_w, m_out_w, v_out_w)

    names = ["c_ctx", "ada_w", "ada_b", "norm_g", "in_w", "pool_w", "pool_scale", "s5_lam_re", "s5_lam_im", "s5_log_dt", "s5_b_re", "s5_b_im", "s5_c_re", "s5_c_im", "s5_d", "glu_w", "glu_b", "out_w", "final_g"]
    return (loss, grad_x[None], *[res["g"][n] for n in names], *[res["d"][n] for n in names], *[res["m"][n] for n in names], *[res["v"][n] for n in names])
```

```python
import numpy as np

import jax
import jax.numpy as jnp
from jax import lax
from jax.experimental import pallas as pl
from jax.experimental.pallas import tpu as pltpu

F32 = jnp.float32
BF16 = jnp.bfloat16
NDEV = 8
EPS = 1e-6
GRID_W = 64
POOL_WINDOWS = (2, 4, 8, 16)
SSM_H = 16
CHUNK_T = 16
LANES = 128
ADAM_LR, ADAM_B1, ADAM_B2, ADAM_EPS, ADAM_WD, ADAM_STEP = 0.001, 0.9, 0.999, 1e-08, 0.01, 10
VMEM_BIG = 56 << 20
MESH_ID = pl.DeviceIdType.MESH

_HBM = pl.BlockSpec(memory_space=pltpu.HBM)
_ANY = pl.BlockSpec(memory_space=pl.ANY)
_SMEM = pl.BlockSpec(memory_space=pltpu.SMEM)


def _sds(shape, dtype=F32):
    return jax.ShapeDtypeStruct(tuple(shape), dtype)


def _cparams(ngrid=0, vmem=None):
    return pltpu.CompilerParams(
        dimension_semantics=("arbitrary",) * ngrid if ngrid else None, vmem_limit_bytes=vmem
    )


def _rows(tm, c, col=0):
    return pl.BlockSpec((tm, c), lambda i: (i, col))


def _whole(shape):
    nd = len(shape)
    return pl.BlockSpec(tuple(shape), lambda *_: (0,) * nd, pipeline_mode=pl.Buffered(1))


def _acc(shape):
    nd = len(shape)
    return pl.BlockSpec(tuple(shape), lambda *_: (0,) * nd)


def _mm(a, b):
    return jnp.dot(a, b, preferred_element_type=F32)


def _mm_nt(a, b):
    return lax.dot_general(a, b, (((1,), (1,)), ((), ())), preferred_element_type=F32)


def _mm_tn(a, b):
    return lax.dot_general(a, b, (((0,), (0,)), ((), ())), preferred_element_type=F32)


def _mm_split(k01, s):
    hi = s.astype(BF16)
    lo = (s - hi.astype(F32)).astype(BF16)
    return _mm(k01, hi) + _mm(k01, lo)


def _sigmoid(v):
    return 0.5 * (jnp.tanh(0.5 * v) + 1.0)


def _silu(v):
    return v * _sigmoid(v)


_GELU_K = 0.7978845608028654
_GELU_C = 0.044715


def _gelu(v):
    return 0.5 * v * (1.0 + jnp.tanh(_GELU_K * (v + _GELU_C * v * v * v)))


def _gelu_grad(v):
    th = jnp.tanh(_GELU_K * (v + _GELU_C * v * v * v))
    return 0.5 * (1.0 + th) + 0.5 * v * (1.0 - th * th) * (_GELU_K * (1.0 + 3.0 * _GELU_C * v * v))


def _colsum(v):
    return jnp.sum(v, axis=0, keepdims=True)


def _rowmean(v):
    return jnp.mean(v, axis=-1, keepdims=True)


NCHIP = NDEV // 2


def _my_pos():
    return lax.axis_index("x"), lax.axis_index("y"), lax.axis_index("c")


def _other_chips():
    x, y, _ = _my_pos()
    return [(1 - x, y), (x, 1 - y), (1 - x, 1 - y)]


def _remote(src, dst, send, recv, to):
    return pltpu.make_async_remote_copy(src, dst, send, recv, device_id=to, device_id_type=MESH_ID)


def _all_gather(arrs, name):
    n = len(arrs)

    def body(*refs):
        start, finish = _gather_copies(refs[:n], refs[n : 2 * n], *refs[2 * n :])
        start()
        finish()

    return pl.pallas_call(
        body, name=name, out_shape=_gather_shapes(arrs), in_specs=[_HBM] * n, out_specs=[_HBM] * n, scratch_shapes=_gather_sems(n)
    )(*arrs)


def _gather_shapes(arrs):
    return [_sds((NDEV,) + a.shape, a.dtype) for a in arrs]


def _gather_sems(n):
    return [pltpu.SemaphoreType.DMA((n, NDEV - 1)), pltpu.SemaphoreType.DMA((n, NDEV - 1)), pltpu.SemaphoreType.DMA((n,))]


def _gather_copies(ins, outs, send, recv, loc):
    n = len(ins)
    x, y, c = _my_pos()
    me, sib = (x, y, c), (x, y, 1 - c)
    chips = _other_chips()

    def slot(a, p):
        return outs[a].at[4 * p[0] + 2 * p[1] + p[2]]

    def copy(a, k, block, to, own=False):
        return _remote(ins[a] if own else slot(a, block), slot(a, block), send.at[a, k], recv.at[a, k], to)

    def mine():
        local = [pltpu.make_async_copy(ins[a], slot(a, me), loc.at[a]) for a in range(n)]
        sent = []
        for a in range(n):
            sent.append(copy(a, 0, me, sib, own=True))
            sent += [copy(a, 1 + j, me, (*chip, c), own=True) for j, chip in enumerate(chips)]
        return local, sent

    def start():
        local, sent = mine()
        for cp in local + sent:
            cp.start()

    def finish():
        local, sent = mine()
        for j, chip in enumerate(chips):
            for a in range(n):
                copy(a, 1 + j, (*chip, c), me).wait_recv()
                fwd = copy(a, 4 + j, (*chip, c), sib)
                fwd.start()
                sent.append(fwd)
        for a in range(n):
            copy(a, 0, sib, me).wait_recv()
            for j, chip in enumerate(chips):
                copy(a, 4 + j, (*chip, 1 - c), me).wait_recv()
        for cp in sent:
            cp.wait_send()
        for cp in local:
            cp.wait()

    return start, finish


def _sibling_swap(arrs, name):
    n = len(arrs)

    def body(*refs):
        ins, outs = refs[:n], refs[n : 2 * n]
        send, recv = refs[2 * n :]
        x, y, c = _my_pos()
        cps = [_remote(ins[a].at[:, 1 - c], outs[a], send.at[a], recv.at[a], (x, y, 1 - c)) for a in range(n)]
        for cp in cps:
            cp.start()
        for cp in cps:
            cp.wait()

    return pl.pallas_call(
        body,
        name=name,
        out_shape=[_sds((a.shape[0],) + a.shape[2:], a.dtype) for a in arrs],
        in_specs=[_HBM] * n,
        out_specs=[_HBM] * n,
        scratch_shapes=[pltpu.SemaphoreType.DMA((n,)), pltpu.SemaphoreType.DMA((n,))],
    )(*arrs)


def _pair_sum(arr, got, wire, name):
    _, _, r, c = arr.shape
    tr = _row_tile(r, c, budget=1 << 20)

    def body(a_ref, g_ref, o_ref):
        o_ref[0] = (a_ref[0, lax.axis_index("c")] + g_ref[0]).astype(wire)

    return pl.pallas_call(
        body,
        name=name,
        grid=(NCHIP, r // tr),
        in_specs=[pl.BlockSpec((1, 2, tr, c), lambda q, i: (q, 0, i, 0)), pl.BlockSpec((1, tr, c), lambda q, i: (q, i, 0))],
        out_specs=pl.BlockSpec((1, tr, c), lambda q, i: (q, i, 0)),
        out_shape=_sds((NCHIP, r, c), wire),
        compiler_params=_cparams(2),
    )(arr, got)


def _chip_exchange(arrs, name):
    n = len(arrs)

    def body(*refs):
        start, finish = _exchange_copies(refs[:n], refs[n : 2 * n], *refs[2 * n :])
        start()
        finish()

    return pl.pallas_call(
        body, name=name, out_shape=[_sds(a.shape, a.dtype) for a in arrs], in_specs=[_HBM] * n, out_specs=[_HBM] * n, scratch_shapes=_exchange_sems(n)
    )(*arrs)


def _exchange_sems(n):
    return [pltpu.SemaphoreType.DMA((n, NCHIP - 1)), pltpu.SemaphoreType.DMA((n, NCHIP - 1)), pltpu.SemaphoreType.DMA((n,))]


def _exchange_copies(ins, outs, send, recv, loc):
    n = len(ins)
    x, y, c = _my_pos()
    mine = 2 * x + y
    chips = _other_chips()

    def copies():
        local = [pltpu.make_async_copy(ins[a].at[mine], outs[a].at[mine], loc.at[a]) for a in range(n)]
        sent = [
            _remote(ins[a].at[2 * px + py], outs[a].at[mine], send.at[a, j], recv.at[a, j], (px, py, c))
            for a in range(n)
            for j, (px, py) in enumerate(chips)
        ]
        return local, sent

    def start():
        local, sent = copies()
        for cp in local + sent:
            cp.start()

    def finish():
        local, sent = copies()
        for a in range(n):
            for j, (px, py) in enumerate(chips):
                _remote(ins[a].at[mine], outs[a].at[2 * px + py], send.at[a, j], recv.at[a, j], (px, py, c)).wait_recv()
        for cp in sent:
            cp.wait_send()
        for cp in local:
            cp.wait()

    return start, finish


def _pair_partials(arrs, wires, tag):
    four = [a.reshape((NCHIP, 2) + a.shape[1:]) for a in arrs]
    got = _sibling_swap(four, "swap_" + tag)
    return [_pair_sum(a, g, w, f"pair_sum_{tag}{i}") for i, (a, g, w) in enumerate(zip(four, got, wires))]


def _reduce_scatter(arrs, wires, tag):
    return _chip_exchange(_pair_partials(arrs, wires, tag), "exchange_" + tag)


def _row_tile(r, c, budget=1 << 20):
    best = r
    for t in range(8, r, 8):
        if r % t == 0 and t * c * 4 <= budget:
            best = t
    if r * c * 4 <= budget:
        best = r
    return best


def _sum_chips(land):
    tot = land[0].astype(F32)
    for q in range(1, NCHIP):
        tot = tot + land[q].astype(F32)
    return tot


def _adam_math(w, g, m, v):
    m2 = ADAM_B1 * m + (1.0 - ADAM_B1) * g
    v2 = ADAM_B2 * v + (1.0 - ADAM_B2) * (g * g)
    mh = m2 / (1.0 - ADAM_B1**ADAM_STEP)
    vh = v2 / (1.0 - ADAM_B2**ADAM_STEP)
    delta = -ADAM_LR * (mh / (jnp.sqrt(vh) + ADAM_EPS) + ADAM_WD * w)
    return delta, m2, v2


def _adam(g, w, m, v, name, land=None):
    r, c = w.shape
    tr = _row_tile(r, c, budget=1 << 20)

    def body(*refs):
        if land is not None:
            l_ref, w_ref, m_ref, v_ref, g_ref, d_ref, m2_ref, v2_ref = refs
            gv = _sum_chips(l_ref)
            g_ref[...] = gv
        else:
            g_in, w_ref, m_ref, v_ref, d_ref, m2_ref, v2_ref = refs
            gv = g_in[...]
        d, m2, v2 = _adam_math(w_ref[...], gv, m_ref[...], v_ref[...])
        d_ref[...] = d
        m2_ref[...] = m2
        v2_ref[...] = v2

    blk = _rows(tr, c)
    if land is not None:
        in_specs = [pl.BlockSpec((NCHIP, tr, c), lambda i: (0, i, 0)), blk, blk, blk]
        out = pl.pallas_call(
            body, name=name, grid=(r // tr,), in_specs=in_specs, out_specs=[blk] * 4, out_shape=[_sds((r, c))] * 4, compiler_params=_cparams(1)
        )(land, w, m, v)
        return out
    out = pl.pallas_call(
        body, name=name, grid=(r // tr,), in_specs=[blk] * 4, out_specs=[blk] * 3, out_shape=[_sds((r, c))] * 3, compiler_params=_cparams(1)
    )(g, w, m, v)
    return (g,) + tuple(out)


def _sum_slots(land, name):
    _, r, c = land.shape
    tr = _row_tile(r, c, budget=1 << 20)

    def body(l_ref, o_ref):
        o_ref[...] = _sum_chips(l_ref)

    return pl.pallas_call(
        body,
        name=name,
        grid=(r // tr,),
        in_specs=[pl.BlockSpec((NCHIP, tr, c), lambda i: (0, i, 0))],
        out_specs=_rows(tr, c),
        out_shape=_sds((r, c)),
        compiler_params=_cparams(1),
    )(land)


def _mod_fwd(call, ada_w, ada_b_loc):
    def body(c_ref, w_ref, b_ref, o_ref):
        s = _silu(c_ref[...]).astype(BF16)
        o_ref[...] = _mm(s, w_ref[...].astype(BF16)) + b_ref[...]

    return pl.pallas_call(
        body, name="mod_fwd", out_shape=_sds((call.shape[0], ada_w.shape[1])), compiler_params=_cparams(0, 32 << 20)
    )(call, ada_w, ada_b_loc)


def _mod_bwd(call, dm_loc, dmc_loc, ada_w):
    d, n = ada_w.shape
    pad = call.shape[0] - NDEV - 1

    def body(c_ref, dm_ref, dmc_ref, w_ref, gw_ref, gb_ref, cp_ref):
        cv = c_ref[...]
        sg = _sigmoid(cv)
        dmc = _colsum(dmc_ref[...])
        dm = dm_ref[...]
        rows = jnp.concatenate([dm, dmc, jnp.zeros((pad, n), F32)], axis=0)
        gw_ref[...] = _mm_tn((cv * sg).astype(BF16), rows.astype(BF16))
        gb_ref[...] = _colsum(dm) + dmc
        back = _mm_nt(rows[NDEV:].astype(BF16), w_ref[...].astype(BF16))
        c8, s8 = cv[NDEV : NDEV + 1], sg[NDEV : NDEV + 1]
        cp_ref[...] = back[0:1] * (s8 * (1.0 + c8 * (1.0 - s8)))

    return pl.pallas_call(
        body, name="mod_bwd", out_shape=[_sds((d, n)), _sds((1, n)), _sds((1, d))], compiler_params=_cparams(0, 40 << 20)
    )(call, dm_loc, dmc_loc, ada_w)


def _riding(body, nin, nout, nr, steps, copies):
    def wrapped(*refs):
        start, finish = copies(refs[nin : nin + nr], refs[nin + nr + nout : nin + 2 * nr + nout], *refs[nin + 2 * nr + nout :])
        pl.when(pl.program_id(0) == 0)(start)
        body(*refs[:nin], *refs[nin + nr : nin + nr + nout])
        pl.when(pl.program_id(0) == steps - 1)(finish)

    return wrapped if nr else body


def _in_proj(xr, norm_g, scale, shift, wg, jsel, tm, name, shards=()):
    lx, d = xr.shape
    cb = wg.shape[2]
    nr = len(shards)

    def body(x_ref, g_ref, sc_ref, sh_ref, w_ref, p_ref, h_ref):
        xv = x_ref[...]
        r = lax.rsqrt(_rowmean(xv * xv) + EPS)
        hb = ((xv * r) * g_ref[...] * (1.0 + sc_ref[...]) + sh_ref[...]).astype(BF16)
        h_ref[...] = hb
        for q, j in enumerate(jsel):
            p_ref[:, q * cb : (q + 1) * cb] = _mm(hb, w_ref[j])

    vec = _whole((1, d))
    out = pl.pallas_call(
        _riding(body, 5, 2, nr, lx // tm, _gather_copies),
        name=name,
        grid=(lx // tm,),
        in_specs=[_rows(tm, d), vec, vec, vec, _whole(wg.shape)] + [_HBM] * nr,
        out_specs=[_rows(tm, len(jsel) * cb), _rows(tm, d)] + [_HBM] * nr,
        out_shape=[_sds((lx, len(jsel) * cb)), _sds((lx, d), BF16)] + _gather_shapes(shards),
        scratch_shapes=_gather_sems(nr) if nr else [],
        compiler_params=_cparams(1, VMEM_BIG),
    )(xr, norm_g, scale, shift, wg, *shards)
    return out[:2], out[2:]


def _in_bwd(xr, dxo, dus, du5, dypre, dz, wg, norm_g, scale, shift, d_skip, dg_init, tm, parts=()):
    lx, d = xr.shape
    nr = len(parts)
    cb = wg.shape[2]
    pgw = dus[0].shape[1]
    sw = du5.shape[1]
    mix = dz.shape[1]
    ncol = NDEV * cb

    def body(x_ref, dxo_ref, u0, u1, u2, u3, d5_ref, dy_ref, dz_ref, w_ref, g_ref, sc_ref, sh_ref, dk_ref, gi_ref,
             gx_ref, dp_ref, dsc_ref, dsh_ref, dg_ref):
        i = pl.program_id(0)

        @pl.when(i == 0)
        def _():
            dsc_ref[...] = jnp.zeros_like(dsc_ref)
            dsh_ref[...] = jnp.zeros_like(dsh_ref)
            dg_ref[...] = gi_ref[...]

        dp = jnp.concatenate(
            [u0[...], u1[...], u2[...], u3[...], d5_ref[...] + dk_ref[...] * dy_ref[...], dz_ref[...]], axis=1
        ).astype(BF16)
        dp_ref[...] = dp
        dh = _mm_nt(dp[:, 0:cb], w_ref[0])
        for j in range(1, NDEV):
            dh = dh + _mm_nt(dp[:, j * cb : (j + 1) * cb], w_ref[j])
        xv = x_ref[...]
        r = lax.rsqrt(_rowmean(xv * xv) + EPS)
        xh = xv * r
        g = g_ref[...]
        one_sc = 1.0 + sc_ref[...]
        dsh_ref[...] += _colsum(dh)
        dsc_ref[...] += _colsum(dh * (xh * g))
        dg_ref[...] += _colsum(dh * one_sc * xh)
        dxh = dh * one_sc * g
        gx_ref[...] = r * (dxh - xh * _rowmean(dxh * xh)) + dxo_ref[...]

    vec = _whole((1, d))
    out = pl.pallas_call(
        _riding(body, 15, 5, nr, lx // tm, _exchange_copies),
        name="in_bwd",
        grid=(lx // tm,),
        in_specs=[_rows(tm, d), _rows(tm, d)] + [_rows(tm, pgw)] * 4 + [_rows(tm, sw), _rows(tm, sw), _rows(tm, mix), _whole(wg.shape), vec, vec, vec, _whole((1, sw)), vec] + [_HBM] * nr,
        out_specs=[_rows(tm, d), _rows(tm, ncol), _acc((1, d)), _acc((1, d)), _acc((1, d))] + [_HBM] * nr,
        out_shape=[_sds((lx, d)), _sds((lx, ncol), BF16), _sds((1, d)), _sds((1, d)), _sds((1, d))] + [_sds(p.shape, p.dtype) for p in parts],
        scratch_shapes=_exchange_sems(nr) if nr else [],
        compiler_params=_cparams(1, VMEM_BIG),
    )(xr, dxo, *dus, du5, dypre, dz, wg, norm_g, scale, shift, d_skip, dg_init, *parts)
    return out[:5], out[5:]


def _in_bwd_ctx(xc, duc, wg, jsel, norm_g, scale, shift):
    lc, d = xc.shape
    cb = wg.shape[2]

    def body(x_ref, du_ref, w_ref, g_ref, sc_ref, sh_ref, dp_ref, dsc_ref, dsh_ref, dg_ref):
        dp = du_ref[...].astype(BF16)
        dp_ref[...] = dp
        dh = _mm_nt(dp[:, 0:cb], w_ref[jsel[0]])
        for q in range(1, len(jsel)):
            dh = dh + _mm_nt(dp[:, q * cb : (q + 1) * cb], w_ref[jsel[q]])
        xv = x_ref[...]
        xh = xv * lax.rsqrt(_rowmean(xv * xv) + EPS)
        dsh_ref[...] = _colsum(dh)
        dsc_ref[...] = _colsum(dh * (xh * g_ref[...]))
        dg_ref[...] = _colsum(dh * (1.0 + sc_ref[...]) * xh)

    return pl.pallas_call(
        body,
        name="in_bwd_ctx",
        out_shape=[_sds(duc.shape, BF16), _sds((1, d)), _sds((1, d)), _sds((1, d))],
        compiler_params=_cparams(0, VMEM_BIG),
    )(xc, duc, wg, norm_g, scale, shift)


def _in_w_grad(hb, dpb, hcb, dpcb, cb, tm):
    lx, d = hb.shape
    lc = hcb.shape[0]
    nb = lx // tm
    cg = 2 * cb

    def body(h_ref, dp_ref, hc_ref, dpc_ref, o_ref):
        j, i = pl.program_id(0), pl.program_id(1)

        @pl.when(i == 0)
        def _():
            o_ref[...] = jnp.zeros_like(o_ref)

        @pl.when(i < nb)
        def _():
            h = h_ref[...]
            o_ref[0] += _mm_tn(h, dp_ref[:, 0:cb])
            o_ref[1] += _mm_tn(h, dp_ref[:, cb:cg])

        @pl.when(jnp.logical_and(i == nb, j == 1))
        def _():
            h = hc_ref[...]
            o_ref[0] += _mm_tn(h, dpc_ref[:, 0:cb])
            o_ref[1] += _mm_tn(h, dpc_ref[:, cb:cg])

    return pl.pallas_call(
        body,
        name="in_w_grad",
        grid=(NDEV // 2, nb + 1),
        in_specs=[
            pl.BlockSpec((tm, d), lambda j, i: (jnp.minimum(i, nb - 1), 0)),
            pl.BlockSpec((tm, cg), lambda j, i: (jnp.minimum(i, nb - 1), j)),
            pl.BlockSpec((lc, d), lambda j, i: (0, 0)),
            pl.BlockSpec((lc, cg), lambda j, i: (0, 0)),
        ],
        out_specs=pl.BlockSpec((2, d, cb), lambda j, i: (j, 0, 0)),
        out_shape=_sds((NDEV, d, cb)),
        compiler_params=_cparams(2, VMEM_BIG),
    )(hb, dpb, hcb, dpcb)


def _pool_tables(w, rows, rb, pgw, transpose):
    t = np.arange(GRID_W)
    lo, hi = np.clip(t - w // 2, 0, GRID_W), np.clip(t + w - w // 2, 0, GRID_W)
    band = ((t[None, :] >= lo[:, None]) & (t[None, :] < hi[:, None])).astype(np.float32)
    if transpose:
        band = band.T
    kc = np.kron(np.eye(rb, dtype=np.float32), band)
    inv_c = np.tile((1.0 / (hi - lo).astype(np.float32))[:, None], (rb, pgw)).astype(np.float32)
    r = np.arange(rows)
    cnt_r = np.clip(r + w - w // 2, 0, rows) - np.clip(r - w // 2, 0, rows)
    inv_r = (1.0 / cnt_r.astype(np.float32)).astype(np.float32)
    return jnp.asarray(kc, BF16), jnp.asarray(inv_r), jnp.asarray(inv_c)


def _pool_fwd(proj, col0, w, pw_g, name):
    lx = proj.shape[0]
    pgw = pw_g.shape[0]
    rows = lx // GRID_W
    rb = min(4, rows)
    tok = rb * GRID_W
    lo = w // 2
    kc, inv_r, inv_c = _pool_tables(w, rows, rb, pgw, False)

    def body(p_hbm, w_ref, kc_ref, ir_ref, ic_ref, lin_ref, dm_ref, xp, sem):
        xp[pl.ds(0, lo * GRID_W), :] = jnp.zeros((lo * GRID_W, pgw), F32)
        xp[pl.ds((lo + rows) * GRID_W, (w - lo) * GRID_W), :] = jnp.zeros(((w - lo) * GRID_W, pgw), F32)
        cp = pltpu.make_async_copy(p_hbm.at[:, pl.ds(col0, pgw)], xp.at[pl.ds(lo * GRID_W, lx), :], sem)
        cp.start()
        cp.wait()

        def blk(b, carry):
            r0 = b * rb
            parts = []
            for rr in range(rb):
                base = pl.multiple_of((r0 + rr) * GRID_W, GRID_W)
                s = xp[pl.ds(base, GRID_W), :]
                for k in range(1, w):
                    s = s + xp[pl.ds(base + k * GRID_W, GRID_W), :]
                parts.append(s * ir_ref[r0 + rr])
            m = _mm_split(kc_ref[...], jnp.concatenate(parts, axis=0)) * ic_ref[...]
            u = xp[pl.ds(pl.multiple_of((r0 + lo) * GRID_W, GRID_W), tok), :]
            db = (m - u).astype(BF16)
            o0 = pl.multiple_of(r0 * GRID_W, GRID_W)
            dm_ref[pl.ds(o0, tok), :] = db
            lin_ref[pl.ds(o0, tok), :] = _mm(db, w_ref[...])
            return carry

        lax.fori_loop(0, rows // rb, blk, 0)

    return pl.pallas_call(
        body,
        name=name,
        in_specs=[_ANY, pl.BlockSpec(memory_space=pltpu.VMEM), pl.BlockSpec(memory_space=pltpu.VMEM), _SMEM, pl.BlockSpec(memory_space=pltpu.VMEM)],
        out_shape=[_sds((lx, pgw)), _sds((lx, pgw), BF16)],
        scratch_shapes=[pltpu.VMEM(((rows + w) * GRID_W, pgw), F32), pltpu.SemaphoreType.DMA],
        compiler_params=_cparams(0, VMEM_BIG),
    )(proj, pw_g, kc, inv_r, inv_c)


def _pool_bwd(dlin, col0, w, pw_g, dmat_g, name):
    lx = dlin.shape[0]
    pgw = pw_g.shape[0]
    rows = lx // GRID_W
    rb = min(4, rows)
    tok = rb * GRID_W
    front = w - w // 2 - 1
    kct, inv_r, inv_c = _pool_tables(w, rows, rb, pgw, True)

    def body(dl_hbm, w_ref, kc_ref, ir_ref, ic_ref, dm_ref, du_ref, dw_ref, tp, dl, sem):
        if front:
            tp[pl.ds(0, front * GRID_W), :] = jnp.zeros((front * GRID_W, pgw), F32)
        tp[pl.ds((front + rows) * GRID_W, (w - front) * GRID_W), :] = jnp.zeros(((w - front) * GRID_W, pgw), F32)
        cp = pltpu.make_async_copy(dl_hbm.at[:, pl.ds(col0, pgw)], dl, sem)
        cp.start()
        cp.wait()
        dw_ref[...] = jnp.zeros_like(dw_ref)

        def blk(b, carry):
            o0 = pl.multiple_of(b * tok, GRID_W)
            dlb = dl[pl.ds(o0, tok), :].astype(BF16)
            dw_ref[...] += _mm_tn(dm_ref[pl.ds(o0, tok), :], dlb)
            dd = _mm_nt(dlb, w_ref[...])
            du_ref[pl.ds(o0, tok), :] = -dd
            t = _mm_split(kc_ref[...], dd * ic_ref[...])
            for rr in range(rb):
                dst = pl.multiple_of((b * rb + rr + front) * GRID_W, GRID_W)
                tp[pl.ds(dst, GRID_W), :] = t[rr * GRID_W : (rr + 1) * GRID_W] * ir_ref[b * rb + rr]
            return carry

        lax.fori_loop(0, rows // rb, blk, 0)

        def rowl(r, carry):
            base = pl.multiple_of(r * GRID_W, GRID_W)
            s = tp[pl.ds(base, GRID_W), :]
            for k in range(1, w):
                s = s + tp[pl.ds(base + k * GRID_W, GRID_W), :]
            du_ref[pl.ds(base, GRID_W), :] += s
            return carry

        lax.fori_loop(0, rows, rowl, 0)

    vm = pl.BlockSpec(memory_space=pltpu.VMEM)
    return pl.pallas_call(
        body,
        name=name,
        in_specs=[_ANY, vm, vm, _SMEM, vm, vm],
        out_shape=[_sds((lx, pgw)), _sds((pgw, pgw))],
        scratch_shapes=[pltpu.VMEM(((rows + w) * GRID_W, pgw), F32), pltpu.VMEM((lx, pgw), F32), pltpu.SemaphoreType.DMA],
        compiler_params=_cparams(0, VMEM_BIG),
    )(dlin, pw_g, kct, inv_r, inv_c, dmat_g)


def _s5_small(lam_re, lam_im, log_dt, b_re, b_im, c_re, c_im):
    t = CHUNK_T
    dt = jnp.exp(log_dt)[..., None]
    zr, zi = lam_re * dt, lam_im * dt
    tau = jnp.arange(t + 1, dtype=F32)
    mag = jnp.exp(zr[..., None] * tau)
    pr, pi = mag * jnp.cos(zi[..., None] * tau), mag * jnp.sin(zi[..., None] * tau)
    ar, ai = pr[..., 1], pi[..., 1]
    den = lam_re * lam_re + lam_im * lam_im
    qr = ((ar - 1.0) * lam_re + ai * lam_im) / den
    qi = (ai * lam_re - (ar - 1.0) * lam_im) / den
    bbr = qr[..., None] * b_re - qi[..., None] * b_im
    bbi = qr[..., None] * b_im + qi[..., None] * b_re
    ctr, cti = jnp.swapaxes(c_re, 2, 3), jnp.swapaxes(c_im, 2, 3)
    lane_pad = lambda v: jnp.pad(jnp.swapaxes(v, 0, 1), ((0, 0), (0, 0), (0, 0), (0, LANES - v.shape[-1])))
    tp = lambda v: jnp.swapaxes(v, 1, 2)
    flip = lambda v: v[..., ::-1]
    pfr, pfi = tp(flip(pr[0, ..., :t])), tp(flip(pi[0, ..., :t]))
    pbr, pbi = tp(pr[1, ..., :t]), tp(pi[1, ..., :t])
    bfr, bfi, bbr_t, bbi_t = tp(bbr[0]), tp(bbi[0]), tp(bbr[1]), tp(bbi[1])
    a1 = jnp.concatenate([pfr, pbr, pfr, pbr], axis=-1)
    a2 = jnp.concatenate([-pfi, -pbi, pfi, pbi], axis=-1)
    b1 = jnp.concatenate([bfr, bbr_t, bfi, bbi_t], axis=-1)
    b2 = jnp.concatenate([bfi, bbi_t, bfr, bbr_t], axis=-1)
    a16 = jnp.concatenate([pr[0, ..., t], pr[1, ..., t], pi[0, ..., t], pi[1, ..., t]], axis=1)
    return (lane_pad(pr), lane_pad(pi), lane_pad(bbr), lane_pad(bbi), lane_pad(ctr), lane_pad(cti), a1, a2, b1, b2), a16


def _split2(v):
    hi = v.astype(BF16)
    return hi, (v - hi.astype(F32)).astype(BF16)


def _dotx(a, b, mm=_mm):
    ah, al = (a, None) if a.dtype == BF16 else _split2(a)
    bh, bl = (b, None) if b.dtype == BF16 else _split2(b)
    out = mm(ah, bh)
    if bl is not None:
        out = out + mm(ah, bl)
    if al is not None:
        out = out + mm(al, bh)
    return out


def _shift_lanes(v, k):
    if k == 0:
        return v
    n = v.shape[-1]
    lane = lax.broadcasted_iota(jnp.int32, v.shape, v.ndim - 1)
    r = pltpu.roll(v, k % n, axis=v.ndim - 1)
    return jnp.where(lane >= k, r, 0.0) if k > 0 else jnp.where(lane < n + k, r, 0.0)


def _op_consts():
    t, h = CHUNK_T, SSM_H
    j, o = np.divmod(np.arange(t * h), h)
    p = np.arange(LANES)[:, None]
    sel = lambda col: (p == col[None, :]).astype(np.float32)
    pw = np.stack([np.concatenate([sel(j), sel(j + 1)], axis=1), np.concatenate([sel(t - 1 - j), sel(t - j)], axis=1)])
    return jnp.asarray(pw, BF16), jnp.asarray(sel(o), BF16), jnp.asarray(sel(j).T, BF16), jnp.asarray(sel(o).T, BF16)


def _op_factors(gi, pr_ref, pi_ref, br_ref, bi_ref, cr_ref, ci_ref, pw_ref, ch_ref):
    n = pr_ref.shape[2]
    c16 = _dotx(jnp.concatenate([cr_ref[gi, 0], ci_ref[gi, 0], cr_ref[gi, 1], ci_ref[gi, 1]], axis=0), ch_ref[...])
    out = []
    for d in range(2):
        e = _dotx(jnp.concatenate([pr_ref[gi, d], pi_ref[gi, d]], axis=0), pw_ref[d])
        c = c16[2 * d * n : (2 * d + 1) * n], c16[(2 * d + 1) * n : (2 * d + 2) * n]
        bst = jnp.concatenate([br_ref[gi, d], -bi_ref[gi, d]], axis=0)
        out.append(((e[:n, :CW], e[n:, :CW]), (e[:n, CW:], e[n:, CW:]), c, bst))
    return out


def _op_spread(m_ref, u_ref, v_ref, gi):
    return _dotx(m_ref[...], jnp.concatenate([_rows_pad(u_ref[gi]), _rows_pad(v_ref[gi])], axis=1))


def _cmul(x, y):
    return x[0] * y[0] - x[1] * y[1], x[0] * y[1] + x[1] * y[0]


def _cmul_conj(g, y):
    return g[0] * y[0] + g[1] * y[1], g[1] * y[0] - g[0] * y[1]


def _rows_pad(v):
    return jnp.concatenate([v, jnp.zeros((LANES - v.shape[0], v.shape[1]), F32)], axis=0)


def _s5_build(small, gb, shards):
    pr, pi, br, bi, cr, ci, a1, a2, b1, b2 = small
    g, _, n, _ = pr.shape
    t, h, ns = CHUNK_T, SSM_H, 4 * n
    pw, ch, rep_s, til_s = _op_consts()
    nr, nin, steps = len(shards), 14, g // gb

    def body(*refs):
        pr_ref, pi_ref, br_ref, bi_ref, cr_ref, ci_ref, a1_ref, a2_ref, b1_ref, b2_ref, pw_ref, ch_ref, rs_ref, ts_ref = refs[:nin]
        t_ref, w_ref, c_ref = refs[nin + nr : nin + nr + 3]
        start, finish = _gather_copies(refs[nin : nin + nr], refs[nin + nr + 3 : nin + 2 * nr + 3], *refs[nin + 2 * nr + 3 :])
        pl.when(pl.program_id(0) == 0)(start)

        def group(gi, carry):
            rows, outs = [], []
            for e16, e1, c16, bst in _op_factors(gi, pr_ref, pi_ref, br_ref, bi_ref, cr_ref, ci_ref, pw_ref, ch_ref):
                rows.append(_dotx(bst, jnp.concatenate(_cmul(c16, e16), axis=0), _mm_tn)[:h])
                outs.append(_cmul(c16, e1))
            c_ref[gi] = jnp.concatenate([outs[0][0], outs[1][0], -outs[0][1], -outs[1][1]], axis=0).astype(BF16)
            for s in range(t):
                t_ref[gi, s * h : (s + 1) * h, :] = (_shift_lanes(rows[0], h * s) + _shift_lanes(rows[1], -h * (t - 1 - s))).astype(BF16)
            ae, be = _op_spread(rs_ref, a1_ref, a2_ref, gi), _op_spread(ts_ref, b1_ref, b2_ref, gi)
            w_ref[gi] = (ae[:, :ns] * be[:, :ns] + ae[:, ns:] * be[:, ns:]).astype(BF16)
            return carry

        lax.fori_loop(0, gb, group, 0)
        pl.when(pl.program_id(0) == steps - 1)(finish)

    sm = pl.BlockSpec((gb, 2, n, LANES), lambda i: (i, 0, 0, 0))
    ab = lambda v: pl.BlockSpec((gb,) + v.shape[1:], lambda i: (i, 0, 0))
    out = pl.pallas_call(
        body,
        name="s5_build",
        grid=(steps,),
        in_specs=[sm] * 6 + [ab(a1), ab(a2), ab(b1), ab(b2), _whole(pw.shape), _whole(ch.shape), _whole(rep_s.shape), _whole(til_s.shape)] + [_HBM] * nr,
        out_specs=[_gspec(gb, CW, CW), _gspec(gb, CW, 4 * n), _gspec(gb, 4 * n, CW)] + [_HBM] * nr,
        out_shape=[_sds((g, CW, CW), BF16), _sds((g, CW, 4 * n), BF16), _sds((g, 4 * n, CW), BF16)] + _gather_shapes(shards),
        scratch_shapes=_gather_sems(nr),
        compiler_params=_cparams(1, VMEM_BIG),
    )(pr, pi, br, bi, cr, ci, a1, a2, b1, b2, pw, ch, rep_s, til_s, *shards)
    return out[:3], out[3:]


def _s5_build_bwd(small, dtsum, dwst, dccat, gb, parts):
    pr, pi, br, bi, cr, ci, a1, a2, b1, b2 = small
    g, _, n, _ = pr.shape
    t, h, ns = CHUNK_T, SSM_H, 4 * n
    pw, ch, rep_s, til_s = _op_consts()
    nr, nin, nout, steps = len(parts), 17, 10, g // gb

    def body(*refs):
        pr_ref, pi_ref, br_ref, bi_ref, cr_ref, ci_ref, a1_ref, a2_ref, b1_ref, b2_ref, pw_ref, ch_ref, rs_ref, ts_ref, dt_ref, dw_ref, dc_ref = refs[:nin]
        dpr_ref, dpi_ref, dbr_ref, dbi_ref, dcr_ref, dci_ref, da1_ref, da2_ref, db1_ref, db2_ref = refs[nin + nr : nin + nr + nout]
        start, finish = _exchange_copies(refs[nin : nin + nr], refs[nin + nr + nout : nin + 2 * nr + nout], *refs[nin + 2 * nr + nout :])
        pl.when(pl.program_id(0) == 0)(start)

        def group(gi, carry):
            drow = [jnp.zeros((h, CW), F32), jnp.zeros((h, CW), F32)]
            for s in range(t):
                blk = dt_ref[gi, s * h : (s + 1) * h, :]
                drow[0] = drow[0] + _shift_lanes(blk, -h * s)
                drow[1] = drow[1] + _shift_lanes(blk, h * (t - 1 - s))
            dcc = dc_ref[gi]
            dcs = []
            for d, (e16, e1, c16, bst) in enumerate(_op_factors(gi, pr_ref, pi_ref, br_ref, bi_ref, cr_ref, ci_ref, pw_ref, ch_ref)):
                dr = _rows_pad(drow[d])
                db = _dotx(jnp.concatenate(_cmul(c16, e16), axis=0), dr, _mm_nt)
                dbr_ref[gi, d] = db[:n]
                dbi_ref[gi, d] = -db[n:]
                dst = _dotx(bst, dr)
                dca = dst[:n], dst[n:]
                gx = dcc[d * n : (d + 1) * n], -dcc[(2 + d) * n : (3 + d) * n]
                dc16, dc16b = _cmul_conj(dca, e16), _cmul_conj(gx, e1)
                dcs += [dc16[0] + dc16b[0], dc16[1] + dc16b[1]]
                de = jnp.concatenate([jnp.concatenate(_cmul_conj(dca, c16), axis=0), jnp.concatenate(_cmul_conj(gx, c16), axis=0)], axis=1)
                dp = _dotx(de, pw_ref[d], _mm_nt)
                dpr_ref[gi, d] = dp[:n]
                dpi_ref[gi, d] = dp[n:]
            dct = _dotx(jnp.concatenate(dcs, axis=0), ch_ref[...], _mm_nt)
            for d in range(2):
                dcr_ref[gi, d] = dct[2 * d * n : (2 * d + 1) * n]
                dci_ref[gi, d] = dct[(2 * d + 1) * n : (2 * d + 2) * n]
            dw = dw_ref[gi]
            ae, be = _op_spread(rs_ref, a1_ref, a2_ref, gi), _op_spread(ts_ref, b1_ref, b2_ref, gi)
            da = _dotx(rs_ref[...], jnp.concatenate([dw * be[:, :ns], dw * be[:, ns:]], axis=1), _mm_tn)
            db = _dotx(ts_ref[...], jnp.concatenate([dw * ae[:, :ns], dw * ae[:, ns:]], axis=1), _mm_tn)
            da1_ref[gi] = da[:t, :ns]
            da2_ref[gi] = da[:t, ns:]
            db1_ref[gi] = db[:h, :ns]
            db2_ref[gi] = db[:h, ns:]
            return carry

        lax.fori_loop(0, gb, group, 0)
        pl.when(pl.program_id(0) == steps - 1)(finish)

    sm = pl.BlockSpec((gb, 2, n, LANES), lambda i: (i, 0, 0, 0))
    ab = lambda v: pl.BlockSpec((gb,) + v.shape[1:], lambda i: (i, 0, 0))
    ops = [_gspec(gb, CW, CW), _gspec(gb, CW, 4 * n), _gspec(gb, 4 * n, CW)]
    out = pl.pallas_call(
        body,
        name="s5_build_bwd",
        grid=(steps,),
        in_specs=[sm] * 6 + [ab(a1), ab(a2), ab(b1), ab(b2), _whole(pw.shape), _whole(ch.shape), _whole(rep_s.shape), _whole(til_s.shape)] + ops + [_HBM] * nr,
        out_specs=[sm] * 6 + [ab(a1), ab(a2), ab(b1), ab(b2)] + [_HBM] * nr,
        out_shape=[_sds(v.shape) for v in small] + [_sds(p.shape, p.dtype) for p in parts],
        scratch_shapes=_exchange_sems(nr),
        compiler_params=_cparams(1, VMEM_BIG),
    )(pr, pi, br, bi, cr, ci, a1, a2, b1, b2, pw, ch, rep_s, til_s, dtsum, dwst, dccat, *parts)
    return out[:nout], out[nout:]


SCAN_UNROLL = 4
GPL = LANES // SSM_H
CW = CHUNK_T * SSM_H
BW = CHUNK_T * LANES


def _chunk_perm():
    o = np.arange(BW)
    src = ((o % CW) // SSM_H) * LANES + (o // CW) * SSM_H + o % SSM_H
    return jnp.asarray(np.arange(BW)[:, None] == src[None, :], BF16)


def _gspec(gb, a, b):
    return pl.BlockSpec((gb, a, b), lambda i: (i, 0, 0))


def _chunk_rows(ref, n):
    return jnp.concatenate([ref[pl.ds(s, n, stride=CHUNK_T), :] for s in range(CHUNK_T)], axis=1)


def _s5_state(proj, col0, uc, perm, wst, ncc, ncl):
    nlb = uc.shape[1] // LANES
    g, _, ns = wst.shape
    nch = ncc + ncl
    half = ns // 2

    def body(p_ref, c_ref, pm_ref, w_ref, u_ref, sr_ref, si_ref):
        xrows = jnp.concatenate([_chunk_rows(c_ref, ncc), _chunk_rows(p_ref, ncl)], axis=0).astype(BF16)
        u = _mm(xrows, pm_ref[...]).astype(BF16)
        u_ref[0] = u
        for gi in range(GPL):
            s = _mm(u[:, gi * CW : (gi + 1) * CW], w_ref[gi].astype(BF16))
            sr_ref[gi] = s[:, :half]
            si_ref[gi] = s[:, half:]

    return pl.pallas_call(
        body,
        name="s5_state",
        grid=(nlb,),
        in_specs=[
            pl.BlockSpec((ncl * CHUNK_T, LANES), lambda i: (0, col0 + i)),
            pl.BlockSpec((ncc * CHUNK_T, LANES), lambda i: (0, i)),
            _whole(perm.shape),
            _gspec(GPL, CW, ns),
        ],
        out_specs=[pl.BlockSpec((1, nch, BW), lambda i: (i, 0, 0)), _gspec(GPL, nch, half), _gspec(GPL, nch, half)],
        out_shape=[_sds((nlb, nch, BW), BF16), _sds((g, nch, half)), _sds((g, nch, half))],
        compiler_params=_cparams(1, VMEM_BIG),
    )(proj, uc, perm, wst)


def _unrolled_loop(n, step, init):
    u = SCAN_UNROLL if n % SCAN_UNROLL == 0 else 1

    def trip(i, c):
        for k in range(u):
            c = step(i * u + k, c)
        return c

    return lax.fori_loop(0, n // u, trip, init)


def _lane_masks(gb, half):
    lane = lax.broadcasted_iota(jnp.int32, (gb, half), 1)
    return lane < (half // 2)


def _s5_scan_fwd(s_re, s_im, a16, ncc, nch, gb):
    g, ns = a16.shape
    half = ns // 2

    def body(sr_ref, si_ref, a_ref, hr_ref, hi_ref):
        ar, ai = a_ref[:, :half], a_ref[:, half:]
        mf = _lane_masks(gb, half)
        zero = jnp.zeros((gb, half), F32)
        hr_ref[...] = jnp.zeros_like(hr_ref)
        hi_ref[...] = jnp.zeros_like(hi_ref)

        def step(t, c):
            fr, fi, br, bi = c
            rf = pl.ds(t, gb, stride=nch)
            rb = pl.ds(jnp.where(t < ncc, ncc - 1 - t, nch - 1 - (t - ncc)), gb, stride=nch)
            hr_ref[rf, :] += jnp.where(mf, fr, 0.0)
            hi_ref[rf, :] += jnp.where(mf, fi, 0.0)
            hr_ref[rb, :] += jnp.where(mf, 0.0, br)
            hi_ref[rb, :] += jnp.where(mf, 0.0, bi)
            return (
                ar * fr - ai * fi + sr_ref[rf, :],
                ar * fi + ai * fr + si_ref[rf, :],
                ar * br - ai * bi + sr_ref[rb, :],
                ar * bi + ai * br + si_ref[rb, :],
            )

        _unrolled_loop(nch, step, (zero, zero, zero, zero))

    blk = pl.BlockSpec((gb * nch, half), lambda i: (i, 0))
    return pl.pallas_call(
        body,
        name="s5_scan_fwd",
        grid=(g // gb,),
        in_specs=[blk, blk, pl.BlockSpec((gb, ns), lambda i: (i, 0))],
        out_specs=[blk, blk],
        out_shape=[_sds((g * nch, half))] * 2,
        compiler_params=_cparams(1, VMEM_BIG),
    )(s_re, s_im, a16)


def _token_rows_store(ref, val, n):
    for s in range(CHUNK_T):
        ref[pl.ds(s, n, stride=CHUNK_T), :] = val[:, s * LANES : (s + 1) * LANES]


def _s5_out(u_all, h_re, h_im, tsum, ccat, perm, ncc, lx):
    nlb, nch, _ = u_all.shape
    g, ns, _ = ccat.shape
    ncl = nch - ncc
    half = ns // 2

    def body(u_ref, hr_ref, hi_ref, t_ref, c_ref, pm_ref, y_ref):
        parts = []
        for gi in range(GPL):
            u = u_ref[0, ncc:, gi * CW : (gi + 1) * CW]
            hs = jnp.concatenate([hr_ref[gi, ncc:, :], hi_ref[gi, ncc:, :]], axis=1).astype(BF16)
            parts.append(_mm(u, t_ref[gi].astype(BF16)) + _mm(hs, c_ref[gi].astype(BF16)))
        y = jnp.concatenate(parts, axis=1)
        hi = y.astype(BF16)
        lo = (y - hi.astype(F32)).astype(BF16)
        _token_rows_store(y_ref, _mm_nt(hi, pm_ref[...]) + _mm_nt(lo, pm_ref[...]), ncl)

    return pl.pallas_call(
        body,
        name="s5_out",
        grid=(nlb,),
        in_specs=[pl.BlockSpec((1, nch, BW), lambda i: (i, 0, 0)), _gspec(GPL, nch, half), _gspec(GPL, nch, half), _gspec(GPL, CW, CW), _gspec(GPL, ns, CW), _whole(perm.shape)],
        out_specs=pl.BlockSpec((lx, LANES), lambda i: (0, i)),
        out_shape=_sds((lx, nlb * LANES)),
        compiler_params=_cparams(1, VMEM_BIG),
    )(u_all, h_re, h_im, tsum, ccat, perm)


def _s5_dstate(dypre, perm, ccat, ncc):
    lx, sw = dypre.shape
    nlb = sw // LANES
    g, ns, _ = ccat.shape
    ncl = lx // CHUNK_T
    nch = ncl + ncc
    half = ns // 2

    def body(dy_ref, pm_ref, c_ref, dyr_ref, dr_ref, di_ref):
        dy = _mm(_chunk_rows(dy_ref, ncl).astype(BF16), pm_ref[...]).astype(BF16)
        dyr_ref[0] = dy
        for gi in range(GPL):
            dh = _mm_nt(dy[:, gi * CW : (gi + 1) * CW], c_ref[gi].astype(BF16))
            dr_ref[gi, :ncc, :] = jnp.zeros((ncc, half), F32)
            di_ref[gi, :ncc, :] = jnp.zeros((ncc, half), F32)
            dr_ref[gi, ncc:, :] = dh[:, :half]
            di_ref[gi, ncc:, :] = dh[:, half:]

    return pl.pallas_call(
        body,
        name="s5_dstate",
        grid=(nlb,),
        in_specs=[pl.BlockSpec((lx, LANES), lambda i: (0, i)), _whole(perm.shape), _gspec(GPL, ns, CW)],
        out_specs=[pl.BlockSpec((1, ncl, BW), lambda i: (i, 0, 0)), _gspec(GPL, nch, half), _gspec(GPL, nch, half)],
        out_shape=[_sds((nlb, ncl, BW), BF16), _sds((g, nch, half)), _sds((g, nch, half))],
        compiler_params=_cparams(1, VMEM_BIG),
    )(dypre, perm, ccat)


def _s5_scan_bwd(dh_re, dh_im, a16, ncc, nch, gb):
    g, ns = a16.shape
    half = ns // 2
    ncl = nch - ncc

    def body(dr_ref, di_ref, a_ref, sr_ref, si_ref):
        ar, ai = a_ref[:, :half], a_ref[:, half:]
        mf = _lane_masks(gb, half)
        zero = jnp.zeros((gb, half), F32)
        sr_ref[...] = jnp.zeros_like(sr_ref)
        si_ref[...] = jnp.zeros_like(si_ref)

        def step(t, c):
            lfr, lfi, lbr, lbi = c
            pf = pl.ds(nch - 1 - t, gb, stride=nch)
            pb = pl.ds(jnp.where(t < ncl, ncc + t, t - ncl), gb, stride=nch)
            sr_ref[pf, :] += jnp.where(mf, lfr, 0.0)
            si_ref[pf, :] += jnp.where(mf, lfi, 0.0)
            sr_ref[pb, :] += jnp.where(mf, 0.0, lbr)
            si_ref[pb, :] += jnp.where(mf, 0.0, lbi)
            return (
                dr_ref[pf, :] + ar * lfr + ai * lfi,
                di_ref[pf, :] + ar * lfi - ai * lfr,
                dr_ref[pb, :] + ar * lbr + ai * lbi,
                di_ref[pb, :] + ar * lbi - ai * lbr,
            )

        _unrolled_loop(nch, step, (zero,) * 4)

    blk = pl.BlockSpec((gb * nch, half), lambda i: (i, 0))
    return pl.pallas_call(
        body,
        name="s5_scan_bwd",
        grid=(g // gb,),
        in_specs=[blk, blk, pl.BlockSpec((gb, ns), lambda i: (i, 0))],
        out_specs=[blk, blk],
        out_shape=[_sds((g * nch, half))] * 2,
        compiler_params=_cparams(1, VMEM_BIG),
    )(dh_re, dh_im, a16)


def _s5_du(dy_all, ds_re, ds_im, tsum, wst, perm, ncc, lx, lc):
    nlb, ncl, _ = dy_all.shape
    g, _, ns = wst.shape
    nch = ncc + ncl
    half = ns // 2

    def body(dy_ref, sr_ref, si_ref, t_ref, w_ref, pm_ref, du_ref, dc_ref):
        parts = []
        for gi in range(GPL):
            ds = jnp.concatenate([sr_ref[gi], si_ref[gi]], axis=1).astype(BF16)
            d_all = _mm_nt(ds, w_ref[gi].astype(BF16))
            d_lat = d_all[ncc:] + _mm_nt(dy_ref[0, :, gi * CW : (gi + 1) * CW], t_ref[gi].astype(BF16))
            parts.append(jnp.concatenate([d_all[:ncc], d_lat], axis=0))
        du = _mm_nt(jnp.concatenate(parts, axis=1).astype(BF16), pm_ref[...])
        _token_rows_store(dc_ref, du[:ncc], ncc)
        _token_rows_store(du_ref, du[ncc:], ncl)

    return pl.pallas_call(
        body,
        name="s5_du",
        grid=(nlb,),
        in_specs=[pl.BlockSpec((1, ncl, BW), lambda i: (i, 0, 0)), _gspec(GPL, nch, half), _gspec(GPL, nch, half), _gspec(GPL, CW, CW), _gspec(GPL, CW, ns), _whole(perm.shape)],
        out_specs=[pl.BlockSpec((lx, LANES), lambda i: (0, i)), pl.BlockSpec((lc, LANES), lambda i: (0, i))],
        out_shape=[_sds((lx, nlb * LANES)), _sds((lc, nlb * LANES))],
        compiler_params=_cparams(1, VMEM_BIG),
    )(dy_all, ds_re, ds_im, tsum, wst, perm)


def _s5_op_grads(u_all, dy_all, ds_re, ds_im, h_re, h_im, ncc):
    nlb, nch, _ = u_all.shape
    g, _, half = ds_re.shape
    ns = 2 * half

    def body(u_ref, dy_ref, sr_ref, si_ref, hr_ref, hi_ref, dw_ref, dt_ref, dc_ref, da_ref):
        das = []
        for gi in range(GPL):
            cols = slice(gi * CW, (gi + 1) * CW)
            sr, si, hr, hi = sr_ref[gi], si_ref[gi], hr_ref[gi], hi_ref[gi]
            ds = jnp.concatenate([sr, si], axis=1).astype(BF16)
            dy = dy_ref[0, :, cols]
            dw_ref[gi] = _mm_tn(u_ref[0, :, cols], ds)
            dt_ref[gi] = _mm_tn(u_ref[0, ncc:, cols], dy)
            hs = jnp.concatenate([hr[ncc:], hi[ncc:]], axis=1).astype(BF16)
            dc_ref[gi] = _mm_tn(hs, dy)
            das.append(jnp.concatenate([_colsum(hr * sr + hi * si), _colsum(hr * si - hi * sr)], axis=1))
        da_ref[...] = jnp.concatenate(das, axis=0)

    return pl.pallas_call(
        body,
        name="s5_op_grads",
        grid=(nlb,),
        in_specs=[pl.BlockSpec((1, nch, BW), lambda i: (i, 0, 0)), pl.BlockSpec((1, nch - ncc, BW), lambda i: (i, 0, 0))] + [_gspec(GPL, nch, half)] * 4,
        out_specs=[_gspec(GPL, CW, ns), _gspec(GPL, CW, CW), _gspec(GPL, ns, CW), pl.BlockSpec((GPL, ns), lambda i: (i, 0))],
        out_shape=[_sds((g, CW, ns)), _sds((g, CW, CW)), _sds((g, ns, CW)), _sds((g, ns))],
        compiler_params=_cparams(1, VMEM_BIG),
    )(u_all, dy_all, ds_re, ds_im, h_re, h_im)


def _merge_fwd(y_ssm, proj, lins, x, tgt, glu_wg, out_w, d_skip, glu_b, pscale, gate, fin_g, tm):
    lx, d = x.shape
    sw = y_ssm.shape[1]
    mix = out_w.shape[0]
    pw = mix - sw
    pgw = lins[0].shape[1]
    cb2 = glu_wg.shape[2]
    nh = NDEV // 2

    def body(y_ref, u_ref, z_ref, l0, l1, l2, l3, x_ref, t_ref, gw_ref, ow_ref, dk_ref, gb_ref, ps_ref, gt_ref, fg_ref,
             ypre_ref, y12_ref, br_ref, dxo_ref, loss_ref, dfg_ref, dgt_ref):
        @pl.when(pl.program_id(0) == 0)
        def _():
            loss_ref[...] = jnp.zeros_like(loss_ref)
            dfg_ref[...] = jnp.zeros_like(dfg_ref)
            dgt_ref[...] = jnp.zeros_like(dgt_ref)

        ypre = y_ref[...] + dk_ref[...] * u_ref[...]
        ypre_ref[...] = ypre
        yg = _gelu(ypre).astype(BF16)
        outs = []
        for j in range(nh):
            y1 = _mm(yg, gw_ref[j]) + gb_ref[:, j * cb2 : (j + 1) * cb2]
            y2 = _mm(yg, gw_ref[nh + j]) + gb_ref[:, (nh + j) * cb2 : (nh + j + 1) * cb2]
            y12_ref[:, j * cb2 : (j + 1) * cb2] = y1
            y12_ref[:, (nh + j) * cb2 : (nh + j + 1) * cb2] = y2
            outs.append(y1 * _sigmoid(y2))
        lin = jnp.concatenate([l0[...], l1[...], l2[...], l3[...]], axis=1) * ps_ref[...]
        brb = (jnp.concatenate([lin] + outs, axis=1) * _silu(z_ref[...])).astype(BF16)
        br_ref[...] = brb
        mixv = _mm(brb, ow_ref[...])
        xo = x_ref[...] + gt_ref[...] * mixv
        r2 = lax.rsqrt(_rowmean(xo * xo) + EPS)
        xh = xo * r2
        fg = fg_ref[...]
        e = xh * fg - t_ref[...]
        loss_ref[...] += 0.5 * _colsum(_rowmean(e * e))
        dy = e * (1.0 / d)
        dfg_ref[...] += _colsum(dy * xh)
        gy = dy * fg
        dxo = r2 * (gy - xh * _rowmean(gy * xh))
        dxo_ref[...] = dxo
        dgt_ref[...] += _colsum(dxo * mixv)

    vec = _whole((1, d))
    return pl.pallas_call(
        body,
        name="merge_fwd",
        grid=(lx // tm,),
        in_specs=[_rows(tm, sw), _rows(tm, sw, 1), _rows(tm, mix, 1)] + [_rows(tm, pgw)] * 4 + [_rows(tm, d), _rows(tm, d), _whole(glu_wg.shape), _whole(out_w.shape), _whole((1, sw)), _whole((1, 2 * sw)), _whole((1, pw)), vec, vec],
        out_specs=[_rows(tm, sw), _rows(tm, 2 * sw), _rows(tm, mix), _rows(tm, d), _acc((1, 1)), _acc((1, d)), _acc((1, d))],
        out_shape=[_sds((lx, sw)), _sds((lx, 2 * sw)), _sds((lx, mix), BF16), _sds((lx, d)), _sds((1, 1)), _sds((1, d)), _sds((1, d))],
        compiler_params=_cparams(1, VMEM_BIG),
    )(y_ssm, proj, proj, *lins, x, tgt, glu_wg, out_w, d_skip, glu_b, pscale, gate, fin_g)


def _out_bwd(dxo, gate, branch, out_w, lins, y12, proj, pscale, tm):
    lx, d = dxo.shape
    mix = out_w.shape[0]
    pgw = lins[0].shape[1]
    pw = 4 * pgw
    sw = mix - pw
    nb = lx // tm

    def body(dxo_ref, gt_ref, br_ref, ow_ref, l0, l1, l2, l3, y_ref, z_ref, ps_ref, dz_ref, dl_ref, dg_ref, dps_ref, dow_hbm, acc, sem):
        i = pl.program_id(0)

        @pl.when(i == 0)
        def _():
            acc[...] = jnp.zeros_like(acc)
            dps_ref[...] = jnp.zeros_like(dps_ref)

        dmix = (dxo_ref[...] * gt_ref[...]).astype(BF16)
        dbr = _mm_nt(dmix, ow_ref[...])
        acc[...] += _mm_tn(br_ref[...], dmix)
        z = z_ref[...]
        sg = _sigmoid(z)
        dbp = dbr * (z * sg)
        y1, y2 = y_ref[:, :sw], y_ref[:, sw:]
        s2 = _sigmoid(y2)
        ps = ps_ref[...]
        lin = jnp.concatenate([l0[...], l1[...], l2[...], l3[...]], axis=1)
        bp = jnp.concatenate([lin * ps, y1 * s2], axis=1)
        dz_ref[...] = dbr * bp * (sg * (1.0 + z * (1.0 - sg)))
        dlp = dbp[:, :pw]
        dl_ref[...] = dlp * ps
        dps_ref[...] += _colsum(dlp * lin)
        dss = dbp[:, pw:]
        dg_ref[...] = jnp.concatenate([dss * s2, dss * y1 * s2 * (1.0 - s2)], axis=1).astype(BF16)

        @pl.when(i == nb - 1)
        def _():
            cp = pltpu.make_async_copy(acc, dow_hbm, sem)
            cp.start()
            cp.wait()

    return pl.pallas_call(
        body,
        name="out_bwd",
        grid=(nb,),
        in_specs=[_rows(tm, d), _whole((1, d)), _rows(tm, mix), _whole(out_w.shape)] + [_rows(tm, pgw)] * 4 + [_rows(tm, 2 * sw), _rows(tm, mix, 1), _whole((1, pw))],
        out_specs=[_rows(tm, mix), _rows(tm, pw), _rows(tm, 2 * sw), _acc((1, pw)), _ANY],
        out_shape=[_sds((lx, mix)), _sds((lx, pw)), _sds((lx, 2 * sw), BF16), _sds((1, pw)), _sds((mix, d))],
        scratch_shapes=[pltpu.VMEM((mix, d), F32), pltpu.SemaphoreType.DMA],
        compiler_params=_cparams(1, VMEM_BIG),
    )(dxo, gate, branch, out_w, *lins, y12, proj, pscale)


def _glu_bwd(dg12, ypre, proj, glu_wg, d_skip, tm):
    lx, sw = ypre.shape
    cb2 = glu_wg.shape[2]
    nb = lx // tm

    def body(dg_ref, yp_ref, u_ref, gw_ref, dk_ref, dyp_ref, dgb_ref, dd_ref, dgw_hbm, acc, sem):
        i = pl.program_id(0)

        @pl.when(i == 0)
        def _():
            acc[...] = jnp.zeros_like(acc)
            dgb_ref[...] = jnp.zeros_like(dgb_ref)
            dd_ref[...] = jnp.zeros_like(dd_ref)

        ypre = yp_ref[...]
        ygb = _gelu(ypre).astype(BF16)
        dg = dg_ref[...]
        dyg = jnp.zeros((tm, sw), F32)
        for j in range(NDEV):
            dgj = dg[:, j * cb2 : (j + 1) * cb2]
            dyg = dyg + _mm_nt(dgj, gw_ref[j])
            acc[j] += _mm_tn(ygb, dgj)
        dgb_ref[...] += _colsum(dg.astype(F32))
        dyp = dyg * _gelu_grad(ypre)
        dyp_ref[...] = dyp
        dd_ref[...] += _colsum(dyp * u_ref[...])

        @pl.when(i == nb - 1)
        def _():
            cp = pltpu.make_async_copy(acc, dgw_hbm, sem)
            cp.start()
            cp.wait()

    return pl.pallas_call(
        body,
        name="glu_bwd",
        grid=(nb,),
        in_specs=[_rows(tm, 2 * sw), _rows(tm, sw), _rows(tm, sw, 1), _whole(glu_wg.shape), _whole((1, sw))],
        out_specs=[_rows(tm, sw), _acc((1, 2 * sw)), _acc((1, sw)), _ANY],
        out_shape=[_sds((lx, sw)), _sds((1, 2 * sw)), _sds((1, sw)), _sds(glu_wg.shape)],
        scratch_shapes=[pltpu.VMEM(glu_wg.shape, F32), pltpu.SemaphoreType.DMA],
        compiler_params=_cparams(1, VMEM_BIG),
    )(dg12, ypre, proj, glu_wg, d_skip)


def _pack(parts, total):
    flat = jnp.concatenate([p.reshape(-1).astype(F32) for p in parts])
    return jnp.pad(flat, (0, total - flat.shape[0]))


def _unpack(flat, shapes):
    out, off = [], 0
    for s in shapes:
        n = int(np.prod(s)) if len(s) else 1
        out.append(flat[off : off + n].reshape(s))
        off += n
    return out


def kernel(x, c, ctx, c_ctx, ada_w, ada_b, norm_g, in_w, pool_w, pool_scale, s5_lam_re, s5_lam_im, s5_log_dt, s5_b_re, s5_b_im, s5_c_re, s5_c_im, s5_d, glu_w, glu_b, out_w, final_g, loss_target, m_c_ctx, m_ada_w, m_ada_b, m_norm_g, m_in_w, m_pool_w, m_pool_scale, m_s5_lam_re, m_s5_lam_im, m_s5_log_dt, m_s5_b_re, m_s5_b_im, m_s5_c_re, m_s5_c_im, m_s5_d, m_glu_w, m_glu_b, m_out_w, m_final_g, v_c_ctx, v_ada_w, v_ada_b, v_norm_g, v_in_w, v_pool_w, v_pool_scale, v_s5_lam_re, v_s5_lam_im, v_s5_log_dt, v_s5_b_re, v_s5_b_im, v_s5_c_re, v_s5_c_im, v_s5_d, v_glu_w, v_glu_b, v_out_w, v_final_g):
    xr, tgt, xc = x[0], loss_target[0], ctx[0]
    lx, d = xr.shape
    lc = xc.shape[0]
    mix = out_w.shape[1] * NDEV
    sw = glu_w.shape[1]
    pw = mix - sw
    pgw = pw // len(POOL_WINDOWS)
    ngrp = sw // SSM_H
    cb = in_w.shape[2]
    cb2 = glu_w.shape[2]
    nmod = ada_w.shape[2]
    ncc, ncl = lc // CHUNK_T, lx // CHUNK_T
    nch = ncc + ncl
    tm = min(256, lx)
    tm_acc = min(512, lx)
    assert mix == d and pw == sw and 2 * cb == pw and NDEV * cb2 == 2 * sw and lx % GRID_W == 0
    mx, my, mc = _my_pos()
    me = 4 * mx + 2 * my + mc

    (c_all,) = _all_gather([c], "gather_c")
    call = jnp.concatenate([c_all.reshape(NDEV, d), c_ctx.reshape(1, d), jnp.zeros((NDEV - 1, d), F32)], axis=0)
    ada_w_l = ada_w[0]
    ada_b_l = lax.dynamic_slice_in_dim(ada_b, me * nmod, nmod, axis=1)
    m_loc = _mod_fwd(call, ada_w_l, ada_b_l)
    (m_all,) = _all_gather([m_loc], "gather_mod")
    mod = lax.dynamic_index_in_dim(m_all, me, axis=1, keepdims=False).reshape(1, NDEV * nmod)
    mod_c = m_all[:, NDEV, :].reshape(1, NDEV * nmod)
    shift, scale, gate = mod[:, :d], mod[:, d : 2 * d], mod[:, 2 * d :]
    shift_c, scale_c = mod_c[:, :d], mod_c[:, d : 2 * d]

    s5_params = tuple(p[0] for p in (s5_lam_re, s5_lam_im, s5_log_dt, s5_b_re, s5_b_im, s5_c_re, s5_c_im))
    (small, a16), small_vjp = jax.vjp(_s5_small, *s5_params)
    gb_ops = min(8, ngrp)
    (tsum, wst, ccat), (in_wg,) = _s5_build(small, gb_ops, [in_w[0].astype(BF16)])

    ssm_blocks = (2, 3)
    (proj, hb), (glu_wg, out_wg, pool_wg) = _in_proj(
        xr, norm_g, scale, shift, in_wg, tuple(range(NDEV)), tm, "in_proj",
        shards=[glu_w[0].astype(BF16), out_w[0].astype(BF16), pool_w[0].reshape(-1, pgw).astype(BF16)],
    )
    out_wf = out_wg.reshape(mix, d)
    pool_wf = pool_wg.reshape(NDEV, len(POOL_WINDOWS), pgw // NDEV, pgw).transpose(1, 0, 2, 3).reshape(len(POOL_WINDOWS), pgw, pgw)
    (uc, hcb), _ = _in_proj(xc, norm_g, scale_c, shift_c, in_wg, ssm_blocks, min(tm, lc), "in_proj_ctx")
    lins, dmats = [], []
    for gi, w in enumerate(POOL_WINDOWS):
        lin_g, dmat_g = _pool_fwd(proj, gi * pgw, w, pool_wf[gi], f"pool_fwd{gi}")
        lins.append(lin_g)
        dmats.append(dmat_g)

    gb_scan = min(16, ngrp)
    half = wst.shape[2] // 2
    perm = _chunk_perm()
    flat = lambda a: a.reshape(ngrp * nch, half)
    grp = lambda a: a.reshape(ngrp, nch, half)
    u_all, s_re, s_im = _s5_state(proj, pw // LANES, uc, perm, wst, ncc, ncl)
    h_re, h_im = _s5_scan_fwd(flat(s_re), flat(s_im), a16, ncc, nch, gb_scan)
    y_ssm = _s5_out(u_all, grp(h_re), grp(h_im), tsum, ccat, perm, ncc, lx)

    ypre, y12, branch, dxo, loss_l, dfg, dgate = _merge_fwd(
        y_ssm, proj, lins, xr, tgt, glu_wg, out_wf, s5_d, glu_b, pool_scale, gate, final_g.reshape(1, d), tm
    )

    dz, dlin, dg12, dps, d_out_w = _out_bwd(dxo, gate, branch, out_wf, lins, y12, proj, pool_scale, tm)
    dypre, dglu_b, dd_skip, d_glu_w = _glu_bwd(dg12, ypre, proj, glu_wg, s5_d, tm_acc)

    dy_all, dh_re, dh_im = _s5_dstate(dypre, perm, ccat, ncc)
    ds_re, ds_im = _s5_scan_bwd(flat(dh_re), flat(dh_im), a16, ncc, nch, gb_scan)
    du5, duc = _s5_du(dy_all, grp(ds_re), grp(ds_im), tsum, wst, perm, ncc, lx, lc)
    dwst, dtsum, dccat, da16 = _s5_op_grads(u_all, dy_all, grp(ds_re), grp(ds_im), grp(h_re), grp(h_im), ncc)
    dus, dpool_w = [], []
    for gi, w in enumerate(POOL_WINDOWS):
        du_g, dw_g = _pool_bwd(dlin, gi * pgw, w, pool_wf[gi], dmats[gi], f"pool_bwd{gi}")
        dus.append(du_g)
        dpool_w.append(dw_g)

    npw = len(POOL_WINDOWS)
    dpool_s = jnp.stack(dpool_w).reshape(npw, NDEV, pgw // NDEV, pgw).transpose(1, 0, 2, 3).reshape(NDEV, npw * (pgw // NDEV), pgw)
    parts_a = _pair_partials([d_glu_w, d_out_w.reshape(NDEV, mix // NDEV, d), dpool_s], [BF16] * 3, "grads_a")
    dpcb, dsc_c, dsh_c, dg_c = _in_bwd_ctx(xc, duc, in_wg, ssm_blocks, norm_g, scale_c, shift_c)
    (grad_x, dpb, dsc, dsh, dnorm_g), (land_glu, land_out, land_pool) = _in_bwd(
        xr, dxo, dus, du5, dypre, dz, in_wg, norm_g, scale, shift, s5_d, dg_c, tm, parts=parts_a
    )
    d_in_w = _in_w_grad(hb, dpb, hcb, dpcb, cb, min(2 * tm_acc, lx))

    dmod = jnp.concatenate(
        [jnp.concatenate([dsh, dsc, dgate], axis=1), jnp.concatenate([dsh_c, dsc_c, jnp.zeros((1, d), F32)], axis=1)], axis=0
    )
    (dmod_all,) = _all_gather([dmod], "gather_dmod")
    dmod_l = lax.dynamic_slice_in_dim(dmod_all, me * nmod, nmod, axis=2)
    g_ada_w, g_ada_b_l, cctx_part = _mod_bwd(call, dmod_l[:, 0, :], dmod_l[:, 1, :], ada_w_l)

    parts_b = _pair_partials([d_in_w], [BF16], "grads_b")
    dsmall, (land_in,) = _s5_build_bwd(small, dtsum, dwst, dccat, gb_ops, parts_b)
    ds5 = small_vjp((tuple(dsmall), da16))

    small_names = ["norm_g", "pool_scale", "s5_lam_re", "s5_lam_im", "s5_log_dt", "s5_b_re", "s5_b_im", "s5_c_re", "s5_c_im", "s5_d", "glu_b", "final_g", "c_ctx"]
    small_w = dict(norm_g=norm_g, pool_scale=pool_scale, s5_lam_re=s5_lam_re, s5_lam_im=s5_lam_im, s5_log_dt=s5_log_dt, s5_b_re=s5_b_re, s5_b_im=s5_b_im, s5_c_re=s5_c_re, s5_c_im=s5_c_im, s5_d=s5_d, glu_b=glu_b, final_g=final_g, c_ctx=c_ctx)
    small_m = dict(norm_g=m_norm_g, pool_scale=m_pool_scale, s5_lam_re=m_s5_lam_re, s5_lam_im=m_s5_lam_im, s5_log_dt=m_s5_log_dt, s5_b_re=m_s5_b_re, s5_b_im=m_s5_b_im, s5_c_re=m_s5_c_re, s5_c_im=m_s5_c_im, s5_d=m_s5_d, glu_b=m_glu_b, final_g=m_final_g, c_ctx=m_c_ctx)
    small_v = dict(norm_g=v_norm_g, pool_scale=v_pool_scale, s5_lam_re=v_s5_lam_re, s5_lam_im=v_s5_lam_im, s5_log_dt=v_s5_log_dt, s5_b_re=v_s5_b_re, s5_b_im=v_s5_b_im, s5_c_re=v_s5_c_re, s5_c_im=v_s5_c_im, s5_d=v_s5_d, glu_b=v_glu_b, final_g=v_final_g, c_ctx=v_c_ctx)
    small_g = dict(norm_g=dnorm_g, pool_scale=dps, s5_lam_re=ds5[0], s5_lam_im=ds5[1], s5_log_dt=ds5[2], s5_b_re=ds5[3], s5_b_im=ds5[4], s5_c_re=ds5[5], s5_c_im=ds5[6], s5_d=dd_skip, glu_b=dglu_b, final_g=dfg, c_ctx=cctx_part)
    shapes = [small_w[k].shape for k in small_names]
    nsmall = sum(int(np.prod(s)) for s in shapes) + 1
    unit = NDEV * 8 * LANES
    tot = -(-nsmall // unit) * unit
    rows = tot // LANES
    gpack = _pack([small_g[k] for k in small_names] + [loss_l], tot).reshape(NDEV, rows // NDEV, LANES)
    (land_small,) = _reduce_scatter([gpack], [F32], "small")
    gsum = _sum_slots(land_small, "sum_small")
    (gall,) = _all_gather([gsum], "gather_small")
    (g_ada_b_all,) = _all_gather([g_ada_b_l], "gather_ada_b")
    gflat = gall.reshape(tot)
    loss = gflat[nsmall - 1]

    nab = NDEV * nmod
    tot2 = tot + -(-nab // (8 * LANES)) * (8 * LANES)
    g2 = jnp.concatenate([gflat, jnp.pad(g_ada_b_all.reshape(nab), (0, tot2 - tot - nab))]).reshape(tot2 // LANES, LANES)

    def pack2(dct, ab):
        return jnp.concatenate(
            [_pack([dct[k] for k in small_names] + [jnp.zeros((1,), F32)], tot), jnp.pad(ab.reshape(nab), (0, tot2 - tot - nab))]
        ).reshape(tot2 // LANES, LANES)

    _, d2, m2, v2 = _adam(g2, pack2(small_w, ada_b), pack2(small_m, m_ada_b), pack2(small_v, v_ada_b), "adam_small")

    def unpack2(flat2):
        flat2 = flat2.reshape(tot2)
        vals = dict(zip(small_names, _unpack(flat2, shapes)))
        vals["ada_b"] = flat2[tot : tot + nab].reshape(ada_b.shape)
        return vals

    res = {k: unpack2(a) for k, a in (("g", g2), ("d", d2), ("m", m2), ("v", v2))}

    def shard(name, g, land, w, m, v):
        shp = w.shape
        w2, m2_, v2_ = (a.reshape(-1, shp[-1]) for a in (w, m, v))
        out = _adam(g, w2, m2_, v2_, "adam_" + name, land=land)
        for k, a in zip(("g", "d", "m", "v"), out):
            res[k][name] = a.reshape(shp)

    shard("ada_w", g_ada_w, None, ada_w, m_ada_w, v_ada_w)
    shard("in_w", None, land_in, in_w, m_in_w, v_in_w)
    shard("pool_w", None, land_pool, pool_w, m_pool_w, v_pool_w)
    shard("glu_w", None, land_glu, glu_w, m_glu_w, v_glu_w)
    shard("out_w", None, land_out, out_w, m_out_w, v_out_w)

    names = ["c_ctx", "ada_w", "ada_b", "norm_g", "in_w", "pool_w", "pool_scale", "s5_lam_re", "s5_lam_im", "s5_log_dt", "s5_b_re", "s5_b_im", "s5_c_re", "s5_c_im", "s5_d", "glu_w", "glu_b", "out_w", "final_g"]
    return (loss, grad_x[None], *[res["g"][n] for n in names], *[res["d"][n] for n in names], *[res["m"][n] for n in names], *[res["v"][n] for n in names])
```

```python
import numpy as np

import jax
import jax.numpy as jnp
from jax import lax
from jax.experimental import pallas as pl
from jax.experimental.pallas import tpu as pltpu

F32 = jnp.float32
BF16 = jnp.bfloat16
NDEV = 8
EPS = 1e-6
GRID_W = 64
POOL_WINDOWS = (2, 4, 8, 16)
SSM_H = 16
CHUNK_T = 16
LANES = 128
ADAM_LR, ADAM_B1, ADAM_B2, ADAM_EPS, ADAM_WD, ADAM_STEP = 0.001, 0.9, 0.999, 1e-08, 0.01, 10
VMEM_BIG = 56 << 20
MESH_ID = pl.DeviceIdType.MESH

_HBM = pl.BlockSpec(memory_space=pltpu.HBM)
_ANY = pl.BlockSpec(memory_space=pl.ANY)
_SMEM = pl.BlockSpec(memory_space=pltpu.SMEM)


def _sds(shape, dtype=F32):
    return jax.ShapeDtypeStruct(tuple(shape), dtype)


def _cparams(ngrid=0, vmem=None):
    return pltpu.CompilerParams(
        dimension_semantics=("arbitrary",) * ngrid if ngrid else None, vmem_limit_bytes=vmem
    )


def _rows(tm, c, col=0):
    return pl.BlockSpec((tm, c), lambda i: (i, col))


def _whole(shape):
    nd = len(shape)
    return pl.BlockSpec(tuple(shape), lambda *_: (0,) * nd, pipeline_mode=pl.Buffered(1))


def _acc(shape):
    nd = len(shape)
    return pl.BlockSpec(tuple(shape), lambda *_: (0,) * nd)


def _mm(a, b):
    return jnp.dot(a, b, preferred_element_type=F32)


def _mm_nt(a, b):
    return lax.dot_general(a, b, (((1,), (1,)), ((), ())), preferred_element_type=F32)


def _mm_tn(a, b):
    return lax.dot_general(a, b, (((0,), (0,)), ((), ())), preferred_element_type=F32)


def _mm_split(k01, s):
    hi = s.astype(BF16)
    lo = (s - hi.astype(F32)).astype(BF16)
    return _mm(k01, hi) + _mm(k01, lo)


def _sigmoid(v):
    return 0.5 * (jnp.tanh(0.5 * v) + 1.0)


def _silu(v):
    return v * _sigmoid(v)


_GELU_K = 0.7978845608028654
_GELU_C = 0.044715


def _gelu(v):
    return 0.5 * v * (1.0 + jnp.tanh(_GELU_K * (v + _GELU_C * v * v * v)))


def _gelu_grad(v):
    th = jnp.tanh(_GELU_K * (v + _GELU_C * v * v * v))
    return 0.5 * (1.0 + th) + 0.5 * v * (1.0 - th * th) * (_GELU_K * (1.0 + 3.0 * _GELU_C * v * v))


def _colsum(v):
    return jnp.sum(v, axis=0, keepdims=True)


def _rowmean(v):
    return jnp.mean(v, axis=-1, keepdims=True)


NCHIP = NDEV // 2


def _my_pos():
    return lax.axis_index("x"), lax.axis_index("y"), lax.axis_index("c")


def _other_chips():
    x, y, _ = _my_pos()
    return [(1 - x, y), (x, 1 - y), (1 - x, 1 - y)]


def _remote(src, dst, send, recv, to):
    return pltpu.make_async_remote_copy(src, dst, send, recv, device_id=to, device_id_type=MESH_ID)


def _all_gather(arrs, name):
    n = len(arrs)

    def body(*refs):
        start, finish = _gather_copies(refs[:n], refs[n : 2 * n], *refs[2 * n :])
        start()
        finish()

    return pl.pallas_call(
        body, name=name, out_shape=_gather_shapes(arrs), in_specs=[_HBM] * n, out_specs=[_HBM] * n, scratch_shapes=_gather_sems(n)
    )(*arrs)


def _gather_shapes(arrs):
    return [_sds((NDEV,) + a.shape, a.dtype) for a in arrs]


def _gather_sems(n):
    return [pltpu.SemaphoreType.DMA((n, NDEV - 1)), pltpu.SemaphoreType.DMA((n, NDEV - 1)), pltpu.SemaphoreType.DMA((n,))]


def _gather_copies(ins, outs, send, recv, loc):
    n = len(ins)
    x, y, c = _my_pos()
    me, sib = (x, y, c), (x, y, 1 - c)
    chips = _other_chips()

    def slot(a, p):
        return outs[a].at[4 * p[0] + 2 * p[1] + p[2]]

    def copy(a, k, block, to, own=False):
        return _remote(ins[a] if own else slot(a, block), slot(a, block), send.at[a, k], recv.at[a, k], to)

    def mine():
        local = [pltpu.make_async_copy(ins[a], slot(a, me), loc.at[a]) for a in range(n)]
        sent = []
        for a in range(n):
            sent.append(copy(a, 0, me, sib, own=True))
            sent += [copy(a, 1 + j, me, (*chip, c), own=True) for j, chip in enumerate(chips)]
        return local, sent

    def start():
        local, sent = mine()
        for cp in local + sent:
            cp.start()

    def finish():
        local, sent = mine()
        for j, chip in enumerate(chips):
            for a in range(n):
                copy(a, 1 + j, (*chip, c), me).wait_recv()
                fwd = copy(a, 4 + j, (*chip, c), sib)
                fwd.start()
                sent.append(fwd)
        for a in range(n):
            copy(a, 0, sib, me).wait_recv()
            for j, chip in enumerate(chips):
                copy(a, 4 + j, (*chip, 1 - c), me).wait_recv()
        for cp in sent:
            cp.wait_send()
        for cp in local:
            cp.wait()

    return start, finish


def _sibling_swap(arrs, name):
    n = len(arrs)

    def body(*refs):
        ins, outs = refs[:n], refs[n : 2 * n]
        send, recv = refs[2 * n :]
        x, y, c = _my_pos()
        cps = [_remote(ins[a].at[:, 1 - c], outs[a], send.at[a], recv.at[a], (x, y, 1 - c)) for a in range(n)]
        for cp in cps:
            cp.start()
        for cp in cps:
            cp.wait()

    return pl.pallas_call(
        body,
        name=name,
        out_shape=[_sds((a.shape[0],) + a.shape[2:], a.dtype) for a in arrs],
        in_specs=[_HBM] * n,
        out_specs=[_HBM] * n,
        scratch_shapes=[pltpu.SemaphoreType.DMA((n,)), pltpu.SemaphoreType.DMA((n,))],
    )(*arrs)


def _pair_sum(arr, got, wire, name):
    _, _, r, c = arr.shape
    tr = _row_tile(r, c, budget=1 << 20)

    def body(a_ref, g_ref, o_ref):
        o_ref[0] = (a_ref[0, lax.axis_index("c")] + g_ref[0]).astype(wire)

    return pl.pallas_call(
        body,
        name=name,
        grid=(NCHIP, r // tr),
        in_specs=[pl.BlockSpec((1, 2, tr, c), lambda q, i: (q, 0, i, 0)), pl.BlockSpec((1, tr, c), lambda q, i: (q, i, 0))],
        out_specs=pl.BlockSpec((1, tr, c), lambda q, i: (q, i, 0)),
        out_shape=_sds((NCHIP, r, c), wire),
        compiler_params=_cparams(2),
    )(arr, got)


def _chip_exchange(arrs, name):
    n = len(arrs)

    def body(*refs):
        start, finish = _exchange_copies(refs[:n], refs[n : 2 * n], *refs[2 * n :])
        start()
        finish()

    return pl.pallas_call(
        body, name=name, out_shape=[_sds(a.shape, a.dtype) for a in arrs], in_specs=[_HBM] * n, out_specs=[_HBM] * n, scratch_shapes=_exchange_sems(n)
    )(*arrs)


def _exchange_sems(n):
    return [pltpu.SemaphoreType.DMA((n, NCHIP - 1)), pltpu.SemaphoreType.DMA((n, NCHIP - 1)), pltpu.SemaphoreType.DMA((n,))]


def _exchange_copies(ins, outs, send, recv, loc):
    n = len(ins)
    x, y, c = _my_pos()
    mine = 2 * x + y
    chips = _other_chips()

    def copies():
        local = [pltpu.make_async_copy(ins[a].at[mine], outs[a].at[mine], loc.at[a]) for a in range(n)]
        sent = [
            _remote(ins[a].at[2 * px + py], outs[a].at[mine], send.at[a, j], recv.at[a, j], (px, py, c))
            for a in range(n)
            for j, (px, py) in enumerate(chips)
        ]
        return local, sent

    def start():
        local, sent = copies()
        for cp in local + sent:
            cp.start()

    def finish():
        local, sent = copies()
        for a in range(n):
            for j, (px, py) in enumerate(chips):
                _remote(ins[a].at[mine], outs[a].at[2 * px + py], send.at[a, j], recv.at[a, j], (px, py, c)).wait_recv()
        for cp in sent:
            cp.wait_send()
        for cp in local:
            cp.wait()

    return start, finish


def _pair_partials(arrs, wires, tag):
    four = [a.reshape((NCHIP, 2) + a.shape[1:]) for a in arrs]
    got = _sibling_swap(four, "swap_" + tag)
    return [_pair_sum(a, g, w, f"pair_sum_{tag}{i}") for i, (a, g, w) in enumerate(zip(four, got, wires))]


def _reduce_scatter(arrs, wires, tag):
    return _chip_exchange(_pair_partials(arrs, wires, tag), "exchange_" + tag)


def _row_tile(r, c, budget=1 << 20):
    best = r
    for t in range(8, r, 8):
        if r % t == 0 and t * c * 4 <= budget:
            best = t
    if r * c * 4 <= budget:
        best = r
    return best


def _sum_chips(land):
    tot = land[0].astype(F32)
    for q in range(1, NCHIP):
        tot = tot + land[q].astype(F32)
    return tot


def _adam_math(w, g, m, v):
    m2 = ADAM_B1 * m + (1.0 - ADAM_B1) * g
    v2 = ADAM_B2 * v + (1.0 - ADAM_B2) * (g * g)
    mh = m2 / (1.0 - ADAM_B1**ADAM_STEP)
    vh = v2 / (1.0 - ADAM_B2**ADAM_STEP)
    delta = -ADAM_LR * (mh / (jnp.sqrt(vh) + ADAM_EPS) + ADAM_WD * w)
    return delta, m2, v2


def _adam(g, w, m, v, name, land=None):
    r, c = w.shape
    tr = _row_tile(r, c, budget=1 << 20)

    def body(*refs):
        if land is not None:
            l_ref, w_ref, m_ref, v_ref, g_ref, d_ref, m2_ref, v2_ref = refs
            gv = _sum_chips(l_ref)
            g_ref[...] = gv
        else:
            g_in, w_ref, m_ref, v_ref, d_ref, m2_ref, v2_ref = refs
            gv = g_in[...]
        d, m2, v2 = _adam_math(w_ref[...], gv, m_ref[...], v_ref[...])
        d_ref[...] = d
        m2_ref[...] = m2
        v2_ref[...] = v2

    blk = _rows(tr, c)
    if land is not None:
        in_specs = [pl.BlockSpec((NCHIP, tr, c), lambda i: (0, i, 0)), blk, blk, blk]
        out = pl.pallas_call(
            body, name=name, grid=(r // tr,), in_specs=in_specs, out_specs=[blk] * 4, out_shape=[_sds((r, c))] * 4, compiler_params=_cparams(1)
        )(land, w, m, v)
        return out
    out = pl.pallas_call(
        body, name=name, grid=(r // tr,), in_specs=[blk] * 4, out_specs=[blk] * 3, out_shape=[_sds((r, c))] * 3, compiler_params=_cparams(1)
    )(g, w, m, v)
    return (g,) + tuple(out)


def _sum_slots(land, name):
    _, r, c = land.shape
    tr = _row_tile(r, c, budget=1 << 20)

    def body(l_ref, o_ref):
        o_ref[...] = _sum_chips(l_ref)

    return pl.pallas_call(
        body,
        name=name,
        grid=(r // tr,),
        in_specs=[pl.BlockSpec((NCHIP, tr, c), lambda i: (0, i, 0))],
        out_specs=_rows(tr, c),
        out_shape=_sds((r, c)),
        compiler_params=_cparams(1),
    )(land)


def _mod_fwd(call, ada_w, ada_b_loc):
    def body(c_ref, w_ref, b_ref, o_ref):
        s = _silu(c_ref[...]).astype(BF16)
        o_ref[...] = _mm(s, w_ref[...].astype(BF16)) + b_ref[...]

    return pl.pallas_call(
        body, name="mod_fwd", out_shape=_sds((call.shape[0], ada_w.shape[1])), compiler_params=_cparams(0, 32 << 20)
    )(call, ada_w, ada_b_loc)


def _mod_bwd(call, dm_loc, dmc_loc, ada_w):
    d, n = ada_w.shape
    pad = call.shape[0] - NDEV - 1

    def body(c_ref, dm_ref, dmc_ref, w_ref, gw_ref, gb_ref, cp_ref):
        cv = c_ref[...]
        sg = _sigmoid(cv)
        dmc = _colsum(dmc_ref[...])
        dm = dm_ref[...]
        rows = jnp.concatenate([dm, dmc, jnp.zeros((pad, n), F32)], axis=0)
        gw_ref[...] = _mm_tn((cv * sg).astype(BF16), rows.astype(BF16))
        gb_ref[...] = _colsum(dm) + dmc
        back = _mm_nt(rows[NDEV:].astype(BF16), w_ref[...].astype(BF16))
        c8, s8 = cv[NDEV : NDEV + 1], sg[NDEV : NDEV + 1]
        cp_ref[...] = back[0:1] * (s8 * (1.0 + c8 * (1.0 - s8)))

    return pl.pallas_call(
        body, name="mod_bwd", out_shape=[_sds((d, n)), _sds((1, n)), _sds((1, d))], compiler_params=_cparams(0, 40 << 20)
    )(call, dm_loc, dmc_loc, ada_w)


def _riding(body, nin, nout, nr, grid, copies):
    def wrapped(*refs):
        start, finish = copies(refs[nin : nin + nr], refs[nin + nr + nout : nin + 2 * nr + nout], *refs[nin + 2 * nr + nout :])
        first, last = True, True
        for ax, size in enumerate(grid):
            first = jnp.logical_and(first, pl.program_id(ax) == 0)
            last = jnp.logical_and(last, pl.program_id(ax) == size - 1)
        pl.when(first)(start)
        body(*refs[:nin], *refs[nin + nr : nin + nr + nout])
        pl.when(last)(finish)

    return wrapped if nr else body


def _in_proj(xr, norm_g, scale, shift, wg, jsel, tm, name, shards=()):
    lx, d = xr.shape
    cb = wg.shape[2]
    nr = len(shards)

    def body(x_ref, g_ref, sc_ref, sh_ref, w_ref, p_ref, h_ref):
        xv = x_ref[...]
        r = lax.rsqrt(_rowmean(xv * xv) + EPS)
        hb = ((xv * r) * g_ref[...] * (1.0 + sc_ref[...]) + sh_ref[...]).astype(BF16)
        h_ref[...] = hb
        for q, j in enumerate(jsel):
            p_ref[:, q * cb : (q + 1) * cb] = _mm(hb, w_ref[j])

    vec = _whole((1, d))
    out = pl.pallas_call(
        _riding(body, 5, 2, nr, (lx // tm,), _gather_copies),
        name=name,
        grid=(lx // tm,),
        in_specs=[_rows(tm, d), vec, vec, vec, _whole(wg.shape)] + [_HBM] * nr,
        out_specs=[_rows(tm, len(jsel) * cb), _rows(tm, d)] + [_HBM] * nr,
        out_shape=[_sds((lx, len(jsel) * cb)), _sds((lx, d), BF16)] + _gather_shapes(shards),
        scratch_shapes=_gather_sems(nr) if nr else [],
        compiler_params=_cparams(1, VMEM_BIG),
    )(xr, norm_g, scale, shift, wg, *shards)
    return out[:2], out[2:]


def _in_bwd(xr, dxo, dus, du5, dypre, dz, wg, norm_g, scale, shift, d_skip, dg_init, tm):
    lx, d = xr.shape
    cb = wg.shape[2]
    pgw = dus[0].shape[1]
    sw = du5.shape[1]
    mix = dz.shape[1]
    ncol = NDEV * cb

    def body(x_ref, dxo_ref, u0, u1, u2, u3, d5_ref, dy_ref, dz_ref, w_ref, g_ref, sc_ref, sh_ref, dk_ref, gi_ref,
             gx_ref, dp_ref, dsc_ref, dsh_ref, dg_ref):
        i = pl.program_id(0)

        @pl.when(i == 0)
        def _():
            dsc_ref[...] = jnp.zeros_like(dsc_ref)
            dsh_ref[...] = jnp.zeros_like(dsh_ref)
            dg_ref[...] = gi_ref[...]

        dp = jnp.concatenate(
            [u0[...], u1[...], u2[...], u3[...], d5_ref[...] + dk_ref[...] * dy_ref[...], dz_ref[...]], axis=1
        ).astype(BF16)
        dp_ref[...] = dp
        dh = _mm_nt(dp[:, 0:cb], w_ref[0])
        for j in range(1, NDEV):
            dh = dh + _mm_nt(dp[:, j * cb : (j + 1) * cb], w_ref[j])
        xv = x_ref[...]
        r = lax.rsqrt(_rowmean(xv * xv) + EPS)
        xh = xv * r
        g = g_ref[...]
        one_sc = 1.0 + sc_ref[...]
        dsh_ref[...] += _colsum(dh)
        dsc_ref[...] += _colsum(dh * (xh * g))
        dg_ref[...] += _colsum(dh * one_sc * xh)
        dxh = dh * one_sc * g
        gx_ref[...] = r * (dxh - xh * _rowmean(dxh * xh)) + dxo_ref[...]

    vec = _whole((1, d))
    return pl.pallas_call(
        body,
        name="in_bwd",
        grid=(lx // tm,),
        in_specs=[_rows(tm, d), _rows(tm, d)] + [_rows(tm, pgw)] * 4 + [_rows(tm, sw), _rows(tm, sw), _rows(tm, mix), _whole(wg.shape), vec, vec, vec, _whole((1, sw)), vec],
        out_specs=[_rows(tm, d), _rows(tm, ncol), _acc((1, d)), _acc((1, d)), _acc((1, d))],
        out_shape=[_sds((lx, d)), _sds((lx, ncol), BF16), _sds((1, d)), _sds((1, d)), _sds((1, d))],
        compiler_params=_cparams(1, VMEM_BIG),
    )(xr, dxo, *dus, du5, dypre, dz, wg, norm_g, scale, shift, d_skip, dg_init)


def _in_bwd_ctx(xc, duc, wg, jsel, norm_g, scale, shift):
    lc, d = xc.shape
    cb = wg.shape[2]

    def body(x_ref, du_ref, w_ref, g_ref, sc_ref, sh_ref, dp_ref, dsc_ref, dsh_ref, dg_ref):
        dp = du_ref[...].astype(BF16)
        dp_ref[...] = dp
        dh = _mm_nt(dp[:, 0:cb], w_ref[jsel[0]])
        for q in range(1, len(jsel)):
            dh = dh + _mm_nt(dp[:, q * cb : (q + 1) * cb], w_ref[jsel[q]])
        xv = x_ref[...]
        xh = xv * lax.rsqrt(_rowmean(xv * xv) + EPS)
        dsh_ref[...] = _colsum(dh)
        dsc_ref[...] = _colsum(dh * (xh * g_ref[...]))
        dg_ref[...] = _colsum(dh * (1.0 + sc_ref[...]) * xh)

    return pl.pallas_call(
        body,
        name="in_bwd_ctx",
        out_shape=[_sds(duc.shape, BF16), _sds((1, d)), _sds((1, d)), _sds((1, d))],
        compiler_params=_cparams(0, VMEM_BIG),
    )(xc, duc, wg, norm_g, scale, shift)


def _in_w_grad(hb, dpb, hcb, dpcb, cb, tm, parts=()):
    lx, d = hb.shape
    lc = hcb.shape[0]
    nb = lx // tm
    cg = 2 * cb
    nr = len(parts)
    grid = (NDEV // 2, nb + 1)

    def body(h_ref, dp_ref, hc_ref, dpc_ref, o_ref):
        j, i = pl.program_id(0), pl.program_id(1)

        @pl.when(i == 0)
        def _():
            o_ref[...] = jnp.zeros_like(o_ref)

        @pl.when(i < nb)
        def _():
            h = h_ref[...]
            o_ref[0] += _mm_tn(h, dp_ref[:, 0:cb])
            o_ref[1] += _mm_tn(h, dp_ref[:, cb:cg])

        @pl.when(jnp.logical_and(i == nb, j == 1))
        def _():
            h = hc_ref[...]
            o_ref[0] += _mm_tn(h, dpc_ref[:, 0:cb])
            o_ref[1] += _mm_tn(h, dpc_ref[:, cb:cg])

    out = pl.pallas_call(
        _riding(body, 4, 1, nr, grid, _exchange_copies),
        name="in_w_grad",
        grid=grid,
        in_specs=[
            pl.BlockSpec((tm, d), lambda j, i: (jnp.minimum(i, nb - 1), 0)),
            pl.BlockSpec((tm, cg), lambda j, i: (jnp.minimum(i, nb - 1), j)),
            pl.BlockSpec((lc, d), lambda j, i: (0, 0)),
            pl.BlockSpec((lc, cg), lambda j, i: (0, 0)),
        ]
        + [_HBM] * nr,
        out_specs=[pl.BlockSpec((2, d, cb), lambda j, i: (j, 0, 0))] + [_HBM] * nr,
        out_shape=[_sds((NDEV, d, cb))] + [_sds(p.shape, p.dtype) for p in parts],
        scratch_shapes=_exchange_sems(nr) if nr else [],
        compiler_params=_cparams(2, VMEM_BIG),
    )(hb, dpb, hcb, dpcb, *parts)
    return out[0], out[1:]


def _pool_tables(w, rows, rb, pgw, transpose):
    t = np.arange(GRID_W)
    lo, hi = np.clip(t - w // 2, 0, GRID_W), np.clip(t + w - w // 2, 0, GRID_W)
    band = ((t[None, :] >= lo[:, None]) & (t[None, :] < hi[:, None])).astype(np.float32)
    if transpose:
        band = band.T
    kc = np.kron(np.eye(rb, dtype=np.float32), band)
    inv_c = np.tile((1.0 / (hi - lo).astype(np.float32))[:, None], (rb, pgw)).astype(np.float32)
    r = np.arange(rows)
    cnt_r = np.clip(r + w - w // 2, 0, rows) - np.clip(r - w // 2, 0, rows)
    inv_r = (1.0 / cnt_r.astype(np.float32)).astype(np.float32)
    return jnp.asarray(kc, BF16), jnp.asarray(inv_r), jnp.asarray(inv_c)


def _pool_fwd(proj, col0, w, pw_g, name):
    lx = proj.shape[0]
    pgw = pw_g.shape[0]
    rows = lx // GRID_W
    rb = min(4, rows)
    tok = rb * GRID_W
    lo = w // 2
    kc, inv_r, inv_c = _pool_tables(w, rows, rb, pgw, False)

    def body(p_hbm, w_ref, kc_ref, ir_ref, ic_ref, lin_ref, dm_ref, xp, sem):
        xp[pl.ds(0, lo * GRID_W), :] = jnp.zeros((lo * GRID_W, pgw), F32)
        xp[pl.ds((lo + rows) * GRID_W, (w - lo) * GRID_W), :] = jnp.zeros(((w - lo) * GRID_W, pgw), F32)
        cp = pltpu.make_async_copy(p_hbm.at[:, pl.ds(col0, pgw)], xp.at[pl.ds(lo * GRID_W, lx), :], sem)
        cp.start()
        cp.wait()

        def blk(b, carry):
            r0 = b * rb
            parts = []
            for rr in range(rb):
                base = pl.multiple_of((r0 + rr) * GRID_W, GRID_W)
                s = xp[pl.ds(base, GRID_W), :]
                for k in range(1, w):
                    s = s + xp[pl.ds(base + k * GRID_W, GRID_W), :]
                parts.append(s * ir_ref[r0 + rr])
            m = _mm_split(kc_ref[...], jnp.concatenate(parts, axis=0)) * ic_ref[...]
            u = xp[pl.ds(pl.multiple_of((r0 + lo) * GRID_W, GRID_W), tok), :]
            db = (m - u).astype(BF16)
            o0 = pl.multiple_of(r0 * GRID_W, GRID_W)
            dm_ref[pl.ds(o0, tok), :] = db
            lin_ref[pl.ds(o0, tok), :] = _mm(db, w_ref[...])
            return carry

        lax.fori_loop(0, rows // rb, blk, 0)

    return pl.pallas_call(
        body,
        name=name,
        in_specs=[_ANY, pl.BlockSpec(memory_space=pltpu.VMEM), pl.BlockSpec(memory_space=pltpu.VMEM), _SMEM, pl.BlockSpec(memory_space=pltpu.VMEM)],
        out_shape=[_sds((lx, pgw)), _sds((lx, pgw), BF16)],
        scratch_shapes=[pltpu.VMEM(((rows + w) * GRID_W, pgw), F32), pltpu.SemaphoreType.DMA],
        compiler_params=_cparams(0, VMEM_BIG),
    )(proj, pw_g, kc, inv_r, inv_c)


def _pool_bwd(dlin, col0, w, pw_g, dmat_g, name):
    lx = dlin.shape[0]
    pgw = pw_g.shape[0]
    rows = lx // GRID_W
    rb = min(4, rows)
    tok = rb * GRID_W
    front = w - w // 2 - 1
    kct, inv_r, inv_c = _pool_tables(w, rows, rb, pgw, True)

    def body(dl_hbm, w_ref, kc_ref, ir_ref, ic_ref, dm_ref, du_ref, dw_ref, tp, dl, sem):
        if front:
            tp[pl.ds(0, front * GRID_W), :] = jnp.zeros((front * GRID_W, pgw), F32)
        tp[pl.ds((front + rows) * GRID_W, (w - front) * GRID_W), :] = jnp.zeros(((w - front) * GRID_W, pgw), F32)
        cp = pltpu.make_async_copy(dl_hbm.at[:, pl.ds(col0, pgw)], dl, sem)
        cp.start()
        cp.wait()
        dw_ref[...] = jnp.zeros_like(dw_ref)

        def blk(b, carry):
            o0 = pl.multiple_of(b * tok, GRID_W)
            dlb = dl[pl.ds(o0, tok), :].astype(BF16)
            dw_ref[...] += _mm_tn(dm_ref[pl.ds(o0, tok), :], dlb)
            dd = _mm_nt(dlb, w_ref[...])
            du_ref[pl.ds(o0, tok), :] = -dd
            t = _mm_split(kc_ref[...], dd * ic_ref[...])
            for rr in range(rb):
                dst = pl.multiple_of((b * rb + rr + front) * GRID_W, GRID_W)
                tp[pl.ds(dst, GRID_W), :] = t[rr * GRID_W : (rr + 1) * GRID_W] * ir_ref[b * rb + rr]
            return carry

        lax.fori_loop(0, rows // rb, blk, 0)

        def rowl(r, carry):
            base = pl.multiple_of(r * GRID_W, GRID_W)
            s = tp[pl.ds(base, GRID_W), :]
            for k in range(1, w):
                s = s + tp[pl.ds(base + k * GRID_W, GRID_W), :]
            du_ref[pl.ds(base, GRID_W), :] += s
            return carry

        lax.fori_loop(0, rows, rowl, 0)

    vm = pl.BlockSpec(memory_space=pltpu.VMEM)
    return pl.pallas_call(
        body,
        name=name,
        in_specs=[_ANY, vm, vm, _SMEM, vm, vm],
        out_shape=[_sds((lx, pgw)), _sds((pgw, pgw))],
        scratch_shapes=[pltpu.VMEM(((rows + w) * GRID_W, pgw), F32), pltpu.VMEM((lx, pgw), F32), pltpu.SemaphoreType.DMA],
        compiler_params=_cparams(0, VMEM_BIG),
    )(dlin, pw_g, kct, inv_r, inv_c, dmat_g)


def _s5_small(lam_re, lam_im, log_dt, b_re, b_im, c_re, c_im):
    t = CHUNK_T
    dt = jnp.exp(log_dt)[..., None]
    zr, zi = lam_re * dt, lam_im * dt
    tau = jnp.arange(t + 1, dtype=F32)
    mag = jnp.exp(zr[..., None] * tau)
    pr, pi = mag * jnp.cos(zi[..., None] * tau), mag * jnp.sin(zi[..., None] * tau)
    ar, ai = pr[..., 1], pi[..., 1]
    den = lam_re * lam_re + lam_im * lam_im
    qr = ((ar - 1.0) * lam_re + ai * lam_im) / den
    qi = (ai * lam_re - (ar - 1.0) * lam_im) / den
    bbr = qr[..., None] * b_re - qi[..., None] * b_im
    bbi = qr[..., None] * b_im + qi[..., None] * b_re
    ctr, cti = jnp.swapaxes(c_re, 2, 3), jnp.swapaxes(c_im, 2, 3)
    lane_pad = lambda v: jnp.pad(jnp.swapaxes(v, 0, 1), ((0, 0), (0, 0), (0, 0), (0, LANES - v.shape[-1])))
    tp = lambda v: jnp.swapaxes(v, 1, 2)
    flip = lambda v: v[..., ::-1]
    pfr, pfi = tp(flip(pr[0, ..., :t])), tp(flip(pi[0, ..., :t]))
    pbr, pbi = tp(pr[1, ..., :t]), tp(pi[1, ..., :t])
    bfr, bfi, bbr_t, bbi_t = tp(bbr[0]), tp(bbi[0]), tp(bbr[1]), tp(bbi[1])
    a1 = jnp.concatenate([pfr, pbr, pfr, pbr], axis=-1)
    a2 = jnp.concatenate([-pfi, -pbi, pfi, pbi], axis=-1)
    b1 = jnp.concatenate([bfr, bbr_t, bfi, bbi_t], axis=-1)
    b2 = jnp.concatenate([bfi, bbi_t, bfr, bbr_t], axis=-1)
    a16 = jnp.concatenate([pr[0, ..., t], pr[1, ..., t], pi[0, ..., t], pi[1, ..., t]], axis=1)
    return (lane_pad(pr), lane_pad(pi), lane_pad(bbr), lane_pad(bbi), lane_pad(ctr), lane_pad(cti), a1, a2, b1, b2), a16


def _split2(v):
    hi = v.astype(BF16)
    return hi, (v - hi.astype(F32)).astype(BF16)


def _dotx(a, b, mm=_mm):
    ah, al = (a, None) if a.dtype == BF16 else _split2(a)
    bh, bl = (b, None) if b.dtype == BF16 else _split2(b)
    out = mm(ah, bh)
    if bl is not None:
        out = out + mm(ah, bl)
    if al is not None:
        out = out + mm(al, bh)
    return out


def _shift_lanes(v, k):
    if k == 0:
        return v
    n = v.shape[-1]
    lane = lax.broadcasted_iota(jnp.int32, v.shape, v.ndim - 1)
    r = pltpu.roll(v, k % n, axis=v.ndim - 1)
    return jnp.where(lane >= k, r, 0.0) if k > 0 else jnp.where(lane < n + k, r, 0.0)


def _op_consts():
    t, h = CHUNK_T, SSM_H
    j, o = np.divmod(np.arange(t * h), h)
    p = np.arange(LANES)[:, None]
    sel = lambda col: (p == col[None, :]).astype(np.float32)
    pw = np.stack([np.concatenate([sel(j), sel(j + 1)], axis=1), np.concatenate([sel(t - 1 - j), sel(t - j)], axis=1)])
    return jnp.asarray(pw, BF16), jnp.asarray(sel(o), BF16), jnp.asarray(sel(j).T, BF16), jnp.asarray(sel(o).T, BF16)


def _op_factors(gi, pr_ref, pi_ref, br_ref, bi_ref, cr_ref, ci_ref, pw_ref, ch_ref):
    n = pr_ref.shape[2]
    c16 = _dotx(jnp.concatenate([cr_ref[gi, 0], ci_ref[gi, 0], cr_ref[gi, 1], ci_ref[gi, 1]], axis=0), ch_ref[...])
    out = []
    for d in range(2):
        e = _dotx(jnp.concatenate([pr_ref[gi, d], pi_ref[gi, d]], axis=0), pw_ref[d])
        c = c16[2 * d * n : (2 * d + 1) * n], c16[(2 * d + 1) * n : (2 * d + 2) * n]
        bst = jnp.concatenate([br_ref[gi, d], -bi_ref[gi, d]], axis=0)
        out.append(((e[:n, :CW], e[n:, :CW]), (e[:n, CW:], e[n:, CW:]), c, bst))
    return out


def _op_spread(m_ref, u_ref, v_ref, gi):
    return _dotx(m_ref[...], jnp.concatenate([_rows_pad(u_ref[gi]), _rows_pad(v_ref[gi])], axis=1))


def _cmul(x, y):
    return x[0] * y[0] - x[1] * y[1], x[0] * y[1] + x[1] * y[0]


def _cmul_conj(g, y):
    return g[0] * y[0] + g[1] * y[1], g[1] * y[0] - g[0] * y[1]


def _rows_pad(v):
    return jnp.concatenate([v, jnp.zeros((LANES - v.shape[0], v.shape[1]), F32)], axis=0)


def _s5_build(small, gb, shards):
    pr, pi, br, bi, cr, ci, a1, a2, b1, b2 = small
    g, _, n, _ = pr.shape
    t, h, ns = CHUNK_T, SSM_H, 4 * n
    pw, ch, rep_s, til_s = _op_consts()
    nr, nin, steps = len(shards), 14, g // gb

    def body(*refs):
        pr_ref, pi_ref, br_ref, bi_ref, cr_ref, ci_ref, a1_ref, a2_ref, b1_ref, b2_ref, pw_ref, ch_ref, rs_ref, ts_ref = refs[:nin]
        t_ref, w_ref, c_ref = refs[nin + nr : nin + nr + 3]
        start, finish = _gather_copies(refs[nin : nin + nr], refs[nin + nr + 3 : nin + 2 * nr + 3], *refs[nin + 2 * nr + 3 :])
        pl.when(pl.program_id(0) == 0)(start)

        def group(gi, carry):
            rows, outs = [], []
            for e16, e1, c16, bst in _op_factors(gi, pr_ref, pi_ref, br_ref, bi_ref, cr_ref, ci_ref, pw_ref, ch_ref):
                rows.append(_dotx(bst, jnp.concatenate(_cmul(c16, e16), axis=0), _mm_tn)[:h])
                outs.append(_cmul(c16, e1))
            c_ref[gi] = jnp.concatenate([outs[0][0], outs[1][0], -outs[0][1], -outs[1][1]], axis=0).astype(BF16)
            for s in range(t):
                t_ref[gi, s * h : (s + 1) * h, :] = (_shift_lanes(rows[0], h * s) + _shift_lanes(rows[1], -h * (t - 1 - s))).astype(BF16)
            ae, be = _op_spread(rs_ref, a1_ref, a2_ref, gi), _op_spread(ts_ref, b1_ref, b2_ref, gi)
            w_ref[gi] = (ae[:, :ns] * be[:, :ns] + ae[:, ns:] * be[:, ns:]).astype(BF16)
            return carry

        lax.fori_loop(0, gb, group, 0)
        pl.when(pl.program_id(0) == steps - 1)(finish)

    sm = pl.BlockSpec((gb, 2, n, LANES), lambda i: (i, 0, 0, 0))
    ab = lambda v: pl.BlockSpec((gb,) + v.shape[1:], lambda i: (i, 0, 0))
    out = pl.pallas_call(
        body,
        name="s5_build",
        grid=(steps,),
        in_specs=[sm] * 6 + [ab(a1), ab(a2), ab(b1), ab(b2), _whole(pw.shape), _whole(ch.shape), _whole(rep_s.shape), _whole(til_s.shape)] + [_HBM] * nr,
        out_specs=[_gspec(gb, CW, CW), _gspec(gb, CW, 4 * n), _gspec(gb, 4 * n, CW)] + [_HBM] * nr,
        out_shape=[_sds((g, CW, CW), BF16), _sds((g, CW, 4 * n), BF16), _sds((g, 4 * n, CW), BF16)] + _gather_shapes(shards),
        scratch_shapes=_gather_sems(nr),
        compiler_params=_cparams(1, VMEM_BIG),
    )(pr, pi, br, bi, cr, ci, a1, a2, b1, b2, pw, ch, rep_s, til_s, *shards)
    return out[:3], out[3:]


def _s5_build_bwd(small, dtsum, dwst, dccat, gb, parts):
    pr, pi, br, bi, cr, ci, a1, a2, b1, b2 = small
    g, _, n, _ = pr.shape
    t, h, ns = CHUNK_T, SSM_H, 4 * n
    pw, ch, rep_s, til_s = _op_consts()
    nr, nin, nout, steps = len(parts), 17, 10, g // gb

    def body(*refs):
        pr_ref, pi_ref, br_ref, bi_ref, cr_ref, ci_ref, a1_ref, a2_ref, b1_ref, b2_ref, pw_ref, ch_ref, rs_ref, ts_ref, dt_ref, dw_ref, dc_ref = refs[:nin]
        dpr_ref, dpi_ref, dbr_ref, dbi_ref, dcr_ref, dci_ref, da1_ref, da2_ref, db1_ref, db2_ref = refs[nin + nr : nin + nr + nout]
        start, finish = _exchange_copies(refs[nin : nin + nr], refs[nin + nr + nout : nin + 2 * nr + nout], *refs[nin + 2 * nr + nout :])
        pl.when(pl.program_id(0) == 0)(start)

        def group(gi, carry):
            drow = [jnp.zeros((h, CW), F32), jnp.zeros((h, CW), F32)]
            for s in range(t):
                blk = dt_ref[gi, s * h : (s + 1) * h, :]
                drow[0] = drow[0] + _shift_lanes(blk, -h * s)
                drow[1] = drow[1] + _shift_lanes(blk, h * (t - 1 - s))
            dcc = dc_ref[gi]
            dcs = []
            for d, (e16, e1, c16, bst) in enumerate(_op_factors(gi, pr_ref, pi_ref, br_ref, bi_ref, cr_ref, ci_ref, pw_ref, ch_ref)):
                dr = _rows_pad(drow[d])
                db = _dotx(jnp.concatenate(_cmul(c16, e16), axis=0), dr, _mm_nt)
                dbr_ref[gi, d] = db[:n]
                dbi_ref[gi, d] = -db[n:]
                dst = _dotx(bst, dr)
                dca = dst[:n], dst[n:]
                gx = dcc[d * n : (d + 1) * n], -dcc[(2 + d) * n : (3 + d) * n]
                dc16, dc16b = _cmul_conj(dca, e16), _cmul_conj(gx, e1)
                dcs += [dc16[0] + dc16b[0], dc16[1] + dc16b[1]]
                de = jnp.concatenate([jnp.concatenate(_cmul_conj(dca, c16), axis=0), jnp.concatenate(_cmul_conj(gx, c16), axis=0)], axis=1)
                dp = _dotx(de, pw_ref[d], _mm_nt)
                dpr_ref[gi, d] = dp[:n]
                dpi_ref[gi, d] = dp[n:]
            dct = _dotx(jnp.concatenate(dcs, axis=0), ch_ref[...], _mm_nt)
            for d in range(2):
                dcr_ref[gi, d] = dct[2 * d * n : (2 * d + 1) * n]
                dci_ref[gi, d] = dct[(2 * d + 1) * n : (2 * d + 2) * n]
            dw = dw_ref[gi]
            ae, be = _op_spread(rs_ref, a1_ref, a2_ref, gi), _op_spread(ts_ref, b1_ref, b2_ref, gi)
            da = _dotx(rs_ref[...], jnp.concatenate([dw * be[:, :ns], dw * be[:, ns:]], axis=1), _mm_tn)
            db = _dotx(ts_ref[...], jnp.concatenate([dw * ae[:, :ns], dw * ae[:, ns:]], axis=1), _mm_tn)
            da1_ref[gi] = da[:t, :ns]
            da2_ref[gi] = da[:t, ns:]
            db1_ref[gi] = db[:h, :ns]
            db2_ref[gi] = db[:h, ns:]
            return carry

        lax.fori_loop(0, gb, group, 0)
        pl.when(pl.program_id(0) == steps - 1)(finish)

    sm = pl.BlockSpec((gb, 2, n, LANES), lambda i: (i, 0, 0, 0))
    ab = lambda v: pl.BlockSpec((gb,) + v.shape[1:], lambda i: (i, 0, 0))
    ops = [_gspec(gb, CW, CW), _gspec(gb, CW, 4 * n), _gspec(gb, 4 * n, CW)]
    out = pl.pallas_call(
        body,
        name="s5_build_bwd",
        grid=(steps,),
        in_specs=[sm] * 6 + [ab(a1), ab(a2), ab(b1), ab(b2), _whole(pw.shape), _whole(ch.shape), _whole(rep_s.shape), _whole(til_s.shape)] + ops + [_HBM] * nr,
        out_specs=[sm] * 6 + [ab(a1), ab(a2), ab(b1), ab(b2)] + [_HBM] * nr,
        out_shape=[_sds(v.shape) for v in small] + [_sds(p.shape, p.dtype) for p in parts],
        scratch_shapes=_exchange_sems(nr),
        compiler_params=_cparams(1, VMEM_BIG),
    )(pr, pi, br, bi, cr, ci, a1, a2, b1, b2, pw, ch, rep_s, til_s, dtsum, dwst, dccat, *parts)
    return out[:nout], out[nout:]


SCAN_UNROLL = 4
GPL = LANES // SSM_H
CW = CHUNK_T * SSM_H
BW = CHUNK_T * LANES


def _chunk_perm():
    o = np.arange(BW)
    src = ((o % CW) // SSM_H) * LANES + (o // CW) * SSM_H + o % SSM_H
    return jnp.asarray(np.arange(BW)[:, None] == src[None, :], BF16)


def _gspec(gb, a, b):
    return pl.BlockSpec((gb, a, b), lambda i: (i, 0, 0))


def _chunk_rows(ref, n):
    return jnp.concatenate([ref[pl.ds(s, n, stride=CHUNK_T), :] for s in range(CHUNK_T)], axis=1)


def _s5_state(proj, col0, uc, perm, wst, ncc, ncl):
    nlb = uc.shape[1] // LANES
    g, _, ns = wst.shape
    nch = ncc + ncl
    half = ns // 2

    def body(p_ref, c_ref, pm_ref, w_ref, u_ref, sr_ref, si_ref):
        xrows = jnp.concatenate([_chunk_rows(c_ref, ncc), _chunk_rows(p_ref, ncl)], axis=0).astype(BF16)
        u = _mm(xrows, pm_ref[...]).astype(BF16)
        u_ref[0] = u
        for gi in range(GPL):
            s = _mm(u[:, gi * CW : (gi + 1) * CW], w_ref[gi].astype(BF16))
            sr_ref[gi] = s[:, :half]
            si_ref[gi] = s[:, half:]

    return pl.pallas_call(
        body,
        name="s5_state",
        grid=(nlb,),
        in_specs=[
            pl.BlockSpec((ncl * CHUNK_T, LANES), lambda i: (0, col0 + i)),
            pl.BlockSpec((ncc * CHUNK_T, LANES), lambda i: (0, i)),
            _whole(perm.shape),
            _gspec(GPL, CW, ns),
        ],
        out_specs=[pl.BlockSpec((1, nch, BW), lambda i: (i, 0, 0)), _gspec(GPL, nch, half), _gspec(GPL, nch, half)],
        out_shape=[_sds((nlb, nch, BW), BF16), _sds((g, nch, half)), _sds((g, nch, half))],
        compiler_params=_cparams(1, VMEM_BIG),
    )(proj, uc, perm, wst)


def _unrolled_loop(n, step, init):
    u = SCAN_UNROLL if n % SCAN_UNROLL == 0 else 1

    def trip(i, c):
        for k in range(u):
            c = step(i * u + k, c)
        return c

    return lax.fori_loop(0, n // u, trip, init)


def _lane_masks(gb, half):
    lane = lax.broadcasted_iota(jnp.int32, (gb, half), 1)
    return lane < (half // 2)


def _s5_scan_fwd(s_re, s_im, a16, ncc, nch, gb):
    g, ns = a16.shape
    half = ns // 2

    def body(sr_ref, si_ref, a_ref, hr_ref, hi_ref):
        ar, ai = a_ref[:, :half], a_ref[:, half:]
        mf = _lane_masks(gb, half)
        zero = jnp.zeros((gb, half), F32)
        hr_ref[...] = jnp.zeros_like(hr_ref)
        hi_ref[...] = jnp.zeros_like(hi_ref)

        def step(t, c):
            fr, fi, br, bi = c
            rf = pl.ds(t, gb, stride=nch)
            rb = pl.ds(jnp.where(t < ncc, ncc - 1 - t, nch - 1 - (t - ncc)), gb, stride=nch)
            hr_ref[rf, :] += jnp.where(mf, fr, 0.0)
            hi_ref[rf, :] += jnp.where(mf, fi, 0.0)
            hr_ref[rb, :] += jnp.where(mf, 0.0, br)
            hi_ref[rb, :] += jnp.where(mf, 0.0, bi)
            return (
                ar * fr - ai * fi + sr_ref[rf, :],
                ar * fi + ai * fr + si_ref[rf, :],
                ar * br - ai * bi + sr_ref[rb, :],
                ar * bi + ai * br + si_ref[rb, :],
            )

        _unrolled_loop(nch, step, (zero, zero, zero, zero))

    blk = pl.BlockSpec((gb * nch, half), lambda i: (i, 0))
    return pl.pallas_call(
        body,
        name="s5_scan_fwd",
        grid=(g // gb,),
        in_specs=[blk, blk, pl.BlockSpec((gb, ns), lambda i: (i, 0))],
        out_specs=[blk, blk],
        out_shape=[_sds((g * nch, half))] * 2,
        compiler_params=_cparams(1, VMEM_BIG),
    )(s_re, s_im, a16)


def _token_rows_store(ref, val, n):
    for s in range(CHUNK_T):
        ref[pl.ds(s, n, stride=CHUNK_T), :] = val[:, s * LANES : (s + 1) * LANES]


def _s5_out(u_all, h_re, h_im, tsum, ccat, perm, ncc, lx):
    nlb, nch, _ = u_all.shape
    g, ns, _ = ccat.shape
    ncl = nch - ncc
    half = ns // 2

    def body(u_ref, hr_ref, hi_ref, t_ref, c_ref, pm_ref, y_ref):
        parts = []
        for gi in range(GPL):
            u = u_ref[0, ncc:, gi * CW : (gi + 1) * CW]
            hs = jnp.concatenate([hr_ref[gi, ncc:, :], hi_ref[gi, ncc:, :]], axis=1).astype(BF16)
            parts.append(_mm(u, t_ref[gi].astype(BF16)) + _mm(hs, c_ref[gi].astype(BF16)))
        y = jnp.concatenate(parts, axis=1)
        hi = y.astype(BF16)
        lo = (y - hi.astype(F32)).astype(BF16)
        _token_rows_store(y_ref, _mm_nt(hi, pm_ref[...]) + _mm_nt(lo, pm_ref[...]), ncl)

    return pl.pallas_call(
        body,
        name="s5_out",
        grid=(nlb,),
        in_specs=[pl.BlockSpec((1, nch, BW), lambda i: (i, 0, 0)), _gspec(GPL, nch, half), _gspec(GPL, nch, half), _gspec(GPL, CW, CW), _gspec(GPL, ns, CW), _whole(perm.shape)],
        out_specs=pl.BlockSpec((lx, LANES), lambda i: (0, i)),
        out_shape=_sds((lx, nlb * LANES)),
        compiler_params=_cparams(1, VMEM_BIG),
    )(u_all, h_re, h_im, tsum, ccat, perm)


def _s5_dstate(dypre, perm, ccat, ncc):
    lx, sw = dypre.shape
    nlb = sw // LANES
    g, ns, _ = ccat.shape
    ncl = lx // CHUNK_T
    nch = ncl + ncc
    half = ns // 2

    def body(dy_ref, pm_ref, c_ref, dyr_ref, dr_ref, di_ref):
        dy = _mm(_chunk_rows(dy_ref, ncl).astype(BF16), pm_ref[...]).astype(BF16)
        dyr_ref[0] = dy
        for gi in range(GPL):
            dh = _mm_nt(dy[:, gi * CW : (gi + 1) * CW], c_ref[gi].astype(BF16))
            dr_ref[gi, :ncc, :] = jnp.zeros((ncc, half), F32)
            di_ref[gi, :ncc, :] = jnp.zeros((ncc, half), F32)
            dr_ref[gi, ncc:, :] = dh[:, :half]
            di_ref[gi, ncc:, :] = dh[:, half:]

    return pl.pallas_call(
        body,
        name="s5_dstate",
        grid=(nlb,),
        in_specs=[pl.BlockSpec((lx, LANES), lambda i: (0, i)), _whole(perm.shape), _gspec(GPL, ns, CW)],
        out_specs=[pl.BlockSpec((1, ncl, BW), lambda i: (i, 0, 0)), _gspec(GPL, nch, half), _gspec(GPL, nch, half)],
        out_shape=[_sds((nlb, ncl, BW), BF16), _sds((g, nch, half)), _sds((g, nch, half))],
        compiler_params=_cparams(1, VMEM_BIG),
    )(dypre, perm, ccat)


def _s5_scan_bwd(dh_re, dh_im, a16, ncc, nch, gb):
    g, ns = a16.shape
    half = ns // 2
    ncl = nch - ncc

    def body(dr_ref, di_ref, a_ref, sr_ref, si_ref):
        ar, ai = a_ref[:, :half], a_ref[:, half:]
        mf = _lane_masks(gb, half)
        zero = jnp.zeros((gb, half), F32)
        sr_ref[...] = jnp.zeros_like(sr_ref)
        si_ref[...] = jnp.zeros_like(si_ref)

        def step(t, c):
            lfr, lfi, lbr, lbi = c
            pf = pl.ds(nch - 1 - t, gb, stride=nch)
            pb = pl.ds(jnp.where(t < ncl, ncc + t, t - ncl), gb, stride=nch)
            sr_ref[pf, :] += jnp.where(mf, lfr, 0.0)
            si_ref[pf, :] += jnp.where(mf, lfi, 0.0)
            sr_ref[pb, :] += jnp.where(mf, 0.0, lbr)
            si_ref[pb, :] += jnp.where(mf, 0.0, lbi)
            return (
                dr_ref[pf, :] + ar * lfr + ai * lfi,
                di_ref[pf, :] + ar * lfi - ai * lfr,
                dr_ref[pb, :] + ar * lbr + ai * lbi,
                di_ref[pb, :] + ar * lbi - ai * lbr,
            )

        _unrolled_loop(nch, step, (zero,) * 4)

    blk = pl.BlockSpec((gb * nch, half), lambda i: (i, 0))
    return pl.pallas_call(
        body,
        name="s5_scan_bwd",
        grid=(g // gb,),
        in_specs=[blk, blk, pl.BlockSpec((gb, ns), lambda i: (i, 0))],
        out_specs=[blk, blk],
        out_shape=[_sds((g * nch, half))] * 2,
        compiler_params=_cparams(1, VMEM_BIG),
    )(dh_re, dh_im, a16)


def _s5_du(dy_all, ds_re, ds_im, tsum, wst, perm, ncc, lx, lc):
    nlb, ncl, _ = dy_all.shape
    g, _, ns = wst.shape
    nch = ncc + ncl
    half = ns // 2

    def body(dy_ref, sr_ref, si_ref, t_ref, w_ref, pm_ref, du_ref, dc_ref):
        parts = []
        for gi in range(GPL):
            ds = jnp.concatenate([sr_ref[gi], si_ref[gi]], axis=1).astype(BF16)
            d_all = _mm_nt(ds, w_ref[gi].astype(BF16))
            d_lat = d_all[ncc:] + _mm_nt(dy_ref[0, :, gi * CW : (gi + 1) * CW], t_ref[gi].astype(BF16))
            parts.append(jnp.concatenate([d_all[:ncc], d_lat], axis=0))
        du = _mm_nt(jnp.concatenate(parts, axis=1).astype(BF16), pm_ref[...])
        _token_rows_store(dc_ref, du[:ncc], ncc)
        _token_rows_store(du_ref, du[ncc:], ncl)

    return pl.pallas_call(
        body,
        name="s5_du",
        grid=(nlb,),
        in_specs=[pl.BlockSpec((1, ncl, BW), lambda i: (i, 0, 0)), _gspec(GPL, nch, half), _gspec(GPL, nch, half), _gspec(GPL, CW, CW), _gspec(GPL, CW, ns), _whole(perm.shape)],
        out_specs=[pl.BlockSpec((lx, LANES), lambda i: (0, i)), pl.BlockSpec((lc, LANES), lambda i: (0, i))],
        out_shape=[_sds((lx, nlb * LANES)), _sds((lc, nlb * LANES))],
        compiler_params=_cparams(1, VMEM_BIG),
    )(dy_all, ds_re, ds_im, tsum, wst, perm)


def _s5_op_grads(u_all, dy_all, ds_re, ds_im, h_re, h_im, ncc):
    nlb, nch, _ = u_all.shape
    g, _, half = ds_re.shape
    ns = 2 * half

    def body(u_ref, dy_ref, sr_ref, si_ref, hr_ref, hi_ref, dw_ref, dt_ref, dc_ref, da_ref):
        das = []
        for gi in range(GPL):
            cols = slice(gi * CW, (gi + 1) * CW)
            sr, si, hr, hi = sr_ref[gi], si_ref[gi], hr_ref[gi], hi_ref[gi]
            ds = jnp.concatenate([sr, si], axis=1).astype(BF16)
            dy = dy_ref[0, :, cols]
            dw_ref[gi] = _mm_tn(u_ref[0, :, cols], ds)
            dt_ref[gi] = _mm_tn(u_ref[0, ncc:, cols], dy)
            hs = jnp.concatenate([hr[ncc:], hi[ncc:]], axis=1).astype(BF16)
            dc_ref[gi] = _mm_tn(hs, dy)
            das.append(jnp.concatenate([_colsum(hr * sr + hi * si), _colsum(hr * si - hi * sr)], axis=1))
        da_ref[...] = jnp.concatenate(das, axis=0)

    return pl.pallas_call(
        body,
        name="s5_op_grads",
        grid=(nlb,),
        in_specs=[pl.BlockSpec((1, nch, BW), lambda i: (i, 0, 0)), pl.BlockSpec((1, nch - ncc, BW), lambda i: (i, 0, 0))] + [_gspec(GPL, nch, half)] * 4,
        out_specs=[_gspec(GPL, CW, ns), _gspec(GPL, CW, CW), _gspec(GPL, ns, CW), pl.BlockSpec((GPL, ns), lambda i: (i, 0))],
        out_shape=[_sds((g, CW, ns)), _sds((g, CW, CW)), _sds((g, ns, CW)), _sds((g, ns))],
        compiler_params=_cparams(1, VMEM_BIG),
    )(u_all, dy_all, ds_re, ds_im, h_re, h_im)


def _merge_fwd(y_ssm, proj, lins, x, tgt, glu_wg, out_w, d_skip, glu_b, pscale, gate, fin_g, tm):
    lx, d = x.shape
    sw = y_ssm.shape[1]
    mix = out_w.shape[0]
    pw = mix - sw
    pgw = lins[0].shape[1]
    cb2 = glu_wg.shape[2]
    nh = NDEV // 2

    def body(y_ref, u_ref, z_ref, l0, l1, l2, l3, x_ref, t_ref, gw_ref, ow_ref, dk_ref, gb_ref, ps_ref, gt_ref, fg_ref,
             ypre_ref, y12_ref, br_ref, dxo_ref, loss_ref, dfg_ref, dgt_ref):
        @pl.when(pl.program_id(0) == 0)
        def _():
            loss_ref[...] = jnp.zeros_like(loss_ref)
            dfg_ref[...] = jnp.zeros_like(dfg_ref)
            dgt_ref[...] = jnp.zeros_like(dgt_ref)

        ypre = y_ref[...] + dk_ref[...] * u_ref[...]
        ypre_ref[...] = ypre
        yg = _gelu(ypre).astype(BF16)
        outs = []
        for j in range(nh):
            y1 = _mm(yg, gw_ref[j]) + gb_ref[:, j * cb2 : (j + 1) * cb2]
            y2 = _mm(yg, gw_ref[nh + j]) + gb_ref[:, (nh + j) * cb2 : (nh + j + 1) * cb2]
            y12_ref[:, j * cb2 : (j + 1) * cb2] = y1
            y12_ref[:, (nh + j) * cb2 : (nh + j + 1) * cb2] = y2
            outs.append(y1 * _sigmoid(y2))
        lin = jnp.concatenate([l0[...], l1[...], l2[...], l3[...]], axis=1) * ps_ref[...]
        brb = (jnp.concatenate([lin] + outs, axis=1) * _silu(z_ref[...])).astype(BF16)
        br_ref[...] = brb
        mixv = _mm(brb, ow_ref[...])
        xo = x_ref[...] + gt_ref[...] * mixv
        r2 = lax.rsqrt(_rowmean(xo * xo) + EPS)
        xh = xo * r2
        fg = fg_ref[...]
        e = xh * fg - t_ref[...]
        loss_ref[...] += 0.5 * _colsum(_rowmean(e * e))
        dy = e * (1.0 / d)
        dfg_ref[...] += _colsum(dy * xh)
        gy = dy * fg
        dxo = r2 * (gy - xh * _rowmean(gy * xh))
        dxo_ref[...] = dxo
        dgt_ref[...] += _colsum(dxo * mixv)

    vec = _whole((1, d))
    return pl.pallas_call(
        body,
        name="merge_fwd",
        grid=(lx // tm,),
        in_specs=[_rows(tm, sw), _rows(tm, sw, 1), _rows(tm, mix, 1)] + [_rows(tm, pgw)] * 4 + [_rows(tm, d), _rows(tm, d), _whole(glu_wg.shape), _whole(out_w.shape), _whole((1, sw)), _whole((1, 2 * sw)), _whole((1, pw)), vec, vec],
        out_specs=[_rows(tm, sw), _rows(tm, 2 * sw), _rows(tm, mix), _rows(tm, d), _acc((1, 1)), _acc((1, d)), _acc((1, d))],
        out_shape=[_sds((lx, sw)), _sds((lx, 2 * sw)), _sds((lx, mix), BF16), _sds((lx, d)), _sds((1, 1)), _sds((1, d)), _sds((1, d))],
        compiler_params=_cparams(1, VMEM_BIG),
    )(y_ssm, proj, proj, *lins, x, tgt, glu_wg, out_w, d_skip, glu_b, pscale, gate, fin_g)


def _out_bwd(dxo, gate, branch, out_w, lins, y12, proj, pscale, tm):
    lx, d = dxo.shape
    mix = out_w.shape[0]
    pgw = lins[0].shape[1]
    pw = 4 * pgw
    sw = mix - pw
    nb = lx // tm

    def body(dxo_ref, gt_ref, br_ref, ow_ref, l0, l1, l2, l3, y_ref, z_ref, ps_ref, dz_ref, dl_ref, dg_ref, dps_ref, dow_hbm, acc, sem):
        i = pl.program_id(0)

        @pl.when(i == 0)
        def _():
            acc[...] = jnp.zeros_like(acc)
            dps_ref[...] = jnp.zeros_like(dps_ref)

        dmix = (dxo_ref[...] * gt_ref[...]).astype(BF16)
        dbr = _mm_nt(dmix, ow_ref[...])
        acc[...] += _mm_tn(br_ref[...], dmix)
        z = z_ref[...]
        sg = _sigmoid(z)
        dbp = dbr * (z * sg)
        y1, y2 = y_ref[:, :sw], y_ref[:, sw:]
        s2 = _sigmoid(y2)
        ps = ps_ref[...]
        lin = jnp.concatenate([l0[...], l1[...], l2[...], l3[...]], axis=1)
        bp = jnp.concatenate([lin * ps, y1 * s2], axis=1)
        dz_ref[...] = dbr * bp * (sg * (1.0 + z * (1.0 - sg)))
        dlp = dbp[:, :pw]
        dl_ref[...] = dlp * ps
        dps_ref[...] += _colsum(dlp * lin)
        dss = dbp[:, pw:]
        dg_ref[...] = jnp.concatenate([dss * s2, dss * y1 * s2 * (1.0 - s2)], axis=1).astype(BF16)

        @pl.when(i == nb - 1)
        def _():
            cp = pltpu.make_async_copy(acc, dow_hbm, sem)
            cp.start()
            cp.wait()

    return pl.pallas_call(
        body,
        name="out_bwd",
        grid=(nb,),
        in_specs=[_rows(tm, d), _whole((1, d)), _rows(tm, mix), _whole(out_w.shape)] + [_rows(tm, pgw)] * 4 + [_rows(tm, 2 * sw), _rows(tm, mix, 1), _whole((1, pw))],
        out_specs=[_rows(tm, mix), _rows(tm, pw), _rows(tm, 2 * sw), _acc((1, pw)), _ANY],
        out_shape=[_sds((lx, mix)), _sds((lx, pw)), _sds((lx, 2 * sw), BF16), _sds((1, pw)), _sds((mix, d))],
        scratch_shapes=[pltpu.VMEM((mix, d), F32), pltpu.SemaphoreType.DMA],
        compiler_params=_cparams(1, VMEM_BIG),
    )(dxo, gate, branch, out_w, *lins, y12, proj, pscale)


def _glu_bwd(dg12, ypre, proj, glu_wg, d_skip, tm):
    lx, sw = ypre.shape
    cb2 = glu_wg.shape[2]
    nb = lx // tm

    def body(dg_ref, yp_ref, u_ref, gw_ref, dk_ref, dyp_ref, dgb_ref, dd_ref, dgw_hbm, acc, sem):
        i = pl.program_id(0)

        @pl.when(i == 0)
        def _():
            acc[...] = jnp.zeros_like(acc)
            dgb_ref[...] = jnp.zeros_like(dgb_ref)
            dd_ref[...] = jnp.zeros_like(dd_ref)

        ypre = yp_ref[...]
        ygb = _gelu(ypre).astype(BF16)
        dg = dg_ref[...]
        dyg = jnp.zeros((tm, sw), F32)
        for j in range(NDEV):
            dgj = dg[:, j * cb2 : (j + 1) * cb2]
            dyg = dyg + _mm_nt(dgj, gw_ref[j])
            acc[j] += _mm_tn(ygb, dgj)
        dgb_ref[...] += _colsum(dg.astype(F32))
        dyp = dyg * _gelu_grad(ypre)
        dyp_ref[...] = dyp
        dd_ref[...] += _colsum(dyp * u_ref[...])

        @pl.when(i == nb - 1)
        def _():
            cp = pltpu.make_async_copy(acc, dgw_hbm, sem)
            cp.start()
            cp.wait()

    return pl.pallas_call(
        body,
        name="glu_bwd",
        grid=(nb,),
        in_specs=[_rows(tm, 2 * sw), _rows(tm, sw), _rows(tm, sw, 1), _whole(glu_wg.shape), _whole((1, sw))],
        out_specs=[_rows(tm, sw), _acc((1, 2 * sw)), _acc((1, sw)), _ANY],
        out_shape=[_sds((lx, sw)), _sds((1, 2 * sw)), _sds((1, sw)), _sds(glu_wg.shape)],
        scratch_shapes=[pltpu.VMEM(glu_wg.shape, F32), pltpu.SemaphoreType.DMA],
        compiler_params=_cparams(1, VMEM_BIG),
    )(dg12, ypre, proj, glu_wg, d_skip)


def _pack(parts, total):
    flat = jnp.concatenate([p.reshape(-1).astype(F32) for p in parts])
    return jnp.pad(flat, (0, total - flat.shape[0]))


def _unpack(flat, shapes):
    out, off = [], 0
    for s in shapes:
        n = int(np.prod(s)) if len(s) else 1
        out.append(flat[off : off + n].reshape(s))
        off += n
    return out


def kernel(x, c, ctx, c_ctx, ada_w, ada_b, norm_g, in_w, pool_w, pool_scale, s5_lam_re, s5_lam_im, s5_log_dt, s5_b_re, s5_b_im, s5_c_re, s5_c_im, s5_d, glu_w, glu_b, out_w, final_g, loss_target, m_c_ctx, m_ada_w, m_ada_b, m_norm_g, m_in_w, m_pool_w, m_pool_scale, m_s5_lam_re, m_s5_lam_im, m_s5_log_dt, m_s5_b_re, m_s5_b_im, m_s5_c_re, m_s5_c_im, m_s5_d, m_glu_w, m_glu_b, m_out_w, m_final_g, v_c_ctx, v_ada_w, v_ada_b, v_norm_g, v_in_w, v_pool_w, v_pool_scale, v_s5_lam_re, v_s5_lam_im, v_s5_log_dt, v_s5_b_re, v_s5_b_im, v_s5_c_re, v_s5_c_im, v_s5_d, v_glu_w, v_glu_b, v_out_w, v_final_g):
    xr, tgt, xc = x[0], loss_target[0], ctx[0]
    lx, d = xr.shape
    lc = xc.shape[0]
    mix = out_w.shape[1] * NDEV
    sw = glu_w.shape[1]
    pw = mix - sw
    pgw = pw // len(POOL_WINDOWS)
    ngrp = sw // SSM_H
    cb = in_w.shape[2]
    cb2 = glu_w.shape[2]
    nmod = ada_w.shape[2]
    ncc, ncl = lc // CHUNK_T, lx // CHUNK_T
    nch = ncc + ncl
    tm = min(256, lx)
    tm_acc = min(512, lx)
    assert mix == d and pw == sw and 2 * cb == pw and NDEV * cb2 == 2 * sw and lx % GRID_W == 0
    mx, my, mc = _my_pos()
    me = 4 * mx + 2 * my + mc

    (c_all,) = _all_gather([c], "gather_c")
    call = jnp.concatenate([c_all.reshape(NDEV, d), c_ctx.reshape(1, d), jnp.zeros((NDEV - 1, d), F32)], axis=0)
    ada_w_l = ada_w[0]
    ada_b_l = lax.dynamic_slice_in_dim(ada_b, me * nmod, nmod, axis=1)
    m_loc = _mod_fwd(call, ada_w_l, ada_b_l)
    (m_all,) = _all_gather([m_loc], "gather_mod")
    mod = lax.dynamic_index_in_dim(m_all, me, axis=1, keepdims=False).reshape(1, NDEV * nmod)
    mod_c = m_all[:, NDEV, :].reshape(1, NDEV * nmod)
    shift, scale, gate = mod[:, :d], mod[:, d : 2 * d], mod[:, 2 * d :]
    shift_c, scale_c = mod_c[:, :d], mod_c[:, d : 2 * d]

    s5_params = tuple(p[0] for p in (s5_lam_re, s5_lam_im, s5_log_dt, s5_b_re, s5_b_im, s5_c_re, s5_c_im))
    (small, a16), small_vjp = jax.vjp(_s5_small, *s5_params)
    gb_ops = min(8, ngrp)
    (tsum, wst, ccat), (in_wg,) = _s5_build(small, gb_ops, [in_w[0].astype(BF16)])

    ssm_blocks = (2, 3)
    (proj, hb), (glu_wg, out_wg, pool_wg) = _in_proj(
        xr, norm_g, scale, shift, in_wg, tuple(range(NDEV)), tm, "in_proj",
        shards=[glu_w[0].astype(BF16), out_w[0].astype(BF16), pool_w[0].reshape(-1, pgw).astype(BF16)],
    )
    out_wf = out_wg.reshape(mix, d)
    pool_wf = pool_wg.reshape(NDEV, len(POOL_WINDOWS), pgw // NDEV, pgw).transpose(1, 0, 2, 3).reshape(len(POOL_WINDOWS), pgw, pgw)
    (uc, hcb), _ = _in_proj(xc, norm_g, scale_c, shift_c, in_wg, ssm_blocks, min(tm, lc), "in_proj_ctx")
    lins, dmats = [], []
    for gi, w in enumerate(POOL_WINDOWS):
        lin_g, dmat_g = _pool_fwd(proj, gi * pgw, w, pool_wf[gi], f"pool_fwd{gi}")
        lins.append(lin_g)
        dmats.append(dmat_g)

    gb_scan = min(16, ngrp)
    half = wst.shape[2] // 2
    perm = _chunk_perm()
    flat = lambda a: a.reshape(ngrp * nch, half)
    grp = lambda a: a.reshape(ngrp, nch, half)
    u_all, s_re, s_im = _s5_state(proj, pw // LANES, uc, perm, wst, ncc, ncl)
    h_re, h_im = _s5_scan_fwd(flat(s_re), flat(s_im), a16, ncc, nch, gb_scan)
    y_ssm = _s5_out(u_all, grp(h_re), grp(h_im), tsum, ccat, perm, ncc, lx)

    ypre, y12, branch, dxo, loss_l, dfg, dgate = _merge_fwd(
        y_ssm, proj, lins, xr, tgt, glu_wg, out_wf, s5_d, glu_b, pool_scale, gate, final_g.reshape(1, d), tm
    )

    dz, dlin, dg12, dps, d_out_w = _out_bwd(dxo, gate, branch, out_wf, lins, y12, proj, pool_scale, tm)
    dypre, dglu_b, dd_skip, d_glu_w = _glu_bwd(dg12, ypre, proj, glu_wg, s5_d, tm_acc)

    dy_all, dh_re, dh_im = _s5_dstate(dypre, perm, ccat, ncc)
    ds_re, ds_im = _s5_scan_bwd(flat(dh_re), flat(dh_im), a16, ncc, nch, gb_scan)
    du5, duc = _s5_du(dy_all, grp(ds_re), grp(ds_im), tsum, wst, perm, ncc, lx, lc)
    dwst, dtsum, dccat, da16 = _s5_op_grads(u_all, dy_all, grp(ds_re), grp(ds_im), grp(h_re), grp(h_im), ncc)
    dus, dpool_w = [], []
    for gi, w in enumerate(POOL_WINDOWS):
        du_g, dw_g = _pool_bwd(dlin, gi * pgw, w, pool_wf[gi], dmats[gi], f"pool_bwd{gi}")
        dus.append(du_g)
        dpool_w.append(dw_g)

    npw = len(POOL_WINDOWS)
    dpool_s = jnp.stack(dpool_w).reshape(npw, NDEV, pgw // NDEV, pgw).transpose(1, 0, 2, 3).reshape(NDEV, npw * (pgw // NDEV), pgw)
    parts_a = _pair_partials([d_glu_w, d_out_w.reshape(NDEV, mix // NDEV, d), dpool_s], [BF16] * 3, "grads_a")
    dpcb, dsc_c, dsh_c, dg_c = _in_bwd_ctx(xc, duc, in_wg, ssm_blocks, norm_g, scale_c, shift_c)
    grad_x, dpb, dsc, dsh, dnorm_g = _in_bwd(xr, dxo, dus, du5, dypre, dz, in_wg, norm_g, scale, shift, s5_d, dg_c, tm)
    d_in_w, (land_glu, land_out, land_pool) = _in_w_grad(hb, dpb, hcb, dpcb, cb, min(2 * tm_acc, lx), parts=parts_a)

    dmod = jnp.concatenate(
        [jnp.concatenate([dsh, dsc, dgate], axis=1), jnp.concatenate([dsh_c, dsc_c, jnp.zeros((1, d), F32)], axis=1)], axis=0
    )
    (dmod_all,) = _all_gather([dmod], "gather_dmod")
    dmod_l = lax.dynamic_slice_in_dim(dmod_all, me * nmod, nmod, axis=2)
    g_ada_w, g_ada_b_l, cctx_part = _mod_bwd(call, dmod_l[:, 0, :], dmod_l[:, 1, :], ada_w_l)

    parts_b = _pair_partials([d_in_w], [BF16], "grads_b")
    dsmall, (land_in,) = _s5_build_bwd(small, dtsum, dwst, dccat, gb_ops, parts_b)
    ds5 = small_vjp((tuple(dsmall), da16))

    small_names = ["norm_g", "pool_scale", "s5_lam_re", "s5_lam_im", "s5_log_dt", "s5_b_re", "s5_b_im", "s5_c_re", "s5_c_im", "s5_d", "glu_b", "final_g", "c_ctx"]
    small_w = dict(norm_g=norm_g, pool_scale=pool_scale, s5_lam_re=s5_lam_re, s5_lam_im=s5_lam_im, s5_log_dt=s5_log_dt, s5_b_re=s5_b_re, s5_b_im=s5_b_im, s5_c_re=s5_c_re, s5_c_im=s5_c_im, s5_d=s5_d, glu_b=glu_b, final_g=final_g, c_ctx=c_ctx)
    small_m = dict(norm_g=m_norm_g, pool_scale=m_pool_scale, s5_lam_re=m_s5_lam_re, s5_lam_im=m_s5_lam_im, s5_log_dt=m_s5_log_dt, s5_b_re=m_s5_b_re, s5_b_im=m_s5_b_im, s5_c_re=m_s5_c_re, s5_c_im=m_s5_c_im, s5_d=m_s5_d, glu_b=m_glu_b, final_g=m_final_g, c_ctx=m_c_ctx)
    small_v = dict(norm_g=v_norm_g, pool_scale=v_pool_scale, s5_lam_re=v_s5_lam_re, s5_lam_im=v_s5_lam_im, s5_log_dt=v_s5_log_dt, s5_b_re=v_s5_b_re, s5_b_im=v_s5_b_im, s5_c_re=v_s5_c_re, s5_c_im=v_s5_c_im, s5_d=v_s5_d, glu_b=v_glu_b, final_g=v_final_g, c_ctx=v_c_ctx)
    small_g = dict(norm_g=dnorm_g, pool_scale=dps, s5_lam_re=ds5[0], s5_lam_im=ds5[1], s5_log_dt=ds5[2], s5_b_re=ds5[3], s5_b_im=ds5[4], s5_c_re=ds5[5], s5_c_im=ds5[6], s5_d=dd_skip, glu_b=dglu_b, final_g=dfg, c_ctx=cctx_part)
    shapes = [small_w[k].shape for k in small_names]
    nsmall = sum(int(np.prod(s)) for s in shapes) + 1
    unit = NDEV * 8 * LANES
    tot = -(-nsmall // unit) * unit
    rows = tot // LANES
    gpack = _pack([small_g[k] for k in small_names] + [loss_l], tot).reshape(NDEV, rows // NDEV, LANES)
    (land_small,) = _reduce_scatter([gpack], [F32], "small")
    gsum = _sum_slots(land_small, "sum_small")
    (gall,) = _all_gather([gsum], "gather_small")
    (g_ada_b_all,) = _all_gather([g_ada_b_l], "gather_ada_b")
    gflat = gall.reshape(tot)
    loss = gflat[nsmall - 1]

    nab = NDEV * nmod
    tot2 = tot + -(-nab // (8 * LANES)) * (8 * LANES)
    g2 = jnp.concatenate([gflat, jnp.pad(g_ada_b_all.reshape(nab), (0, tot2 - tot - nab))]).reshape(tot2 // LANES, LANES)

    def pack2(dct, ab):
        return jnp.concatenate(
            [_pack([dct[k] for k in small_names] + [jnp.zeros((1,), F32)], tot), jnp.pad(ab.reshape(nab), (0, tot2 - tot - nab))]
        ).reshape(tot2 // LANES, LANES)

    _, d2, m2, v2 = _adam(g2, pack2(small_w, ada_b), pack2(small_m, m_ada_b), pack2(small_v, v_ada_b), "adam_small")

    def unpack2(flat2):
        flat2 = flat2.reshape(tot2)
        vals = dict(zip(small_names, _unpack(flat2, shapes)))
        vals["ada_b"] = flat2[tot : tot + nab].reshape(ada_b.shape)
        return vals

    res = {k: unpack2(a) for k, a in (("g", g2), ("d", d2), ("m", m2), ("v", v2))}

    def shard(name, g, land, w, m, v):
        shp = w.shape
        w2, m2_, v2_ = (a.reshape(-1, shp[-1]) for a in (w, m, v))
        out = _adam(g, w2, m2_, v2_, "adam_" + name, land=land)
        for k, a in zip(("g", "d", "m", "v"), out):
            res[k][name] = a.reshape(shp)

    shard("ada_w", g_ada_w, None, ada_w, m_ada_w, v_ada_w)
    shard("in_w", None, land_in, in_w, m_in_w, v_in_w)
    shard("pool_w", None, land_pool, pool_w, m_pool_w, v_pool_w)
    shard("glu_w", None, land_glu, glu_w, m_glu_w, v_glu_w)
    shard("out_w", None, land_out, out_w, m_out_w, v_out_w)

    names = ["c_ctx", "ada_w", "ada_b", "norm_g", "in_w", "pool_w", "pool_scale", "s5_lam_re", "s5_lam_im", "s5_log_dt", "s5_b_re", "s5_b_im", "s5_c_re", "s5_c_im", "s5_d", "glu_w", "glu_b", "out_w", "final_g"]
    return (loss, grad_x[None], *[res["g"][n] for n in names], *[res["d"][n] for n in names], *[res["m"][n] for n in names], *[res["v"][n] for n in names])
```

```python
import numpy as np

import jax
import jax.numpy as jnp
from jax import lax
from jax.experimental import pallas as pl
from jax.experimental.pallas import tpu as pltpu

F32 = jnp.float32
BF16 = jnp.bfloat16
NDEV = 8
EPS = 1e-6
GRID_W = 64
POOL_WINDOWS = (2, 4, 8, 16)
SSM_H = 16
CHUNK_T = 16
LANES = 128
ADAM_LR, ADAM_B1, ADAM_B2, ADAM_EPS, ADAM_WD, ADAM_STEP = 0.001, 0.9, 0.999, 1e-08, 0.01, 10
VMEM_BIG = 56 << 20
MESH_ID = pl.DeviceIdType.MESH

_HBM = pl.BlockSpec(memory_space=pltpu.HBM)
_ANY = pl.BlockSpec(memory_space=pl.ANY)
_SMEM = pl.BlockSpec(memory_space=pltpu.SMEM)


def _sds(shape, dtype=F32):
    return jax.ShapeDtypeStruct(tuple(shape), dtype)


def _cparams(ngrid=0, vmem=None):
    return pltpu.CompilerParams(
        dimension_semantics=("arbitrary",) * ngrid if ngrid else None, vmem_limit_bytes=vmem
    )


def _rows(tm, c, col=0):
    return pl.BlockSpec((tm, c), lambda i: (i, col))


def _whole(shape):
    nd = len(shape)
    return pl.BlockSpec(tuple(shape), lambda *_: (0,) * nd, pipeline_mode=pl.Buffered(1))


def _acc(shape):
    nd = len(shape)
    return pl.BlockSpec(tuple(shape), lambda *_: (0,) * nd)


def _mm(a, b):
    return jnp.dot(a, b, preferred_element_type=F32)


def _mm_nt(a, b):
    return lax.dot_general(a, b, (((1,), (1,)), ((), ())), preferred_element_type=F32)


def _mm_tn(a, b):
    return lax.dot_general(a, b, (((0,), (0,)), ((), ())), preferred_element_type=F32)


def _mm_split(k01, s):
    hi = s.astype(BF16)
    lo = (s - hi.astype(F32)).astype(BF16)
    return _mm(k01, hi) + _mm(k01, lo)


def _sigmoid(v):
    return 0.5 * (jnp.tanh(0.5 * v) + 1.0)


def _silu(v):
    return v * _sigmoid(v)


_GELU_K = 0.7978845608028654
_GELU_C = 0.044715


def _gelu(v):
    return 0.5 * v * (1.0 + jnp.tanh(_GELU_K * (v + _GELU_C * v * v * v)))


def _gelu_grad(v):
    th = jnp.tanh(_GELU_K * (v + _GELU_C * v * v * v))
    return 0.5 * (1.0 + th) + 0.5 * v * (1.0 - th * th) * (_GELU_K * (1.0 + 3.0 * _GELU_C * v * v))


def _colsum(v):
    return jnp.sum(v, axis=0, keepdims=True)


def _rowmean(v):
    return jnp.mean(v, axis=-1, keepdims=True)


NCHIP = NDEV // 2


def _my_pos():
    return lax.axis_index("x"), lax.axis_index("y"), lax.axis_index("c")


def _other_chips():
    x, y, _ = _my_pos()
    return [(1 - x, y), (x, 1 - y), (1 - x, 1 - y)]


def _remote(src, dst, send, recv, to):
    return pltpu.make_async_remote_copy(src, dst, send, recv, device_id=to, device_id_type=MESH_ID)


def _all_gather(arrs, name):
    n = len(arrs)

    def body(*refs):
        start, finish = _gather_copies(refs[:n], refs[n : 2 * n], *refs[2 * n :])
        start()
        finish()

    return pl.pallas_call(
        body, name=name, out_shape=_gather_shapes(arrs), in_specs=[_HBM] * n, out_specs=[_HBM] * n, scratch_shapes=_gather_sems(n)
    )(*arrs)


def _gather_shapes(arrs):
    return [_sds((NDEV,) + a.shape, a.dtype) for a in arrs]


def _gather_sems(n):
    return [pltpu.SemaphoreType.DMA((n, NDEV - 1)), pltpu.SemaphoreType.DMA((n, NDEV - 1)), pltpu.SemaphoreType.DMA((n,))]


def _gather_copies(ins, outs, send, recv, loc):
    n = len(ins)
    x, y, c = _my_pos()
    me, sib = (x, y, c), (x, y, 1 - c)
    chips = _other_chips()

    def slot(a, p):
        return outs[a].at[4 * p[0] + 2 * p[1] + p[2]]

    def copy(a, k, block, to, own=False):
        return _remote(ins[a] if own else slot(a, block), slot(a, block), send.at[a, k], recv.at[a, k], to)

    def mine():
        local = [pltpu.make_async_copy(ins[a], slot(a, me), loc.at[a]) for a in range(n)]
        sent = []
        for a in range(n):
            sent.append(copy(a, 0, me, sib, own=True))
            sent += [copy(a, 1 + j, me, (*chip, c), own=True) for j, chip in enumerate(chips)]
        return local, sent

    def start():
        local, sent = mine()
        for cp in local + sent:
            cp.start()

    def finish():
        local, sent = mine()
        for j, chip in enumerate(chips):
            for a in range(n):
                copy(a, 1 + j, (*chip, c), me).wait_recv()
                fwd = copy(a, 4 + j, (*chip, c), sib)
                fwd.start()
                sent.append(fwd)
        for a in range(n):
            copy(a, 0, sib, me).wait_recv()
            for j, chip in enumerate(chips):
                copy(a, 4 + j, (*chip, 1 - c), me).wait_recv()
        for cp in sent:
            cp.wait_send()
        for cp in local:
            cp.wait()

    return start, finish


def _sibling_swap(arrs, name):
    n = len(arrs)

    def body(*refs):
        ins, outs = refs[:n], refs[n : 2 * n]
        send, recv = refs[2 * n :]
        x, y, c = _my_pos()
        cps = [_remote(ins[a].at[:, 1 - c], outs[a], send.at[a], recv.at[a], (x, y, 1 - c)) for a in range(n)]
        for cp in cps:
            cp.start()
        for cp in cps:
            cp.wait()

    return pl.pallas_call(
        body,
        name=name,
        out_shape=[_sds((a.shape[0],) + a.shape[2:], a.dtype) for a in arrs],
        in_specs=[_HBM] * n,
        out_specs=[_HBM] * n,
        scratch_shapes=[pltpu.SemaphoreType.DMA((n,)), pltpu.SemaphoreType.DMA((n,))],
    )(*arrs)


def _pair_sum(arr, got, wire, name):
    _, _, r, c = arr.shape
    tr = _row_tile(r, c, budget=1 << 20)

    def body(a_ref, g_ref, o_ref):
        o_ref[0] = (a_ref[0, lax.axis_index("c")] + g_ref[0]).astype(wire)

    return pl.pallas_call(
        body,
        name=name,
        grid=(NCHIP, r // tr),
        in_specs=[pl.BlockSpec((1, 2, tr, c), lambda q, i: (q, 0, i, 0)), pl.BlockSpec((1, tr, c), lambda q, i: (q, i, 0))],
        out_specs=pl.BlockSpec((1, tr, c), lambda q, i: (q, i, 0)),
        out_shape=_sds((NCHIP, r, c), wire),
        compiler_params=_cparams(2),
    )(arr, got)


def _chip_exchange(arrs, name):
    n = len(arrs)

    def body(*refs):
        start, finish = _exchange_copies(refs[:n], refs[n : 2 * n], *refs[2 * n :])
        start()
        finish()

    return pl.pallas_call(
        body, name=name, out_shape=[_sds(a.shape, a.dtype) for a in arrs], in_specs=[_HBM] * n, out_specs=[_HBM] * n, scratch_shapes=_exchange_sems(n)
    )(*arrs)


def _exchange_sems(n):
    return [pltpu.SemaphoreType.DMA((n, NCHIP - 1)), pltpu.SemaphoreType.DMA((n, NCHIP - 1)), pltpu.SemaphoreType.DMA((n,))]


def _exchange_copies(ins, outs, send, recv, loc):
    n = len(ins)
    x, y, c = _my_pos()
    mine = 2 * x + y
    chips = _other_chips()

    def copies():
        local = [pltpu.make_async_copy(ins[a].at[mine], outs[a].at[mine], loc.at[a]) for a in range(n)]
        sent = [
            _remote(ins[a].at[2 * px + py], outs[a].at[mine], send.at[a, j], recv.at[a, j], (px, py, c))
            for a in range(n)
            for j, (px, py) in enumerate(chips)
        ]
        return local, sent

    def start():
        local, sent = copies()
        for cp in local + sent:
            cp.start()

    def finish():
        local, sent = copies()
        for a in range(n):
            for j, (px, py) in enumerate(chips):
                _remote(ins[a].at[mine], outs[a].at[2 * px + py], send.at[a, j], recv.at[a, j], (px, py, c)).wait_recv()
        for cp in sent:
            cp.wait_send()
        for cp in local:
            cp.wait()

    return start, finish


def _pair_partials(arrs, wires, tag):
    four = [a.reshape((NCHIP, 2) + a.shape[1:]) for a in arrs]
    got = _sibling_swap(four, "swap_" + tag)
    return [_pair_sum(a, g, w, f"pair_sum_{tag}{i}") for i, (a, g, w) in enumerate(zip(four, got, wires))]


def _reduce_scatter(arrs, wires, tag):
    return _chip_exchange(_pair_partials(arrs, wires, tag), "exchange_" + tag)


def _row_tile(r, c, budget=1 << 20):
    best = r
    for t in range(8, r, 8):
        if r % t == 0 and t * c * 4 <= budget:
            best = t
    if r * c * 4 <= budget:
        best = r
    return best


def _sum_chips(land):
    tot = land[0].astype(F32)
    for q in range(1, NCHIP):
        tot = tot + land[q].astype(F32)
    return tot


def _adam_math(w, g, m, v):
    m2 = ADAM_B1 * m + (1.0 - ADAM_B1) * g
    v2 = ADAM_B2 * v + (1.0 - ADAM_B2) * (g * g)
    mh = m2 / (1.0 - ADAM_B1**ADAM_STEP)
    vh = v2 / (1.0 - ADAM_B2**ADAM_STEP)
    delta = -ADAM_LR * (mh / (jnp.sqrt(vh) + ADAM_EPS) + ADAM_WD * w)
    return delta, m2, v2


def _adam(g, w, m, v, name, land=None):
    r, c = w.shape
    tr = _row_tile(r, c, budget=1 << 20)

    def body(*refs):
        if land is not None:
            l_ref, w_ref, m_ref, v_ref, g_ref, d_ref, m2_ref, v2_ref = refs
            gv = _sum_chips(l_ref)
            g_ref[...] = gv
        else:
            g_in, w_ref, m_ref, v_ref, d_ref, m2_ref, v2_ref = refs
            gv = g_in[...]
        d, m2, v2 = _adam_math(w_ref[...], gv, m_ref[...], v_ref[...])
        d_ref[...] = d
        m2_ref[...] = m2
        v2_ref[...] = v2

    blk = _rows(tr, c)
    if land is not None:
        in_specs = [pl.BlockSpec((NCHIP, tr, c), lambda i: (0, i, 0)), blk, blk, blk]
        out = pl.pallas_call(
            body, name=name, grid=(r // tr,), in_specs=in_specs, out_specs=[blk] * 4, out_shape=[_sds((r, c))] * 4, compiler_params=_cparams(1)
        )(land, w, m, v)
        return out
    out = pl.pallas_call(
        body, name=name, grid=(r // tr,), in_specs=[blk] * 4, out_specs=[blk] * 3, out_shape=[_sds((r, c))] * 3, compiler_params=_cparams(1)
    )(g, w, m, v)
    return (g,) + tuple(out)


def _sum_slots(land, name):
    _, r, c = land.shape
    tr = _row_tile(r, c, budget=1 << 20)

    def body(l_ref, o_ref):
        o_ref[...] = _sum_chips(l_ref)

    return pl.pallas_call(
        body,
        name=name,
        grid=(r // tr,),
        in_specs=[pl.BlockSpec((NCHIP, tr, c), lambda i: (0, i, 0))],
        out_specs=_rows(tr, c),
        out_shape=_sds((r, c)),
        compiler_params=_cparams(1),
    )(land)


def _mod_fwd(call, ada_w, ada_b_loc):
    def body(c_ref, w_ref, b_ref, o_ref):
        s = _silu(c_ref[...]).astype(BF16)
        o_ref[...] = _mm(s, w_ref[...].astype(BF16)) + b_ref[...]

    return pl.pallas_call(
        body, name="mod_fwd", out_shape=_sds((call.shape[0], ada_w.shape[1])), compiler_params=_cparams(0, 32 << 20)
    )(call, ada_w, ada_b_loc)


def _mod_bwd(call, dm_loc, dmc_loc, ada_w):
    d, n = ada_w.shape
    pad = call.shape[0] - NDEV - 1

    def body(c_ref, dm_ref, dmc_ref, w_ref, gw_ref, gb_ref, cp_ref):
        cv = c_ref[...]
        sg = _sigmoid(cv)
        dmc = _colsum(dmc_ref[...])
        dm = dm_ref[...]
        rows = jnp.concatenate([dm, dmc, jnp.zeros((pad, n), F32)], axis=0)
        gw_ref[...] = _mm_tn((cv * sg).astype(BF16), rows.astype(BF16))
        gb_ref[...] = _colsum(dm) + dmc
        back = _mm_nt(rows[NDEV:].astype(BF16), w_ref[...].astype(BF16))
        c8, s8 = cv[NDEV : NDEV + 1], sg[NDEV : NDEV + 1]
        cp_ref[...] = back[0:1] * (s8 * (1.0 + c8 * (1.0 - s8)))

    return pl.pallas_call(
        body, name="mod_bwd", out_shape=[_sds((d, n)), _sds((1, n)), _sds((1, d))], compiler_params=_cparams(0, 40 << 20)
    )(call, dm_loc, dmc_loc, ada_w)


def _riding(body, nin, nout, nr, grid, copies):
    def wrapped(*refs):
        start, finish = copies(refs[nin : nin + nr], refs[nin + nr + nout : nin + 2 * nr + nout], *refs[nin + 2 * nr + nout :])
        first, last = True, True
        for ax, size in enumerate(grid):
            first = jnp.logical_and(first, pl.program_id(ax) == 0)
            last = jnp.logical_and(last, pl.program_id(ax) == size - 1)
        pl.when(first)(start)
        body(*refs[:nin], *refs[nin + nr : nin + nr + nout])
        pl.when(last)(finish)

    return wrapped if nr else body


def _in_proj(xr, norm_g, scale, shift, wg, jsel, tm, name, shards=()):
    lx, d = xr.shape
    cb = wg.shape[2]
    nr = len(shards)

    def body(x_ref, g_ref, sc_ref, sh_ref, w_ref, p_ref, h_ref):
        xv = x_ref[...]
        r = lax.rsqrt(_rowmean(xv * xv) + EPS)
        hb = ((xv * r) * g_ref[...] * (1.0 + sc_ref[...]) + sh_ref[...]).astype(BF16)
        h_ref[...] = hb
        for q, j in enumerate(jsel):
            p_ref[:, q * cb : (q + 1) * cb] = _mm(hb, w_ref[j])

    vec = _whole((1, d))
    out = pl.pallas_call(
        _riding(body, 5, 2, nr, (lx // tm,), _gather_copies),
        name=name,
        grid=(lx // tm,),
        in_specs=[_rows(tm, d), vec, vec, vec, _whole(wg.shape)] + [_HBM] * nr,
        out_specs=[_rows(tm, len(jsel) * cb), _rows(tm, d)] + [_HBM] * nr,
        out_shape=[_sds((lx, len(jsel) * cb)), _sds((lx, d), BF16)] + _gather_shapes(shards),
        scratch_shapes=_gather_sems(nr) if nr else [],
        compiler_params=_cparams(1, VMEM_BIG),
    )(xr, norm_g, scale, shift, wg, *shards)
    return out[:2], out[2:]


def _in_bwd(xr, dxo, dup, du5, dypre, dz, wg, norm_g, scale, shift, d_skip, dg_init, tm):
    lx, d = xr.shape
    cb = wg.shape[2]
    pw = dup.shape[1]
    sw = du5.shape[1]
    mix = dz.shape[1]
    ncol = NDEV * cb

    def body(x_ref, dxo_ref, up_ref, d5_ref, dy_ref, dz_ref, w_ref, g_ref, sc_ref, sh_ref, dk_ref, gi_ref,
             gx_ref, dp_ref, dsc_ref, dsh_ref, dg_ref):
        i = pl.program_id(0)

        @pl.when(i == 0)
        def _():
            dsc_ref[...] = jnp.zeros_like(dsc_ref)
            dsh_ref[...] = jnp.zeros_like(dsh_ref)
            dg_ref[...] = gi_ref[...]

        dp = jnp.concatenate(
            [up_ref[...], d5_ref[...] + dk_ref[...] * dy_ref[...], dz_ref[...]], axis=1
        ).astype(BF16)
        dp_ref[...] = dp
        dh = _mm_nt(dp[:, 0:cb], w_ref[0])
        for j in range(1, NDEV):
            dh = dh + _mm_nt(dp[:, j * cb : (j + 1) * cb], w_ref[j])
        xv = x_ref[...]
        r = lax.rsqrt(_rowmean(xv * xv) + EPS)
        xh = xv * r
        g = g_ref[...]
        one_sc = 1.0 + sc_ref[...]
        dsh_ref[...] += _colsum(dh)
        dsc_ref[...] += _colsum(dh * (xh * g))
        dg_ref[...] += _colsum(dh * one_sc * xh)
        dxh = dh * one_sc * g
        gx_ref[...] = r * (dxh - xh * _rowmean(dxh * xh)) + dxo_ref[...]

    vec = _whole((1, d))
    return pl.pallas_call(
        body,
        name="in_bwd",
        grid=(lx // tm,),
        in_specs=[_rows(tm, d), _rows(tm, d), _rows(tm, pw), _rows(tm, sw), _rows(tm, sw), _rows(tm, mix), _whole(wg.shape), vec, vec, vec, _whole((1, sw)), vec],
        out_specs=[_rows(tm, d), _rows(tm, ncol), _acc((1, d)), _acc((1, d)), _acc((1, d))],
        out_shape=[_sds((lx, d)), _sds((lx, ncol), BF16), _sds((1, d)), _sds((1, d)), _sds((1, d))],
        compiler_params=_cparams(1, VMEM_BIG),
    )(xr, dxo, dup, du5, dypre, dz, wg, norm_g, scale, shift, d_skip, dg_init)


def _in_bwd_ctx(xc, duc, wg, jsel, norm_g, scale, shift):
    lc, d = xc.shape
    cb = wg.shape[2]

    def body(x_ref, du_ref, w_ref, g_ref, sc_ref, sh_ref, dp_ref, dsc_ref, dsh_ref, dg_ref):
        dp = du_ref[...].astype(BF16)
        dp_ref[...] = dp
        dh = _mm_nt(dp[:, 0:cb], w_ref[jsel[0]])
        for q in range(1, len(jsel)):
            dh = dh + _mm_nt(dp[:, q * cb : (q + 1) * cb], w_ref[jsel[q]])
        xv = x_ref[...]
        xh = xv * lax.rsqrt(_rowmean(xv * xv) + EPS)
        dsh_ref[...] = _colsum(dh)
        dsc_ref[...] = _colsum(dh * (xh * g_ref[...]))
        dg_ref[...] = _colsum(dh * (1.0 + sc_ref[...]) * xh)

    return pl.pallas_call(
        body,
        name="in_bwd_ctx",
        out_shape=[_sds(duc.shape, BF16), _sds((1, d)), _sds((1, d)), _sds((1, d))],
        compiler_params=_cparams(0, VMEM_BIG),
    )(xc, duc, wg, norm_g, scale, shift)


def _in_w_grad(hb, dpb, hcb, dpcb, cb, tm, parts=()):
    lx, d = hb.shape
    lc = hcb.shape[0]
    nb = lx // tm
    cg = 2 * cb
    nr = len(parts)
    grid = (NDEV // 2, nb + 1)

    def body(h_ref, dp_ref, hc_ref, dpc_ref, o_ref):
        j, i = pl.program_id(0), pl.program_id(1)

        @pl.when(i == 0)
        def _():
            o_ref[...] = jnp.zeros_like(o_ref)

        @pl.when(i < nb)
        def _():
            h = h_ref[...]
            o_ref[0] += _mm_tn(h, dp_ref[:, 0:cb])
            o_ref[1] += _mm_tn(h, dp_ref[:, cb:cg])

        @pl.when(jnp.logical_and(i == nb, j == 1))
        def _():
            h = hc_ref[...]
            o_ref[0] += _mm_tn(h, dpc_ref[:, 0:cb])
            o_ref[1] += _mm_tn(h, dpc_ref[:, cb:cg])

    out = pl.pallas_call(
        _riding(body, 4, 1, nr, grid, _exchange_copies),
        name="in_w_grad",
        grid=grid,
        in_specs=[
            pl.BlockSpec((tm, d), lambda j, i: (jnp.minimum(i, nb - 1), 0)),
            pl.BlockSpec((tm, cg), lambda j, i: (jnp.minimum(i, nb - 1), j)),
            pl.BlockSpec((lc, d), lambda j, i: (0, 0)),
            pl.BlockSpec((lc, cg), lambda j, i: (0, 0)),
        ]
        + [_HBM] * nr,
        out_specs=[pl.BlockSpec((2, d, cb), lambda j, i: (j, 0, 0))] + [_HBM] * nr,
        out_shape=[_sds((NDEV, d, cb))] + [_sds(p.shape, p.dtype) for p in parts],
        scratch_shapes=_exchange_sems(nr) if nr else [],
        compiler_params=_cparams(2, VMEM_BIG),
    )(hb, dpb, hcb, dpcb, *parts)
    return out[0], out[1:]


def _pool_tables(w, rows, rb, pgw, transpose):
    t = np.arange(GRID_W)
    lo, hi = np.clip(t - w // 2, 0, GRID_W), np.clip(t + w - w // 2, 0, GRID_W)
    band = ((t[None, :] >= lo[:, None]) & (t[None, :] < hi[:, None])).astype(np.float32)
    if transpose:
        band = band.T
    kc = np.kron(np.eye(rb, dtype=np.float32), band)
    inv_c = np.tile((1.0 / (hi - lo).astype(np.float32))[:, None], (rb, pgw)).astype(np.float32)
    r = np.arange(rows)
    cnt_r = np.clip(r + w - w // 2, 0, rows) - np.clip(r - w // 2, 0, rows)
    inv_r = (1.0 / cnt_r.astype(np.float32)).astype(np.float32)
    return jnp.asarray(kc, BF16), jnp.asarray(inv_r), jnp.asarray(inv_c)


def _pool_stack(rows, rb, pgw, transpose):
    tabs = [_pool_tables(w, rows, rb, pgw, transpose) for w in POOL_WINDOWS]
    return tuple(jnp.stack([t[i] for t in tabs]) for i in range(3))


def _per_window(group):
    for k, w in enumerate(POOL_WINDOWS):
        pl.when(pl.program_id(0) == k)(lambda k=k, w=w: group(k, w))


def _pool_fwd(proj, pool_wf):
    lx = proj.shape[0]
    ng, pgw, _ = pool_wf.shape
    rows = lx // GRID_W
    rb = min(4, rows)
    tok = rb * GRID_W
    kc, inv_r, inv_c = _pool_stack(rows, rb, pgw, False)

    def body(p_ref, w_ref, kc_ref, ir_ref, ic_ref, lin_ref, dm_ref, xp):
        def group(kk, w):
            lo = w // 2
            xp[pl.ds(0, lo * GRID_W), :] = jnp.zeros((lo * GRID_W, pgw), F32)
            xp[pl.ds((lo + rows) * GRID_W, (w - lo) * GRID_W), :] = jnp.zeros(((w - lo) * GRID_W, pgw), F32)
            xp[pl.ds(lo * GRID_W, lx), :] = p_ref[...]

            def blk(b, carry):
                r0 = b * rb
                parts = []
                for rr in range(rb):
                    base = pl.multiple_of((r0 + rr) * GRID_W, GRID_W)
                    s = xp[pl.ds(base, GRID_W), :]
                    for k in range(1, w):
                        s = s + xp[pl.ds(base + k * GRID_W, GRID_W), :]
                    parts.append(s * ir_ref[kk, r0 + rr])
                m = _mm_split(kc_ref[0], jnp.concatenate(parts, axis=0)) * ic_ref[0]
                u = xp[pl.ds(pl.multiple_of((r0 + lo) * GRID_W, GRID_W), tok), :]
                db = (m - u).astype(BF16)
                o0 = pl.multiple_of(r0 * GRID_W, GRID_W)
                dm_ref[pl.ds(o0, tok), :] = db
                lin_ref[pl.ds(o0, tok), :] = _mm(db, w_ref[0])
                return carry

            lax.fori_loop(0, rows // rb, blk, 0)

        _per_window(group)

    col = pl.BlockSpec((lx, pgw), lambda g: (0, g))
    return pl.pallas_call(
        body,
        name="pool_fwd",
        grid=(ng,),
        in_specs=[col, _gspec(1, pgw, pgw), _gspec(1, tok, tok), _SMEM, _gspec(1, tok, pgw)],
        out_specs=[col, col],
        out_shape=[_sds((lx, ng * pgw)), _sds((lx, ng * pgw), BF16)],
        scratch_shapes=[pltpu.VMEM(((rows + max(POOL_WINDOWS)) * GRID_W, pgw), F32)],
        compiler_params=_cparams(1, VMEM_BIG),
    )(proj, pool_wf, kc, inv_r, inv_c)


def _pool_bwd(dlin, pool_wf, dmat):
    lx = dlin.shape[0]
    ng, pgw, _ = pool_wf.shape
    rows = lx // GRID_W
    rb = min(4, rows)
    tok = rb * GRID_W
    kct, inv_r, inv_c = _pool_stack(rows, rb, pgw, True)

    def body(dl_ref, w_ref, kc_ref, ir_ref, ic_ref, dm_ref, du_ref, dw_ref, tp):
        def group(kk, w):
            front = w - w // 2 - 1
            if front:
                tp[pl.ds(0, front * GRID_W), :] = jnp.zeros((front * GRID_W, pgw), F32)
            tp[pl.ds((front + rows) * GRID_W, (w - front) * GRID_W), :] = jnp.zeros(((w - front) * GRID_W, pgw), F32)
            dw_ref[...] = jnp.zeros_like(dw_ref)

            def blk(b, carry):
                o0 = pl.multiple_of(b * tok, GRID_W)
                dlb = dl_ref[pl.ds(o0, tok), :].astype(BF16)
                dw_ref[0] += _mm_tn(dm_ref[pl.ds(o0, tok), :], dlb)
                dd = _mm_nt(dlb, w_ref[0])
                du_ref[pl.ds(o0, tok), :] = -dd
                t = _mm_split(kc_ref[0], dd * ic_ref[0])
                for rr in range(rb):
                    dst = pl.multiple_of((b * rb + rr + front) * GRID_W, GRID_W)
                    tp[pl.ds(dst, GRID_W), :] = t[rr * GRID_W : (rr + 1) * GRID_W] * ir_ref[kk, b * rb + rr]
                return carry

            lax.fori_loop(0, rows // rb, blk, 0)

            def rowl(r, carry):
                base = pl.multiple_of(r * GRID_W, GRID_W)
                s = tp[pl.ds(base, GRID_W), :]
                for k in range(1, w):
                    s = s + tp[pl.ds(base + k * GRID_W, GRID_W), :]
                du_ref[pl.ds(base, GRID_W), :] += s
                return carry

            lax.fori_loop(0, rows, rowl, 0)

        _per_window(group)

    col = pl.BlockSpec((lx, pgw), lambda g: (0, g))
    return pl.pallas_call(
        body,
        name="pool_bwd",
        grid=(ng,),
        in_specs=[col, _gspec(1, pgw, pgw), _gspec(1, tok, tok), _SMEM, _gspec(1, tok, pgw), col],
        out_specs=[col, _gspec(1, pgw, pgw)],
        out_shape=[_sds((lx, ng * pgw)), _sds((ng, pgw, pgw))],
        scratch_shapes=[pltpu.VMEM(((rows + max(POOL_WINDOWS)) * GRID_W, pgw), F32)],
        compiler_params=_cparams(1, VMEM_BIG),
    )(dlin, pool_wf, kct, inv_r, inv_c, dmat)


def _s5_small(lam_re, lam_im, log_dt, b_re, b_im, c_re, c_im):
    t = CHUNK_T
    dt = jnp.exp(log_dt)[..., None]
    zr, zi = lam_re * dt, lam_im * dt
    tau = jnp.arange(t + 1, dtype=F32)
    mag = jnp.exp(zr[..., None] * tau)
    pr, pi = mag * jnp.cos(zi[..., None] * tau), mag * jnp.sin(zi[..., None] * tau)
    ar, ai = pr[..., 1], pi[..., 1]
    den = lam_re * lam_re + lam_im * lam_im
    qr = ((ar - 1.0) * lam_re + ai * lam_im) / den
    qi = (ai * lam_re - (ar - 1.0) * lam_im) / den
    bbr = qr[..., None] * b_re - qi[..., None] * b_im
    bbi = qr[..., None] * b_im + qi[..., None] * b_re
    ctr, cti = jnp.swapaxes(c_re, 2, 3), jnp.swapaxes(c_im, 2, 3)
    lane_pad = lambda v: jnp.pad(jnp.swapaxes(v, 0, 1), ((0, 0), (0, 0), (0, 0), (0, LANES - v.shape[-1])))
    tp = lambda v: jnp.swapaxes(v, 1, 2)
    flip = lambda v: v[..., ::-1]
    pfr, pfi = tp(flip(pr[0, ..., :t])), tp(flip(pi[0, ..., :t]))
    pbr, pbi = tp(pr[1, ..., :t]), tp(pi[1, ..., :t])
    bfr, bfi, bbr_t, bbi_t = tp(bbr[0]), tp(bbi[0]), tp(bbr[1]), tp(bbi[1])
    a1 = jnp.concatenate([pfr, pbr, pfr, pbr], axis=-1)
    a2 = jnp.concatenate([-pfi, -pbi, pfi, pbi], axis=-1)
    b1 = jnp.concatenate([bfr, bbr_t, bfi, bbi_t], axis=-1)
    b2 = jnp.concatenate([bfi, bbi_t, bfr, bbr_t], axis=-1)
    a16 = jnp.concatenate([pr[0, ..., t], pr[1, ..., t], pi[0, ..., t], pi[1, ..., t]], axis=1)
    return (lane_pad(pr), lane_pad(pi), lane_pad(bbr), lane_pad(bbi), lane_pad(ctr), lane_pad(cti), a1, a2, b1, b2), a16


def _split2(v):
    hi = v.astype(BF16)
    return hi, (v - hi.astype(F32)).astype(BF16)


def _dotx(a, b, mm=_mm):
    ah, al = (a, None) if a.dtype == BF16 else _split2(a)
    bh, bl = (b, None) if b.dtype == BF16 else _split2(b)
    out = mm(ah, bh)
    if bl is not None:
        out = out + mm(ah, bl)
    if al is not None:
        out = out + mm(al, bh)
    return out


def _shift_lanes(v, k):
    if k == 0:
        return v
    n = v.shape[-1]
    lane = lax.broadcasted_iota(jnp.int32, v.shape, v.ndim - 1)
    r = pltpu.roll(v, k % n, axis=v.ndim - 1)
    return jnp.where(lane >= k, r, 0.0) if k > 0 else jnp.where(lane < n + k, r, 0.0)


def _op_consts():
    t, h = CHUNK_T, SSM_H
    j, o = np.divmod(np.arange(t * h), h)
    p = np.arange(LANES)[:, None]
    sel = lambda col: (p == col[None, :]).astype(np.float32)
    pw = np.stack([np.concatenate([sel(j), sel(j + 1)], axis=1), np.concatenate([sel(t - 1 - j), sel(t - j)], axis=1)])
    return jnp.asarray(pw, BF16), jnp.asarray(sel(o), BF16), jnp.asarray(sel(j).T, BF16), jnp.asarray(sel(o).T, BF16)


def _op_factors(gi, pr_ref, pi_ref, br_ref, bi_ref, cr_ref, ci_ref, pw_ref, ch_ref):
    n = pr_ref.shape[2]
    c16 = _dotx(jnp.concatenate([cr_ref[gi, 0], ci_ref[gi, 0], cr_ref[gi, 1], ci_ref[gi, 1]], axis=0), ch_ref[...])
    out = []
    for d in range(2):
        e = _dotx(jnp.concatenate([pr_ref[gi, d], pi_ref[gi, d]], axis=0), pw_ref[d])
        c = c16[2 * d * n : (2 * d + 1) * n], c16[(2 * d + 1) * n : (2 * d + 2) * n]
        bst = jnp.concatenate([br_ref[gi, d], -bi_ref[gi, d]], axis=0)
        out.append(((e[:n, :CW], e[n:, :CW]), (e[:n, CW:], e[n:, CW:]), c, bst))
    return out


def _op_spread(m_ref, u_ref, v_ref, gi):
    return _dotx(m_ref[...], jnp.concatenate([_rows_pad(u_ref[gi]), _rows_pad(v_ref[gi])], axis=1))


def _cmul(x, y):
    return x[0] * y[0] - x[1] * y[1], x[0] * y[1] + x[1] * y[0]


def _cmul_conj(g, y):
    return g[0] * y[0] + g[1] * y[1], g[1] * y[0] - g[0] * y[1]


def _rows_pad(v):
    return jnp.concatenate([v, jnp.zeros((LANES - v.shape[0], v.shape[1]), F32)], axis=0)


def _s5_build(small, gb, shards):
    pr, pi, br, bi, cr, ci, a1, a2, b1, b2 = small
    g, _, n, _ = pr.shape
    t, h, ns = CHUNK_T, SSM_H, 4 * n
    pw, ch, rep_s, til_s = _op_consts()
    nr, nin, steps = len(shards), 14, g // gb

    def body(*refs):
        pr_ref, pi_ref, br_ref, bi_ref, cr_ref, ci_ref, a1_ref, a2_ref, b1_ref, b2_ref, pw_ref, ch_ref, rs_ref, ts_ref = refs[:nin]
        t_ref, w_ref, c_ref = refs[nin + nr : nin + nr + 3]
        start, finish = _gather_copies(refs[nin : nin + nr], refs[nin + nr + 3 : nin + 2 * nr + 3], *refs[nin + 2 * nr + 3 :])
        pl.when(pl.program_id(0) == 0)(start)

        def group(gi, carry):
            rows, outs = [], []
            for e16, e1, c16, bst in _op_factors(gi, pr_ref, pi_ref, br_ref, bi_ref, cr_ref, ci_ref, pw_ref, ch_ref):
                rows.append(_dotx(bst, jnp.concatenate(_cmul(c16, e16), axis=0), _mm_tn)[:h])
                outs.append(_cmul(c16, e1))
            c_ref[gi] = jnp.concatenate([outs[0][0], outs[1][0], -outs[0][1], -outs[1][1]], axis=0).astype(BF16)
            for s in range(t):
                t_ref[gi, s * h : (s + 1) * h, :] = (_shift_lanes(rows[0], h * s) + _shift_lanes(rows[1], -h * (t - 1 - s))).astype(BF16)
            ae, be = _op_spread(rs_ref, a1_ref, a2_ref, gi), _op_spread(ts_ref, b1_ref, b2_ref, gi)
            w_ref[gi] = (ae[:, :ns] * be[:, :ns] + ae[:, ns:] * be[:, ns:]).astype(BF16)
            return carry

        lax.fori_loop(0, gb, group, 0)
        pl.when(pl.program_id(0) == steps - 1)(finish)

    sm = pl.BlockSpec((gb, 2, n, LANES), lambda i: (i, 0, 0, 0))
    ab = lambda v: pl.BlockSpec((gb,) + v.shape[1:], lambda i: (i, 0, 0))
    out = pl.pallas_call(
        body,
        name="s5_build",
        grid=(steps,),
        in_specs=[sm] * 6 + [ab(a1), ab(a2), ab(b1), ab(b2), _whole(pw.shape), _whole(ch.shape), _whole(rep_s.shape), _whole(til_s.shape)] + [_HBM] * nr,
        out_specs=[_gspec(gb, CW, CW), _gspec(gb, CW, 4 * n), _gspec(gb, 4 * n, CW)] + [_HBM] * nr,
        out_shape=[_sds((g, CW, CW), BF16), _sds((g, CW, 4 * n), BF16), _sds((g, 4 * n, CW), BF16)] + _gather_shapes(shards),
        scratch_shapes=_gather_sems(nr),
        compiler_params=_cparams(1, VMEM_BIG),
    )(pr, pi, br, bi, cr, ci, a1, a2, b1, b2, pw, ch, rep_s, til_s, *shards)
    return out[:3], out[3:]


def _s5_build_bwd(small, dtsum, dwst, dccat, gb, parts):
    pr, pi, br, bi, cr, ci, a1, a2, b1, b2 = small
    g, _, n, _ = pr.shape
    t, h, ns = CHUNK_T, SSM_H, 4 * n
    pw, ch, rep_s, til_s = _op_consts()
    nr, nin, nout, steps = len(parts), 17, 10, g // gb

    def body(*refs):
        pr_ref, pi_ref, br_ref, bi_ref, cr_ref, ci_ref, a1_ref, a2_ref, b1_ref, b2_ref, pw_ref, ch_ref, rs_ref, ts_ref, dt_ref, dw_ref, dc_ref = refs[:nin]
        dpr_ref, dpi_ref, dbr_ref, dbi_ref, dcr_ref, dci_ref, da1_ref, da2_ref, db1_ref, db2_ref = refs[nin + nr : nin + nr + nout]
        start, finish = _exchange_copies(refs[nin : nin + nr], refs[nin + nr + nout : nin + 2 * nr + nout], *refs[nin + 2 * nr + nout :])
        pl.when(pl.program_id(0) == 0)(start)

        def group(gi, carry):
            drow = [jnp.zeros((h, CW), F32), jnp.zeros((h, CW), F32)]
            for s in range(t):
                blk = dt_ref[gi, s * h : (s + 1) * h, :]
                drow[0] = drow[0] + _shift_lanes(blk, -h * s)
                drow[1] = drow[1] + _shift_lanes(blk, h * (t - 1 - s))
            dcc = dc_ref[gi]
            dcs = []
            for d, (e16, e1, c16, bst) in enumerate(_op_factors(gi, pr_ref, pi_ref, br_ref, bi_ref, cr_ref, ci_ref, pw_ref, ch_ref)):
                dr = _rows_pad(drow[d])
                db = _dotx(jnp.concatenate(_cmul(c16, e16), axis=0), dr, _mm_nt)
                dbr_ref[gi, d] = db[:n]
                dbi_ref[gi, d] = -db[n:]
                dst = _dotx(bst, dr)
                dca = dst[:n], dst[n:]
                gx = dcc[d * n : (d + 1) * n], -dcc[(2 + d) * n : (3 + d) * n]
                dc16, dc16b = _cmul_conj(dca, e16), _cmul_conj(gx, e1)
                dcs += [dc16[0] + dc16b[0], dc16[1] + dc16b[1]]
                de = jnp.concatenate([jnp.concatenate(_cmul_conj(dca, c16), axis=0), jnp.concatenate(_cmul_conj(gx, c16), axis=0)], axis=1)
                dp = _dotx(de, pw_ref[d], _mm_nt)
                dpr_ref[gi, d] = dp[:n]
                dpi_ref[gi, d] = dp[n:]
            dct = _dotx(jnp.concatenate(dcs, axis=0), ch_ref[...], _mm_nt)
            for d in range(2):
                dcr_ref[gi, d] = dct[2 * d * n : (2 * d + 1) * n]
                dci_ref[gi, d] = dct[(2 * d + 1) * n : (2 * d + 2) * n]
            dw = dw_ref[gi]
            ae, be = _op_spread(rs_ref, a1_ref, a2_ref, gi), _op_spread(ts_ref, b1_ref, b2_ref, gi)
            da = _dotx(rs_ref[...], jnp.concatenate([dw * be[:, :ns], dw * be[:, ns:]], axis=1), _mm_tn)
            db = _dotx(ts_ref[...], jnp.concatenate([dw * ae[:, :ns], dw * ae[:, ns:]], axis=1), _mm_tn)
            da1_ref[gi] = da[:t, :ns]
            da2_ref[gi] = da[:t, ns:]
            db1_ref[gi] = db[:h, :ns]
            db2_ref[gi] = db[:h, ns:]
            return carry

        lax.fori_loop(0, gb, group, 0)
        pl.when(pl.program_id(0) == steps - 1)(finish)

    sm = pl.BlockSpec((gb, 2, n, LANES), lambda i: (i, 0, 0, 0))
    ab = lambda v: pl.BlockSpec((gb,) + v.shape[1:], lambda i: (i, 0, 0))
    ops = [_gspec(gb, CW, CW), _gspec(gb, CW, 4 * n), _gspec(gb, 4 * n, CW)]
    out = pl.pallas_call(
        body,
        name="s5_build_bwd",
        grid=(steps,),
        in_specs=[sm] * 6 + [ab(a1), ab(a2), ab(b1), ab(b2), _whole(pw.shape), _whole(ch.shape), _whole(rep_s.shape), _whole(til_s.shape)] + ops + [_HBM] * nr,
        out_specs=[sm] * 6 + [ab(a1), ab(a2), ab(b1), ab(b2)] + [_HBM] * nr,
        out_shape=[_sds(v.shape) for v in small] + [_sds(p.shape, p.dtype) for p in parts],
        scratch_shapes=_exchange_sems(nr),
        compiler_params=_cparams(1, VMEM_BIG),
    )(pr, pi, br, bi, cr, ci, a1, a2, b1, b2, pw, ch, rep_s, til_s, dtsum, dwst, dccat, *parts)
    return out[:nout], out[nout:]


SCAN_UNROLL = 4
GPL = LANES // SSM_H
CW = CHUNK_T * SSM_H
BW = CHUNK_T * LANES


def _chunk_perm():
    o = np.arange(BW)
    src = ((o % CW) // SSM_H) * LANES + (o // CW) * SSM_H + o % SSM_H
    return jnp.asarray(np.arange(BW)[:, None] == src[None, :], BF16)


def _gspec(gb, a, b):
    return pl.BlockSpec((gb, a, b), lambda i: (i, 0, 0))


def _chunk_rows(ref, n):
    return jnp.concatenate([ref[pl.ds(s, n, stride=CHUNK_T), :] for s in range(CHUNK_T)], axis=1)


def _s5_state(proj, col0, uc, perm, wst, ncc, ncl):
    nlb = uc.shape[1] // LANES
    g, _, ns = wst.shape
    nch = ncc + ncl
    half = ns // 2

    def body(p_ref, c_ref, pm_ref, w_ref, u_ref, sr_ref, si_ref):
        xrows = jnp.concatenate([_chunk_rows(c_ref, ncc), _chunk_rows(p_ref, ncl)], axis=0).astype(BF16)
        u = _mm(xrows, pm_ref[...]).astype(BF16)
        u_ref[0] = u
        for gi in range(GPL):
            s = _mm(u[:, gi * CW : (gi + 1) * CW], w_ref[gi].astype(BF16))
            sr_ref[gi] = s[:, :half]
            si_ref[gi] = s[:, half:]

    return pl.pallas_call(
        body,
        name="s5_state",
        grid=(nlb,),
        in_specs=[
            pl.BlockSpec((ncl * CHUNK_T, LANES), lambda i: (0, col0 + i)),
            pl.BlockSpec((ncc * CHUNK_T, LANES), lambda i: (0, i)),
            _whole(perm.shape),
            _gspec(GPL, CW, ns),
        ],
        out_specs=[pl.BlockSpec((1, nch, BW), lambda i: (i, 0, 0)), _gspec(GPL, nch, half), _gspec(GPL, nch, half)],
        out_shape=[_sds((nlb, nch, BW), BF16), _sds((g, nch, half)), _sds((g, nch, half))],
        compiler_params=_cparams(1, VMEM_BIG),
    )(proj, uc, perm, wst)


def _unrolled_loop(n, step, init):
    u = SCAN_UNROLL if n % SCAN_UNROLL == 0 else 1

    def trip(i, c):
        for k in range(u):
            c = step(i * u + k, c)
        return c

    return lax.fori_loop(0, n // u, trip, init)


def _lane_masks(gb, half):
    lane = lax.broadcasted_iota(jnp.int32, (gb, half), 1)
    return lane < (half // 2)


def _s5_scan_fwd(s_re, s_im, a16, ncc, nch, gb):
    g, ns = a16.shape
    half = ns // 2

    def body(sr_ref, si_ref, a_ref, hr_ref, hi_ref):
        ar, ai = a_ref[:, :half], a_ref[:, half:]
        mf = _lane_masks(gb, half)
        zero = jnp.zeros((gb, half), F32)
        hr_ref[...] = jnp.zeros_like(hr_ref)
        hi_ref[...] = jnp.zeros_like(hi_ref)

        def step(t, c):
            fr, fi, br, bi = c
            rf = pl.ds(t, gb, stride=nch)
            rb = pl.ds(jnp.where(t < ncc, ncc - 1 - t, nch - 1 - (t - ncc)), gb, stride=nch)
            hr_ref[rf, :] += jnp.where(mf, fr, 0.0)
            hi_ref[rf, :] += jnp.where(mf, fi, 0.0)
            hr_ref[rb, :] += jnp.where(mf, 0.0, br)
            hi_ref[rb, :] += jnp.where(mf, 0.0, bi)
            return (
                ar * fr - ai * fi + sr_ref[rf, :],
                ar * fi + ai * fr + si_ref[rf, :],
                ar * br - ai * bi + sr_ref[rb, :],
                ar * bi + ai * br + si_ref[rb, :],
            )

        _unrolled_loop(nch, step, (zero, zero, zero, zero))

    blk = pl.BlockSpec((gb * nch, half), lambda i: (i, 0))
    return pl.pallas_call(
        body,
        name="s5_scan_fwd",
        grid=(g // gb,),
        in_specs=[blk, blk, pl.BlockSpec((gb, ns), lambda i: (i, 0))],
        out_specs=[blk, blk],
        out_shape=[_sds((g * nch, half))] * 2,
        compiler_params=_cparams(1, VMEM_BIG),
    )(s_re, s_im, a16)


def _token_rows_store(ref, val, n):
    for s in range(CHUNK_T):
        ref[pl.ds(s, n, stride=CHUNK_T), :] = val[:, s * LANES : (s + 1) * LANES]


def _s5_out(u_all, h_re, h_im, tsum, ccat, perm, ncc, lx):
    nlb, nch, _ = u_all.shape
    g, ns, _ = ccat.shape
    ncl = nch - ncc
    half = ns // 2

    def body(u_ref, hr_ref, hi_ref, t_ref, c_ref, pm_ref, y_ref):
        parts = []
        for gi in range(GPL):
            u = u_ref[0, ncc:, gi * CW : (gi + 1) * CW]
            hs = jnp.concatenate([hr_ref[gi, ncc:, :], hi_ref[gi, ncc:, :]], axis=1).astype(BF16)
            parts.append(_mm(u, t_ref[gi].astype(BF16)) + _mm(hs, c_ref[gi].astype(BF16)))
        _token_rows_store(y_ref, _mm_nt(jnp.concatenate(parts, axis=1).astype(BF16), pm_ref[...]), ncl)

    return pl.pallas_call(
        body,
        name="s5_out",
        grid=(nlb,),
        in_specs=[pl.BlockSpec((1, nch, BW), lambda i: (i, 0, 0)), _gspec(GPL, nch, half), _gspec(GPL, nch, half), _gspec(GPL, CW, CW), _gspec(GPL, ns, CW), _whole(perm.shape)],
        out_specs=pl.BlockSpec((lx, LANES), lambda i: (0, i)),
        out_shape=_sds((lx, nlb * LANES)),
        compiler_params=_cparams(1, VMEM_BIG),
    )(u_all, h_re, h_im, tsum, ccat, perm)


def _s5_dstate(dypre, perm, ccat, ncc):
    lx, sw = dypre.shape
    nlb = sw // LANES
    g, ns, _ = ccat.shape
    ncl = lx // CHUNK_T
    nch = ncl + ncc
    half = ns // 2

    def body(dy_ref, pm_ref, c_ref, dyr_ref, dr_ref, di_ref):
        dy = _mm(_chunk_rows(dy_ref, ncl).astype(BF16), pm_ref[...]).astype(BF16)
        dyr_ref[0] = dy
        for gi in range(GPL):
            dh = _mm_nt(dy[:, gi * CW : (gi + 1) * CW], c_ref[gi].astype(BF16))
            dr_ref[gi, :ncc, :] = jnp.zeros((ncc, half), F32)
            di_ref[gi, :ncc, :] = jnp.zeros((ncc, half), F32)
            dr_ref[gi, ncc:, :] = dh[:, :half]
            di_ref[gi, ncc:, :] = dh[:, half:]

    return pl.pallas_call(
        body,
        name="s5_dstate",
        grid=(nlb,),
        in_specs=[pl.BlockSpec((lx, LANES), lambda i: (0, i)), _whole(perm.shape), _gspec(GPL, ns, CW)],
        out_specs=[pl.BlockSpec((1, ncl, BW), lambda i: (i, 0, 0)), _gspec(GPL, nch, half), _gspec(GPL, nch, half)],
        out_shape=[_sds((nlb, ncl, BW), BF16), _sds((g, nch, half)), _sds((g, nch, half))],
        compiler_params=_cparams(1, VMEM_BIG),
    )(dypre, perm, ccat)


def _s5_scan_bwd(dh_re, dh_im, a16, ncc, nch, gb):
    g, ns = a16.shape
    half = ns // 2
    ncl = nch - ncc

    def body(dr_ref, di_ref, a_ref, sr_ref, si_ref):
        ar, ai = a_ref[:, :half], a_ref[:, half:]
        mf = _lane_masks(gb, half)
        zero = jnp.zeros((gb, half), F32)
        sr_ref[...] = jnp.zeros_like(sr_ref)
        si_ref[...] = jnp.zeros_like(si_ref)

        def step(t, c):
            lfr, lfi, lbr, lbi = c
            pf = pl.ds(nch - 1 - t, gb, stride=nch)
            pb = pl.ds(jnp.where(t < ncl, ncc + t, t - ncl), gb, stride=nch)
            sr_ref[pf, :] += jnp.where(mf, lfr, 0.0)
            si_ref[pf, :] += jnp.where(mf, lfi, 0.0)
            sr_ref[pb, :] += jnp.where(mf, 0.0, lbr)
            si_ref[pb, :] += jnp.where(mf, 0.0, lbi)
            return (
                dr_ref[pf, :] + ar * lfr + ai * lfi,
                di_ref[pf, :] + ar * lfi - ai * lfr,
                dr_ref[pb, :] + ar * lbr + ai * lbi,
                di_ref[pb, :] + ar * lbi - ai * lbr,
            )

        _unrolled_loop(nch, step, (zero,) * 4)

    blk = pl.BlockSpec((gb * nch, half), lambda i: (i, 0))
    return pl.pallas_call(
        body,
        name="s5_scan_bwd",
        grid=(g // gb,),
        in_specs=[blk, blk, pl.BlockSpec((gb, ns), lambda i: (i, 0))],
        out_specs=[blk, blk],
        out_shape=[_sds((g * nch, half))] * 2,
        compiler_params=_cparams(1, VMEM_BIG),
    )(dh_re, dh_im, a16)


def _s5_du(dy_all, ds_re, ds_im, tsum, wst, perm, ncc, lx, lc):
    nlb, ncl, _ = dy_all.shape
    g, _, ns = wst.shape
    nch = ncc + ncl
    half = ns // 2

    def body(dy_ref, sr_ref, si_ref, t_ref, w_ref, pm_ref, du_ref, dc_ref):
        parts = []
        for gi in range(GPL):
            ds = jnp.concatenate([sr_ref[gi], si_ref[gi]], axis=1).astype(BF16)
            d_all = _mm_nt(ds, w_ref[gi].astype(BF16))
            d_lat = d_all[ncc:] + _mm_nt(dy_ref[0, :, gi * CW : (gi + 1) * CW], t_ref[gi].astype(BF16))
            parts.append(jnp.concatenate([d_all[:ncc], d_lat], axis=0))
        du = _mm_nt(jnp.concatenate(parts, axis=1).astype(BF16), pm_ref[...])
        _token_rows_store(dc_ref, du[:ncc], ncc)
        _token_rows_store(du_ref, du[ncc:], ncl)

    return pl.pallas_call(
        body,
        name="s5_du",
        grid=(nlb,),
        in_specs=[pl.BlockSpec((1, ncl, BW), lambda i: (i, 0, 0)), _gspec(GPL, nch, half), _gspec(GPL, nch, half), _gspec(GPL, CW, CW), _gspec(GPL, CW, ns), _whole(perm.shape)],
        out_specs=[pl.BlockSpec((lx, LANES), lambda i: (0, i)), pl.BlockSpec((lc, LANES), lambda i: (0, i))],
        out_shape=[_sds((lx, nlb * LANES)), _sds((lc, nlb * LANES))],
        compiler_params=_cparams(1, VMEM_BIG),
    )(dy_all, ds_re, ds_im, tsum, wst, perm)


def _s5_op_grads(u_all, dy_all, ds_re, ds_im, h_re, h_im, ncc):
    nlb, nch, _ = u_all.shape
    g, _, half = ds_re.shape
    ns = 2 * half

    def body(u_ref, dy_ref, sr_ref, si_ref, hr_ref, hi_ref, dw_ref, dt_ref, dc_ref, da_ref):
        das = []
        for gi in range(GPL):
            cols = slice(gi * CW, (gi + 1) * CW)
            sr, si, hr, hi = sr_ref[gi], si_ref[gi], hr_ref[gi], hi_ref[gi]
            ds = jnp.concatenate([sr, si], axis=1).astype(BF16)
            dy = dy_ref[0, :, cols]
            dw_ref[gi] = _mm_tn(u_ref[0, :, cols], ds)
            dt_ref[gi] = _mm_tn(u_ref[0, ncc:, cols], dy)
            hs = jnp.concatenate([hr[ncc:], hi[ncc:]], axis=1).astype(BF16)
            dc_ref[gi] = _mm_tn(hs, dy)
            das.append(jnp.concatenate([_colsum(hr * sr + hi * si), _colsum(hr * si - hi * sr)], axis=1))
        da_ref[...] = jnp.concatenate(das, axis=0)

    return pl.pallas_call(
        body,
        name="s5_op_grads",
        grid=(nlb,),
        in_specs=[pl.BlockSpec((1, nch, BW), lambda i: (i, 0, 0)), pl.BlockSpec((1, nch - ncc, BW), lambda i: (i, 0, 0))] + [_gspec(GPL, nch, half)] * 4,
        out_specs=[_gspec(GPL, CW, ns), _gspec(GPL, CW, CW), _gspec(GPL, ns, CW), pl.BlockSpec((GPL, ns), lambda i: (i, 0))],
        out_shape=[_sds((g, CW, ns)), _sds((g, CW, CW)), _sds((g, ns, CW)), _sds((g, ns))],
        compiler_params=_cparams(1, VMEM_BIG),
    )(u_all, dy_all, ds_re, ds_im, h_re, h_im)


def _merge_fwd(y_ssm, proj, lin, x, tgt, glu_wg, out_w, d_skip, glu_b, pscale, gate, fin_g, tm):
    lx, d = x.shape
    sw = y_ssm.shape[1]
    mix = out_w.shape[0]
    pw = mix - sw
    cb2 = glu_wg.shape[2]
    nh = NDEV // 2

    def body(y_ref, u_ref, z_ref, l_ref, x_ref, t_ref, gw_ref, ow_ref, dk_ref, gb_ref, ps_ref, gt_ref, fg_ref,
             ypre_ref, y12_ref, br_ref, dxo_ref, loss_ref, dfg_ref, dgt_ref):
        @pl.when(pl.program_id(0) == 0)
        def _():
            loss_ref[...] = jnp.zeros_like(loss_ref)
            dfg_ref[...] = jnp.zeros_like(dfg_ref)
            dgt_ref[...] = jnp.zeros_like(dgt_ref)

        ypre = y_ref[...] + dk_ref[...] * u_ref[...]
        ypre_ref[...] = ypre
        yg = _gelu(ypre).astype(BF16)
        outs = []
        for j in range(nh):
            y1 = _mm(yg, gw_ref[j]) + gb_ref[:, j * cb2 : (j + 1) * cb2]
            y2 = _mm(yg, gw_ref[nh + j]) + gb_ref[:, (nh + j) * cb2 : (nh + j + 1) * cb2]
            y12_ref[:, j * cb2 : (j + 1) * cb2] = y1
            y12_ref[:, (nh + j) * cb2 : (nh + j + 1) * cb2] = y2
            outs.append(y1 * _sigmoid(y2))
        lin = l_ref[...] * ps_ref[...]
        brb = (jnp.concatenate([lin] + outs, axis=1) * _silu(z_ref[...])).astype(BF16)
        br_ref[...] = brb
        mixv = _mm(brb, ow_ref[...])
        xo = x_ref[...] + gt_ref[...] * mixv
        r2 = lax.rsqrt(_rowmean(xo * xo) + EPS)
        xh = xo * r2
        fg = fg_ref[...]
        e = xh * fg - t_ref[...]
        loss_ref[...] += 0.5 * _colsum(_rowmean(e * e))
        dy = e * (1.0 / d)
        dfg_ref[...] += _colsum(dy * xh)
        gy = dy * fg
        dxo = r2 * (gy - xh * _rowmean(gy * xh))
        dxo_ref[...] = dxo
        dgt_ref[...] += _colsum(dxo * mixv)

    vec = _whole((1, d))
    return pl.pallas_call(
        body,
        name="merge_fwd",
        grid=(lx // tm,),
        in_specs=[_rows(tm, sw), _rows(tm, sw, 1), _rows(tm, mix, 1), _rows(tm, pw), _rows(tm, d), _rows(tm, d), _whole(glu_wg.shape), _whole(out_w.shape), _whole((1, sw)), _whole((1, 2 * sw)), _whole((1, pw)), vec, vec],
        out_specs=[_rows(tm, sw), _rows(tm, 2 * sw), _rows(tm, mix), _rows(tm, d), _acc((1, 1)), _acc((1, d)), _acc((1, d))],
        out_shape=[_sds((lx, sw)), _sds((lx, 2 * sw)), _sds((lx, mix), BF16), _sds((lx, d)), _sds((1, 1)), _sds((1, d)), _sds((1, d))],
        compiler_params=_cparams(1, VMEM_BIG),
    )(y_ssm, proj, proj, lin, x, tgt, glu_wg, out_w, d_skip, glu_b, pscale, gate, fin_g)


def _out_bwd(dxo, gate, branch, out_w, lin, y12, proj, pscale, tm):
    lx, d = dxo.shape
    mix = out_w.shape[0]
    pw = lin.shape[1]
    sw = mix - pw
    nb = lx // tm

    def body(dxo_ref, gt_ref, br_ref, ow_ref, l_ref, y_ref, z_ref, ps_ref, dz_ref, dl_ref, dg_ref, dps_ref, dow_hbm, acc, sem):
        i = pl.program_id(0)

        @pl.when(i == 0)
        def _():
            acc[...] = jnp.zeros_like(acc)
            dps_ref[...] = jnp.zeros_like(dps_ref)

        dmix = (dxo_ref[...] * gt_ref[...]).astype(BF16)
        dbr = _mm_nt(dmix, ow_ref[...])
        acc[...] += _mm_tn(br_ref[...], dmix)
        z = z_ref[...]
        sg = _sigmoid(z)
        dbp = dbr * (z * sg)
        y1, y2 = y_ref[:, :sw], y_ref[:, sw:]
        s2 = _sigmoid(y2)
        ps = ps_ref[...]
        lin = l_ref[...]
        bp = jnp.concatenate([lin * ps, y1 * s2], axis=1)
        dz_ref[...] = dbr * bp * (sg * (1.0 + z * (1.0 - sg)))
        dlp = dbp[:, :pw]
        dl_ref[...] = dlp * ps
        dps_ref[...] += _colsum(dlp * lin)
        dss = dbp[:, pw:]
        dg_ref[...] = jnp.concatenate([dss * s2, dss * y1 * s2 * (1.0 - s2)], axis=1).astype(BF16)

        @pl.when(i == nb - 1)
        def _():
            cp = pltpu.make_async_copy(acc, dow_hbm, sem)
            cp.start()
            cp.wait()

    return pl.pallas_call(
        body,
        name="out_bwd",
        grid=(nb,),
        in_specs=[_rows(tm, d), _whole((1, d)), _rows(tm, mix), _whole(out_w.shape), _rows(tm, pw), _rows(tm, 2 * sw), _rows(tm, mix, 1), _whole((1, pw))],
        out_specs=[_rows(tm, mix), _rows(tm, pw), _rows(tm, 2 * sw), _acc((1, pw)), _ANY],
        out_shape=[_sds((lx, mix)), _sds((lx, pw)), _sds((lx, 2 * sw), BF16), _sds((1, pw)), _sds((mix, d))],
        scratch_shapes=[pltpu.VMEM((mix, d), F32), pltpu.SemaphoreType.DMA],
        compiler_params=_cparams(1, VMEM_BIG),
    )(dxo, gate, branch, out_w, lin, y12, proj, pscale)


def _glu_bwd(dg12, ypre, proj, glu_wg, d_skip, tm):
    lx, sw = ypre.shape
    cb2 = glu_wg.shape[2]
    nb = lx // tm

    def body(dg_ref, yp_ref, u_ref, gw_ref, dk_ref, dyp_ref, dgb_ref, dd_ref, dgw_hbm, acc, sem):
        i = pl.program_id(0)

        @pl.when(i == 0)
        def _():
            acc[...] = jnp.zeros_like(acc)
            dgb_ref[...] = jnp.zeros_like(dgb_ref)
            dd_ref[...] = jnp.zeros_like(dd_ref)

        ypre = yp_ref[...]
        ygb = _gelu(ypre).astype(BF16)
        dg = dg_ref[...]
        dyg = jnp.zeros((tm, sw), F32)
        for j in range(NDEV):
            dgj = dg[:, j * cb2 : (j + 1) * cb2]
            dyg = dyg + _mm_nt(dgj, gw_ref[j])
            acc[j] += _mm_tn(ygb, dgj)
        dgb_ref[...] += _colsum(dg.astype(F32))
        dyp = dyg * _gelu_grad(ypre)
        dyp_ref[...] = dyp
        dd_ref[...] += _colsum(dyp * u_ref[...])

        @pl.when(i == nb - 1)
        def _():
            cp = pltpu.make_async_copy(acc, dgw_hbm, sem)
            cp.start()
            cp.wait()

    return pl.pallas_call(
        body,
        name="glu_bwd",
        grid=(nb,),
        in_specs=[_rows(tm, 2 * sw), _rows(tm, sw), _rows(tm, sw, 1), _whole(glu_wg.shape), _whole((1, sw))],
        out_specs=[_rows(tm, sw), _acc((1, 2 * sw)), _acc((1, sw)), _ANY],
        out_shape=[_sds((lx, sw)), _sds((1, 2 * sw)), _sds((1, sw)), _sds(glu_wg.shape)],
        scratch_shapes=[pltpu.VMEM(glu_wg.shape, F32), pltpu.SemaphoreType.DMA],
        compiler_params=_cparams(1, VMEM_BIG),
    )(dg12, ypre, proj, glu_wg, d_skip)


def _pack(parts, total):
    flat = jnp.concatenate([p.reshape(-1).astype(F32) for p in parts])
    return jnp.pad(flat, (0, total - flat.shape[0]))


def _unpack(flat, shapes):
    out, off = [], 0
    for s in shapes:
        n = int(np.prod(s)) if len(s) else 1
        out.append(flat[off : off + n].reshape(s))
        off += n
    return out


def kernel(x, c, ctx, c_ctx, ada_w, ada_b, norm_g, in_w, pool_w, pool_scale, s5_lam_re, s5_lam_im, s5_log_dt, s5_b_re, s5_b_im, s5_c_re, s5_c_im, s5_d, glu_w, glu_b, out_w, final_g, loss_target, m_c_ctx, m_ada_w, m_ada_b, m_norm_g, m_in_w, m_pool_w, m_pool_scale, m_s5_lam_re, m_s5_lam_im, m_s5_log_dt, m_s5_b_re, m_s5_b_im, m_s5_c_re, m_s5_c_im, m_s5_d, m_glu_w, m_glu_b, m_out_w, m_final_g, v_c_ctx, v_ada_w, v_ada_b, v_norm_g, v_in_w, v_pool_w, v_pool_scale, v_s5_lam_re, v_s5_lam_im, v_s5_log_dt, v_s5_b_re, v_s5_b_im, v_s5_c_re, v_s5_c_im, v_s5_d, v_glu_w, v_glu_b, v_out_w, v_final_g):
    xr, tgt, xc = x[0], loss_target[0], ctx[0]
    lx, d = xr.shape
    lc = xc.shape[0]
    mix = out_w.shape[1] * NDEV
    sw = glu_w.shape[1]
    pw = mix - sw
    pgw = pw // len(POOL_WINDOWS)
    ngrp = sw // SSM_H
    cb = in_w.shape[2]
    cb2 = glu_w.shape[2]
    nmod = ada_w.shape[2]
    ncc, ncl = lc // CHUNK_T, lx // CHUNK_T
    nch = ncc + ncl
    tm = min(256, lx)
    tm_acc = min(512, lx)
    assert mix == d and pw == sw and 2 * cb == pw and NDEV * cb2 == 2 * sw and lx % GRID_W == 0
    mx, my, mc = _my_pos()
    me = 4 * mx + 2 * my + mc

    (c_all,) = _all_gather([c], "gather_c")
    call = jnp.concatenate([c_all.reshape(NDEV, d), c_ctx.reshape(1, d), jnp.zeros((NDEV - 1, d), F32)], axis=0)
    ada_w_l = ada_w[0]
    ada_b_l = lax.dynamic_slice_in_dim(ada_b, me * nmod, nmod, axis=1)
    m_loc = _mod_fwd(call, ada_w_l, ada_b_l)
    (m_all,) = _all_gather([m_loc], "gather_mod")
    mod = lax.dynamic_index_in_dim(m_all, me, axis=1, keepdims=False).reshape(1, NDEV * nmod)
    mod_c = m_all[:, NDEV, :].reshape(1, NDEV * nmod)
    shift, scale, gate = mod[:, :d], mod[:, d : 2 * d], mod[:, 2 * d :]
    shift_c, scale_c = mod_c[:, :d], mod_c[:, d : 2 * d]

    s5_params = tuple(p[0] for p in (s5_lam_re, s5_lam_im, s5_log_dt, s5_b_re, s5_b_im, s5_c_re, s5_c_im))
    (small, a16), small_vjp = jax.vjp(_s5_small, *s5_params)
    gb_ops = min(8, ngrp)
    (tsum, wst, ccat), (in_wg,) = _s5_build(small, gb_ops, [in_w[0].astype(BF16)])

    ssm_blocks = (2, 3)
    (proj, hb), (glu_wg, out_wg, pool_wg) = _in_proj(
        xr, norm_g, scale, shift, in_wg, tuple(range(NDEV)), tm, "in_proj",
        shards=[glu_w[0].astype(BF16), out_w[0].astype(BF16), pool_w[0].reshape(-1, pgw).astype(BF16)],
    )
    out_wf = out_wg.reshape(mix, d)
    pool_wf = pool_wg.reshape(NDEV, len(POOL_WINDOWS), pgw // NDEV, pgw).transpose(1, 0, 2, 3).reshape(len(POOL_WINDOWS), pgw, pgw)
    (uc, hcb), _ = _in_proj(xc, norm_g, scale_c, shift_c, in_wg, ssm_blocks, min(tm, lc), "in_proj_ctx")
    lin, dmat = _pool_fwd(proj, pool_wf)

    gb_scan = min(16, ngrp)
    half = wst.shape[2] // 2
    perm = _chunk_perm()
    flat = lambda a: a.reshape(ngrp * nch, half)
    grp = lambda a: a.reshape(ngrp, nch, half)
    u_all, s_re, s_im = _s5_state(proj, pw // LANES, uc, perm, wst, ncc, ncl)
    h_re, h_im = _s5_scan_fwd(flat(s_re), flat(s_im), a16, ncc, nch, gb_scan)
    y_ssm = _s5_out(u_all, grp(h_re), grp(h_im), tsum, ccat, perm, ncc, lx)

    ypre, y12, branch, dxo, loss_l, dfg, dgate = _merge_fwd(
        y_ssm, proj, lin, xr, tgt, glu_wg, out_wf, s5_d, glu_b, pool_scale, gate, final_g.reshape(1, d), tm
    )

    dz, dlin, dg12, dps, d_out_w = _out_bwd(dxo, gate, branch, out_wf, lin, y12, proj, pool_scale, tm)
    dypre, dglu_b, dd_skip, d_glu_w = _glu_bwd(dg12, ypre, proj, glu_wg, s5_d, tm_acc)

    dy_all, dh_re, dh_im = _s5_dstate(dypre, perm, ccat, ncc)
    ds_re, ds_im = _s5_scan_bwd(flat(dh_re), flat(dh_im), a16, ncc, nch, gb_scan)
    du5, duc = _s5_du(dy_all, grp(ds_re), grp(ds_im), tsum, wst, perm, ncc, lx, lc)
    dwst, dtsum, dccat, da16 = _s5_op_grads(u_all, dy_all, grp(ds_re), grp(ds_im), grp(h_re), grp(h_im), ncc)
    dup, dpool_w = _pool_bwd(dlin, pool_wf, dmat)

    npw = len(POOL_WINDOWS)
    dpool_s = dpool_w.reshape(npw, NDEV, pgw // NDEV, pgw).transpose(1, 0, 2, 3).reshape(NDEV, npw * (pgw // NDEV), pgw)
    parts_a = _pair_partials([d_glu_w, d_out_w.reshape(NDEV, mix // NDEV, d), dpool_s], [BF16] * 3, "grads_a")
    dpcb, dsc_c, dsh_c, dg_c = _in_bwd_ctx(xc, duc, in_wg, ssm_blocks, norm_g, scale_c, shift_c)
    grad_x, dpb, dsc, dsh, dnorm_g = _in_bwd(xr, dxo, dup, du5, dypre, dz, in_wg, norm_g, scale, shift, s5_d, dg_c, tm)
    d_in_w, (land_glu, land_out, land_pool) = _in_w_grad(hb, dpb, hcb, dpcb, cb, min(2 * tm_acc, lx), parts=parts_a)

    dmod = jnp.concatenate(
        [jnp.concatenate([dsh, dsc, dgate], axis=1), jnp.concatenate([dsh_c, dsc_c, jnp.zeros((1, d), F32)], axis=1)], axis=0
    )
    (dmod_all,) = _all_gather([dmod], "gather_dmod")
    dmod_l = lax.dynamic_slice_in_dim(dmod_all, me * nmod, nmod, axis=2)
    g_ada_w, g_ada_b_l, cctx_part = _mod_bwd(call, dmod_l[:, 0, :], dmod_l[:, 1, :], ada_w_l)

    parts_b = _pair_partials([d_in_w], [BF16], "grads_b")
    dsmall, (land_in,) = _s5_build_bwd(small, dtsum, dwst, dccat, gb_ops, parts_b)
    ds5 = small_vjp((tuple(dsmall), da16))

    small_names = ["norm_g", "pool_scale", "s5_lam_re", "s5_lam_im", "s5_log_dt", "s5_b_re", "s5_b_im", "s5_c_re", "s5_c_im", "s5_d", "glu_b", "final_g", "c_ctx"]
    small_w = dict(norm_g=norm_g, pool_scale=pool_scale, s5_lam_re=s5_lam_re, s5_lam_im=s5_lam_im, s5_log_dt=s5_log_dt, s5_b_re=s5_b_re, s5_b_im=s5_b_im, s5_c_re=s5_c_re, s5_c_im=s5_c_im, s5_d=s5_d, glu_b=glu_b, final_g=final_g, c_ctx=c_ctx)
    small_m = dict(norm_g=m_norm_g, pool_scale=m_pool_scale, s5_lam_re=m_s5_lam_re, s5_lam_im=m_s5_lam_im, s5_log_dt=m_s5_log_dt, s5_b_re=m_s5_b_re, s5_b_im=m_s5_b_im, s5_c_re=m_s5_c_re, s5_c_im=m_s5_c_im, s5_d=m_s5_d, glu_b=m_glu_b, final_g=m_final_g, c_ctx=m_c_ctx)
    small_v = dict(norm_g=v_norm_g, pool_scale=v_pool_scale, s5_lam_re=v_s5_lam_re, s5_lam_im=v_s5_lam_im, s5_log_dt=v_s5_log_dt, s5_b_re=v_s5_b_re, s5_b_im=v_s5_b_im, s5_c_re=v_s5_c_re, s5_c_im=v_s5_c_im, s5_d=v_s5_d, glu_b=v_glu_b, final_g=v_final_g, c_ctx=v_c_ctx)
    small_g = dict(norm_g=dnorm_g, pool_scale=dps, s5_lam_re=ds5[0], s5_lam_im=ds5[1], s5_log_dt=ds5[2], s5_b_re=ds5[3], s5_b_im=ds5[4], s5_c_re=ds5[5], s5_c_im=ds5[6], s5_d=dd_skip, glu_b=dglu_b, final_g=dfg, c_ctx=cctx_part)
    shapes = [small_w[k].shape for k in small_names]
    nsmall = sum(int(np.prod(s)) for s in shapes) + 1
    unit = NDEV * 8 * LANES
    tot = -(-nsmall // unit) * unit
    rows = tot // LANES
    gpack = _pack([small_g[k] for k in small_names] + [loss_l], tot).reshape(NDEV, rows // NDEV, LANES)
    (land_small,) = _reduce_scatter([gpack], [F32], "small")
    gsum = _sum_slots(land_small, "sum_small")
    (gall,) = _all_gather([gsum], "gather_small")
    (g_ada_b_all,) = _all_gather([g_ada_b_l], "gather_ada_b")
    gflat = gall.reshape(tot)
    loss = gflat[nsmall - 1]

    nab = NDEV * nmod
    tot2 = tot + -(-nab // (8 * LANES)) * (8 * LANES)
    g2 = jnp.concatenate([gflat, jnp.pad(g_ada_b_all.reshape(nab), (0, tot2 - tot - nab))]).reshape(tot2 // LANES, LANES)

    def pack2(dct, ab):
        return jnp.concatenate(
            [_pack([dct[k] for k in small_names] + [jnp.zeros((1,), F32)], tot), jnp.pad(ab.reshape(nab), (0, tot2 - tot - nab))]
        ).reshape(tot2 // LANES, LANES)

    _, d2, m2, v2 = _adam(g2, pack2(small_w, ada_b), pack2(small_m, m_ada_b), pack2(small_v, v_ada_b), "adam_small")

    def unpack2(flat2):
        flat2 = flat2.reshape(tot2)
        vals = dict(zip(small_names, _unpack(flat2, shapes)))
        vals["ada_b"] = flat2[tot : tot + nab].reshape(ada_b.shape)
        return vals

    res = {k: unpack2(a) for k, a in (("g", g2), ("d", d2), ("m", m2), ("v", v2))}

    def shard(name, g, land, w, m, v):
        shp = w.shape
        w2, m2_, v2_ = (a.reshape(-1, shp[-1]) for a in (w, m, v))
        out = _adam(g, w2, m2_, v2_, "adam_" + name, land=land)
        for k, a in zip(("g", "d", "m", "v"), out):
            res[k][name] = a.reshape(shp)

    shard("ada_w", g_ada_w, None, ada_w, m_ada_w, v_ada_w)
    shard("in_w", None, land_in, in_w, m_in_w, v_in_w)
    shard("pool_w", None, land_pool, pool_w, m_pool_w, v_pool_w)
    shard("glu_w", None, land_glu, glu_w, m_glu_w, v_glu_w)
    shard("out_w", None, land_out, out_w, m_out_w, v_out_w)

    names = ["c_ctx", "ada_w", "ada_b", "norm_g", "in_w", "pool_w", "pool_scale", "s5_lam_re", "s5_lam_im", "s5_log_dt", "s5_b_re", "s5_b_im", "s5_c_re", "s5_c_im", "s5_d", "glu_w", "glu_b", "out_w", "final_g"]
    return (loss, grad_x[None], *[res["g"][n] for n in names], *[res["d"][n] for n in names], *[res["m"][n] for n in names], *[res["v"][n] for n in names])
```

```python
import numpy as np

import jax
import jax.numpy as jnp
from jax import lax
from jax.experimental import pallas as pl
from jax.experimental.pallas import tpu as pltpu

F32 = jnp.float32
BF16 = jnp.bfloat16
NDEV = 8
EPS = 1e-6
GRID_W = 64
POOL_WINDOWS = (2, 4, 8, 16)
SSM_H = 16
CHUNK_T = 16
LANES = 128
ADAM_LR, ADAM_B1, ADAM_B2, ADAM_EPS, ADAM_WD, ADAM_STEP = 0.001, 0.9, 0.999, 1e-08, 0.01, 10
VMEM_BIG = 56 << 20
MESH_ID = pl.DeviceIdType.MESH

_HBM = pl.BlockSpec(memory_space=pltpu.HBM)
_ANY = pl.BlockSpec(memory_space=pl.ANY)
_SMEM = pl.BlockSpec(memory_space=pltpu.SMEM)


def _sds(shape, dtype=F32):
    return jax.ShapeDtypeStruct(tuple(shape), dtype)


def _cparams(ngrid=0, vmem=None):
    return pltpu.CompilerParams(
        dimension_semantics=("arbitrary",) * ngrid if ngrid else None, vmem_limit_bytes=vmem
    )


def _rows(tm, c, col=0):
    return pl.BlockSpec((tm, c), lambda i: (i, col))


def _whole(shape):
    nd = len(shape)
    return pl.BlockSpec(tuple(shape), lambda *_: (0,) * nd, pipeline_mode=pl.Buffered(1))


def _acc(shape):
    nd = len(shape)
    return pl.BlockSpec(tuple(shape), lambda *_: (0,) * nd)


def _mm(a, b):
    return jnp.dot(a, b, preferred_element_type=F32)


def _mm_nt(a, b):
    return lax.dot_general(a, b, (((1,), (1,)), ((), ())), preferred_element_type=F32)


def _mm_tn(a, b):
    return lax.dot_general(a, b, (((0,), (0,)), ((), ())), preferred_element_type=F32)


def _mm_split(k01, s):
    hi = s.astype(BF16)
    lo = (s - hi.astype(F32)).astype(BF16)
    return _mm(k01, hi) + _mm(k01, lo)


def _sigmoid(v):
    return 0.5 * (jnp.tanh(0.5 * v) + 1.0)


def _silu(v):
    return v * _sigmoid(v)


_GELU_K = 0.7978845608028654
_GELU_C = 0.044715


def _gelu(v):
    return 0.5 * v * (1.0 + jnp.tanh(_GELU_K * (v + _GELU_C * v * v * v)))


def _gelu_grad(v):
    th = jnp.tanh(_GELU_K * (v + _GELU_C * v * v * v))
    return 0.5 * (1.0 + th) + 0.5 * v * (1.0 - th * th) * (_GELU_K * (1.0 + 3.0 * _GELU_C * v * v))


def _colsum(v):
    return jnp.sum(v, axis=0, keepdims=True)


def _rowmean(v):
    return jnp.mean(v, axis=-1, keepdims=True)


NCHIP = NDEV // 2


def _my_pos():
    return lax.axis_index("x"), lax.axis_index("y"), lax.axis_index("c")


def _other_chips():
    x, y, _ = _my_pos()
    return [(1 - x, y), (x, 1 - y), (1 - x, 1 - y)]


def _remote(src, dst, send, recv, to):
    return pltpu.make_async_remote_copy(src, dst, send, recv, device_id=to, device_id_type=MESH_ID)


def _all_gather(arrs, name):
    n = len(arrs)

    def body(*refs):
        start, finish = _gather_copies(refs[:n], refs[n : 2 * n], *refs[2 * n :])
        start()
        finish()

    return pl.pallas_call(
        body, name=name, out_shape=_gather_shapes(arrs), in_specs=[_HBM] * n, out_specs=[_HBM] * n, scratch_shapes=_gather_sems(n)
    )(*arrs)


def _gather_shapes(arrs):
    return [_sds((NDEV,) + a.shape, a.dtype) for a in arrs]


def _gather_sems(n):
    return [pltpu.SemaphoreType.DMA((n, NDEV - 1)), pltpu.SemaphoreType.DMA((n, NDEV - 1)), pltpu.SemaphoreType.DMA((n,))]


def _gather_copies(ins, outs, send, recv, loc):
    n = len(ins)
    x, y, c = _my_pos()
    me, sib = (x, y, c), (x, y, 1 - c)
    chips = _other_chips()

    def slot(a, p):
        return outs[a].at[4 * p[0] + 2 * p[1] + p[2]]

    def copy(a, k, block, to, own=False):
        return _remote(ins[a] if own else slot(a, block), slot(a, block), send.at[a, k], recv.at[a, k], to)

    def mine():
        local = [pltpu.make_async_copy(ins[a], slot(a, me), loc.at[a]) for a in range(n)]
        sent = []
        for a in range(n):
            sent.append(copy(a, 0, me, sib, own=True))
            sent += [copy(a, 1 + j, me, (*chip, c), own=True) for j, chip in enumerate(chips)]
        return local, sent

    def start():
        local, sent = mine()
        for cp in local + sent:
            cp.start()

    def finish():
        local, sent = mine()
        for j, chip in enumerate(chips):
            for a in range(n):
                copy(a, 1 + j, (*chip, c), me).wait_recv()
                fwd = copy(a, 4 + j, (*chip, c), sib)
                fwd.start()
                sent.append(fwd)
        for a in range(n):
            copy(a, 0, sib, me).wait_recv()
            for j, chip in enumerate(chips):
                copy(a, 4 + j, (*chip, 1 - c), me).wait_recv()
        for cp in sent:
            cp.wait_send()
        for cp in local:
            cp.wait()

    return start, finish


def _sibling_swap(arrs, name):
    n = len(arrs)

    def body(*refs):
        ins, outs = refs[:n], refs[n : 2 * n]
        send, recv = refs[2 * n :]
        x, y, c = _my_pos()
        cps = [_remote(ins[a].at[:, 1 - c], outs[a], send.at[a], recv.at[a], (x, y, 1 - c)) for a in range(n)]
        for cp in cps:
            cp.start()
        for cp in cps:
            cp.wait()

    return pl.pallas_call(
        body,
        name=name,
        out_shape=[_sds((a.shape[0],) + a.shape[2:], a.dtype) for a in arrs],
        in_specs=[_HBM] * n,
        out_specs=[_HBM] * n,
        scratch_shapes=[pltpu.SemaphoreType.DMA((n,)), pltpu.SemaphoreType.DMA((n,))],
    )(*arrs)


def _pair_sum(arr, got, wire, name):
    _, _, r, c = arr.shape
    tr = _row_tile(r, c, budget=1 << 20)

    def body(a_ref, g_ref, o_ref):
        o_ref[0] = (a_ref[0, lax.axis_index("c")] + g_ref[0]).astype(wire)

    return pl.pallas_call(
        body,
        name=name,
        grid=(NCHIP, r // tr),
        in_specs=[pl.BlockSpec((1, 2, tr, c), lambda q, i: (q, 0, i, 0)), pl.BlockSpec((1, tr, c), lambda q, i: (q, i, 0))],
        out_specs=pl.BlockSpec((1, tr, c), lambda q, i: (q, i, 0)),
        out_shape=_sds((NCHIP, r, c), wire),
        compiler_params=_cparams(2),
    )(arr, got)


def _chip_exchange(arrs, name):
    n = len(arrs)

    def body(*refs):
        start, finish = _exchange_copies(refs[:n], refs[n : 2 * n], *refs[2 * n :])
        start()
        finish()

    return pl.pallas_call(
        body, name=name, out_shape=[_sds(a.shape, a.dtype) for a in arrs], in_specs=[_HBM] * n, out_specs=[_HBM] * n, scratch_shapes=_exchange_sems(n)
    )(*arrs)


def _exchange_sems(n):
    return [pltpu.SemaphoreType.DMA((n, NCHIP - 1)), pltpu.SemaphoreType.DMA((n, NCHIP - 1)), pltpu.SemaphoreType.DMA((n,))]


def _exchange_copies(ins, outs, send, recv, loc):
    n = len(ins)
    x, y, c = _my_pos()
    mine = 2 * x + y
    chips = _other_chips()

    def copies():
        local = [pltpu.make_async_copy(ins[a].at[mine], outs[a].at[mine], loc.at[a]) for a in range(n)]
        sent = [
            _remote(ins[a].at[2 * px + py], outs[a].at[mine], send.at[a, j], recv.at[a, j], (px, py, c))
            for a in range(n)
            for j, (px, py) in enumerate(chips)
        ]
        return local, sent

    def start():
        local, sent = copies()
        for cp in local + sent:
            cp.start()

    def finish():
        local, sent = copies()
        for a in range(n):
            for j, (px, py) in enumerate(chips):
                _remote(ins[a].at[mine], outs[a].at[2 * px + py], send.at[a, j], recv.at[a, j], (px, py, c)).wait_recv()
        for cp in sent:
            cp.wait_send()
        for cp in local:
            cp.wait()

    return start, finish


def _pair_partials(arrs, wires, tag):
    four = [a.reshape((NCHIP, 2) + a.shape[1:]) for a in arrs]
    got = _sibling_swap(four, "swap_" + tag)
    return [_pair_sum(a, g, w, f"pair_sum_{tag}{i}") for i, (a, g, w) in enumerate(zip(four, got, wires))]


def _reduce_scatter(arrs, wires, tag):
    return _chip_exchange(_pair_partials(arrs, wires, tag), "exchange_" + tag)


def _row_tile(r, c, budget=1 << 20):
    best = r
    for t in range(8, r, 8):
        if r % t == 0 and t * c * 4 <= budget:
            best = t
    if r * c * 4 <= budget:
        best = r
    return best


def _sum_chips(land):
    tot = land[0].astype(F32)
    for q in range(1, NCHIP):
        tot = tot + land[q].astype(F32)
    return tot


def _adam_math(w, g, m, v):
    m2 = ADAM_B1 * m + (1.0 - ADAM_B1) * g
    v2 = ADAM_B2 * v + (1.0 - ADAM_B2) * (g * g)
    mh = m2 / (1.0 - ADAM_B1**ADAM_STEP)
    vh = v2 / (1.0 - ADAM_B2**ADAM_STEP)
    delta = -ADAM_LR * (mh / (jnp.sqrt(vh) + ADAM_EPS) + ADAM_WD * w)
    return delta, m2, v2


def _adam(g, w, m, v, name, land=None):
    r, c = w.shape
    tr = _row_tile(r, c, budget=1 << 20)

    def body(*refs):
        if land is not None:
            l_ref, w_ref, m_ref, v_ref, g_ref, d_ref, m2_ref, v2_ref = refs
            gv = _sum_chips(l_ref)
            g_ref[...] = gv
        else:
            g_in, w_ref, m_ref, v_ref, d_ref, m2_ref, v2_ref = refs
            gv = g_in[...]
        d, m2, v2 = _adam_math(w_ref[...], gv, m_ref[...], v_ref[...])
        d_ref[...] = d
        m2_ref[...] = m2
        v2_ref[...] = v2

    blk = _rows(tr, c)
    if land is not None:
        in_specs = [pl.BlockSpec((NCHIP, tr, c), lambda i: (0, i, 0)), blk, blk, blk]
        out = pl.pallas_call(
            body, name=name, grid=(r // tr,), in_specs=in_specs, out_specs=[blk] * 4, out_shape=[_sds((r, c))] * 4, compiler_params=_cparams(1)
        )(land, w, m, v)
        return out
    out = pl.pallas_call(
        body, name=name, grid=(r // tr,), in_specs=[blk] * 4, out_specs=[blk] * 3, out_shape=[_sds((r, c))] * 3, compiler_params=_cparams(1)
    )(g, w, m, v)
    return (g,) + tuple(out)


def _sum_slots(land, name):
    _, r, c = land.shape
    tr = _row_tile(r, c, budget=1 << 20)

    def body(l_ref, o_ref):
        o_ref[...] = _sum_chips(l_ref)

    return pl.pallas_call(
        body,
        name=name,
        grid=(r // tr,),
        in_specs=[pl.BlockSpec((NCHIP, tr, c), lambda i: (0, i, 0))],
        out_specs=_rows(tr, c),
        out_shape=_sds((r, c)),
        compiler_params=_cparams(1),
    )(land)


def _mod_fwd(call, ada_w, ada_b_loc):
    def body(c_ref, w_ref, b_ref, o_ref):
        s = _silu(c_ref[...]).astype(BF16)
        o_ref[...] = _mm(s, w_ref[...].astype(BF16)) + b_ref[...]

    return pl.pallas_call(
        body, name="mod_fwd", out_shape=_sds((call.shape[0], ada_w.shape[1])), compiler_params=_cparams(0, 32 << 20)
    )(call, ada_w, ada_b_loc)


def _mod_bwd(call, dm_loc, dmc_loc, ada_w):
    d, n = ada_w.shape
    pad = call.shape[0] - NDEV - 1

    def body(c_ref, dm_ref, dmc_ref, w_ref, gw_ref, gb_ref, cp_ref):
        cv = c_ref[...]
        sg = _sigmoid(cv)
        dmc = _colsum(dmc_ref[...])
        dm = dm_ref[...]
        rows = jnp.concatenate([dm, dmc, jnp.zeros((pad, n), F32)], axis=0)
        gw_ref[...] = _mm_tn((cv * sg).astype(BF16), rows.astype(BF16))
        gb_ref[...] = _colsum(dm) + dmc
        back = _mm_nt(rows[NDEV:].astype(BF16), w_ref[...].astype(BF16))
        c8, s8 = cv[NDEV : NDEV + 1], sg[NDEV : NDEV + 1]
        cp_ref[...] = back[0:1] * (s8 * (1.0 + c8 * (1.0 - s8)))

    return pl.pallas_call(
        body, name="mod_bwd", out_shape=[_sds((d, n)), _sds((1, n)), _sds((1, d))], compiler_params=_cparams(0, 40 << 20)
    )(call, dm_loc, dmc_loc, ada_w)


def _riding(body, nin, nout, nr, grid, copies):
    def wrapped(*refs):
        start, finish = copies(refs[nin : nin + nr], refs[nin + nr + nout : nin + 2 * nr + nout], *refs[nin + 2 * nr + nout :])
        first, last = True, True
        for ax, size in enumerate(grid):
            first = jnp.logical_and(first, pl.program_id(ax) == 0)
            last = jnp.logical_and(last, pl.program_id(ax) == size - 1)
        pl.when(first)(start)
        body(*refs[:nin], *refs[nin + nr : nin + nr + nout])
        pl.when(last)(finish)

    return wrapped if nr else body


def _in_proj(xr, norm_g, scale, shift, wg, jsel, tm, name, shards=()):
    lx, d = xr.shape
    cb = wg.shape[2]
    nr = len(shards)

    def body(x_ref, g_ref, sc_ref, sh_ref, w_ref, p_ref, h_ref):
        xv = x_ref[...]
        r = lax.rsqrt(_rowmean(xv * xv) + EPS)
        hb = ((xv * r) * g_ref[...] * (1.0 + sc_ref[...]) + sh_ref[...]).astype(BF16)
        h_ref[...] = hb
        for q, j in enumerate(jsel):
            p_ref[:, q * cb : (q + 1) * cb] = _mm(hb, w_ref[j])

    vec = _whole((1, d))
    out = pl.pallas_call(
        _riding(body, 5, 2, nr, (lx // tm,), _gather_copies),
        name=name,
        grid=(lx // tm,),
        in_specs=[_rows(tm, d), vec, vec, vec, _whole(wg.shape)] + [_HBM] * nr,
        out_specs=[_rows(tm, len(jsel) * cb), _rows(tm, d)] + [_HBM] * nr,
        out_shape=[_sds((lx, len(jsel) * cb)), _sds((lx, d), BF16)] + _gather_shapes(shards),
        scratch_shapes=_gather_sems(nr) if nr else [],
        compiler_params=_cparams(1, VMEM_BIG),
    )(xr, norm_g, scale, shift, wg, *shards)
    return out[:2], out[2:]


def _in_bwd(xr, dxo, dup, du5, dypre, dz, wg, norm_g, scale, shift, d_skip, dg_init, tm):
    lx, d = xr.shape
    cb = wg.shape[2]
    pw = dup.shape[1]
    sw = du5.shape[1]
    mix = dz.shape[1]
    ncol = NDEV * cb

    def body(x_ref, dxo_ref, up_ref, d5_ref, dy_ref, dz_ref, w_ref, g_ref, sc_ref, sh_ref, dk_ref, gi_ref,
             gx_ref, dp_ref, dsc_ref, dsh_ref, dg_ref):
        i = pl.program_id(0)

        @pl.when(i == 0)
        def _():
            dsc_ref[...] = jnp.zeros_like(dsc_ref)
            dsh_ref[...] = jnp.zeros_like(dsh_ref)
            dg_ref[...] = gi_ref[...]

        dp = jnp.concatenate(
            [up_ref[...], d5_ref[...] + dk_ref[...] * dy_ref[...], dz_ref[...]], axis=1
        ).astype(BF16)
        dp_ref[...] = dp
        dh = _mm_nt(dp[:, 0:cb], w_ref[0])
        for j in range(1, NDEV):
            dh = dh + _mm_nt(dp[:, j * cb : (j + 1) * cb], w_ref[j])
        xv = x_ref[...]
        r = lax.rsqrt(_rowmean(xv * xv) + EPS)
        xh = xv * r
        g = g_ref[...]
        one_sc = 1.0 + sc_ref[...]
        dsh_ref[...] += _colsum(dh)
        dsc_ref[...] += _colsum(dh * (xh * g))
        dg_ref[...] += _colsum(dh * one_sc * xh)
        dxh = dh * one_sc * g
        gx_ref[...] = r * (dxh - xh * _rowmean(dxh * xh)) + dxo_ref[...]

    vec = _whole((1, d))
    return pl.pallas_call(
        body,
        name="in_bwd",
        grid=(lx // tm,),
        in_specs=[_rows(tm, d), _rows(tm, d), _rows(tm, pw), _rows(tm, sw), _rows(tm, sw), _rows(tm, mix), _whole(wg.shape), vec, vec, vec, _whole((1, sw)), vec],
        out_specs=[_rows(tm, d), _rows(tm, ncol), _acc((1, d)), _acc((1, d)), _acc((1, d))],
        out_shape=[_sds((lx, d)), _sds((lx, ncol), BF16), _sds((1, d)), _sds((1, d)), _sds((1, d))],
        compiler_params=_cparams(1, VMEM_BIG),
    )(xr, dxo, dup, du5, dypre, dz, wg, norm_g, scale, shift, d_skip, dg_init)


def _in_bwd_ctx(xc, duc, wg, jsel, norm_g, scale, shift):
    lc, d = xc.shape
    cb = wg.shape[2]

    def body(x_ref, du_ref, w_ref, g_ref, sc_ref, sh_ref, dp_ref, dsc_ref, dsh_ref, dg_ref):
        dp = du_ref[...].astype(BF16)
        dp_ref[...] = dp
        dh = _mm_nt(dp[:, 0:cb], w_ref[jsel[0]])
        for q in range(1, len(jsel)):
            dh = dh + _mm_nt(dp[:, q * cb : (q + 1) * cb], w_ref[jsel[q]])
        xv = x_ref[...]
        xh = xv * lax.rsqrt(_rowmean(xv * xv) + EPS)
        dsh_ref[...] = _colsum(dh)
        dsc_ref[...] = _colsum(dh * (xh * g_ref[...]))
        dg_ref[...] = _colsum(dh * (1.0 + sc_ref[...]) * xh)

    return pl.pallas_call(
        body,
        name="in_bwd_ctx",
        out_shape=[_sds(duc.shape, BF16), _sds((1, d)), _sds((1, d)), _sds((1, d))],
        compiler_params=_cparams(0, VMEM_BIG),
    )(xc, duc, wg, norm_g, scale, shift)


def _in_w_grad(hb, dpb, hcb, dpcb, cb, tm, parts=()):
    lx, d = hb.shape
    lc = hcb.shape[0]
    nb = lx // tm
    cg = 2 * cb
    nr = len(parts)
    grid = (NDEV // 2, nb + 1)

    def body(h_ref, dp_ref, hc_ref, dpc_ref, o_ref):
        j, i = pl.program_id(0), pl.program_id(1)

        @pl.when(i == 0)
        def _():
            o_ref[...] = jnp.zeros_like(o_ref)

        @pl.when(i < nb)
        def _():
            h = h_ref[...]
            o_ref[0] += _mm_tn(h, dp_ref[:, 0:cb])
            o_ref[1] += _mm_tn(h, dp_ref[:, cb:cg])

        @pl.when(jnp.logical_and(i == nb, j == 1))
        def _():
            h = hc_ref[...]
            o_ref[0] += _mm_tn(h, dpc_ref[:, 0:cb])
            o_ref[1] += _mm_tn(h, dpc_ref[:, cb:cg])

    out = pl.pallas_call(
        _riding(body, 4, 1, nr, grid, _exchange_copies),
        name="in_w_grad",
        grid=grid,
        in_specs=[
            pl.BlockSpec((tm, d), lambda j, i: (jnp.minimum(i, nb - 1), 0)),
            pl.BlockSpec((tm, cg), lambda j, i: (jnp.minimum(i, nb - 1), j)),
            pl.BlockSpec((lc, d), lambda j, i: (0, 0)),
            pl.BlockSpec((lc, cg), lambda j, i: (0, 0)),
        ]
        + [_HBM] * nr,
        out_specs=[pl.BlockSpec((2, d, cb), lambda j, i: (j, 0, 0))] + [_HBM] * nr,
        out_shape=[_sds((NDEV, d, cb))] + [_sds(p.shape, p.dtype) for p in parts],
        scratch_shapes=_exchange_sems(nr) if nr else [],
        compiler_params=_cparams(2, VMEM_BIG),
    )(hb, dpb, hcb, dpcb, *parts)
    return out[0], out[1:]


def _pool_tables(w, rows, rb, pgw, transpose):
    t = np.arange(GRID_W)
    lo, hi = np.clip(t - w // 2, 0, GRID_W), np.clip(t + w - w // 2, 0, GRID_W)
    band = ((t[None, :] >= lo[:, None]) & (t[None, :] < hi[:, None])).astype(np.float32)
    if transpose:
        band = band.T
    kc = np.kron(np.eye(rb, dtype=np.float32), band)
    inv_c = np.tile((1.0 / (hi - lo).astype(np.float32))[:, None], (rb, pgw)).astype(np.float32)
    r = np.arange(rows)
    cnt_r = np.clip(r + w - w // 2, 0, rows) - np.clip(r - w // 2, 0, rows)
    inv_r = (1.0 / cnt_r.astype(np.float32)).astype(np.float32)
    return jnp.asarray(kc, BF16), jnp.asarray(inv_r), jnp.asarray(inv_c)


def _pool_stack(rows, rb, pgw, transpose):
    tabs = [_pool_tables(w, rows, rb, pgw, transpose) for w in POOL_WINDOWS]
    return tuple(jnp.stack([t[i] for t in tabs]) for i in range(3))


def _per_window(group):
    for k, w in enumerate(POOL_WINDOWS):
        pl.when(pl.program_id(0) == k)(lambda k=k, w=w: group(k, w))


def _pool_fwd(proj, pool_wf):
    lx = proj.shape[0]
    ng, pgw, _ = pool_wf.shape
    rows = lx // GRID_W
    rb = min(4, rows)
    tok = rb * GRID_W
    kc, inv_r, inv_c = _pool_stack(rows, rb, pgw, False)

    def body(p_ref, w_ref, kc_ref, ir_ref, ic_ref, lin_ref, dm_ref, xp):
        def group(kk, w):
            lo = w // 2
            xp[pl.ds(0, lo * GRID_W), :] = jnp.zeros((lo * GRID_W, pgw), F32)
            xp[pl.ds((lo + rows) * GRID_W, (w - lo) * GRID_W), :] = jnp.zeros(((w - lo) * GRID_W, pgw), F32)
            xp[pl.ds(lo * GRID_W, lx), :] = p_ref[...]

            def blk(b, carry):
                r0 = b * rb
                parts = []
                for rr in range(rb):
                    base = pl.multiple_of((r0 + rr) * GRID_W, GRID_W)
                    s = xp[pl.ds(base, GRID_W), :]
                    for k in range(1, w):
                        s = s + xp[pl.ds(base + k * GRID_W, GRID_W), :]
                    parts.append(s * ir_ref[kk, r0 + rr])
                m = _mm_split(kc_ref[0], jnp.concatenate(parts, axis=0)) * ic_ref[0]
                u = xp[pl.ds(pl.multiple_of((r0 + lo) * GRID_W, GRID_W), tok), :]
                db = (m - u).astype(BF16)
                o0 = pl.multiple_of(r0 * GRID_W, GRID_W)
                dm_ref[pl.ds(o0, tok), :] = db
                lin_ref[pl.ds(o0, tok), :] = _mm(db, w_ref[0])
                return carry

            lax.fori_loop(0, rows // rb, blk, 0)

        _per_window(group)

    col = pl.BlockSpec((lx, pgw), lambda g: (0, g))
    return pl.pallas_call(
        body,
        name="pool_fwd",
        grid=(ng,),
        in_specs=[col, _gspec(1, pgw, pgw), _gspec(1, tok, tok), _SMEM, _gspec(1, tok, pgw)],
        out_specs=[col, col],
        out_shape=[_sds((lx, ng * pgw)), _sds((lx, ng * pgw), BF16)],
        scratch_shapes=[pltpu.VMEM(((rows + max(POOL_WINDOWS)) * GRID_W, pgw), F32)],
        compiler_params=_cparams(1, VMEM_BIG),
    )(proj, pool_wf, kc, inv_r, inv_c)


def _pool_bwd(dlin, pool_wf, dmat):
    lx = dlin.shape[0]
    ng, pgw, _ = pool_wf.shape
    rows = lx // GRID_W
    rb = min(4, rows)
    tok = rb * GRID_W
    kct, inv_r, inv_c = _pool_stack(rows, rb, pgw, True)

    def body(dl_ref, w_ref, kc_ref, ir_ref, ic_ref, dm_ref, du_ref, dw_ref, tp):
        def group(kk, w):
            front = w - w // 2 - 1
            if front:
                tp[pl.ds(0, front * GRID_W), :] = jnp.zeros((front * GRID_W, pgw), F32)
            tp[pl.ds((front + rows) * GRID_W, (w - front) * GRID_W), :] = jnp.zeros(((w - front) * GRID_W, pgw), F32)
            dw_ref[...] = jnp.zeros_like(dw_ref)

            def blk(b, carry):
                o0 = pl.multiple_of(b * tok, GRID_W)
                dlb = dl_ref[pl.ds(o0, tok), :].astype(BF16)
                dw_ref[0] += _mm_tn(dm_ref[pl.ds(o0, tok), :], dlb)
                dd = _mm_nt(dlb, w_ref[0])
                du_ref[pl.ds(o0, tok), :] = -dd
                t = _mm_split(kc_ref[0], dd * ic_ref[0])
                for rr in range(rb):
                    dst = pl.multiple_of((b * rb + rr + front) * GRID_W, GRID_W)
                    tp[pl.ds(dst, GRID_W), :] = t[rr * GRID_W : (rr + 1) * GRID_W] * ir_ref[kk, b * rb + rr]
                return carry

            lax.fori_loop(0, rows // rb, blk, 0)

            def rowl(r, carry):
                base = pl.multiple_of(r * GRID_W, GRID_W)
                s = tp[pl.ds(base, GRID_W), :]
                for k in range(1, w):
                    s = s + tp[pl.ds(base + k * GRID_W, GRID_W), :]
                du_ref[pl.ds(base, GRID_W), :] += s
                return carry

            lax.fori_loop(0, rows, rowl, 0)

        _per_window(group)

    col = pl.BlockSpec((lx, pgw), lambda g: (0, g))
    return pl.pallas_call(
        body,
        name="pool_bwd",
        grid=(ng,),
        in_specs=[col, _gspec(1, pgw, pgw), _gspec(1, tok, tok), _SMEM, _gspec(1, tok, pgw), col],
        out_specs=[col, _gspec(1, pgw, pgw)],
        out_shape=[_sds((lx, ng * pgw)), _sds((ng, pgw, pgw))],
        scratch_shapes=[pltpu.VMEM(((rows + max(POOL_WINDOWS)) * GRID_W, pgw), F32)],
        compiler_params=_cparams(1, VMEM_BIG),
    )(dlin, pool_wf, kct, inv_r, inv_c, dmat)


def _s5_small(lam_re, lam_im, log_dt, b_re, b_im, c_re, c_im):
    t = CHUNK_T
    dt = jnp.exp(log_dt)[..., None]
    zr, zi = lam_re * dt, lam_im * dt
    tau = jnp.arange(t + 1, dtype=F32)
    mag = jnp.exp(zr[..., None] * tau)
    pr, pi = mag * jnp.cos(zi[..., None] * tau), mag * jnp.sin(zi[..., None] * tau)
    ar, ai = pr[..., 1], pi[..., 1]
    den = lam_re * lam_re + lam_im * lam_im
    qr = ((ar - 1.0) * lam_re + ai * lam_im) / den
    qi = (ai * lam_re - (ar - 1.0) * lam_im) / den
    bbr = qr[..., None] * b_re - qi[..., None] * b_im
    bbi = qr[..., None] * b_im + qi[..., None] * b_re
    ctr, cti = jnp.swapaxes(c_re, 2, 3), jnp.swapaxes(c_im, 2, 3)
    lane_pad = lambda v: jnp.pad(jnp.swapaxes(v, 0, 1), ((0, 0), (0, 0), (0, 0), (0, LANES - v.shape[-1])))
    tp = lambda v: jnp.swapaxes(v, 1, 2)
    flip = lambda v: v[..., ::-1]
    pfr, pfi = tp(flip(pr[0, ..., :t])), tp(flip(pi[0, ..., :t]))
    pbr, pbi = tp(pr[1, ..., :t]), tp(pi[1, ..., :t])
    bfr, bfi, bbr_t, bbi_t = tp(bbr[0]), tp(bbi[0]), tp(bbr[1]), tp(bbi[1])
    a1 = jnp.concatenate([pfr, pbr, pfr, pbr], axis=-1)
    a2 = jnp.concatenate([-pfi, -pbi, pfi, pbi], axis=-1)
    b1 = jnp.concatenate([bfr, bbr_t, bfi, bbi_t], axis=-1)
    b2 = jnp.concatenate([bfi, bbi_t, bfr, bbr_t], axis=-1)
    a16 = jnp.concatenate([pr[0, ..., t], pr[1, ..., t], pi[0, ..., t], pi[1, ..., t]], axis=1)
    return (lane_pad(pr), lane_pad(pi), lane_pad(bbr), lane_pad(bbi), lane_pad(ctr), lane_pad(cti), a1, a2, b1, b2), a16


def _split2(v):
    hi = v.astype(BF16)
    return hi, (v - hi.astype(F32)).astype(BF16)


def _dotx(a, b, mm=_mm):
    ah, al = (a, None) if a.dtype == BF16 else _split2(a)
    bh, bl = (b, None) if b.dtype == BF16 else _split2(b)
    out = mm(ah, bh)
    if bl is not None:
        out = out + mm(ah, bl)
    if al is not None:
        out = out + mm(al, bh)
    return out


def _shift_lanes(v, k):
    if k == 0:
        return v
    n = v.shape[-1]
    lane = lax.broadcasted_iota(jnp.int32, v.shape, v.ndim - 1)
    r = pltpu.roll(v, k % n, axis=v.ndim - 1)
    return jnp.where(lane >= k, r, 0.0) if k > 0 else jnp.where(lane < n + k, r, 0.0)


def _op_consts():
    t, h = CHUNK_T, SSM_H
    j, o = np.divmod(np.arange(t * h), h)
    p = np.arange(LANES)[:, None]
    sel = lambda col: (p == col[None, :]).astype(np.float32)
    pw = np.stack([np.concatenate([sel(j), sel(j + 1)], axis=1), np.concatenate([sel(t - 1 - j), sel(t - j)], axis=1)])
    return jnp.asarray(pw, BF16), jnp.asarray(sel(o), BF16), jnp.asarray(sel(j).T, BF16), jnp.asarray(sel(o).T, BF16)


def _op_factors(gi, pr_ref, pi_ref, br_ref, bi_ref, cr_ref, ci_ref, pw_ref, ch_ref):
    n = pr_ref.shape[2]
    c16 = _dotx(jnp.concatenate([cr_ref[gi, 0], ci_ref[gi, 0], cr_ref[gi, 1], ci_ref[gi, 1]], axis=0), ch_ref[...])
    out = []
    for d in range(2):
        e = _dotx(jnp.concatenate([pr_ref[gi, d], pi_ref[gi, d]], axis=0), pw_ref[d])
        c = c16[2 * d * n : (2 * d + 1) * n], c16[(2 * d + 1) * n : (2 * d + 2) * n]
        bst = jnp.concatenate([br_ref[gi, d], -bi_ref[gi, d]], axis=0)
        out.append(((e[:n, :CW], e[n:, :CW]), (e[:n, CW:], e[n:, CW:]), c, bst))
    return out


def _op_spread(m_ref, u_ref, v_ref, gi):
    return _dotx(m_ref[...], jnp.concatenate([_rows_pad(u_ref[gi]), _rows_pad(v_ref[gi])], axis=1))


def _cmul(x, y):
    return x[0] * y[0] - x[1] * y[1], x[0] * y[1] + x[1] * y[0]


def _cmul_conj(g, y):
    return g[0] * y[0] + g[1] * y[1], g[1] * y[0] - g[0] * y[1]


def _rows_pad(v):
    return jnp.concatenate([v, jnp.zeros((LANES - v.shape[0], v.shape[1]), F32)], axis=0)


def _s5_build(small, gb, shards):
    pr, pi, br, bi, cr, ci, a1, a2, b1, b2 = small
    g, _, n, _ = pr.shape
    t, h, ns = CHUNK_T, SSM_H, 4 * n
    pw, ch, rep_s, til_s = _op_consts()
    nr, nin, steps = len(shards), 14, g // gb

    def body(*refs):
        pr_ref, pi_ref, br_ref, bi_ref, cr_ref, ci_ref, a1_ref, a2_ref, b1_ref, b2_ref, pw_ref, ch_ref, rs_ref, ts_ref = refs[:nin]
        t_ref, w_ref, c_ref = refs[nin + nr : nin + nr + 3]
        start, finish = _gather_copies(refs[nin : nin + nr], refs[nin + nr + 3 : nin + 2 * nr + 3], *refs[nin + 2 * nr + 3 :])
        pl.when(pl.program_id(0) == 0)(start)

        def group(gi, carry):
            rows, outs = [], []
            for e16, e1, c16, bst in _op_factors(gi, pr_ref, pi_ref, br_ref, bi_ref, cr_ref, ci_ref, pw_ref, ch_ref):
                rows.append(_dotx(bst, jnp.concatenate(_cmul(c16, e16), axis=0), _mm_tn)[:h])
                outs.append(_cmul(c16, e1))
            c_ref[gi] = jnp.concatenate([outs[0][0], outs[1][0], -outs[0][1], -outs[1][1]], axis=0).astype(BF16)
            for s in range(t):
                t_ref[gi, s * h : (s + 1) * h, :] = (_shift_lanes(rows[0], h * s) + _shift_lanes(rows[1], -h * (t - 1 - s))).astype(BF16)
            ae, be = _op_spread(rs_ref, a1_ref, a2_ref, gi), _op_spread(ts_ref, b1_ref, b2_ref, gi)
            w_ref[gi] = (ae[:, :ns] * be[:, :ns] + ae[:, ns:] * be[:, ns:]).astype(BF16)
            return carry

        lax.fori_loop(0, gb, group, 0)
        pl.when(pl.program_id(0) == steps - 1)(finish)

    sm = pl.BlockSpec((gb, 2, n, LANES), lambda i: (i, 0, 0, 0))
    ab = lambda v: pl.BlockSpec((gb,) + v.shape[1:], lambda i: (i, 0, 0))
    out = pl.pallas_call(
        body,
        name="s5_build",
        grid=(steps,),
        in_specs=[sm] * 6 + [ab(a1), ab(a2), ab(b1), ab(b2), _whole(pw.shape), _whole(ch.shape), _whole(rep_s.shape), _whole(til_s.shape)] + [_HBM] * nr,
        out_specs=[_gspec(gb, CW, CW), _gspec(gb, CW, 4 * n), _gspec(gb, 4 * n, CW)] + [_HBM] * nr,
        out_shape=[_sds((g, CW, CW), BF16), _sds((g, CW, 4 * n), BF16), _sds((g, 4 * n, CW), BF16)] + _gather_shapes(shards),
        scratch_shapes=_gather_sems(nr),
        compiler_params=_cparams(1, VMEM_BIG),
    )(pr, pi, br, bi, cr, ci, a1, a2, b1, b2, pw, ch, rep_s, til_s, *shards)
    return out[:3], out[3:]


def _s5_build_bwd(small, dtsum, dwst, dccat, gb, parts):
    pr, pi, br, bi, cr, ci, a1, a2, b1, b2 = small
    g, _, n, _ = pr.shape
    t, h, ns = CHUNK_T, SSM_H, 4 * n
    pw, ch, rep_s, til_s = _op_consts()
    nr, nin, nout, steps = len(parts), 17, 10, g // gb

    def body(*refs):
        pr_ref, pi_ref, br_ref, bi_ref, cr_ref, ci_ref, a1_ref, a2_ref, b1_ref, b2_ref, pw_ref, ch_ref, rs_ref, ts_ref, dt_ref, dw_ref, dc_ref = refs[:nin]
        dpr_ref, dpi_ref, dbr_ref, dbi_ref, dcr_ref, dci_ref, da1_ref, da2_ref, db1_ref, db2_ref = refs[nin + nr : nin + nr + nout]
        start, finish = _exchange_copies(refs[nin : nin + nr], refs[nin + nr + nout : nin + 2 * nr + nout], *refs[nin + 2 * nr + nout :])
        pl.when(pl.program_id(0) == 0)(start)

        def group(gi, carry):
            drow = [jnp.zeros((h, CW), F32), jnp.zeros((h, CW), F32)]
            for s in range(t):
                blk = dt_ref[gi, s * h : (s + 1) * h, :]
                drow[0] = drow[0] + _shift_lanes(blk, -h * s)
                drow[1] = drow[1] + _shift_lanes(blk, h * (t - 1 - s))
            dcc = dc_ref[gi]
            dcs = []
            for d, (e16, e1, c16, bst) in enumerate(_op_factors(gi, pr_ref, pi_ref, br_ref, bi_ref, cr_ref, ci_ref, pw_ref, ch_ref)):
                dr = _rows_pad(drow[d])
                db = _dotx(jnp.concatenate(_cmul(c16, e16), axis=0), dr, _mm_nt)
                dbr_ref[gi, d] = db[:n]
                dbi_ref[gi, d] = -db[n:]
                dst = _dotx(bst, dr)
                dca = dst[:n], dst[n:]
                gx = dcc[d * n : (d + 1) * n], -dcc[(2 + d) * n : (3 + d) * n]
                dc16, dc16b = _cmul_conj(dca, e16), _cmul_conj(gx, e1)
                dcs += [dc16[0] + dc16b[0], dc16[1] + dc16b[1]]
                de = jnp.concatenate([jnp.concatenate(_cmul_conj(dca, c16), axis=0), jnp.concatenate(_cmul_conj(gx, c16), axis=0)], axis=1)
                dp = _dotx(de, pw_ref[d], _mm_nt)
                dpr_ref[gi, d] = dp[:n]
                dpi_ref[gi, d] = dp[n:]
            dct = _dotx(jnp.concatenate(dcs, axis=0), ch_ref[...], _mm_nt)
            for d in range(2):
                dcr_ref[gi, d] = dct[2 * d * n : (2 * d + 1) * n]
                dci_ref[gi, d] = dct[(2 * d + 1) * n : (2 * d + 2) * n]
            dw = dw_ref[gi]
            ae, be = _op_spread(rs_ref, a1_ref, a2_ref, gi), _op_spread(ts_ref, b1_ref, b2_ref, gi)
            da = _dotx(rs_ref[...], jnp.concatenate([dw * be[:, :ns], dw * be[:, ns:]], axis=1), _mm_tn)
            db = _dotx(ts_ref[...], jnp.concatenate([dw * ae[:, :ns], dw * ae[:, ns:]], axis=1), _mm_tn)
            da1_ref[gi] = da[:t, :ns]
            da2_ref[gi] = da[:t, ns:]
            db1_ref[gi] = db[:h, :ns]
            db2_ref[gi] = db[:h, ns:]
            return carry

        lax.fori_loop(0, gb, group, 0)
        pl.when(pl.program_id(0) == steps - 1)(finish)

    sm = pl.BlockSpec((gb, 2, n, LANES), lambda i: (i, 0, 0, 0))
    ab = lambda v: pl.BlockSpec((gb,) + v.shape[1:], lambda i: (i, 0, 0))
    ops = [_gspec(gb, CW, CW), _gspec(gb, CW, 4 * n), _gspec(gb, 4 * n, CW)]
    out = pl.pallas_call(
        body,
        name="s5_build_bwd",
        grid=(steps,),
        in_specs=[sm] * 6 + [ab(a1), ab(a2), ab(b1), ab(b2), _whole(pw.shape), _whole(ch.shape), _whole(rep_s.shape), _whole(til_s.shape)] + ops + [_HBM] * nr,
        out_specs=[sm] * 6 + [ab(a1), ab(a2), ab(b1), ab(b2)] + [_HBM] * nr,
        out_shape=[_sds(v.shape) for v in small] + [_sds(p.shape, p.dtype) for p in parts],
        scratch_shapes=_exchange_sems(nr),
        compiler_params=_cparams(1, VMEM_BIG),
    )(pr, pi, br, bi, cr, ci, a1, a2, b1, b2, pw, ch, rep_s, til_s, dtsum, dwst, dccat, *parts)
    return out[:nout], out[nout:]


SCAN_UNROLL = 4
GPL = LANES // SSM_H
CW = CHUNK_T * SSM_H
BW = CHUNK_T * LANES


def _chunk_perm():
    o = np.arange(BW)
    src = ((o % CW) // SSM_H) * LANES + (o // CW) * SSM_H + o % SSM_H
    return jnp.asarray(np.arange(BW)[:, None] == src[None, :], BF16)


def _gspec(gb, a, b):
    return pl.BlockSpec((gb, a, b), lambda i: (i, 0, 0))


def _chunk_rows(ref, n):
    return jnp.concatenate([ref[pl.ds(s, n, stride=CHUNK_T), :] for s in range(CHUNK_T)], axis=1)


def _s5_state(proj, col0, uc, perm, wst, ncc, ncl):
    nlb = uc.shape[1] // LANES
    g, _, ns = wst.shape
    nch = ncc + ncl
    half = ns // 2

    def body(p_ref, c_ref, pm_ref, w_ref, u_ref, sr_ref, si_ref):
        xrows = jnp.concatenate([_chunk_rows(c_ref, ncc), _chunk_rows(p_ref, ncl)], axis=0).astype(BF16)
        u = _mm(xrows, pm_ref[...]).astype(BF16)
        u_ref[0] = u
        for gi in range(GPL):
            s = _mm(u[:, gi * CW : (gi + 1) * CW], w_ref[gi].astype(BF16))
            sr_ref[gi] = s[:, :half]
            si_ref[gi] = s[:, half:]

    return pl.pallas_call(
        body,
        name="s5_state",
        grid=(nlb,),
        in_specs=[
            pl.BlockSpec((ncl * CHUNK_T, LANES), lambda i: (0, col0 + i)),
            pl.BlockSpec((ncc * CHUNK_T, LANES), lambda i: (0, i)),
            _whole(perm.shape),
            _gspec(GPL, CW, ns),
        ],
        out_specs=[pl.BlockSpec((1, nch, BW), lambda i: (i, 0, 0)), _gspec(GPL, nch, half), _gspec(GPL, nch, half)],
        out_shape=[_sds((nlb, nch, BW), BF16), _sds((g, nch, half)), _sds((g, nch, half))],
        compiler_params=_cparams(1, VMEM_BIG),
    )(proj, uc, perm, wst)


def _unrolled_loop(n, step, init):
    u = SCAN_UNROLL if n % SCAN_UNROLL == 0 else 1

    def trip(i, c):
        for k in range(u):
            c = step(i * u + k, c)
        return c

    return lax.fori_loop(0, n // u, trip, init)


def _lane_masks(gb, half):
    lane = lax.broadcasted_iota(jnp.int32, (gb, half), 1)
    return lane < (half // 2)


def _roll_half(v):
    return pltpu.roll(v, v.shape[-1] // 2, axis=v.ndim - 1)


def _pack_dirs(lo, hi, mf):
    return jnp.where(mf, lo, _roll_half(hi)), jnp.where(mf, _roll_half(lo), hi)


def _unpack_dirs(pf, pb, mf):
    return jnp.where(mf, pf, _roll_half(pb)), jnp.where(mf, _roll_half(pf), pb)


def _packed_spec(gb, nch, half, buffers=None):
    mode = {} if buffers is None else dict(pipeline_mode=pl.Buffered(buffers))
    if gb == 2 * GPL:
        return pl.BlockSpec((GPL, nch, half), lambda i: (i // 2, 0, 0), **mode)
    assert gb == GPL
    return pl.BlockSpec((gb // 2, nch, half), lambda i: (i, 0, 0), **mode)


def _unpacked(pf_ref, pb_ref, j, gb, mf):
    hp = gb // 2
    if gb == 2 * GPL:
        lo, hi = _unpack_dirs(pf_ref[j], pb_ref[j], mf)
        return jnp.where(pl.program_id(0) % 2 == 0, lo, hi)
    lo, hi = _unpack_dirs(pf_ref[j % hp], pb_ref[j % hp], mf)
    return lo if j < hp else hi


def _scan_rows(ref, row, hp, nch, mf):
    return _pack_dirs(ref[pl.ds(row, hp, stride=nch), :], ref[pl.ds(hp * nch + row, hp, stride=nch), :], mf)


def _s5_scan_fwd(s_re, s_im, a16, ncc, nch, gb):
    g, ns = a16.shape
    half = ns // 2
    hp = gb // 2

    def body(sr_ref, si_ref, a_ref, hfr_ref, hfi_ref, hbr_ref, hbi_ref):
        mf = _lane_masks(hp, half)
        afr, abr = _pack_dirs(a_ref[:hp, :half], a_ref[hp:, :half], mf)
        afi, abi = _pack_dirs(a_ref[:hp, half:], a_ref[hp:, half:], mf)
        zero = jnp.zeros((hp, half), F32)

        def step(t, c):
            fr, fi, br, bi = c
            rb = jnp.where(t < ncc, ncc - 1 - t, nch - 1 - (t - ncc))
            hfr_ref[pl.ds(t, hp, stride=nch), :] = fr
            hfi_ref[pl.ds(t, hp, stride=nch), :] = fi
            hbr_ref[pl.ds(rb, hp, stride=nch), :] = br
            hbi_ref[pl.ds(rb, hp, stride=nch), :] = bi
            sfr, sfi = _scan_rows(sr_ref, t, hp, nch, mf)[0], _scan_rows(si_ref, t, hp, nch, mf)[0]
            sbr, sbi = _scan_rows(sr_ref, rb, hp, nch, mf)[1], _scan_rows(si_ref, rb, hp, nch, mf)[1]
            return (afr * fr - afi * fi + sfr, afr * fi + afi * fr + sfi, abr * br - abi * bi + sbr, abr * bi + abi * br + sbi)

        _unrolled_loop(nch, step, (zero, zero, zero, zero))

    blk = pl.BlockSpec((gb * nch, half), lambda i: (i, 0))
    pblk = pl.BlockSpec((hp * nch, half), lambda i: (i, 0))
    return pl.pallas_call(
        body,
        name="s5_scan_fwd",
        grid=(g // gb,),
        in_specs=[blk, blk, pl.BlockSpec((gb, ns), lambda i: (i, 0))],
        out_specs=[pblk] * 4,
        out_shape=[_sds((g // 2 * nch, half))] * 4,
        compiler_params=_cparams(1, VMEM_BIG),
    )(s_re, s_im, a16)


def _token_rows_store(ref, val, n):
    for s in range(CHUNK_T):
        ref[pl.ds(s, n, stride=CHUNK_T), :] = val[:, s * LANES : (s + 1) * LANES]


def _s5_out(u_all, h4, tsum, ccat, perm, ncc, lx, gb):
    nlb, nch, _ = u_all.shape
    g, ns, _ = ccat.shape
    ncl = nch - ncc
    half = ns // 2

    def body(u_ref, hfr_ref, hfi_ref, hbr_ref, hbi_ref, t_ref, c_ref, pm_ref, y_ref):
        parts = []
        mf = _lane_masks(nch, half)
        for gi in range(GPL):
            u = u_ref[0, ncc:, gi * CW : (gi + 1) * CW]
            hr, hi = _unpacked(hfr_ref, hbr_ref, gi, gb, mf), _unpacked(hfi_ref, hbi_ref, gi, gb, mf)
            hs = jnp.concatenate([hr[ncc:], hi[ncc:]], axis=1).astype(BF16)
            parts.append(_mm(u, t_ref[gi].astype(BF16)) + _mm(hs, c_ref[gi].astype(BF16)))
        _token_rows_store(y_ref, _mm_nt(jnp.concatenate(parts, axis=1).astype(BF16), pm_ref[...]), ncl)

    return pl.pallas_call(
        body,
        name="s5_out",
        grid=(nlb,),
        in_specs=[pl.BlockSpec((1, nch, BW), lambda i: (i, 0, 0))] + [_packed_spec(gb, nch, half)] * 4 + [_gspec(GPL, CW, CW), _gspec(GPL, ns, CW), _whole(perm.shape)],
        out_specs=pl.BlockSpec((lx, LANES), lambda i: (0, i)),
        out_shape=_sds((lx, nlb * LANES)),
        compiler_params=_cparams(1, VMEM_BIG),
    )(u_all, *h4, tsum, ccat, perm)


def _s5_dstate(dypre, perm, ccat, ncc):
    lx, sw = dypre.shape
    nlb = sw // LANES
    g, ns, _ = ccat.shape
    ncl = lx // CHUNK_T
    nch = ncl + ncc
    half = ns // 2

    def body(dy_ref, pm_ref, c_ref, dyr_ref, dr_ref, di_ref):
        dy = _mm(_chunk_rows(dy_ref, ncl).astype(BF16), pm_ref[...]).astype(BF16)
        dyr_ref[0] = dy
        for gi in range(GPL):
            dh = _mm_nt(dy[:, gi * CW : (gi + 1) * CW], c_ref[gi].astype(BF16))
            dr_ref[gi, :ncc, :] = jnp.zeros((ncc, half), F32)
            di_ref[gi, :ncc, :] = jnp.zeros((ncc, half), F32)
            dr_ref[gi, ncc:, :] = dh[:, :half]
            di_ref[gi, ncc:, :] = dh[:, half:]

    return pl.pallas_call(
        body,
        name="s5_dstate",
        grid=(nlb,),
        in_specs=[pl.BlockSpec((lx, LANES), lambda i: (0, i)), _whole(perm.shape), _gspec(GPL, ns, CW)],
        out_specs=[pl.BlockSpec((1, ncl, BW), lambda i: (i, 0, 0)), _gspec(GPL, nch, half), _gspec(GPL, nch, half)],
        out_shape=[_sds((nlb, ncl, BW), BF16), _sds((g, nch, half)), _sds((g, nch, half))],
        compiler_params=_cparams(1, VMEM_BIG),
    )(dypre, perm, ccat)


def _s5_scan_bwd(dh_re, dh_im, a16, ncc, nch, gb):
    g, ns = a16.shape
    half = ns // 2
    ncl = nch - ncc
    hp = gb // 2

    def body(dr_ref, di_ref, a_ref, sfr_ref, sfi_ref, sbr_ref, sbi_ref):
        mf = _lane_masks(hp, half)
        afr, abr = _pack_dirs(a_ref[:hp, :half], a_ref[hp:, :half], mf)
        afi, abi = _pack_dirs(a_ref[:hp, half:], a_ref[hp:, half:], mf)
        zero = jnp.zeros((hp, half), F32)

        def step(t, c):
            lfr, lfi, lbr, lbi = c
            pf = nch - 1 - t
            pb = jnp.where(t < ncl, ncc + t, t - ncl)
            sfr_ref[pl.ds(pf, hp, stride=nch), :] = lfr
            sfi_ref[pl.ds(pf, hp, stride=nch), :] = lfi
            sbr_ref[pl.ds(pb, hp, stride=nch), :] = lbr
            sbi_ref[pl.ds(pb, hp, stride=nch), :] = lbi
            dfr, dfi = _scan_rows(dr_ref, pf, hp, nch, mf)[0], _scan_rows(di_ref, pf, hp, nch, mf)[0]
            dbr, dbi = _scan_rows(dr_ref, pb, hp, nch, mf)[1], _scan_rows(di_ref, pb, hp, nch, mf)[1]
            return (dfr + afr * lfr + afi * lfi, dfi + afr * lfi - afi * lfr, dbr + abr * lbr + abi * lbi, dbi + abr * lbi - abi * lbr)

        _unrolled_loop(nch, step, (zero,) * 4)

    blk = pl.BlockSpec((gb * nch, half), lambda i: (i, 0))
    pblk = pl.BlockSpec((hp * nch, half), lambda i: (i, 0))
    return pl.pallas_call(
        body,
        name="s5_scan_bwd",
        grid=(g // gb,),
        in_specs=[blk, blk, pl.BlockSpec((gb, ns), lambda i: (i, 0))],
        out_specs=[pblk] * 4,
        out_shape=[_sds((g // 2 * nch, half))] * 4,
        compiler_params=_cparams(1, VMEM_BIG),
    )(dh_re, dh_im, a16)


def _s5_du(dy_all, ds4, tsum, wst, perm, ncc, lx, lc, gb):
    nlb, ncl, _ = dy_all.shape
    g, _, ns = wst.shape
    nch = ncc + ncl
    half = ns // 2

    def body(dy_ref, sfr_ref, sfi_ref, sbr_ref, sbi_ref, t_ref, w_ref, pm_ref, du_ref, dc_ref):
        parts = []
        mf = _lane_masks(nch, half)
        for gi in range(GPL):
            ds = jnp.concatenate([_unpacked(sfr_ref, sbr_ref, gi, gb, mf), _unpacked(sfi_ref, sbi_ref, gi, gb, mf)], axis=1).astype(BF16)
            d_all = _mm_nt(ds, w_ref[gi].astype(BF16))
            d_lat = d_all[ncc:] + _mm_nt(dy_ref[0, :, gi * CW : (gi + 1) * CW], t_ref[gi].astype(BF16))
            parts.append(jnp.concatenate([d_all[:ncc], d_lat], axis=0))
        du = _mm_nt(jnp.concatenate(parts, axis=1).astype(BF16), pm_ref[...])
        _token_rows_store(dc_ref, du[:ncc], ncc)
        _token_rows_store(du_ref, du[ncc:], ncl)

    return pl.pallas_call(
        body,
        name="s5_du",
        grid=(nlb,),
        in_specs=[pl.BlockSpec((1, ncl, BW), lambda i: (i, 0, 0))] + [_packed_spec(gb, nch, half)] * 4 + [_gspec(GPL, CW, CW), _gspec(GPL, CW, ns), _whole(perm.shape)],
        out_specs=[pl.BlockSpec((lx, LANES), lambda i: (0, i)), pl.BlockSpec((lc, LANES), lambda i: (0, i))],
        out_shape=[_sds((lx, nlb * LANES)), _sds((lc, nlb * LANES))],
        compiler_params=_cparams(1, VMEM_BIG),
    )(dy_all, *ds4, tsum, wst, perm)


def _s5_op_grads(u_all, dy_all, ds4, h4, ncc, gb):
    nlb, nch, _ = u_all.shape
    half = ds4[0].shape[2]
    g = 2 * ds4[0].shape[0]
    ns = 2 * half

    def body(u_ref, dy_ref, sfr_ref, sfi_ref, sbr_ref, sbi_ref, hfr_ref, hfi_ref, hbr_ref, hbi_ref, dw_ref, dt_ref, dc_ref, da_ref):
        das = []
        mf = _lane_masks(nch, half)
        for gi in range(GPL):
            cols = slice(gi * CW, (gi + 1) * CW)
            sr, si = _unpacked(sfr_ref, sbr_ref, gi, gb, mf), _unpacked(sfi_ref, sbi_ref, gi, gb, mf)
            hr, hi = _unpacked(hfr_ref, hbr_ref, gi, gb, mf), _unpacked(hfi_ref, hbi_ref, gi, gb, mf)
            ds = jnp.concatenate([sr, si], axis=1).astype(BF16)
            dy = dy_ref[0, :, cols]
            dw_ref[gi] = _mm_tn(u_ref[0, :, cols], ds)
            dt_ref[gi] = _mm_tn(u_ref[0, ncc:, cols], dy)
            hs = jnp.concatenate([hr[ncc:], hi[ncc:]], axis=1).astype(BF16)
            dc_ref[gi] = _mm_tn(hs, dy)
            das.append(jnp.concatenate([_colsum(hr * sr + hi * si), _colsum(hr * si - hi * sr)], axis=1))
        da_ref[...] = jnp.concatenate(das, axis=0)

    return pl.pallas_call(
        body,
        name="s5_op_grads",
        grid=(nlb,),
        in_specs=[pl.BlockSpec((1, nch, BW), lambda i: (i, 0, 0)), pl.BlockSpec((1, nch - ncc, BW), lambda i: (i, 0, 0))]
        + [_packed_spec(gb, nch, half)] * 4
        + [_packed_spec(gb, nch, half, buffers=1)] * 4,
        out_specs=[_gspec(GPL, CW, ns), _gspec(GPL, CW, CW), _gspec(GPL, ns, CW), pl.BlockSpec((GPL, ns), lambda i: (i, 0))],
        out_shape=[_sds((g, CW, ns)), _sds((g, CW, CW)), _sds((g, ns, CW)), _sds((g, ns))],
        compiler_params=_cparams(1, VMEM_BIG),
    )(u_all, dy_all, *ds4, *h4)


def _merge_fwd(y_ssm, proj, lin, x, tgt, glu_wg, out_w, d_skip, glu_b, pscale, gate, fin_g, tm):
    lx, d = x.shape
    sw = y_ssm.shape[1]
    mix = out_w.shape[0]
    pw = mix - sw
    cb2 = glu_wg.shape[2]
    nh = NDEV // 2

    def body(y_ref, u_ref, z_ref, l_ref, x_ref, t_ref, gw_ref, ow_ref, dk_ref, gb_ref, ps_ref, gt_ref, fg_ref,
             ypre_ref, y12_ref, br_ref, dxo_ref, loss_ref, dfg_ref, dgt_ref):
        @pl.when(pl.program_id(0) == 0)
        def _():
            loss_ref[...] = jnp.zeros_like(loss_ref)
            dfg_ref[...] = jnp.zeros_like(dfg_ref)
            dgt_ref[...] = jnp.zeros_like(dgt_ref)

        ypre = y_ref[...] + dk_ref[...] * u_ref[...]
        ypre_ref[...] = ypre
        yg = _gelu(ypre).astype(BF16)
        outs = []
        for j in range(nh):
            y1 = _mm(yg, gw_ref[j]) + gb_ref[:, j * cb2 : (j + 1) * cb2]
            y2 = _mm(yg, gw_ref[nh + j]) + gb_ref[:, (nh + j) * cb2 : (nh + j + 1) * cb2]
            y12_ref[:, j * cb2 : (j + 1) * cb2] = y1
            y12_ref[:, (nh + j) * cb2 : (nh + j + 1) * cb2] = y2
            outs.append(y1 * _sigmoid(y2))
        lin = l_ref[...] * ps_ref[...]
        brb = (jnp.concatenate([lin] + outs, axis=1) * _silu(z_ref[...])).astype(BF16)
        br_ref[...] = brb
        mixv = _mm(brb, ow_ref[...])
        xo = x_ref[...] + gt_ref[...] * mixv
        r2 = lax.rsqrt(_rowmean(xo * xo) + EPS)
        xh = xo * r2
        fg = fg_ref[...]
        e = xh * fg - t_ref[...]
        loss_ref[...] += 0.5 * _colsum(_rowmean(e * e))
        dy = e * (1.0 / d)
        dfg_ref[...] += _colsum(dy * xh)
        gy = dy * fg
        dxo = r2 * (gy - xh * _rowmean(gy * xh))
        dxo_ref[...] = dxo
        dgt_ref[...] += _colsum(dxo * mixv)

    vec = _whole((1, d))
    return pl.pallas_call(
        body,
        name="merge_fwd",
        grid=(lx // tm,),
        in_specs=[_rows(tm, sw), _rows(tm, sw, 1), _rows(tm, mix, 1), _rows(tm, pw), _rows(tm, d), _rows(tm, d), _whole(glu_wg.shape), _whole(out_w.shape), _whole((1, sw)), _whole((1, 2 * sw)), _whole((1, pw)), vec, vec],
        out_specs=[_rows(tm, sw), _rows(tm, 2 * sw), _rows(tm, mix), _rows(tm, d), _acc((1, 1)), _acc((1, d)), _acc((1, d))],
        out_shape=[_sds((lx, sw)), _sds((lx, 2 * sw)), _sds((lx, mix), BF16), _sds((lx, d)), _sds((1, 1)), _sds((1, d)), _sds((1, d))],
        compiler_params=_cparams(1, VMEM_BIG),
    )(y_ssm, proj, proj, lin, x, tgt, glu_wg, out_w, d_skip, glu_b, pscale, gate, fin_g)


def _out_bwd(dxo, gate, branch, out_w, lin, y12, proj, pscale, tm):
    lx, d = dxo.shape
    mix = out_w.shape[0]
    pw = lin.shape[1]
    sw = mix - pw
    nb = lx // tm

    def body(dxo_ref, gt_ref, br_ref, ow_ref, l_ref, y_ref, z_ref, ps_ref, dz_ref, dl_ref, dg_ref, dps_ref, dow_hbm, acc, sem):
        i = pl.program_id(0)

        @pl.when(i == 0)
        def _():
            acc[...] = jnp.zeros_like(acc)
            dps_ref[...] = jnp.zeros_like(dps_ref)

        dmix = (dxo_ref[...] * gt_ref[...]).astype(BF16)
        dbr = _mm_nt(dmix, ow_ref[...])
        acc[...] += _mm_tn(br_ref[...], dmix)
        z = z_ref[...]
        sg = _sigmoid(z)
        dbp = dbr * (z * sg)
        y1, y2 = y_ref[:, :sw], y_ref[:, sw:]
        s2 = _sigmoid(y2)
        ps = ps_ref[...]
        lin = l_ref[...]
        bp = jnp.concatenate([lin * ps, y1 * s2], axis=1)
        dz_ref[...] = dbr * bp * (sg * (1.0 + z * (1.0 - sg)))
        dlp = dbp[:, :pw]
        dl_ref[...] = dlp * ps
        dps_ref[...] += _colsum(dlp * lin)
        dss = dbp[:, pw:]
        dg_ref[...] = jnp.concatenate([dss * s2, dss * y1 * s2 * (1.0 - s2)], axis=1).astype(BF16)

        @pl.when(i == nb - 1)
        def _():
            cp = pltpu.make_async_copy(acc, dow_hbm, sem)
            cp.start()
            cp.wait()

    return pl.pallas_call(
        body,
        name="out_bwd",
        grid=(nb,),
        in_specs=[_rows(tm, d), _whole((1, d)), _rows(tm, mix), _whole(out_w.shape), _rows(tm, pw), _rows(tm, 2 * sw), _rows(tm, mix, 1), _whole((1, pw))],
        out_specs=[_rows(tm, mix), _rows(tm, pw), _rows(tm, 2 * sw), _acc((1, pw)), _ANY],
        out_shape=[_sds((lx, mix)), _sds((lx, pw)), _sds((lx, 2 * sw), BF16), _sds((1, pw)), _sds((mix, d))],
        scratch_shapes=[pltpu.VMEM((mix, d), F32), pltpu.SemaphoreType.DMA],
        compiler_params=_cparams(1, VMEM_BIG),
    )(dxo, gate, branch, out_w, lin, y12, proj, pscale)


def _glu_bwd(dg12, ypre, proj, glu_wg, d_skip, tm):
    lx, sw = ypre.shape
    cb2 = glu_wg.shape[2]
    nb = lx // tm

    def body(dg_ref, yp_ref, u_ref, gw_ref, dk_ref, dyp_ref, dgb_ref, dd_ref, dgw_hbm, acc, sem):
        i = pl.program_id(0)

        @pl.when(i == 0)
        def _():
            acc[...] = jnp.zeros_like(acc)
            dgb_ref[...] = jnp.zeros_like(dgb_ref)
            dd_ref[...] = jnp.zeros_like(dd_ref)

        ypre = yp_ref[...]
        ygb = _gelu(ypre).astype(BF16)
        dg = dg_ref[...]
        dyg = jnp.zeros((tm, sw), F32)
        for j in range(NDEV):
            dgj = dg[:, j * cb2 : (j + 1) * cb2]
            dyg = dyg + _mm_nt(dgj, gw_ref[j])
            acc[j] += _mm_tn(ygb, dgj)
        dgb_ref[...] += _colsum(dg.astype(F32))
        dyp = dyg * _gelu_grad(ypre)
        dyp_ref[...] = dyp
        dd_ref[...] += _colsum(dyp * u_ref[...])

        @pl.when(i == nb - 1)
        def _():
            cp = pltpu.make_async_copy(acc, dgw_hbm, sem)
            cp.start()
            cp.wait()

    return pl.pallas_call(
        body,
        name="glu_bwd",
        grid=(nb,),
        in_specs=[_rows(tm, 2 * sw), _rows(tm, sw), _rows(tm, sw, 1), _whole(glu_wg.shape), _whole((1, sw))],
        out_specs=[_rows(tm, sw), _acc((1, 2 * sw)), _acc((1, sw)), _ANY],
        out_shape=[_sds((lx, sw)), _sds((1, 2 * sw)), _sds((1, sw)), _sds(glu_wg.shape)],
        scratch_shapes=[pltpu.VMEM(glu_wg.shape, F32), pltpu.SemaphoreType.DMA],
        compiler_params=_cparams(1, VMEM_BIG),
    )(dg12, ypre, proj, glu_wg, d_skip)


def _pack(parts, total):
    flat = jnp.concatenate([p.reshape(-1).astype(F32) for p in parts])
    return jnp.pad(flat, (0, total - flat.shape[0]))


def _unpack(flat, shapes):
    out, off = [], 0
    for s in shapes:
        n = int(np.prod(s)) if len(s) else 1
        out.append(flat[off : off + n].reshape(s))
        off += n
    return out


def kernel(x, c, ctx, c_ctx, ada_w, ada_b, norm_g, in_w, pool_w, pool_scale, s5_lam_re, s5_lam_im, s5_log_dt, s5_b_re, s5_b_im, s5_c_re, s5_c_im, s5_d, glu_w, glu_b, out_w, final_g, loss_target, m_c_ctx, m_ada_w, m_ada_b, m_norm_g, m_in_w, m_pool_w, m_pool_scale, m_s5_lam_re, m_s5_lam_im, m_s5_log_dt, m_s5_b_re, m_s5_b_im, m_s5_c_re, m_s5_c_im, m_s5_d, m_glu_w, m_glu_b, m_out_w, m_final_g, v_c_ctx, v_ada_w, v_ada_b, v_norm_g, v_in_w, v_pool_w, v_pool_scale, v_s5_lam_re, v_s5_lam_im, v_s5_log_dt, v_s5_b_re, v_s5_b_im, v_s5_c_re, v_s5_c_im, v_s5_d, v_glu_w, v_glu_b, v_out_w, v_final_g):
    xr, tgt, xc = x[0], loss_target[0], ctx[0]
    lx, d = xr.shape
    lc = xc.shape[0]
    mix = out_w.shape[1] * NDEV
    sw = glu_w.shape[1]
    pw = mix - sw
    pgw = pw // len(POOL_WINDOWS)
    ngrp = sw // SSM_H
    cb = in_w.shape[2]
    cb2 = glu_w.shape[2]
    nmod = ada_w.shape[2]
    ncc, ncl = lc // CHUNK_T, lx // CHUNK_T
    nch = ncc + ncl
    tm = min(256, lx)
    tm_acc = min(512, lx)
    assert mix == d and pw == sw and 2 * cb == pw and NDEV * cb2 == 2 * sw and lx % GRID_W == 0
    mx, my, mc = _my_pos()
    me = 4 * mx + 2 * my + mc

    (c_all,) = _all_gather([c], "gather_c")
    call = jnp.concatenate([c_all.reshape(NDEV, d), c_ctx.reshape(1, d), jnp.zeros((NDEV - 1, d), F32)], axis=0)
    ada_w_l = ada_w[0]
    ada_b_l = lax.dynamic_slice_in_dim(ada_b, me * nmod, nmod, axis=1)
    m_loc = _mod_fwd(call, ada_w_l, ada_b_l)
    (m_all,) = _all_gather([m_loc], "gather_mod")
    mod = lax.dynamic_index_in_dim(m_all, me, axis=1, keepdims=False).reshape(1, NDEV * nmod)
    mod_c = m_all[:, NDEV, :].reshape(1, NDEV * nmod)
    shift, scale, gate = mod[:, :d], mod[:, d : 2 * d], mod[:, 2 * d :]
    shift_c, scale_c = mod_c[:, :d], mod_c[:, d : 2 * d]

    s5_params = tuple(p[0] for p in (s5_lam_re, s5_lam_im, s5_log_dt, s5_b_re, s5_b_im, s5_c_re, s5_c_im))
    (small, a16), small_vjp = jax.vjp(_s5_small, *s5_params)
    gb_ops = min(8, ngrp)
    (tsum, wst, ccat), (in_wg,) = _s5_build(small, gb_ops, [in_w[0].astype(BF16)])

    ssm_blocks = (2, 3)
    (proj, hb), (glu_wg, out_wg, pool_wg) = _in_proj(
        xr, norm_g, scale, shift, in_wg, tuple(range(NDEV)), tm, "in_proj",
        shards=[glu_w[0].astype(BF16), out_w[0].astype(BF16), pool_w[0].reshape(-1, pgw).astype(BF16)],
    )
    out_wf = out_wg.reshape(mix, d)
    pool_wf = pool_wg.reshape(NDEV, len(POOL_WINDOWS), pgw // NDEV, pgw).transpose(1, 0, 2, 3).reshape(len(POOL_WINDOWS), pgw, pgw)
    (uc, hcb), _ = _in_proj(xc, norm_g, scale_c, shift_c, in_wg, ssm_blocks, min(tm, lc), "in_proj_ctx")
    lin, dmat = _pool_fwd(proj, pool_wf)

    gb_scan = min(16, ngrp)
    half = wst.shape[2] // 2
    perm = _chunk_perm()
    flat = lambda a: a.reshape(ngrp * nch, half)
    pairs = lambda arrs: [a.reshape(ngrp // 2, nch, half) for a in arrs]
    u_all, s_re, s_im = _s5_state(proj, pw // LANES, uc, perm, wst, ncc, ncl)
    h4 = pairs(_s5_scan_fwd(flat(s_re), flat(s_im), a16, ncc, nch, gb_scan))
    y_ssm = _s5_out(u_all, h4, tsum, ccat, perm, ncc, lx, gb_scan)

    ypre, y12, branch, dxo, loss_l, dfg, dgate = _merge_fwd(
        y_ssm, proj, lin, xr, tgt, glu_wg, out_wf, s5_d, glu_b, pool_scale, gate, final_g.reshape(1, d), tm
    )

    dz, dlin, dg12, dps, d_out_w = _out_bwd(dxo, gate, branch, out_wf, lin, y12, proj, pool_scale, tm)
    dypre, dglu_b, dd_skip, d_glu_w = _glu_bwd(dg12, ypre, proj, glu_wg, s5_d, tm_acc)

    dy_all, dh_re, dh_im = _s5_dstate(dypre, perm, ccat, ncc)
    ds4 = pairs(_s5_scan_bwd(flat(dh_re), flat(dh_im), a16, ncc, nch, gb_scan))
    du5, duc = _s5_du(dy_all, ds4, tsum, wst, perm, ncc, lx, lc, gb_scan)
    dwst, dtsum, dccat, da16 = _s5_op_grads(u_all, dy_all, ds4, h4, ncc, gb_scan)
    dup, dpool_w = _pool_bwd(dlin, pool_wf, dmat)

    npw = len(POOL_WINDOWS)
    dpool_s = dpool_w.reshape(npw, NDEV, pgw // NDEV, pgw).transpose(1, 0, 2, 3).reshape(NDEV, npw * (pgw // NDEV), pgw)
    parts_a = _pair_partials([d_glu_w, d_out_w.reshape(NDEV, mix // NDEV, d), dpool_s], [BF16] * 3, "grads_a")
    dpcb, dsc_c, dsh_c, dg_c = _in_bwd_ctx(xc, duc, in_wg, ssm_blocks, norm_g, scale_c, shift_c)
    grad_x, dpb, dsc, dsh, dnorm_g = _in_bwd(xr, dxo, dup, du5, dypre, dz, in_wg, norm_g, scale, shift, s5_d, dg_c, tm)
    d_in_w, (land_glu, land_out, land_pool) = _in_w_grad(hb, dpb, hcb, dpcb, cb, min(2 * tm_acc, lx), parts=parts_a)

    dmod = jnp.concatenate(
        [jnp.concatenate([dsh, dsc, dgate], axis=1), jnp.concatenate([dsh_c, dsc_c, jnp.zeros((1, d), F32)], axis=1)], axis=0
    )
    (dmod_all,) = _all_gather([dmod], "gather_dmod")
    dmod_l = lax.dynamic_slice_in_dim(dmod_all, me * nmod, nmod, axis=2)
    g_ada_w, g_ada_b_l, cctx_part = _mod_bwd(call, dmod_l[:, 0, :], dmod_l[:, 1, :], ada_w_l)

    parts_b = _pair_partials([d_in_w], [BF16], "grads_b")
    dsmall, (land_in,) = _s5_build_bwd(small, dtsum, dwst, dccat, gb_ops, parts_b)
    ds5 = small_vjp((tuple(dsmall), da16))

    small_names = ["norm_g", "pool_scale", "s5_lam_re", "s5_lam_im", "s5_log_dt", "s5_b_re", "s5_b_im", "s5_c_re", "s5_c_im", "s5_d", "glu_b", "final_g", "c_ctx"]
    small_w = dict(norm_g=norm_g, pool_scale=pool_scale, s5_lam_re=s5_lam_re, s5_lam_im=s5_lam_im, s5_log_dt=s5_log_dt, s5_b_re=s5_b_re, s5_b_im=s5_b_im, s5_c_re=s5_c_re, s5_c_im=s5_c_im, s5_d=s5_d, glu_b=glu_b, final_g=final_g, c_ctx=c_ctx)
    small_m = dict(norm_g=m_norm_g, pool_scale=m_pool_scale, s5_lam_re=m_s5_lam_re, s5_lam_im=m_s5_lam_im, s5_log_dt=m_s5_log_dt, s5_b_re=m_s5_b_re, s5_b_im=m_s5_b_im, s5_c_re=m_s5_c_re, s5_c_im=m_s5_c_im, s5_d=m_s5_d, glu_b=m_glu_b, final_g=m_final_g, c_ctx=m_c_ctx)
    small_v = dict(norm_g=v_norm_g, pool_scale=v_pool_scale, s5_lam_re=v_s5_lam_re, s5_lam_im=v_s5_lam_im, s5_log_dt=v_s5_log_dt, s5_b_re=v_s5_b_re, s5_b_im=v_s5_b_im, s5_c_re=v_s5_c_re, s5_c_im=v_s5_c_im, s5_d=v_s5_d, glu_b=v_glu_b, final_g=v_final_g, c_ctx=v_c_ctx)
    small_g = dict(norm_g=dnorm_g, pool_scale=dps, s5_lam_re=ds5[0], s5_lam_im=ds5[1], s5_log_dt=ds5[2], s5_b_re=ds5[3], s5_b_im=ds5[4], s5_c_re=ds5[5], s5_c_im=ds5[6], s5_d=dd_skip, glu_b=dglu_b, final_g=dfg, c_ctx=cctx_part)
    shapes = [small_w[k].shape for k in small_names]
    nsmall = sum(int(np.prod(s)) for s in shapes) + 1
    unit = NDEV * 8 * LANES
    tot = -(-nsmall // unit) * unit
    rows = tot // LANES
    gpack = _pack([small_g[k] for k in small_names] + [loss_l], tot).reshape(NDEV, rows // NDEV, LANES)
    (land_small,) = _reduce_scatter([gpack], [F32], "small")
    gsum = _sum_slots(land_small, "sum_small")
    (gall,) = _all_gather([gsum], "gather_small")
    (g_ada_b_all,) = _all_gather([g_ada_b_l], "gather_ada_b")
    gflat = gall.reshape(tot)
    loss = gflat[nsmall - 1]

    nab = NDEV * nmod
    tot2 = tot + -(-nab // (8 * LANES)) * (8 * LANES)
    g2 = jnp.concatenate([gflat, jnp.pad(g_ada_b_all.reshape(nab), (0, tot2 - tot - nab))]).reshape(tot2 // LANES, LANES)

    def pack2(dct, ab):
        return jnp.concatenate(
            [_pack([dct[k] for k in small_names] + [jnp.zeros((1,), F32)], tot), jnp.pad(ab.reshape(nab), (0, tot2 - tot - nab))]
        ).reshape(tot2 // LANES, LANES)

    _, d2, m2, v2 = _adam(g2, pack2(small_w, ada_b), pack2(small_m, m_ada_b), pack2(small_v, v_ada_b), "adam_small")

    def unpack2(flat2):
        flat2 = flat2.reshape(tot2)
        vals = dict(zip(small_names, _unpack(flat2, shapes)))
        vals["ada_b"] = flat2[tot : tot + nab].reshape(ada_b.shape)
        return vals

    res = {k: unpack2(a) for k, a in (("g", g2), ("d", d2), ("m", m2), ("v", v2))}

    def shard(name, g, land, w, m, v):
        shp = w.shape
        w2, m2_, v2_ = (a.reshape(-1, shp[-1]) for a in (w, m, v))
        out = _adam(g, w2, m2_, v2_, "adam_" + name, land=land)
        for k, a in zip(("g", "d", "m", "v"), out):
            res[k][name] = a.reshape(shp)

    shard("ada_w", g_ada_w, None, ada_w, m_ada_w, v_ada_w)
    shard("in_w", None, land_in, in_w, m_in_w, v_in_w)
    shard("pool_w", None, land_pool, pool_w, m_pool_w, v_pool_w)
    shard("glu_w", None, land_glu, glu_w, m_glu_w, v_glu_w)
    shard("out_w", None, land_out, out_w, m_out_w, v_out_w)

    names = ["c_ctx", "ada_w", "ada_b", "norm_g", "in_w", "pool_w", "pool_scale", "s5_lam_re", "s5_lam_im", "s5_log_dt", "s5_b_re", "s5_b_im", "s5_c_re", "s5_c_im", "s5_d", "glu_w", "glu_b", "out_w", "final_g"]
    return (loss, grad_x[None], *[res["g"][n] for n in names], *[res["d"][n] for n in names], *[res["m"][n] for n in names], *[res["v"][n] for n in names])
```

```python
import numpy as np

import jax
import jax.numpy as jnp
from jax import lax
from jax.experimental import pallas as pl
from jax.experimental.pallas import tpu as pltpu

F32 = jnp.float32
BF16 = jnp.bfloat16
NDEV = 8
EPS = 1e-6
GRID_W = 64
POOL_WINDOWS = (2, 4, 8, 16)
SSM_H = 16
CHUNK_T = 16
LANES = 128
ADAM_LR, ADAM_B1, ADAM_B2, ADAM_EPS, ADAM_WD, ADAM_STEP = 0.001, 0.9, 0.999, 1e-08, 0.01, 10
VMEM_BIG = 56 << 20
MESH_ID = pl.DeviceIdType.MESH

_HBM = pl.BlockSpec(memory_space=pltpu.HBM)
_ANY = pl.BlockSpec(memory_space=pl.ANY)
_SMEM = pl.BlockSpec(memory_space=pltpu.SMEM)


def _sds(shape, dtype=F32):
    return jax.ShapeDtypeStruct(tuple(shape), dtype)


def _cparams(ngrid=0, vmem=None):
    return pltpu.CompilerParams(
        dimension_semantics=("arbitrary",) * ngrid if ngrid else None, vmem_limit_bytes=vmem
    )


def _rows(tm, c, col=0):
    return pl.BlockSpec((tm, c), lambda i: (i, col))


def _whole(shape):
    nd = len(shape)
    return pl.BlockSpec(tuple(shape), lambda *_: (0,) * nd, pipeline_mode=pl.Buffered(1))


def _acc(shape):
    nd = len(shape)
    return pl.BlockSpec(tuple(shape), lambda *_: (0,) * nd)


def _mm(a, b):
    return jnp.dot(a, b, preferred_element_type=F32)


def _mm_nt(a, b):
    return lax.dot_general(a, b, (((1,), (1,)), ((), ())), preferred_element_type=F32)


def _mm_tn(a, b):
    return lax.dot_general(a, b, (((0,), (0,)), ((), ())), preferred_element_type=F32)


def _mm_split(k01, s):
    hi = s.astype(BF16)
    lo = (s - hi.astype(F32)).astype(BF16)
    return _mm(k01, hi) + _mm(k01, lo)


def _sigmoid(v):
    return 0.5 * (jnp.tanh(0.5 * v) + 1.0)


def _silu(v):
    return v * _sigmoid(v)


_GELU_K = 0.7978845608028654
_GELU_C = 0.044715


def _gelu(v):
    return 0.5 * v * (1.0 + jnp.tanh(_GELU_K * (v + _GELU_C * v * v * v)))


def _gelu_grad(v):
    th = jnp.tanh(_GELU_K * (v + _GELU_C * v * v * v))
    return 0.5 * (1.0 + th) + 0.5 * v * (1.0 - th * th) * (_GELU_K * (1.0 + 3.0 * _GELU_C * v * v))


def _colsum(v):
    return jnp.sum(v, axis=0, keepdims=True)


def _rowmean(v):
    return jnp.mean(v, axis=-1, keepdims=True)


NCHIP = NDEV // 2


def _my_pos():
    return lax.axis_index("x"), lax.axis_index("y"), lax.axis_index("c")


def _other_chips():
    x, y, _ = _my_pos()
    return [(1 - x, y), (x, 1 - y), (1 - x, 1 - y)]


def _remote(src, dst, send, recv, to):
    return pltpu.make_async_remote_copy(src, dst, send, recv, device_id=to, device_id_type=MESH_ID)


def _all_gather(arrs, name):
    n = len(arrs)

    def body(*refs):
        start, finish = _gather_copies(refs[:n], refs[n : 2 * n], *refs[2 * n :])
        start()
        finish()

    return pl.pallas_call(
        body, name=name, out_shape=_gather_shapes(arrs), in_specs=[_HBM] * n, out_specs=[_HBM] * n, scratch_shapes=_gather_sems(n)
    )(*arrs)


def _gather_shapes(arrs):
    return [_sds((NDEV,) + a.shape, a.dtype) for a in arrs]


def _gather_sems(n):
    return [pltpu.SemaphoreType.DMA((n, NDEV - 1)), pltpu.SemaphoreType.DMA((n, NDEV - 1)), pltpu.SemaphoreType.DMA((n,))]


def _gather_copies(ins, outs, send, recv, loc):
    n = len(ins)
    x, y, c = _my_pos()
    me, sib = (x, y, c), (x, y, 1 - c)
    chips = _other_chips()

    def slot(a, p):
        return outs[a].at[4 * p[0] + 2 * p[1] + p[2]]

    def copy(a, k, block, to, own=False):
        return _remote(ins[a] if own else slot(a, block), slot(a, block), send.at[a, k], recv.at[a, k], to)

    def mine():
        local = [pltpu.make_async_copy(ins[a], slot(a, me), loc.at[a]) for a in range(n)]
        sent = []
        for a in range(n):
            sent.append(copy(a, 0, me, sib, own=True))
            sent += [copy(a, 1 + j, me, (*chip, c), own=True) for j, chip in enumerate(chips)]
        return local, sent

    def start():
        local, sent = mine()
        for cp in local + sent:
            cp.start()

    def finish():
        local, sent = mine()
        for j, chip in enumerate(chips):
            for a in range(n):
                copy(a, 1 + j, (*chip, c), me).wait_recv()
                fwd = copy(a, 4 + j, (*chip, c), sib)
                fwd.start()
                sent.append(fwd)
        for a in range(n):
            copy(a, 0, sib, me).wait_recv()
            for j, chip in enumerate(chips):
                copy(a, 4 + j, (*chip, 1 - c), me).wait_recv()
        for cp in sent:
            cp.wait_send()
        for cp in local:
            cp.wait()

    return start, finish


def _sibling_swap(arrs, name):
    n = len(arrs)

    def body(*refs):
        start, finish = _swap_copies(refs[:n], refs[n : 2 * n], *refs[2 * n :])
        start()
        finish()

    return pl.pallas_call(
        body, name=name, out_shape=_swap_shapes(arrs), in_specs=[_HBM] * n, out_specs=[_HBM] * n, scratch_shapes=_swap_sems(n)
    )(*arrs)


def _swap_shapes(arrs):
    return [_sds((a.shape[0],) + a.shape[2:], a.dtype) for a in arrs]


def _swap_sems(n):
    return [pltpu.SemaphoreType.DMA((n,)), pltpu.SemaphoreType.DMA((n,))]


def _swap_copies(ins, outs, send, recv):
    x, y, c = _my_pos()

    def copies():
        return [_remote(ins[a].at[:, 1 - c], outs[a], send.at[a], recv.at[a], (x, y, 1 - c)) for a in range(len(ins))]

    def start():
        for cp in copies():
            cp.start()

    def finish():
        for cp in copies():
            cp.wait()

    return start, finish


def _pair_sum(arr, got, wire, name):
    _, _, r, c = arr.shape
    tr = _row_tile(r, c, budget=1 << 20)

    def body(a_ref, g_ref, o_ref):
        o_ref[0] = (a_ref[0, lax.axis_index("c")] + g_ref[0]).astype(wire)

    return pl.pallas_call(
        body,
        name=name,
        grid=(NCHIP, r // tr),
        in_specs=[pl.BlockSpec((1, 2, tr, c), lambda q, i: (q, 0, i, 0)), pl.BlockSpec((1, tr, c), lambda q, i: (q, i, 0))],
        out_specs=pl.BlockSpec((1, tr, c), lambda q, i: (q, i, 0)),
        out_shape=_sds((NCHIP, r, c), wire),
        compiler_params=_cparams(2),
    )(arr, got)


def _chip_exchange(arrs, name):
    n = len(arrs)

    def body(*refs):
        start, finish = _exchange_copies(refs[:n], refs[n : 2 * n], *refs[2 * n :])
        start()
        finish()

    return pl.pallas_call(
        body, name=name, out_shape=[_sds(a.shape, a.dtype) for a in arrs], in_specs=[_HBM] * n, out_specs=[_HBM] * n, scratch_shapes=_exchange_sems(n)
    )(*arrs)


def _exchange_sems(n):
    return [pltpu.SemaphoreType.DMA((n, NCHIP - 1)), pltpu.SemaphoreType.DMA((n, NCHIP - 1)), pltpu.SemaphoreType.DMA((n,))]


def _exchange_copies(ins, outs, send, recv, loc):
    n = len(ins)
    x, y, c = _my_pos()
    mine = 2 * x + y
    chips = _other_chips()

    def copies():
        local = [pltpu.make_async_copy(ins[a].at[mine], outs[a].at[mine], loc.at[a]) for a in range(n)]
        sent = [
            _remote(ins[a].at[2 * px + py], outs[a].at[mine], send.at[a, j], recv.at[a, j], (px, py, c))
            for a in range(n)
            for j, (px, py) in enumerate(chips)
        ]
        return local, sent

    def start():
        local, sent = copies()
        for cp in local + sent:
            cp.start()

    def finish():
        local, sent = copies()
        for a in range(n):
            for j, (px, py) in enumerate(chips):
                _remote(ins[a].at[mine], outs[a].at[2 * px + py], send.at[a, j], recv.at[a, j], (px, py, c)).wait_recv()
        for cp in sent:
            cp.wait_send()
        for cp in local:
            cp.wait()

    return start, finish


def _by_chip_and_core(a):
    return a.reshape((NCHIP, 2) + a.shape[1:])


def _pair_partials(arrs, wires, tag, got=None):
    four = [_by_chip_and_core(a) for a in arrs]
    if got is None:
        got = _sibling_swap(four, "swap_" + tag)
    return [_pair_sum(a, g, w, f"pair_sum_{tag}{i}") for i, (a, g, w) in enumerate(zip(four, got, wires))]


def _reduce_scatter(arrs, wires, tag):
    return _chip_exchange(_pair_partials(arrs, wires, tag), "exchange_" + tag)


def _row_tile(r, c, budget=1 << 20):
    best = r
    for t in range(8, r, 8):
        if r % t == 0 and t * c * 4 <= budget:
            best = t
    if r * c * 4 <= budget:
        best = r
    return best


def _sum_chips(land):
    tot = land[0].astype(F32)
    for q in range(1, NCHIP):
        tot = tot + land[q].astype(F32)
    return tot


def _adam_math(w, g, m, v):
    m2 = ADAM_B1 * m + (1.0 - ADAM_B1) * g
    v2 = ADAM_B2 * v + (1.0 - ADAM_B2) * (g * g)
    mh = m2 / (1.0 - ADAM_B1**ADAM_STEP)
    vh = v2 / (1.0 - ADAM_B2**ADAM_STEP)
    delta = -ADAM_LR * (mh / (jnp.sqrt(vh) + ADAM_EPS) + ADAM_WD * w)
    return delta, m2, v2


def _adam(g, w, m, v, name, land=None):
    r, c = w.shape
    tr = _row_tile(r, c, budget=1 << 20)

    def body(*refs):
        if land is not None:
            l_ref, w_ref, m_ref, v_ref, g_ref, d_ref, m2_ref, v2_ref = refs
            gv = _sum_chips(l_ref)
            g_ref[...] = gv
        else:
            g_in, w_ref, m_ref, v_ref, d_ref, m2_ref, v2_ref = refs
            gv = g_in[...]
        d, m2, v2 = _adam_math(w_ref[...], gv, m_ref[...], v_ref[...])
        d_ref[...] = d
        m2_ref[...] = m2
        v2_ref[...] = v2

    blk = _rows(tr, c)
    if land is not None:
        in_specs = [pl.BlockSpec((NCHIP, tr, c), lambda i: (0, i, 0)), blk, blk, blk]
        out = pl.pallas_call(
            body, name=name, grid=(r // tr,), in_specs=in_specs, out_specs=[blk] * 4, out_shape=[_sds((r, c))] * 4, compiler_params=_cparams(1)
        )(land, w, m, v)
        return out
    out = pl.pallas_call(
        body, name=name, grid=(r // tr,), in_specs=[blk] * 4, out_specs=[blk] * 3, out_shape=[_sds((r, c))] * 3, compiler_params=_cparams(1)
    )(g, w, m, v)
    return (g,) + tuple(out)


def _sum_slots(land, name):
    _, r, c = land.shape
    tr = _row_tile(r, c, budget=1 << 20)

    def body(l_ref, o_ref):
        o_ref[...] = _sum_chips(l_ref)

    return pl.pallas_call(
        body,
        name=name,
        grid=(r // tr,),
        in_specs=[pl.BlockSpec((NCHIP, tr, c), lambda i: (0, i, 0))],
        out_specs=_rows(tr, c),
        out_shape=_sds((r, c)),
        compiler_params=_cparams(1),
    )(land)


def _mod_fwd(call, ada_w, ada_b_loc):
    def body(c_ref, w_ref, b_ref, o_ref):
        s = _silu(c_ref[...]).astype(BF16)
        o_ref[...] = _mm(s, w_ref[...].astype(BF16)) + b_ref[...]

    return pl.pallas_call(
        body, name="mod_fwd", out_shape=_sds((call.shape[0], ada_w.shape[1])), compiler_params=_cparams(0, 32 << 20)
    )(call, ada_w, ada_b_loc)


def _mod_bwd(call, dm_loc, dmc_loc, ada_w):
    d, n = ada_w.shape
    pad = call.shape[0] - NDEV - 1

    def body(c_ref, dm_ref, dmc_ref, w_ref, gw_ref, gb_ref, cp_ref):
        cv = c_ref[...]
        sg = _sigmoid(cv)
        dmc = _colsum(dmc_ref[...])
        dm = dm_ref[...]
        rows = jnp.concatenate([dm, dmc, jnp.zeros((pad, n), F32)], axis=0)
        gw_ref[...] = _mm_tn((cv * sg).astype(BF16), rows.astype(BF16))
        gb_ref[...] = _colsum(dm) + dmc
        back = _mm_nt(rows[NDEV:].astype(BF16), w_ref[...].astype(BF16))
        c8, s8 = cv[NDEV : NDEV + 1], sg[NDEV : NDEV + 1]
        cp_ref[...] = back[0:1] * (s8 * (1.0 + c8 * (1.0 - s8)))

    return pl.pallas_call(
        body, name="mod_bwd", out_shape=[_sds((d, n)), _sds((1, n)), _sds((1, d))], compiler_params=_cparams(0, 40 << 20)
    )(call, dm_loc, dmc_loc, ada_w)


def _riding(body, nin, nout, nr, grid, copies):
    def wrapped(*refs):
        start, finish = copies(refs[nin : nin + nr], refs[nin + nr + nout : nin + 2 * nr + nout], *refs[nin + 2 * nr + nout :])
        first, last = True, True
        for ax, size in enumerate(grid):
            first = jnp.logical_and(first, pl.program_id(ax) == 0)
            last = jnp.logical_and(last, pl.program_id(ax) == size - 1)
        pl.when(first)(start)
        body(*refs[:nin], *refs[nin + nr : nin + nr + nout])
        pl.when(last)(finish)

    return wrapped if nr else body


def _in_proj(xr, norm_g, scale, shift, wg, jsel, tm, name, shards=()):
    lx, d = xr.shape
    cb = wg.shape[2]
    nr = len(shards)

    def body(x_ref, g_ref, sc_ref, sh_ref, w_ref, p_ref, h_ref):
        xv = x_ref[...]
        r = lax.rsqrt(_rowmean(xv * xv) + EPS)
        hb = ((xv * r) * g_ref[...] * (1.0 + sc_ref[...]) + sh_ref[...]).astype(BF16)
        h_ref[...] = hb
        for q, j in enumerate(jsel):
            p_ref[:, q * cb : (q + 1) * cb] = _mm(hb, w_ref[j])

    vec = _whole((1, d))
    out = pl.pallas_call(
        _riding(body, 5, 2, nr, (lx // tm,), _gather_copies),
        name=name,
        grid=(lx // tm,),
        in_specs=[_rows(tm, d), vec, vec, vec, _whole(wg.shape)] + [_HBM] * nr,
        out_specs=[_rows(tm, len(jsel) * cb), _rows(tm, d)] + [_HBM] * nr,
        out_shape=[_sds((lx, len(jsel) * cb)), _sds((lx, d), BF16)] + _gather_shapes(shards),
        scratch_shapes=_gather_sems(nr) if nr else [],
        compiler_params=_cparams(1, VMEM_BIG),
    )(xr, norm_g, scale, shift, wg, *shards)
    return out[:2], out[2:]


def _in_bwd(xr, dxo, dup, du5, dypre, dz, wg, norm_g, scale, shift, d_skip, dg_init, tm):
    lx, d = xr.shape
    cb = wg.shape[2]
    pw = dup.shape[1]
    sw = du5.shape[1]
    mix = dz.shape[1]
    ncol = NDEV * cb

    def body(x_ref, dxo_ref, up_ref, d5_ref, dy_ref, dz_ref, w_ref, g_ref, sc_ref, sh_ref, dk_ref, gi_ref,
             gx_ref, dp_ref, dsc_ref, dsh_ref, dg_ref):
        i = pl.program_id(0)

        @pl.when(i == 0)
        def _():
            dsc_ref[...] = jnp.zeros_like(dsc_ref)
            dsh_ref[...] = jnp.zeros_like(dsh_ref)
            dg_ref[...] = gi_ref[...]

        dp = jnp.concatenate(
            [up_ref[...], d5_ref[...] + dk_ref[...] * dy_ref[...], dz_ref[...]], axis=1
        ).astype(BF16)
        dp_ref[...] = dp
        dh = _mm_nt(dp[:, 0:cb], w_ref[0])
        for j in range(1, NDEV):
            dh = dh + _mm_nt(dp[:, j * cb : (j + 1) * cb], w_ref[j])
        xv = x_ref[...]
        r = lax.rsqrt(_rowmean(xv * xv) + EPS)
        xh = xv * r
        g = g_ref[...]
        one_sc = 1.0 + sc_ref[...]
        dsh_ref[...] += _colsum(dh)
        dsc_ref[...] += _colsum(dh * (xh * g))
        dg_ref[...] += _colsum(dh * one_sc * xh)
        dxh = dh * one_sc * g
        gx_ref[...] = r * (dxh - xh * _rowmean(dxh * xh)) + dxo_ref[...]

    vec = _whole((1, d))
    return pl.pallas_call(
        body,
        name="in_bwd",
        grid=(lx // tm,),
        in_specs=[_rows(tm, d), _rows(tm, d), _rows(tm, pw), _rows(tm, sw), _rows(tm, sw), _rows(tm, mix), _whole(wg.shape), vec, vec, vec, _whole((1, sw)), vec],
        out_specs=[_rows(tm, d), _rows(tm, ncol), _acc((1, d)), _acc((1, d)), _acc((1, d))],
        out_shape=[_sds((lx, d)), _sds((lx, ncol), BF16), _sds((1, d)), _sds((1, d)), _sds((1, d))],
        compiler_params=_cparams(1, VMEM_BIG),
    )(xr, dxo, dup, du5, dypre, dz, wg, norm_g, scale, shift, d_skip, dg_init)


def _in_bwd_ctx(xc, duc, wg, jsel, norm_g, scale, shift):
    lc, d = xc.shape
    cb = wg.shape[2]

    def body(x_ref, du_ref, w_ref, g_ref, sc_ref, sh_ref, dp_ref, dsc_ref, dsh_ref, dg_ref):
        dp = du_ref[...].astype(BF16)
        dp_ref[...] = dp
        dh = _mm_nt(dp[:, 0:cb], w_ref[jsel[0]])
        for q in range(1, len(jsel)):
            dh = dh + _mm_nt(dp[:, q * cb : (q + 1) * cb], w_ref[jsel[q]])
        xv = x_ref[...]
        xh = xv * lax.rsqrt(_rowmean(xv * xv) + EPS)
        dsh_ref[...] = _colsum(dh)
        dsc_ref[...] = _colsum(dh * (xh * g_ref[...]))
        dg_ref[...] = _colsum(dh * (1.0 + sc_ref[...]) * xh)

    return pl.pallas_call(
        body,
        name="in_bwd_ctx",
        out_shape=[_sds(duc.shape, BF16), _sds((1, d)), _sds((1, d)), _sds((1, d))],
        compiler_params=_cparams(0, VMEM_BIG),
    )(xc, duc, wg, norm_g, scale, shift)


def _in_w_grad(hb, dpb, hcb, dpcb, cb, tm, parts=()):
    lx, d = hb.shape
    lc = hcb.shape[0]
    nb = lx // tm
    cg = 2 * cb
    nr = len(parts)
    grid = (NDEV // 2, nb + 1)

    def body(h_ref, dp_ref, hc_ref, dpc_ref, o_ref):
        j, i = pl.program_id(0), pl.program_id(1)

        @pl.when(i == 0)
        def _():
            o_ref[...] = jnp.zeros_like(o_ref)

        @pl.when(i < nb)
        def _():
            h = h_ref[...]
            o_ref[0] += _mm_tn(h, dp_ref[:, 0:cb])
            o_ref[1] += _mm_tn(h, dp_ref[:, cb:cg])

        @pl.when(jnp.logical_and(i == nb, j == 1))
        def _():
            h = hc_ref[...]
            o_ref[0] += _mm_tn(h, dpc_ref[:, 0:cb])
            o_ref[1] += _mm_tn(h, dpc_ref[:, cb:cg])

    out = pl.pallas_call(
        _riding(body, 4, 1, nr, grid, _exchange_copies),
        name="in_w_grad",
        grid=grid,
        in_specs=[
            pl.BlockSpec((tm, d), lambda j, i: (jnp.minimum(i, nb - 1), 0)),
            pl.BlockSpec((tm, cg), lambda j, i: (jnp.minimum(i, nb - 1), j)),
            pl.BlockSpec((lc, d), lambda j, i: (0, 0)),
            pl.BlockSpec((lc, cg), lambda j, i: (0, 0)),
        ]
        + [_HBM] * nr,
        out_specs=[pl.BlockSpec((2, d, cb), lambda j, i: (j, 0, 0))] + [_HBM] * nr,
        out_shape=[_sds((NDEV, d, cb))] + [_sds(p.shape, p.dtype) for p in parts],
        scratch_shapes=_exchange_sems(nr) if nr else [],
        compiler_params=_cparams(2, VMEM_BIG),
    )(hb, dpb, hcb, dpcb, *parts)
    return out[0], out[1:]


def _pool_tables(w, rows, rb, pgw, transpose):
    t = np.arange(GRID_W)
    lo, hi = np.clip(t - w // 2, 0, GRID_W), np.clip(t + w - w // 2, 0, GRID_W)
    band = ((t[None, :] >= lo[:, None]) & (t[None, :] < hi[:, None])).astype(np.float32)
    if transpose:
        band = band.T
    kc = np.kron(np.eye(rb, dtype=np.float32), band)
    inv_c = np.tile((1.0 / (hi - lo).astype(np.float32))[:, None], (rb, pgw)).astype(np.float32)
    r = np.arange(rows)
    cnt_r = np.clip(r + w - w // 2, 0, rows) - np.clip(r - w // 2, 0, rows)
    inv_r = (1.0 / cnt_r.astype(np.float32)).astype(np.float32)
    return jnp.asarray(kc, BF16), jnp.asarray(inv_r), jnp.asarray(inv_c)


def _pool_stack(rows, rb, pgw, transpose):
    tabs = [_pool_tables(w, rows, rb, pgw, transpose) for w in POOL_WINDOWS]
    return tuple(jnp.stack([t[i] for t in tabs]) for i in range(3))


def _per_window(group):
    for k, w in enumerate(POOL_WINDOWS):
        pl.when(pl.program_id(0) == k)(lambda k=k, w=w: group(k, w))


def _pool_fwd(proj, pool_wf):
    lx = proj.shape[0]
    ng, pgw, _ = pool_wf.shape
    rows = lx // GRID_W
    rb = min(4, rows)
    tok = rb * GRID_W
    kc, inv_r, inv_c = _pool_stack(rows, rb, pgw, False)

    def body(p_ref, w_ref, kc_ref, ir_ref, ic_ref, lin_ref, dm_ref, xp):
        def group(kk, w):
            lo = w // 2
            xp[pl.ds(0, lo * GRID_W), :] = jnp.zeros((lo * GRID_W, pgw), F32)
            xp[pl.ds((lo + rows) * GRID_W, (w - lo) * GRID_W), :] = jnp.zeros(((w - lo) * GRID_W, pgw), F32)
            xp[pl.ds(lo * GRID_W, lx), :] = p_ref[...]

            def blk(b, carry):
                r0 = b * rb
                parts = []
                for rr in range(rb):
                    base = pl.multiple_of((r0 + rr) * GRID_W, GRID_W)
                    s = xp[pl.ds(base, GRID_W), :]
                    for k in range(1, w):
                        s = s + xp[pl.ds(base + k * GRID_W, GRID_W), :]
                    parts.append(s * ir_ref[kk, r0 + rr])
                m = _mm_split(kc_ref[0], jnp.concatenate(parts, axis=0)) * ic_ref[0]
                u = xp[pl.ds(pl.multiple_of((r0 + lo) * GRID_W, GRID_W), tok), :]
                db = (m - u).astype(BF16)
                o0 = pl.multiple_of(r0 * GRID_W, GRID_W)
                dm_ref[pl.ds(o0, tok), :] = db
                lin_ref[pl.ds(o0, tok), :] = _mm(db, w_ref[0])
                return carry

            lax.fori_loop(0, rows // rb, blk, 0)

        _per_window(group)

    col = pl.BlockSpec((lx, pgw), lambda g: (0, g))
    return pl.pallas_call(
        body,
        name="pool_fwd",
        grid=(ng,),
        in_specs=[col, _gspec(1, pgw, pgw), _gspec(1, tok, tok), _SMEM, _gspec(1, tok, pgw)],
        out_specs=[col, col],
        out_shape=[_sds((lx, ng * pgw)), _sds((lx, ng * pgw), BF16)],
        scratch_shapes=[pltpu.VMEM(((rows + max(POOL_WINDOWS)) * GRID_W, pgw), F32)],
        compiler_params=_cparams(1, VMEM_BIG),
    )(proj, pool_wf, kc, inv_r, inv_c)


def _pool_bwd(dlin, pool_wf, dmat):
    lx = dlin.shape[0]
    ng, pgw, _ = pool_wf.shape
    rows = lx // GRID_W
    rb = min(4, rows)
    tok = rb * GRID_W
    kct, inv_r, inv_c = _pool_stack(rows, rb, pgw, True)

    def body(dl_ref, w_ref, kc_ref, ir_ref, ic_ref, dm_ref, du_ref, dw_ref, tp):
        def group(kk, w):
            front = w - w // 2 - 1
            if front:
                tp[pl.ds(0, front * GRID_W), :] = jnp.zeros((front * GRID_W, pgw), F32)
            tp[pl.ds((front + rows) * GRID_W, (w - front) * GRID_W), :] = jnp.zeros(((w - front) * GRID_W, pgw), F32)
            dw_ref[...] = jnp.zeros_like(dw_ref)

            def blk(b, carry):
                o0 = pl.multiple_of(b * tok, GRID_W)
                dlb = dl_ref[pl.ds(o0, tok), :].astype(BF16)
                dw_ref[0] += _mm_tn(dm_ref[pl.ds(o0, tok), :], dlb)
                dd = _mm_nt(dlb, w_ref[0])
                du_ref[pl.ds(o0, tok), :] = -dd
                t = _mm(kc_ref[0], (dd * ic_ref[0]).astype(BF16))
                for rr in range(rb):
                    dst = pl.multiple_of((b * rb + rr + front) * GRID_W, GRID_W)
                    tp[pl.ds(dst, GRID_W), :] = t[rr * GRID_W : (rr + 1) * GRID_W] * ir_ref[kk, b * rb + rr]
                return carry

            lax.fori_loop(0, rows // rb, blk, 0)

            def rowl(r, carry):
                base = pl.multiple_of(r * GRID_W, GRID_W)
                s = tp[pl.ds(base, GRID_W), :]
                for k in range(1, w):
                    s = s + tp[pl.ds(base + k * GRID_W, GRID_W), :]
                du_ref[pl.ds(base, GRID_W), :] += s
                return carry

            lax.fori_loop(0, rows, rowl, 0)

        _per_window(group)

    col = pl.BlockSpec((lx, pgw), lambda g: (0, g))
    return pl.pallas_call(
        body,
        name="pool_bwd",
        grid=(ng,),
        in_specs=[col, _gspec(1, pgw, pgw), _gspec(1, tok, tok), _SMEM, _gspec(1, tok, pgw), col],
        out_specs=[col, _gspec(1, pgw, pgw)],
        out_shape=[_sds((lx, ng * pgw)), _sds((ng, pgw, pgw))],
        scratch_shapes=[pltpu.VMEM(((rows + max(POOL_WINDOWS)) * GRID_W, pgw), F32)],
        compiler_params=_cparams(1, VMEM_BIG),
    )(dlin, pool_wf, kct, inv_r, inv_c, dmat)


def _s5_small(lam_re, lam_im, log_dt, b_re, b_im, c_re, c_im):
    t = CHUNK_T
    dt = jnp.exp(log_dt)[..., None]
    zr, zi = lam_re * dt, lam_im * dt
    tau = jnp.arange(t + 1, dtype=F32)
    mag = jnp.exp(zr[..., None] * tau)
    pr, pi = mag * jnp.cos(zi[..., None] * tau), mag * jnp.sin(zi[..., None] * tau)
    ar, ai = pr[..., 1], pi[..., 1]
    den = lam_re * lam_re + lam_im * lam_im
    qr = ((ar - 1.0) * lam_re + ai * lam_im) / den
    qi = (ai * lam_re - (ar - 1.0) * lam_im) / den
    bbr = qr[..., None] * b_re - qi[..., None] * b_im
    bbi = qr[..., None] * b_im + qi[..., None] * b_re
    ctr, cti = jnp.swapaxes(c_re, 2, 3), jnp.swapaxes(c_im, 2, 3)
    lane_pad = lambda v: jnp.pad(jnp.swapaxes(v, 0, 1), ((0, 0), (0, 0), (0, 0), (0, LANES - v.shape[-1])))
    tp = lambda v: jnp.swapaxes(v, 1, 2)
    flip = lambda v: v[..., ::-1]
    pfr, pfi = tp(flip(pr[0, ..., :t])), tp(flip(pi[0, ..., :t]))
    pbr, pbi = tp(pr[1, ..., :t]), tp(pi[1, ..., :t])
    bfr, bfi, bbr_t, bbi_t = tp(bbr[0]), tp(bbi[0]), tp(bbr[1]), tp(bbi[1])
    a1 = jnp.concatenate([pfr, pbr, pfr, pbr], axis=-1)
    a2 = jnp.concatenate([-pfi, -pbi, pfi, pbi], axis=-1)
    b1 = jnp.concatenate([bfr, bbr_t, bfi, bbi_t], axis=-1)
    b2 = jnp.concatenate([bfi, bbi_t, bfr, bbr_t], axis=-1)
    a16 = jnp.concatenate([pr[0, ..., t], pr[1, ..., t], pi[0, ..., t], pi[1, ..., t]], axis=1)
    return (lane_pad(pr), lane_pad(pi), lane_pad(bbr), lane_pad(bbi), lane_pad(ctr), lane_pad(cti), a1, a2, b1, b2), a16


def _split2(v):
    hi = v.astype(BF16)
    return hi, (v - hi.astype(F32)).astype(BF16)


def _dotx(a, b, mm=_mm):
    ah, al = (a, None) if a.dtype == BF16 else _split2(a)
    bh, bl = (b, None) if b.dtype == BF16 else _split2(b)
    out = mm(ah, bh)
    if bl is not None:
        out = out + mm(ah, bl)
    if al is not None:
        out = out + mm(al, bh)
    return out


def _shift_lanes(v, k):
    if k == 0:
        return v
    n = v.shape[-1]
    lane = lax.broadcasted_iota(jnp.int32, v.shape, v.ndim - 1)
    r = pltpu.roll(v, k % n, axis=v.ndim - 1)
    return jnp.where(lane >= k, r, 0.0) if k > 0 else jnp.where(lane < n + k, r, 0.0)


def _op_consts():
    t, h = CHUNK_T, SSM_H
    j, o = np.divmod(np.arange(t * h), h)
    p = np.arange(LANES)[:, None]
    sel = lambda col: (p == col[None, :]).astype(np.float32)
    pw = np.stack([np.concatenate([sel(j), sel(j + 1)], axis=1), np.concatenate([sel(t - 1 - j), sel(t - j)], axis=1)])
    return jnp.asarray(pw, BF16), jnp.asarray(sel(o), BF16), jnp.asarray(sel(j).T, BF16), jnp.asarray(sel(o).T, BF16)


def _op_factors(gi, pr_ref, pi_ref, br_ref, bi_ref, cr_ref, ci_ref, pw_ref, ch_ref):
    n = pr_ref.shape[2]
    c16 = _dotx(jnp.concatenate([cr_ref[gi, 0], ci_ref[gi, 0], cr_ref[gi, 1], ci_ref[gi, 1]], axis=0), ch_ref[...])
    out = []
    for d in range(2):
        e = _dotx(jnp.concatenate([pr_ref[gi, d], pi_ref[gi, d]], axis=0), pw_ref[d])
        c = c16[2 * d * n : (2 * d + 1) * n], c16[(2 * d + 1) * n : (2 * d + 2) * n]
        bst = jnp.concatenate([br_ref[gi, d], -bi_ref[gi, d]], axis=0)
        out.append(((e[:n, :CW], e[n:, :CW]), (e[:n, CW:], e[n:, CW:]), c, bst))
    return out


def _op_spread(m_ref, u_ref, v_ref, gi):
    return _dotx(m_ref[...], jnp.concatenate([_rows_pad(u_ref[gi]), _rows_pad(v_ref[gi])], axis=1))


def _cmul(x, y):
    return x[0] * y[0] - x[1] * y[1], x[0] * y[1] + x[1] * y[0]


def _cmul_conj(g, y):
    return g[0] * y[0] + g[1] * y[1], g[1] * y[0] - g[0] * y[1]


def _rows_pad(v):
    return jnp.concatenate([v, jnp.zeros((LANES - v.shape[0], v.shape[1]), F32)], axis=0)


def _s5_build(small, gb, shards):
    pr, pi, br, bi, cr, ci, a1, a2, b1, b2 = small
    g, _, n, _ = pr.shape
    t, h, ns = CHUNK_T, SSM_H, 4 * n
    pw, ch, rep_s, til_s = _op_consts()
    nr, nin, steps = len(shards), 14, g // gb

    def body(*refs):
        pr_ref, pi_ref, br_ref, bi_ref, cr_ref, ci_ref, a1_ref, a2_ref, b1_ref, b2_ref, pw_ref, ch_ref, rs_ref, ts_ref = refs[:nin]
        t_ref, w_ref, c_ref = refs[nin + nr : nin + nr + 3]
        start, finish = _gather_copies(refs[nin : nin + nr], refs[nin + nr + 3 : nin + 2 * nr + 3], *refs[nin + 2 * nr + 3 :])
        pl.when(pl.program_id(0) == 0)(start)

        def group(gi, carry):
            rows, outs = [], []
            for e16, e1, c16, bst in _op_factors(gi, pr_ref, pi_ref, br_ref, bi_ref, cr_ref, ci_ref, pw_ref, ch_ref):
                rows.append(_dotx(bst, jnp.concatenate(_cmul(c16, e16), axis=0), _mm_tn)[:h])
                outs.append(_cmul(c16, e1))
            c_ref[gi] = jnp.concatenate([outs[0][0], outs[1][0], -outs[0][1], -outs[1][1]], axis=0).astype(BF16)
            for s in range(t):
                t_ref[gi, s * h : (s + 1) * h, :] = (_shift_lanes(rows[0], h * s) + _shift_lanes(rows[1], -h * (t - 1 - s))).astype(BF16)
            ae, be = _op_spread(rs_ref, a1_ref, a2_ref, gi), _op_spread(ts_ref, b1_ref, b2_ref, gi)
            w_ref[gi] = (ae[:, :ns] * be[:, :ns] + ae[:, ns:] * be[:, ns:]).astype(BF16)
            return carry

        lax.fori_loop(0, gb, group, 0)
        pl.when(pl.program_id(0) == steps - 1)(finish)

    sm = pl.BlockSpec((gb, 2, n, LANES), lambda i: (i, 0, 0, 0))
    ab = lambda v: pl.BlockSpec((gb,) + v.shape[1:], lambda i: (i, 0, 0))
    out = pl.pallas_call(
        body,
        name="s5_build",
        grid=(steps,),
        in_specs=[sm] * 6 + [ab(a1), ab(a2), ab(b1), ab(b2), _whole(pw.shape), _whole(ch.shape), _whole(rep_s.shape), _whole(til_s.shape)] + [_HBM] * nr,
        out_specs=[_gspec(gb, CW, CW), _gspec(gb, CW, 4 * n), _gspec(gb, 4 * n, CW)] + [_HBM] * nr,
        out_shape=[_sds((g, CW, CW), BF16), _sds((g, CW, 4 * n), BF16), _sds((g, 4 * n, CW), BF16)] + _gather_shapes(shards),
        scratch_shapes=_gather_sems(nr),
        compiler_params=_cparams(1, VMEM_BIG),
    )(pr, pi, br, bi, cr, ci, a1, a2, b1, b2, pw, ch, rep_s, til_s, *shards)
    return out[:3], out[3:]


def _s5_build_bwd(small, dtsum, dwst, dccat, gb, parts):
    pr, pi, br, bi, cr, ci, a1, a2, b1, b2 = small
    g, _, n, _ = pr.shape
    t, h, ns = CHUNK_T, SSM_H, 4 * n
    pw, ch, rep_s, til_s = _op_consts()
    nr, nin, nout, steps = len(parts), 17, 10, g // gb

    def body(*refs):
        pr_ref, pi_ref, br_ref, bi_ref, cr_ref, ci_ref, a1_ref, a2_ref, b1_ref, b2_ref, pw_ref, ch_ref, rs_ref, ts_ref, dt_ref, dw_ref, dc_ref = refs[:nin]
        dpr_ref, dpi_ref, dbr_ref, dbi_ref, dcr_ref, dci_ref, da1_ref, da2_ref, db1_ref, db2_ref = refs[nin + nr : nin + nr + nout]
        start, finish = _exchange_copies(refs[nin : nin + nr], refs[nin + nr + nout : nin + 2 * nr + nout], *refs[nin + 2 * nr + nout :])
        pl.when(pl.program_id(0) == 0)(start)

        def group(gi, carry):
            drow = [jnp.zeros((h, CW), F32), jnp.zeros((h, CW), F32)]
            for s in range(t):
                blk = dt_ref[gi, s * h : (s + 1) * h, :]
                drow[0] = drow[0] + _shift_lanes(blk, -h * s)
                drow[1] = drow[1] + _shift_lanes(blk, h * (t - 1 - s))
            dcc = dc_ref[gi]
            dcs = []
            for d, (e16, e1, c16, bst) in enumerate(_op_factors(gi, pr_ref, pi_ref, br_ref, bi_ref, cr_ref, ci_ref, pw_ref, ch_ref)):
                dr = _rows_pad(drow[d])
                db = _dotx(jnp.concatenate(_cmul(c16, e16), axis=0), dr, _mm_nt)
                dbr_ref[gi, d] = db[:n]
                dbi_ref[gi, d] = -db[n:]
                dst = _dotx(bst, dr)
                dca = dst[:n], dst[n:]
                gx = dcc[d * n : (d + 1) * n], -dcc[(2 + d) * n : (3 + d) * n]
                dc16, dc16b = _cmul_conj(dca, e16), _cmul_conj(gx, e1)
                dcs += [dc16[0] + dc16b[0], dc16[1] + dc16b[1]]
                de = jnp.concatenate([jnp.concatenate(_cmul_conj(dca, c16), axis=0), jnp.concatenate(_cmul_conj(gx, c16), axis=0)], axis=1)
                dp = _dotx(de, pw_ref[d], _mm_nt)
                dpr_ref[gi, d] = dp[:n]
                dpi_ref[gi, d] = dp[n:]
            dct = _dotx(jnp.concatenate(dcs, axis=0), ch_ref[...], _mm_nt)
            for d in range(2):
                dcr_ref[gi, d] = dct[2 * d * n : (2 * d + 1) * n]
                dci_ref[gi, d] = dct[(2 * d + 1) * n : (2 * d + 2) * n]
            dw = dw_ref[gi]
            ae, be = _op_spread(rs_ref, a1_ref, a2_ref, gi), _op_spread(ts_ref, b1_ref, b2_ref, gi)
            da = _dotx(rs_ref[...], jnp.concatenate([dw * be[:, :ns], dw * be[:, ns:]], axis=1), _mm_tn)
            db = _dotx(ts_ref[...], jnp.concatenate([dw * ae[:, :ns], dw * ae[:, ns:]], axis=1), _mm_tn)
            da1_ref[gi] = da[:t, :ns]
            da2_ref[gi] = da[:t, ns:]
            db1_ref[gi] = db[:h, :ns]
            db2_ref[gi] = db[:h, ns:]
            return carry

        lax.fori_loop(0, gb, group, 0)
        pl.when(pl.program_id(0) == steps - 1)(finish)

    sm = pl.BlockSpec((gb, 2, n, LANES), lambda i: (i, 0, 0, 0))
    ab = lambda v: pl.BlockSpec((gb,) + v.shape[1:], lambda i: (i, 0, 0))
    ops = [_gspec(gb, CW, CW), _gspec(gb, CW, 4 * n), _gspec(gb, 4 * n, CW)]
    out = pl.pallas_call(
        body,
        name="s5_build_bwd",
        grid=(steps,),
        in_specs=[sm] * 6 + [ab(a1), ab(a2), ab(b1), ab(b2), _whole(pw.shape), _whole(ch.shape), _whole(rep_s.shape), _whole(til_s.shape)] + ops + [_HBM] * nr,
        out_specs=[sm] * 6 + [ab(a1), ab(a2), ab(b1), ab(b2)] + [_HBM] * nr,
        out_shape=[_sds(v.shape) for v in small] + [_sds(p.shape, p.dtype) for p in parts],
        scratch_shapes=_exchange_sems(nr),
        compiler_params=_cparams(1, VMEM_BIG),
    )(pr, pi, br, bi, cr, ci, a1, a2, b1, b2, pw, ch, rep_s, til_s, dtsum, dwst, dccat, *parts)
    return out[:nout], out[nout:]


SCAN_UNROLL = 4
GPL = LANES // SSM_H
CW = CHUNK_T * SSM_H
BW = CHUNK_T * LANES


def _chunk_perm():
    o = np.arange(BW)
    src = ((o % CW) // SSM_H) * LANES + (o // CW) * SSM_H + o % SSM_H
    return jnp.asarray(np.arange(BW)[:, None] == src[None, :], BF16)


def _gspec(gb, a, b):
    return pl.BlockSpec((gb, a, b), lambda i: (i, 0, 0))


def _chunk_rows(ref, n):
    return jnp.concatenate([ref[pl.ds(s, n, stride=CHUNK_T), :] for s in range(CHUNK_T)], axis=1)


def _s5_state(proj, col0, uc, perm, wst, ncc, ncl):
    nlb = uc.shape[1] // LANES
    g, _, ns = wst.shape
    nch = ncc + ncl
    half = ns // 2

    def body(p_ref, c_ref, pm_ref, w_ref, u_ref, sr_ref, si_ref):
        xrows = jnp.concatenate([_chunk_rows(c_ref, ncc), _chunk_rows(p_ref, ncl)], axis=0).astype(BF16)
        u = _mm(xrows, pm_ref[...]).astype(BF16)
        u_ref[0] = u
        for gi in range(GPL):
            s = _mm(u[:, gi * CW : (gi + 1) * CW], w_ref[gi].astype(BF16))
            sr_ref[gi] = s[:, :half]
            si_ref[gi] = s[:, half:]

    return pl.pallas_call(
        body,
        name="s5_state",
        grid=(nlb,),
        in_specs=[
            pl.BlockSpec((ncl * CHUNK_T, LANES), lambda i: (0, col0 + i)),
            pl.BlockSpec((ncc * CHUNK_T, LANES), lambda i: (0, i)),
            _whole(perm.shape),
            _gspec(GPL, CW, ns),
        ],
        out_specs=[pl.BlockSpec((1, nch, BW), lambda i: (i, 0, 0)), _gspec(GPL, nch, half), _gspec(GPL, nch, half)],
        out_shape=[_sds((nlb, nch, BW), BF16), _sds((g, nch, half)), _sds((g, nch, half))],
        compiler_params=_cparams(1, VMEM_BIG),
    )(proj, uc, perm, wst)


def _unrolled_loop(n, step, init):
    u = SCAN_UNROLL if n % SCAN_UNROLL == 0 else 1

    def trip(i, c):
        for k in range(u):
            c = step(i * u + k, c)
        return c

    return lax.fori_loop(0, n // u, trip, init)


def _lane_masks(gb, half):
    lane = lax.broadcasted_iota(jnp.int32, (gb, half), 1)
    return lane < (half // 2)


def _roll_half(v):
    return pltpu.roll(v, v.shape[-1] // 2, axis=v.ndim - 1)


def _pack_dirs(lo, hi, mf):
    return jnp.where(mf, lo, _roll_half(hi)), jnp.where(mf, _roll_half(lo), hi)


def _unpack_dirs(pf, pb, mf):
    return jnp.where(mf, pf, _roll_half(pb)), jnp.where(mf, _roll_half(pf), pb)


def _packed_spec(gb, nch, half, buffers=None):
    mode = {} if buffers is None else dict(pipeline_mode=pl.Buffered(buffers))
    if gb == 2 * GPL:
        return pl.BlockSpec((GPL, nch, half), lambda i: (i // 2, 0, 0), **mode)
    assert gb == GPL
    return pl.BlockSpec((gb // 2, nch, half), lambda i: (i, 0, 0), **mode)


def _pair_index(j, gb):
    if gb == 2 * GPL:
        return j, pl.program_id(0) % 2
    return j % (gb // 2), j // (gb // 2)


def _pick(which, lo, hi):
    return (hi if which else lo) if isinstance(which, int) else jnp.where(which == 0, lo, hi)


def _unpacked(pf_ref, pb_ref, j, gb, mf):
    jl, which = _pair_index(j, gb)
    return _pick(which, *_unpack_dirs(pf_ref[jl], pb_ref[jl], mf))


def _pair_tile(refs, jl, rows):
    return jnp.concatenate([r[jl, rows, :] for r in refs], axis=1)


def _pair_rows_of(op, which, q):
    z = jnp.zeros((q, op.shape[1]), op.dtype)
    blocks = [op[0:q], op[2 * q : 3 * q], op[q : 2 * q], op[3 * q : 4 * q]]
    lo = jnp.concatenate([x for b in blocks for x in (b, z)], axis=0)
    hi = jnp.concatenate([x for b in blocks for x in (z, b)], axis=0)
    return _pick(which, lo, hi)


def _group_rows_of(pair_rows, which, q):
    def take(off):
        blk = lambda k: pair_rows[2 * q * k + off : 2 * q * k + off + q]
        return jnp.concatenate([blk(0), blk(2), blk(1), blk(3)], axis=0)

    return _pick(which, take(0), take(q))


def _scan_rows(ref, row, hp, nch, mf):
    return _pack_dirs(ref[pl.ds(row, hp, stride=nch), :], ref[pl.ds(hp * nch + row, hp, stride=nch), :], mf)


def _s5_scan_fwd(s_re, s_im, a16, ncc, nch, gb):
    g, ns = a16.shape
    half = ns // 2
    hp = gb // 2

    def body(sr_ref, si_ref, a_ref, hfr_ref, hfi_ref, hbr_ref, hbi_ref):
        mf = _lane_masks(hp, half)
        afr, abr = _pack_dirs(a_ref[:hp, :half], a_ref[hp:, :half], mf)
        afi, abi = _pack_dirs(a_ref[:hp, half:], a_ref[hp:, half:], mf)
        zero = jnp.zeros((hp, half), F32)

        def step(t, c):
            fr, fi, br, bi = c
            rb = jnp.where(t < ncc, ncc - 1 - t, nch - 1 - (t - ncc))
            hfr_ref[pl.ds(t, hp, stride=nch), :] = fr
            hfi_ref[pl.ds(t, hp, stride=nch), :] = fi
            hbr_ref[pl.ds(rb, hp, stride=nch), :] = br
            hbi_ref[pl.ds(rb, hp, stride=nch), :] = bi
            sfr, sfi = _scan_rows(sr_ref, t, hp, nch, mf)[0], _scan_rows(si_ref, t, hp, nch, mf)[0]
            sbr, sbi = _scan_rows(sr_ref, rb, hp, nch, mf)[1], _scan_rows(si_ref, rb, hp, nch, mf)[1]
            return (afr * fr - afi * fi + sfr, afr * fi + afi * fr + sfi, abr * br - abi * bi + sbr, abr * bi + abi * br + sbi)

        _unrolled_loop(nch, step, (zero, zero, zero, zero))

    blk = pl.BlockSpec((gb * nch, half), lambda i: (i, 0))
    pblk = pl.BlockSpec((hp * nch, half), lambda i: (i, 0))
    return pl.pallas_call(
        body,
        name="s5_scan_fwd",
        grid=(g // gb,),
        in_specs=[blk, blk, pl.BlockSpec((gb, ns), lambda i: (i, 0))],
        out_specs=[pblk] * 4,
        out_shape=[_sds((g // 2 * nch, half))] * 4,
        compiler_params=_cparams(1, VMEM_BIG),
    )(s_re, s_im, a16)


def _token_rows_store(ref, val, n):
    for s in range(CHUNK_T):
        ref[pl.ds(s, n, stride=CHUNK_T), :] = val[:, s * LANES : (s + 1) * LANES]


def _s5_out(u_all, h4, tsum, ccat, perm, ncc, lx, gb):
    nlb, nch, _ = u_all.shape
    g, ns, _ = ccat.shape
    ncl = nch - ncc
    half = ns // 2

    def body(u_ref, hfr_ref, hfi_ref, hbr_ref, hbi_ref, t_ref, c_ref, pm_ref, y_ref):
        parts = []
        for gi in range(GPL):
            u = u_ref[0, ncc:, gi * CW : (gi + 1) * CW]
            jl, which = _pair_index(gi, gb)
            hs = _pair_tile((hfr_ref, hfi_ref, hbr_ref, hbi_ref), jl, slice(ncc, None)).astype(BF16)
            parts.append(_mm(u, t_ref[gi].astype(BF16)) + _mm(hs, _pair_rows_of(c_ref[gi].astype(BF16), which, ns // 4)))
        _token_rows_store(y_ref, _mm_nt(jnp.concatenate(parts, axis=1).astype(BF16), pm_ref[...]), ncl)

    return pl.pallas_call(
        body,
        name="s5_out",
        grid=(nlb,),
        in_specs=[pl.BlockSpec((1, nch, BW), lambda i: (i, 0, 0))] + [_packed_spec(gb, nch, half)] * 4 + [_gspec(GPL, CW, CW), _gspec(GPL, ns, CW), _whole(perm.shape)],
        out_specs=pl.BlockSpec((lx, LANES), lambda i: (0, i)),
        out_shape=_sds((lx, nlb * LANES)),
        compiler_params=_cparams(1, VMEM_BIG),
    )(u_all, *h4, tsum, ccat, perm)


def _s5_dstate(dypre, perm, ccat, ncc, swaps=()):
    lx, sw = dypre.shape
    nr = len(swaps)
    nlb = sw // LANES
    g, ns, _ = ccat.shape
    ncl = lx // CHUNK_T
    nch = ncl + ncc
    half = ns // 2

    def body(dy_ref, pm_ref, c_ref, dyr_ref, dr_ref, di_ref):
        dy = _mm(_chunk_rows(dy_ref, ncl).astype(BF16), pm_ref[...]).astype(BF16)
        dyr_ref[0] = dy
        for gi in range(GPL):
            dh = _mm_nt(dy[:, gi * CW : (gi + 1) * CW], c_ref[gi].astype(BF16))
            dr_ref[gi, :ncc, :] = jnp.zeros((ncc, half), F32)
            di_ref[gi, :ncc, :] = jnp.zeros((ncc, half), F32)
            dr_ref[gi, ncc:, :] = dh[:, :half]
            di_ref[gi, ncc:, :] = dh[:, half:]

    out = pl.pallas_call(
        _riding(body, 3, 3, nr, (nlb,), _swap_copies),
        name="s5_dstate",
        grid=(nlb,),
        in_specs=[pl.BlockSpec((lx, LANES), lambda i: (0, i)), _whole(perm.shape), _gspec(GPL, ns, CW)] + [_HBM] * nr,
        out_specs=[pl.BlockSpec((1, ncl, BW), lambda i: (i, 0, 0)), _gspec(GPL, nch, half), _gspec(GPL, nch, half)] + [_HBM] * nr,
        out_shape=[_sds((nlb, ncl, BW), BF16), _sds((g, nch, half)), _sds((g, nch, half))] + _swap_shapes(swaps),
        scratch_shapes=_swap_sems(nr) if nr else [],
        compiler_params=_cparams(1, VMEM_BIG),
    )(dypre, perm, ccat, *swaps)
    return out[:3], out[3:]


def _s5_scan_bwd(dh_re, dh_im, a16, ncc, nch, gb):
    g, ns = a16.shape
    half = ns // 2
    ncl = nch - ncc
    hp = gb // 2

    def body(dr_ref, di_ref, a_ref, sfr_ref, sfi_ref, sbr_ref, sbi_ref):
        mf = _lane_masks(hp, half)
        afr, abr = _pack_dirs(a_ref[:hp, :half], a_ref[hp:, :half], mf)
        afi, abi = _pack_dirs(a_ref[:hp, half:], a_ref[hp:, half:], mf)
        zero = jnp.zeros((hp, half), F32)

        def step(t, c):
            lfr, lfi, lbr, lbi = c
            pf = nch - 1 - t
            pb = jnp.where(t < ncl, ncc + t, t - ncl)
            sfr_ref[pl.ds(pf, hp, stride=nch), :] = lfr
            sfi_ref[pl.ds(pf, hp, stride=nch), :] = lfi
            sbr_ref[pl.ds(pb, hp, stride=nch), :] = lbr
            sbi_ref[pl.ds(pb, hp, stride=nch), :] = lbi
            dfr, dfi = _scan_rows(dr_ref, pf, hp, nch, mf)[0], _scan_rows(di_ref, pf, hp, nch, mf)[0]
            dbr, dbi = _scan_rows(dr_ref, pb, hp, nch, mf)[1], _scan_rows(di_ref, pb, hp, nch, mf)[1]
            return (dfr + afr * lfr + afi * lfi, dfi + afr * lfi - afi * lfr, dbr + abr * lbr + abi * lbi, dbi + abr * lbi - abi * lbr)

        _unrolled_loop(nch, step, (zero,) * 4)

    blk = pl.BlockSpec((gb * nch, half), lambda i: (i, 0))
    pblk = pl.BlockSpec((hp * nch, half), lambda i: (i, 0))
    return pl.pallas_call(
        body,
        name="s5_scan_bwd",
        grid=(g // gb,),
        in_specs=[blk, blk, pl.BlockSpec((gb, ns), lambda i: (i, 0))],
        out_specs=[pblk] * 4,
        out_shape=[_sds((g // 2 * nch, half))] * 4,
        compiler_params=_cparams(1, VMEM_BIG),
    )(dh_re, dh_im, a16)


def _s5_du(dy_all, ds4, tsum, wst, perm, ncc, lx, lc, gb):
    nlb, ncl, _ = dy_all.shape
    g, _, ns = wst.shape
    nch = ncc + ncl
    half = ns // 2

    def body(dy_ref, sfr_ref, sfi_ref, sbr_ref, sbi_ref, t_ref, w_ref, pm_ref, du_ref, dc_ref):
        parts = []
        mf = _lane_masks(nch, half)
        for gi in range(GPL):
            ds = jnp.concatenate([_unpacked(sfr_ref, sbr_ref, gi, gb, mf), _unpacked(sfi_ref, sbi_ref, gi, gb, mf)], axis=1).astype(BF16)
            d_all = _mm_nt(ds, w_ref[gi].astype(BF16))
            d_lat = d_all[ncc:] + _mm_nt(dy_ref[0, :, gi * CW : (gi + 1) * CW], t_ref[gi].astype(BF16))
            parts.append(jnp.concatenate([d_all[:ncc], d_lat], axis=0))
        du = _mm_nt(jnp.concatenate(parts, axis=1).astype(BF16), pm_ref[...])
        _token_rows_store(dc_ref, du[:ncc], ncc)
        _token_rows_store(du_ref, du[ncc:], ncl)

    return pl.pallas_call(
        body,
        name="s5_du",
        grid=(nlb,),
        in_specs=[pl.BlockSpec((1, ncl, BW), lambda i: (i, 0, 0))] + [_packed_spec(gb, nch, half)] * 4 + [_gspec(GPL, CW, CW), _gspec(GPL, CW, ns), _whole(perm.shape)],
        out_specs=[pl.BlockSpec((lx, LANES), lambda i: (0, i)), pl.BlockSpec((lc, LANES), lambda i: (0, i))],
        out_shape=[_sds((lx, nlb * LANES)), _sds((lc, nlb * LANES))],
        compiler_params=_cparams(1, VMEM_BIG),
    )(dy_all, *ds4, tsum, wst, perm)


def _s5_op_grads(u_all, dy_all, ds4, h4, ncc, gb):
    nlb, nch, _ = u_all.shape
    half = ds4[0].shape[2]
    g = 2 * ds4[0].shape[0]
    ns = 2 * half

    def body(u_ref, dy_ref, sfr_ref, sfi_ref, sbr_ref, sbi_ref, hfr_ref, hfi_ref, hbr_ref, hbi_ref, dw_ref, dt_ref, dc_ref, da_ref):
        das = []
        mf = _lane_masks(nch, half)
        mf1 = _lane_masks(1, half)
        for gi in range(GPL):
            cols = slice(gi * CW, (gi + 1) * CW)
            jl, which = _pair_index(gi, gb)
            sr, si = _unpacked(sfr_ref, sbr_ref, gi, gb, mf), _unpacked(sfi_ref, sbi_ref, gi, gb, mf)
            ds = jnp.concatenate([sr, si], axis=1).astype(BF16)
            dy = dy_ref[0, :, cols]
            dw_ref[gi] = _mm_tn(u_ref[0, :, cols], ds)
            dt_ref[gi] = _mm_tn(u_ref[0, ncc:, cols], dy)
            hs = _pair_tile((hfr_ref, hfi_ref, hbr_ref, hbi_ref), jl, slice(ncc, None)).astype(BF16)
            dc_ref[gi] = _group_rows_of(_mm_tn(hs, dy), which, ns // 4)
            hfr, hfi, hbr, hbi = hfr_ref[jl], hfi_ref[jl], hbr_ref[jl], hbi_ref[jl]
            sfr, sfi, sbr, sbi = sfr_ref[jl], sfi_ref[jl], sbr_ref[jl], sbi_ref[jl]
            a_re = _pick(which, *_unpack_dirs(_colsum(hfr * sfr + hfi * sfi), _colsum(hbr * sbr + hbi * sbi), mf1))
            a_im = _pick(which, *_unpack_dirs(_colsum(hfr * sfi - hfi * sfr), _colsum(hbr * sbi - hbi * sbr), mf1))
            das.append(jnp.concatenate([a_re, a_im], axis=1))
        da_ref[...] = jnp.concatenate(das, axis=0)

    return pl.pallas_call(
        body,
        name="s5_op_grads",
        grid=(nlb,),
        in_specs=[pl.BlockSpec((1, nch, BW), lambda i: (i, 0, 0)), pl.BlockSpec((1, nch - ncc, BW), lambda i: (i, 0, 0))]
        + [_packed_spec(gb, nch, half)] * 4
        + [_packed_spec(gb, nch, half, buffers=1)] * 4,
        out_specs=[_gspec(GPL, CW, ns), _gspec(GPL, CW, CW), _gspec(GPL, ns, CW), pl.BlockSpec((GPL, ns), lambda i: (i, 0))],
        out_shape=[_sds((g, CW, ns)), _sds((g, CW, CW)), _sds((g, ns, CW)), _sds((g, ns))],
        compiler_params=_cparams(1, VMEM_BIG),
    )(u_all, dy_all, *ds4, *h4)


def _merge_fwd(y_ssm, proj, lin, x, tgt, glu_wg, out_w, d_skip, glu_b, pscale, gate, fin_g, tm):
    lx, d = x.shape
    sw = y_ssm.shape[1]
    mix = out_w.shape[0]
    pw = mix - sw
    cb2 = glu_wg.shape[2]
    nh = NDEV // 2

    def body(y_ref, u_ref, z_ref, l_ref, x_ref, t_ref, gw_ref, ow_ref, dk_ref, gb_ref, ps_ref, gt_ref, fg_ref,
             ypre_ref, y12_ref, br_ref, dxo_ref, loss_ref, dfg_ref, dgt_ref):
        @pl.when(pl.program_id(0) == 0)
        def _():
            loss_ref[...] = jnp.zeros_like(loss_ref)
            dfg_ref[...] = jnp.zeros_like(dfg_ref)
            dgt_ref[...] = jnp.zeros_like(dgt_ref)

        ypre = y_ref[...] + dk_ref[...] * u_ref[...]
        ypre_ref[...] = ypre
        yg = _gelu(ypre).astype(BF16)
        outs = []
        for j in range(nh):
            y1 = _mm(yg, gw_ref[j]) + gb_ref[:, j * cb2 : (j + 1) * cb2]
            y2 = _mm(yg, gw_ref[nh + j]) + gb_ref[:, (nh + j) * cb2 : (nh + j + 1) * cb2]
            y12_ref[:, j * cb2 : (j + 1) * cb2] = y1
            y12_ref[:, (nh + j) * cb2 : (nh + j + 1) * cb2] = y2
            outs.append(y1 * _sigmoid(y2))
        lin = l_ref[...] * ps_ref[...]
        brb = (jnp.concatenate([lin] + outs, axis=1) * _silu(z_ref[...])).astype(BF16)
        br_ref[...] = brb
        mixv = _mm(brb, ow_ref[...])
        xo = x_ref[...] + gt_ref[...] * mixv
        r2 = lax.rsqrt(_rowmean(xo * xo) + EPS)
        xh = xo * r2
        fg = fg_ref[...]
        e = xh * fg - t_ref[...]
        loss_ref[...] += 0.5 * _colsum(_rowmean(e * e))
        dy = e * (1.0 / d)
        dfg_ref[...] += _colsum(dy * xh)
        gy = dy * fg
        dxo = r2 * (gy - xh * _rowmean(gy * xh))
        dxo_ref[...] = dxo
        dgt_ref[...] += _colsum(dxo * mixv)

    vec = _whole((1, d))
    return pl.pallas_call(
        body,
        name="merge_fwd",
        grid=(lx // tm,),
        in_specs=[_rows(tm, sw), _rows(tm, sw, 1), _rows(tm, mix, 1), _rows(tm, pw), _rows(tm, d), _rows(tm, d), _whole(glu_wg.shape), _whole(out_w.shape), _whole((1, sw)), _whole((1, 2 * sw)), _whole((1, pw)), vec, vec],
        out_specs=[_rows(tm, sw), _rows(tm, 2 * sw), _rows(tm, mix), _rows(tm, d), _acc((1, 1)), _acc((1, d)), _acc((1, d))],
        out_shape=[_sds((lx, sw)), _sds((lx, 2 * sw)), _sds((lx, mix), BF16), _sds((lx, d)), _sds((1, 1)), _sds((1, d)), _sds((1, d))],
        compiler_params=_cparams(1, VMEM_BIG),
    )(y_ssm, proj, proj, lin, x, tgt, glu_wg, out_w, d_skip, glu_b, pscale, gate, fin_g)


def _out_bwd(dxo, gate, branch, out_w, lin, y12, proj, pscale, tm):
    lx, d = dxo.shape
    mix = out_w.shape[0]
    pw = lin.shape[1]
    sw = mix - pw
    nb = lx // tm

    def body(dxo_ref, gt_ref, br_ref, ow_ref, l_ref, y_ref, z_ref, ps_ref, dz_ref, dl_ref, dg_ref, dps_ref, dow_hbm, acc, sem):
        i = pl.program_id(0)

        @pl.when(i == 0)
        def _():
            acc[...] = jnp.zeros_like(acc)
            dps_ref[...] = jnp.zeros_like(dps_ref)

        dmix = (dxo_ref[...] * gt_ref[...]).astype(BF16)
        dbr = _mm_nt(dmix, ow_ref[...])
        acc[...] += _mm_tn(br_ref[...], dmix)
        z = z_ref[...]
        sg = _sigmoid(z)
        dbp = dbr * (z * sg)
        y1, y2 = y_ref[:, :sw], y_ref[:, sw:]
        s2 = _sigmoid(y2)
        ps = ps_ref[...]
        lin = l_ref[...]
        bp = jnp.concatenate([lin * ps, y1 * s2], axis=1)
        dz_ref[...] = dbr * bp * (sg * (1.0 + z * (1.0 - sg)))
        dlp = dbp[:, :pw]
        dl_ref[...] = dlp * ps
        dps_ref[...] += _colsum(dlp * lin)
        dss = dbp[:, pw:]
        dg_ref[...] = jnp.concatenate([dss * s2, dss * y1 * s2 * (1.0 - s2)], axis=1).astype(BF16)

        @pl.when(i == nb - 1)
        def _():
            cp = pltpu.make_async_copy(acc, dow_hbm, sem)
            cp.start()
            cp.wait()

    return pl.pallas_call(
        body,
        name="out_bwd",
        grid=(nb,),
        in_specs=[_rows(tm, d), _whole((1, d)), _rows(tm, mix), _whole(out_w.shape), _rows(tm, pw), _rows(tm, 2 * sw), _rows(tm, mix, 1), _whole((1, pw))],
        out_specs=[_rows(tm, mix), _rows(tm, pw), _rows(tm, 2 * sw), _acc((1, pw)), _ANY],
        out_shape=[_sds((lx, mix)), _sds((lx, pw)), _sds((lx, 2 * sw), BF16), _sds((1, pw)), _sds((mix, d))],
        scratch_shapes=[pltpu.VMEM((mix, d), F32), pltpu.SemaphoreType.DMA],
        compiler_params=_cparams(1, VMEM_BIG),
    )(dxo, gate, branch, out_w, lin, y12, proj, pscale)


def _glu_bwd(dg12, ypre, proj, glu_wg, d_skip, tm):
    lx, sw = ypre.shape
    cb2 = glu_wg.shape[2]
    nb = lx // tm

    def body(dg_ref, yp_ref, u_ref, gw_ref, dk_ref, dyp_ref, dgb_ref, dd_ref, dgw_hbm, acc, sem):
        i = pl.program_id(0)

        @pl.when(i == 0)
        def _():
            acc[...] = jnp.zeros_like(acc)
            dgb_ref[...] = jnp.zeros_like(dgb_ref)
            dd_ref[...] = jnp.zeros_like(dd_ref)

        ypre = yp_ref[...]
        ygb = _gelu(ypre).astype(BF16)
        dg = dg_ref[...]
        dyg = jnp.zeros((tm, sw), F32)
        for j in range(NDEV):
            dgj = dg[:, j * cb2 : (j + 1) * cb2]
            dyg = dyg + _mm_nt(dgj, gw_ref[j])
            acc[j] += _mm_tn(ygb, dgj)
        dgb_ref[...] += _colsum(dg.astype(F32))
        dyp = dyg * _gelu_grad(ypre)
        dyp_ref[...] = dyp
        dd_ref[...] += _colsum(dyp * u_ref[...])

        @pl.when(i == nb - 1)
        def _():
            cp = pltpu.make_async_copy(acc, dgw_hbm, sem)
            cp.start()
            cp.wait()

    return pl.pallas_call(
        body,
        name="glu_bwd",
        grid=(nb,),
        in_specs=[_rows(tm, 2 * sw), _rows(tm, sw), _rows(tm, sw, 1), _whole(glu_wg.shape), _whole((1, sw))],
        out_specs=[_rows(tm, sw), _acc((1, 2 * sw)), _acc((1, sw)), _ANY],
        out_shape=[_sds((lx, sw)), _sds((1, 2 * sw)), _sds((1, sw)), _sds(glu_wg.shape)],
        scratch_shapes=[pltpu.VMEM(glu_wg.shape, F32), pltpu.SemaphoreType.DMA],
        compiler_params=_cparams(1, VMEM_BIG),
    )(dg12, ypre, proj, glu_wg, d_skip)


def _pack(parts, total):
    flat = jnp.concatenate([p.reshape(-1).astype(F32) for p in parts])
    return jnp.pad(flat, (0, total - flat.shape[0]))


def _unpack(flat, shapes):
    out, off = [], 0
    for s in shapes:
        n = int(np.prod(s)) if len(s) else 1
        out.append(flat[off : off + n].reshape(s))
        off += n
    return out


def kernel(x, c, ctx, c_ctx, ada_w, ada_b, norm_g, in_w, pool_w, pool_scale, s5_lam_re, s5_lam_im, s5_log_dt, s5_b_re, s5_b_im, s5_c_re, s5_c_im, s5_d, glu_w, glu_b, out_w, final_g, loss_target, m_c_ctx, m_ada_w, m_ada_b, m_norm_g, m_in_w, m_pool_w, m_pool_scale, m_s5_lam_re, m_s5_lam_im, m_s5_log_dt, m_s5_b_re, m_s5_b_im, m_s5_c_re, m_s5_c_im, m_s5_d, m_glu_w, m_glu_b, m_out_w, m_final_g, v_c_ctx, v_ada_w, v_ada_b, v_norm_g, v_in_w, v_pool_w, v_pool_scale, v_s5_lam_re, v_s5_lam_im, v_s5_log_dt, v_s5_b_re, v_s5_b_im, v_s5_c_re, v_s5_c_im, v_s5_d, v_glu_w, v_glu_b, v_out_w, v_final_g):
    xr, tgt, xc = x[0], loss_target[0], ctx[0]
    lx, d = xr.shape
    lc = xc.shape[0]
    mix = out_w.shape[1] * NDEV
    sw = glu_w.shape[1]
    pw = mix - sw
    pgw = pw // len(POOL_WINDOWS)
    ngrp = sw // SSM_H
    cb = in_w.shape[2]
    cb2 = glu_w.shape[2]
    nmod = ada_w.shape[2]
    ncc, ncl = lc // CHUNK_T, lx // CHUNK_T
    nch = ncc + ncl
    tm = min(256, lx)
    tm_acc = min(512, lx)
    assert mix == d and pw == sw and 2 * cb == pw and NDEV * cb2 == 2 * sw and lx % GRID_W == 0
    mx, my, mc = _my_pos()
    me = 4 * mx + 2 * my + mc

    (c_all,) = _all_gather([c], "gather_c")
    call = jnp.concatenate([c_all.reshape(NDEV, d), c_ctx.reshape(1, d), jnp.zeros((NDEV - 1, d), F32)], axis=0)
    ada_w_l = ada_w[0]
    ada_b_l = lax.dynamic_slice_in_dim(ada_b, me * nmod, nmod, axis=1)
    m_loc = _mod_fwd(call, ada_w_l, ada_b_l)
    (m_all,) = _all_gather([m_loc], "gather_mod")
    mod = lax.dynamic_index_in_dim(m_all, me, axis=1, keepdims=False).reshape(1, NDEV * nmod)
    mod_c = m_all[:, NDEV, :].reshape(1, NDEV * nmod)
    shift, scale, gate = mod[:, :d], mod[:, d : 2 * d], mod[:, 2 * d :]
    shift_c, scale_c = mod_c[:, :d], mod_c[:, d : 2 * d]

    s5_params = tuple(p[0] for p in (s5_lam_re, s5_lam_im, s5_log_dt, s5_b_re, s5_b_im, s5_c_re, s5_c_im))
    (small, a16), small_vjp = jax.vjp(_s5_small, *s5_params)
    gb_ops = min(8, ngrp)
    (tsum, wst, ccat), (in_wg,) = _s5_build(small, gb_ops, [in_w[0].astype(BF16)])

    ssm_blocks = (2, 3)
    (proj, hb), (glu_wg, out_wg, pool_wg) = _in_proj(
        xr, norm_g, scale, shift, in_wg, tuple(range(NDEV)), tm, "in_proj",
        shards=[glu_w[0].astype(BF16), out_w[0].astype(BF16), pool_w[0].reshape(-1, pgw).astype(BF16)],
    )
    out_wf = out_wg.reshape(mix, d)
    pool_wf = pool_wg.reshape(NDEV, len(POOL_WINDOWS), pgw // NDEV, pgw).transpose(1, 0, 2, 3).reshape(len(POOL_WINDOWS), pgw, pgw)
    (uc, hcb), _ = _in_proj(xc, norm_g, scale_c, shift_c, in_wg, ssm_blocks, min(tm, lc), "in_proj_ctx")
    lin, dmat = _pool_fwd(proj, pool_wf)

    gb_scan = min(16, ngrp)
    half = wst.shape[2] // 2
    perm = _chunk_perm()
    flat = lambda a: a.reshape(ngrp * nch, half)
    pairs = lambda arrs: [a.reshape(ngrp // 2, nch, half) for a in arrs]
    u_all, s_re, s_im = _s5_state(proj, pw // LANES, uc, perm, wst, ncc, ncl)
    h4 = pairs(_s5_scan_fwd(flat(s_re), flat(s_im), a16, ncc, nch, gb_scan))
    y_ssm = _s5_out(u_all, h4, tsum, ccat, perm, ncc, lx, gb_scan)

    ypre, y12, branch, dxo, loss_l, dfg, dgate = _merge_fwd(
        y_ssm, proj, lin, xr, tgt, glu_wg, out_wf, s5_d, glu_b, pool_scale, gate, final_g.reshape(1, d), tm
    )

    dz, dlin, dg12, dps, d_out_w = _out_bwd(dxo, gate, branch, out_wf, lin, y12, proj, pool_scale, tm)
    dypre, dglu_b, dd_skip, d_glu_w = _glu_bwd(dg12, ypre, proj, glu_wg, s5_d, tm_acc)

    d_glu_out = [d_glu_w, d_out_w.reshape(NDEV, mix // NDEV, d)]
    (dy_all, dh_re, dh_im), got_glu_out = _s5_dstate(dypre, perm, ccat, ncc, swaps=[_by_chip_and_core(a) for a in d_glu_out])
    ds4 = pairs(_s5_scan_bwd(flat(dh_re), flat(dh_im), a16, ncc, nch, gb_scan))
    du5, duc = _s5_du(dy_all, ds4, tsum, wst, perm, ncc, lx, lc, gb_scan)
    dwst, dtsum, dccat, da16 = _s5_op_grads(u_all, dy_all, ds4, h4, ncc, gb_scan)
    dup, dpool_w = _pool_bwd(dlin, pool_wf, dmat)

    npw = len(POOL_WINDOWS)
    dpool_s = dpool_w.reshape(npw, NDEV, pgw // NDEV, pgw).transpose(1, 0, 2, 3).reshape(NDEV, npw * (pgw // NDEV), pgw)
    parts_a = _pair_partials(d_glu_out, [BF16] * 2, "grads_a", got=got_glu_out) + _pair_partials([dpool_s], [BF16], "grads_p")
    dpcb, dsc_c, dsh_c, dg_c = _in_bwd_ctx(xc, duc, in_wg, ssm_blocks, norm_g, scale_c, shift_c)
    grad_x, dpb, dsc, dsh, dnorm_g = _in_bwd(xr, dxo, dup, du5, dypre, dz, in_wg, norm_g, scale, shift, s5_d, dg_c, tm)
    d_in_w, (land_glu, land_out, land_pool) = _in_w_grad(hb, dpb, hcb, dpcb, cb, min(2 * tm_acc, lx), parts=parts_a)

    dmod = jnp.concatenate(
        [jnp.concatenate([dsh, dsc, dgate], axis=1), jnp.concatenate([dsh_c, dsc_c, jnp.zeros((1, d), F32)], axis=1)], axis=0
    )
    (dmod_all,) = _all_gather([dmod], "gather_dmod")
    dmod_l = lax.dynamic_slice_in_dim(dmod_all, me * nmod, nmod, axis=2)
    g_ada_w, g_ada_b_l, cctx_part = _mod_bwd(call, dmod_l[:, 0, :], dmod_l[:, 1, :], ada_w_l)

    parts_b = _pair_partials([d_in_w], [BF16], "grads_b")
    dsmall, (land_in,) = _s5_build_bwd(small, dtsum, dwst, dccat, gb_ops, parts_b)
    ds5 = small_vjp((tuple(dsmall), da16))

    small_names = ["norm_g", "pool_scale", "s5_lam_re", "s5_lam_im", "s5_log_dt", "s5_b_re", "s5_b_im", "s5_c_re", "s5_c_im", "s5_d", "glu_b", "final_g", "c_ctx"]
    small_w = dict(norm_g=norm_g, pool_scale=pool_scale, s5_lam_re=s5_lam_re, s5_lam_im=s5_lam_im, s5_log_dt=s5_log_dt, s5_b_re=s5_b_re, s5_b_im=s5_b_im, s5_c_re=s5_c_re, s5_c_im=s5_c_im, s5_d=s5_d, glu_b=glu_b, final_g=final_g, c_ctx=c_ctx)
    small_m = dict(norm_g=m_norm_g, pool_scale=m_pool_scale, s5_lam_re=m_s5_lam_re, s5_lam_im=m_s5_lam_im, s5_log_dt=m_s5_log_dt, s5_b_re=m_s5_b_re, s5_b_im=m_s5_b_im, s5_c_re=m_s5_c_re, s5_c_im=m_s5_c_im, s5_d=m_s5_d, glu_b=m_glu_b, final_g=m_final_g, c_ctx=m_c_ctx)
    small_v = dict(norm_g=v_norm_g, pool_scale=v_pool_scale, s5_lam_re=v_s5_lam_re, s5_lam_im=v_s5_lam_im, s5_log_dt=v_s5_log_dt, s5_b_re=v_s5_b_re, s5_b_im=v_s5_b_im, s5_c_re=v_s5_c_re, s5_c_im=v_s5_c_im, s5_d=v_s5_d, glu_b=v_glu_b, final_g=v_final_g, c_ctx=v_c_ctx)
    small_g = dict(norm_g=dnorm_g, pool_scale=dps, s5_lam_re=ds5[0], s5_lam_im=ds5[1], s5_log_dt=ds5[2], s5_b_re=ds5[3], s5_b_im=ds5[4], s5_c_re=ds5[5], s5_c_im=ds5[6], s5_d=dd_skip, glu_b=dglu_b, final_g=dfg, c_ctx=cctx_part)
    shapes = [small_w[k].shape for k in small_names]
    nsmall = sum(int(np.prod(s)) for s in shapes) + 1
    unit = NDEV * 8 * LANES
    tot = -(-nsmall // unit) * unit
    rows = tot // LANES
    gpack = _pack([small_g[k] for k in small_names] + [loss_l], tot).reshape(NDEV, rows // NDEV, LANES)
    (land_small,) = _reduce_scatter([gpack], [F32], "small")
    gsum = _sum_slots(land_small, "sum_small")
    (gall,) = _all_gather([gsum], "gather_small")
    (g_ada_b_all,) = _all_gather([g_ada_b_l], "gather_ada_b")
    gflat = gall.reshape(tot)
    loss = gflat[nsmall - 1]

    nab = NDEV * nmod
    tot2 = tot + -(-nab // (8 * LANES)) * (8 * LANES)
    g2 = jnp.concatenate([gflat, jnp.pad(g_ada_b_all.reshape(nab), (0, tot2 - tot - nab))]).reshape(tot2 // LANES, LANES)

    def pack2(dct, ab):
        return jnp.concatenate(
            [_pack([dct[k] for k in small_names] + [jnp.zeros((1,), F32)], tot), jnp.pad(ab.reshape(nab), (0, tot2 - tot - nab))]
        ).reshape(tot2 // LANES, LANES)

    _, d2, m2, v2 = _adam(g2, pack2(small_w, ada_b), pack2(small_m, m_ada_b), pack2(small_v, v_ada_b), "adam_small")

    def unpack2(flat2):
        flat2 = flat2.reshape(tot2)
        vals = dict(zip(small_names, _unpack(flat2, shapes)))
        vals["ada_b"] = flat2[tot : tot + nab].reshape(ada_b.shape)
        return vals

    res = {k: unpack2(a) for k, a in (("g", g2), ("d", d2), ("m", m2), ("v", v2))}

    def shard(name, g, land, w, m, v):
        shp = w.shape
        w2, m2_, v2_ = (a.reshape(-1, shp[-1]) for a in (w, m, v))
        out = _adam(g, w2, m2_, v2_, "adam_" + name, land=land)
        for k, a in zip(("g", "d", "m", "v"), out):
            res[k][name] = a.reshape(shp)

    shard("ada_w", g_ada_w, None, ada_w, m_ada_w, v_ada_w)
    shard("in_w", None, land_in, in_w, m_in_w, v_in_w)
    shard("pool_w", None, land_pool, pool_w, m_pool_w, v_pool_w)
    shard("glu_w", None, land_glu, glu_w, m_glu_w, v_glu_w)
    shard("out_w", None, land_out, out_w, m_out_w, v_out_w)

    names = ["c_ctx", "ada_w", "ada_b", "norm_g", "in_w", "pool_w", "pool_scale", "s5_lam_re", "s5_lam_im", "s5_log_dt", "s5_b_re", "s5_b_im", "s5_c_re", "s5_c_im", "s5_d", "glu_w", "glu_b", "out_w", "final_g"]
    return (loss, grad_x[None], *[res["g"][n] for n in names], *[res["d"][n] for n in names], *[res["m"][n] for n in names], *[res["v"][n] for n in names])
```

```python
import numpy as np

import jax
import jax.numpy as jnp
from jax import lax
from jax.experimental import pallas as pl
from jax.experimental.pallas import tpu as pltpu

F32 = jnp.float32
BF16 = jnp.bfloat16
NDEV = 8
EPS = 1e-6
GRID_W = 64
POOL_WINDOWS = (2, 4, 8, 16)
SSM_H = 16
CHUNK_T = 16
LANES = 128
ADAM_LR, ADAM_B1, ADAM_B2, ADAM_EPS, ADAM_WD, ADAM_STEP = 0.001, 0.9, 0.999, 1e-08, 0.01, 10
VMEM_BIG = 56 << 20
MESH_ID = pl.DeviceIdType.MESH

_HBM = pl.BlockSpec(memory_space=pltpu.HBM)
_ANY = pl.BlockSpec(memory_space=pl.ANY)
_SMEM = pl.BlockSpec(memory_space=pltpu.SMEM)


def _sds(shape, dtype=F32):
    return jax.ShapeDtypeStruct(tuple(shape), dtype)


def _cparams(ngrid=0, vmem=None):
    return pltpu.CompilerParams(
        dimension_semantics=("arbitrary",) * ngrid if ngrid else None, vmem_limit_bytes=vmem
    )


def _rows(tm, c, col=0):
    return pl.BlockSpec((tm, c), lambda i: (i, col))


def _whole(shape):
    nd = len(shape)
    return pl.BlockSpec(tuple(shape), lambda *_: (0,) * nd, pipeline_mode=pl.Buffered(1))


def _acc(shape):
    nd = len(shape)
    return pl.BlockSpec(tuple(shape), lambda *_: (0,) * nd)


def _mm(a, b):
    return jnp.dot(a, b, preferred_element_type=F32)


def _mm_nt(a, b):
    return lax.dot_general(a, b, (((1,), (1,)), ((), ())), preferred_element_type=F32)


def _mm_tn(a, b):
    return lax.dot_general(a, b, (((0,), (0,)), ((), ())), preferred_element_type=F32)


def _mm_split(k01, s):
    hi = s.astype(BF16)
    lo = (s - hi.astype(F32)).astype(BF16)
    return _mm(k01, hi) + _mm(k01, lo)


def _sigmoid(v):
    return 0.5 * (jnp.tanh(0.5 * v) + 1.0)


def _silu(v):
    return v * _sigmoid(v)


_GELU_K = 0.7978845608028654
_GELU_C = 0.044715


def _gelu(v):
    return 0.5 * v * (1.0 + jnp.tanh(_GELU_K * (v + _GELU_C * v * v * v)))


def _gelu_grad(v):
    th = jnp.tanh(_GELU_K * (v + _GELU_C * v * v * v))
    return 0.5 * (1.0 + th) + 0.5 * v * (1.0 - th * th) * (_GELU_K * (1.0 + 3.0 * _GELU_C * v * v))


def _colsum(v):
    return jnp.sum(v, axis=0, keepdims=True)


def _rowmean(v):
    return jnp.mean(v, axis=-1, keepdims=True)


NCHIP = NDEV // 2


def _my_pos():
    return lax.axis_index("x"), lax.axis_index("y"), lax.axis_index("c")


def _other_chips():
    x, y, _ = _my_pos()
    return [(1 - x, y), (x, 1 - y), (1 - x, 1 - y)]


def _remote(src, dst, send, recv, to):
    return pltpu.make_async_remote_copy(src, dst, send, recv, device_id=to, device_id_type=MESH_ID)


def _all_gather(arrs, name):
    n = len(arrs)

    def body(*refs):
        start, finish = _gather_copies(refs[:n], refs[n : 2 * n], *refs[2 * n :])
        start()
        finish()

    return pl.pallas_call(
        body, name=name, out_shape=_gather_shapes(arrs), in_specs=[_HBM] * n, out_specs=[_HBM] * n, scratch_shapes=_gather_sems(n)
    )(*arrs)


def _gather_shapes(arrs):
    return [_sds((NDEV,) + a.shape, a.dtype) for a in arrs]


def _gather_sems(n):
    return [pltpu.SemaphoreType.DMA((n, NDEV - 1)), pltpu.SemaphoreType.DMA((n, NDEV - 1)), pltpu.SemaphoreType.DMA((n,))]


def _gather_copies(ins, outs, send, recv, loc):
    n = len(ins)
    x, y, c = _my_pos()
    me, sib = (x, y, c), (x, y, 1 - c)
    chips = _other_chips()

    def slot(a, p):
        return outs[a].at[4 * p[0] + 2 * p[1] + p[2]]

    def copy(a, k, block, to, own=False):
        return _remote(ins[a] if own else slot(a, block), slot(a, block), send.at[a, k], recv.at[a, k], to)

    def mine():
        local = [pltpu.make_async_copy(ins[a], slot(a, me), loc.at[a]) for a in range(n)]
        sent = []
        for a in range(n):
            sent.append(copy(a, 0, me, sib, own=True))
            sent += [copy(a, 1 + j, me, (*chip, c), own=True) for j, chip in enumerate(chips)]
        return local, sent

    def start():
        local, sent = mine()
        for cp in local + sent:
            cp.start()

    def finish():
        local, sent = mine()
        for j, chip in enumerate(chips):
            for a in range(n):
                copy(a, 1 + j, (*chip, c), me).wait_recv()
                fwd = copy(a, 4 + j, (*chip, c), sib)
                fwd.start()
                sent.append(fwd)
        for a in range(n):
            copy(a, 0, sib, me).wait_recv()
            for j, chip in enumerate(chips):
                copy(a, 4 + j, (*chip, 1 - c), me).wait_recv()
        for cp in sent:
            cp.wait_send()
        for cp in local:
            cp.wait()

    return start, finish


def _sibling_swap(arrs, name):
    n = len(arrs)

    def body(*refs):
        start, finish = _swap_copies(refs[:n], refs[n : 2 * n], *refs[2 * n :])
        start()
        finish()

    return pl.pallas_call(
        body, name=name, out_shape=_swap_shapes(arrs), in_specs=[_HBM] * n, out_specs=[_HBM] * n, scratch_shapes=_swap_sems(n)
    )(*arrs)


def _swap_shapes(arrs):
    return [_sds((a.shape[0],) + a.shape[2:], a.dtype) for a in arrs]


def _swap_sems(n):
    return [pltpu.SemaphoreType.DMA((n,)), pltpu.SemaphoreType.DMA((n,))]


def _swap_copies(ins, outs, send, recv):
    x, y, c = _my_pos()

    def copies():
        return [_remote(ins[a].at[:, 1 - c], outs[a], send.at[a], recv.at[a], (x, y, 1 - c)) for a in range(len(ins))]

    def start():
        for cp in copies():
            cp.start()

    def finish():
        for cp in copies():
            cp.wait()

    return start, finish


def _pair_sum(arr, got, wire, name):
    _, _, r, c = arr.shape
    tr = _row_tile(r, c, budget=1 << 20)

    def body(a_ref, g_ref, o_ref):
        o_ref[0] = (a_ref[0, lax.axis_index("c")].astype(F32) + g_ref[0].astype(F32)).astype(wire)

    return pl.pallas_call(
        body,
        name=name,
        grid=(NCHIP, r // tr),
        in_specs=[pl.BlockSpec((1, 2, tr, c), lambda q, i: (q, 0, i, 0)), pl.BlockSpec((1, tr, c), lambda q, i: (q, i, 0))],
        out_specs=pl.BlockSpec((1, tr, c), lambda q, i: (q, i, 0)),
        out_shape=_sds((NCHIP, r, c), wire),
        compiler_params=_cparams(2),
    )(arr, got)


def _chip_exchange(arrs, name):
    n = len(arrs)

    def body(*refs):
        start, finish = _exchange_copies(refs[:n], refs[n : 2 * n], *refs[2 * n :])
        start()
        finish()

    return pl.pallas_call(
        body, name=name, out_shape=[_sds(a.shape, a.dtype) for a in arrs], in_specs=[_HBM] * n, out_specs=[_HBM] * n, scratch_shapes=_exchange_sems(n)
    )(*arrs)


def _exchange_sems(n):
    return [pltpu.SemaphoreType.DMA((n, NCHIP - 1)), pltpu.SemaphoreType.DMA((n, NCHIP - 1)), pltpu.SemaphoreType.DMA((n,))]


def _exchange_copies(ins, outs, send, recv, loc):
    n = len(ins)
    x, y, c = _my_pos()
    mine = 2 * x + y
    chips = _other_chips()

    def copies():
        local = [pltpu.make_async_copy(ins[a].at[mine], outs[a].at[mine], loc.at[a]) for a in range(n)]
        sent = [
            _remote(ins[a].at[2 * px + py], outs[a].at[mine], send.at[a, j], recv.at[a, j], (px, py, c))
            for a in range(n)
            for j, (px, py) in enumerate(chips)
        ]
        return local, sent

    def start():
        local, sent = copies()
        for cp in local + sent:
            cp.start()

    def finish():
        local, sent = copies()
        for a in range(n):
            for j, (px, py) in enumerate(chips):
                _remote(ins[a].at[mine], outs[a].at[2 * px + py], send.at[a, j], recv.at[a, j], (px, py, c)).wait_recv()
        for cp in sent:
            cp.wait_send()
        for cp in local:
            cp.wait()

    return start, finish


def _by_chip_and_core(a):
    return a.reshape((NCHIP, 2) + a.shape[1:])


def _pair_partials(arrs, wires, tag, got=None):
    four = [_by_chip_and_core(a) for a in arrs]
    if got is None:
        got = _sibling_swap(four, "swap_" + tag)
    return [_pair_sum(a, g, w, f"pair_sum_{tag}{i}") for i, (a, g, w) in enumerate(zip(four, got, wires))]


def _reduce_scatter(arrs, wires, tag):
    return _chip_exchange(_pair_partials(arrs, wires, tag), "exchange_" + tag)


def _row_tile(r, c, budget=1 << 20):
    best = r
    for t in range(8, r, 8):
        if r % t == 0 and t * c * 4 <= budget:
            best = t
    if r * c * 4 <= budget:
        best = r
    return best


def _sum_chips(land):
    tot = land[0].astype(F32)
    for q in range(1, NCHIP):
        tot = tot + land[q].astype(F32)
    return tot


def _adam_math(w, g, m, v):
    m2 = ADAM_B1 * m + (1.0 - ADAM_B1) * g
    v2 = ADAM_B2 * v + (1.0 - ADAM_B2) * (g * g)
    mh = m2 / (1.0 - ADAM_B1**ADAM_STEP)
    vh = v2 / (1.0 - ADAM_B2**ADAM_STEP)
    delta = -ADAM_LR * (mh / (jnp.sqrt(vh) + ADAM_EPS) + ADAM_WD * w)
    return delta, m2, v2


def _adam(g, w, m, v, name, land=None):
    r, c = w.shape
    tr = _row_tile(r, c, budget=1 << 20)

    def body(*refs):
        if land is not None:
            l_ref, w_ref, m_ref, v_ref, g_ref, d_ref, m2_ref, v2_ref = refs
            gv = _sum_chips(l_ref)
            g_ref[...] = gv
        else:
            g_in, w_ref, m_ref, v_ref, d_ref, m2_ref, v2_ref = refs
            gv = g_in[...]
        d, m2, v2 = _adam_math(w_ref[...], gv, m_ref[...], v_ref[...])
        d_ref[...] = d
        m2_ref[...] = m2
        v2_ref[...] = v2

    blk = _rows(tr, c)
    if land is not None:
        in_specs = [pl.BlockSpec((NCHIP, tr, c), lambda i: (0, i, 0)), blk, blk, blk]
        out = pl.pallas_call(
            body, name=name, grid=(r // tr,), in_specs=in_specs, out_specs=[blk] * 4, out_shape=[_sds((r, c))] * 4, compiler_params=_cparams(1)
        )(land, w, m, v)
        return out
    out = pl.pallas_call(
        body, name=name, grid=(r // tr,), in_specs=[blk] * 4, out_specs=[blk] * 3, out_shape=[_sds((r, c))] * 3, compiler_params=_cparams(1)
    )(g, w, m, v)
    return (g,) + tuple(out)


def _sum_slots(land, name):
    _, r, c = land.shape
    tr = _row_tile(r, c, budget=1 << 20)

    def body(l_ref, o_ref):
        o_ref[...] = _sum_chips(l_ref)

    return pl.pallas_call(
        body,
        name=name,
        grid=(r // tr,),
        in_specs=[pl.BlockSpec((NCHIP, tr, c), lambda i: (0, i, 0))],
        out_specs=_rows(tr, c),
        out_shape=_sds((r, c)),
        compiler_params=_cparams(1),
    )(land)


def _mod_fwd(call, ada_w, ada_b_loc):
    def body(c_ref, w_ref, b_ref, o_ref):
        s = _silu(c_ref[...]).astype(BF16)
        o_ref[...] = _mm(s, w_ref[...].astype(BF16)) + b_ref[...]

    return pl.pallas_call(
        body, name="mod_fwd", out_shape=_sds((call.shape[0], ada_w.shape[1])), compiler_params=_cparams(0, 32 << 20)
    )(call, ada_w, ada_b_loc)


def _mod_bwd(call, dm_loc, dmc_loc, ada_w):
    d, n = ada_w.shape
    pad = call.shape[0] - NDEV - 1

    def body(c_ref, dm_ref, dmc_ref, w_ref, gw_ref, gb_ref, cp_ref):
        cv = c_ref[...]
        sg = _sigmoid(cv)
        dmc = _colsum(dmc_ref[...])
        dm = dm_ref[...]
        rows = jnp.concatenate([dm, dmc, jnp.zeros((pad, n), F32)], axis=0)
        gw_ref[...] = _mm_tn((cv * sg).astype(BF16), rows.astype(BF16))
        gb_ref[...] = _colsum(dm) + dmc
        back = _mm_nt(rows[NDEV:].astype(BF16), w_ref[...].astype(BF16))
        c8, s8 = cv[NDEV : NDEV + 1], sg[NDEV : NDEV + 1]
        cp_ref[...] = back[0:1] * (s8 * (1.0 + c8 * (1.0 - s8)))

    return pl.pallas_call(
        body, name="mod_bwd", out_shape=[_sds((d, n)), _sds((1, n)), _sds((1, d))], compiler_params=_cparams(0, 40 << 20)
    )(call, dm_loc, dmc_loc, ada_w)


def _riding(body, nin, nout, nr, grid, copies, nscratch=0):
    def wrapped(*refs):
        base = nin + 2 * nr + nout
        start, finish = copies(refs[nin : nin + nr], refs[nin + nr + nout : base], *refs[base + nscratch :])
        first, last = True, True
        for ax, size in enumerate(grid):
            first = jnp.logical_and(first, pl.program_id(ax) == 0)
            last = jnp.logical_and(last, pl.program_id(ax) == size - 1)
        pl.when(first)(start)
        body(*refs[:nin], *refs[nin + nr : nin + nr + nout], *refs[base : base + nscratch])
        pl.when(last)(finish)

    return wrapped if nr else body


def _in_proj(xr, norm_g, scale, shift, wg, jsel, tm, name, shards=()):
    lx, d = xr.shape
    cb = wg.shape[2]
    nr = len(shards)

    def body(x_ref, g_ref, sc_ref, sh_ref, w_ref, p_ref, h_ref):
        xv = x_ref[...]
        r = lax.rsqrt(_rowmean(xv * xv) + EPS)
        hb = ((xv * r) * g_ref[...] * (1.0 + sc_ref[...]) + sh_ref[...]).astype(BF16)
        h_ref[...] = hb
        for q, j in enumerate(jsel):
            p_ref[:, q * cb : (q + 1) * cb] = _mm(hb, w_ref[j])

    vec = _whole((1, d))
    out = pl.pallas_call(
        _riding(body, 5, 2, nr, (lx // tm,), _gather_copies),
        name=name,
        grid=(lx // tm,),
        in_specs=[_rows(tm, d), vec, vec, vec, _whole(wg.shape)] + [_HBM] * nr,
        out_specs=[_rows(tm, len(jsel) * cb), _rows(tm, d)] + [_HBM] * nr,
        out_shape=[_sds((lx, len(jsel) * cb)), _sds((lx, d), BF16)] + _gather_shapes(shards),
        scratch_shapes=_gather_sems(nr) if nr else [],
        compiler_params=_cparams(1, VMEM_BIG),
    )(xr, norm_g, scale, shift, wg, *shards)
    return out[:2], out[2:]


def _in_bwd(xr, dxo, dup, du5, dypre, dz, wg, norm_g, scale, shift, d_skip, dg_init, tm):
    lx, d = xr.shape
    cb = wg.shape[2]
    pw = dup.shape[1]
    sw = du5.shape[1]
    mix = dz.shape[1]
    ncol = NDEV * cb

    def body(x_ref, dxo_ref, up_ref, d5_ref, dy_ref, dz_ref, w_ref, g_ref, sc_ref, sh_ref, dk_ref, gi_ref,
             gx_ref, dp_ref, dsc_ref, dsh_ref, dg_ref):
        i = pl.program_id(0)

        @pl.when(i == 0)
        def _():
            dsc_ref[...] = jnp.zeros_like(dsc_ref)
            dsh_ref[...] = jnp.zeros_like(dsh_ref)
            dg_ref[...] = gi_ref[...]

        dp = jnp.concatenate(
            [up_ref[...], d5_ref[...] + dk_ref[...] * dy_ref[...], dz_ref[...]], axis=1
        ).astype(BF16)
        dp_ref[...] = dp
        dh = _mm_nt(dp[:, 0:cb], w_ref[0])
        for j in range(1, NDEV):
            dh = dh + _mm_nt(dp[:, j * cb : (j + 1) * cb], w_ref[j])
        xv = x_ref[...]
        r = lax.rsqrt(_rowmean(xv * xv) + EPS)
        xh = xv * r
        g = g_ref[...]
        one_sc = 1.0 + sc_ref[...]
        dsh_ref[...] += _colsum(dh)
        dsc_ref[...] += _colsum(dh * (xh * g))
        dg_ref[...] += _colsum(dh * one_sc * xh)
        dxh = dh * one_sc * g
        gx_ref[...] = r * (dxh - xh * _rowmean(dxh * xh)) + dxo_ref[...]

    vec = _whole((1, d))
    return pl.pallas_call(
        body,
        name="in_bwd",
        grid=(lx // tm,),
        in_specs=[_rows(tm, d), _rows(tm, d), _rows(tm, pw), _rows(tm, sw), _rows(tm, sw), _rows(tm, mix), _whole(wg.shape), vec, vec, vec, _whole((1, sw)), vec],
        out_specs=[_rows(tm, d), _rows(tm, ncol), _acc((1, d)), _acc((1, d)), _acc((1, d))],
        out_shape=[_sds((lx, d)), _sds((lx, ncol), BF16), _sds((1, d)), _sds((1, d)), _sds((1, d))],
        compiler_params=_cparams(1, VMEM_BIG),
    )(xr, dxo, dup, du5, dypre, dz, wg, norm_g, scale, shift, d_skip, dg_init)


def _in_bwd_ctx(xc, duc, wg, jsel, norm_g, scale, shift):
    lc, d = xc.shape
    cb = wg.shape[2]

    def body(x_ref, du_ref, w_ref, g_ref, sc_ref, sh_ref, dp_ref, dsc_ref, dsh_ref, dg_ref):
        dp = du_ref[...].astype(BF16)
        dp_ref[...] = dp
        dh = _mm_nt(dp[:, 0:cb], w_ref[jsel[0]])
        for q in range(1, len(jsel)):
            dh = dh + _mm_nt(dp[:, q * cb : (q + 1) * cb], w_ref[jsel[q]])
        xv = x_ref[...]
        xh = xv * lax.rsqrt(_rowmean(xv * xv) + EPS)
        dsh_ref[...] = _colsum(dh)
        dsc_ref[...] = _colsum(dh * (xh * g_ref[...]))
        dg_ref[...] = _colsum(dh * (1.0 + sc_ref[...]) * xh)

    return pl.pallas_call(
        body,
        name="in_bwd_ctx",
        out_shape=[_sds(duc.shape, BF16), _sds((1, d)), _sds((1, d)), _sds((1, d))],
        compiler_params=_cparams(0, VMEM_BIG),
    )(xc, duc, wg, norm_g, scale, shift)


def _in_w_grad(hb, dpb, hcb, dpcb, cb, tm, parts=()):
    lx, d = hb.shape
    lc = hcb.shape[0]
    nb = lx // tm
    cg = 2 * cb
    nr = len(parts)
    grid = (NDEV // 2, nb + 1)

    def body(h_ref, dp_ref, hc_ref, dpc_ref, o_ref, acc):
        j, i = pl.program_id(0), pl.program_id(1)

        @pl.when(i == 0)
        def _():
            acc[...] = jnp.zeros_like(acc)

        @pl.when(i < nb)
        def _():
            h = h_ref[...]
            acc[0] += _mm_tn(h, dp_ref[:, 0:cb])
            acc[1] += _mm_tn(h, dp_ref[:, cb:cg])

        @pl.when(jnp.logical_and(i == nb, j == 1))
        def _():
            h = hc_ref[...]
            acc[0] += _mm_tn(h, dpc_ref[:, 0:cb])
            acc[1] += _mm_tn(h, dpc_ref[:, cb:cg])

        @pl.when(i == nb)
        def _():
            o_ref[...] = acc[...].astype(o_ref.dtype)

    out = pl.pallas_call(
        _riding(body, 4, 1, nr, grid, _exchange_copies, nscratch=1),
        name="in_w_grad",
        grid=grid,
        in_specs=[
            pl.BlockSpec((tm, d), lambda j, i: (jnp.minimum(i, nb - 1), 0)),
            pl.BlockSpec((tm, cg), lambda j, i: (jnp.minimum(i, nb - 1), j)),
            pl.BlockSpec((lc, d), lambda j, i: (0, 0)),
            pl.BlockSpec((lc, cg), lambda j, i: (0, 0)),
        ]
        + [_HBM] * nr,
        out_specs=[pl.BlockSpec((2, d, cb), lambda j, i: (j, 0, 0))] + [_HBM] * nr,
        out_shape=[_sds((NDEV, d, cb), BF16)] + [_sds(p.shape, p.dtype) for p in parts],
        scratch_shapes=[pltpu.VMEM((2, d, cb), F32)] + (_exchange_sems(nr) if nr else []),
        compiler_params=_cparams(2, VMEM_BIG),
    )(hb, dpb, hcb, dpcb, *parts)
    return out[0], out[1:]


def _pool_tables(w, rows, rb, pgw, transpose):
    t = np.arange(GRID_W)
    lo, hi = np.clip(t - w // 2, 0, GRID_W), np.clip(t + w - w // 2, 0, GRID_W)
    band = ((t[None, :] >= lo[:, None]) & (t[None, :] < hi[:, None])).astype(np.float32)
    if transpose:
        band = band.T
    kc = np.kron(np.eye(rb, dtype=np.float32), band)
    inv_c = np.tile((1.0 / (hi - lo).astype(np.float32))[:, None], (rb, pgw)).astype(np.float32)
    r = np.arange(rows)
    cnt_r = np.clip(r + w - w // 2, 0, rows) - np.clip(r - w // 2, 0, rows)
    inv_r = (1.0 / cnt_r.astype(np.float32)).astype(np.float32)
    return jnp.asarray(kc, BF16), jnp.asarray(inv_r), jnp.asarray(inv_c)


def _pool_stack(rows, rb, pgw, transpose):
    tabs = [_pool_tables(w, rows, rb, pgw, transpose) for w in POOL_WINDOWS]
    return tuple(jnp.stack([t[i] for t in tabs]) for i in range(3))


def _window_sums(ref, r0, rb, w):
    slab = lambda off: ref[pl.ds(pl.multiple_of((r0 + off) * GRID_W, GRID_W), GRID_W), :]
    shared = range(rb - 1, w)
    common = None
    for off in shared:
        common = slab(off) if common is None else common + slab(off)
    sums = []
    for rr in range(rb):
        s = common
        for off in range(rr, rr + w):
            if off not in shared:
                s = slab(off) if s is None else s + slab(off)
        sums.append(s)
    return sums


def _per_window(group):
    for k, w in enumerate(POOL_WINDOWS):
        pl.when(pl.program_id(0) == k)(lambda k=k, w=w: group(k, w))


def _pool_fwd(proj, pool_wf):
    lx = proj.shape[0]
    ng, pgw, _ = pool_wf.shape
    rows = lx // GRID_W
    rb = min(4, rows)
    tok = rb * GRID_W
    kc, inv_r, inv_c = _pool_stack(rows, rb, pgw, False)

    def body(p_ref, w_ref, kc_ref, ir_ref, ic_ref, lin_ref, dm_ref, xp):
        def group(kk, w):
            lo = w // 2
            xp[pl.ds(0, lo * GRID_W), :] = jnp.zeros((lo * GRID_W, pgw), F32)
            xp[pl.ds((lo + rows) * GRID_W, (w - lo) * GRID_W), :] = jnp.zeros(((w - lo) * GRID_W, pgw), F32)
            xp[pl.ds(lo * GRID_W, lx), :] = p_ref[...]

            def blk(b, carry):
                r0 = b * rb
                parts = [s * ir_ref[kk, r0 + rr] for rr, s in enumerate(_window_sums(xp, r0, rb, w))]
                m = _mm_split(kc_ref[0], jnp.concatenate(parts, axis=0)) * ic_ref[0]
                u = xp[pl.ds(pl.multiple_of((r0 + lo) * GRID_W, GRID_W), tok), :]
                db = (m - u).astype(BF16)
                o0 = pl.multiple_of(r0 * GRID_W, GRID_W)
                dm_ref[pl.ds(o0, tok), :] = db
                lin_ref[pl.ds(o0, tok), :] = _mm(db, w_ref[0])
                return carry

            lax.fori_loop(0, rows // rb, blk, 0)

        _per_window(group)

    col = pl.BlockSpec((lx, pgw), lambda g: (0, g))
    return pl.pallas_call(
        body,
        name="pool_fwd",
        grid=(ng,),
        in_specs=[col, _gspec(1, pgw, pgw), _gspec(1, tok, tok), _SMEM, _gspec(1, tok, pgw)],
        out_specs=[col, col],
        out_shape=[_sds((lx, ng * pgw)), _sds((lx, ng * pgw), BF16)],
        scratch_shapes=[pltpu.VMEM(((rows + max(POOL_WINDOWS)) * GRID_W, pgw), F32)],
        compiler_params=_cparams(1, VMEM_BIG),
    )(proj, pool_wf, kc, inv_r, inv_c)


def _pool_bwd(dlin, pool_wf, dmat):
    lx = dlin.shape[0]
    ng, pgw, _ = pool_wf.shape
    rows = lx // GRID_W
    rb = min(4, rows)
    tok = rb * GRID_W
    kct, inv_r, inv_c = _pool_stack(rows, rb, pgw, True)

    def body(dl_ref, w_ref, kc_ref, ir_ref, ic_ref, dm_ref, du_ref, dw_ref, tp):
        def group(kk, w):
            front = w - w // 2 - 1
            if front:
                tp[pl.ds(0, front * GRID_W), :] = jnp.zeros((front * GRID_W, pgw), F32)
            tp[pl.ds((front + rows) * GRID_W, (w - front) * GRID_W), :] = jnp.zeros(((w - front) * GRID_W, pgw), F32)
            dw_ref[...] = jnp.zeros_like(dw_ref)

            def blk(b, carry):
                o0 = pl.multiple_of(b * tok, GRID_W)
                dlb = dl_ref[pl.ds(o0, tok), :].astype(BF16)
                dw_ref[0] += _mm_tn(dm_ref[pl.ds(o0, tok), :], dlb)
                dd = _mm_nt(dlb, w_ref[0])
                du_ref[pl.ds(o0, tok), :] = -dd
                t = _mm(kc_ref[0], (dd * ic_ref[0]).astype(BF16))
                for rr in range(rb):
                    dst = pl.multiple_of((b * rb + rr + front) * GRID_W, GRID_W)
                    tp[pl.ds(dst, GRID_W), :] = t[rr * GRID_W : (rr + 1) * GRID_W] * ir_ref[kk, b * rb + rr]
                return carry

            lax.fori_loop(0, rows // rb, blk, 0)

            def rowl(b, carry):
                for rr, s in enumerate(_window_sums(tp, b * rb, rb, w)):
                    du_ref[pl.ds(pl.multiple_of((b * rb + rr) * GRID_W, GRID_W), GRID_W), :] += s
                return carry

            lax.fori_loop(0, rows // rb, rowl, 0)

        _per_window(group)

    col = pl.BlockSpec((lx, pgw), lambda g: (0, g))
    return pl.pallas_call(
        body,
        name="pool_bwd",
        grid=(ng,),
        in_specs=[col, _gspec(1, pgw, pgw), _gspec(1, tok, tok), _SMEM, _gspec(1, tok, pgw), col],
        out_specs=[col, _gspec(1, pgw, pgw)],
        out_shape=[_sds((lx, ng * pgw)), _sds((ng, pgw, pgw))],
        scratch_shapes=[pltpu.VMEM(((rows + max(POOL_WINDOWS)) * GRID_W, pgw), F32)],
        compiler_params=_cparams(1, VMEM_BIG),
    )(dlin, pool_wf, kct, inv_r, inv_c, dmat)


def _s5_small(lam_re, lam_im, log_dt, b_re, b_im, c_re, c_im):
    t = CHUNK_T
    dt = jnp.exp(log_dt)[..., None]
    zr, zi = lam_re * dt, lam_im * dt
    tau = jnp.arange(t + 1, dtype=F32)
    mag = jnp.exp(zr[..., None] * tau)
    pr, pi = mag * jnp.cos(zi[..., None] * tau), mag * jnp.sin(zi[..., None] * tau)
    ar, ai = pr[..., 1], pi[..., 1]
    den = lam_re * lam_re + lam_im * lam_im
    qr = ((ar - 1.0) * lam_re + ai * lam_im) / den
    qi = (ai * lam_re - (ar - 1.0) * lam_im) / den
    bbr = qr[..., None] * b_re - qi[..., None] * b_im
    bbi = qr[..., None] * b_im + qi[..., None] * b_re
    ctr, cti = jnp.swapaxes(c_re, 2, 3), jnp.swapaxes(c_im, 2, 3)
    lane_pad = lambda v: jnp.pad(jnp.swapaxes(v, 0, 1), ((0, 0), (0, 0), (0, 0), (0, LANES - v.shape[-1])))
    tp = lambda v: jnp.swapaxes(v, 1, 2)
    flip = lambda v: v[..., ::-1]
    pfr, pfi = tp(flip(pr[0, ..., :t])), tp(flip(pi[0, ..., :t]))
    pbr, pbi = tp(pr[1, ..., :t]), tp(pi[1, ..., :t])
    bfr, bfi, bbr_t, bbi_t = tp(bbr[0]), tp(bbi[0]), tp(bbr[1]), tp(bbi[1])
    a1 = jnp.concatenate([pfr, pbr, pfr, pbr], axis=-1)
    a2 = jnp.concatenate([-pfi, -pbi, pfi, pbi], axis=-1)
    b1 = jnp.concatenate([bfr, bbr_t, bfi, bbi_t], axis=-1)
    b2 = jnp.concatenate([bfi, bbi_t, bfr, bbr_t], axis=-1)
    a16 = jnp.concatenate([pr[0, ..., t], pr[1, ..., t], pi[0, ..., t], pi[1, ..., t]], axis=1)
    return (lane_pad(pr), lane_pad(pi), lane_pad(bbr), lane_pad(bbi), lane_pad(ctr), lane_pad(cti), a1, a2, b1, b2), a16


def _split2(v):
    hi = v.astype(BF16)
    return hi, (v - hi.astype(F32)).astype(BF16)


def _dotx(a, b, mm=_mm):
    ah, al = (a, None) if a.dtype == BF16 else _split2(a)
    bh, bl = (b, None) if b.dtype == BF16 else _split2(b)
    out = mm(ah, bh)
    if bl is not None:
        out = out + mm(ah, bl)
    if al is not None:
        out = out + mm(al, bh)
    return out


def _shift_lanes(v, k):
    if k == 0:
        return v
    n = v.shape[-1]
    lane = lax.broadcasted_iota(jnp.int32, v.shape, v.ndim - 1)
    r = pltpu.roll(v, k % n, axis=v.ndim - 1)
    return jnp.where(lane >= k, r, 0.0) if k > 0 else jnp.where(lane < n + k, r, 0.0)


def _op_consts():
    t, h = CHUNK_T, SSM_H
    j, o = np.divmod(np.arange(t * h), h)
    p = np.arange(LANES)[:, None]
    sel = lambda col: (p == col[None, :]).astype(np.float32)
    pw = np.stack([np.concatenate([sel(j), sel(j + 1)], axis=1), np.concatenate([sel(t - 1 - j), sel(t - j)], axis=1)])
    return jnp.asarray(pw, BF16), jnp.asarray(sel(o), BF16), jnp.asarray(sel(j).T, BF16), jnp.asarray(sel(o).T, BF16)


def _op_factors(gi, pr_ref, pi_ref, br_ref, bi_ref, cr_ref, ci_ref, pw_ref, ch_ref):
    n = pr_ref.shape[2]
    c16 = _dotx(jnp.concatenate([cr_ref[gi, 0], ci_ref[gi, 0], cr_ref[gi, 1], ci_ref[gi, 1]], axis=0), ch_ref[...])
    out = []
    for d in range(2):
        e = _dotx(jnp.concatenate([pr_ref[gi, d], pi_ref[gi, d]], axis=0), pw_ref[d])
        c = c16[2 * d * n : (2 * d + 1) * n], c16[(2 * d + 1) * n : (2 * d + 2) * n]
        bst = jnp.concatenate([br_ref[gi, d], -bi_ref[gi, d]], axis=0)
        out.append(((e[:n, :CW], e[n:, :CW]), (e[:n, CW:], e[n:, CW:]), c, bst))
    return out


def _op_spread(m_ref, u_ref, v_ref, gi):
    return _dotx(m_ref[...], jnp.concatenate([_rows_pad(u_ref[gi]), _rows_pad(v_ref[gi])], axis=1))


def _cmul(x, y):
    return x[0] * y[0] - x[1] * y[1], x[0] * y[1] + x[1] * y[0]


def _cmul_conj(g, y):
    return g[0] * y[0] + g[1] * y[1], g[1] * y[0] - g[0] * y[1]


def _rows_pad(v):
    return jnp.concatenate([v, jnp.zeros((LANES - v.shape[0], v.shape[1]), F32)], axis=0)


def _s5_build(small, gb, shards):
    pr, pi, br, bi, cr, ci, a1, a2, b1, b2 = small
    g, _, n, _ = pr.shape
    t, h, ns = CHUNK_T, SSM_H, 4 * n
    pw, ch, rep_s, til_s = _op_consts()
    nr, nin, steps = len(shards), 14, g // gb

    def body(*refs):
        pr_ref, pi_ref, br_ref, bi_ref, cr_ref, ci_ref, a1_ref, a2_ref, b1_ref, b2_ref, pw_ref, ch_ref, rs_ref, ts_ref = refs[:nin]
        t_ref, w_ref, c_ref = refs[nin + nr : nin + nr + 3]
        start, finish = _gather_copies(refs[nin : nin + nr], refs[nin + nr + 3 : nin + 2 * nr + 3], *refs[nin + 2 * nr + 3 :])
        pl.when(pl.program_id(0) == 0)(start)

        def group(gi, carry):
            rows, outs = [], []
            for e16, e1, c16, bst in _op_factors(gi, pr_ref, pi_ref, br_ref, bi_ref, cr_ref, ci_ref, pw_ref, ch_ref):
                rows.append(_dotx(bst, jnp.concatenate(_cmul(c16, e16), axis=0), _mm_tn)[:h])
                outs.append(_cmul(c16, e1))
            c_ref[gi] = jnp.concatenate([outs[0][0], outs[1][0], -outs[0][1], -outs[1][1]], axis=0).astype(BF16)
            for s in range(t):
                t_ref[gi, s * h : (s + 1) * h, :] = (_shift_lanes(rows[0], h * s) + _shift_lanes(rows[1], -h * (t - 1 - s))).astype(BF16)
            ae, be = _op_spread(rs_ref, a1_ref, a2_ref, gi), _op_spread(ts_ref, b1_ref, b2_ref, gi)
            w_ref[gi] = (ae[:, :ns] * be[:, :ns] + ae[:, ns:] * be[:, ns:]).astype(BF16)
            return carry

        lax.fori_loop(0, gb, group, 0)
        pl.when(pl.program_id(0) == steps - 1)(finish)

    sm = pl.BlockSpec((gb, 2, n, LANES), lambda i: (i, 0, 0, 0))
    ab = lambda v: pl.BlockSpec((gb,) + v.shape[1:], lambda i: (i, 0, 0))
    out = pl.pallas_call(
        body,
        name="s5_build",
        grid=(steps,),
        in_specs=[sm] * 6 + [ab(a1), ab(a2), ab(b1), ab(b2), _whole(pw.shape), _whole(ch.shape), _whole(rep_s.shape), _whole(til_s.shape)] + [_HBM] * nr,
        out_specs=[_gspec(gb, CW, CW), _gspec(gb, CW, 4 * n), _gspec(gb, 4 * n, CW)] + [_HBM] * nr,
        out_shape=[_sds((g, CW, CW), BF16), _sds((g, CW, 4 * n), BF16), _sds((g, 4 * n, CW), BF16)] + _gather_shapes(shards),
        scratch_shapes=_gather_sems(nr),
        compiler_params=_cparams(1, VMEM_BIG),
    )(pr, pi, br, bi, cr, ci, a1, a2, b1, b2, pw, ch, rep_s, til_s, *shards)
    return out[:3], out[3:]


def _s5_build_bwd(small, dtsum, dwst, dccat, gb, parts):
    pr, pi, br, bi, cr, ci, a1, a2, b1, b2 = small
    g, _, n, _ = pr.shape
    t, h, ns = CHUNK_T, SSM_H, 4 * n
    pw, ch, rep_s, til_s = _op_consts()
    nr, nin, nout, steps = len(parts), 17, 10, g // gb

    def body(*refs):
        pr_ref, pi_ref, br_ref, bi_ref, cr_ref, ci_ref, a1_ref, a2_ref, b1_ref, b2_ref, pw_ref, ch_ref, rs_ref, ts_ref, dt_ref, dw_ref, dc_ref = refs[:nin]
        dpr_ref, dpi_ref, dbr_ref, dbi_ref, dcr_ref, dci_ref, da1_ref, da2_ref, db1_ref, db2_ref = refs[nin + nr : nin + nr + nout]
        start, finish = _exchange_copies(refs[nin : nin + nr], refs[nin + nr + nout : nin + 2 * nr + nout], *refs[nin + 2 * nr + nout :])
        pl.when(pl.program_id(0) == 0)(start)

        def group(gi, carry):
            drow = [jnp.zeros((h, CW), F32), jnp.zeros((h, CW), F32)]
            for s in range(t):
                blk = dt_ref[gi, s * h : (s + 1) * h, :]
                drow[0] = drow[0] + _shift_lanes(blk, -h * s)
                drow[1] = drow[1] + _shift_lanes(blk, h * (t - 1 - s))
            dcc = dc_ref[gi]
            dcs = []
            for d, (e16, e1, c16, bst) in enumerate(_op_factors(gi, pr_ref, pi_ref, br_ref, bi_ref, cr_ref, ci_ref, pw_ref, ch_ref)):
                dr = _rows_pad(drow[d])
                db = _dotx(jnp.concatenate(_cmul(c16, e16), axis=0), dr, _mm_nt)
                dbr_ref[gi, d] = db[:n]
                dbi_ref[gi, d] = -db[n:]
                dst = _dotx(bst, dr)
                dca = dst[:n], dst[n:]
                gx = dcc[d * n : (d + 1) * n], -dcc[(2 + d) * n : (3 + d) * n]
                dc16, dc16b = _cmul_conj(dca, e16), _cmul_conj(gx, e1)
                dcs += [dc16[0] + dc16b[0], dc16[1] + dc16b[1]]
                de = jnp.concatenate([jnp.concatenate(_cmul_conj(dca, c16), axis=0), jnp.concatenate(_cmul_conj(gx, c16), axis=0)], axis=1)
                dp = _dotx(de, pw_ref[d], _mm_nt)
                dpr_ref[gi, d] = dp[:n]
                dpi_ref[gi, d] = dp[n:]
            dct = _dotx(jnp.concatenate(dcs, axis=0), ch_ref[...], _mm_nt)
            for d in range(2):
                dcr_ref[gi, d] = dct[2 * d * n : (2 * d + 1) * n]
                dci_ref[gi, d] = dct[(2 * d + 1) * n : (2 * d + 2) * n]
            dw = dw_ref[gi]
            ae, be = _op_spread(rs_ref, a1_ref, a2_ref, gi), _op_spread(ts_ref, b1_ref, b2_ref, gi)
            da = _dotx(rs_ref[...], jnp.concatenate([dw * be[:, :ns], dw * be[:, ns:]], axis=1), _mm_tn)
            db = _dotx(ts_ref[...], jnp.concatenate([dw * ae[:, :ns], dw * ae[:, ns:]], axis=1), _mm_tn)
            da1_ref[gi] = da[:t, :ns]
            da2_ref[gi] = da[:t, ns:]
            db1_ref[gi] = db[:h, :ns]
            db2_ref[gi] = db[:h, ns:]
            return carry

        lax.fori_loop(0, gb, group, 0)
        pl.when(pl.program_id(0) == steps - 1)(finish)

    sm = pl.BlockSpec((gb, 2, n, LANES), lambda i: (i, 0, 0, 0))
    ab = lambda v: pl.BlockSpec((gb,) + v.shape[1:], lambda i: (i, 0, 0))
    ops = [_gspec(gb, CW, CW), _gspec(gb, CW, 4 * n), _gspec(gb, 4 * n, CW)]
    out = pl.pallas_call(
        body,
        name="s5_build_bwd",
        grid=(steps,),
        in_specs=[sm] * 6 + [ab(a1), ab(a2), ab(b1), ab(b2), _whole(pw.shape), _whole(ch.shape), _whole(rep_s.shape), _whole(til_s.shape)] + ops + [_HBM] * nr,
        out_specs=[sm] * 6 + [ab(a1), ab(a2), ab(b1), ab(b2)] + [_HBM] * nr,
        out_shape=[_sds(v.shape) for v in small] + [_sds(p.shape, p.dtype) for p in parts],
        scratch_shapes=_exchange_sems(nr),
        compiler_params=_cparams(1, VMEM_BIG),
    )(pr, pi, br, bi, cr, ci, a1, a2, b1, b2, pw, ch, rep_s, til_s, dtsum, dwst, dccat, *parts)
    return out[:nout], out[nout:]


SCAN_UNROLL = 4
GPL = LANES // SSM_H
CW = CHUNK_T * SSM_H
BW = CHUNK_T * LANES


def _chunk_perm():
    o = np.arange(BW)
    src = ((o % CW) // SSM_H) * LANES + (o // CW) * SSM_H + o % SSM_H
    return jnp.asarray(np.arange(BW)[:, None] == src[None, :], BF16)


def _gspec(gb, a, b):
    return pl.BlockSpec((gb, a, b), lambda i: (i, 0, 0))


def _chunk_rows(ref, n):
    return jnp.concatenate([ref[pl.ds(s, n, stride=CHUNK_T), :] for s in range(CHUNK_T)], axis=1)


def _s5_state(proj, col0, uc, perm, wst, ncc, ncl):
    nlb = uc.shape[1] // LANES
    g, _, ns = wst.shape
    nch = ncc + ncl
    half = ns // 2

    def body(p_ref, c_ref, pm_ref, w_ref, u_ref, sr_ref, si_ref):
        xrows = jnp.concatenate([_chunk_rows(c_ref, ncc), _chunk_rows(p_ref, ncl)], axis=0).astype(BF16)
        u = _mm(xrows, pm_ref[...]).astype(BF16)
        u_ref[0] = u
        for gi in range(GPL):
            s = _mm(u[:, gi * CW : (gi + 1) * CW], w_ref[gi].astype(BF16))
            sr_ref[gi] = s[:, :half]
            si_ref[gi] = s[:, half:]

    return pl.pallas_call(
        body,
        name="s5_state",
        grid=(nlb,),
        in_specs=[
            pl.BlockSpec((ncl * CHUNK_T, LANES), lambda i: (0, col0 + i)),
            pl.BlockSpec((ncc * CHUNK_T, LANES), lambda i: (0, i)),
            _whole(perm.shape),
            _gspec(GPL, CW, ns),
        ],
        out_specs=[pl.BlockSpec((1, nch, BW), lambda i: (i, 0, 0)), _gspec(GPL, nch, half), _gspec(GPL, nch, half)],
        out_shape=[_sds((nlb, nch, BW), BF16), _sds((g, nch, half)), _sds((g, nch, half))],
        compiler_params=_cparams(1, VMEM_BIG),
    )(proj, uc, perm, wst)


def _unrolled_loop(n, step, init):
    u = SCAN_UNROLL if n % SCAN_UNROLL == 0 else 1

    def trip(i, c):
        for k in range(u):
            c = step(i * u + k, c)
        return c

    return lax.fori_loop(0, n // u, trip, init)


def _lane_masks(gb, half):
    lane = lax.broadcasted_iota(jnp.int32, (gb, half), 1)
    return lane < (half // 2)


def _roll_half(v):
    return pltpu.roll(v, v.shape[-1] // 2, axis=v.ndim - 1)


def _pack_dirs(lo, hi, mf):
    return jnp.where(mf, lo, _roll_half(hi)), jnp.where(mf, _roll_half(lo), hi)


def _unpack_dirs(pf, pb, mf):
    return jnp.where(mf, pf, _roll_half(pb)), jnp.where(mf, _roll_half(pf), pb)


def _packed_spec(gb, nch, half, buffers=None):
    mode = {} if buffers is None else dict(pipeline_mode=pl.Buffered(buffers))
    if gb == 2 * GPL:
        return pl.BlockSpec((GPL, nch, half), lambda i: (i // 2, 0, 0), **mode)
    assert gb == GPL
    return pl.BlockSpec((gb // 2, nch, half), lambda i: (i, 0, 0), **mode)


def _pair_index(j, gb):
    if gb == 2 * GPL:
        return j, pl.program_id(0) % 2
    return j % (gb // 2), j // (gb // 2)


def _pick(which, lo, hi):
    return (hi if which else lo) if isinstance(which, int) else jnp.where(which == 0, lo, hi)


def _unpacked(pf_ref, pb_ref, j, gb, mf):
    jl, which = _pair_index(j, gb)
    return _pick(which, *_unpack_dirs(pf_ref[jl], pb_ref[jl], mf))


def _pair_tile(refs, jl, rows):
    return jnp.concatenate([r[jl, rows, :] for r in refs], axis=1)


def _pair_rows_of(op, which, q):
    z = jnp.zeros((q, op.shape[1]), op.dtype)
    blocks = [op[0:q], op[2 * q : 3 * q], op[q : 2 * q], op[3 * q : 4 * q]]
    lo = jnp.concatenate([x for b in blocks for x in (b, z)], axis=0)
    hi = jnp.concatenate([x for b in blocks for x in (z, b)], axis=0)
    return _pick(which, lo, hi)


def _group_rows_of(pair_rows, which, q):
    def take(off):
        blk = lambda k: pair_rows[2 * q * k + off : 2 * q * k + off + q]
        return jnp.concatenate([blk(0), blk(2), blk(1), blk(3)], axis=0)

    return _pick(which, take(0), take(q))


def _scan_rows(ref, row, hp, nch, mf):
    return _pack_dirs(ref[pl.ds(row, hp, stride=nch), :], ref[pl.ds(hp * nch + row, hp, stride=nch), :], mf)


def _s5_scan_fwd(s_re, s_im, a16, ncc, nch, gb):
    g, ns = a16.shape
    half = ns // 2
    hp = gb // 2

    def body(sr_ref, si_ref, a_ref, hfr_ref, hfi_ref, hbr_ref, hbi_ref):
        mf = _lane_masks(hp, half)
        afr, abr = _pack_dirs(a_ref[:hp, :half], a_ref[hp:, :half], mf)
        afi, abi = _pack_dirs(a_ref[:hp, half:], a_ref[hp:, half:], mf)
        zero = jnp.zeros((hp, half), F32)

        def step(t, c):
            fr, fi, br, bi = c
            rb = jnp.where(t < ncc, ncc - 1 - t, nch - 1 - (t - ncc))
            hfr_ref[pl.ds(t, hp, stride=nch), :] = fr
            hfi_ref[pl.ds(t, hp, stride=nch), :] = fi
            hbr_ref[pl.ds(rb, hp, stride=nch), :] = br
            hbi_ref[pl.ds(rb, hp, stride=nch), :] = bi
            sfr, sfi = _scan_rows(sr_ref, t, hp, nch, mf)[0], _scan_rows(si_ref, t, hp, nch, mf)[0]
            sbr, sbi = _scan_rows(sr_ref, rb, hp, nch, mf)[1], _scan_rows(si_ref, rb, hp, nch, mf)[1]
            return (afr * fr - afi * fi + sfr, afr * fi + afi * fr + sfi, abr * br - abi * bi + sbr, abr * bi + abi * br + sbi)

        _unrolled_loop(nch, step, (zero, zero, zero, zero))

    blk = pl.BlockSpec((gb * nch, half), lambda i: (i, 0))
    pblk = pl.BlockSpec((hp * nch, half), lambda i: (i, 0))
    return pl.pallas_call(
        body,
        name="s5_scan_fwd",
        grid=(g // gb,),
        in_specs=[blk, blk, pl.BlockSpec((gb, ns), lambda i: (i, 0))],
        out_specs=[pblk] * 4,
        out_shape=[_sds((g // 2 * nch, half))] * 4,
        compiler_params=_cparams(1, VMEM_BIG),
    )(s_re, s_im, a16)


def _token_rows_store(ref, val, n):
    for s in range(CHUNK_T):
        ref[pl.ds(s, n, stride=CHUNK_T), :] = val[:, s * LANES : (s + 1) * LANES]


def _s5_out(u_all, h4, tsum, ccat, perm, ncc, lx, gb):
    nlb, nch, _ = u_all.shape
    g, ns, _ = ccat.shape
    ncl = nch - ncc
    half = ns // 2

    def body(u_ref, hfr_ref, hfi_ref, hbr_ref, hbi_ref, t_ref, c_ref, pm_ref, y_ref):
        parts = []
        for gi in range(GPL):
            u = u_ref[0, ncc:, gi * CW : (gi + 1) * CW]
            jl, which = _pair_index(gi, gb)
            hs = _pair_tile((hfr_ref, hfi_ref, hbr_ref, hbi_ref), jl, slice(ncc, None)).astype(BF16)
            parts.append(_mm(u, t_ref[gi].astype(BF16)) + _mm(hs, _pair_rows_of(c_ref[gi].astype(BF16), which, ns // 4)))
        _token_rows_store(y_ref, _mm_nt(jnp.concatenate(parts, axis=1).astype(BF16), pm_ref[...]), ncl)

    return pl.pallas_call(
        body,
        name="s5_out",
        grid=(nlb,),
        in_specs=[pl.BlockSpec((1, nch, BW), lambda i: (i, 0, 0))] + [_packed_spec(gb, nch, half)] * 4 + [_gspec(GPL, CW, CW), _gspec(GPL, ns, CW), _whole(perm.shape)],
        out_specs=pl.BlockSpec((lx, LANES), lambda i: (0, i)),
        out_shape=_sds((lx, nlb * LANES)),
        compiler_params=_cparams(1, VMEM_BIG),
    )(u_all, *h4, tsum, ccat, perm)


def _s5_dstate(dypre, perm, ccat, ncc, swaps=()):
    lx, sw = dypre.shape
    nr = len(swaps)
    nlb = sw // LANES
    g, ns, _ = ccat.shape
    ncl = lx // CHUNK_T
    nch = ncl + ncc
    half = ns // 2

    def body(dy_ref, pm_ref, c_ref, dyr_ref, dr_ref, di_ref):
        dy = _mm(_chunk_rows(dy_ref, ncl).astype(BF16), pm_ref[...]).astype(BF16)
        dyr_ref[0] = dy
        for gi in range(GPL):
            dh = _mm_nt(dy[:, gi * CW : (gi + 1) * CW], c_ref[gi].astype(BF16))
            dr_ref[gi, :ncc, :] = jnp.zeros((ncc, half), F32)
            di_ref[gi, :ncc, :] = jnp.zeros((ncc, half), F32)
            dr_ref[gi, ncc:, :] = dh[:, :half]
            di_ref[gi, ncc:, :] = dh[:, half:]

    out = pl.pallas_call(
        _riding(body, 3, 3, nr, (nlb,), _swap_copies),
        name="s5_dstate",
        grid=(nlb,),
        in_specs=[pl.BlockSpec((lx, LANES), lambda i: (0, i)), _whole(perm.shape), _gspec(GPL, ns, CW)] + [_HBM] * nr,
        out_specs=[pl.BlockSpec((1, ncl, BW), lambda i: (i, 0, 0)), _gspec(GPL, nch, half), _gspec(GPL, nch, half)] + [_HBM] * nr,
        out_shape=[_sds((nlb, ncl, BW), BF16), _sds((g, nch, half)), _sds((g, nch, half))] + _swap_shapes(swaps),
        scratch_shapes=_swap_sems(nr) if nr else [],
        compiler_params=_cparams(1, VMEM_BIG),
    )(dypre, perm, ccat, *swaps)
    return out[:3], out[3:]


def _s5_scan_bwd(dh_re, dh_im, a16, ncc, nch, gb):
    g, ns = a16.shape
    half = ns // 2
    ncl = nch - ncc
    hp = gb // 2

    def body(dr_ref, di_ref, a_ref, sfr_ref, sfi_ref, sbr_ref, sbi_ref):
        mf = _lane_masks(hp, half)
        afr, abr = _pack_dirs(a_ref[:hp, :half], a_ref[hp:, :half], mf)
        afi, abi = _pack_dirs(a_ref[:hp, half:], a_ref[hp:, half:], mf)
        zero = jnp.zeros((hp, half), F32)

        def step(t, c):
            lfr, lfi, lbr, lbi = c
            pf = nch - 1 - t
            pb = jnp.where(t < ncl, ncc + t, t - ncl)
            sfr_ref[pl.ds(pf, hp, stride=nch), :] = lfr
            sfi_ref[pl.ds(pf, hp, stride=nch), :] = lfi
            sbr_ref[pl.ds(pb, hp, stride=nch), :] = lbr
            sbi_ref[pl.ds(pb, hp, stride=nch), :] = lbi
            dfr, dfi = _scan_rows(dr_ref, pf, hp, nch, mf)[0], _scan_rows(di_ref, pf, hp, nch, mf)[0]
            dbr, dbi = _scan_rows(dr_ref, pb, hp, nch, mf)[1], _scan_rows(di_ref, pb, hp, nch, mf)[1]
            return (dfr + afr * lfr + afi * lfi, dfi + afr * lfi - afi * lfr, dbr + abr * lbr + abi * lbi, dbi + abr * lbi - abi * lbr)

        _unrolled_loop(nch, step, (zero,) * 4)

    blk = pl.BlockSpec((gb * nch, half), lambda i: (i, 0))
    pblk = pl.BlockSpec((hp * nch, half), lambda i: (i, 0))
    return pl.pallas_call(
        body,
        name="s5_scan_bwd",
        grid=(g // gb,),
        in_specs=[blk, blk, pl.BlockSpec((gb, ns), lambda i: (i, 0))],
        out_specs=[pblk] * 4,
        out_shape=[_sds((g // 2 * nch, half))] * 4,
        compiler_params=_cparams(1, VMEM_BIG),
    )(dh_re, dh_im, a16)


def _s5_du(u_all, dy_all, ds4, tsum, wst, perm, ncc, lx, lc, gb):
    nlb, ncl, _ = dy_all.shape
    g, _, ns = wst.shape
    nch = ncc + ncl
    half = ns // 2

    def body(u_ref, dy_ref, sfr_ref, sfi_ref, sbr_ref, sbi_ref, t_ref, w_ref, pm_ref, du_ref, dc_ref, dw_ref):
        parts = []
        mf = _lane_masks(nch, half)
        for gi in range(GPL):
            ds = jnp.concatenate([_unpacked(sfr_ref, sbr_ref, gi, gb, mf), _unpacked(sfi_ref, sbi_ref, gi, gb, mf)], axis=1).astype(BF16)
            dw_ref[gi] = _mm_tn(u_ref[0, :, gi * CW : (gi + 1) * CW], ds)
            d_all = _mm_nt(ds, w_ref[gi].astype(BF16))
            d_lat = d_all[ncc:] + _mm_nt(dy_ref[0, :, gi * CW : (gi + 1) * CW], t_ref[gi].astype(BF16))
            parts.append(jnp.concatenate([d_all[:ncc], d_lat], axis=0))
        du = _mm_nt(jnp.concatenate(parts, axis=1).astype(BF16), pm_ref[...])
        _token_rows_store(dc_ref, du[:ncc], ncc)
        _token_rows_store(du_ref, du[ncc:], ncl)

    return pl.pallas_call(
        body,
        name="s5_du",
        grid=(nlb,),
        in_specs=[pl.BlockSpec((1, nch, BW), lambda i: (i, 0, 0)), pl.BlockSpec((1, ncl, BW), lambda i: (i, 0, 0))]
        + [_packed_spec(gb, nch, half)] * 4
        + [_gspec(GPL, CW, CW), _gspec(GPL, CW, ns), _whole(perm.shape)],
        out_specs=[pl.BlockSpec((lx, LANES), lambda i: (0, i)), pl.BlockSpec((lc, LANES), lambda i: (0, i)), _gspec(GPL, CW, ns)],
        out_shape=[_sds((lx, nlb * LANES)), _sds((lc, nlb * LANES)), _sds((g, CW, ns))],
        compiler_params=_cparams(1, VMEM_BIG),
    )(u_all, dy_all, *ds4, tsum, wst, perm)


def _s5_op_grads(u_all, dy_all, ds4, h4, ncc, gb):
    nlb, nch, _ = u_all.shape
    half = ds4[0].shape[2]
    g = 2 * ds4[0].shape[0]
    ns = 2 * half

    def body(u_ref, dy_ref, sfr_ref, sfi_ref, sbr_ref, sbi_ref, hfr_ref, hfi_ref, hbr_ref, hbi_ref, dt_ref, dc_ref, da_ref):
        das = []
        mf1 = _lane_masks(1, half)
        for gi in range(GPL):
            cols = slice(gi * CW, (gi + 1) * CW)
            jl, which = _pair_index(gi, gb)
            dy = dy_ref[0, :, cols]
            dt_ref[gi] = _mm_tn(u_ref[0, ncc:, cols], dy)
            hs = _pair_tile((hfr_ref, hfi_ref, hbr_ref, hbi_ref), jl, slice(ncc, None)).astype(BF16)
            dc_ref[gi] = _group_rows_of(_mm_tn(hs, dy), which, ns // 4)
            hfr, hfi, hbr, hbi = hfr_ref[jl], hfi_ref[jl], hbr_ref[jl], hbi_ref[jl]
            sfr, sfi, sbr, sbi = sfr_ref[jl], sfi_ref[jl], sbr_ref[jl], sbi_ref[jl]
            a_re = _pick(which, *_unpack_dirs(_colsum(hfr * sfr + hfi * sfi), _colsum(hbr * sbr + hbi * sbi), mf1))
            a_im = _pick(which, *_unpack_dirs(_colsum(hfr * sfi - hfi * sfr), _colsum(hbr * sbi - hbi * sbr), mf1))
            das.append(jnp.concatenate([a_re, a_im], axis=1))
        da_ref[...] = jnp.concatenate(das, axis=0)

    return pl.pallas_call(
        body,
        name="s5_op_grads",
        grid=(nlb,),
        in_specs=[pl.BlockSpec((1, nch, BW), lambda i: (i, 0, 0)), pl.BlockSpec((1, nch - ncc, BW), lambda i: (i, 0, 0))]
        + [_packed_spec(gb, nch, half)] * 4
        + [_packed_spec(gb, nch, half, buffers=1)] * 4,
        out_specs=[_gspec(GPL, CW, CW), _gspec(GPL, ns, CW), pl.BlockSpec((GPL, ns), lambda i: (i, 0))],
        out_shape=[_sds((g, CW, CW)), _sds((g, ns, CW)), _sds((g, ns))],
        compiler_params=_cparams(1, VMEM_BIG),
    )(u_all, dy_all, *ds4, *h4)


def _merge_fwd(y_ssm, proj, lin, x, tgt, glu_wg, out_w, d_skip, glu_b, pscale, gate, fin_g, tm):
    lx, d = x.shape
    sw = y_ssm.shape[1]
    mix = out_w.shape[0]
    pw = mix - sw
    cb2 = glu_wg.shape[2]
    nh = NDEV // 2

    def body(y_ref, u_ref, z_ref, l_ref, x_ref, t_ref, gw_ref, ow_ref, dk_ref, gb_ref, ps_ref, gt_ref, fg_ref,
             ypre_ref, y12_ref, br_ref, dxo_ref, loss_ref, dfg_ref, dgt_ref):
        @pl.when(pl.program_id(0) == 0)
        def _():
            loss_ref[...] = jnp.zeros_like(loss_ref)
            dfg_ref[...] = jnp.zeros_like(dfg_ref)
            dgt_ref[...] = jnp.zeros_like(dgt_ref)

        ypre = y_ref[...] + dk_ref[...] * u_ref[...]
        ypre_ref[...] = ypre
        yg = _gelu(ypre).astype(BF16)
        outs = []
        for j in range(nh):
            y1 = _mm(yg, gw_ref[j]) + gb_ref[:, j * cb2 : (j + 1) * cb2]
            y2 = _mm(yg, gw_ref[nh + j]) + gb_ref[:, (nh + j) * cb2 : (nh + j + 1) * cb2]
            y12_ref[:, j * cb2 : (j + 1) * cb2] = y1
            y12_ref[:, (nh + j) * cb2 : (nh + j + 1) * cb2] = y2
            outs.append(y1 * _sigmoid(y2))
        lin = l_ref[...] * ps_ref[...]
        brb = (jnp.concatenate([lin] + outs, axis=1) * _silu(z_ref[...])).astype(BF16)
        br_ref[...] = brb
        mixv = _mm(brb, ow_ref[...])
        xo = x_ref[...] + gt_ref[...] * mixv
        r2 = lax.rsqrt(_rowmean(xo * xo) + EPS)
        xh = xo * r2
        fg = fg_ref[...]
        e = xh * fg - t_ref[...]
        loss_ref[...] += 0.5 * _colsum(_rowmean(e * e))
        dy = e * (1.0 / d)
        dfg_ref[...] += _colsum(dy * xh)
        gy = dy * fg
        dxo = r2 * (gy - xh * _rowmean(gy * xh))
        dxo_ref[...] = dxo
        dgt_ref[...] += _colsum(dxo * mixv)

    vec = _whole((1, d))
    return pl.pallas_call(
        body,
        name="merge_fwd",
        grid=(lx // tm,),
        in_specs=[_rows(tm, sw), _rows(tm, sw, 1), _rows(tm, mix, 1), _rows(tm, pw), _rows(tm, d), _rows(tm, d), _whole(glu_wg.shape), _whole(out_w.shape), _whole((1, sw)), _whole((1, 2 * sw)), _whole((1, pw)), vec, vec],
        out_specs=[_rows(tm, sw), _rows(tm, 2 * sw), _rows(tm, mix), _rows(tm, d), _acc((1, 1)), _acc((1, d)), _acc((1, d))],
        out_shape=[_sds((lx, sw)), _sds((lx, 2 * sw)), _sds((lx, mix), BF16), _sds((lx, d)), _sds((1, 1)), _sds((1, d)), _sds((1, d))],
        compiler_params=_cparams(1, VMEM_BIG),
    )(y_ssm, proj, proj, lin, x, tgt, glu_wg, out_w, d_skip, glu_b, pscale, gate, fin_g)


def _out_bwd(dxo, gate, branch, out_w, lin, y12, proj, pscale, tm):
    lx, d = dxo.shape
    mix = out_w.shape[0]
    pw = lin.shape[1]
    sw = mix - pw
    nb = lx // tm

    def body(dxo_ref, gt_ref, br_ref, ow_ref, l_ref, y_ref, z_ref, ps_ref, dz_ref, dl_ref, dg_ref, dps_ref, dow_hbm, acc, sem):
        i = pl.program_id(0)

        @pl.when(i == 0)
        def _():
            acc[...] = jnp.zeros_like(acc)
            dps_ref[...] = jnp.zeros_like(dps_ref)

        dmix = (dxo_ref[...] * gt_ref[...]).astype(BF16)
        dbr = _mm_nt(dmix, ow_ref[...])
        acc[...] += _mm_tn(br_ref[...], dmix)
        z = z_ref[...]
        sg = _sigmoid(z)
        dbp = dbr * (z * sg)
        y1, y2 = y_ref[:, :sw], y_ref[:, sw:]
        s2 = _sigmoid(y2)
        ps = ps_ref[...]
        lin = l_ref[...]
        bp = jnp.concatenate([lin * ps, y1 * s2], axis=1)
        dz_ref[...] = dbr * bp * (sg * (1.0 + z * (1.0 - sg)))
        dlp = dbp[:, :pw]
        dl_ref[...] = dlp * ps
        dps_ref[...] += _colsum(dlp * lin)
        dss = dbp[:, pw:]
        dg_ref[...] = jnp.concatenate([dss * s2, dss * y1 * s2 * (1.0 - s2)], axis=1).astype(BF16)

        @pl.when(i == nb - 1)
        def _():
            cp = pltpu.make_async_copy(acc, dow_hbm, sem)
            cp.start()
            cp.wait()

    return pl.pallas_call(
        body,
        name="out_bwd",
        grid=(nb,),
        in_specs=[_rows(tm, d), _whole((1, d)), _rows(tm, mix), _whole(out_w.shape), _rows(tm, pw), _rows(tm, 2 * sw), _rows(tm, mix, 1), _whole((1, pw))],
        out_specs=[_rows(tm, mix), _rows(tm, pw), _rows(tm, 2 * sw), _acc((1, pw)), _ANY],
        out_shape=[_sds((lx, mix)), _sds((lx, pw)), _sds((lx, 2 * sw), BF16), _sds((1, pw)), _sds((mix, d))],
        scratch_shapes=[pltpu.VMEM((mix, d), F32), pltpu.SemaphoreType.DMA],
        compiler_params=_cparams(1, VMEM_BIG),
    )(dxo, gate, branch, out_w, lin, y12, proj, pscale)


def _glu_bwd(dg12, ypre, proj, glu_wg, d_skip, tm):
    lx, sw = ypre.shape
    cb2 = glu_wg.shape[2]
    nb = lx // tm

    def body(dg_ref, yp_ref, u_ref, gw_ref, dk_ref, dyp_ref, dgb_ref, dd_ref, dgw_hbm, acc, sem):
        i = pl.program_id(0)

        @pl.when(i == 0)
        def _():
            acc[...] = jnp.zeros_like(acc)
            dgb_ref[...] = jnp.zeros_like(dgb_ref)
            dd_ref[...] = jnp.zeros_like(dd_ref)

        ypre = yp_ref[...]
        ygb = _gelu(ypre).astype(BF16)
        dg = dg_ref[...]
        dyg = jnp.zeros((tm, sw), F32)
        for j in range(NDEV):
            dgj = dg[:, j * cb2 : (j + 1) * cb2]
            dyg = dyg + _mm_nt(dgj, gw_ref[j])
            acc[j] += _mm_tn(ygb, dgj)
        dgb_ref[...] += _colsum(dg.astype(F32))
        dyp = dyg * _gelu_grad(ypre)
        dyp_ref[...] = dyp
        dd_ref[...] += _colsum(dyp * u_ref[...])

        @pl.when(i == nb - 1)
        def _():
            cp = pltpu.make_async_copy(acc, dgw_hbm, sem)
            cp.start()
            cp.wait()

    return pl.pallas_call(
        body,
        name="glu_bwd",
        grid=(nb,),
        in_specs=[_rows(tm, 2 * sw), _rows(tm, sw), _rows(tm, sw, 1), _whole(glu_wg.shape), _whole((1, sw))],
        out_specs=[_rows(tm, sw), _acc((1, 2 * sw)), _acc((1, sw)), _ANY],
        out_shape=[_sds((lx, sw)), _sds((1, 2 * sw)), _sds((1, sw)), _sds(glu_wg.shape)],
        scratch_shapes=[pltpu.VMEM(glu_wg.shape, F32), pltpu.SemaphoreType.DMA],
        compiler_params=_cparams(1, VMEM_BIG),
    )(dg12, ypre, proj, glu_wg, d_skip)


def _pack(parts, total):
    flat = jnp.concatenate([p.reshape(-1).astype(F32) for p in parts])
    return jnp.pad(flat, (0, total - flat.shape[0]))


def _unpack(flat, shapes):
    out, off = [], 0
    for s in shapes:
        n = int(np.prod(s)) if len(s) else 1
        out.append(flat[off : off + n].reshape(s))
        off += n
    return out


def kernel(x, c, ctx, c_ctx, ada_w, ada_b, norm_g, in_w, pool_w, pool_scale, s5_lam_re, s5_lam_im, s5_log_dt, s5_b_re, s5_b_im, s5_c_re, s5_c_im, s5_d, glu_w, glu_b, out_w, final_g, loss_target, m_c_ctx, m_ada_w, m_ada_b, m_norm_g, m_in_w, m_pool_w, m_pool_scale, m_s5_lam_re, m_s5_lam_im, m_s5_log_dt, m_s5_b_re, m_s5_b_im, m_s5_c_re, m_s5_c_im, m_s5_d, m_glu_w, m_glu_b, m_out_w, m_final_g, v_c_ctx, v_ada_w, v_ada_b, v_norm_g, v_in_w, v_pool_w, v_pool_scale, v_s5_lam_re, v_s5_lam_im, v_s5_log_dt, v_s5_b_re, v_s5_b_im, v_s5_c_re, v_s5_c_im, v_s5_d, v_glu_w, v_glu_b, v_out_w, v_final_g):
    xr, tgt, xc = x[0], loss_target[0], ctx[0]
    lx, d = xr.shape
    lc = xc.shape[0]
    mix = out_w.shape[1] * NDEV
    sw = glu_w.shape[1]
    pw = mix - sw
    pgw = pw // len(POOL_WINDOWS)
    ngrp = sw // SSM_H
    cb = in_w.shape[2]
    cb2 = glu_w.shape[2]
    nmod = ada_w.shape[2]
    ncc, ncl = lc // CHUNK_T, lx // CHUNK_T
    nch = ncc + ncl
    tm = min(256, lx)
    tm_acc = min(512, lx)
    assert mix == d and pw == sw and 2 * cb == pw and NDEV * cb2 == 2 * sw and lx % GRID_W == 0
    mx, my, mc = _my_pos()
    me = 4 * mx + 2 * my + mc

    (c_all,) = _all_gather([c], "gather_c")
    call = jnp.concatenate([c_all.reshape(NDEV, d), c_ctx.reshape(1, d), jnp.zeros((NDEV - 1, d), F32)], axis=0)
    ada_w_l = ada_w[0]
    ada_b_l = lax.dynamic_slice_in_dim(ada_b, me * nmod, nmod, axis=1)
    m_loc = _mod_fwd(call, ada_w_l, ada_b_l)
    (m_all,) = _all_gather([m_loc], "gather_mod")
    mod = lax.dynamic_index_in_dim(m_all, me, axis=1, keepdims=False).reshape(1, NDEV * nmod)
    mod_c = m_all[:, NDEV, :].reshape(1, NDEV * nmod)
    shift, scale, gate = mod[:, :d], mod[:, d : 2 * d], mod[:, 2 * d :]
    shift_c, scale_c = mod_c[:, :d], mod_c[:, d : 2 * d]

    s5_params = tuple(p[0] for p in (s5_lam_re, s5_lam_im, s5_log_dt, s5_b_re, s5_b_im, s5_c_re, s5_c_im))
    (small, a16), small_vjp = jax.vjp(_s5_small, *s5_params)
    gb_ops = min(8, ngrp)
    (tsum, wst, ccat), (in_wg,) = _s5_build(small, gb_ops, [in_w[0].astype(BF16)])

    ssm_blocks = (2, 3)
    (proj, hb), (glu_wg, out_wg, pool_wg) = _in_proj(
        xr, norm_g, scale, shift, in_wg, tuple(range(NDEV)), tm, "in_proj",
        shards=[glu_w[0].astype(BF16), out_w[0].astype(BF16), pool_w[0].reshape(-1, pgw).astype(BF16)],
    )
    out_wf = out_wg.reshape(mix, d)
    pool_wf = pool_wg.reshape(NDEV, len(POOL_WINDOWS), pgw // NDEV, pgw).transpose(1, 0, 2, 3).reshape(len(POOL_WINDOWS), pgw, pgw)
    (uc, hcb), _ = _in_proj(xc, norm_g, scale_c, shift_c, in_wg, ssm_blocks, min(tm, lc), "in_proj_ctx")
    lin, dmat = _pool_fwd(proj, pool_wf)

    gb_scan = min(16, ngrp)
    half = wst.shape[2] // 2
    perm = _chunk_perm()
    flat = lambda a: a.reshape(ngrp * nch, half)
    pairs = lambda arrs: [a.reshape(ngrp // 2, nch, half) for a in arrs]
    u_all, s_re, s_im = _s5_state(proj, pw // LANES, uc, perm, wst, ncc, ncl)
    h4 = pairs(_s5_scan_fwd(flat(s_re), flat(s_im), a16, ncc, nch, gb_scan))
    y_ssm = _s5_out(u_all, h4, tsum, ccat, perm, ncc, lx, gb_scan)

    ypre, y12, branch, dxo, loss_l, dfg, dgate = _merge_fwd(
        y_ssm, proj, lin, xr, tgt, glu_wg, out_wf, s5_d, glu_b, pool_scale, gate, final_g.reshape(1, d), tm
    )

    dz, dlin, dg12, dps, d_out_w = _out_bwd(dxo, gate, branch, out_wf, lin, y12, proj, pool_scale, tm)
    dypre, dglu_b, dd_skip, d_glu_w = _glu_bwd(dg12, ypre, proj, glu_wg, s5_d, tm_acc)

    d_glu_out = [d_glu_w, d_out_w.reshape(NDEV, mix // NDEV, d)]
    (dy_all, dh_re, dh_im), got_glu_out = _s5_dstate(dypre, perm, ccat, ncc, swaps=[_by_chip_and_core(a) for a in d_glu_out])
    ds4 = pairs(_s5_scan_bwd(flat(dh_re), flat(dh_im), a16, ncc, nch, gb_scan))
    du5, duc, dwst = _s5_du(u_all, dy_all, ds4, tsum, wst, perm, ncc, lx, lc, gb_scan)
    dtsum, dccat, da16 = _s5_op_grads(u_all, dy_all, ds4, h4, ncc, gb_scan)
    dup, dpool_w = _pool_bwd(dlin, pool_wf, dmat)

    npw = len(POOL_WINDOWS)
    dpool_s = dpool_w.reshape(npw, NDEV, pgw // NDEV, pgw).transpose(1, 0, 2, 3).reshape(NDEV, npw * (pgw // NDEV), pgw)
    parts_a = _pair_partials(d_glu_out, [BF16] * 2, "grads_a", got=got_glu_out) + _pair_partials([dpool_s], [BF16], "grads_p")
    dpcb, dsc_c, dsh_c, dg_c = _in_bwd_ctx(xc, duc, in_wg, ssm_blocks, norm_g, scale_c, shift_c)
    grad_x, dpb, dsc, dsh, dnorm_g = _in_bwd(xr, dxo, dup, du5, dypre, dz, in_wg, norm_g, scale, shift, s5_d, dg_c, tm)
    d_in_w, (land_glu, land_out, land_pool) = _in_w_grad(hb, dpb, hcb, dpcb, cb, min(2 * tm_acc, lx), parts=parts_a)

    dmod = jnp.concatenate(
        [jnp.concatenate([dsh, dsc, dgate], axis=1), jnp.concatenate([dsh_c, dsc_c, jnp.zeros((1, d), F32)], axis=1)], axis=0
    )
    (dmod_all,) = _all_gather([dmod], "gather_dmod")
    dmod_l = lax.dynamic_slice_in_dim(dmod_all, me * nmod, nmod, axis=2)
    g_ada_w, g_ada_b_l, cctx_part = _mod_bwd(call, dmod_l[:, 0, :], dmod_l[:, 1, :], ada_w_l)

    parts_b = _pair_partials([d_in_w], [BF16], "grads_b")
    dsmall, (land_in,) = _s5_build_bwd(small, dtsum, dwst, dccat, gb_ops, parts_b)
    ds5 = small_vjp((tuple(dsmall), da16))

    small_names = ["norm_g", "pool_scale", "s5_lam_re", "s5_lam_im", "s5_log_dt", "s5_b_re", "s5_b_im", "s5_c_re", "s5_c_im", "s5_d", "glu_b", "final_g", "c_ctx"]
    small_w = dict(norm_g=norm_g, pool_scale=pool_scale, s5_lam_re=s5_lam_re, s5_lam_im=s5_lam_im, s5_log_dt=s5_log_dt, s5_b_re=s5_b_re, s5_b_im=s5_b_im, s5_c_re=s5_c_re, s5_c_im=s5_c_im, s5_d=s5_d, glu_b=glu_b, final_g=final_g, c_ctx=c_ctx)
    small_m = dict(norm_g=m_norm_g, pool_scale=m_pool_scale, s5_lam_re=m_s5_lam_re, s5_lam_im=m_s5_lam_im, s5_log_dt=m_s5_log_dt, s5_b_re=m_s5_b_re, s5_b_im=m_s5_b_im, s5_c_re=m_s5_c_re, s5_c_im=m_s5_c_im, s5_d=m_s5_d, glu_b=m_glu_b, final_g=m_final_g, c_ctx=m_c_ctx)
    small_v = dict(norm_g=v_norm_g, pool_scale=v_pool_scale, s5_lam_re=v_s5_lam_re, s5_lam_im=v_s5_lam_im, s5_log_dt=v_s5_log_dt, s5_b_re=v_s5_b_re, s5_b_im=v_s5_b_im, s5_c_re=v_s5_c_re, s5_c_im=v_s5_c_im, s5_d=v_s5_d, glu_b=v_glu_b, final_g=v_final_g, c_ctx=v_c_ctx)
    small_g = dict(norm_g=dnorm_g, pool_scale=dps, s5_lam_re=ds5[0], s5_lam_im=ds5[1], s5_log_dt=ds5[2], s5_b_re=ds5[3], s5_b_im=ds5[4], s5_c_re=ds5[5], s5_c_im=ds5[6], s5_d=dd_skip, glu_b=dglu_b, final_g=dfg, c_ctx=cctx_part)
    shapes = [small_w[k].shape for k in small_names]
    nsmall = sum(int(np.prod(s)) for s in shapes) + 1
    unit = NDEV * 8 * LANES
    tot = -(-nsmall // unit) * unit
    rows = tot // LANES
    gpack = _pack([small_g[k] for k in small_names] + [loss_l], tot).reshape(NDEV, rows // NDEV, LANES)
    (land_small,) = _reduce_scatter([gpack], [F32], "small")
    gsum = _sum_slots(land_small, "sum_small")
    (gall,) = _all_gather([gsum], "gather_small")
    (g_ada_b_all,) = _all_gather([g_ada_b_l], "gather_ada_b")
    gflat = gall.reshape(tot)
    loss = gflat[nsmall - 1]

    nab = NDEV * nmod
    tot2 = tot + -(-nab // (8 * LANES)) * (8 * LANES)
    g2 = jnp.concatenate([gflat, jnp.pad(g_ada_b_all.reshape(nab), (0, tot2 - tot - nab))]).reshape(tot2 // LANES, LANES)

    def pack2(dct, ab):
        return jnp.concatenate(
            [_pack([dct[k] for k in small_names] + [jnp.zeros((1,), F32)], tot), jnp.pad(ab.reshape(nab), (0, tot2 - tot - nab))]
        ).reshape(tot2 // LANES, LANES)

    _, d2, m2, v2 = _adam(g2, pack2(small_w, ada_b), pack2(small_m, m_ada_b), pack2(small_v, v_ada_b), "adam_small")

    def unpack2(flat2):
        flat2 = flat2.reshape(tot2)
        vals = dict(zip(small_names, _unpack(flat2, shapes)))
        vals["ada_b"] = flat2[tot : tot + nab].reshape(ada_b.shape)
        return vals

    res = {k: unpack2(a) for k, a in (("g", g2), ("d", d2), ("m", m2), ("v", v2))}

    def shard(name, g, land, w, m, v):
        shp = w.shape
        w2, m2_, v2_ = (a.reshape(-1, shp[-1]) for a in (w, m, v))
        out = _adam(g, w2, m2_, v2_, "adam_" + name, land=land)
        for k, a in zip(("g", "d", "m", "v"), out):
            res[k][name] = a.reshape(shp)

    shard("ada_w", g_ada_w, None, ada_w, m_ada_w, v_ada_w)
    shard("in_w", None, land_in, in_w, m_in_w, v_in_w)
    shard("pool_w", None, land_pool, pool_w, m_pool_w, v_pool_w)
    shard("glu_w", None, land_glu, glu_w, m_glu_w, v_glu_w)
    shard("out_w", None, land_out, out_w, m_out_w, v_out_w)

    names = ["c_ctx", "ada_w", "ada_b", "norm_g", "in_w", "pool_w", "pool_scale", "s5_lam_re", "s5_lam_im", "s5_log_dt", "s5_b_re", "s5_b_im", "s5_c_re", "s5_c_im", "s5_d", "glu_w", "glu_b", "out_w", "final_g"]
    return (loss, grad_x[None], *[res["g"][n] for n in names], *[res["d"][n] for n in names], *[res["m"][n] for n in names], *[res["v"][n] for n in names])
```

```python
import numpy as np

import jax
import jax.numpy as jnp
from jax import lax
from jax.experimental import pallas as pl
from jax.experimental.pallas import tpu as pltpu

F32 = jnp.float32
BF16 = jnp.bfloat16
NDEV = 8
EPS = 1e-6
GRID_W = 64
POOL_WINDOWS = (2, 4, 8, 16)
SSM_H = 16
CHUNK_T = 16
LANES = 128
ADAM_LR, ADAM_B1, ADAM_B2, ADAM_EPS, ADAM_WD, ADAM_STEP = 0.001, 0.9, 0.999, 1e-08, 0.01, 10
VMEM_BIG = 56 << 20
MESH_ID = pl.DeviceIdType.MESH

_HBM = pl.BlockSpec(memory_space=pltpu.HBM)
_ANY = pl.BlockSpec(memory_space=pl.ANY)
_SMEM = pl.BlockSpec(memory_space=pltpu.SMEM)


def _sds(shape, dtype=F32):
    return jax.ShapeDtypeStruct(tuple(shape), dtype)


def _cparams(ngrid=0, vmem=None):
    return pltpu.CompilerParams(
        dimension_semantics=("arbitrary",) * ngrid if ngrid else None, vmem_limit_bytes=vmem
    )


def _rows(tm, c, col=0):
    return pl.BlockSpec((tm, c), lambda i: (i, col))


def _whole(shape):
    nd = len(shape)
    return pl.BlockSpec(tuple(shape), lambda *_: (0,) * nd, pipeline_mode=pl.Buffered(1))


def _acc(shape):
    nd = len(shape)
    return pl.BlockSpec(tuple(shape), lambda *_: (0,) * nd)


def _mm(a, b):
    return jnp.dot(a, b, preferred_element_type=F32)


def _mm_nt(a, b):
    return lax.dot_general(a, b, (((1,), (1,)), ((), ())), preferred_element_type=F32)


def _mm_tn(a, b):
    return lax.dot_general(a, b, (((0,), (0,)), ((), ())), preferred_element_type=F32)


def _mm_split(k01, s):
    hi = s.astype(BF16)
    lo = (s - hi.astype(F32)).astype(BF16)
    return _mm(k01, hi) + _mm(k01, lo)


def _sigmoid(v):
    return 0.5 * (jnp.tanh(0.5 * v) + 1.0)


def _silu(v):
    return v * _sigmoid(v)


_GELU_K = 0.7978845608028654
_GELU_C = 0.044715


def _gelu(v):
    return 0.5 * v * (1.0 + jnp.tanh(_GELU_K * (v + _GELU_C * v * v * v)))


def _gelu_grad(v):
    th = jnp.tanh(_GELU_K * (v + _GELU_C * v * v * v))
    return 0.5 * (1.0 + th) + 0.5 * v * (1.0 - th * th) * (_GELU_K * (1.0 + 3.0 * _GELU_C * v * v))


def _colsum(v):
    return jnp.sum(v, axis=0, keepdims=True)


def _rowmean(v):
    return jnp.mean(v, axis=-1, keepdims=True)


NCHIP = NDEV // 2


def _my_pos():
    return lax.axis_index("x"), lax.axis_index("y"), lax.axis_index("c")


def _other_chips():
    x, y, _ = _my_pos()
    return [(1 - x, y), (x, 1 - y), (1 - x, 1 - y)]


def _remote(src, dst, send, recv, to):
    return pltpu.make_async_remote_copy(src, dst, send, recv, device_id=to, device_id_type=MESH_ID)


def _all_gather(arrs, name):
    n = len(arrs)

    def body(*refs):
        start, finish = _gather_copies(refs[:n], refs[n : 2 * n], *refs[2 * n :])
        start()
        finish()

    return pl.pallas_call(
        body, name=name, out_shape=_gather_shapes(arrs), in_specs=[_HBM] * n, out_specs=[_HBM] * n, scratch_shapes=_gather_sems(n)
    )(*arrs)


def _gather_shapes(arrs):
    return [_sds((NDEV,) + a.shape, a.dtype) for a in arrs]


def _gather_sems(n):
    return [pltpu.SemaphoreType.DMA((n, NDEV - 1)), pltpu.SemaphoreType.DMA((n, NDEV - 1)), pltpu.SemaphoreType.DMA((n,))]


def _gather_copies(ins, outs, send, recv, loc):
    n = len(ins)
    x, y, c = _my_pos()
    me, sib = (x, y, c), (x, y, 1 - c)
    chips = _other_chips()

    def slot(a, p):
        return outs[a].at[4 * p[0] + 2 * p[1] + p[2]]

    def copy(a, k, block, to, own=False):
        return _remote(ins[a] if own else slot(a, block), slot(a, block), send.at[a, k], recv.at[a, k], to)

    def mine():
        local = [pltpu.make_async_copy(ins[a], slot(a, me), loc.at[a]) for a in range(n)]
        sent = []
        for a in range(n):
            sent.append(copy(a, 0, me, sib, own=True))
            sent += [copy(a, 1 + j, me, (*chip, c), own=True) for j, chip in enumerate(chips)]
        return local, sent

    def start():
        local, sent = mine()
        for cp in local + sent:
            cp.start()

    def finish():
        local, sent = mine()
        for j, chip in enumerate(chips):
            for a in range(n):
                copy(a, 1 + j, (*chip, c), me).wait_recv()
                fwd = copy(a, 4 + j, (*chip, c), sib)
                fwd.start()
                sent.append(fwd)
        for a in range(n):
            copy(a, 0, sib, me).wait_recv()
            for j, chip in enumerate(chips):
                copy(a, 4 + j, (*chip, 1 - c), me).wait_recv()
        for cp in sent:
            cp.wait_send()
        for cp in local:
            cp.wait()

    return start, finish


def _sibling_swap(arrs, name):
    n = len(arrs)

    def body(*refs):
        start, finish = _swap_copies(refs[:n], refs[n : 2 * n], *refs[2 * n :])
        start()
        finish()

    return pl.pallas_call(
        body, name=name, out_shape=_swap_shapes(arrs), in_specs=[_HBM] * n, out_specs=[_HBM] * n, scratch_shapes=_swap_sems(n)
    )(*arrs)


def _swap_shapes(arrs):
    return [_sds((a.shape[0],) + a.shape[2:], a.dtype) for a in arrs]


def _swap_sems(n):
    return [pltpu.SemaphoreType.DMA((n,)), pltpu.SemaphoreType.DMA((n,))]


def _swap_copies(ins, outs, send, recv):
    x, y, c = _my_pos()

    def copies():
        return [_remote(ins[a].at[:, 1 - c], outs[a], send.at[a], recv.at[a], (x, y, 1 - c)) for a in range(len(ins))]

    def start():
        for cp in copies():
            cp.start()

    def finish():
        for cp in copies():
            cp.wait()

    return start, finish


def _pair_sum(arr, got, wire, name):
    _, _, r, c = arr.shape
    tr = _row_tile(r, c, budget=1 << 20)

    def body(a_ref, g_ref, o_ref):
        o_ref[0] = (a_ref[0, lax.axis_index("c")].astype(F32) + g_ref[0].astype(F32)).astype(wire)

    return pl.pallas_call(
        body,
        name=name,
        grid=(NCHIP, r // tr),
        in_specs=[pl.BlockSpec((1, 2, tr, c), lambda q, i: (q, 0, i, 0)), pl.BlockSpec((1, tr, c), lambda q, i: (q, i, 0))],
        out_specs=pl.BlockSpec((1, tr, c), lambda q, i: (q, i, 0)),
        out_shape=_sds((NCHIP, r, c), wire),
        compiler_params=_cparams(2),
    )(arr, got)


def _chip_exchange(arrs, name):
    n = len(arrs)

    def body(*refs):
        start, finish = _exchange_copies(refs[:n], refs[n : 2 * n], *refs[2 * n :])
        start()
        finish()

    return pl.pallas_call(
        body, name=name, out_shape=[_sds(a.shape, a.dtype) for a in arrs], in_specs=[_HBM] * n, out_specs=[_HBM] * n, scratch_shapes=_exchange_sems(n)
    )(*arrs)


def _exchange_sems(n):
    return [pltpu.SemaphoreType.DMA((n, NCHIP - 1)), pltpu.SemaphoreType.DMA((n, NCHIP - 1)), pltpu.SemaphoreType.DMA((n,))]


def _exchange_copies(ins, outs, send, recv, loc):
    n = len(ins)
    x, y, c = _my_pos()
    mine = 2 * x + y
    chips = _other_chips()

    def copies():
        local = [pltpu.make_async_copy(ins[a].at[mine], outs[a].at[mine], loc.at[a]) for a in range(n)]
        sent = [
            _remote(ins[a].at[2 * px + py], outs[a].at[mine], send.at[a, j], recv.at[a, j], (px, py, c))
            for a in range(n)
            for j, (px, py) in enumerate(chips)
        ]
        return local, sent

    def start():
        local, sent = copies()
        for cp in local + sent:
            cp.start()

    def finish():
        local, sent = copies()
        for a in range(n):
            for j, (px, py) in enumerate(chips):
                _remote(ins[a].at[mine], outs[a].at[2 * px + py], send.at[a, j], recv.at[a, j], (px, py, c)).wait_recv()
        for cp in sent:
            cp.wait_send()
        for cp in local:
            cp.wait()

    return start, finish


def _by_chip_and_core(a):
    return a.reshape((NCHIP, 2) + a.shape[1:])


def _pair_partials(arrs, wires, tag, got=None):
    four = [_by_chip_and_core(a) for a in arrs]
    if got is None:
        got = _sibling_swap(four, "swap_" + tag)
    return [_pair_sum(a, g, w, f"pair_sum_{tag}{i}") for i, (a, g, w) in enumerate(zip(four, got, wires))]


def _reduce_scatter(arrs, wires, tag):
    return _chip_exchange(_pair_partials(arrs, wires, tag), "exchange_" + tag)


def _row_tile(r, c, budget=1 << 20):
    best = r
    for t in range(8, r, 8):
        if r % t == 0 and t * c * 4 <= budget:
            best = t
    if r * c * 4 <= budget:
        best = r
    return best


def _sum_chips(land):
    tot = land[0].astype(F32)
    for q in range(1, NCHIP):
        tot = tot + land[q].astype(F32)
    return tot


def _adam_math(w, g, m, v):
    m2 = ADAM_B1 * m + (1.0 - ADAM_B1) * g
    v2 = ADAM_B2 * v + (1.0 - ADAM_B2) * (g * g)
    mh = m2 / (1.0 - ADAM_B1**ADAM_STEP)
    vh = v2 / (1.0 - ADAM_B2**ADAM_STEP)
    delta = -ADAM_LR * (mh / (jnp.sqrt(vh) + ADAM_EPS) + ADAM_WD * w)
    return delta, m2, v2


def _adam(g, w, m, v, name, land=None):
    r, c = w.shape
    tr = _row_tile(r, c, budget=1 << 20)

    def body(*refs):
        if land is not None:
            l_ref, w_ref, m_ref, v_ref, g_ref, d_ref, m2_ref, v2_ref = refs
            gv = _sum_chips(l_ref)
            g_ref[...] = gv
        else:
            g_in, w_ref, m_ref, v_ref, d_ref, m2_ref, v2_ref = refs
            gv = g_in[...]
        d, m2, v2 = _adam_math(w_ref[...], gv, m_ref[...], v_ref[...])
        d_ref[...] = d
        m2_ref[...] = m2
        v2_ref[...] = v2

    blk = _rows(tr, c)
    if land is not None:
        in_specs = [pl.BlockSpec((NCHIP, tr, c), lambda i: (0, i, 0)), blk, blk, blk]
        out = pl.pallas_call(
            body, name=name, grid=(r // tr,), in_specs=in_specs, out_specs=[blk] * 4, out_shape=[_sds((r, c))] * 4, compiler_params=_cparams(1)
        )(land, w, m, v)
        return out
    out = pl.pallas_call(
        body, name=name, grid=(r // tr,), in_specs=[blk] * 4, out_specs=[blk] * 3, out_shape=[_sds((r, c))] * 3, compiler_params=_cparams(1)
    )(g, w, m, v)
    return (g,) + tuple(out)


def _sum_slots(land, name):
    _, r, c = land.shape
    tr = _row_tile(r, c, budget=1 << 20)

    def body(l_ref, o_ref):
        o_ref[...] = _sum_chips(l_ref)

    return pl.pallas_call(
        body,
        name=name,
        grid=(r // tr,),
        in_specs=[pl.BlockSpec((NCHIP, tr, c), lambda i: (0, i, 0))],
        out_specs=_rows(tr, c),
        out_shape=_sds((r, c)),
        compiler_params=_cparams(1),
    )(land)


def _mod_fwd(call, ada_w, ada_b_loc):
    def body(c_ref, w_ref, b_ref, o_ref):
        s = _silu(c_ref[...]).astype(BF16)
        o_ref[...] = _mm(s, w_ref[...].astype(BF16)) + b_ref[...]

    return pl.pallas_call(
        body, name="mod_fwd", out_shape=_sds((call.shape[0], ada_w.shape[1])), compiler_params=_cparams(0, 32 << 20)
    )(call, ada_w, ada_b_loc)


def _mod_bwd(call, dm_loc, dmc_loc, ada_w):
    d, n = ada_w.shape
    pad = call.shape[0] - NDEV - 1

    def body(c_ref, dm_ref, dmc_ref, w_ref, gw_ref, gb_ref, cp_ref):
        cv = c_ref[...]
        sg = _sigmoid(cv)
        dmc = _colsum(dmc_ref[...])
        dm = dm_ref[...]
        rows = jnp.concatenate([dm, dmc, jnp.zeros((pad, n), F32)], axis=0)
        gw_ref[...] = _mm_tn((cv * sg).astype(BF16), rows.astype(BF16))
        gb_ref[...] = _colsum(dm) + dmc
        back = _mm_nt(rows[NDEV:].astype(BF16), w_ref[...].astype(BF16))
        c8, s8 = cv[NDEV : NDEV + 1], sg[NDEV : NDEV + 1]
        cp_ref[...] = back[0:1] * (s8 * (1.0 + c8 * (1.0 - s8)))

    return pl.pallas_call(
        body, name="mod_bwd", out_shape=[_sds((d, n)), _sds((1, n)), _sds((1, d))], compiler_params=_cparams(0, 40 << 20)
    )(call, dm_loc, dmc_loc, ada_w)


def _riding(body, nin, nout, nr, grid, copies, nscratch=0):
    def wrapped(*refs):
        base = nin + 2 * nr + nout
        start, finish = copies(refs[nin : nin + nr], refs[nin + nr + nout : base], *refs[base + nscratch :])
        first, last = True, True
        for ax, size in enumerate(grid):
            first = jnp.logical_and(first, pl.program_id(ax) == 0)
            last = jnp.logical_and(last, pl.program_id(ax) == size - 1)
        pl.when(first)(start)
        body(*refs[:nin], *refs[nin + nr : nin + nr + nout], *refs[base : base + nscratch])
        pl.when(last)(finish)

    return wrapped if nr else body


def _in_proj(xr, norm_g, scale, shift, wg, jsel, tm, name, shards=()):
    lx, d = xr.shape
    cb = wg.shape[2]
    nr = len(shards)

    def body(x_ref, g_ref, sc_ref, sh_ref, w_ref, p_ref, h_ref):
        xv = x_ref[...]
        r = lax.rsqrt(_rowmean(xv * xv) + EPS)
        hb = ((xv * r) * g_ref[...] * (1.0 + sc_ref[...]) + sh_ref[...]).astype(BF16)
        h_ref[...] = hb
        for q, j in enumerate(jsel):
            p_ref[:, q * cb : (q + 1) * cb] = _mm(hb, w_ref[j])

    vec = _whole((1, d))
    out = pl.pallas_call(
        _riding(body, 5, 2, nr, (lx // tm,), _gather_copies),
        name=name,
        grid=(lx // tm,),
        in_specs=[_rows(tm, d), vec, vec, vec, _whole(wg.shape)] + [_HBM] * nr,
        out_specs=[_rows(tm, len(jsel) * cb), _rows(tm, d)] + [_HBM] * nr,
        out_shape=[_sds((lx, len(jsel) * cb)), _sds((lx, d), BF16)] + _gather_shapes(shards),
        scratch_shapes=_gather_sems(nr) if nr else [],
        compiler_params=_cparams(1, VMEM_BIG),
    )(xr, norm_g, scale, shift, wg, *shards)
    return out[:2], out[2:]


def _in_bwd(xr, dxo, dup, du5, dypre, dz, wg, norm_g, scale, shift, d_skip, dg_init, tm):
    lx, d = xr.shape
    cb = wg.shape[2]
    pw = dup.shape[1]
    sw = du5.shape[1]
    mix = dz.shape[1]
    ncol = NDEV * cb

    def body(x_ref, dxo_ref, up_ref, d5_ref, dy_ref, dz_ref, w_ref, g_ref, sc_ref, sh_ref, dk_ref, gi_ref,
             gx_ref, dp_ref, dsc_ref, dsh_ref, dg_ref):
        i = pl.program_id(0)

        @pl.when(i == 0)
        def _():
            dsc_ref[...] = jnp.zeros_like(dsc_ref)
            dsh_ref[...] = jnp.zeros_like(dsh_ref)
            dg_ref[...] = gi_ref[...]

        dp = jnp.concatenate(
            [up_ref[...], d5_ref[...] + dk_ref[...] * dy_ref[...], dz_ref[...]], axis=1
        ).astype(BF16)
        dp_ref[...] = dp
        dh = _mm_nt(dp[:, 0:cb], w_ref[0])
        for j in range(1, NDEV):
            dh = dh + _mm_nt(dp[:, j * cb : (j + 1) * cb], w_ref[j])
        xv = x_ref[...]
        r = lax.rsqrt(_rowmean(xv * xv) + EPS)
        xh = xv * r
        g = g_ref[...]
        one_sc = 1.0 + sc_ref[...]
        dsh_ref[...] += _colsum(dh)
        dsc_ref[...] += _colsum(dh * (xh * g))
        dg_ref[...] += _colsum(dh * one_sc * xh)
        dxh = dh * one_sc * g
        gx_ref[...] = r * (dxh - xh * _rowmean(dxh * xh)) + dxo_ref[...]

    vec = _whole((1, d))
    return pl.pallas_call(
        body,
        name="in_bwd",
        grid=(lx // tm,),
        in_specs=[_rows(tm, d), _rows(tm, d), _rows(tm, pw), _rows(tm, sw), _rows(tm, sw), _rows(tm, mix), _whole(wg.shape), vec, vec, vec, _whole((1, sw)), vec],
        out_specs=[_rows(tm, d), _rows(tm, ncol), _acc((1, d)), _acc((1, d)), _acc((1, d))],
        out_shape=[_sds((lx, d)), _sds((lx, ncol), BF16), _sds((1, d)), _sds((1, d)), _sds((1, d))],
        compiler_params=_cparams(1, VMEM_BIG),
    )(xr, dxo, dup, du5, dypre, dz, wg, norm_g, scale, shift, d_skip, dg_init)


def _in_bwd_ctx(xc, duc, wg, jsel, norm_g, scale, shift):
    lc, d = xc.shape
    cb = wg.shape[2]

    def body(x_ref, du_ref, w_ref, g_ref, sc_ref, sh_ref, dp_ref, dsc_ref, dsh_ref, dg_ref):
        dp = du_ref[...].astype(BF16)
        dp_ref[...] = dp
        dh = _mm_nt(dp[:, 0:cb], w_ref[jsel[0]])
        for q in range(1, len(jsel)):
            dh = dh + _mm_nt(dp[:, q * cb : (q + 1) * cb], w_ref[jsel[q]])
        xv = x_ref[...]
        xh = xv * lax.rsqrt(_rowmean(xv * xv) + EPS)
        dsh_ref[...] = _colsum(dh)
        dsc_ref[...] = _colsum(dh * (xh * g_ref[...]))
        dg_ref[...] = _colsum(dh * (1.0 + sc_ref[...]) * xh)

    return pl.pallas_call(
        body,
        name="in_bwd_ctx",
        out_shape=[_sds(duc.shape, BF16), _sds((1, d)), _sds((1, d)), _sds((1, d))],
        compiler_params=_cparams(0, VMEM_BIG),
    )(xc, duc, wg, norm_g, scale, shift)


def _in_w_grad(hb, dpb, hcb, dpcb, cb, tm, parts=()):
    lx, d = hb.shape
    lc = hcb.shape[0]
    nb = lx // tm
    cg = 2 * cb
    nr = len(parts)
    grid = (NDEV // 2, nb + 1)

    def body(h_ref, dp_ref, hc_ref, dpc_ref, o_ref, acc):
        j, i = pl.program_id(0), pl.program_id(1)

        @pl.when(i == 0)
        def _():
            acc[...] = jnp.zeros_like(acc)

        @pl.when(i < nb)
        def _():
            h = h_ref[...]
            acc[0] += _mm_tn(h, dp_ref[:, 0:cb])
            acc[1] += _mm_tn(h, dp_ref[:, cb:cg])

        @pl.when(jnp.logical_and(i == nb, j == 1))
        def _():
            h = hc_ref[...]
            acc[0] += _mm_tn(h, dpc_ref[:, 0:cb])
            acc[1] += _mm_tn(h, dpc_ref[:, cb:cg])

        @pl.when(i == nb)
        def _():
            o_ref[...] = acc[...].astype(o_ref.dtype)

    out = pl.pallas_call(
        _riding(body, 4, 1, nr, grid, _exchange_copies, nscratch=1),
        name="in_w_grad",
        grid=grid,
        in_specs=[
            pl.BlockSpec((tm, d), lambda j, i: (jnp.minimum(i, nb - 1), 0)),
            pl.BlockSpec((tm, cg), lambda j, i: (jnp.minimum(i, nb - 1), j)),
            pl.BlockSpec((lc, d), lambda j, i: (0, 0)),
            pl.BlockSpec((lc, cg), lambda j, i: (0, 0)),
        ]
        + [_HBM] * nr,
        out_specs=[pl.BlockSpec((2, d, cb), lambda j, i: (j, 0, 0))] + [_HBM] * nr,
        out_shape=[_sds((NDEV, d, cb), BF16)] + [_sds(p.shape, p.dtype) for p in parts],
        scratch_shapes=[pltpu.VMEM((2, d, cb), F32)] + (_exchange_sems(nr) if nr else []),
        compiler_params=_cparams(2, VMEM_BIG),
    )(hb, dpb, hcb, dpcb, *parts)
    return out[0], out[1:]


def _pool_tables(w, rows, rb, pgw, transpose):
    t = np.arange(GRID_W)
    lo, hi = np.clip(t - w // 2, 0, GRID_W), np.clip(t + w - w // 2, 0, GRID_W)
    band = ((t[None, :] >= lo[:, None]) & (t[None, :] < hi[:, None])).astype(np.float32)
    if transpose:
        band = band.T
    kc = np.kron(np.eye(rb, dtype=np.float32), band)
    inv_c = np.tile((1.0 / (hi - lo).astype(np.float32))[:, None], (rb, pgw)).astype(np.float32)
    r = np.arange(rows)
    cnt_r = np.clip(r + w - w // 2, 0, rows) - np.clip(r - w // 2, 0, rows)
    inv_r = (1.0 / cnt_r.astype(np.float32)).astype(np.float32)
    return jnp.asarray(kc, BF16), jnp.asarray(inv_r), jnp.asarray(inv_c)


def _pool_stack(rows, rb, pgw, transpose):
    tabs = [_pool_tables(w, rows, rb, pgw, transpose) for w in POOL_WINDOWS]
    return tuple(jnp.stack([t[i] for t in tabs]) for i in range(3))


def _window_sums(ref, r0, rb, w):
    slab = lambda off: ref[pl.ds(pl.multiple_of((r0 + off) * GRID_W, GRID_W), GRID_W), :]
    shared = range(rb - 1, w)
    common = None
    for off in shared:
        common = slab(off) if common is None else common + slab(off)
    sums = []
    for rr in range(rb):
        s = common
        for off in range(rr, rr + w):
            if off not in shared:
                s = slab(off) if s is None else s + slab(off)
        sums.append(s)
    return sums


def _per_window(group):
    for k, w in enumerate(POOL_WINDOWS):
        pl.when(pl.program_id(0) == k)(lambda k=k, w=w: group(k, w))


def _pool_fwd(proj, pool_wf):
    lx = proj.shape[0]
    ng, pgw, _ = pool_wf.shape
    rows = lx // GRID_W
    rb = min(4, rows)
    tok = rb * GRID_W
    kc, inv_r, inv_c = _pool_stack(rows, rb, pgw, False)

    def body(p_ref, w_ref, kc_ref, ir_ref, ic_ref, lin_ref, dm_ref, xp):
        def group(kk, w):
            lo = w // 2
            xp[pl.ds(0, lo * GRID_W), :] = jnp.zeros((lo * GRID_W, pgw), F32)
            xp[pl.ds((lo + rows) * GRID_W, (w - lo) * GRID_W), :] = jnp.zeros(((w - lo) * GRID_W, pgw), F32)
            xp[pl.ds(lo * GRID_W, lx), :] = p_ref[...]

            def blk(b, carry):
                r0 = b * rb
                parts = [s * ir_ref[kk, r0 + rr] for rr, s in enumerate(_window_sums(xp, r0, rb, w))]
                m = _mm_split(kc_ref[0], jnp.concatenate(parts, axis=0)) * ic_ref[0]
                u = xp[pl.ds(pl.multiple_of((r0 + lo) * GRID_W, GRID_W), tok), :]
                db = (m - u).astype(BF16)
                o0 = pl.multiple_of(r0 * GRID_W, GRID_W)
                dm_ref[pl.ds(o0, tok), :] = db
                lin_ref[pl.ds(o0, tok), :] = _mm(db, w_ref[0])
                return carry

            lax.fori_loop(0, rows // rb, blk, 0)

        _per_window(group)

    col = pl.BlockSpec((lx, pgw), lambda g: (0, g))
    return pl.pallas_call(
        body,
        name="pool_fwd",
        grid=(ng,),
        in_specs=[col, _gspec(1, pgw, pgw), _gspec(1, tok, tok), _SMEM, _gspec(1, tok, pgw)],
        out_specs=[col, col],
        out_shape=[_sds((lx, ng * pgw)), _sds((lx, ng * pgw), BF16)],
        scratch_shapes=[pltpu.VMEM(((rows + max(POOL_WINDOWS)) * GRID_W, pgw), F32)],
        compiler_params=_cparams(1, VMEM_BIG),
    )(proj, pool_wf, kc, inv_r, inv_c)


def _pool_bwd(dlin, pool_wf, dmat):
    lx = dlin.shape[0]
    ng, pgw, _ = pool_wf.shape
    rows = lx // GRID_W
    rb = min(4, rows)
    tok = rb * GRID_W
    kct, inv_r, inv_c = _pool_stack(rows, rb, pgw, True)

    def body(dl_ref, w_ref, kc_ref, ir_ref, ic_ref, dm_ref, du_ref, dw_ref, tp):
        def group(kk, w):
            front = w - w // 2 - 1
            if front:
                tp[pl.ds(0, front * GRID_W), :] = jnp.zeros((front * GRID_W, pgw), F32)
            tp[pl.ds((front + rows) * GRID_W, (w - front) * GRID_W), :] = jnp.zeros(((w - front) * GRID_W, pgw), F32)
            dw_ref[...] = jnp.zeros_like(dw_ref)

            def blk(b, carry):
                o0 = pl.multiple_of(b * tok, GRID_W)
                dlb = dl_ref[pl.ds(o0, tok), :].astype(BF16)
                dw_ref[0] += _mm_tn(dm_ref[pl.ds(o0, tok), :], dlb)
                dd = _mm_nt(dlb, w_ref[0])
                du_ref[pl.ds(o0, tok), :] = -dd
                t = _mm(kc_ref[0], (dd * ic_ref[0]).astype(BF16))
                for rr in range(rb):
                    dst = pl.multiple_of((b * rb + rr + front) * GRID_W, GRID_W)
                    tp[pl.ds(dst, GRID_W), :] = t[rr * GRID_W : (rr + 1) * GRID_W] * ir_ref[kk, b * rb + rr]
                return carry

            lax.fori_loop(0, rows // rb, blk, 0)

            def rowl(b, carry):
                for rr, s in enumerate(_window_sums(tp, b * rb, rb, w)):
                    du_ref[pl.ds(pl.multiple_of((b * rb + rr) * GRID_W, GRID_W), GRID_W), :] += s
                return carry

            lax.fori_loop(0, rows // rb, rowl, 0)

        _per_window(group)

    col = pl.BlockSpec((lx, pgw), lambda g: (0, g))
    return pl.pallas_call(
        body,
        name="pool_bwd",
        grid=(ng,),
        in_specs=[col, _gspec(1, pgw, pgw), _gspec(1, tok, tok), _SMEM, _gspec(1, tok, pgw), col],
        out_specs=[col, _gspec(1, pgw, pgw)],
        out_shape=[_sds((lx, ng * pgw)), _sds((ng, pgw, pgw))],
        scratch_shapes=[pltpu.VMEM(((rows + max(POOL_WINDOWS)) * GRID_W, pgw), F32)],
        compiler_params=_cparams(1, VMEM_BIG),
    )(dlin, pool_wf, kct, inv_r, inv_c, dmat)


def _s5_small(lam_re, lam_im, log_dt, b_re, b_im, c_re, c_im):
    t = CHUNK_T
    dt = jnp.exp(log_dt)[..., None]
    zr, zi = lam_re * dt, lam_im * dt
    tau = jnp.arange(t + 1, dtype=F32)
    mag = jnp.exp(zr[..., None] * tau)
    pr, pi = mag * jnp.cos(zi[..., None] * tau), mag * jnp.sin(zi[..., None] * tau)
    ar, ai = pr[..., 1], pi[..., 1]
    den = lam_re * lam_re + lam_im * lam_im
    qr = ((ar - 1.0) * lam_re + ai * lam_im) / den
    qi = (ai * lam_re - (ar - 1.0) * lam_im) / den
    bbr = qr[..., None] * b_re - qi[..., None] * b_im
    bbi = qr[..., None] * b_im + qi[..., None] * b_re
    ctr, cti = jnp.swapaxes(c_re, 2, 3), jnp.swapaxes(c_im, 2, 3)
    lane_pad = lambda v: jnp.pad(jnp.swapaxes(v, 0, 1), ((0, 0), (0, 0), (0, 0), (0, LANES - v.shape[-1])))
    tp = lambda v: jnp.swapaxes(v, 1, 2)
    flip = lambda v: v[..., ::-1]
    pfr, pfi = tp(flip(pr[0, ..., :t])), tp(flip(pi[0, ..., :t]))
    pbr, pbi = tp(pr[1, ..., :t]), tp(pi[1, ..., :t])
    bfr, bfi, bbr_t, bbi_t = tp(bbr[0]), tp(bbi[0]), tp(bbr[1]), tp(bbi[1])
    a1 = jnp.concatenate([pfr, pbr, pfr, pbr], axis=-1)
    a2 = jnp.concatenate([-pfi, -pbi, pfi, pbi], axis=-1)
    b1 = jnp.concatenate([bfr, bbr_t, bfi, bbi_t], axis=-1)
    b2 = jnp.concatenate([bfi, bbi_t, bfr, bbr_t], axis=-1)
    a16 = jnp.concatenate([pr[0, ..., t], pr[1, ..., t], pi[0, ..., t], pi[1, ..., t]], axis=1)
    return (lane_pad(pr), lane_pad(pi), lane_pad(bbr), lane_pad(bbi), lane_pad(ctr), lane_pad(cti), a1, a2, b1, b2), a16


def _split2(v):
    hi = v.astype(BF16)
    return hi, (v - hi.astype(F32)).astype(BF16)


def _dotx(a, b, mm=_mm):
    ah, al = (a, None) if a.dtype == BF16 else _split2(a)
    bh, bl = (b, None) if b.dtype == BF16 else _split2(b)
    out = mm(ah, bh)
    if bl is not None:
        out = out + mm(ah, bl)
    if al is not None:
        out = out + mm(al, bh)
    return out


def _shift_lanes(v, k):
    if k == 0:
        return v
    n = v.shape[-1]
    lane = lax.broadcasted_iota(jnp.int32, v.shape, v.ndim - 1)
    r = pltpu.roll(v, k % n, axis=v.ndim - 1)
    return jnp.where(lane >= k, r, 0.0) if k > 0 else jnp.where(lane < n + k, r, 0.0)


def _op_consts():
    t, h = CHUNK_T, SSM_H
    j, o = np.divmod(np.arange(t * h), h)
    p = np.arange(LANES)[:, None]
    sel = lambda col: (p == col[None, :]).astype(np.float32)
    pw = np.stack([np.concatenate([sel(j), sel(j + 1)], axis=1), np.concatenate([sel(t - 1 - j), sel(t - j)], axis=1)])
    return jnp.asarray(pw, BF16), jnp.asarray(sel(o), BF16), jnp.asarray(sel(j).T, BF16), jnp.asarray(sel(o).T, BF16)


def _op_factors(gi, pr_ref, pi_ref, br_ref, bi_ref, cr_ref, ci_ref, pw_ref, ch_ref):
    n = pr_ref.shape[2]
    c16 = _dotx(jnp.concatenate([cr_ref[gi, 0], ci_ref[gi, 0], cr_ref[gi, 1], ci_ref[gi, 1]], axis=0), ch_ref[...])
    out = []
    for d in range(2):
        e = _dotx(jnp.concatenate([pr_ref[gi, d], pi_ref[gi, d]], axis=0), pw_ref[d])
        c = c16[2 * d * n : (2 * d + 1) * n], c16[(2 * d + 1) * n : (2 * d + 2) * n]
        bst = jnp.concatenate([br_ref[gi, d], -bi_ref[gi, d]], axis=0)
        out.append(((e[:n, :CW], e[n:, :CW]), (e[:n, CW:], e[n:, CW:]), c, bst))
    return out


def _op_spread(m_ref, u_ref, v_ref, gi):
    return _dotx(m_ref[...], jnp.concatenate([_rows_pad(u_ref[gi]), _rows_pad(v_ref[gi])], axis=1))


def _cmul(x, y):
    return x[0] * y[0] - x[1] * y[1], x[0] * y[1] + x[1] * y[0]


def _cmul_conj(g, y):
    return g[0] * y[0] + g[1] * y[1], g[1] * y[0] - g[0] * y[1]


def _rows_pad(v):
    return jnp.concatenate([v, jnp.zeros((LANES - v.shape[0], v.shape[1]), F32)], axis=0)


def _s5_build(small, gb, shards):
    pr, pi, br, bi, cr, ci, a1, a2, b1, b2 = small
    g, _, n, _ = pr.shape
    t, h, ns = CHUNK_T, SSM_H, 4 * n
    pw, ch, rep_s, til_s = _op_consts()
    nr, nin, steps = len(shards), 14, g // gb

    def body(*refs):
        pr_ref, pi_ref, br_ref, bi_ref, cr_ref, ci_ref, a1_ref, a2_ref, b1_ref, b2_ref, pw_ref, ch_ref, rs_ref, ts_ref = refs[:nin]
        t_ref, w_ref, c_ref = refs[nin + nr : nin + nr + 3]
        start, finish = _gather_copies(refs[nin : nin + nr], refs[nin + nr + 3 : nin + 2 * nr + 3], *refs[nin + 2 * nr + 3 :])
        pl.when(pl.program_id(0) == 0)(start)

        def group(gi, carry):
            rows, outs = [], []
            for e16, e1, c16, bst in _op_factors(gi, pr_ref, pi_ref, br_ref, bi_ref, cr_ref, ci_ref, pw_ref, ch_ref):
                rows.append(_dotx(bst, jnp.concatenate(_cmul(c16, e16), axis=0), _mm_tn)[:h])
                outs.append(_cmul(c16, e1))
            c_ref[gi] = jnp.concatenate([outs[0][0], outs[1][0], -outs[0][1], -outs[1][1]], axis=0).astype(BF16)
            for s in range(t):
                t_ref[gi, s * h : (s + 1) * h, :] = (_shift_lanes(rows[0], h * s) + _shift_lanes(rows[1], -h * (t - 1 - s))).astype(BF16)
            ae, be = _op_spread(rs_ref, a1_ref, a2_ref, gi), _op_spread(ts_ref, b1_ref, b2_ref, gi)
            w_ref[gi] = (ae[:, :ns] * be[:, :ns] + ae[:, ns:] * be[:, ns:]).astype(BF16)
            return carry

        lax.fori_loop(0, gb, group, 0)
        pl.when(pl.program_id(0) == steps - 1)(finish)

    sm = pl.BlockSpec((gb, 2, n, LANES), lambda i: (i, 0, 0, 0))
    ab = lambda v: pl.BlockSpec((gb,) + v.shape[1:], lambda i: (i, 0, 0))
    out = pl.pallas_call(
        body,
        name="s5_build",
        grid=(steps,),
        in_specs=[sm] * 6 + [ab(a1), ab(a2), ab(b1), ab(b2), _whole(pw.shape), _whole(ch.shape), _whole(rep_s.shape), _whole(til_s.shape)] + [_HBM] * nr,
        out_specs=[_gspec(gb, CW, CW), _gspec(gb, CW, 4 * n), _gspec(gb, 4 * n, CW)] + [_HBM] * nr,
        out_shape=[_sds((g, CW, CW), BF16), _sds((g, CW, 4 * n), BF16), _sds((g, 4 * n, CW), BF16)] + _gather_shapes(shards),
        scratch_shapes=_gather_sems(nr),
        compiler_params=_cparams(1, VMEM_BIG),
    )(pr, pi, br, bi, cr, ci, a1, a2, b1, b2, pw, ch, rep_s, til_s, *shards)
    return out[:3], out[3:]


def _s5_build_bwd(small, dtsum, dwst, dccat, gb, parts):
    pr, pi, br, bi, cr, ci, a1, a2, b1, b2 = small
    g, _, n, _ = pr.shape
    t, h, ns = CHUNK_T, SSM_H, 4 * n
    pw, ch, rep_s, til_s = _op_consts()
    nr, nin, nout, steps = len(parts), 17, 10, g // gb

    def body(*refs):
        pr_ref, pi_ref, br_ref, bi_ref, cr_ref, ci_ref, a1_ref, a2_ref, b1_ref, b2_ref, pw_ref, ch_ref, rs_ref, ts_ref, dt_ref, dw_ref, dc_ref = refs[:nin]
        dpr_ref, dpi_ref, dbr_ref, dbi_ref, dcr_ref, dci_ref, da1_ref, da2_ref, db1_ref, db2_ref = refs[nin + nr : nin + nr + nout]
        start, finish = _exchange_copies(refs[nin : nin + nr], refs[nin + nr + nout : nin + 2 * nr + nout], *refs[nin + 2 * nr + nout :])
        pl.when(pl.program_id(0) == 0)(start)

        def group(gi, carry):
            drow = [jnp.zeros((h, CW), F32), jnp.zeros((h, CW), F32)]
            for s in range(t):
                blk = dt_ref[gi, s * h : (s + 1) * h, :]
                drow[0] = drow[0] + _shift_lanes(blk, -h * s)
                drow[1] = drow[1] + _shift_lanes(blk, h * (t - 1 - s))
            dcc = dc_ref[gi]
            dcs = []
            for d, (e16, e1, c16, bst) in enumerate(_op_factors(gi, pr_ref, pi_ref, br_ref, bi_ref, cr_ref, ci_ref, pw_ref, ch_ref)):
                dr = _rows_pad(drow[d])
                db = _dotx(jnp.concatenate(_cmul(c16, e16), axis=0), dr, _mm_nt)
                dbr_ref[gi, d] = db[:n]
                dbi_ref[gi, d] = -db[n:]
                dst = _dotx(bst, dr)
                dca = dst[:n], dst[n:]
                gx = dcc[d * n : (d + 1) * n], -dcc[(2 + d) * n : (3 + d) * n]
                dc16, dc16b = _cmul_conj(dca, e16), _cmul_conj(gx, e1)
                dcs += [dc16[0] + dc16b[0], dc16[1] + dc16b[1]]
                de = jnp.concatenate([jnp.concatenate(_cmul_conj(dca, c16), axis=0), jnp.concatenate(_cmul_conj(gx, c16), axis=0)], axis=1)
                dp = _dotx(de, pw_ref[d], _mm_nt)
                dpr_ref[gi, d] = dp[:n]
                dpi_ref[gi, d] = dp[n:]
            dct = _dotx(jnp.concatenate(dcs, axis=0), ch_ref[...], _mm_nt)
            for d in range(2):
                dcr_ref[gi, d] = dct[2 * d * n : (2 * d + 1) * n]
                dci_ref[gi, d] = dct[(2 * d + 1) * n : (2 * d + 2) * n]
            dw = dw_ref[gi]
            ae, be = _op_spread(rs_ref, a1_ref, a2_ref, gi), _op_spread(ts_ref, b1_ref, b2_ref, gi)
            da = _dotx(rs_ref[...], jnp.concatenate([dw * be[:, :ns], dw * be[:, ns:]], axis=1), _mm_tn)
            db = _dotx(ts_ref[...], jnp.concatenate([dw * ae[:, :ns], dw * ae[:, ns:]], axis=1), _mm_tn)
            da1_ref[gi] = da[:t, :ns]
            da2_ref[gi] = da[:t, ns:]
            db1_ref[gi] = db[:h, :ns]
            db2_ref[gi] = db[:h, ns:]
            return carry

        lax.fori_loop(0, gb, group, 0)
        pl.when(pl.program_id(0) == steps - 1)(finish)

    sm = pl.BlockSpec((gb, 2, n, LANES), lambda i: (i, 0, 0, 0))
    ab = lambda v: pl.BlockSpec((gb,) + v.shape[1:], lambda i: (i, 0, 0))
    ops = [_gspec(gb, CW, CW), _gspec(gb, CW, 4 * n), _gspec(gb, 4 * n, CW)]
    out = pl.pallas_call(
        body,
        name="s5_build_bwd",
        grid=(steps,),
        in_specs=[sm] * 6 + [ab(a1), ab(a2), ab(b1), ab(b2), _whole(pw.shape), _whole(ch.shape), _whole(rep_s.shape), _whole(til_s.shape)] + ops + [_HBM] * nr,
        out_specs=[sm] * 6 + [ab(a1), ab(a2), ab(b1), ab(b2)] + [_HBM] * nr,
        out_shape=[_sds(v.shape) for v in small] + [_sds(p.shape, p.dtype) for p in parts],
        scratch_shapes=_exchange_sems(nr),
        compiler_params=_cparams(1, VMEM_BIG),
    )(pr, pi, br, bi, cr, ci, a1, a2, b1, b2, pw, ch, rep_s, til_s, dtsum, dwst, dccat, *parts)
    return out[:nout], out[nout:]


SCAN_UNROLL = 4
GPL = LANES // SSM_H
CW = CHUNK_T * SSM_H
BW = CHUNK_T * LANES


def _chunk_perm():
    o = np.arange(BW)
    src = ((o % CW) // SSM_H) * LANES + (o // CW) * SSM_H + o % SSM_H
    return jnp.asarray(np.arange(BW)[:, None] == src[None, :], BF16)


def _gspec(gb, a, b):
    return pl.BlockSpec((gb, a, b), lambda i: (i, 0, 0))


def _chunk_rows(ref, n):
    return jnp.concatenate([ref[pl.ds(s, n, stride=CHUNK_T), :] for s in range(CHUNK_T)], axis=1)


def _s5_state(proj, col0, uc, perm, wst, ncc, ncl):
    nlb = uc.shape[1] // LANES
    g, _, ns = wst.shape
    nch = ncc + ncl
    half = ns // 2

    def body(p_ref, c_ref, pm_ref, w_ref, u_ref, sr_ref, si_ref):
        xrows = jnp.concatenate([_chunk_rows(c_ref, ncc), _chunk_rows(p_ref, ncl)], axis=0).astype(BF16)
        u = _mm(xrows, pm_ref[...]).astype(BF16)
        u_ref[0] = u
        for gi in range(GPL):
            s = _mm(u[:, gi * CW : (gi + 1) * CW], w_ref[gi].astype(BF16))
            sr_ref[gi] = s[:, :half]
            si_ref[gi] = s[:, half:]

    return pl.pallas_call(
        body,
        name="s5_state",
        grid=(nlb,),
        in_specs=[
            pl.BlockSpec((ncl * CHUNK_T, LANES), lambda i: (0, col0 + i)),
            pl.BlockSpec((ncc * CHUNK_T, LANES), lambda i: (0, i)),
            _whole(perm.shape),
            _gspec(GPL, CW, ns),
        ],
        out_specs=[pl.BlockSpec((1, nch, BW), lambda i: (i, 0, 0)), _gspec(GPL, nch, half), _gspec(GPL, nch, half)],
        out_shape=[_sds((nlb, nch, BW), BF16), _sds((g, nch, half)), _sds((g, nch, half))],
        compiler_params=_cparams(1, VMEM_BIG),
    )(proj, uc, perm, wst)


def _unrolled_loop(n, step, init):
    u = SCAN_UNROLL if n % SCAN_UNROLL == 0 else 1

    def trip(i, c):
        for k in range(u):
            c = step(i * u + k, c)
        return c

    return lax.fori_loop(0, n // u, trip, init)


def _lane_masks(gb, half):
    lane = lax.broadcasted_iota(jnp.int32, (gb, half), 1)
    return lane < (half // 2)


def _roll_half(v):
    return pltpu.roll(v, v.shape[-1] // 2, axis=v.ndim - 1)


def _pack_dirs(lo, hi, mf):
    return jnp.where(mf, lo, _roll_half(hi)), jnp.where(mf, _roll_half(lo), hi)


def _unpack_dirs(pf, pb, mf):
    return jnp.where(mf, pf, _roll_half(pb)), jnp.where(mf, _roll_half(pf), pb)


def _packed_spec(gb, nch, half, buffers=None):
    mode = {} if buffers is None else dict(pipeline_mode=pl.Buffered(buffers))
    if gb == 2 * GPL:
        return pl.BlockSpec((GPL, nch, half), lambda i: (i // 2, 0, 0), **mode)
    assert gb == GPL
    return pl.BlockSpec((gb // 2, nch, half), lambda i: (i, 0, 0), **mode)


def _pair_index(j, gb):
    if gb == 2 * GPL:
        return j, pl.program_id(0) % 2
    return j % (gb // 2), j // (gb // 2)


def _pick(which, lo, hi):
    return (hi if which else lo) if isinstance(which, int) else jnp.where(which == 0, lo, hi)


def _unpacked(pf_ref, pb_ref, j, gb, mf):
    jl, which = _pair_index(j, gb)
    return _pick(which, *_unpack_dirs(pf_ref[jl], pb_ref[jl], mf))


def _pair_tile(refs, jl, rows):
    return jnp.concatenate([r[jl, rows, :] for r in refs], axis=1)


def _pair_rows_of(op, which, q):
    z = jnp.zeros((q, op.shape[1]), op.dtype)
    blocks = [op[0:q], op[2 * q : 3 * q], op[q : 2 * q], op[3 * q : 4 * q]]
    lo = jnp.concatenate([x for b in blocks for x in (b, z)], axis=0)
    hi = jnp.concatenate([x for b in blocks for x in (z, b)], axis=0)
    return _pick(which, lo, hi)


def _group_rows_of(pair_rows, which, q):
    def take(off):
        blk = lambda k: pair_rows[2 * q * k + off : 2 * q * k + off + q]
        return jnp.concatenate([blk(0), blk(2), blk(1), blk(3)], axis=0)

    return _pick(which, take(0), take(q))


def _scan_rows(ref, row, hp, nch, mf):
    return _pack_dirs(ref[pl.ds(row, hp, stride=nch), :], ref[pl.ds(hp * nch + row, hp, stride=nch), :], mf)


def _s5_scan_fwd(s_re, s_im, a16, ncc, nch, gb):
    g, ns = a16.shape
    half = ns // 2
    hp = gb // 2

    def body(sr_ref, si_ref, a_ref, hfr_ref, hfi_ref, hbr_ref, hbi_ref):
        mf = _lane_masks(hp, half)
        afr, abr = _pack_dirs(a_ref[:hp, :half], a_ref[hp:, :half], mf)
        afi, abi = _pack_dirs(a_ref[:hp, half:], a_ref[hp:, half:], mf)
        zero = jnp.zeros((hp, half), F32)

        def step(t, c):
            fr, fi, br, bi = c
            rb = jnp.where(t < ncc, ncc - 1 - t, nch - 1 - (t - ncc))
            hfr_ref[pl.ds(t, hp, stride=nch), :] = fr
            hfi_ref[pl.ds(t, hp, stride=nch), :] = fi
            hbr_ref[pl.ds(rb, hp, stride=nch), :] = br
            hbi_ref[pl.ds(rb, hp, stride=nch), :] = bi
            sfr, sfi = _scan_rows(sr_ref, t, hp, nch, mf)[0], _scan_rows(si_ref, t, hp, nch, mf)[0]
            sbr, sbi = _scan_rows(sr_ref, rb, hp, nch, mf)[1], _scan_rows(si_ref, rb, hp, nch, mf)[1]
            return (afr * fr - afi * fi + sfr, afr * fi + afi * fr + sfi, abr * br - abi * bi + sbr, abr * bi + abi * br + sbi)

        _unrolled_loop(nch, step, (zero, zero, zero, zero))

    blk = pl.BlockSpec((gb * nch, half), lambda i: (i, 0))
    pblk = pl.BlockSpec((hp * nch, half), lambda i: (i, 0))
    return pl.pallas_call(
        body,
        name="s5_scan_fwd",
        grid=(g // gb,),
        in_specs=[blk, blk, pl.BlockSpec((gb, ns), lambda i: (i, 0))],
        out_specs=[pblk] * 4,
        out_shape=[_sds((g // 2 * nch, half))] * 4,
        compiler_params=_cparams(1, VMEM_BIG),
    )(s_re, s_im, a16)


def _token_rows_store(ref, val, n):
    for s in range(CHUNK_T):
        ref[pl.ds(s, n, stride=CHUNK_T), :] = val[:, s * LANES : (s + 1) * LANES]


def _s5_out(u_all, h4, tsum, ccat, perm, ncc, lx, gb):
    nlb, nch, _ = u_all.shape
    g, ns, _ = ccat.shape
    ncl = nch - ncc
    half = ns // 2

    def body(u_ref, hfr_ref, hfi_ref, hbr_ref, hbi_ref, t_ref, c_ref, pm_ref, y_ref):
        parts = []
        for gi in range(GPL):
            u = u_ref[0, ncc:, gi * CW : (gi + 1) * CW]
            jl, which = _pair_index(gi, gb)
            hs = _pair_tile((hfr_ref, hfi_ref, hbr_ref, hbi_ref), jl, slice(ncc, None)).astype(BF16)
            parts.append(_mm(u, t_ref[gi].astype(BF16)) + _mm(hs, _pair_rows_of(c_ref[gi].astype(BF16), which, ns // 4)))
        _token_rows_store(y_ref, _mm_nt(jnp.concatenate(parts, axis=1).astype(BF16), pm_ref[...]), ncl)

    return pl.pallas_call(
        body,
        name="s5_out",
        grid=(nlb,),
        in_specs=[pl.BlockSpec((1, nch, BW), lambda i: (i, 0, 0))] + [_packed_spec(gb, nch, half)] * 4 + [_gspec(GPL, CW, CW), _gspec(GPL, ns, CW), _whole(perm.shape)],
        out_specs=pl.BlockSpec((lx, LANES), lambda i: (0, i)),
        out_shape=_sds((lx, nlb * LANES)),
        compiler_params=_cparams(1, VMEM_BIG),
    )(u_all, *h4, tsum, ccat, perm)


def _s5_dstate(dypre, perm, ccat, ncc, swaps=()):
    lx, sw = dypre.shape
    nr = len(swaps)
    nlb = sw // LANES
    g, ns, _ = ccat.shape
    ncl = lx // CHUNK_T
    nch = ncl + ncc
    half = ns // 2

    def body(dy_ref, pm_ref, c_ref, dyr_ref, dr_ref, di_ref):
        dy = _mm(_chunk_rows(dy_ref, ncl).astype(BF16), pm_ref[...]).astype(BF16)
        dyr_ref[0] = dy
        for gi in range(GPL):
            dh = _mm_nt(dy[:, gi * CW : (gi + 1) * CW], c_ref[gi].astype(BF16))
            dr_ref[gi, :ncc, :] = jnp.zeros((ncc, half), F32)
            di_ref[gi, :ncc, :] = jnp.zeros((ncc, half), F32)
            dr_ref[gi, ncc:, :] = dh[:, :half]
            di_ref[gi, ncc:, :] = dh[:, half:]

    out = pl.pallas_call(
        _riding(body, 3, 3, nr, (nlb,), _swap_copies),
        name="s5_dstate",
        grid=(nlb,),
        in_specs=[pl.BlockSpec((lx, LANES), lambda i: (0, i)), _whole(perm.shape), _gspec(GPL, ns, CW)] + [_HBM] * nr,
        out_specs=[pl.BlockSpec((1, ncl, BW), lambda i: (i, 0, 0)), _gspec(GPL, nch, half), _gspec(GPL, nch, half)] + [_HBM] * nr,
        out_shape=[_sds((nlb, ncl, BW), BF16), _sds((g, nch, half)), _sds((g, nch, half))] + _swap_shapes(swaps),
        scratch_shapes=_swap_sems(nr) if nr else [],
        compiler_params=_cparams(1, VMEM_BIG),
    )(dypre, perm, ccat, *swaps)
    return out[:3], out[3:]


def _s5_scan_bwd(dh_re, dh_im, a16, ncc, nch, gb):
    g, ns = a16.shape
    half = ns // 2
    ncl = nch - ncc
    hp = gb // 2

    def body(dr_ref, di_ref, a_ref, sfr_ref, sfi_ref, sbr_ref, sbi_ref):
        mf = _lane_masks(hp, half)
        afr, abr = _pack_dirs(a_ref[:hp, :half], a_ref[hp:, :half], mf)
        afi, abi = _pack_dirs(a_ref[:hp, half:], a_ref[hp:, half:], mf)
        zero = jnp.zeros((hp, half), F32)

        def step(t, c):
            lfr, lfi, lbr, lbi = c
            pf = nch - 1 - t
            pb = jnp.where(t < ncl, ncc + t, t - ncl)
            sfr_ref[pl.ds(pf, hp, stride=nch), :] = lfr
            sfi_ref[pl.ds(pf, hp, stride=nch), :] = lfi
            sbr_ref[pl.ds(pb, hp, stride=nch), :] = lbr
            sbi_ref[pl.ds(pb, hp, stride=nch), :] = lbi
            dfr, dfi = _scan_rows(dr_ref, pf, hp, nch, mf)[0], _scan_rows(di_ref, pf, hp, nch, mf)[0]
            dbr, dbi = _scan_rows(dr_ref, pb, hp, nch, mf)[1], _scan_rows(di_ref, pb, hp, nch, mf)[1]
            return (dfr + afr * lfr + afi * lfi, dfi + afr * lfi - afi * lfr, dbr + abr * lbr + abi * lbi, dbi + abr * lbi - abi * lbr)

        _unrolled_loop(nch, step, (zero,) * 4)

    blk = pl.BlockSpec((gb * nch, half), lambda i: (i, 0))
    pblk = pl.BlockSpec((hp * nch, half), lambda i: (i, 0))
    return pl.pallas_call(
        body,
        name="s5_scan_bwd",
        grid=(g // gb,),
        in_specs=[blk, blk, pl.BlockSpec((gb, ns), lambda i: (i, 0))],
        out_specs=[pblk] * 4,
        out_shape=[_sds((g // 2 * nch, half))] * 4,
        compiler_params=_cparams(1, VMEM_BIG),
    )(dh_re, dh_im, a16)


def _s5_du(u_all, dy_all, ds4, tsum, wst, perm, ncc, lx, lc, gb):
    nlb, ncl, _ = dy_all.shape
    g, _, ns = wst.shape
    nch = ncc + ncl
    half = ns // 2

    def body(u_ref, dy_ref, sfr_ref, sfi_ref, sbr_ref, sbi_ref, t_ref, w_ref, pm_ref, du_ref, dc_ref, dw_ref):
        parts = []
        mf = _lane_masks(nch, half)
        for gi in range(GPL):
            ds = jnp.concatenate([_unpacked(sfr_ref, sbr_ref, gi, gb, mf), _unpacked(sfi_ref, sbi_ref, gi, gb, mf)], axis=1).astype(BF16)
            dw_ref[gi] = _mm_tn(u_ref[0, :, gi * CW : (gi + 1) * CW], ds)
            d_all = _mm_nt(ds, w_ref[gi].astype(BF16))
            d_lat = d_all[ncc:] + _mm_nt(dy_ref[0, :, gi * CW : (gi + 1) * CW], t_ref[gi].astype(BF16))
            parts.append(jnp.concatenate([d_all[:ncc], d_lat], axis=0))
        du = _mm_nt(jnp.concatenate(parts, axis=1).astype(BF16), pm_ref[...])
        _token_rows_store(dc_ref, du[:ncc], ncc)
        _token_rows_store(du_ref, du[ncc:], ncl)

    return pl.pallas_call(
        body,
        name="s5_du",
        grid=(nlb,),
        in_specs=[pl.BlockSpec((1, nch, BW), lambda i: (i, 0, 0)), pl.BlockSpec((1, ncl, BW), lambda i: (i, 0, 0))]
        + [_packed_spec(gb, nch, half)] * 4
        + [_gspec(GPL, CW, CW), _gspec(GPL, CW, ns), _whole(perm.shape)],
        out_specs=[pl.BlockSpec((lx, LANES), lambda i: (0, i)), pl.BlockSpec((lc, LANES), lambda i: (0, i)), _gspec(GPL, CW, ns)],
        out_shape=[_sds((lx, nlb * LANES)), _sds((lc, nlb * LANES)), _sds((g, CW, ns))],
        compiler_params=_cparams(1, VMEM_BIG),
    )(u_all, dy_all, *ds4, tsum, wst, perm)


def _s5_op_grads(u_all, dy_all, ds4, h4, ncc, gb):
    nlb, nch, _ = u_all.shape
    half = ds4[0].shape[2]
    g = 2 * ds4[0].shape[0]
    ns = 2 * half

    def body(u_ref, dy_ref, sfr_ref, sfi_ref, sbr_ref, sbi_ref, hfr_ref, hfi_ref, hbr_ref, hbi_ref, dt_ref, dc_ref, da_ref):
        das = []
        mf1 = _lane_masks(1, half)
        for gi in range(GPL):
            cols = slice(gi * CW, (gi + 1) * CW)
            jl, which = _pair_index(gi, gb)
            dy = dy_ref[0, :, cols]
            dt_ref[gi] = _mm_tn(u_ref[0, ncc:, cols], dy)
            hs = _pair_tile((hfr_ref, hfi_ref, hbr_ref, hbi_ref), jl, slice(ncc, None)).astype(BF16)
            dc_ref[gi] = _group_rows_of(_mm_tn(hs, dy), which, ns // 4)
            hfr, hfi, hbr, hbi = hfr_ref[jl], hfi_ref[jl], hbr_ref[jl], hbi_ref[jl]
            sfr, sfi, sbr, sbi = sfr_ref[jl], sfi_ref[jl], sbr_ref[jl], sbi_ref[jl]
            a_re = _pick(which, *_unpack_dirs(_colsum(hfr * sfr + hfi * sfi), _colsum(hbr * sbr + hbi * sbi), mf1))
            a_im = _pick(which, *_unpack_dirs(_colsum(hfr * sfi - hfi * sfr), _colsum(hbr * sbi - hbi * sbr), mf1))
            das.append(jnp.concatenate([a_re, a_im], axis=1))
        da_ref[...] = jnp.concatenate(das, axis=0)

    return pl.pallas_call(
        body,
        name="s5_op_grads",
        grid=(nlb,),
        in_specs=[pl.BlockSpec((1, nch, BW), lambda i: (i, 0, 0)), pl.BlockSpec((1, nch - ncc, BW), lambda i: (i, 0, 0))]
        + [_packed_spec(gb, nch, half)] * 4
        + [_packed_spec(gb, nch, half, buffers=1)] * 4,
        out_specs=[_gspec(GPL, CW, CW), _gspec(GPL, ns, CW), pl.BlockSpec((GPL, ns), lambda i: (i, 0))],
        out_shape=[_sds((g, CW, CW)), _sds((g, ns, CW)), _sds((g, ns))],
        compiler_params=_cparams(1, VMEM_BIG),
    )(u_all, dy_all, *ds4, *h4)


def _merge_fwd(y_ssm, proj, lin, x, tgt, glu_wg, out_w, d_skip, glu_b, pscale, gate, fin_g, tm):
    lx, d = x.shape
    sw = y_ssm.shape[1]
    mix = out_w.shape[0]
    pw = mix - sw
    cb2 = glu_wg.shape[2]
    nh = NDEV // 2

    def body(y_ref, u_ref, z_ref, l_ref, x_ref, t_ref, gw_ref, ow_ref, dk_ref, gb_ref, ps_ref, gt_ref, fg_ref,
             ypre_ref, y12_ref, br_ref, dxo_ref, loss_ref, dfg_ref, dgt_ref):
        @pl.when(pl.program_id(0) == 0)
        def _():
            loss_ref[...] = jnp.zeros_like(loss_ref)
            dfg_ref[...] = jnp.zeros_like(dfg_ref)
            dgt_ref[...] = jnp.zeros_like(dgt_ref)

        ypre = y_ref[...] + dk_ref[...] * u_ref[...]
        ypre_ref[...] = ypre
        yg = _gelu(ypre).astype(BF16)
        outs = []
        for j in range(nh):
            y1 = _mm(yg, gw_ref[j]) + gb_ref[:, j * cb2 : (j + 1) * cb2]
            y2 = _mm(yg, gw_ref[nh + j]) + gb_ref[:, (nh + j) * cb2 : (nh + j + 1) * cb2]
            y12_ref[:, j * cb2 : (j + 1) * cb2] = y1
            y12_ref[:, (nh + j) * cb2 : (nh + j + 1) * cb2] = y2
            outs.append(y1 * _sigmoid(y2))
        lin = l_ref[...] * ps_ref[...]
        brb = (jnp.concatenate([lin] + outs, axis=1) * _silu(z_ref[...])).astype(BF16)
        br_ref[...] = brb
        mixv = _mm(brb, ow_ref[...])
        xo = x_ref[...] + gt_ref[...] * mixv
        r2 = lax.rsqrt(_rowmean(xo * xo) + EPS)
        xh = xo * r2
        fg = fg_ref[...]
        e = xh * fg - t_ref[...]
        loss_ref[...] += 0.5 * _colsum(_rowmean(e * e))
        dy = e * (1.0 / d)
        dfg_ref[...] += _colsum(dy * xh)
        gy = dy * fg
        dxo = r2 * (gy - xh * _rowmean(gy * xh))
        dxo_ref[...] = dxo
        dgt_ref[...] += _colsum(dxo * mixv)

    vec = _whole((1, d))
    return pl.pallas_call(
        body,
        name="merge_fwd",
        grid=(lx // tm,),
        in_specs=[_rows(tm, sw), _rows(tm, sw, 1), _rows(tm, mix, 1), _rows(tm, pw), _rows(tm, d), _rows(tm, d), _whole(glu_wg.shape), _whole(out_w.shape), _whole((1, sw)), _whole((1, 2 * sw)), _whole((1, pw)), vec, vec],
        out_specs=[_rows(tm, sw), _rows(tm, 2 * sw), _rows(tm, mix), _rows(tm, d), _acc((1, 1)), _acc((1, d)), _acc((1, d))],
        out_shape=[_sds((lx, sw)), _sds((lx, 2 * sw)), _sds((lx, mix), BF16), _sds((lx, d)), _sds((1, 1)), _sds((1, d)), _sds((1, d))],
        compiler_params=_cparams(1, VMEM_BIG),
    )(y_ssm, proj, proj, lin, x, tgt, glu_wg, out_w, d_skip, glu_b, pscale, gate, fin_g)


def _out_bwd(dxo, gate, branch, out_w, lin, y12, proj, pscale, tm):
    lx, d = dxo.shape
    mix = out_w.shape[0]
    pw = lin.shape[1]
    sw = mix - pw
    nb = lx // tm

    def body(dxo_ref, gt_ref, br_ref, ow_ref, l_ref, y_ref, z_ref, ps_ref, dz_ref, dl_ref, dg_ref, dps_ref, dow_hbm, acc, sem):
        i = pl.program_id(0)

        @pl.when(i == 0)
        def _():
            acc[...] = jnp.zeros_like(acc)
            dps_ref[...] = jnp.zeros_like(dps_ref)

        dmix = (dxo_ref[...] * gt_ref[...]).astype(BF16)
        dbr = _mm_nt(dmix, ow_ref[...])
        acc[...] += _mm_tn(br_ref[...], dmix)
        z = z_ref[...]
        sg = _sigmoid(z)
        dbp = dbr * (z * sg)
        y1, y2 = y_ref[:, :sw], y_ref[:, sw:]
        s2 = _sigmoid(y2)
        ps = ps_ref[...]
        lin = l_ref[...]
        bp = jnp.concatenate([lin * ps, y1 * s2], axis=1)
        dz_ref[...] = dbr * bp * (sg * (1.0 + z * (1.0 - sg)))
        dlp = dbp[:, :pw]
        dl_ref[...] = dlp * ps
        dps_ref[...] += _colsum(dlp * lin)
        dss = dbp[:, pw:]
        dg_ref[...] = jnp.concatenate([dss * s2, dss * y1 * s2 * (1.0 - s2)], axis=1).astype(BF16)

        @pl.when(i == nb - 1)
        def _():
            cp = pltpu.make_async_copy(acc, dow_hbm, sem)
            cp.start()
            cp.wait()

    return pl.pallas_call(
        body,
        name="out_bwd",
        grid=(nb,),
        in_specs=[_rows(tm, d), _whole((1, d)), _rows(tm, mix), _whole(out_w.shape), _rows(tm, pw), _rows(tm, 2 * sw), _rows(tm, mix, 1), _whole((1, pw))],
        out_specs=[_rows(tm, mix), _rows(tm, pw), _rows(tm, 2 * sw), _acc((1, pw)), _ANY],
        out_shape=[_sds((lx, mix)), _sds((lx, pw)), _sds((lx, 2 * sw), BF16), _sds((1, pw)), _sds((mix, d))],
        scratch_shapes=[pltpu.VMEM((mix, d), F32), pltpu.SemaphoreType.DMA],
        compiler_params=_cparams(1, VMEM_BIG),
    )(dxo, gate, branch, out_w, lin, y12, proj, pscale)


def _glu_bwd(dg12, ypre, proj, glu_wg, d_skip, tm):
    lx, sw = ypre.shape
    cb2 = glu_wg.shape[2]
    nb = lx // tm

    def body(dg_ref, yp_ref, u_ref, gw_ref, dk_ref, dyp_ref, dgb_ref, dd_ref, dgw_hbm, acc, sem):
        i = pl.program_id(0)

        @pl.when(i == 0)
        def _():
            acc[...] = jnp.zeros_like(acc)
            dgb_ref[...] = jnp.zeros_like(dgb_ref)
            dd_ref[...] = jnp.zeros_like(dd_ref)

        ypre = yp_ref[...]
        ygb = _gelu(ypre).astype(BF16)
        dg = dg_ref[...]
        dyg = jnp.zeros((tm, sw), F32)
        for j in range(NDEV):
            dgj = dg[:, j * cb2 : (j + 1) * cb2]
            dyg = dyg + _mm_nt(dgj, gw_ref[j])
            acc[j] += _mm_tn(ygb, dgj)
        dgb_ref[...] += _colsum(dg.astype(F32))
        dyp = dyg * _gelu_grad(ypre)
        dyp_ref[...] = dyp
        dd_ref[...] += _colsum(dyp * u_ref[...])

        @pl.when(i == nb - 1)
        def _():
            cp = pltpu.make_async_copy(acc, dgw_hbm, sem)
            cp.start()
            cp.wait()

    return pl.pallas_call(
        body,
        name="glu_bwd",
        grid=(nb,),
        in_specs=[_rows(tm, 2 * sw), _rows(tm, sw), _rows(tm, sw, 1), _whole(glu_wg.shape), _whole((1, sw))],
        out_specs=[_rows(tm, sw), _acc((1, 2 * sw)), _acc((1, sw)), _ANY],
        out_shape=[_sds((lx, sw)), _sds((1, 2 * sw)), _sds((1, sw)), _sds(glu_wg.shape)],
        scratch_shapes=[pltpu.VMEM(glu_wg.shape, F32), pltpu.SemaphoreType.DMA],
        compiler_params=_cparams(1, VMEM_BIG),
    )(dg12, ypre, proj, glu_wg, d_skip)


def _pack(parts, total):
    flat = jnp.concatenate([p.reshape(-1).astype(F32) for p in parts])
    return jnp.pad(flat, (0, total - flat.shape[0]))


def _unpack(flat, shapes):
    out, off = [], 0
    for s in shapes:
        n = int(np.prod(s)) if len(s) else 1
        out.append(flat[off : off + n].reshape(s))
        off += n
    return out


def kernel(x, c, ctx, c_ctx, ada_w, ada_b, norm_g, in_w, pool_w, pool_scale, s5_lam_re, s5_lam_im, s5_log_dt, s5_b_re, s5_b_im, s5_c_re, s5_c_im, s5_d, glu_w, glu_b, out_w, final_g, loss_target, m_c_ctx, m_ada_w, m_ada_b, m_norm_g, m_in_w, m_pool_w, m_pool_scale, m_s5_lam_re, m_s5_lam_im, m_s5_log_dt, m_s5_b_re, m_s5_b_im, m_s5_c_re, m_s5_c_im, m_s5_d, m_glu_w, m_glu_b, m_out_w, m_final_g, v_c_ctx, v_ada_w, v_ada_b, v_norm_g, v_in_w, v_pool_w, v_pool_scale, v_s5_lam_re, v_s5_lam_im, v_s5_log_dt, v_s5_b_re, v_s5_b_im, v_s5_c_re, v_s5_c_im, v_s5_d, v_glu_w, v_glu_b, v_out_w, v_final_g):
    xr, tgt, xc = x[0], loss_target[0], ctx[0]
    lx, d = xr.shape
    lc = xc.shape[0]
    mix = out_w.shape[1] * NDEV
    sw = glu_w.shape[1]
    pw = mix - sw
    pgw = pw // len(POOL_WINDOWS)
    ngrp = sw // SSM_H
    cb = in_w.shape[2]
    cb2 = glu_w.shape[2]
    nmod = ada_w.shape[2]
    ncc, ncl = lc // CHUNK_T, lx // CHUNK_T
    nch = ncc + ncl
    tm = min(256, lx)
    tm_acc = min(512, lx)
    assert mix == d and pw == sw and 2 * cb == pw and NDEV * cb2 == 2 * sw and lx % GRID_W == 0
    mx, my, mc = _my_pos()
    me = 4 * mx + 2 * my + mc

    (c_all,) = _all_gather([c], "gather_c")
    call = jnp.concatenate([c_all.reshape(NDEV, d), c_ctx.reshape(1, d), jnp.zeros((NDEV - 1, d), F32)], axis=0)
    ada_w_l = ada_w[0]
    ada_b_l = lax.dynamic_slice_in_dim(ada_b, me * nmod, nmod, axis=1)
    m_loc = _mod_fwd(call, ada_w_l, ada_b_l)
    (m_all,) = _all_gather([m_loc], "gather_mod")
    mod = lax.dynamic_index_in_dim(m_all, me, axis=1, keepdims=False).reshape(1, NDEV * nmod)
    mod_c = m_all[:, NDEV, :].reshape(1, NDEV * nmod)
    shift, scale, gate = mod[:, :d], mod[:, d : 2 * d], mod[:, 2 * d :]
    shift_c, scale_c = mod_c[:, :d], mod_c[:, d : 2 * d]

    s5_params = tuple(p[0] for p in (s5_lam_re, s5_lam_im, s5_log_dt, s5_b_re, s5_b_im, s5_c_re, s5_c_im))
    (small, a16), small_vjp = jax.vjp(_s5_small, *s5_params)
    gb_ops = min(8, ngrp)
    (tsum, wst, ccat), (in_wg,) = _s5_build(small, gb_ops, [in_w[0].astype(BF16)])

    ssm_blocks = (2, 3)
    (proj, hb), (glu_wg, out_wg, pool_wg) = _in_proj(
        xr, norm_g, scale, shift, in_wg, tuple(range(NDEV)), tm, "in_proj",
        shards=[glu_w[0].astype(BF16), out_w[0].astype(BF16), pool_w[0].reshape(-1, pgw).astype(BF16)],
    )
    out_wf = out_wg.reshape(mix, d)
    pool_wf = pool_wg.reshape(NDEV, len(POOL_WINDOWS), pgw // NDEV, pgw).transpose(1, 0, 2, 3).reshape(len(POOL_WINDOWS), pgw, pgw)
    (uc, hcb), _ = _in_proj(xc, norm_g, scale_c, shift_c, in_wg, ssm_blocks, min(tm, lc), "in_proj_ctx")
    lin, dmat = _pool_fwd(proj, pool_wf)

    gb_scan = min(16, ngrp)
    half = wst.shape[2] // 2
    perm = _chunk_perm()
    flat = lambda a: a.reshape(ngrp * nch, half)
    pairs = lambda arrs: [a.reshape(ngrp // 2, nch, half) for a in arrs]
    u_all, s_re, s_im = _s5_state(proj, pw // LANES, uc, perm, wst, ncc, ncl)
    h4 = pairs(_s5_scan_fwd(flat(s_re), flat(s_im), a16, ncc, nch, gb_scan))
    y_ssm = _s5_out(u_all, h4, tsum, ccat, perm, ncc, lx, gb_scan)

    ypre, y12, branch, dxo, loss_l, dfg, dgate = _merge_fwd(
        y_ssm, proj, lin, xr, tgt, glu_wg, out_wf, s5_d, glu_b, pool_scale, gate, final_g.reshape(1, d), tm
    )

    dz, dlin, dg12, dps, d_out_w = _out_bwd(dxo, gate, branch, out_wf, lin, y12, proj, pool_scale, tm)
    dypre, dglu_b, dd_skip, d_glu_w = _glu_bwd(dg12, ypre, proj, glu_wg, s5_d, tm_acc)

    d_glu_out = [d_glu_w, d_out_w.reshape(NDEV, mix // NDEV, d)]
    (dy_all, dh_re, dh_im), got_glu_out = _s5_dstate(dypre, perm, ccat, ncc, swaps=[_by_chip_and_core(a) for a in d_glu_out])
    ds4 = pairs(_s5_scan_bwd(flat(dh_re), flat(dh_im), a16, ncc, nch, gb_scan))
    du5, duc, dwst = _s5_du(u_all, dy_all, ds4, tsum, wst, perm, ncc, lx, lc, gb_scan)
    dtsum, dccat, da16 = _s5_op_grads(u_all, dy_all, ds4, h4, ncc, gb_scan)
    dup, dpool_w = _pool_bwd(dlin, pool_wf, dmat)

    npw = len(POOL_WINDOWS)
    dpool_s = dpool_w.reshape(npw, NDEV, pgw // NDEV, pgw).transpose(1, 0, 2, 3).reshape(NDEV, npw * (pgw // NDEV), pgw)
    parts_a = _pair_partials(d_glu_out, [BF16] * 2, "grads_a", got=got_glu_out) + _pair_partials([dpool_s], [BF16], "grads_p")
    dpcb, dsc_c, dsh_c, dg_c = _in_bwd_ctx(xc, duc, in_wg, ssm_blocks, norm_g, scale_c, shift_c)
    grad_x, dpb, dsc, dsh, dnorm_g = _in_bwd(xr, dxo, dup, du5, dypre, dz, in_wg, norm_g, scale, shift, s5_d, dg_c, tm)
    d_in_w, (land_glu, land_out, land_pool) = _in_w_grad(hb, dpb, hcb, dpcb, cb, min(2 * tm_acc, lx), parts=parts_a)

    dmod = jnp.concatenate(
        [jnp.concatenate([dsh, dsc, dgate], axis=1), jnp.concatenate([dsh_c, dsc_c, jnp.zeros((1, d), F32)], axis=1)], axis=0
    )
    (dmod_all,) = _all_gather([dmod], "gather_dmod")
    dmod_l = lax.dynamic_slice_in_dim(dmod_all, me * nmod, nmod, axis=2)
    g_ada_w, g_ada_b_l, cctx_part = _mod_bwd(call, dmod_l[:, 0, :], dmod_l[:, 1, :], ada_w_l)

    parts_b = _pair_partials([d_in_w], [BF16], "grads_b")
    dsmall, (land_in,) = _s5_build_bwd(small, dtsum, dwst, dccat, gb_ops, parts_b)
    ds5 = small_vjp((tuple(dsmall), da16))

    small_names = ["norm_g", "pool_scale", "s5_lam_re", "s5_lam_im", "s5_log_dt", "s5_b_re", "s5_b_im", "s5_c_re", "s5_c_im", "s5_d", "glu_b", "final_g", "c_ctx"]
    small_w = dict(norm_g=norm_g, pool_scale=pool_scale, s5_lam_re=s5_lam_re, s5_lam_im=s5_lam_im, s5_log_dt=s5_log_dt, s5_b_re=s5_b_re, s5_b_im=s5_b_im, s5_c_re=s5_c_re, s5_c_im=s5_c_im, s5_d=s5_d, glu_b=glu_b, final_g=final_g, c_ctx=c_ctx)
    small_m = dict(norm_g=m_norm_g, pool_scale=m_pool_scale, s5_lam_re=m_s5_lam_re, s5_lam_im=m_s5_lam_im, s5_log_dt=m_s5_log_dt, s5_b_re=m_s5_b_re, s5_b_im=m_s5_b_im, s5_c_re=m_s5_c_re, s5_c_im=m_s5_c_im, s5_d=m_s5_d, glu_b=m_glu_b, final_g=m_final_g, c_ctx=m_c_ctx)
    small_v = dict(norm_g=v_norm_g, pool_scale=v_pool_scale, s5_lam_re=v_s5_lam_re, s5_lam_im=v_s5_lam_im, s5_log_dt=v_s5_log_dt, s5_b_re=v_s5_b_re, s5_b_im=v_s5_b_im, s5_c_re=v_s5_c_re, s5_c_im=v_s5_c_im, s5_d=v_s5_d, glu_b=v_glu_b, final_g=v_final_g, c_ctx=v_c_ctx)
    small_g = dict(norm_g=dnorm_g, pool_scale=dps, s5_lam_re=ds5[0], s5_lam_im=ds5[1], s5_log_dt=ds5[2], s5_b_re=ds5[3], s5_b_im=ds5[4], s5_c_re=ds5[5], s5_c_im=ds5[6], s5_d=dd_skip, glu_b=dglu_b, final_g=dfg, c_ctx=cctx_part)
    shapes = [small_w[k].shape for k in small_names]
    nsmall = sum(int(np.prod(s)) for s in shapes) + 1
    unit = NDEV * 8 * LANES
    tot = -(-nsmall // unit) * unit
    rows = tot // LANES
    gpack = _pack([small_g[k] for k in small_names] + [loss_l], tot).reshape(NDEV, rows // NDEV, LANES)
    (land_small,) = _reduce_scatter([gpack], [F32], "small")
    gsum = _sum_slots(land_small, "sum_small")
    gall, g_ada_b_all = _all_gather([gsum, g_ada_b_l], "gather_small")
    gflat = gall.reshape(tot)
    loss = gflat[nsmall - 1]

    nab = NDEV * nmod
    tot2 = tot + -(-nab // (8 * LANES)) * (8 * LANES)
    g2 = jnp.concatenate([gflat, jnp.pad(g_ada_b_all.reshape(nab), (0, tot2 - tot - nab))]).reshape(tot2 // LANES, LANES)

    def pack2(dct, ab):
        return jnp.concatenate(
            [_pack([dct[k] for k in small_names] + [jnp.zeros((1,), F32)], tot), jnp.pad(ab.reshape(nab), (0, tot2 - tot - nab))]
        ).reshape(tot2 // LANES, LANES)

    _, d2, m2, v2 = _adam(g2, pack2(small_w, ada_b), pack2(small_m, m_ada_b), pack2(small_v, v_ada_b), "adam_small")

    def unpack2(flat2):
        flat2 = flat2.reshape(tot2)
        vals = dict(zip(small_names, _unpack(flat2, shapes)))
        vals["ada_b"] = flat2[tot : tot + nab].reshape(ada_b.shape)
        return vals

    res = {k: unpack2(a) for k, a in (("g", g2), ("d", d2), ("m", m2), ("v", v2))}

    def shard(name, g, land, w, m, v):
        shp = w.shape
        w2, m2_, v2_ = (a.reshape(-1, shp[-1]) for a in (w, m, v))
        out = _adam(g, w2, m2_, v2_, "adam_" + name, land=land)
        for k, a in zip(("g", "d", "m", "v"), out):
            res[k][name] = a.reshape(shp)

    shard("ada_w", g_ada_w, None, ada_w, m_ada_w, v_ada_w)
    shard("in_w", None, land_in, in_w, m_in_w, v_in_w)
    shard("pool_w", None, land_pool, pool_w, m_pool_w, v_pool_w)
    shard("glu_w", None, land_glu, glu_w, m_glu_w, v_glu_w)
    shard("out_w", None, land_out, out_w, m_out_w, v_out_w)

    names = ["c_ctx", "ada_w", "ada_b", "norm_g", "in_w", "pool_w", "pool_scale", "s5_lam_re", "s5_lam_im", "s5_log_dt", "s5_b_re", "s5_b_im", "s5_c_re", "s5_c_im", "s5_d", "glu_w", "glu_b", "out_w", "final_g"]
    return (loss, grad_x[None], *[res["g"][n] for n in names], *[res["d"][n] for n in names], *[res["m"][n] for n in names], *[res["v"][n] for n in names])
```

```python
import numpy as np

import jax
import jax.numpy as jnp
from jax import lax
from jax.experimental import pallas as pl
from jax.experimental.pallas import tpu as pltpu

F32 = jnp.float32
BF16 = jnp.bfloat16
NDEV = 8
EPS = 1e-6
GRID_W = 64
POOL_WINDOWS = (2, 4, 8, 16)
SSM_H = 16
CHUNK_T = 16
LANES = 128
ADAM_LR, ADAM_B1, ADAM_B2, ADAM_EPS, ADAM_WD, ADAM_STEP = 0.001, 0.9, 0.999, 1e-08, 0.01, 10
VMEM_BIG = 56 << 20
MESH_ID = pl.DeviceIdType.MESH

_HBM = pl.BlockSpec(memory_space=pltpu.HBM)
_ANY = pl.BlockSpec(memory_space=pl.ANY)
_SMEM = pl.BlockSpec(memory_space=pltpu.SMEM)


def _sds(shape, dtype=F32):
    return jax.ShapeDtypeStruct(tuple(shape), dtype)


def _cparams(ngrid=0, vmem=None):
    return pltpu.CompilerParams(
        dimension_semantics=("arbitrary",) * ngrid if ngrid else None, vmem_limit_bytes=vmem
    )


def _rows(tm, c, col=0):
    return pl.BlockSpec((tm, c), lambda i: (i, col))


def _whole(shape):
    nd = len(shape)
    return pl.BlockSpec(tuple(shape), lambda *_: (0,) * nd, pipeline_mode=pl.Buffered(1))


def _acc(shape):
    nd = len(shape)
    return pl.BlockSpec(tuple(shape), lambda *_: (0,) * nd)


def _mm(a, b):
    return jnp.dot(a, b, preferred_element_type=F32)


def _mm_nt(a, b):
    return lax.dot_general(a, b, (((1,), (1,)), ((), ())), preferred_element_type=F32)


def _mm_tn(a, b):
    return lax.dot_general(a, b, (((0,), (0,)), ((), ())), preferred_element_type=F32)


def _mm_split(k01, s):
    hi = s.astype(BF16)
    lo = (s - hi.astype(F32)).astype(BF16)
    return _mm(k01, hi) + _mm(k01, lo)


def _sigmoid(v):
    return 0.5 * (jnp.tanh(0.5 * v) + 1.0)


def _silu(v):
    return v * _sigmoid(v)


_GELU_K = 0.7978845608028654
_GELU_C = 0.044715


def _gelu(v):
    return 0.5 * v * (1.0 + jnp.tanh(_GELU_K * (v + _GELU_C * v * v * v)))


def _gelu_grad(v):
    th = jnp.tanh(_GELU_K * (v + _GELU_C * v * v * v))
    return 0.5 * (1.0 + th) + 0.5 * v * (1.0 - th * th) * (_GELU_K * (1.0 + 3.0 * _GELU_C * v * v))


def _colsum(v):
    return jnp.sum(v, axis=0, keepdims=True)


def _rowmean(v):
    return jnp.mean(v, axis=-1, keepdims=True)


NCHIP = NDEV // 2


def _my_pos():
    return lax.axis_index("x"), lax.axis_index("y"), lax.axis_index("c")


def _other_chips():
    x, y, _ = _my_pos()
    return [(1 - x, y), (x, 1 - y), (1 - x, 1 - y)]


def _remote(src, dst, send, recv, to):
    return pltpu.make_async_remote_copy(src, dst, send, recv, device_id=to, device_id_type=MESH_ID)


def _all_gather(arrs, name):
    n = len(arrs)

    def body(*refs):
        start, finish = _gather_copies(refs[:n], refs[n : 2 * n], *refs[2 * n :])
        start()
        finish()

    return pl.pallas_call(
        body, name=name, out_shape=_gather_shapes(arrs), in_specs=[_HBM] * n, out_specs=[_HBM] * n, scratch_shapes=_gather_sems(n)
    )(*arrs)


def _gather_shapes(arrs):
    return [_sds((NDEV,) + a.shape, a.dtype) for a in arrs]


def _gather_sems(n):
    return [pltpu.SemaphoreType.DMA((n, NDEV - 1)), pltpu.SemaphoreType.DMA((n, NDEV - 1)), pltpu.SemaphoreType.DMA((n,))]


def _gather_copies(ins, outs, send, recv, loc):
    n = len(ins)
    x, y, c = _my_pos()
    me, sib = (x, y, c), (x, y, 1 - c)
    chips = _other_chips()

    def slot(a, p):
        return outs[a].at[4 * p[0] + 2 * p[1] + p[2]]

    def copy(a, k, block, to, own=False):
        return _remote(ins[a] if own else slot(a, block), slot(a, block), send.at[a, k], recv.at[a, k], to)

    def mine():
        local = [pltpu.make_async_copy(ins[a], slot(a, me), loc.at[a]) for a in range(n)]
        sent = []
        for a in range(n):
            sent.append(copy(a, 0, me, sib, own=True))
            sent += [copy(a, 1 + j, me, (*chip, c), own=True) for j, chip in enumerate(chips)]
        return local, sent

    def start():
        local, sent = mine()
        for cp in local + sent:
            cp.start()

    def finish():
        local, sent = mine()
        for j, chip in enumerate(chips):
            for a in range(n):
                copy(a, 1 + j, (*chip, c), me).wait_recv()
                fwd = copy(a, 4 + j, (*chip, c), sib)
                fwd.start()
                sent.append(fwd)
        for a in range(n):
            copy(a, 0, sib, me).wait_recv()
            for j, chip in enumerate(chips):
                copy(a, 4 + j, (*chip, 1 - c), me).wait_recv()
        for cp in sent:
            cp.wait_send()
        for cp in local:
            cp.wait()

    return start, finish


def _sibling_swap(arrs, name):
    n = len(arrs)

    def body(*refs):
        start, finish = _swap_copies(refs[:n], refs[n : 2 * n], *refs[2 * n :])
        start()
        finish()

    return pl.pallas_call(
        body, name=name, out_shape=_swap_shapes(arrs), in_specs=[_HBM] * n, out_specs=[_HBM] * n, scratch_shapes=_swap_sems(n)
    )(*arrs)


def _swap_shapes(arrs):
    return [_sds((a.shape[0],) + a.shape[2:], a.dtype) for a in arrs]


def _swap_sems(n):
    return [pltpu.SemaphoreType.DMA((n,)), pltpu.SemaphoreType.DMA((n,))]


def _swap_copies(ins, outs, send, recv):
    x, y, c = _my_pos()

    def copies():
        return [_remote(ins[a].at[:, 1 - c], outs[a], send.at[a], recv.at[a], (x, y, 1 - c)) for a in range(len(ins))]

    def start():
        for cp in copies():
            cp.start()

    def finish():
        for cp in copies():
            cp.wait()

    return start, finish


def _pair_sum(arr, got, wire, name):
    _, _, r, c = arr.shape
    tr = _row_tile(r, c, budget=1 << 20)

    def body(a_ref, g_ref, o_ref):
        o_ref[0] = (a_ref[0, lax.axis_index("c")].astype(F32) + g_ref[0].astype(F32)).astype(wire)

    return pl.pallas_call(
        body,
        name=name,
        grid=(NCHIP, r // tr),
        in_specs=[pl.BlockSpec((1, 2, tr, c), lambda q, i: (q, 0, i, 0)), pl.BlockSpec((1, tr, c), lambda q, i: (q, i, 0))],
        out_specs=pl.BlockSpec((1, tr, c), lambda q, i: (q, i, 0)),
        out_shape=_sds((NCHIP, r, c), wire),
        compiler_params=_cparams(2),
    )(arr, got)


def _chip_exchange(arrs, name):
    n = len(arrs)

    def body(*refs):
        start, finish = _exchange_copies(refs[:n], refs[n : 2 * n], *refs[2 * n :])
        start()
        finish()

    return pl.pallas_call(
        body, name=name, out_shape=[_sds(a.shape, a.dtype) for a in arrs], in_specs=[_HBM] * n, out_specs=[_HBM] * n, scratch_shapes=_exchange_sems(n)
    )(*arrs)


def _exchange_sems(n):
    return [pltpu.SemaphoreType.DMA((n, NCHIP - 1)), pltpu.SemaphoreType.DMA((n, NCHIP - 1)), pltpu.SemaphoreType.DMA((n,))]


def _exchange_copies(ins, outs, send, recv, loc):
    n = len(ins)
    x, y, c = _my_pos()
    mine = 2 * x + y
    chips = _other_chips()

    def copies():
        local = [pltpu.make_async_copy(ins[a].at[mine], outs[a].at[mine], loc.at[a]) for a in range(n)]
        sent = [
            _remote(ins[a].at[2 * px + py], outs[a].at[mine], send.at[a, j], recv.at[a, j], (px, py, c))
            for a in range(n)
            for j, (px, py) in enumerate(chips)
        ]
        return local, sent

    def start():
        local, sent = copies()
        for cp in local + sent:
            cp.start()

    def finish():
        local, sent = copies()
        for a in range(n):
            for j, (px, py) in enumerate(chips):
                _remote(ins[a].at[mine], outs[a].at[2 * px + py], send.at[a, j], recv.at[a, j], (px, py, c)).wait_recv()
        for cp in sent:
            cp.wait_send()
        for cp in local:
            cp.wait()

    return start, finish


def _by_chip_and_core(a):
    return a.reshape((NCHIP, 2) + a.shape[1:])


def _pair_partials(arrs, wires, tag, got=None):
    four = [_by_chip_and_core(a) for a in arrs]
    if got is None:
        got = _sibling_swap(four, "swap_" + tag)
    return [_pair_sum(a, g, w, f"pair_sum_{tag}{i}") for i, (a, g, w) in enumerate(zip(four, got, wires))]


def _reduce_scatter(arrs, wires, tag):
    return _chip_exchange(_pair_partials(arrs, wires, tag), "exchange_" + tag)


def _row_tile(r, c, budget=1 << 20):
    width = max(c, LANES) * 4
    best = r
    for t in range(8, r, 8):
        if r % t == 0 and t * width <= budget:
            best = t
    if r * width <= budget:
        best = r
    return best


def _sum_chips(land):
    tot = land[0].astype(F32)
    for q in range(1, NCHIP):
        tot = tot + land[q].astype(F32)
    return tot


def _adam_math(w, g, m, v):
    m2 = ADAM_B1 * m + (1.0 - ADAM_B1) * g
    v2 = ADAM_B2 * v + (1.0 - ADAM_B2) * (g * g)
    mh = m2 / (1.0 - ADAM_B1**ADAM_STEP)
    vh = v2 / (1.0 - ADAM_B2**ADAM_STEP)
    delta = -ADAM_LR * (mh / (jnp.sqrt(vh) + ADAM_EPS) + ADAM_WD * w)
    return delta, m2, v2


def _adam(g, w, m, v, name, land=None):
    r, c = w.shape
    tr = _row_tile(r, c, budget=1 << 20)

    def body(*refs):
        if land is not None:
            l_ref, w_ref, m_ref, v_ref, g_ref, d_ref, m2_ref, v2_ref = refs
            gv = _sum_chips(l_ref)
            g_ref[...] = gv
        else:
            g_in, w_ref, m_ref, v_ref, d_ref, m2_ref, v2_ref = refs
            gv = g_in[...]
        d, m2, v2 = _adam_math(w_ref[...], gv, m_ref[...], v_ref[...])
        d_ref[...] = d
        m2_ref[...] = m2
        v2_ref[...] = v2

    blk = _rows(tr, c)
    if land is not None:
        in_specs = [pl.BlockSpec((NCHIP, tr, c), lambda i: (0, i, 0)), blk, blk, blk]
        out = pl.pallas_call(
            body, name=name, grid=(r // tr,), in_specs=in_specs, out_specs=[blk] * 4, out_shape=[_sds((r, c))] * 4, compiler_params=_cparams(1)
        )(land, w, m, v)
        return out
    out = pl.pallas_call(
        body, name=name, grid=(r // tr,), in_specs=[blk] * 4, out_specs=[blk] * 3, out_shape=[_sds((r, c))] * 3, compiler_params=_cparams(1)
    )(g, w, m, v)
    return (g,) + tuple(out)


def _sum_slots(land, name):
    _, r, c = land.shape
    tr = _row_tile(r, c, budget=1 << 20)

    def body(l_ref, o_ref):
        o_ref[...] = _sum_chips(l_ref)

    return pl.pallas_call(
        body,
        name=name,
        grid=(r // tr,),
        in_specs=[pl.BlockSpec((NCHIP, tr, c), lambda i: (0, i, 0))],
        out_specs=_rows(tr, c),
        out_shape=_sds((r, c)),
        compiler_params=_cparams(1),
    )(land)


def _mod_fwd(call, ada_w, ada_b_loc):
    def body(c_ref, w_ref, b_ref, o_ref):
        s = _silu(c_ref[...]).astype(BF16)
        o_ref[...] = _mm(s, w_ref[...].astype(BF16)) + b_ref[...]

    return pl.pallas_call(
        body, name="mod_fwd", out_shape=_sds((call.shape[0], ada_w.shape[1])), compiler_params=_cparams(0, 32 << 20)
    )(call, ada_w, ada_b_loc)


def _mod_bwd(call, dm_loc, dmc_loc, ada_w):
    d, n = ada_w.shape
    pad = call.shape[0] - NDEV - 1

    def body(c_ref, dm_ref, dmc_ref, w_ref, gw_ref, gb_ref, cp_ref):
        cv = c_ref[...]
        sg = _sigmoid(cv)
        dmc = _colsum(dmc_ref[...])
        dm = dm_ref[...]
        rows = jnp.concatenate([dm, dmc, jnp.zeros((pad, n), F32)], axis=0)
        gw_ref[...] = _mm_tn((cv * sg).astype(BF16), rows.astype(BF16))
        gb_ref[...] = _colsum(dm) + dmc
        back = _mm_nt(rows[NDEV:].astype(BF16), w_ref[...].astype(BF16))
        c8, s8 = cv[NDEV : NDEV + 1], sg[NDEV : NDEV + 1]
        cp_ref[...] = back[0:1] * (s8 * (1.0 + c8 * (1.0 - s8)))

    return pl.pallas_call(
        body, name="mod_bwd", out_shape=[_sds((d, n)), _sds((1, n)), _sds((1, d))], compiler_params=_cparams(0, 40 << 20)
    )(call, dm_loc, dmc_loc, ada_w)


def _riding(body, nin, nout, nr, grid, copies, nscratch=0):
    def wrapped(*refs):
        base = nin + 2 * nr + nout
        start, finish = copies(refs[nin : nin + nr], refs[nin + nr + nout : base], *refs[base + nscratch :])
        first, last = True, True
        for ax, size in enumerate(grid):
            first = jnp.logical_and(first, pl.program_id(ax) == 0)
            last = jnp.logical_and(last, pl.program_id(ax) == size - 1)
        pl.when(first)(start)
        body(*refs[:nin], *refs[nin + nr : nin + nr + nout], *refs[base : base + nscratch])
        pl.when(last)(finish)

    return wrapped if nr else body


def _in_proj(xr, norm_g, scale, shift, wg, jsel, tm, name, shards=()):
    lx, d = xr.shape
    cb = wg.shape[2]
    nr = len(shards)

    def body(x_ref, g_ref, sc_ref, sh_ref, w_ref, p_ref, h_ref):
        xv = x_ref[...]
        r = lax.rsqrt(_rowmean(xv * xv) + EPS)
        hb = ((xv * r) * g_ref[...] * (1.0 + sc_ref[...]) + sh_ref[...]).astype(BF16)
        h_ref[...] = hb
        for q, j in enumerate(jsel):
            p_ref[:, q * cb : (q + 1) * cb] = _mm(hb, w_ref[j])

    vec = _whole((1, d))
    out = pl.pallas_call(
        _riding(body, 5, 2, nr, (lx // tm,), _gather_copies),
        name=name,
        grid=(lx // tm,),
        in_specs=[_rows(tm, d), vec, vec, vec, _whole(wg.shape)] + [_HBM] * nr,
        out_specs=[_rows(tm, len(jsel) * cb), _rows(tm, d)] + [_HBM] * nr,
        out_shape=[_sds((lx, len(jsel) * cb)), _sds((lx, d), BF16)] + _gather_shapes(shards),
        scratch_shapes=_gather_sems(nr) if nr else [],
        compiler_params=_cparams(1, VMEM_BIG),
    )(xr, norm_g, scale, shift, wg, *shards)
    return out[:2], out[2:]


def _in_bwd(xr, dxo, dup, du5, dypre, dz, wg, norm_g, scale, shift, d_skip, dg_init, tm):
    lx, d = xr.shape
    cb = wg.shape[2]
    pw = dup.shape[1]
    sw = du5.shape[1]
    mix = dz.shape[1]
    ncol = NDEV * cb

    def body(x_ref, dxo_ref, up_ref, d5_ref, dy_ref, dz_ref, w_ref, g_ref, sc_ref, sh_ref, dk_ref, gi_ref,
             gx_ref, dp_ref, dsc_ref, dsh_ref, dg_ref):
        i = pl.program_id(0)

        @pl.when(i == 0)
        def _():
            dsc_ref[...] = jnp.zeros_like(dsc_ref)
            dsh_ref[...] = jnp.zeros_like(dsh_ref)
            dg_ref[...] = gi_ref[...]

        dp = jnp.concatenate(
            [up_ref[...], d5_ref[...] + dk_ref[...] * dy_ref[...], dz_ref[...]], axis=1
        ).astype(BF16)
        dp_ref[...] = dp
        dh = _mm_nt(dp[:, 0:cb], w_ref[0])
        for j in range(1, NDEV):
            dh = dh + _mm_nt(dp[:, j * cb : (j + 1) * cb], w_ref[j])
        xv = x_ref[...]
        r = lax.rsqrt(_rowmean(xv * xv) + EPS)
        xh = xv * r
        g = g_ref[...]
        one_sc = 1.0 + sc_ref[...]
        dsh_ref[...] += _colsum(dh)
        dsc_ref[...] += _colsum(dh * (xh * g))
        dg_ref[...] += _colsum(dh * one_sc * xh)
        dxh = dh * one_sc * g
        gx_ref[...] = r * (dxh - xh * _rowmean(dxh * xh)) + dxo_ref[...]

    vec = _whole((1, d))
    return pl.pallas_call(
        body,
        name="in_bwd",
        grid=(lx // tm,),
        in_specs=[_rows(tm, d), _rows(tm, d), _rows(tm, pw), _rows(tm, sw), _rows(tm, sw), _rows(tm, mix), _whole(wg.shape), vec, vec, vec, _whole((1, sw)), vec],
        out_specs=[_rows(tm, d), _rows(tm, ncol), _acc((1, d)), _acc((1, d)), _acc((1, d))],
        out_shape=[_sds((lx, d)), _sds((lx, ncol), BF16), _sds((1, d)), _sds((1, d)), _sds((1, d))],
        compiler_params=_cparams(1, VMEM_BIG),
    )(xr, dxo, dup, du5, dypre, dz, wg, norm_g, scale, shift, d_skip, dg_init)


def _in_bwd_ctx(xc, duc, wg, jsel, norm_g, scale, shift):
    lc, d = xc.shape
    cb = wg.shape[2]

    def body(x_ref, du_ref, w_ref, g_ref, sc_ref, sh_ref, dp_ref, dsc_ref, dsh_ref, dg_ref):
        dp = du_ref[...].astype(BF16)
        dp_ref[...] = dp
        dh = _mm_nt(dp[:, 0:cb], w_ref[jsel[0]])
        for q in range(1, len(jsel)):
            dh = dh + _mm_nt(dp[:, q * cb : (q + 1) * cb], w_ref[jsel[q]])
        xv = x_ref[...]
        xh = xv * lax.rsqrt(_rowmean(xv * xv) + EPS)
        dsh_ref[...] = _colsum(dh)
        dsc_ref[...] = _colsum(dh * (xh * g_ref[...]))
        dg_ref[...] = _colsum(dh * (1.0 + sc_ref[...]) * xh)

    return pl.pallas_call(
        body,
        name="in_bwd_ctx",
        out_shape=[_sds(duc.shape, BF16), _sds((1, d)), _sds((1, d)), _sds((1, d))],
        compiler_params=_cparams(0, VMEM_BIG),
    )(xc, duc, wg, norm_g, scale, shift)


def _in_w_grad(hb, dpb, hcb, dpcb, cb, tm, parts=()):
    lx, d = hb.shape
    lc = hcb.shape[0]
    nb = lx // tm
    cg = 2 * cb
    nr = len(parts)
    grid = (NDEV // 2, nb + 1)

    def body(h_ref, dp_ref, hc_ref, dpc_ref, o_ref, acc):
        j, i = pl.program_id(0), pl.program_id(1)

        @pl.when(i == 0)
        def _():
            acc[...] = jnp.zeros_like(acc)

        @pl.when(i < nb)
        def _():
            h = h_ref[...]
            acc[0] += _mm_tn(h, dp_ref[:, 0:cb])
            acc[1] += _mm_tn(h, dp_ref[:, cb:cg])

        @pl.when(jnp.logical_and(i == nb, j == 1))
        def _():
            h = hc_ref[...]
            acc[0] += _mm_tn(h, dpc_ref[:, 0:cb])
            acc[1] += _mm_tn(h, dpc_ref[:, cb:cg])

        @pl.when(i == nb)
        def _():
            o_ref[...] = acc[...].astype(o_ref.dtype)

    out = pl.pallas_call(
        _riding(body, 4, 1, nr, grid, _exchange_copies, nscratch=1),
        name="in_w_grad",
        grid=grid,
        in_specs=[
            pl.BlockSpec((tm, d), lambda j, i: (jnp.minimum(i, nb - 1), 0)),
            pl.BlockSpec((tm, cg), lambda j, i: (jnp.minimum(i, nb - 1), j)),
            pl.BlockSpec((lc, d), lambda j, i: (0, 0)),
            pl.BlockSpec((lc, cg), lambda j, i: (0, 0)),
        ]
        + [_HBM] * nr,
        out_specs=[pl.BlockSpec((2, d, cb), lambda j, i: (j, 0, 0))] + [_HBM] * nr,
        out_shape=[_sds((NDEV, d, cb), BF16)] + [_sds(p.shape, p.dtype) for p in parts],
        scratch_shapes=[pltpu.VMEM((2, d, cb), F32)] + (_exchange_sems(nr) if nr else []),
        compiler_params=_cparams(2, VMEM_BIG),
    )(hb, dpb, hcb, dpcb, *parts)
    return out[0], out[1:]


def _pool_tables(w, rows, rb, pgw, transpose):
    t = np.arange(GRID_W)
    lo, hi = np.clip(t - w // 2, 0, GRID_W), np.clip(t + w - w // 2, 0, GRID_W)
    band = ((t[None, :] >= lo[:, None]) & (t[None, :] < hi[:, None])).astype(np.float32)
    if transpose:
        band = band.T
    kc = np.kron(np.eye(rb, dtype=np.float32), band)
    inv_c = np.tile((1.0 / (hi - lo).astype(np.float32))[:, None], (rb, pgw)).astype(np.float32)
    r = np.arange(rows)
    cnt_r = np.clip(r + w - w // 2, 0, rows) - np.clip(r - w // 2, 0, rows)
    inv_r = (1.0 / cnt_r.astype(np.float32)).astype(np.float32)
    return jnp.asarray(kc, BF16), jnp.asarray(inv_r), jnp.asarray(inv_c)


def _pool_stack(rows, rb, pgw, transpose):
    tabs = [_pool_tables(w, rows, rb, pgw, transpose) for w in POOL_WINDOWS]
    return tuple(jnp.stack([t[i] for t in tabs]) for i in range(3))


def _window_sums(ref, r0, rb, w):
    slab = lambda off: ref[pl.ds(pl.multiple_of((r0 + off) * GRID_W, GRID_W), GRID_W), :]
    shared = range(rb - 1, w)
    common = None
    for off in shared:
        common = slab(off) if common is None else common + slab(off)
    sums = []
    for rr in range(rb):
        s = common
        for off in range(rr, rr + w):
            if off not in shared:
                s = slab(off) if s is None else s + slab(off)
        sums.append(s)
    return sums


def _per_window(group):
    for k, w in enumerate(POOL_WINDOWS):
        pl.when(pl.program_id(0) == k)(lambda k=k, w=w: group(k, w))


def _pool_fwd(proj, pool_wf):
    lx = proj.shape[0]
    ng, pgw, _ = pool_wf.shape
    rows = lx // GRID_W
    rb = min(4, rows)
    tok = rb * GRID_W
    kc, inv_r, inv_c = _pool_stack(rows, rb, pgw, False)

    def body(p_ref, w_ref, kc_ref, ir_ref, ic_ref, lin_ref, dm_ref, xp):
        def group(kk, w):
            lo = w // 2
            xp[pl.ds(0, lo * GRID_W), :] = jnp.zeros((lo * GRID_W, pgw), F32)
            xp[pl.ds((lo + rows) * GRID_W, (w - lo) * GRID_W), :] = jnp.zeros(((w - lo) * GRID_W, pgw), F32)
            xp[pl.ds(lo * GRID_W, lx), :] = p_ref[...]

            def blk(b, carry):
                r0 = b * rb
                parts = [s * ir_ref[kk, r0 + rr] for rr, s in enumerate(_window_sums(xp, r0, rb, w))]
                m = _mm_split(kc_ref[0], jnp.concatenate(parts, axis=0)) * ic_ref[0]
                u = xp[pl.ds(pl.multiple_of((r0 + lo) * GRID_W, GRID_W), tok), :]
                db = (m - u).astype(BF16)
                o0 = pl.multiple_of(r0 * GRID_W, GRID_W)
                dm_ref[pl.ds(o0, tok), :] = db
                lin_ref[pl.ds(o0, tok), :] = _mm(db, w_ref[0])
                return carry

            lax.fori_loop(0, rows // rb, blk, 0)

        _per_window(group)

    col = pl.BlockSpec((lx, pgw), lambda g: (0, g))
    return pl.pallas_call(
        body,
        name="pool_fwd",
        grid=(ng,),
        in_specs=[col, _gspec(1, pgw, pgw), _gspec(1, tok, tok), _SMEM, _gspec(1, tok, pgw)],
        out_specs=[col, col],
        out_shape=[_sds((lx, ng * pgw)), _sds((lx, ng * pgw), BF16)],
        scratch_shapes=[pltpu.VMEM(((rows + max(POOL_WINDOWS)) * GRID_W, pgw), F32)],
        compiler_params=_cparams(1, VMEM_BIG),
    )(proj, pool_wf, kc, inv_r, inv_c)


def _pool_bwd(dlin, pool_wf, dmat):
    lx = dlin.shape[0]
    ng, pgw, _ = pool_wf.shape
    rows = lx // GRID_W
    rb = min(4, rows)
    tok = rb * GRID_W
    kct, inv_r, inv_c = _pool_stack(rows, rb, pgw, True)

    def body(dl_ref, w_ref, kc_ref, ir_ref, ic_ref, dm_ref, du_ref, dw_ref, tp):
        def group(kk, w):
            front = w - w // 2 - 1
            if front:
                tp[pl.ds(0, front * GRID_W), :] = jnp.zeros((front * GRID_W, pgw), F32)
            tp[pl.ds((front + rows) * GRID_W, (w - front) * GRID_W), :] = jnp.zeros(((w - front) * GRID_W, pgw), F32)
            dw_ref[...] = jnp.zeros_like(dw_ref)

            def blk(b, carry):
                o0 = pl.multiple_of(b * tok, GRID_W)
                dlb = dl_ref[pl.ds(o0, tok), :].astype(BF16)
                dw_ref[0] += _mm_tn(dm_ref[pl.ds(o0, tok), :], dlb)
                dd = _mm_nt(dlb, w_ref[0])
                du_ref[pl.ds(o0, tok), :] = -dd
                t = _mm(kc_ref[0], (dd * ic_ref[0]).astype(BF16))
                for rr in range(rb):
                    dst = pl.multiple_of((b * rb + rr + front) * GRID_W, GRID_W)
                    tp[pl.ds(dst, GRID_W), :] = t[rr * GRID_W : (rr + 1) * GRID_W] * ir_ref[kk, b * rb + rr]
                return carry

            lax.fori_loop(0, rows // rb, blk, 0)

            def rowl(b, carry):
                for rr, s in enumerate(_window_sums(tp, b * rb, rb, w)):
                    du_ref[pl.ds(pl.multiple_of((b * rb + rr) * GRID_W, GRID_W), GRID_W), :] += s
                return carry

            lax.fori_loop(0, rows // rb, rowl, 0)

        _per_window(group)

    col = pl.BlockSpec((lx, pgw), lambda g: (0, g))
    return pl.pallas_call(
        body,
        name="pool_bwd",
        grid=(ng,),
        in_specs=[col, _gspec(1, pgw, pgw), _gspec(1, tok, tok), _SMEM, _gspec(1, tok, pgw), col],
        out_specs=[col, _gspec(1, pgw, pgw)],
        out_shape=[_sds((lx, ng * pgw)), _sds((ng, pgw, pgw))],
        scratch_shapes=[pltpu.VMEM(((rows + max(POOL_WINDOWS)) * GRID_W, pgw), F32)],
        compiler_params=_cparams(1, VMEM_BIG),
    )(dlin, pool_wf, kct, inv_r, inv_c, dmat)


def _s5_small(lam_re, lam_im, log_dt, b_re, b_im, c_re, c_im):
    t = CHUNK_T
    dt = jnp.exp(log_dt)[..., None]
    zr, zi = lam_re * dt, lam_im * dt
    tau = jnp.arange(t + 1, dtype=F32)
    mag = jnp.exp(zr[..., None] * tau)
    pr, pi = mag * jnp.cos(zi[..., None] * tau), mag * jnp.sin(zi[..., None] * tau)
    ar, ai = pr[..., 1], pi[..., 1]
    den = lam_re * lam_re + lam_im * lam_im
    qr = ((ar - 1.0) * lam_re + ai * lam_im) / den
    qi = (ai * lam_re - (ar - 1.0) * lam_im) / den
    bbr = qr[..., None] * b_re - qi[..., None] * b_im
    bbi = qr[..., None] * b_im + qi[..., None] * b_re
    ctr, cti = jnp.swapaxes(c_re, 2, 3), jnp.swapaxes(c_im, 2, 3)
    lane_pad = lambda v: jnp.pad(jnp.swapaxes(v, 0, 1), ((0, 0), (0, 0), (0, 0), (0, LANES - v.shape[-1])))
    tp = lambda v: jnp.swapaxes(v, 1, 2)
    flip = lambda v: v[..., ::-1]
    pfr, pfi = tp(flip(pr[0, ..., :t])), tp(flip(pi[0, ..., :t]))
    pbr, pbi = tp(pr[1, ..., :t]), tp(pi[1, ..., :t])
    bfr, bfi, bbr_t, bbi_t = tp(bbr[0]), tp(bbi[0]), tp(bbr[1]), tp(bbi[1])
    a1 = jnp.concatenate([pfr, pbr, pfr, pbr], axis=-1)
    a2 = jnp.concatenate([-pfi, -pbi, pfi, pbi], axis=-1)
    b1 = jnp.concatenate([bfr, bbr_t, bfi, bbi_t], axis=-1)
    b2 = jnp.concatenate([bfi, bbi_t, bfr, bbr_t], axis=-1)
    a16 = jnp.concatenate([pr[0, ..., t], pr[1, ..., t], pi[0, ..., t], pi[1, ..., t]], axis=1)
    return (lane_pad(pr), lane_pad(pi), lane_pad(bbr), lane_pad(bbi), lane_pad(ctr), lane_pad(cti), a1, a2, b1, b2), a16


def _split2(v):
    hi = v.astype(BF16)
    return hi, (v - hi.astype(F32)).astype(BF16)


def _dotx(a, b, mm=_mm):
    ah, al = (a, None) if a.dtype == BF16 else _split2(a)
    bh, bl = (b, None) if b.dtype == BF16 else _split2(b)
    out = mm(ah, bh)
    if bl is not None:
        out = out + mm(ah, bl)
    if al is not None:
        out = out + mm(al, bh)
    return out


def _shift_lanes(v, k):
    if k == 0:
        return v
    n = v.shape[-1]
    lane = lax.broadcasted_iota(jnp.int32, v.shape, v.ndim - 1)
    r = pltpu.roll(v, k % n, axis=v.ndim - 1)
    return jnp.where(lane >= k, r, 0.0) if k > 0 else jnp.where(lane < n + k, r, 0.0)


def _op_consts():
    t, h = CHUNK_T, SSM_H
    j, o = np.divmod(np.arange(t * h), h)
    p = np.arange(LANES)[:, None]
    sel = lambda col: (p == col[None, :]).astype(np.float32)
    pw = np.stack([np.concatenate([sel(j), sel(j + 1)], axis=1), np.concatenate([sel(t - 1 - j), sel(t - j)], axis=1)])
    return jnp.asarray(pw, BF16), jnp.asarray(sel(o), BF16), jnp.asarray(sel(j).T, BF16), jnp.asarray(sel(o).T, BF16)


def _op_factors(gi, pr_ref, pi_ref, br_ref, bi_ref, cr_ref, ci_ref, pw_ref, ch_ref):
    n = pr_ref.shape[2]
    c16 = _dotx(jnp.concatenate([cr_ref[gi, 0], ci_ref[gi, 0], cr_ref[gi, 1], ci_ref[gi, 1]], axis=0), ch_ref[...])
    out = []
    for d in range(2):
        e = _dotx(jnp.concatenate([pr_ref[gi, d], pi_ref[gi, d]], axis=0), pw_ref[d])
        c = c16[2 * d * n : (2 * d + 1) * n], c16[(2 * d + 1) * n : (2 * d + 2) * n]
        bst = jnp.concatenate([br_ref[gi, d], -bi_ref[gi, d]], axis=0)
        out.append(((e[:n, :CW], e[n:, :CW]), (e[:n, CW:], e[n:, CW:]), c, bst))
    return out


def _op_spread(m_ref, u_ref, v_ref, gi):
    return _dotx(m_ref[...], jnp.concatenate([_rows_pad(u_ref[gi]), _rows_pad(v_ref[gi])], axis=1))


def _cmul(x, y):
    return x[0] * y[0] - x[1] * y[1], x[0] * y[1] + x[1] * y[0]


def _cmul_conj(g, y):
    return g[0] * y[0] + g[1] * y[1], g[1] * y[0] - g[0] * y[1]


def _rows_pad(v):
    return jnp.concatenate([v, jnp.zeros((LANES - v.shape[0], v.shape[1]), F32)], axis=0)


def _s5_build(small, gb, shards):
    pr, pi, br, bi, cr, ci, a1, a2, b1, b2 = small
    g, _, n, _ = pr.shape
    t, h, ns = CHUNK_T, SSM_H, 4 * n
    pw, ch, rep_s, til_s = _op_consts()
    nr, nin, steps = len(shards), 14, g // gb

    def body(*refs):
        pr_ref, pi_ref, br_ref, bi_ref, cr_ref, ci_ref, a1_ref, a2_ref, b1_ref, b2_ref, pw_ref, ch_ref, rs_ref, ts_ref = refs[:nin]
        t_ref, w_ref, c_ref = refs[nin + nr : nin + nr + 3]
        start, finish = _gather_copies(refs[nin : nin + nr], refs[nin + nr + 3 : nin + 2 * nr + 3], *refs[nin + 2 * nr + 3 :])
        pl.when(pl.program_id(0) == 0)(start)

        def group(gi, carry):
            rows, outs = [], []
            for e16, e1, c16, bst in _op_factors(gi, pr_ref, pi_ref, br_ref, bi_ref, cr_ref, ci_ref, pw_ref, ch_ref):
                rows.append(_dotx(bst, jnp.concatenate(_cmul(c16, e16), axis=0), _mm_tn)[:h])
                outs.append(_cmul(c16, e1))
            c_ref[gi] = jnp.concatenate([outs[0][0], outs[1][0], -outs[0][1], -outs[1][1]], axis=0).astype(BF16)
            for s in range(t):
                t_ref[gi, s * h : (s + 1) * h, :] = (_shift_lanes(rows[0], h * s) + _shift_lanes(rows[1], -h * (t - 1 - s))).astype(BF16)
            ae, be = _op_spread(rs_ref, a1_ref, a2_ref, gi), _op_spread(ts_ref, b1_ref, b2_ref, gi)
            w_ref[gi] = (ae[:, :ns] * be[:, :ns] + ae[:, ns:] * be[:, ns:]).astype(BF16)
            return carry

        lax.fori_loop(0, gb, group, 0)
        pl.when(pl.program_id(0) == steps - 1)(finish)

    sm = pl.BlockSpec((gb, 2, n, LANES), lambda i: (i, 0, 0, 0))
    ab = lambda v: pl.BlockSpec((gb,) + v.shape[1:], lambda i: (i, 0, 0))
    out = pl.pallas_call(
        body,
        name="s5_build",
        grid=(steps,),
        in_specs=[sm] * 6 + [ab(a1), ab(a2), ab(b1), ab(b2), _whole(pw.shape), _whole(ch.shape), _whole(rep_s.shape), _whole(til_s.shape)] + [_HBM] * nr,
        out_specs=[_gspec(gb, CW, CW), _gspec(gb, CW, 4 * n), _gspec(gb, 4 * n, CW)] + [_HBM] * nr,
        out_shape=[_sds((g, CW, CW), BF16), _sds((g, CW, 4 * n), BF16), _sds((g, 4 * n, CW), BF16)] + _gather_shapes(shards),
        scratch_shapes=_gather_sems(nr),
        compiler_params=_cparams(1, VMEM_BIG),
    )(pr, pi, br, bi, cr, ci, a1, a2, b1, b2, pw, ch, rep_s, til_s, *shards)
    return out[:3], out[3:]


def _s5_build_bwd(small, dtsum, dwst, dccat, gb, parts):
    pr, pi, br, bi, cr, ci, a1, a2, b1, b2 = small
    g, _, n, _ = pr.shape
    t, h, ns = CHUNK_T, SSM_H, 4 * n
    pw, ch, rep_s, til_s = _op_consts()
    nr, nin, nout, steps = len(parts), 17, 10, g // gb

    def body(*refs):
        pr_ref, pi_ref, br_ref, bi_ref, cr_ref, ci_ref, a1_ref, a2_ref, b1_ref, b2_ref, pw_ref, ch_ref, rs_ref, ts_ref, dt_ref, dw_ref, dc_ref = refs[:nin]
        dpr_ref, dpi_ref, dbr_ref, dbi_ref, dcr_ref, dci_ref, da1_ref, da2_ref, db1_ref, db2_ref = refs[nin + nr : nin + nr + nout]
        start, finish = _exchange_copies(refs[nin : nin + nr], refs[nin + nr + nout : nin + 2 * nr + nout], *refs[nin + 2 * nr + nout :])
        pl.when(pl.program_id(0) == 0)(start)

        def group(gi, carry):
            drow = [jnp.zeros((h, CW), F32), jnp.zeros((h, CW), F32)]
            for s in range(t):
                blk = dt_ref[gi, s * h : (s + 1) * h, :]
                drow[0] = drow[0] + _shift_lanes(blk, -h * s)
                drow[1] = drow[1] + _shift_lanes(blk, h * (t - 1 - s))
            dcc = dc_ref[gi]
            dcs = []
            for d, (e16, e1, c16, bst) in enumerate(_op_factors(gi, pr_ref, pi_ref, br_ref, bi_ref, cr_ref, ci_ref, pw_ref, ch_ref)):
                dr = _rows_pad(drow[d])
                db = _dotx(jnp.concatenate(_cmul(c16, e16), axis=0), dr, _mm_nt)
                dbr_ref[gi, d] = db[:n]
                dbi_ref[gi, d] = -db[n:]
                dst = _dotx(bst, dr)
                dca = dst[:n], dst[n:]
                gx = dcc[d * n : (d + 1) * n], -dcc[(2 + d) * n : (3 + d) * n]
                dc16, dc16b = _cmul_conj(dca, e16), _cmul_conj(gx, e1)
                dcs += [dc16[0] + dc16b[0], dc16[1] + dc16b[1]]
                de = jnp.concatenate([jnp.concatenate(_cmul_conj(dca, c16), axis=0), jnp.concatenate(_cmul_conj(gx, c16), axis=0)], axis=1)
                dp = _dotx(de, pw_ref[d], _mm_nt)
                dpr_ref[gi, d] = dp[:n]
                dpi_ref[gi, d] = dp[n:]
            dct = _dotx(jnp.concatenate(dcs, axis=0), ch_ref[...], _mm_nt)
            for d in range(2):
                dcr_ref[gi, d] = dct[2 * d * n : (2 * d + 1) * n]
                dci_ref[gi, d] = dct[(2 * d + 1) * n : (2 * d + 2) * n]
            dw = dw_ref[gi]
            ae, be = _op_spread(rs_ref, a1_ref, a2_ref, gi), _op_spread(ts_ref, b1_ref, b2_ref, gi)
            da = _dotx(rs_ref[...], jnp.concatenate([dw * be[:, :ns], dw * be[:, ns:]], axis=1), _mm_tn)
            db = _dotx(ts_ref[...], jnp.concatenate([dw * ae[:, :ns], dw * ae[:, ns:]], axis=1), _mm_tn)
            da1_ref[gi] = da[:t, :ns]
            da2_ref[gi] = da[:t, ns:]
            db1_ref[gi] = db[:h, :ns]
            db2_ref[gi] = db[:h, ns:]
            return carry

        lax.fori_loop(0, gb, group, 0)
        pl.when(pl.program_id(0) == steps - 1)(finish)

    sm = pl.BlockSpec((gb, 2, n, LANES), lambda i: (i, 0, 0, 0))
    ab = lambda v: pl.BlockSpec((gb,) + v.shape[1:], lambda i: (i, 0, 0))
    ops = [_gspec(gb, CW, CW), _gspec(gb, CW, 4 * n), _gspec(gb, 4 * n, CW)]
    out = pl.pallas_call(
        body,
        name="s5_build_bwd",
        grid=(steps,),
        in_specs=[sm] * 6 + [ab(a1), ab(a2), ab(b1), ab(b2), _whole(pw.shape), _whole(ch.shape), _whole(rep_s.shape), _whole(til_s.shape)] + ops + [_HBM] * nr,
        out_specs=[sm] * 6 + [ab(a1), ab(a2), ab(b1), ab(b2)] + [_HBM] * nr,
        out_shape=[_sds(v.shape) for v in small] + [_sds(p.shape, p.dtype) for p in parts],
        scratch_shapes=_exchange_sems(nr),
        compiler_params=_cparams(1, VMEM_BIG),
    )(pr, pi, br, bi, cr, ci, a1, a2, b1, b2, pw, ch, rep_s, til_s, dtsum, dwst, dccat, *parts)
    return out[:nout], out[nout:]


SCAN_UNROLL = 4
GPL = LANES // SSM_H
CW = CHUNK_T * SSM_H
BW = CHUNK_T * LANES


def _chunk_perm():
    o = np.arange(BW)
    src = ((o % CW) // SSM_H) * LANES + (o // CW) * SSM_H + o % SSM_H
    return jnp.asarray(np.arange(BW)[:, None] == src[None, :], BF16)


def _gspec(gb, a, b):
    return pl.BlockSpec((gb, a, b), lambda i: (i, 0, 0))


def _chunk_rows(ref, n):
    return jnp.concatenate([ref[pl.ds(s, n, stride=CHUNK_T), :] for s in range(CHUNK_T)], axis=1)


def _s5_state(proj, col0, uc, perm, wst, ncc, ncl):
    nlb = uc.shape[1] // LANES
    g, _, ns = wst.shape
    nch = ncc + ncl
    half = ns // 2

    def body(p_ref, c_ref, pm_ref, w_ref, u_ref, sr_ref, si_ref):
        xrows = jnp.concatenate([_chunk_rows(c_ref, ncc), _chunk_rows(p_ref, ncl)], axis=0).astype(BF16)
        u = _mm(xrows, pm_ref[...]).astype(BF16)
        u_ref[0] = u
        for gi in range(GPL):
            s = _mm(u[:, gi * CW : (gi + 1) * CW], w_ref[gi].astype(BF16))
            sr_ref[gi] = s[:, :half]
            si_ref[gi] = s[:, half:]

    return pl.pallas_call(
        body,
        name="s5_state",
        grid=(nlb,),
        in_specs=[
            pl.BlockSpec((ncl * CHUNK_T, LANES), lambda i: (0, col0 + i)),
            pl.BlockSpec((ncc * CHUNK_T, LANES), lambda i: (0, i)),
            _whole(perm.shape),
            _gspec(GPL, CW, ns),
        ],
        out_specs=[pl.BlockSpec((1, nch, BW), lambda i: (i, 0, 0)), _gspec(GPL, nch, half), _gspec(GPL, nch, half)],
        out_shape=[_sds((nlb, nch, BW), BF16), _sds((g, nch, half)), _sds((g, nch, half))],
        compiler_params=_cparams(1, VMEM_BIG),
    )(proj, uc, perm, wst)


def _unrolled_loop(n, step, init):
    u = SCAN_UNROLL if n % SCAN_UNROLL == 0 else 1

    def trip(i, c):
        for k in range(u):
            c = step(i * u + k, c)
        return c

    return lax.fori_loop(0, n // u, trip, init)


def _lane_masks(gb, half):
    lane = lax.broadcasted_iota(jnp.int32, (gb, half), 1)
    return lane < (half // 2)


def _roll_half(v):
    return pltpu.roll(v, v.shape[-1] // 2, axis=v.ndim - 1)


def _pack_dirs(lo, hi, mf):
    return jnp.where(mf, lo, _roll_half(hi)), jnp.where(mf, _roll_half(lo), hi)


def _unpack_dirs(pf, pb, mf):
    return jnp.where(mf, pf, _roll_half(pb)), jnp.where(mf, _roll_half(pf), pb)


def _packed_spec(gb, nch, half, buffers=None):
    mode = {} if buffers is None else dict(pipeline_mode=pl.Buffered(buffers))
    if gb == 2 * GPL:
        return pl.BlockSpec((GPL, nch, half), lambda i: (i // 2, 0, 0), **mode)
    assert gb == GPL
    return pl.BlockSpec((gb // 2, nch, half), lambda i: (i, 0, 0), **mode)


def _pair_index(j, gb):
    if gb == 2 * GPL:
        return j, pl.program_id(0) % 2
    return j % (gb // 2), j // (gb // 2)


def _pick(which, lo, hi):
    return (hi if which else lo) if isinstance(which, int) else jnp.where(which == 0, lo, hi)


def _unpacked(pf_ref, pb_ref, j, gb, mf):
    jl, which = _pair_index(j, gb)
    return _pick(which, *_unpack_dirs(pf_ref[jl], pb_ref[jl], mf))


def _pair_tile(refs, jl, rows):
    return jnp.concatenate([r[jl, rows, :] for r in refs], axis=1)


def _pair_rows_of(op, which, q):
    z = jnp.zeros((q, op.shape[1]), op.dtype)
    blocks = [op[0:q], op[2 * q : 3 * q], op[q : 2 * q], op[3 * q : 4 * q]]
    lo = jnp.concatenate([x for b in blocks for x in (b, z)], axis=0)
    hi = jnp.concatenate([x for b in blocks for x in (z, b)], axis=0)
    return _pick(which, lo, hi)


def _group_rows_of(pair_rows, which, q):
    def take(off):
        blk = lambda k: pair_rows[2 * q * k + off : 2 * q * k + off + q]
        return jnp.concatenate([blk(0), blk(2), blk(1), blk(3)], axis=0)

    return _pick(which, take(0), take(q))


def _scan_rows(ref, row, hp, nch, mf):
    return _pack_dirs(ref[pl.ds(row, hp, stride=nch), :], ref[pl.ds(hp * nch + row, hp, stride=nch), :], mf)


def _s5_scan_fwd(s_re, s_im, a16, ncc, nch, gb):
    g, ns = a16.shape
    half = ns // 2
    hp = gb // 2

    def body(sr_ref, si_ref, a_ref, hfr_ref, hfi_ref, hbr_ref, hbi_ref):
        mf = _lane_masks(hp, half)
        afr, abr = _pack_dirs(a_ref[:hp, :half], a_ref[hp:, :half], mf)
        afi, abi = _pack_dirs(a_ref[:hp, half:], a_ref[hp:, half:], mf)
        zero = jnp.zeros((hp, half), F32)

        def step(t, c):
            fr, fi, br, bi = c
            rb = jnp.where(t < ncc, ncc - 1 - t, nch - 1 - (t - ncc))
            hfr_ref[pl.ds(t, hp, stride=nch), :] = fr
            hfi_ref[pl.ds(t, hp, stride=nch), :] = fi
            hbr_ref[pl.ds(rb, hp, stride=nch), :] = br
            hbi_ref[pl.ds(rb, hp, stride=nch), :] = bi
            sfr, sfi = _scan_rows(sr_ref, t, hp, nch, mf)[0], _scan_rows(si_ref, t, hp, nch, mf)[0]
            sbr, sbi = _scan_rows(sr_ref, rb, hp, nch, mf)[1], _scan_rows(si_ref, rb, hp, nch, mf)[1]
            return (afr * fr - afi * fi + sfr, afr * fi + afi * fr + sfi, abr * br - abi * bi + sbr, abr * bi + abi * br + sbi)

        _unrolled_loop(nch, step, (zero, zero, zero, zero))

    blk = pl.BlockSpec((gb * nch, half), lambda i: (i, 0))
    pblk = pl.BlockSpec((hp * nch, half), lambda i: (i, 0))
    return pl.pallas_call(
        body,
        name="s5_scan_fwd",
        grid=(g // gb,),
        in_specs=[blk, blk, pl.BlockSpec((gb, ns), lambda i: (i, 0))],
        out_specs=[pblk] * 4,
        out_shape=[_sds((g // 2 * nch, half))] * 4,
        compiler_params=_cparams(1, VMEM_BIG),
    )(s_re, s_im, a16)


def _token_rows_store(ref, val, n):
    for s in range(CHUNK_T):
        ref[pl.ds(s, n, stride=CHUNK_T), :] = val[:, s * LANES : (s + 1) * LANES]


def _s5_out(u_all, h4, tsum, ccat, perm, ncc, lx, gb):
    nlb, nch, _ = u_all.shape
    g, ns, _ = ccat.shape
    ncl = nch - ncc
    half = ns // 2

    def body(u_ref, hfr_ref, hfi_ref, hbr_ref, hbi_ref, t_ref, c_ref, pm_ref, y_ref):
        parts = []
        for gi in range(GPL):
            u = u_ref[0, ncc:, gi * CW : (gi + 1) * CW]
            jl, which = _pair_index(gi, gb)
            hs = _pair_tile((hfr_ref, hfi_ref, hbr_ref, hbi_ref), jl, slice(ncc, None)).astype(BF16)
            parts.append(_mm(u, t_ref[gi].astype(BF16)) + _mm(hs, _pair_rows_of(c_ref[gi].astype(BF16), which, ns // 4)))
        _token_rows_store(y_ref, _mm_nt(jnp.concatenate(parts, axis=1).astype(BF16), pm_ref[...]), ncl)

    return pl.pallas_call(
        body,
        name="s5_out",
        grid=(nlb,),
        in_specs=[pl.BlockSpec((1, nch, BW), lambda i: (i, 0, 0))] + [_packed_spec(gb, nch, half)] * 4 + [_gspec(GPL, CW, CW), _gspec(GPL, ns, CW), _whole(perm.shape)],
        out_specs=pl.BlockSpec((lx, LANES), lambda i: (0, i)),
        out_shape=_sds((lx, nlb * LANES)),
        compiler_params=_cparams(1, VMEM_BIG),
    )(u_all, *h4, tsum, ccat, perm)


def _s5_dstate(dypre, perm, ccat, ncc, swaps=()):
    lx, sw = dypre.shape
    nr = len(swaps)
    nlb = sw // LANES
    g, ns, _ = ccat.shape
    ncl = lx // CHUNK_T
    nch = ncl + ncc
    half = ns // 2

    def body(dy_ref, pm_ref, c_ref, dyr_ref, dr_ref, di_ref):
        dy = _mm(_chunk_rows(dy_ref, ncl).astype(BF16), pm_ref[...]).astype(BF16)
        dyr_ref[0] = dy
        for gi in range(GPL):
            dh = _mm_nt(dy[:, gi * CW : (gi + 1) * CW], c_ref[gi].astype(BF16))
            dr_ref[gi, :ncc, :] = jnp.zeros((ncc, half), F32)
            di_ref[gi, :ncc, :] = jnp.zeros((ncc, half), F32)
            dr_ref[gi, ncc:, :] = dh[:, :half]
            di_ref[gi, ncc:, :] = dh[:, half:]

    out = pl.pallas_call(
        _riding(body, 3, 3, nr, (nlb,), _swap_copies),
        name="s5_dstate",
        grid=(nlb,),
        in_specs=[pl.BlockSpec((lx, LANES), lambda i: (0, i)), _whole(perm.shape), _gspec(GPL, ns, CW)] + [_HBM] * nr,
        out_specs=[pl.BlockSpec((1, ncl, BW), lambda i: (i, 0, 0)), _gspec(GPL, nch, half), _gspec(GPL, nch, half)] + [_HBM] * nr,
        out_shape=[_sds((nlb, ncl, BW), BF16), _sds((g, nch, half)), _sds((g, nch, half))] + _swap_shapes(swaps),
        scratch_shapes=_swap_sems(nr) if nr else [],
        compiler_params=_cparams(1, VMEM_BIG),
    )(dypre, perm, ccat, *swaps)
    return out[:3], out[3:]


def _s5_scan_bwd(dh_re, dh_im, a16, ncc, nch, gb):
    g, ns = a16.shape
    half = ns // 2
    ncl = nch - ncc
    hp = gb // 2

    def body(dr_ref, di_ref, a_ref, sfr_ref, sfi_ref, sbr_ref, sbi_ref):
        mf = _lane_masks(hp, half)
        afr, abr = _pack_dirs(a_ref[:hp, :half], a_ref[hp:, :half], mf)
        afi, abi = _pack_dirs(a_ref[:hp, half:], a_ref[hp:, half:], mf)
        zero = jnp.zeros((hp, half), F32)

        def step(t, c):
            lfr, lfi, lbr, lbi = c
            pf = nch - 1 - t
            pb = jnp.where(t < ncl, ncc + t, t - ncl)
            sfr_ref[pl.ds(pf, hp, stride=nch), :] = lfr
            sfi_ref[pl.ds(pf, hp, stride=nch), :] = lfi
            sbr_ref[pl.ds(pb, hp, stride=nch), :] = lbr
            sbi_ref[pl.ds(pb, hp, stride=nch), :] = lbi
            dfr, dfi = _scan_rows(dr_ref, pf, hp, nch, mf)[0], _scan_rows(di_ref, pf, hp, nch, mf)[0]
            dbr, dbi = _scan_rows(dr_ref, pb, hp, nch, mf)[1], _scan_rows(di_ref, pb, hp, nch, mf)[1]
            return (dfr + afr * lfr + afi * lfi, dfi + afr * lfi - afi * lfr, dbr + abr * lbr + abi * lbi, dbi + abr * lbi - abi * lbr)

        _unrolled_loop(nch, step, (zero,) * 4)

    blk = pl.BlockSpec((gb * nch, half), lambda i: (i, 0))
    pblk = pl.BlockSpec((hp * nch, half), lambda i: (i, 0))
    return pl.pallas_call(
        body,
        name="s5_scan_bwd",
        grid=(g // gb,),
        in_specs=[blk, blk, pl.BlockSpec((gb, ns), lambda i: (i, 0))],
        out_specs=[pblk] * 4,
        out_shape=[_sds((g // 2 * nch, half))] * 4,
        compiler_params=_cparams(1, VMEM_BIG),
    )(dh_re, dh_im, a16)


def _s5_du(u_all, dy_all, ds4, tsum, wst, perm, ncc, lx, lc, gb):
    nlb, ncl, _ = dy_all.shape
    g, _, ns = wst.shape
    nch = ncc + ncl
    half = ns // 2

    def body(u_ref, dy_ref, sfr_ref, sfi_ref, sbr_ref, sbi_ref, t_ref, w_ref, pm_ref, du_ref, dc_ref, dw_ref):
        parts = []
        mf = _lane_masks(nch, half)
        for gi in range(GPL):
            ds = jnp.concatenate([_unpacked(sfr_ref, sbr_ref, gi, gb, mf), _unpacked(sfi_ref, sbi_ref, gi, gb, mf)], axis=1).astype(BF16)
            dw_ref[gi] = _mm_tn(u_ref[0, :, gi * CW : (gi + 1) * CW], ds)
            d_all = _mm_nt(ds, w_ref[gi].astype(BF16))
            d_lat = d_all[ncc:] + _mm_nt(dy_ref[0, :, gi * CW : (gi + 1) * CW], t_ref[gi].astype(BF16))
            parts.append(jnp.concatenate([d_all[:ncc], d_lat], axis=0))
        du = _mm_nt(jnp.concatenate(parts, axis=1).astype(BF16), pm_ref[...])
        _token_rows_store(dc_ref, du[:ncc], ncc)
        _token_rows_store(du_ref, du[ncc:], ncl)

    return pl.pallas_call(
        body,
        name="s5_du",
        grid=(nlb,),
        in_specs=[pl.BlockSpec((1, nch, BW), lambda i: (i, 0, 0)), pl.BlockSpec((1, ncl, BW), lambda i: (i, 0, 0))]
        + [_packed_spec(gb, nch, half)] * 4
        + [_gspec(GPL, CW, CW), _gspec(GPL, CW, ns), _whole(perm.shape)],
        out_specs=[pl.BlockSpec((lx, LANES), lambda i: (0, i)), pl.BlockSpec((lc, LANES), lambda i: (0, i)), _gspec(GPL, CW, ns)],
        out_shape=[_sds((lx, nlb * LANES)), _sds((lc, nlb * LANES)), _sds((g, CW, ns))],
        compiler_params=_cparams(1, VMEM_BIG),
    )(u_all, dy_all, *ds4, tsum, wst, perm)


def _s5_op_grads(u_all, dy_all, ds4, h4, ncc, gb):
    nlb, nch, _ = u_all.shape
    half = ds4[0].shape[2]
    g = 2 * ds4[0].shape[0]
    ns = 2 * half

    def body(u_ref, dy_ref, sfr_ref, sfi_ref, sbr_ref, sbi_ref, hfr_ref, hfi_ref, hbr_ref, hbi_ref, dt_ref, dc_ref, da_ref):
        das = []
        mf1 = _lane_masks(1, half)
        for gi in range(GPL):
            cols = slice(gi * CW, (gi + 1) * CW)
            jl, which = _pair_index(gi, gb)
            dy = dy_ref[0, :, cols]
            dt_ref[gi] = _mm_tn(u_ref[0, ncc:, cols], dy)
            hs = _pair_tile((hfr_ref, hfi_ref, hbr_ref, hbi_ref), jl, slice(ncc, None)).astype(BF16)
            dc_ref[gi] = _group_rows_of(_mm_tn(hs, dy), which, ns // 4)
            hfr, hfi, hbr, hbi = hfr_ref[jl], hfi_ref[jl], hbr_ref[jl], hbi_ref[jl]
            sfr, sfi, sbr, sbi = sfr_ref[jl], sfi_ref[jl], sbr_ref[jl], sbi_ref[jl]
            a_re = _pick(which, *_unpack_dirs(_colsum(hfr * sfr + hfi * sfi), _colsum(hbr * sbr + hbi * sbi), mf1))
            a_im = _pick(which, *_unpack_dirs(_colsum(hfr * sfi - hfi * sfr), _colsum(hbr * sbi - hbi * sbr), mf1))
            das.append(jnp.concatenate([a_re, a_im], axis=1))
        da_ref[...] = jnp.concatenate(das, axis=0)

    return pl.pallas_call(
        body,
        name="s5_op_grads",
        grid=(nlb,),
        in_specs=[pl.BlockSpec((1, nch, BW), lambda i: (i, 0, 0)), pl.BlockSpec((1, nch - ncc, BW), lambda i: (i, 0, 0))]
        + [_packed_spec(gb, nch, half)] * 4
        + [_packed_spec(gb, nch, half, buffers=1)] * 4,
        out_specs=[_gspec(GPL, CW, CW), _gspec(GPL, ns, CW), pl.BlockSpec((GPL, ns), lambda i: (i, 0))],
        out_shape=[_sds((g, CW, CW)), _sds((g, ns, CW)), _sds((g, ns))],
        compiler_params=_cparams(1, VMEM_BIG),
    )(u_all, dy_all, *ds4, *h4)


def _merge_fwd(y_ssm, proj, lin, x, tgt, glu_wg, out_w, d_skip, glu_b, pscale, gate, fin_g, tm):
    lx, d = x.shape
    sw = y_ssm.shape[1]
    mix = out_w.shape[0]
    pw = mix - sw
    cb2 = glu_wg.shape[2]
    nh = NDEV // 2

    def body(y_ref, u_ref, z_ref, l_ref, x_ref, t_ref, gw_ref, ow_ref, dk_ref, gb_ref, ps_ref, gt_ref, fg_ref,
             ypre_ref, y12_ref, br_ref, dxo_ref, loss_ref, dfg_ref, dgt_ref):
        @pl.when(pl.program_id(0) == 0)
        def _():
            loss_ref[...] = jnp.zeros_like(loss_ref)
            dfg_ref[...] = jnp.zeros_like(dfg_ref)
            dgt_ref[...] = jnp.zeros_like(dgt_ref)

        ypre = y_ref[...] + dk_ref[...] * u_ref[...]
        ypre_ref[...] = ypre
        yg = _gelu(ypre).astype(BF16)
        outs = []
        for j in range(nh):
            y1 = _mm(yg, gw_ref[j]) + gb_ref[:, j * cb2 : (j + 1) * cb2]
            y2 = _mm(yg, gw_ref[nh + j]) + gb_ref[:, (nh + j) * cb2 : (nh + j + 1) * cb2]
            y12_ref[:, j * cb2 : (j + 1) * cb2] = y1
            y12_ref[:, (nh + j) * cb2 : (nh + j + 1) * cb2] = y2
            outs.append(y1 * _sigmoid(y2))
        lin = l_ref[...] * ps_ref[...]
        brb = (jnp.concatenate([lin] + outs, axis=1) * _silu(z_ref[...])).astype(BF16)
        br_ref[...] = brb
        mixv = _mm(brb, ow_ref[...])
        xo = x_ref[...] + gt_ref[...] * mixv
        r2 = lax.rsqrt(_rowmean(xo * xo) + EPS)
        xh = xo * r2
        fg = fg_ref[...]
        e = xh * fg - t_ref[...]
        loss_ref[...] += 0.5 * _colsum(_rowmean(e * e))
        dy = e * (1.0 / d)
        dfg_ref[...] += _colsum(dy * xh)
        gy = dy * fg
        dxo = r2 * (gy - xh * _rowmean(gy * xh))
        dxo_ref[...] = dxo
        dgt_ref[...] += _colsum(dxo * mixv)

    vec = _whole((1, d))
    return pl.pallas_call(
        body,
        name="merge_fwd",
        grid=(lx // tm,),
        in_specs=[_rows(tm, sw), _rows(tm, sw, 1), _rows(tm, mix, 1), _rows(tm, pw), _rows(tm, d), _rows(tm, d), _whole(glu_wg.shape), _whole(out_w.shape), _whole((1, sw)), _whole((1, 2 * sw)), _whole((1, pw)), vec, vec],
        out_specs=[_rows(tm, sw), _rows(tm, 2 * sw), _rows(tm, mix), _rows(tm, d), _acc((1, 1)), _acc((1, d)), _acc((1, d))],
        out_shape=[_sds((lx, sw)), _sds((lx, 2 * sw)), _sds((lx, mix), BF16), _sds((lx, d)), _sds((1, 1)), _sds((1, d)), _sds((1, d))],
        compiler_params=_cparams(1, VMEM_BIG),
    )(y_ssm, proj, proj, lin, x, tgt, glu_wg, out_w, d_skip, glu_b, pscale, gate, fin_g)


def _out_bwd(dxo, gate, branch, out_w, lin, y12, proj, pscale, tm):
    lx, d = dxo.shape
    mix = out_w.shape[0]
    pw = lin.shape[1]
    sw = mix - pw
    nb = lx // tm

    def body(dxo_ref, gt_ref, br_ref, ow_ref, l_ref, y_ref, z_ref, ps_ref, dz_ref, dl_ref, dg_ref, dps_ref, dow_hbm, acc, sem):
        i = pl.program_id(0)

        @pl.when(i == 0)
        def _():
            acc[...] = jnp.zeros_like(acc)
            dps_ref[...] = jnp.zeros_like(dps_ref)

        dmix = (dxo_ref[...] * gt_ref[...]).astype(BF16)
        dbr = _mm_nt(dmix, ow_ref[...])
        acc[...] += _mm_tn(br_ref[...], dmix)
        z = z_ref[...]
        sg = _sigmoid(z)
        dbp = dbr * (z * sg)
        y1, y2 = y_ref[:, :sw], y_ref[:, sw:]
        s2 = _sigmoid(y2)
        ps = ps_ref[...]
        lin = l_ref[...]
        bp = jnp.concatenate([lin * ps, y1 * s2], axis=1)
        dz_ref[...] = dbr * bp * (sg * (1.0 + z * (1.0 - sg)))
        dlp = dbp[:, :pw]
        dl_ref[...] = dlp * ps
        dps_ref[...] += _colsum(dlp * lin)
        dss = dbp[:, pw:]
        dg_ref[...] = jnp.concatenate([dss * s2, dss * y1 * s2 * (1.0 - s2)], axis=1).astype(BF16)

        @pl.when(i == nb - 1)
        def _():
            cp = pltpu.make_async_copy(acc, dow_hbm, sem)
            cp.start()
            cp.wait()

    return pl.pallas_call(
        body,
        name="out_bwd",
        grid=(nb,),
        in_specs=[_rows(tm, d), _whole((1, d)), _rows(tm, mix), _whole(out_w.shape), _rows(tm, pw), _rows(tm, 2 * sw), _rows(tm, mix, 1), _whole((1, pw))],
        out_specs=[_rows(tm, mix), _rows(tm, pw), _rows(tm, 2 * sw), _acc((1, pw)), _ANY],
        out_shape=[_sds((lx, mix)), _sds((lx, pw)), _sds((lx, 2 * sw), BF16), _sds((1, pw)), _sds((mix, d))],
        scratch_shapes=[pltpu.VMEM((mix, d), F32), pltpu.SemaphoreType.DMA],
        compiler_params=_cparams(1, VMEM_BIG),
    )(dxo, gate, branch, out_w, lin, y12, proj, pscale)


def _glu_bwd(dg12, ypre, proj, glu_wg, d_skip, tm):
    lx, sw = ypre.shape
    cb2 = glu_wg.shape[2]
    nb = lx // tm

    def body(dg_ref, yp_ref, u_ref, gw_ref, dk_ref, dyp_ref, dgb_ref, dd_ref, dgw_hbm, acc, sem):
        i = pl.program_id(0)

        @pl.when(i == 0)
        def _():
            acc[...] = jnp.zeros_like(acc)
            dgb_ref[...] = jnp.zeros_like(dgb_ref)
            dd_ref[...] = jnp.zeros_like(dd_ref)

        ypre = yp_ref[...]
        ygb = _gelu(ypre).astype(BF16)
        dg = dg_ref[...]
        dyg = jnp.zeros((tm, sw), F32)
        for j in range(NDEV):
            dgj = dg[:, j * cb2 : (j + 1) * cb2]
            dyg = dyg + _mm_nt(dgj, gw_ref[j])
            acc[j] += _mm_tn(ygb, dgj)
        dgb_ref[...] += _colsum(dg.astype(F32))
        dyp = dyg * _gelu_grad(ypre)
        dyp_ref[...] = dyp
        dd_ref[...] += _colsum(dyp * u_ref[...])

        @pl.when(i == nb - 1)
        def _():
            cp = pltpu.make_async_copy(acc, dgw_hbm, sem)
            cp.start()
            cp.wait()

    return pl.pallas_call(
        body,
        name="glu_bwd",
        grid=(nb,),
        in_specs=[_rows(tm, 2 * sw), _rows(tm, sw), _rows(tm, sw, 1), _whole(glu_wg.shape), _whole((1, sw))],
        out_specs=[_rows(tm, sw), _acc((1, 2 * sw)), _acc((1, sw)), _ANY],
        out_shape=[_sds((lx, sw)), _sds((1, 2 * sw)), _sds((1, sw)), _sds(glu_wg.shape)],
        scratch_shapes=[pltpu.VMEM(glu_wg.shape, F32), pltpu.SemaphoreType.DMA],
        compiler_params=_cparams(1, VMEM_BIG),
    )(dg12, ypre, proj, glu_wg, d_skip)


def _pack(parts, total):
    flat = jnp.concatenate([p.reshape(-1).astype(F32) for p in parts])
    return jnp.pad(flat, (0, total - flat.shape[0]))


def _unpack(flat, shapes):
    out, off = [], 0
    for s in shapes:
        n = int(np.prod(s)) if len(s) else 1
        out.append(flat[off : off + n].reshape(s))
        off += n
    return out


def kernel(x, c, ctx, c_ctx, ada_w, ada_b, norm_g, in_w, pool_w, pool_scale, s5_lam_re, s5_lam_im, s5_log_dt, s5_b_re, s5_b_im, s5_c_re, s5_c_im, s5_d, glu_w, glu_b, out_w, final_g, loss_target, m_c_ctx, m_ada_w, m_ada_b, m_norm_g, m_in_w, m_pool_w, m_pool_scale, m_s5_lam_re, m_s5_lam_im, m_s5_log_dt, m_s5_b_re, m_s5_b_im, m_s5_c_re, m_s5_c_im, m_s5_d, m_glu_w, m_glu_b, m_out_w, m_final_g, v_c_ctx, v_ada_w, v_ada_b, v_norm_g, v_in_w, v_pool_w, v_pool_scale, v_s5_lam_re, v_s5_lam_im, v_s5_log_dt, v_s5_b_re, v_s5_b_im, v_s5_c_re, v_s5_c_im, v_s5_d, v_glu_w, v_glu_b, v_out_w, v_final_g):
    xr, tgt, xc = x[0], loss_target[0], ctx[0]
    lx, d = xr.shape
    lc = xc.shape[0]
    mix = out_w.shape[1] * NDEV
    sw = glu_w.shape[1]
    pw = mix - sw
    pgw = pw // len(POOL_WINDOWS)
    ngrp = sw // SSM_H
    cb = in_w.shape[2]
    cb2 = glu_w.shape[2]
    nmod = ada_w.shape[2]
    ncc, ncl = lc // CHUNK_T, lx // CHUNK_T
    nch = ncc + ncl
    tm = min(256, lx)
    tm_acc = min(512, lx)
    assert mix == d and pw == sw and 2 * cb == pw and NDEV * cb2 == 2 * sw and lx % GRID_W == 0
    mx, my, mc = _my_pos()
    me = 4 * mx + 2 * my + mc

    (c_all,) = _all_gather([c], "gather_c")
    call = jnp.concatenate([c_all.reshape(NDEV, d), c_ctx.reshape(1, d), jnp.zeros((NDEV - 1, d), F32)], axis=0)
    ada_w_l = ada_w[0]
    ada_b_l = lax.dynamic_slice_in_dim(ada_b, me * nmod, nmod, axis=1)
    m_loc = _mod_fwd(call, ada_w_l, ada_b_l)
    (m_all,) = _all_gather([m_loc], "gather_mod")
    mod = lax.dynamic_index_in_dim(m_all, me, axis=1, keepdims=False).reshape(1, NDEV * nmod)
    mod_c = m_all[:, NDEV, :].reshape(1, NDEV * nmod)
    shift, scale, gate = mod[:, :d], mod[:, d : 2 * d], mod[:, 2 * d :]
    shift_c, scale_c = mod_c[:, :d], mod_c[:, d : 2 * d]

    s5_params = tuple(p[0] for p in (s5_lam_re, s5_lam_im, s5_log_dt, s5_b_re, s5_b_im, s5_c_re, s5_c_im))
    (small, a16), small_vjp = jax.vjp(_s5_small, *s5_params)
    gb_ops = min(8, ngrp)
    (tsum, wst, ccat), (in_wg,) = _s5_build(small, gb_ops, [in_w[0].astype(BF16)])

    ssm_blocks = (2, 3)
    (proj, hb), (glu_wg, out_wg, pool_wg) = _in_proj(
        xr, norm_g, scale, shift, in_wg, tuple(range(NDEV)), tm, "in_proj",
        shards=[glu_w[0].astype(BF16), out_w[0].astype(BF16), pool_w[0].reshape(-1, pgw).astype(BF16)],
    )
    out_wf = out_wg.reshape(mix, d)
    pool_wf = pool_wg.reshape(NDEV, len(POOL_WINDOWS), pgw // NDEV, pgw).transpose(1, 0, 2, 3).reshape(len(POOL_WINDOWS), pgw, pgw)
    (uc, hcb), _ = _in_proj(xc, norm_g, scale_c, shift_c, in_wg, ssm_blocks, min(tm, lc), "in_proj_ctx")
    lin, dmat = _pool_fwd(proj, pool_wf)

    gb_scan = min(16, ngrp)
    half = wst.shape[2] // 2
    perm = _chunk_perm()
    flat = lambda a: a.reshape(ngrp * nch, half)
    pairs = lambda arrs: [a.reshape(ngrp // 2, nch, half) for a in arrs]
    u_all, s_re, s_im = _s5_state(proj, pw // LANES, uc, perm, wst, ncc, ncl)
    h4 = pairs(_s5_scan_fwd(flat(s_re), flat(s_im), a16, ncc, nch, gb_scan))
    y_ssm = _s5_out(u_all, h4, tsum, ccat, perm, ncc, lx, gb_scan)

    ypre, y12, branch, dxo, loss_l, dfg, dgate = _merge_fwd(
        y_ssm, proj, lin, xr, tgt, glu_wg, out_wf, s5_d, glu_b, pool_scale, gate, final_g.reshape(1, d), tm
    )

    dz, dlin, dg12, dps, d_out_w = _out_bwd(dxo, gate, branch, out_wf, lin, y12, proj, pool_scale, tm)
    dypre, dglu_b, dd_skip, d_glu_w = _glu_bwd(dg12, ypre, proj, glu_wg, s5_d, tm_acc)

    d_glu_out = [d_glu_w, d_out_w.reshape(NDEV, mix // NDEV, d)]
    (dy_all, dh_re, dh_im), got_glu_out = _s5_dstate(dypre, perm, ccat, ncc, swaps=[_by_chip_and_core(a) for a in d_glu_out])
    ds4 = pairs(_s5_scan_bwd(flat(dh_re), flat(dh_im), a16, ncc, nch, gb_scan))
    du5, duc, dwst = _s5_du(u_all, dy_all, ds4, tsum, wst, perm, ncc, lx, lc, gb_scan)
    dtsum, dccat, da16 = _s5_op_grads(u_all, dy_all, ds4, h4, ncc, gb_scan)
    dup, dpool_w = _pool_bwd(dlin, pool_wf, dmat)

    npw = len(POOL_WINDOWS)
    dpool_s = dpool_w.reshape(npw, NDEV, pgw // NDEV, pgw).transpose(1, 0, 2, 3).reshape(NDEV, npw * (pgw // NDEV), pgw)
    parts_a = _pair_partials(d_glu_out, [BF16] * 2, "grads_a", got=got_glu_out) + _pair_partials([dpool_s], [BF16], "grads_p")
    dpcb, dsc_c, dsh_c, dg_c = _in_bwd_ctx(xc, duc, in_wg, ssm_blocks, norm_g, scale_c, shift_c)
    grad_x, dpb, dsc, dsh, dnorm_g = _in_bwd(xr, dxo, dup, du5, dypre, dz, in_wg, norm_g, scale, shift, s5_d, dg_c, tm)
    d_in_w, (land_glu, land_out, land_pool) = _in_w_grad(hb, dpb, hcb, dpcb, cb, min(2 * tm_acc, lx), parts=parts_a)

    dmod = jnp.concatenate(
        [jnp.concatenate([dsh, dsc, dgate], axis=1), jnp.concatenate([dsh_c, dsc_c, jnp.zeros((1, d), F32)], axis=1)], axis=0
    )
    (dmod_all,) = _all_gather([dmod], "gather_dmod")
    dmod_l = lax.dynamic_slice_in_dim(dmod_all, me * nmod, nmod, axis=2)
    g_ada_w, g_ada_b_l, cctx_part = _mod_bwd(call, dmod_l[:, 0, :], dmod_l[:, 1, :], ada_w_l)

    parts_b = _pair_partials([d_in_w], [BF16], "grads_b")
    dsmall, (land_in,) = _s5_build_bwd(small, dtsum, dwst, dccat, gb_ops, parts_b)
    ds5 = small_vjp((tuple(dsmall), da16))

    big_names = ["s5_b_re", "s5_b_im", "s5_c_re", "s5_c_im"]
    small_names = ["norm_g", "pool_scale", "s5_lam_re", "s5_lam_im", "s5_log_dt", "s5_d", "glu_b", "final_g", "c_ctx"]
    small_w = dict(norm_g=norm_g, pool_scale=pool_scale, s5_lam_re=s5_lam_re, s5_lam_im=s5_lam_im, s5_log_dt=s5_log_dt, s5_b_re=s5_b_re, s5_b_im=s5_b_im, s5_c_re=s5_c_re, s5_c_im=s5_c_im, s5_d=s5_d, glu_b=glu_b, final_g=final_g, c_ctx=c_ctx)
    small_m = dict(norm_g=m_norm_g, pool_scale=m_pool_scale, s5_lam_re=m_s5_lam_re, s5_lam_im=m_s5_lam_im, s5_log_dt=m_s5_log_dt, s5_b_re=m_s5_b_re, s5_b_im=m_s5_b_im, s5_c_re=m_s5_c_re, s5_c_im=m_s5_c_im, s5_d=m_s5_d, glu_b=m_glu_b, final_g=m_final_g, c_ctx=m_c_ctx)
    small_v = dict(norm_g=v_norm_g, pool_scale=v_pool_scale, s5_lam_re=v_s5_lam_re, s5_lam_im=v_s5_lam_im, s5_log_dt=v_s5_log_dt, s5_b_re=v_s5_b_re, s5_b_im=v_s5_b_im, s5_c_re=v_s5_c_re, s5_c_im=v_s5_c_im, s5_d=v_s5_d, glu_b=v_glu_b, final_g=v_final_g, c_ctx=v_c_ctx)
    small_g = dict(norm_g=dnorm_g, pool_scale=dps, s5_lam_re=ds5[0], s5_lam_im=ds5[1], s5_log_dt=ds5[2], s5_b_re=ds5[3], s5_b_im=ds5[4], s5_c_re=ds5[5], s5_c_im=ds5[6], s5_d=dd_skip, glu_b=dglu_b, final_g=dfg, c_ctx=cctx_part)
    shapes = [small_w[k].shape for k in small_names]
    nsmall = sum(int(np.prod(s)) for s in shapes) + 1
    unit = NDEV * 8 * LANES
    tot_s = -(-nsmall // unit) * unit
    big_n = [int(np.prod(small_w[k].shape)) for k in big_names]
    assert all(n % unit == 0 for n in big_n)
    tot = tot_s + sum(big_n)
    rows = tot // LANES
    gpack = jnp.concatenate(
        [_pack([small_g[k] for k in small_names] + [loss_l], tot_s)] + [small_g[k].reshape(-1) for k in big_names]
    ).reshape(NDEV, rows // NDEV, LANES)
    (land_small,) = _reduce_scatter([gpack], [F32], "small")
    gsum = _sum_slots(land_small, "sum_small")
    gall, g_ada_b_all = _all_gather([gsum, g_ada_b_l], "gather_small")
    gflat = gall.reshape(tot)
    loss = gflat[nsmall - 1]

    nab = NDEV * nmod
    tot2 = tot_s + -(-nab // (8 * LANES)) * (8 * LANES)
    g2 = jnp.concatenate([gflat[:tot_s], jnp.pad(g_ada_b_all.reshape(nab), (0, tot2 - tot_s - nab))]).reshape(tot2 // LANES, LANES)

    def pack2(dct, ab):
        return jnp.concatenate(
            [_pack([dct[k] for k in small_names] + [jnp.zeros((1,), F32)], tot_s), jnp.pad(ab.reshape(nab), (0, tot2 - tot_s - nab))]
        ).reshape(tot2 // LANES, LANES)

    _, d2, m2, v2 = _adam(g2, pack2(small_w, ada_b), pack2(small_m, m_ada_b), pack2(small_v, v_ada_b), "adam_small")

    def unpack2(flat2):
        flat2 = flat2.reshape(tot2)
        vals = dict(zip(small_names, _unpack(flat2, shapes)))
        vals["ada_b"] = flat2[tot_s : tot_s + nab].reshape(ada_b.shape)
        return vals

    res = {k: unpack2(a) for k, a in (("g", g2), ("d", d2), ("m", m2), ("v", v2))}

    def shard(name, g, land, w, m, v):
        shp = w.shape
        w2, m2_, v2_ = (a.reshape(-1, shp[-1]) for a in (w, m, v))
        out = _adam(g, w2, m2_, v2_, "adam_" + name, land=land)
        for k, a in zip(("g", "d", "m", "v"), out):
            res[k][name] = a.reshape(shp)

    shard("ada_w", g_ada_w, None, ada_w, m_ada_w, v_ada_w)
    shard("in_w", None, land_in, in_w, m_in_w, v_in_w)
    shard("pool_w", None, land_pool, pool_w, m_pool_w, v_pool_w)
    shard("glu_w", None, land_glu, glu_w, m_glu_w, v_glu_w)
    shard("out_w", None, land_out, out_w, m_out_w, v_out_w)
    off = tot_s
    for k, n in zip(big_names, big_n):
        shard(k, gflat[off : off + n].reshape(-1, small_w[k].shape[-1]), None, small_w[k], small_m[k], small_v[k])
        off += n

    names = ["c_ctx", "ada_w", "ada_b", "norm_g", "in_w", "pool_w", "pool_scale", "s5_lam_re", "s5_lam_im", "s5_log_dt", "s5_b_re", "s5_b_im", "s5_c_re", "s5_c_im", "s5_d", "glu_w", "glu_b", "out_w", "final_g"]
    return (loss, grad_x[None], *[res["g"][n] for n in names], *[res["d"][n] for n in names], *[res["m"][n] for n in names], *[res["v"][n] for n in names])
```

```python
import numpy as np

import jax
import jax.numpy as jnp
from jax import lax
from jax.experimental import pallas as pl
from jax.experimental.pallas import tpu as pltpu

F32 = jnp.float32
BF16 = jnp.bfloat16
NDEV = 8
EPS = 1e-6
GRID_W = 64
POOL_WINDOWS = (2, 4, 8, 16)
SSM_H = 16
CHUNK_T = 16
LANES = 128
ADAM_LR, ADAM_B1, ADAM_B2, ADAM_EPS, ADAM_WD, ADAM_STEP = 0.001, 0.9, 0.999, 1e-08, 0.01, 10
VMEM_BIG = 56 << 20
MESH_ID = pl.DeviceIdType.MESH

_HBM = pl.BlockSpec(memory_space=pltpu.HBM)
_ANY = pl.BlockSpec(memory_space=pl.ANY)
_SMEM = pl.BlockSpec(memory_space=pltpu.SMEM)


def _sds(shape, dtype=F32):
    return jax.ShapeDtypeStruct(tuple(shape), dtype)


def _cparams(ngrid=0, vmem=None):
    return pltpu.CompilerParams(
        dimension_semantics=("arbitrary",) * ngrid if ngrid else None, vmem_limit_bytes=vmem
    )


def _rows(tm, c, col=0):
    return pl.BlockSpec((tm, c), lambda i: (i, col))


def _whole(shape):
    nd = len(shape)
    return pl.BlockSpec(tuple(shape), lambda *_: (0,) * nd, pipeline_mode=pl.Buffered(1))


def _acc(shape):
    nd = len(shape)
    return pl.BlockSpec(tuple(shape), lambda *_: (0,) * nd)


def _mm(a, b):
    return jnp.dot(a, b, preferred_element_type=F32)


def _mm_nt(a, b):
    return lax.dot_general(a, b, (((1,), (1,)), ((), ())), preferred_element_type=F32)


def _mm_tn(a, b):
    return lax.dot_general(a, b, (((0,), (0,)), ((), ())), preferred_element_type=F32)


def _mm_split(k01, s):
    hi = s.astype(BF16)
    lo = (s - hi.astype(F32)).astype(BF16)
    return _mm(k01, hi) + _mm(k01, lo)


def _sigmoid(v):
    return 0.5 * (jnp.tanh(0.5 * v) + 1.0)


def _silu(v):
    return v * _sigmoid(v)


_GELU_K = 0.7978845608028654
_GELU_C = 0.044715


def _gelu(v):
    return 0.5 * v * (1.0 + jnp.tanh(_GELU_K * (v + _GELU_C * v * v * v)))


def _gelu_grad(v):
    th = jnp.tanh(_GELU_K * (v + _GELU_C * v * v * v))
    return 0.5 * (1.0 + th) + 0.5 * v * (1.0 - th * th) * (_GELU_K * (1.0 + 3.0 * _GELU_C * v * v))


def _colsum(v):
    return jnp.sum(v, axis=0, keepdims=True)


def _rowmean(v):
    return jnp.mean(v, axis=-1, keepdims=True)


NCHIP = NDEV // 2


def _my_pos():
    return lax.axis_index("x"), lax.axis_index("y"), lax.axis_index("c")


def _other_chips():
    x, y, _ = _my_pos()
    return [(1 - x, y), (x, 1 - y), (1 - x, 1 - y)]


def _remote(src, dst, send, recv, to):
    return pltpu.make_async_remote_copy(src, dst, send, recv, device_id=to, device_id_type=MESH_ID)


def _all_gather(arrs, name):
    n = len(arrs)

    def body(*refs):
        start, finish = _gather_copies(refs[:n], refs[n : 2 * n], *refs[2 * n :])
        start()
        finish()

    return pl.pallas_call(
        body, name=name, out_shape=_gather_shapes(arrs), in_specs=[_HBM] * n, out_specs=[_HBM] * n, scratch_shapes=_gather_sems(n)
    )(*arrs)


def _gather_shapes(arrs):
    return [_sds((NDEV,) + a.shape, a.dtype) for a in arrs]


def _gather_sems(n):
    return [pltpu.SemaphoreType.DMA((n, NDEV - 1)), pltpu.SemaphoreType.DMA((n, NDEV - 1)), pltpu.SemaphoreType.DMA((n,))]


def _gather_copies(ins, outs, send, recv, loc):
    n = len(ins)
    x, y, c = _my_pos()
    me, sib = (x, y, c), (x, y, 1 - c)
    chips = _other_chips()

    def slot(a, p):
        return outs[a].at[4 * p[0] + 2 * p[1] + p[2]]

    def copy(a, k, block, to, own=False):
        return _remote(ins[a] if own else slot(a, block), slot(a, block), send.at[a, k], recv.at[a, k], to)

    def mine():
        local = [pltpu.make_async_copy(ins[a], slot(a, me), loc.at[a]) for a in range(n)]
        sent = []
        for a in range(n):
            sent.append(copy(a, 0, me, sib, own=True))
            sent += [copy(a, 1 + j, me, (*chip, c), own=True) for j, chip in enumerate(chips)]
        return local, sent

    def start():
        local, sent = mine()
        for cp in local + sent:
            cp.start()

    def finish():
        local, sent = mine()
        for j, chip in enumerate(chips):
            for a in range(n):
                copy(a, 1 + j, (*chip, c), me).wait_recv()
                fwd = copy(a, 4 + j, (*chip, c), sib)
                fwd.start()
                sent.append(fwd)
        for a in range(n):
            copy(a, 0, sib, me).wait_recv()
            for j, chip in enumerate(chips):
                copy(a, 4 + j, (*chip, 1 - c), me).wait_recv()
        for cp in sent:
            cp.wait_send()
        for cp in local:
            cp.wait()

    return start, finish


def _sibling_swap(arrs, name):
    n = len(arrs)

    def body(*refs):
        start, finish = _swap_copies(refs[:n], refs[n : 2 * n], *refs[2 * n :])
        start()
        finish()

    return pl.pallas_call(
        body, name=name, out_shape=_swap_shapes(arrs), in_specs=[_HBM] * n, out_specs=[_HBM] * n, scratch_shapes=_swap_sems(n)
    )(*arrs)


def _swap_shapes(arrs):
    return [_sds((a.shape[0],) + a.shape[2:], a.dtype) for a in arrs]


def _swap_sems(n):
    return [pltpu.SemaphoreType.DMA((n,)), pltpu.SemaphoreType.DMA((n,))]


def _swap_copies(ins, outs, send, recv):
    x, y, c = _my_pos()

    def copies():
        return [_remote(ins[a].at[:, 1 - c], outs[a], send.at[a], recv.at[a], (x, y, 1 - c)) for a in range(len(ins))]

    def start():
        for cp in copies():
            cp.start()

    def finish():
        for cp in copies():
            cp.wait()

    return start, finish


def _pair_sum(arr, got, wire, name):
    _, _, r, c = arr.shape
    tr = _row_tile(r, c, budget=1 << 20)

    def body(a_ref, g_ref, o_ref):
        o_ref[0] = (a_ref[0, lax.axis_index("c")].astype(F32) + g_ref[0].astype(F32)).astype(wire)

    return pl.pallas_call(
        body,
        name=name,
        grid=(NCHIP, r // tr),
        in_specs=[pl.BlockSpec((1, 2, tr, c), lambda q, i: (q, 0, i, 0)), pl.BlockSpec((1, tr, c), lambda q, i: (q, i, 0))],
        out_specs=pl.BlockSpec((1, tr, c), lambda q, i: (q, i, 0)),
        out_shape=_sds((NCHIP, r, c), wire),
        compiler_params=_cparams(2),
    )(arr, got)


def _chip_exchange(arrs, name):
    n = len(arrs)

    def body(*refs):
        start, finish = _exchange_copies(refs[:n], refs[n : 2 * n], *refs[2 * n :])
        start()
        finish()

    return pl.pallas_call(
        body, name=name, out_shape=[_sds(a.shape, a.dtype) for a in arrs], in_specs=[_HBM] * n, out_specs=[_HBM] * n, scratch_shapes=_exchange_sems(n)
    )(*arrs)


def _exchange_sems(n):
    return [pltpu.SemaphoreType.DMA((n, NCHIP - 1)), pltpu.SemaphoreType.DMA((n, NCHIP - 1)), pltpu.SemaphoreType.DMA((n,))]


def _exchange_copies(ins, outs, send, recv, loc):
    n = len(ins)
    x, y, c = _my_pos()
    mine = 2 * x + y
    chips = _other_chips()

    def copies():
        local = [pltpu.make_async_copy(ins[a].at[mine], outs[a].at[mine], loc.at[a]) for a in range(n)]
        sent = [
            _remote(ins[a].at[2 * px + py], outs[a].at[mine], send.at[a, j], recv.at[a, j], (px, py, c))
            for a in range(n)
            for j, (px, py) in enumerate(chips)
        ]
        return local, sent

    def start():
        local, sent = copies()
        for cp in local + sent:
            cp.start()

    def finish():
        local, sent = copies()
        for a in range(n):
            for j, (px, py) in enumerate(chips):
                _remote(ins[a].at[mine], outs[a].at[2 * px + py], send.at[a, j], recv.at[a, j], (px, py, c)).wait_recv()
        for cp in sent:
            cp.wait_send()
        for cp in local:
            cp.wait()

    return start, finish


def _by_chip_and_core(a):
    return a.reshape((NCHIP, 2) + a.shape[1:])


def _pair_partials(arrs, wires, tag, got=None):
    four = [_by_chip_and_core(a) for a in arrs]
    if got is None:
        got = _sibling_swap(four, "swap_" + tag)
    return [_pair_sum(a, g, w, f"pair_sum_{tag}{i}") for i, (a, g, w) in enumerate(zip(four, got, wires))]


def _reduce_scatter(arrs, wires, tag):
    return _chip_exchange(_pair_partials(arrs, wires, tag), "exchange_" + tag)


def _row_tile(r, c, budget=1 << 20):
    width = max(c, LANES) * 4
    best = r
    for t in range(8, r, 8):
        if r % t == 0 and t * width <= budget:
            best = t
    if r * width <= budget:
        best = r
    return best


def _sum_chips(land):
    tot = land[0].astype(F32)
    for q in range(1, NCHIP):
        tot = tot + land[q].astype(F32)
    return tot


def _adam_math(w, g, m, v):
    m2 = ADAM_B1 * m + (1.0 - ADAM_B1) * g
    v2 = ADAM_B2 * v + (1.0 - ADAM_B2) * (g * g)
    mh = m2 / (1.0 - ADAM_B1**ADAM_STEP)
    vh = v2 / (1.0 - ADAM_B2**ADAM_STEP)
    delta = -ADAM_LR * (mh / (jnp.sqrt(vh) + ADAM_EPS) + ADAM_WD * w)
    return delta, m2, v2


def _adam_many(gs, ws, ms, vs, name):
    n = len(gs)

    def body(*refs):
        for t in range(n):
            d, m2, v2 = _adam_math(refs[n + t][...], refs[t][...], refs[2 * n + t][...], refs[3 * n + t][...])
            refs[4 * n + t][...] = d
            refs[5 * n + t][...] = m2
            refs[6 * n + t][...] = v2

    out = pl.pallas_call(body, name=name, out_shape=[_sds(w.shape) for w in ws] * 3, compiler_params=_cparams(0, 32 << 20))(*gs, *ws, *ms, *vs)
    return out[:n], out[n : 2 * n], out[2 * n :]


def _adam(g, w, m, v, name, land=None):
    r, c = w.shape
    tr = _row_tile(r, c, budget=1 << 20)

    def body(*refs):
        if land is not None:
            l_ref, w_ref, m_ref, v_ref, g_ref, d_ref, m2_ref, v2_ref = refs
            gv = _sum_chips(l_ref)
            g_ref[...] = gv
        else:
            g_in, w_ref, m_ref, v_ref, d_ref, m2_ref, v2_ref = refs
            gv = g_in[...]
        d, m2, v2 = _adam_math(w_ref[...], gv, m_ref[...], v_ref[...])
        d_ref[...] = d
        m2_ref[...] = m2
        v2_ref[...] = v2

    blk = _rows(tr, c)
    if land is not None:
        in_specs = [pl.BlockSpec((NCHIP, tr, c), lambda i: (0, i, 0)), blk, blk, blk]
        out = pl.pallas_call(
            body, name=name, grid=(r // tr,), in_specs=in_specs, out_specs=[blk] * 4, out_shape=[_sds((r, c))] * 4, compiler_params=_cparams(1)
        )(land, w, m, v)
        return out
    out = pl.pallas_call(
        body, name=name, grid=(r // tr,), in_specs=[blk] * 4, out_specs=[blk] * 3, out_shape=[_sds((r, c))] * 3, compiler_params=_cparams(1)
    )(g, w, m, v)
    return (g,) + tuple(out)


def _sum_slots(land, name):
    _, r, c = land.shape
    tr = _row_tile(r, c, budget=1 << 20)

    def body(l_ref, o_ref):
        o_ref[...] = _sum_chips(l_ref)

    return pl.pallas_call(
        body,
        name=name,
        grid=(r // tr,),
        in_specs=[pl.BlockSpec((NCHIP, tr, c), lambda i: (0, i, 0))],
        out_specs=_rows(tr, c),
        out_shape=_sds((r, c)),
        compiler_params=_cparams(1),
    )(land)


def _mod_fwd(call, ada_w, ada_b_loc):
    def body(c_ref, w_ref, b_ref, o_ref):
        s = _silu(c_ref[...]).astype(BF16)
        o_ref[...] = _mm(s, w_ref[...].astype(BF16)) + b_ref[...]

    return pl.pallas_call(
        body, name="mod_fwd", out_shape=_sds((call.shape[0], ada_w.shape[1])), compiler_params=_cparams(0, 32 << 20)
    )(call, ada_w, ada_b_loc)


def _mod_bwd(call, dm_loc, dmc_loc, ada_w):
    d, n = ada_w.shape
    pad = call.shape[0] - NDEV - 1

    def body(c_ref, dm_ref, dmc_ref, w_ref, gw_ref, gb_ref, cp_ref):
        cv = c_ref[...]
        sg = _sigmoid(cv)
        dmc = _colsum(dmc_ref[...])
        dm = dm_ref[...]
        rows = jnp.concatenate([dm, dmc, jnp.zeros((pad, n), F32)], axis=0)
        gw_ref[...] = _mm_tn((cv * sg).astype(BF16), rows.astype(BF16))
        gb_ref[...] = _colsum(dm) + dmc
        back = _mm_nt(rows[NDEV:].astype(BF16), w_ref[...].astype(BF16))
        c8, s8 = cv[NDEV : NDEV + 1], sg[NDEV : NDEV + 1]
        cp_ref[...] = back[0:1] * (s8 * (1.0 + c8 * (1.0 - s8)))

    return pl.pallas_call(
        body, name="mod_bwd", out_shape=[_sds((d, n)), _sds((1, n)), _sds((1, d))], compiler_params=_cparams(0, 40 << 20)
    )(call, dm_loc, dmc_loc, ada_w)


def _riding(body, nin, nout, nr, grid, copies, nscratch=0):
    def wrapped(*refs):
        base = nin + 2 * nr + nout
        start, finish = copies(refs[nin : nin + nr], refs[nin + nr + nout : base], *refs[base + nscratch :])
        first, last = True, True
        for ax, size in enumerate(grid):
            first = jnp.logical_and(first, pl.program_id(ax) == 0)
            last = jnp.logical_and(last, pl.program_id(ax) == size - 1)
        pl.when(first)(start)
        body(*refs[:nin], *refs[nin + nr : nin + nr + nout], *refs[base : base + nscratch])
        pl.when(last)(finish)

    return wrapped if nr else body


def _in_proj(xr, norm_g, scale, shift, wg, jsel, tm, name, shards=()):
    lx, d = xr.shape
    cb = wg.shape[2]
    nr = len(shards)

    def body(x_ref, g_ref, sc_ref, sh_ref, w_ref, p_ref, h_ref):
        xv = x_ref[...]
        r = lax.rsqrt(_rowmean(xv * xv) + EPS)
        hb = ((xv * r) * g_ref[...] * (1.0 + sc_ref[...]) + sh_ref[...]).astype(BF16)
        h_ref[...] = hb
        for q, j in enumerate(jsel):
            p_ref[:, q * cb : (q + 1) * cb] = _mm(hb, w_ref[j])

    vec = _whole((1, d))
    out = pl.pallas_call(
        _riding(body, 5, 2, nr, (lx // tm,), _gather_copies),
        name=name,
        grid=(lx // tm,),
        in_specs=[_rows(tm, d), vec, vec, vec, _whole(wg.shape)] + [_HBM] * nr,
        out_specs=[_rows(tm, len(jsel) * cb), _rows(tm, d)] + [_HBM] * nr,
        out_shape=[_sds((lx, len(jsel) * cb)), _sds((lx, d), BF16)] + _gather_shapes(shards),
        scratch_shapes=_gather_sems(nr) if nr else [],
        compiler_params=_cparams(1, VMEM_BIG),
    )(xr, norm_g, scale, shift, wg, *shards)
    return out[:2], out[2:]


def _in_bwd(xr, dxo, dup, du5, dypre, dz, wg, norm_g, scale, shift, d_skip, dg_init, tm):
    lx, d = xr.shape
    cb = wg.shape[2]
    pw = dup.shape[1]
    sw = du5.shape[1]
    mix = dz.shape[1]
    ncol = NDEV * cb

    def body(x_ref, dxo_ref, up_ref, d5_ref, dy_ref, dz_ref, w_ref, g_ref, sc_ref, sh_ref, dk_ref, gi_ref,
             gx_ref, dp_ref, dsc_ref, dsh_ref, dg_ref):
        i = pl.program_id(0)

        @pl.when(i == 0)
        def _():
            dsc_ref[...] = jnp.zeros_like(dsc_ref)
            dsh_ref[...] = jnp.zeros_like(dsh_ref)
            dg_ref[...] = gi_ref[...]

        dp = jnp.concatenate(
            [up_ref[...], d5_ref[...] + dk_ref[...] * dy_ref[...], dz_ref[...]], axis=1
        ).astype(BF16)
        dp_ref[...] = dp
        dh = _mm_nt(dp[:, 0:cb], w_ref[0])
        for j in range(1, NDEV):
            dh = dh + _mm_nt(dp[:, j * cb : (j + 1) * cb], w_ref[j])
        xv = x_ref[...]
        r = lax.rsqrt(_rowmean(xv * xv) + EPS)
        xh = xv * r
        g = g_ref[...]
        one_sc = 1.0 + sc_ref[...]
        dsh_ref[...] += _colsum(dh)
        dsc_ref[...] += _colsum(dh * (xh * g))
        dg_ref[...] += _colsum(dh * one_sc * xh)
        dxh = dh * one_sc * g
        gx_ref[...] = r * (dxh - xh * _rowmean(dxh * xh)) + dxo_ref[...]

    vec = _whole((1, d))
    return pl.pallas_call(
        body,
        name="in_bwd",
        grid=(lx // tm,),
        in_specs=[_rows(tm, d), _rows(tm, d), _rows(tm, pw), _rows(tm, sw), _rows(tm, sw), _rows(tm, mix), _whole(wg.shape), vec, vec, vec, _whole((1, sw)), vec],
        out_specs=[_rows(tm, d), _rows(tm, ncol), _acc((1, d)), _acc((1, d)), _acc((1, d))],
        out_shape=[_sds((lx, d)), _sds((lx, ncol), BF16), _sds((1, d)), _sds((1, d)), _sds((1, d))],
        compiler_params=_cparams(1, VMEM_BIG),
    )(xr, dxo, dup, du5, dypre, dz, wg, norm_g, scale, shift, d_skip, dg_init)


def _in_bwd_ctx(xc, duc, wg, jsel, norm_g, scale, shift):
    lc, d = xc.shape
    cb = wg.shape[2]

    def body(x_ref, du_ref, w_ref, g_ref, sc_ref, sh_ref, dp_ref, dsc_ref, dsh_ref, dg_ref):
        dp = du_ref[...].astype(BF16)
        dp_ref[...] = dp
        dh = _mm_nt(dp[:, 0:cb], w_ref[jsel[0]])
        for q in range(1, len(jsel)):
            dh = dh + _mm_nt(dp[:, q * cb : (q + 1) * cb], w_ref[jsel[q]])
        xv = x_ref[...]
        xh = xv * lax.rsqrt(_rowmean(xv * xv) + EPS)
        dsh_ref[...] = _colsum(dh)
        dsc_ref[...] = _colsum(dh * (xh * g_ref[...]))
        dg_ref[...] = _colsum(dh * (1.0 + sc_ref[...]) * xh)

    return pl.pallas_call(
        body,
        name="in_bwd_ctx",
        out_shape=[_sds(duc.shape, BF16), _sds((1, d)), _sds((1, d)), _sds((1, d))],
        compiler_params=_cparams(0, VMEM_BIG),
    )(xc, duc, wg, norm_g, scale, shift)


def _in_w_grad(hb, dpb, hcb, dpcb, cb, tm, parts=()):
    lx, d = hb.shape
    lc = hcb.shape[0]
    nb = lx // tm
    cg = 2 * cb
    nr = len(parts)
    grid = (NDEV // 2, nb + 1)

    def body(h_ref, dp_ref, hc_ref, dpc_ref, o_ref, acc):
        j, i = pl.program_id(0), pl.program_id(1)

        @pl.when(i == 0)
        def _():
            acc[...] = jnp.zeros_like(acc)

        @pl.when(i < nb)
        def _():
            h = h_ref[...]
            acc[0] += _mm_tn(h, dp_ref[:, 0:cb])
            acc[1] += _mm_tn(h, dp_ref[:, cb:cg])

        @pl.when(jnp.logical_and(i == nb, j == 1))
        def _():
            h = hc_ref[...]
            acc[0] += _mm_tn(h, dpc_ref[:, 0:cb])
            acc[1] += _mm_tn(h, dpc_ref[:, cb:cg])

        @pl.when(i == nb)
        def _():
            o_ref[...] = acc[...].astype(o_ref.dtype)

    out = pl.pallas_call(
        _riding(body, 4, 1, nr, grid, _exchange_copies, nscratch=1),
        name="in_w_grad",
        grid=grid,
        in_specs=[
            pl.BlockSpec((tm, d), lambda j, i: (jnp.minimum(i, nb - 1), 0)),
            pl.BlockSpec((tm, cg), lambda j, i: (jnp.minimum(i, nb - 1), j)),
            pl.BlockSpec((lc, d), lambda j, i: (0, 0)),
            pl.BlockSpec((lc, cg), lambda j, i: (0, 0)),
        ]
        + [_HBM] * nr,
        out_specs=[pl.BlockSpec((2, d, cb), lambda j, i: (j, 0, 0))] + [_HBM] * nr,
        out_shape=[_sds((NDEV, d, cb), BF16)] + [_sds(p.shape, p.dtype) for p in parts],
        scratch_shapes=[pltpu.VMEM((2, d, cb), F32)] + (_exchange_sems(nr) if nr else []),
        compiler_params=_cparams(2, VMEM_BIG),
    )(hb, dpb, hcb, dpcb, *parts)
    return out[0], out[1:]


def _pool_tables(w, rows, rb, pgw, transpose):
    t = np.arange(GRID_W)
    lo, hi = np.clip(t - w // 2, 0, GRID_W), np.clip(t + w - w // 2, 0, GRID_W)
    band = ((t[None, :] >= lo[:, None]) & (t[None, :] < hi[:, None])).astype(np.float32)
    if transpose:
        band = band.T
    kc = np.kron(np.eye(rb, dtype=np.float32), band)
    inv_c = np.tile((1.0 / (hi - lo).astype(np.float32))[:, None], (rb, pgw)).astype(np.float32)
    r = np.arange(rows)
    cnt_r = np.clip(r + w - w // 2, 0, rows) - np.clip(r - w // 2, 0, rows)
    inv_r = (1.0 / cnt_r.astype(np.float32)).astype(np.float32)
    return jnp.asarray(kc, BF16), jnp.asarray(inv_r), jnp.asarray(inv_c)


def _pool_stack(rows, rb, pgw, transpose):
    tabs = [_pool_tables(w, rows, rb, pgw, transpose) for w in POOL_WINDOWS]
    return tuple(jnp.stack([t[i] for t in tabs]) for i in range(3))


def _window_sums(ref, r0, rb, w):
    slab = lambda off: ref[pl.ds(pl.multiple_of((r0 + off) * GRID_W, GRID_W), GRID_W), :]
    shared = range(rb - 1, w)
    common = None
    for off in shared:
        common = slab(off) if common is None else common + slab(off)
    sums = []
    for rr in range(rb):
        s = common
        for off in range(rr, rr + w):
            if off not in shared:
                s = slab(off) if s is None else s + slab(off)
        sums.append(s)
    return sums


def _per_window(group):
    for k, w in enumerate(POOL_WINDOWS):
        pl.when(pl.program_id(0) == k)(lambda k=k, w=w: group(k, w))


def _pool_fwd(proj, pool_wf):
    lx = proj.shape[0]
    ng, pgw, _ = pool_wf.shape
    rows = lx // GRID_W
    rb = min(4, rows)
    tok = rb * GRID_W
    kc, inv_r, inv_c = _pool_stack(rows, rb, pgw, False)

    def body(p_ref, w_ref, kc_ref, ir_ref, ic_ref, lin_ref, dm_ref, xp):
        def group(kk, w):
            lo = w // 2
            xp[pl.ds(0, lo * GRID_W), :] = jnp.zeros((lo * GRID_W, pgw), F32)
            xp[pl.ds((lo + rows) * GRID_W, (w - lo) * GRID_W), :] = jnp.zeros(((w - lo) * GRID_W, pgw), F32)
            xp[pl.ds(lo * GRID_W, lx), :] = p_ref[...]

            def blk(b, carry):
                r0 = b * rb
                parts = [s * ir_ref[kk, r0 + rr] for rr, s in enumerate(_window_sums(xp, r0, rb, w))]
                m = _mm_split(kc_ref[0], jnp.concatenate(parts, axis=0)) * ic_ref[0]
                u = xp[pl.ds(pl.multiple_of((r0 + lo) * GRID_W, GRID_W), tok), :]
                db = (m - u).astype(BF16)
                o0 = pl.multiple_of(r0 * GRID_W, GRID_W)
                dm_ref[pl.ds(o0, tok), :] = db
                lin_ref[pl.ds(o0, tok), :] = _mm(db, w_ref[0])
                return carry

            lax.fori_loop(0, rows // rb, blk, 0)

        _per_window(group)

    col = pl.BlockSpec((lx, pgw), lambda g: (0, g))
    return pl.pallas_call(
        body,
        name="pool_fwd",
        grid=(ng,),
        in_specs=[col, _gspec(1, pgw, pgw), _gspec(1, tok, tok), _SMEM, _gspec(1, tok, pgw)],
        out_specs=[col, col],
        out_shape=[_sds((lx, ng * pgw)), _sds((lx, ng * pgw), BF16)],
        scratch_shapes=[pltpu.VMEM(((rows + max(POOL_WINDOWS)) * GRID_W, pgw), F32)],
        compiler_params=_cparams(1, VMEM_BIG),
    )(proj, pool_wf, kc, inv_r, inv_c)


def _pool_bwd(dlin, pool_wf, dmat):
    lx = dlin.shape[0]
    ng, pgw, _ = pool_wf.shape
    rows = lx // GRID_W
    rb = min(4, rows)
    tok = rb * GRID_W
    kct, inv_r, inv_c = _pool_stack(rows, rb, pgw, True)

    def body(dl_ref, w_ref, kc_ref, ir_ref, ic_ref, dm_ref, du_ref, dw_ref, tp):
        def group(kk, w):
            front = w - w // 2 - 1
            if front:
                tp[pl.ds(0, front * GRID_W), :] = jnp.zeros((front * GRID_W, pgw), F32)
            tp[pl.ds((front + rows) * GRID_W, (w - front) * GRID_W), :] = jnp.zeros(((w - front) * GRID_W, pgw), F32)
            dw_ref[...] = jnp.zeros_like(dw_ref)

            def blk(b, carry):
                o0 = pl.multiple_of(b * tok, GRID_W)
                dlb = dl_ref[pl.ds(o0, tok), :].astype(BF16)
                dw_ref[0] += _mm_tn(dm_ref[pl.ds(o0, tok), :], dlb)
                dd = _mm_nt(dlb, w_ref[0])
                du_ref[pl.ds(o0, tok), :] = -dd
                t = _mm(kc_ref[0], (dd * ic_ref[0]).astype(BF16))
                for rr in range(rb):
                    dst = pl.multiple_of((b * rb + rr + front) * GRID_W, GRID_W)
                    tp[pl.ds(dst, GRID_W), :] = t[rr * GRID_W : (rr + 1) * GRID_W] * ir_ref[kk, b * rb + rr]
                return carry

            lax.fori_loop(0, rows // rb, blk, 0)

            def rowl(b, carry):
                for rr, s in enumerate(_window_sums(tp, b * rb, rb, w)):
                    du_ref[pl.ds(pl.multiple_of((b * rb + rr) * GRID_W, GRID_W), GRID_W), :] += s
                return carry

            lax.fori_loop(0, rows // rb, rowl, 0)

        _per_window(group)

    col = pl.BlockSpec((lx, pgw), lambda g: (0, g))
    return pl.pallas_call(
        body,
        name="pool_bwd",
        grid=(ng,),
        in_specs=[col, _gspec(1, pgw, pgw), _gspec(1, tok, tok), _SMEM, _gspec(1, tok, pgw), col],
        out_specs=[col, _gspec(1, pgw, pgw)],
        out_shape=[_sds((lx, ng * pgw)), _sds((ng, pgw, pgw))],
        scratch_shapes=[pltpu.VMEM(((rows + max(POOL_WINDOWS)) * GRID_W, pgw), F32)],
        compiler_params=_cparams(1, VMEM_BIG),
    )(dlin, pool_wf, kct, inv_r, inv_c, dmat)


def _s5_small(lam_re, lam_im, log_dt, b_re, b_im, c_re, c_im):
    t = CHUNK_T
    dt = jnp.exp(log_dt)[..., None]
    zr, zi = lam_re * dt, lam_im * dt
    tau = jnp.arange(t + 1, dtype=F32)
    mag = jnp.exp(zr[..., None] * tau)
    pr, pi = mag * jnp.cos(zi[..., None] * tau), mag * jnp.sin(zi[..., None] * tau)
    ar, ai = pr[..., 1], pi[..., 1]
    den = lam_re * lam_re + lam_im * lam_im
    qr = ((ar - 1.0) * lam_re + ai * lam_im) / den
    qi = (ai * lam_re - (ar - 1.0) * lam_im) / den
    bbr = qr[..., None] * b_re - qi[..., None] * b_im
    bbi = qr[..., None] * b_im + qi[..., None] * b_re
    ctr, cti = jnp.swapaxes(c_re, 2, 3), jnp.swapaxes(c_im, 2, 3)
    lane_pad = lambda v: jnp.pad(jnp.swapaxes(v, 0, 1), ((0, 0), (0, 0), (0, 0), (0, LANES - v.shape[-1])))
    tp = lambda v: jnp.swapaxes(v, 1, 2)
    flip = lambda v: v[..., ::-1]
    pfr, pfi = tp(flip(pr[0, ..., :t])), tp(flip(pi[0, ..., :t]))
    pbr, pbi = tp(pr[1, ..., :t]), tp(pi[1, ..., :t])
    bfr, bfi, bbr_t, bbi_t = tp(bbr[0]), tp(bbi[0]), tp(bbr[1]), tp(bbi[1])
    a1 = jnp.concatenate([pfr, pbr, pfr, pbr], axis=-1)
    a2 = jnp.concatenate([-pfi, -pbi, pfi, pbi], axis=-1)
    b1 = jnp.concatenate([bfr, bbr_t, bfi, bbi_t], axis=-1)
    b2 = jnp.concatenate([bfi, bbi_t, bfr, bbr_t], axis=-1)
    a16 = jnp.concatenate([pr[0, ..., t], pr[1, ..., t], pi[0, ..., t], pi[1, ..., t]], axis=1)
    return (lane_pad(pr), lane_pad(pi), lane_pad(bbr), lane_pad(bbi), lane_pad(ctr), lane_pad(cti), a1, a2, b1, b2), a16


def _split2(v):
    hi = v.astype(BF16)
    return hi, (v - hi.astype(F32)).astype(BF16)


def _dotx(a, b, mm=_mm):
    ah, al = (a, None) if a.dtype == BF16 else _split2(a)
    bh, bl = (b, None) if b.dtype == BF16 else _split2(b)
    out = mm(ah, bh)
    if bl is not None:
        out = out + mm(ah, bl)
    if al is not None:
        out = out + mm(al, bh)
    return out


def _shift_lanes(v, k):
    if k == 0:
        return v
    n = v.shape[-1]
    lane = lax.broadcasted_iota(jnp.int32, v.shape, v.ndim - 1)
    r = pltpu.roll(v, k % n, axis=v.ndim - 1)
    return jnp.where(lane >= k, r, 0.0) if k > 0 else jnp.where(lane < n + k, r, 0.0)


def _op_consts():
    t, h = CHUNK_T, SSM_H
    j, o = np.divmod(np.arange(t * h), h)
    p = np.arange(LANES)[:, None]
    sel = lambda col: (p == col[None, :]).astype(np.float32)
    pw = np.stack([np.concatenate([sel(j), sel(j + 1)], axis=1), np.concatenate([sel(t - 1 - j), sel(t - j)], axis=1)])
    return jnp.asarray(pw, BF16), jnp.asarray(sel(o), BF16), jnp.asarray(sel(j).T, BF16), jnp.asarray(sel(o).T, BF16)


def _op_factors(gi, pr_ref, pi_ref, br_ref, bi_ref, cr_ref, ci_ref, pw_ref, ch_ref):
    n = pr_ref.shape[2]
    c16 = _dotx(jnp.concatenate([cr_ref[gi, 0], ci_ref[gi, 0], cr_ref[gi, 1], ci_ref[gi, 1]], axis=0), ch_ref[...])
    out = []
    for d in range(2):
        e = _dotx(jnp.concatenate([pr_ref[gi, d], pi_ref[gi, d]], axis=0), pw_ref[d])
        c = c16[2 * d * n : (2 * d + 1) * n], c16[(2 * d + 1) * n : (2 * d + 2) * n]
        bst = jnp.concatenate([br_ref[gi, d], -bi_ref[gi, d]], axis=0)
        out.append(((e[:n, :CW], e[n:, :CW]), (e[:n, CW:], e[n:, CW:]), c, bst))
    return out


def _op_spread(m_ref, u_ref, v_ref, gi):
    return _dotx(m_ref[...], jnp.concatenate([_rows_pad(u_ref[gi]), _rows_pad(v_ref[gi])], axis=1))


def _cmul(x, y):
    return x[0] * y[0] - x[1] * y[1], x[0] * y[1] + x[1] * y[0]


def _cmul_conj(g, y):
    return g[0] * y[0] + g[1] * y[1], g[1] * y[0] - g[0] * y[1]


def _rows_pad(v):
    return jnp.concatenate([v, jnp.zeros((LANES - v.shape[0], v.shape[1]), F32)], axis=0)


def _s5_build(small, gb, shards):
    pr, pi, br, bi, cr, ci, a1, a2, b1, b2 = small
    g, _, n, _ = pr.shape
    t, h, ns = CHUNK_T, SSM_H, 4 * n
    pw, ch, rep_s, til_s = _op_consts()
    nr, nin, steps = len(shards), 14, g // gb

    def body(*refs):
        pr_ref, pi_ref, br_ref, bi_ref, cr_ref, ci_ref, a1_ref, a2_ref, b1_ref, b2_ref, pw_ref, ch_ref, rs_ref, ts_ref = refs[:nin]
        t_ref, w_ref, c_ref = refs[nin + nr : nin + nr + 3]
        start, finish = _gather_copies(refs[nin : nin + nr], refs[nin + nr + 3 : nin + 2 * nr + 3], *refs[nin + 2 * nr + 3 :])
        pl.when(pl.program_id(0) == 0)(start)

        def group(gi, carry):
            rows, outs = [], []
            for e16, e1, c16, bst in _op_factors(gi, pr_ref, pi_ref, br_ref, bi_ref, cr_ref, ci_ref, pw_ref, ch_ref):
                rows.append(_dotx(bst, jnp.concatenate(_cmul(c16, e16), axis=0), _mm_tn)[:h])
                outs.append(_cmul(c16, e1))
            c_ref[gi] = jnp.concatenate([outs[0][0], outs[1][0], -outs[0][1], -outs[1][1]], axis=0).astype(BF16)
            for s in range(t):
                t_ref[gi, s * h : (s + 1) * h, :] = (_shift_lanes(rows[0], h * s) + _shift_lanes(rows[1], -h * (t - 1 - s))).astype(BF16)
            ae, be = _op_spread(rs_ref, a1_ref, a2_ref, gi), _op_spread(ts_ref, b1_ref, b2_ref, gi)
            w_ref[gi] = (ae[:, :ns] * be[:, :ns] + ae[:, ns:] * be[:, ns:]).astype(BF16)
            return carry

        lax.fori_loop(0, gb, group, 0)
        pl.when(pl.program_id(0) == steps - 1)(finish)

    sm = pl.BlockSpec((gb, 2, n, LANES), lambda i: (i, 0, 0, 0))
    ab = lambda v: pl.BlockSpec((gb,) + v.shape[1:], lambda i: (i, 0, 0))
    out = pl.pallas_call(
        body,
        name="s5_build",
        grid=(steps,),
        in_specs=[sm] * 6 + [ab(a1), ab(a2), ab(b1), ab(b2), _whole(pw.shape), _whole(ch.shape), _whole(rep_s.shape), _whole(til_s.shape)] + [_HBM] * nr,
        out_specs=[_gspec(gb, CW, CW), _gspec(gb, CW, 4 * n), _gspec(gb, 4 * n, CW)] + [_HBM] * nr,
        out_shape=[_sds((g, CW, CW), BF16), _sds((g, CW, 4 * n), BF16), _sds((g, 4 * n, CW), BF16)] + _gather_shapes(shards),
        scratch_shapes=_gather_sems(nr),
        compiler_params=_cparams(1, VMEM_BIG),
    )(pr, pi, br, bi, cr, ci, a1, a2, b1, b2, pw, ch, rep_s, til_s, *shards)
    return out[:3], out[3:]


def _s5_build_bwd(small, dtsum, dwst, dccat, gb, parts):
    pr, pi, br, bi, cr, ci, a1, a2, b1, b2 = small
    g, _, n, _ = pr.shape
    t, h, ns = CHUNK_T, SSM_H, 4 * n
    pw, ch, rep_s, til_s = _op_consts()
    nr, nin, nout, steps = len(parts), 17, 10, g // gb

    def body(*refs):
        pr_ref, pi_ref, br_ref, bi_ref, cr_ref, ci_ref, a1_ref, a2_ref, b1_ref, b2_ref, pw_ref, ch_ref, rs_ref, ts_ref, dt_ref, dw_ref, dc_ref = refs[:nin]
        dpr_ref, dpi_ref, dbr_ref, dbi_ref, dcr_ref, dci_ref, da1_ref, da2_ref, db1_ref, db2_ref = refs[nin + nr : nin + nr + nout]
        start, finish = _exchange_copies(refs[nin : nin + nr], refs[nin + nr + nout : nin + 2 * nr + nout], *refs[nin + 2 * nr + nout :])
        pl.when(pl.program_id(0) == 0)(start)

        def group(gi, carry):
            drow = [jnp.zeros((h, CW), F32), jnp.zeros((h, CW), F32)]
            for s in range(t):
                blk = dt_ref[gi, s * h : (s + 1) * h, :]
                drow[0] = drow[0] + _shift_lanes(blk, -h * s)
                drow[1] = drow[1] + _shift_lanes(blk, h * (t - 1 - s))
            dcc = dc_ref[gi]
            dcs = []
            for d, (e16, e1, c16, bst) in enumerate(_op_factors(gi, pr_ref, pi_ref, br_ref, bi_ref, cr_ref, ci_ref, pw_ref, ch_ref)):
                dr = _rows_pad(drow[d])
                db = _dotx(jnp.concatenate(_cmul(c16, e16), axis=0), dr, _mm_nt)
                dbr_ref[gi, d] = db[:n]
                dbi_ref[gi, d] = -db[n:]
                dst = _dotx(bst, dr)
                dca = dst[:n], dst[n:]
                gx = dcc[d * n : (d + 1) * n], -dcc[(2 + d) * n : (3 + d) * n]
                dc16, dc16b = _cmul_conj(dca, e16), _cmul_conj(gx, e1)
                dcs += [dc16[0] + dc16b[0], dc16[1] + dc16b[1]]
                de = jnp.concatenate([jnp.concatenate(_cmul_conj(dca, c16), axis=0), jnp.concatenate(_cmul_conj(gx, c16), axis=0)], axis=1)
                dp = _dotx(de, pw_ref[d], _mm_nt)
                dpr_ref[gi, d] = dp[:n]
                dpi_ref[gi, d] = dp[n:]
            dct = _dotx(jnp.concatenate(dcs, axis=0), ch_ref[...], _mm_nt)
            for d in range(2):
                dcr_ref[gi, d] = dct[2 * d * n : (2 * d + 1) * n]
                dci_ref[gi, d] = dct[(2 * d + 1) * n : (2 * d + 2) * n]
            dw = dw_ref[gi]
            ae, be = _op_spread(rs_ref, a1_ref, a2_ref, gi), _op_spread(ts_ref, b1_ref, b2_ref, gi)
            da = _dotx(rs_ref[...], jnp.concatenate([dw * be[:, :ns], dw * be[:, ns:]], axis=1), _mm_tn)
            db = _dotx(ts_ref[...], jnp.concatenate([dw * ae[:, :ns], dw * ae[:, ns:]], axis=1), _mm_tn)
            da1_ref[gi] = da[:t, :ns]
            da2_ref[gi] = da[:t, ns:]
            db1_ref[gi] = db[:h, :ns]
            db2_ref[gi] = db[:h, ns:]
            return carry

        lax.fori_loop(0, gb, group, 0)
        pl.when(pl.program_id(0) == steps - 1)(finish)

    sm = pl.BlockSpec((gb, 2, n, LANES), lambda i: (i, 0, 0, 0))
    ab = lambda v: pl.BlockSpec((gb,) + v.shape[1:], lambda i: (i, 0, 0))
    ops = [_gspec(gb, CW, CW), _gspec(gb, CW, 4 * n), _gspec(gb, 4 * n, CW)]
    out = pl.pallas_call(
        body,
        name="s5_build_bwd",
        grid=(steps,),
        in_specs=[sm] * 6 + [ab(a1), ab(a2), ab(b1), ab(b2), _whole(pw.shape), _whole(ch.shape), _whole(rep_s.shape), _whole(til_s.shape)] + ops + [_HBM] * nr,
        out_specs=[sm] * 6 + [ab(a1), ab(a2), ab(b1), ab(b2)] + [_HBM] * nr,
        out_shape=[_sds(v.shape) for v in small] + [_sds(p.shape, p.dtype) for p in parts],
        scratch_shapes=_exchange_sems(nr),
        compiler_params=_cparams(1, VMEM_BIG),
    )(pr, pi, br, bi, cr, ci, a1, a2, b1, b2, pw, ch, rep_s, til_s, dtsum, dwst, dccat, *parts)
    return out[:nout], out[nout:]


SCAN_UNROLL = 4
GPL = LANES // SSM_H
CW = CHUNK_T * SSM_H
BW = CHUNK_T * LANES


def _chunk_perm():
    o = np.arange(BW)
    src = ((o % CW) // SSM_H) * LANES + (o // CW) * SSM_H + o % SSM_H
    return jnp.asarray(np.arange(BW)[:, None] == src[None, :], BF16)


def _gspec(gb, a, b):
    return pl.BlockSpec((gb, a, b), lambda i: (i, 0, 0))


def _chunk_rows(ref, n):
    return jnp.concatenate([ref[pl.ds(s, n, stride=CHUNK_T), :] for s in range(CHUNK_T)], axis=1)


def _s5_state(proj, col0, uc, perm, wst, ncc, ncl):
    nlb = uc.shape[1] // LANES
    g, _, ns = wst.shape
    nch = ncc + ncl
    half = ns // 2

    def body(p_ref, c_ref, pm_ref, w_ref, u_ref, sr_ref, si_ref):
        xrows = jnp.concatenate([_chunk_rows(c_ref, ncc), _chunk_rows(p_ref, ncl)], axis=0).astype(BF16)
        u = _mm(xrows, pm_ref[...]).astype(BF16)
        u_ref[0] = u
        for gi in range(GPL):
            s = _mm(u[:, gi * CW : (gi + 1) * CW], w_ref[gi].astype(BF16))
            sr_ref[gi] = s[:, :half]
            si_ref[gi] = s[:, half:]

    return pl.pallas_call(
        body,
        name="s5_state",
        grid=(nlb,),
        in_specs=[
            pl.BlockSpec((ncl * CHUNK_T, LANES), lambda i: (0, col0 + i)),
            pl.BlockSpec((ncc * CHUNK_T, LANES), lambda i: (0, i)),
            _whole(perm.shape),
            _gspec(GPL, CW, ns),
        ],
        out_specs=[pl.BlockSpec((1, nch, BW), lambda i: (i, 0, 0)), _gspec(GPL, nch, half), _gspec(GPL, nch, half)],
        out_shape=[_sds((nlb, nch, BW), BF16), _sds((g, nch, half)), _sds((g, nch, half))],
        compiler_params=_cparams(1, VMEM_BIG),
    )(proj, uc, perm, wst)


def _unrolled_loop(n, step, init):
    u = SCAN_UNROLL if n % SCAN_UNROLL == 0 else 1

    def trip(i, c):
        for k in range(u):
            c = step(i * u + k, c)
        return c

    return lax.fori_loop(0, n // u, trip, init)


def _lane_masks(gb, half):
    lane = lax.broadcasted_iota(jnp.int32, (gb, half), 1)
    return lane < (half // 2)


def _roll_half(v):
    return pltpu.roll(v, v.shape[-1] // 2, axis=v.ndim - 1)


def _pack_dirs(lo, hi, mf):
    return jnp.where(mf, lo, _roll_half(hi)), jnp.where(mf, _roll_half(lo), hi)


def _unpack_dirs(pf, pb, mf):
    return jnp.where(mf, pf, _roll_half(pb)), jnp.where(mf, _roll_half(pf), pb)


def _packed_spec(gb, nch, half, buffers=None):
    mode = {} if buffers is None else dict(pipeline_mode=pl.Buffered(buffers))
    if gb == 2 * GPL:
        return pl.BlockSpec((GPL, nch, half), lambda i: (i // 2, 0, 0), **mode)
    assert gb == GPL
    return pl.BlockSpec((gb // 2, nch, half), lambda i: (i, 0, 0), **mode)


def _pair_index(j, gb):
    if gb == 2 * GPL:
        return j, pl.program_id(0) % 2
    return j % (gb // 2), j // (gb // 2)


def _pick(which, lo, hi):
    return (hi if which else lo) if isinstance(which, int) else jnp.where(which == 0, lo, hi)


def _unpacked(pf_ref, pb_ref, j, gb, mf):
    jl, which = _pair_index(j, gb)
    return _pick(which, *_unpack_dirs(pf_ref[jl], pb_ref[jl], mf))


def _pair_tile(refs, jl, rows):
    return jnp.concatenate([r[jl, rows, :] for r in refs], axis=1)


def _pair_rows_of(op, which, q):
    z = jnp.zeros((q, op.shape[1]), op.dtype)
    blocks = [op[0:q], op[2 * q : 3 * q], op[q : 2 * q], op[3 * q : 4 * q]]
    lo = jnp.concatenate([x for b in blocks for x in (b, z)], axis=0)
    hi = jnp.concatenate([x for b in blocks for x in (z, b)], axis=0)
    return _pick(which, lo, hi)


def _group_rows_of(pair_rows, which, q):
    def take(off):
        blk = lambda k: pair_rows[2 * q * k + off : 2 * q * k + off + q]
        return jnp.concatenate([blk(0), blk(2), blk(1), blk(3)], axis=0)

    return _pick(which, take(0), take(q))


def _scan_rows(ref, row, hp, nch, mf):
    return _pack_dirs(ref[pl.ds(row, hp, stride=nch), :], ref[pl.ds(hp * nch + row, hp, stride=nch), :], mf)


def _s5_scan_fwd(s_re, s_im, a16, ncc, nch, gb):
    g, ns = a16.shape
    half = ns // 2
    hp = gb // 2

    def body(sr_ref, si_ref, a_ref, hfr_ref, hfi_ref, hbr_ref, hbi_ref):
        mf = _lane_masks(hp, half)
        afr, abr = _pack_dirs(a_ref[:hp, :half], a_ref[hp:, :half], mf)
        afi, abi = _pack_dirs(a_ref[:hp, half:], a_ref[hp:, half:], mf)
        zero = jnp.zeros((hp, half), F32)

        def step(t, c):
            fr, fi, br, bi = c
            rb = jnp.where(t < ncc, ncc - 1 - t, nch - 1 - (t - ncc))
            hfr_ref[pl.ds(t, hp, stride=nch), :] = fr
            hfi_ref[pl.ds(t, hp, stride=nch), :] = fi
            hbr_ref[pl.ds(rb, hp, stride=nch), :] = br
            hbi_ref[pl.ds(rb, hp, stride=nch), :] = bi
            sfr, sfi = _scan_rows(sr_ref, t, hp, nch, mf)[0], _scan_rows(si_ref, t, hp, nch, mf)[0]
            sbr, sbi = _scan_rows(sr_ref, rb, hp, nch, mf)[1], _scan_rows(si_ref, rb, hp, nch, mf)[1]
            return (afr * fr - afi * fi + sfr, afr * fi + afi * fr + sfi, abr * br - abi * bi + sbr, abr * bi + abi * br + sbi)

        _unrolled_loop(nch, step, (zero, zero, zero, zero))

    blk = pl.BlockSpec((gb * nch, half), lambda i: (i, 0))
    pblk = pl.BlockSpec((hp * nch, half), lambda i: (i, 0))
    return pl.pallas_call(
        body,
        name="s5_scan_fwd",
        grid=(g // gb,),
        in_specs=[blk, blk, pl.BlockSpec((gb, ns), lambda i: (i, 0))],
        out_specs=[pblk] * 4,
        out_shape=[_sds((g // 2 * nch, half))] * 4,
        compiler_params=_cparams(1, VMEM_BIG),
    )(s_re, s_im, a16)


def _token_rows_store(ref, val, n):
    for s in range(CHUNK_T):
        ref[pl.ds(s, n, stride=CHUNK_T), :] = val[:, s * LANES : (s + 1) * LANES]


def _s5_out(u_all, h4, tsum, ccat, perm, ncc, lx, gb):
    nlb, nch, _ = u_all.shape
    g, ns, _ = ccat.shape
    ncl = nch - ncc
    half = ns // 2

    def body(u_ref, hfr_ref, hfi_ref, hbr_ref, hbi_ref, t_ref, c_ref, pm_ref, y_ref):
        parts = []
        for gi in range(GPL):
            u = u_ref[0, ncc:, gi * CW : (gi + 1) * CW]
            jl, which = _pair_index(gi, gb)
            hs = _pair_tile((hfr_ref, hfi_ref, hbr_ref, hbi_ref), jl, slice(ncc, None)).astype(BF16)
            parts.append(_mm(u, t_ref[gi].astype(BF16)) + _mm(hs, _pair_rows_of(c_ref[gi].astype(BF16), which, ns // 4)))
        _token_rows_store(y_ref, _mm_nt(jnp.concatenate(parts, axis=1).astype(BF16), pm_ref[...]), ncl)

    return pl.pallas_call(
        body,
        name="s5_out",
        grid=(nlb,),
        in_specs=[pl.BlockSpec((1, nch, BW), lambda i: (i, 0, 0))] + [_packed_spec(gb, nch, half)] * 4 + [_gspec(GPL, CW, CW), _gspec(GPL, ns, CW), _whole(perm.shape)],
        out_specs=pl.BlockSpec((lx, LANES), lambda i: (0, i)),
        out_shape=_sds((lx, nlb * LANES)),
        compiler_params=_cparams(1, VMEM_BIG),
    )(u_all, *h4, tsum, ccat, perm)


def _s5_dstate(dypre, perm, ccat, ncc, swaps=()):
    lx, sw = dypre.shape
    nr = len(swaps)
    nlb = sw // LANES
    g, ns, _ = ccat.shape
    ncl = lx // CHUNK_T
    nch = ncl + ncc
    half = ns // 2

    def body(dy_ref, pm_ref, c_ref, dyr_ref, dr_ref, di_ref):
        dy = _mm(_chunk_rows(dy_ref, ncl).astype(BF16), pm_ref[...]).astype(BF16)
        dyr_ref[0] = dy
        for gi in range(GPL):
            dh = _mm_nt(dy[:, gi * CW : (gi + 1) * CW], c_ref[gi].astype(BF16))
            dr_ref[gi, :ncc, :] = jnp.zeros((ncc, half), F32)
            di_ref[gi, :ncc, :] = jnp.zeros((ncc, half), F32)
            dr_ref[gi, ncc:, :] = dh[:, :half]
            di_ref[gi, ncc:, :] = dh[:, half:]

    out = pl.pallas_call(
        _riding(body, 3, 3, nr, (nlb,), _swap_copies),
        name="s5_dstate",
        grid=(nlb,),
        in_specs=[pl.BlockSpec((lx, LANES), lambda i: (0, i)), _whole(perm.shape), _gspec(GPL, ns, CW)] + [_HBM] * nr,
        out_specs=[pl.BlockSpec((1, ncl, BW), lambda i: (i, 0, 0)), _gspec(GPL, nch, half), _gspec(GPL, nch, half)] + [_HBM] * nr,
        out_shape=[_sds((nlb, ncl, BW), BF16), _sds((g, nch, half)), _sds((g, nch, half))] + _swap_shapes(swaps),
        scratch_shapes=_swap_sems(nr) if nr else [],
        compiler_params=_cparams(1, VMEM_BIG),
    )(dypre, perm, ccat, *swaps)
    return out[:3], out[3:]


def _s5_scan_bwd(dh_re, dh_im, a16, ncc, nch, gb):
    g, ns = a16.shape
    half = ns // 2
    ncl = nch - ncc
    hp = gb // 2

    def body(dr_ref, di_ref, a_ref, sfr_ref, sfi_ref, sbr_ref, sbi_ref):
        mf = _lane_masks(hp, half)
        afr, abr = _pack_dirs(a_ref[:hp, :half], a_ref[hp:, :half], mf)
        afi, abi = _pack_dirs(a_ref[:hp, half:], a_ref[hp:, half:], mf)
        zero = jnp.zeros((hp, half), F32)

        def step(t, c):
            lfr, lfi, lbr, lbi = c
            pf = nch - 1 - t
            pb = jnp.where(t < ncl, ncc + t, t - ncl)
            sfr_ref[pl.ds(pf, hp, stride=nch), :] = lfr
            sfi_ref[pl.ds(pf, hp, stride=nch), :] = lfi
            sbr_ref[pl.ds(pb, hp, stride=nch), :] = lbr
            sbi_ref[pl.ds(pb, hp, stride=nch), :] = lbi
            dfr, dfi = _scan_rows(dr_ref, pf, hp, nch, mf)[0], _scan_rows(di_ref, pf, hp, nch, mf)[0]
            dbr, dbi = _scan_rows(dr_ref, pb, hp, nch, mf)[1], _scan_rows(di_ref, pb, hp, nch, mf)[1]
            return (dfr + afr * lfr + afi * lfi, dfi + afr * lfi - afi * lfr, dbr + abr * lbr + abi * lbi, dbi + abr * lbi - abi * lbr)

        _unrolled_loop(nch, step, (zero,) * 4)

    blk = pl.BlockSpec((gb * nch, half), lambda i: (i, 0))
    pblk = pl.BlockSpec((hp * nch, half), lambda i: (i, 0))
    return pl.pallas_call(
        body,
        name="s5_scan_bwd",
        grid=(g // gb,),
        in_specs=[blk, blk, pl.BlockSpec((gb, ns), lambda i: (i, 0))],
        out_specs=[pblk] * 4,
        out_shape=[_sds((g // 2 * nch, half))] * 4,
        compiler_params=_cparams(1, VMEM_BIG),
    )(dh_re, dh_im, a16)


def _s5_du(u_all, dy_all, ds4, tsum, wst, perm, ncc, lx, lc, gb):
    nlb, ncl, _ = dy_all.shape
    g, _, ns = wst.shape
    nch = ncc + ncl
    half = ns // 2

    def body(u_ref, dy_ref, sfr_ref, sfi_ref, sbr_ref, sbi_ref, t_ref, w_ref, pm_ref, du_ref, dc_ref, dw_ref):
        parts = []
        mf = _lane_masks(nch, half)
        for gi in range(GPL):
            ds = jnp.concatenate([_unpacked(sfr_ref, sbr_ref, gi, gb, mf), _unpacked(sfi_ref, sbi_ref, gi, gb, mf)], axis=1).astype(BF16)
            dw_ref[gi] = _mm_tn(u_ref[0, :, gi * CW : (gi + 1) * CW], ds)
            d_all = _mm_nt(ds, w_ref[gi].astype(BF16))
            d_lat = d_all[ncc:] + _mm_nt(dy_ref[0, :, gi * CW : (gi + 1) * CW], t_ref[gi].astype(BF16))
            parts.append(jnp.concatenate([d_all[:ncc], d_lat], axis=0))
        du = _mm_nt(jnp.concatenate(parts, axis=1).astype(BF16), pm_ref[...])
        _token_rows_store(dc_ref, du[:ncc], ncc)
        _token_rows_store(du_ref, du[ncc:], ncl)

    return pl.pallas_call(
        body,
        name="s5_du",
        grid=(nlb,),
        in_specs=[pl.BlockSpec((1, nch, BW), lambda i: (i, 0, 0)), pl.BlockSpec((1, ncl, BW), lambda i: (i, 0, 0))]
        + [_packed_spec(gb, nch, half)] * 4
        + [_gspec(GPL, CW, CW), _gspec(GPL, CW, ns), _whole(perm.shape)],
        out_specs=[pl.BlockSpec((lx, LANES), lambda i: (0, i)), pl.BlockSpec((lc, LANES), lambda i: (0, i)), _gspec(GPL, CW, ns)],
        out_shape=[_sds((lx, nlb * LANES)), _sds((lc, nlb * LANES)), _sds((g, CW, ns))],
        compiler_params=_cparams(1, VMEM_BIG),
    )(u_all, dy_all, *ds4, tsum, wst, perm)


def _s5_op_grads(u_all, dy_all, ds4, h4, ncc, gb):
    nlb, nch, _ = u_all.shape
    half = ds4[0].shape[2]
    g = 2 * ds4[0].shape[0]
    ns = 2 * half

    def body(u_ref, dy_ref, sfr_ref, sfi_ref, sbr_ref, sbi_ref, hfr_ref, hfi_ref, hbr_ref, hbi_ref, dt_ref, dc_ref, da_ref):
        das = []
        mf1 = _lane_masks(1, half)
        for gi in range(GPL):
            cols = slice(gi * CW, (gi + 1) * CW)
            jl, which = _pair_index(gi, gb)
            dy = dy_ref[0, :, cols]
            dt_ref[gi] = _mm_tn(u_ref[0, ncc:, cols], dy)
            hs = _pair_tile((hfr_ref, hfi_ref, hbr_ref, hbi_ref), jl, slice(ncc, None)).astype(BF16)
            dc_ref[gi] = _group_rows_of(_mm_tn(hs, dy), which, ns // 4)
            hfr, hfi, hbr, hbi = hfr_ref[jl], hfi_ref[jl], hbr_ref[jl], hbi_ref[jl]
            sfr, sfi, sbr, sbi = sfr_ref[jl], sfi_ref[jl], sbr_ref[jl], sbi_ref[jl]
            a_re = _pick(which, *_unpack_dirs(_colsum(hfr * sfr + hfi * sfi), _colsum(hbr * sbr + hbi * sbi), mf1))
            a_im = _pick(which, *_unpack_dirs(_colsum(hfr * sfi - hfi * sfr), _colsum(hbr * sbi - hbi * sbr), mf1))
            das.append(jnp.concatenate([a_re, a_im], axis=1))
        da_ref[...] = jnp.concatenate(das, axis=0)

    return pl.pallas_call(
        body,
        name="s5_op_grads",
        grid=(nlb,),
        in_specs=[pl.BlockSpec((1, nch, BW), lambda i: (i, 0, 0)), pl.BlockSpec((1, nch - ncc, BW), lambda i: (i, 0, 0))]
        + [_packed_spec(gb, nch, half)] * 4
        + [_packed_spec(gb, nch, half, buffers=1)] * 4,
        out_specs=[_gspec(GPL, CW, CW), _gspec(GPL, ns, CW), pl.BlockSpec((GPL, ns), lambda i: (i, 0))],
        out_shape=[_sds((g, CW, CW)), _sds((g, ns, CW)), _sds((g, ns))],
        compiler_params=_cparams(1, VMEM_BIG),
    )(u_all, dy_all, *ds4, *h4)


def _merge_fwd(y_ssm, proj, lin, x, tgt, glu_wg, out_w, d_skip, glu_b, pscale, gate, fin_g, tm):
    lx, d = x.shape
    sw = y_ssm.shape[1]
    mix = out_w.shape[0]
    pw = mix - sw
    cb2 = glu_wg.shape[2]
    nh = NDEV // 2

    def body(y_ref, u_ref, z_ref, l_ref, x_ref, t_ref, gw_ref, ow_ref, dk_ref, gb_ref, ps_ref, gt_ref, fg_ref,
             ypre_ref, y12_ref, br_ref, dxo_ref, loss_ref, dfg_ref, dgt_ref):
        @pl.when(pl.program_id(0) == 0)
        def _():
            loss_ref[...] = jnp.zeros_like(loss_ref)
            dfg_ref[...] = jnp.zeros_like(dfg_ref)
            dgt_ref[...] = jnp.zeros_like(dgt_ref)

        ypre = y_ref[...] + dk_ref[...] * u_ref[...]
        ypre_ref[...] = ypre
        yg = _gelu(ypre).astype(BF16)
        outs = []
        for j in range(nh):
            y1 = _mm(yg, gw_ref[j]) + gb_ref[:, j * cb2 : (j + 1) * cb2]
            y2 = _mm(yg, gw_ref[nh + j]) + gb_ref[:, (nh + j) * cb2 : (nh + j + 1) * cb2]
            y12_ref[:, j * cb2 : (j + 1) * cb2] = y1
            y12_ref[:, (nh + j) * cb2 : (nh + j + 1) * cb2] = y2
            outs.append(y1 * _sigmoid(y2))
        lin = l_ref[...] * ps_ref[...]
        brb = (jnp.concatenate([lin] + outs, axis=1) * _silu(z_ref[...])).astype(BF16)
        br_ref[...] = brb
        mixv = _mm(brb, ow_ref[...])
        xo = x_ref[...] + gt_ref[...] * mixv
        r2 = lax.rsqrt(_rowmean(xo * xo) + EPS)
        xh = xo * r2
        fg = fg_ref[...]
        e = xh * fg - t_ref[...]
        loss_ref[...] += 0.5 * _colsum(_rowmean(e * e))
        dy = e * (1.0 / d)
        dfg_ref[...] += _colsum(dy * xh)
        gy = dy * fg
        dxo = r2 * (gy - xh * _rowmean(gy * xh))
        dxo_ref[...] = dxo
        dgt_ref[...] += _colsum(dxo * mixv)

    vec = _whole((1, d))
    return pl.pallas_call(
        body,
        name="merge_fwd",
        grid=(lx // tm,),
        in_specs=[_rows(tm, sw), _rows(tm, sw, 1), _rows(tm, mix, 1), _rows(tm, pw), _rows(tm, d), _rows(tm, d), _whole(glu_wg.shape), _whole(out_w.shape), _whole((1, sw)), _whole((1, 2 * sw)), _whole((1, pw)), vec, vec],
        out_specs=[_rows(tm, sw), _rows(tm, 2 * sw), _rows(tm, mix), _rows(tm, d), _acc((1, 1)), _acc((1, d)), _acc((1, d))],
        out_shape=[_sds((lx, sw)), _sds((lx, 2 * sw)), _sds((lx, mix), BF16), _sds((lx, d)), _sds((1, 1)), _sds((1, d)), _sds((1, d))],
        compiler_params=_cparams(1, VMEM_BIG),
    )(y_ssm, proj, proj, lin, x, tgt, glu_wg, out_w, d_skip, glu_b, pscale, gate, fin_g)


def _out_bwd(dxo, gate, branch, out_w, lin, y12, proj, pscale, tm):
    lx, d = dxo.shape
    mix = out_w.shape[0]
    pw = lin.shape[1]
    sw = mix - pw
    nb = lx // tm

    def body(dxo_ref, gt_ref, br_ref, ow_ref, l_ref, y_ref, z_ref, ps_ref, dz_ref, dl_ref, dg_ref, dps_ref, dow_hbm, acc, sem):
        i = pl.program_id(0)

        @pl.when(i == 0)
        def _():
            acc[...] = jnp.zeros_like(acc)
            dps_ref[...] = jnp.zeros_like(dps_ref)

        dmix = (dxo_ref[...] * gt_ref[...]).astype(BF16)
        dbr = _mm_nt(dmix, ow_ref[...])
        acc[...] += _mm_tn(br_ref[...], dmix)
        z = z_ref[...]
        sg = _sigmoid(z)
        dbp = dbr * (z * sg)
        y1, y2 = y_ref[:, :sw], y_ref[:, sw:]
        s2 = _sigmoid(y2)
        ps = ps_ref[...]
        lin = l_ref[...]
        bp = jnp.concatenate([lin * ps, y1 * s2], axis=1)
        dz_ref[...] = dbr * bp * (sg * (1.0 + z * (1.0 - sg)))
        dlp = dbp[:, :pw]
        dl_ref[...] = dlp * ps
        dps_ref[...] += _colsum(dlp * lin)
        dss = dbp[:, pw:]
        dg_ref[...] = jnp.concatenate([dss * s2, dss * y1 * s2 * (1.0 - s2)], axis=1).astype(BF16)

        @pl.when(i == nb - 1)
        def _():
            cp = pltpu.make_async_copy(acc, dow_hbm, sem)
            cp.start()
            cp.wait()

    return pl.pallas_call(
        body,
        name="out_bwd",
        grid=(nb,),
        in_specs=[_rows(tm, d), _whole((1, d)), _rows(tm, mix), _whole(out_w.shape), _rows(tm, pw), _rows(tm, 2 * sw), _rows(tm, mix, 1), _whole((1, pw))],
        out_specs=[_rows(tm, mix), _rows(tm, pw), _rows(tm, 2 * sw), _acc((1, pw)), _ANY],
        out_shape=[_sds((lx, mix)), _sds((lx, pw)), _sds((lx, 2 * sw), BF16), _sds((1, pw)), _sds((mix, d))],
        scratch_shapes=[pltpu.VMEM((mix, d), F32), pltpu.SemaphoreType.DMA],
        compiler_params=_cparams(1, VMEM_BIG),
    )(dxo, gate, branch, out_w, lin, y12, proj, pscale)


def _glu_bwd(dg12, ypre, proj, glu_wg, d_skip, tm):
    lx, sw = ypre.shape
    cb2 = glu_wg.shape[2]
    nb = lx // tm

    def body(dg_ref, yp_ref, u_ref, gw_ref, dk_ref, dyp_ref, dgb_ref, dd_ref, dgw_hbm, acc, sem):
        i = pl.program_id(0)

        @pl.when(i == 0)
        def _():
            acc[...] = jnp.zeros_like(acc)
            dgb_ref[...] = jnp.zeros_like(dgb_ref)
            dd_ref[...] = jnp.zeros_like(dd_ref)

        ypre = yp_ref[...]
        ygb = _gelu(ypre).astype(BF16)
        dg = dg_ref[...]
        dyg = jnp.zeros((tm, sw), F32)
        for j in range(NDEV):
            dgj = dg[:, j * cb2 : (j + 1) * cb2]
            dyg = dyg + _mm_nt(dgj, gw_ref[j])
            acc[j] += _mm_tn(ygb, dgj)
        dgb_ref[...] += _colsum(dg.astype(F32))
        dyp = dyg * _gelu_grad(ypre)
        dyp_ref[...] = dyp
        dd_ref[...] += _colsum(dyp * u_ref[...])

        @pl.when(i == nb - 1)
        def _():
            cp = pltpu.make_async_copy(acc, dgw_hbm, sem)
            cp.start()
            cp.wait()

    return pl.pallas_call(
        body,
        name="glu_bwd",
        grid=(nb,),
        in_specs=[_rows(tm, 2 * sw), _rows(tm, sw), _rows(tm, sw, 1), _whole(glu_wg.shape), _whole((1, sw))],
        out_specs=[_rows(tm, sw), _acc((1, 2 * sw)), _acc((1, sw)), _ANY],
        out_shape=[_sds((lx, sw)), _sds((1, 2 * sw)), _sds((1, sw)), _sds(glu_wg.shape)],
        scratch_shapes=[pltpu.VMEM(glu_wg.shape, F32), pltpu.SemaphoreType.DMA],
        compiler_params=_cparams(1, VMEM_BIG),
    )(dg12, ypre, proj, glu_wg, d_skip)


def _pack(parts, total):
    flat = jnp.concatenate([p.reshape(-1).astype(F32) for p in parts])
    return jnp.pad(flat, (0, total - flat.shape[0]))


def _unpack(flat, shapes):
    out, off = [], 0
    for s in shapes:
        n = int(np.prod(s)) if len(s) else 1
        out.append(flat[off : off + n].reshape(s))
        off += n
    return out


def kernel(x, c, ctx, c_ctx, ada_w, ada_b, norm_g, in_w, pool_w, pool_scale, s5_lam_re, s5_lam_im, s5_log_dt, s5_b_re, s5_b_im, s5_c_re, s5_c_im, s5_d, glu_w, glu_b, out_w, final_g, loss_target, m_c_ctx, m_ada_w, m_ada_b, m_norm_g, m_in_w, m_pool_w, m_pool_scale, m_s5_lam_re, m_s5_lam_im, m_s5_log_dt, m_s5_b_re, m_s5_b_im, m_s5_c_re, m_s5_c_im, m_s5_d, m_glu_w, m_glu_b, m_out_w, m_final_g, v_c_ctx, v_ada_w, v_ada_b, v_norm_g, v_in_w, v_pool_w, v_pool_scale, v_s5_lam_re, v_s5_lam_im, v_s5_log_dt, v_s5_b_re, v_s5_b_im, v_s5_c_re, v_s5_c_im, v_s5_d, v_glu_w, v_glu_b, v_out_w, v_final_g):
    xr, tgt, xc = x[0], loss_target[0], ctx[0]
    lx, d = xr.shape
    lc = xc.shape[0]
    mix = out_w.shape[1] * NDEV
    sw = glu_w.shape[1]
    pw = mix - sw
    pgw = pw // len(POOL_WINDOWS)
    ngrp = sw // SSM_H
    cb = in_w.shape[2]
    cb2 = glu_w.shape[2]
    nmod = ada_w.shape[2]
    ncc, ncl = lc // CHUNK_T, lx // CHUNK_T
    nch = ncc + ncl
    tm = min(256, lx)
    tm_acc = min(512, lx)
    assert mix == d and pw == sw and 2 * cb == pw and NDEV * cb2 == 2 * sw and lx % GRID_W == 0
    mx, my, mc = _my_pos()
    me = 4 * mx + 2 * my + mc

    (c_all,) = _all_gather([c], "gather_c")
    call = jnp.concatenate([c_all.reshape(NDEV, d), c_ctx.reshape(1, d), jnp.zeros((NDEV - 1, d), F32)], axis=0)
    ada_w_l = ada_w[0]
    ada_b_l = lax.dynamic_slice_in_dim(ada_b, me * nmod, nmod, axis=1)
    m_loc = _mod_fwd(call, ada_w_l, ada_b_l)
    (m_all,) = _all_gather([m_loc], "gather_mod")
    mod = lax.dynamic_index_in_dim(m_all, me, axis=1, keepdims=False).reshape(1, NDEV * nmod)
    mod_c = m_all[:, NDEV, :].reshape(1, NDEV * nmod)
    shift, scale, gate = mod[:, :d], mod[:, d : 2 * d], mod[:, 2 * d :]
    shift_c, scale_c = mod_c[:, :d], mod_c[:, d : 2 * d]

    s5_params = tuple(p[0] for p in (s5_lam_re, s5_lam_im, s5_log_dt, s5_b_re, s5_b_im, s5_c_re, s5_c_im))
    (small, a16), small_vjp = jax.vjp(_s5_small, *s5_params)
    gb_ops = min(8, ngrp)
    (tsum, wst, ccat), (in_wg,) = _s5_build(small, gb_ops, [in_w[0].astype(BF16)])

    ssm_blocks = (2, 3)
    (proj, hb), (glu_wg, out_wg, pool_wg) = _in_proj(
        xr, norm_g, scale, shift, in_wg, tuple(range(NDEV)), tm, "in_proj",
        shards=[glu_w[0].astype(BF16), out_w[0].astype(BF16), pool_w[0].reshape(-1, pgw).astype(BF16)],
    )
    out_wf = out_wg.reshape(mix, d)
    pool_wf = pool_wg.reshape(NDEV, len(POOL_WINDOWS), pgw // NDEV, pgw).transpose(1, 0, 2, 3).reshape(len(POOL_WINDOWS), pgw, pgw)
    (uc, hcb), _ = _in_proj(xc, norm_g, scale_c, shift_c, in_wg, ssm_blocks, min(tm, lc), "in_proj_ctx")
    lin, dmat = _pool_fwd(proj, pool_wf)

    gb_scan = min(16, ngrp)
    half = wst.shape[2] // 2
    perm = _chunk_perm()
    flat = lambda a: a.reshape(ngrp * nch, half)
    pairs = lambda arrs: [a.reshape(ngrp // 2, nch, half) for a in arrs]
    u_all, s_re, s_im = _s5_state(proj, pw // LANES, uc, perm, wst, ncc, ncl)
    h4 = pairs(_s5_scan_fwd(flat(s_re), flat(s_im), a16, ncc, nch, gb_scan))
    y_ssm = _s5_out(u_all, h4, tsum, ccat, perm, ncc, lx, gb_scan)

    ypre, y12, branch, dxo, loss_l, dfg, dgate = _merge_fwd(
        y_ssm, proj, lin, xr, tgt, glu_wg, out_wf, s5_d, glu_b, pool_scale, gate, final_g.reshape(1, d), tm
    )

    dz, dlin, dg12, dps, d_out_w = _out_bwd(dxo, gate, branch, out_wf, lin, y12, proj, pool_scale, tm)
    dypre, dglu_b, dd_skip, d_glu_w = _glu_bwd(dg12, ypre, proj, glu_wg, s5_d, tm_acc)

    d_glu_out = [d_glu_w, d_out_w.reshape(NDEV, mix // NDEV, d)]
    (dy_all, dh_re, dh_im), got_glu_out = _s5_dstate(dypre, perm, ccat, ncc, swaps=[_by_chip_and_core(a) for a in d_glu_out])
    ds4 = pairs(_s5_scan_bwd(flat(dh_re), flat(dh_im), a16, ncc, nch, gb_scan))
    du5, duc, dwst = _s5_du(u_all, dy_all, ds4, tsum, wst, perm, ncc, lx, lc, gb_scan)
    dtsum, dccat, da16 = _s5_op_grads(u_all, dy_all, ds4, h4, ncc, gb_scan)
    dup, dpool_w = _pool_bwd(dlin, pool_wf, dmat)

    npw = len(POOL_WINDOWS)
    dpool_s = dpool_w.reshape(npw, NDEV, pgw // NDEV, pgw).transpose(1, 0, 2, 3).reshape(NDEV, npw * (pgw // NDEV), pgw)
    parts_a = _pair_partials(d_glu_out, [BF16] * 2, "grads_a", got=got_glu_out) + _pair_partials([dpool_s], [BF16], "grads_p")
    dpcb, dsc_c, dsh_c, dg_c = _in_bwd_ctx(xc, duc, in_wg, ssm_blocks, norm_g, scale_c, shift_c)
    grad_x, dpb, dsc, dsh, dnorm_g = _in_bwd(xr, dxo, dup, du5, dypre, dz, in_wg, norm_g, scale, shift, s5_d, dg_c, tm)
    d_in_w, (land_glu, land_out, land_pool) = _in_w_grad(hb, dpb, hcb, dpcb, cb, min(2 * tm_acc, lx), parts=parts_a)

    dmod = jnp.concatenate(
        [jnp.concatenate([dsh, dsc, dgate], axis=1), jnp.concatenate([dsh_c, dsc_c, jnp.zeros((1, d), F32)], axis=1)], axis=0
    )
    (dmod_all,) = _all_gather([dmod], "gather_dmod")
    dmod_l = lax.dynamic_slice_in_dim(dmod_all, me * nmod, nmod, axis=2)
    g_ada_w, g_ada_b_l, cctx_part = _mod_bwd(call, dmod_l[:, 0, :], dmod_l[:, 1, :], ada_w_l)

    parts_b = _pair_partials([d_in_w], [BF16], "grads_b")
    dsmall, (land_in,) = _s5_build_bwd(small, dtsum, dwst, dccat, gb_ops, parts_b)
    ds5 = small_vjp((tuple(dsmall), da16))

    big_names = ["s5_b_re", "s5_b_im", "s5_c_re", "s5_c_im"]
    small_names = ["norm_g", "pool_scale", "s5_lam_re", "s5_lam_im", "s5_log_dt", "s5_d", "glu_b", "final_g", "c_ctx"]
    small_w = dict(norm_g=norm_g, pool_scale=pool_scale, s5_lam_re=s5_lam_re, s5_lam_im=s5_lam_im, s5_log_dt=s5_log_dt, s5_b_re=s5_b_re, s5_b_im=s5_b_im, s5_c_re=s5_c_re, s5_c_im=s5_c_im, s5_d=s5_d, glu_b=glu_b, final_g=final_g, c_ctx=c_ctx)
    small_m = dict(norm_g=m_norm_g, pool_scale=m_pool_scale, s5_lam_re=m_s5_lam_re, s5_lam_im=m_s5_lam_im, s5_log_dt=m_s5_log_dt, s5_b_re=m_s5_b_re, s5_b_im=m_s5_b_im, s5_c_re=m_s5_c_re, s5_c_im=m_s5_c_im, s5_d=m_s5_d, glu_b=m_glu_b, final_g=m_final_g, c_ctx=m_c_ctx)
    small_v = dict(norm_g=v_norm_g, pool_scale=v_pool_scale, s5_lam_re=v_s5_lam_re, s5_lam_im=v_s5_lam_im, s5_log_dt=v_s5_log_dt, s5_b_re=v_s5_b_re, s5_b_im=v_s5_b_im, s5_c_re=v_s5_c_re, s5_c_im=v_s5_c_im, s5_d=v_s5_d, glu_b=v_glu_b, final_g=v_final_g, c_ctx=v_c_ctx)
    small_g = dict(norm_g=dnorm_g, pool_scale=dps, s5_lam_re=ds5[0], s5_lam_im=ds5[1], s5_log_dt=ds5[2], s5_b_re=ds5[3], s5_b_im=ds5[4], s5_c_re=ds5[5], s5_c_im=ds5[6], s5_d=dd_skip, glu_b=dglu_b, final_g=dfg, c_ctx=cctx_part)
    shapes = [small_w[k].shape for k in small_names]
    nsmall = sum(int(np.prod(s)) for s in shapes) + 1
    unit = NDEV * 8 * LANES
    tot_s = -(-nsmall // unit) * unit
    big_n = [int(np.prod(small_w[k].shape)) for k in big_names]
    assert all(n % unit == 0 for n in big_n)
    tot = tot_s + sum(big_n)
    rows = tot // LANES
    gpack = jnp.concatenate(
        [_pack([small_g[k] for k in small_names] + [loss_l], tot_s)] + [small_g[k].reshape(-1) for k in big_names]
    ).reshape(NDEV, rows // NDEV, LANES)
    (land_small,) = _reduce_scatter([gpack], [F32], "small")
    gsum = _sum_slots(land_small, "sum_small")
    gall, g_ada_b_all = _all_gather([gsum, g_ada_b_l], "gather_small")
    gflat = gall.reshape(tot)
    loss = gflat[nsmall - 1]

    two_d = lambda a: a.reshape(-1, a.shape[-1])
    snames = small_names + ["ada_b"]
    sw = [small_w[k] for k in small_names] + [ada_b]
    sg = _unpack(gflat, shapes) + [g_ada_b_all.reshape(ada_b.shape)]
    sd, sm, sv = _adam_many(
        [two_d(a) for a in sg],
        [two_d(a) for a in sw],
        [two_d(small_m[k]) for k in small_names] + [two_d(m_ada_b)],
        [two_d(small_v[k]) for k in small_names] + [two_d(v_ada_b)],
        "adam_small",
    )
    res = {key: {k: a.reshape(w.shape) for k, a, w in zip(snames, vals, sw)} for key, vals in (("g", sg), ("d", sd), ("m", sm), ("v", sv))}

    def shard(name, g, land, w, m, v):
        shp = w.shape
        w2, m2_, v2_ = (a.reshape(-1, shp[-1]) for a in (w, m, v))
        out = _adam(g, w2, m2_, v2_, "adam_" + name, land=land)
        for k, a in zip(("g", "d", "m", "v"), out):
            res[k][name] = a.reshape(shp)

    shard("ada_w", g_ada_w, None, ada_w, m_ada_w, v_ada_w)
    shard("in_w", None, land_in, in_w, m_in_w, v_in_w)
    shard("pool_w", None, land_pool, pool_w, m_pool_w, v_pool_w)
    shard("glu_w", None, land_glu, glu_w, m_glu_w, v_glu_w)
    shard("out_w", None, land_out, out_w, m_out_w, v_out_w)
    off = tot_s
    for k, n in zip(big_names, big_n):
        shard(k, gflat[off : off + n].reshape(-1, small_w[k].shape[-1]), None, small_w[k], small_m[k], small_v[k])
        off += n

    names = ["c_ctx", "ada_w", "ada_b", "norm_g", "in_w", "pool_w", "pool_scale", "s5_lam_re", "s5_lam_im", "s5_log_dt", "s5_b_re", "s5_b_im", "s5_c_re", "s5_c_im", "s5_d", "glu_w", "glu_b", "out_w", "final_g"]
    return (loss, grad_x[None], *[res["g"][n] for n in names], *[res["d"][n] for n in names], *[res["m"][n] for n in names], *[res["v"][n] for n in names])
```

```python
import numpy as np

import jax
import jax.numpy as jnp
from jax import lax
from jax.experimental import pallas as pl
from jax.experimental.pallas import tpu as pltpu

F32 = jnp.float32
BF16 = jnp.bfloat16
NDEV = 8
EPS = 1e-6
GRID_W = 64
POOL_WINDOWS = (2, 4, 8, 16)
SSM_H = 16
CHUNK_T = 16
LANES = 128
ADAM_LR, ADAM_B1, ADAM_B2, ADAM_EPS, ADAM_WD, ADAM_STEP = 0.001, 0.9, 0.999, 1e-08, 0.01, 10
VMEM_BIG = 56 << 20
MESH_ID = pl.DeviceIdType.MESH

_HBM = pl.BlockSpec(memory_space=pltpu.HBM)
_ANY = pl.BlockSpec(memory_space=pl.ANY)
_SMEM = pl.BlockSpec(memory_space=pltpu.SMEM)


def _sds(shape, dtype=F32):
    return jax.ShapeDtypeStruct(tuple(shape), dtype)


def _cparams(ngrid=0, vmem=None):
    return pltpu.CompilerParams(
        dimension_semantics=("arbitrary",) * ngrid if ngrid else None, vmem_limit_bytes=vmem
    )


def _rows(tm, c, col=0):
    return pl.BlockSpec((tm, c), lambda i: (i, col))


def _whole(shape):
    nd = len(shape)
    return pl.BlockSpec(tuple(shape), lambda *_: (0,) * nd, pipeline_mode=pl.Buffered(1))


def _acc(shape):
    nd = len(shape)
    return pl.BlockSpec(tuple(shape), lambda *_: (0,) * nd)


def _mm(a, b):
    return jnp.dot(a, b, preferred_element_type=F32)


def _mm_nt(a, b):
    return lax.dot_general(a, b, (((1,), (1,)), ((), ())), preferred_element_type=F32)


def _mm_tn(a, b):
    return lax.dot_general(a, b, (((0,), (0,)), ((), ())), preferred_element_type=F32)


def _mm_split(k01, s):
    hi = s.astype(BF16)
    lo = (s - hi.astype(F32)).astype(BF16)
    return _mm(k01, hi) + _mm(k01, lo)


def _sigmoid(v):
    return 0.5 * (jnp.tanh(0.5 * v) + 1.0)


def _silu(v):
    return v * _sigmoid(v)


_GELU_K = 0.7978845608028654
_GELU_C = 0.044715


def _gelu(v):
    return 0.5 * v * (1.0 + jnp.tanh(_GELU_K * (v + _GELU_C * v * v * v)))


def _gelu_grad(v):
    th = jnp.tanh(_GELU_K * (v + _GELU_C * v * v * v))
    return 0.5 * (1.0 + th) + 0.5 * v * (1.0 - th * th) * (_GELU_K * (1.0 + 3.0 * _GELU_C * v * v))


def _colsum(v):
    return jnp.sum(v, axis=0, keepdims=True)


def _rowmean(v):
    return jnp.mean(v, axis=-1, keepdims=True)


NCHIP = NDEV // 2


def _my_pos():
    return lax.axis_index("x"), lax.axis_index("y"), lax.axis_index("c")


def _other_chips():
    x, y, _ = _my_pos()
    return [(1 - x, y), (x, 1 - y), (1 - x, 1 - y)]


def _remote(src, dst, send, recv, to):
    return pltpu.make_async_remote_copy(src, dst, send, recv, device_id=to, device_id_type=MESH_ID)


def _all_gather(arrs, name):
    n = len(arrs)

    def body(*refs):
        start, finish = _gather_copies(refs[:n], refs[n : 2 * n], *refs[2 * n :])
        start()
        finish()

    return pl.pallas_call(
        body, name=name, out_shape=_gather_shapes(arrs), in_specs=[_HBM] * n, out_specs=[_HBM] * n, scratch_shapes=_gather_sems(n)
    )(*arrs)


def _gather_shapes(arrs):
    return [_sds((NDEV,) + a.shape, a.dtype) for a in arrs]


def _gather_sems(n):
    return [pltpu.SemaphoreType.DMA((n, NDEV - 1)), pltpu.SemaphoreType.DMA((n, NDEV - 1)), pltpu.SemaphoreType.DMA((n,))]


def _gather_copies(ins, outs, send, recv, loc):
    n = len(ins)
    x, y, c = _my_pos()
    me, sib = (x, y, c), (x, y, 1 - c)
    chips = _other_chips()

    def slot(a, p):
        return outs[a].at[4 * p[0] + 2 * p[1] + p[2]]

    def copy(a, k, block, to, own=False):
        return _remote(ins[a] if own else slot(a, block), slot(a, block), send.at[a, k], recv.at[a, k], to)

    def mine():
        local = [pltpu.make_async_copy(ins[a], slot(a, me), loc.at[a]) for a in range(n)]
        sent = []
        for a in range(n):
            sent.append(copy(a, 0, me, sib, own=True))
            sent += [copy(a, 1 + j, me, (*chip, c), own=True) for j, chip in enumerate(chips)]
        return local, sent

    def start():
        local, sent = mine()
        for cp in local + sent:
            cp.start()

    def finish():
        local, sent = mine()
        for j, chip in enumerate(chips):
            for a in range(n):
                copy(a, 1 + j, (*chip, c), me).wait_recv()
                fwd = copy(a, 4 + j, (*chip, c), sib)
                fwd.start()
                sent.append(fwd)
        for a in range(n):
            copy(a, 0, sib, me).wait_recv()
            for j, chip in enumerate(chips):
                copy(a, 4 + j, (*chip, 1 - c), me).wait_recv()
        for cp in sent:
            cp.wait_send()
        for cp in local:
            cp.wait()

    return start, finish


def _sibling_swap(arrs, name):
    n = len(arrs)

    def body(*refs):
        start, finish = _swap_copies(refs[:n], refs[n : 2 * n], *refs[2 * n :])
        start()
        finish()

    return pl.pallas_call(
        body, name=name, out_shape=_swap_shapes(arrs), in_specs=[_HBM] * n, out_specs=[_HBM] * n, scratch_shapes=_swap_sems(n)
    )(*arrs)


def _swap_shapes(arrs):
    return [_sds((a.shape[0],) + a.shape[2:], a.dtype) for a in arrs]


def _swap_sems(n):
    return [pltpu.SemaphoreType.DMA((n,)), pltpu.SemaphoreType.DMA((n,))]


def _swap_copies(ins, outs, send, recv):
    x, y, c = _my_pos()

    def copies():
        return [_remote(ins[a].at[:, 1 - c], outs[a], send.at[a], recv.at[a], (x, y, 1 - c)) for a in range(len(ins))]

    def start():
        for cp in copies():
            cp.start()

    def finish():
        for cp in copies():
            cp.wait()

    return start, finish


def _pair_sum(arr, got, wire, name):
    _, _, r, c = arr.shape
    tr = _row_tile(r, c, budget=1 << 20)

    def body(a_ref, g_ref, o_ref):
        o_ref[0] = (a_ref[0, lax.axis_index("c")].astype(F32) + g_ref[0].astype(F32)).astype(wire)

    return pl.pallas_call(
        body,
        name=name,
        grid=(NCHIP, r // tr),
        in_specs=[pl.BlockSpec((1, 2, tr, c), lambda q, i: (q, 0, i, 0)), pl.BlockSpec((1, tr, c), lambda q, i: (q, i, 0))],
        out_specs=pl.BlockSpec((1, tr, c), lambda q, i: (q, i, 0)),
        out_shape=_sds((NCHIP, r, c), wire),
        compiler_params=_cparams(2),
    )(arr, got)


def _chip_exchange(arrs, name):
    n = len(arrs)

    def body(*refs):
        start, finish = _exchange_copies(refs[:n], refs[n : 2 * n], *refs[2 * n :])
        start()
        finish()

    return pl.pallas_call(
        body, name=name, out_shape=[_sds(a.shape, a.dtype) for a in arrs], in_specs=[_HBM] * n, out_specs=[_HBM] * n, scratch_shapes=_exchange_sems(n)
    )(*arrs)


def _exchange_sems(n):
    return [pltpu.SemaphoreType.DMA((n, NCHIP - 1)), pltpu.SemaphoreType.DMA((n, NCHIP - 1)), pltpu.SemaphoreType.DMA((n,))]


def _exchange_copies(ins, outs, send, recv, loc):
    n = len(ins)
    x, y, c = _my_pos()
    mine = 2 * x + y
    chips = _other_chips()

    def copies():
        local = [pltpu.make_async_copy(ins[a].at[mine], outs[a].at[mine], loc.at[a]) for a in range(n)]
        sent = [
            _remote(ins[a].at[2 * px + py], outs[a].at[mine], send.at[a, j], recv.at[a, j], (px, py, c))
            for a in range(n)
            for j, (px, py) in enumerate(chips)
        ]
        return local, sent

    def start():
        local, sent = copies()
        for cp in local + sent:
            cp.start()

    def finish():
        local, sent = copies()
        for a in range(n):
            for j, (px, py) in enumerate(chips):
                _remote(ins[a].at[mine], outs[a].at[2 * px + py], send.at[a, j], recv.at[a, j], (px, py, c)).wait_recv()
        for cp in sent:
            cp.wait_send()
        for cp in local:
            cp.wait()

    return start, finish


def _by_chip_and_core(a):
    return a.reshape((NCHIP, 2) + a.shape[1:])


def _pair_partials(arrs, wires, tag, got=None):
    four = [_by_chip_and_core(a) for a in arrs]
    if got is None:
        got = _sibling_swap(four, "swap_" + tag)
    return [_pair_sum(a, g, w, f"pair_sum_{tag}{i}") for i, (a, g, w) in enumerate(zip(four, got, wires))]


def _reduce_scatter(arrs, wires, tag):
    return _chip_exchange(_pair_partials(arrs, wires, tag), "exchange_" + tag)


def _row_tile(r, c, budget=1 << 20):
    width = max(c, LANES) * 4
    best = r
    for t in range(8, r, 8):
        if r % t == 0 and t * width <= budget:
            best = t
    if r * width <= budget:
        best = r
    return best


def _sum_chips(land):
    tot = land[0].astype(F32)
    for q in range(1, NCHIP):
        tot = tot + land[q].astype(F32)
    return tot


def _adam_math(w, g, m, v):
    m2 = ADAM_B1 * m + (1.0 - ADAM_B1) * g
    v2 = ADAM_B2 * v + (1.0 - ADAM_B2) * (g * g)
    mh = m2 / (1.0 - ADAM_B1**ADAM_STEP)
    vh = v2 / (1.0 - ADAM_B2**ADAM_STEP)
    delta = -ADAM_LR * (mh / (jnp.sqrt(vh) + ADAM_EPS) + ADAM_WD * w)
    return delta, m2, v2


def _adam_many(gs, ws, ms, vs, name):
    n = len(gs)

    def body(*refs):
        for t in range(n):
            d, m2, v2 = _adam_math(refs[n + t][...], refs[t][...], refs[2 * n + t][...], refs[3 * n + t][...])
            refs[4 * n + t][...] = d
            refs[5 * n + t][...] = m2
            refs[6 * n + t][...] = v2

    out = pl.pallas_call(body, name=name, out_shape=[_sds(w.shape) for w in ws] * 3, compiler_params=_cparams(0, 32 << 20))(*gs, *ws, *ms, *vs)
    return out[:n], out[n : 2 * n], out[2 * n :]


def _adam(g, w, m, v, name, land=None):
    r, c = w.shape
    tr = _row_tile(r, c, budget=1 << 20)

    def body(*refs):
        if land is not None:
            l_ref, w_ref, m_ref, v_ref, g_ref, d_ref, m2_ref, v2_ref = refs
            gv = _sum_chips(l_ref)
            g_ref[...] = gv
        else:
            g_in, w_ref, m_ref, v_ref, d_ref, m2_ref, v2_ref = refs
            gv = g_in[...]
        d, m2, v2 = _adam_math(w_ref[...], gv, m_ref[...], v_ref[...])
        d_ref[...] = d
        m2_ref[...] = m2
        v2_ref[...] = v2

    blk = _rows(tr, c)
    if land is not None:
        in_specs = [pl.BlockSpec((NCHIP, tr, c), lambda i: (0, i, 0)), blk, blk, blk]
        out = pl.pallas_call(
            body, name=name, grid=(r // tr,), in_specs=in_specs, out_specs=[blk] * 4, out_shape=[_sds((r, c))] * 4, compiler_params=_cparams(1)
        )(land, w, m, v)
        return out
    out = pl.pallas_call(
        body, name=name, grid=(r // tr,), in_specs=[blk] * 4, out_specs=[blk] * 3, out_shape=[_sds((r, c))] * 3, compiler_params=_cparams(1)
    )(g, w, m, v)
    return (g,) + tuple(out)


def _sum_slots(land, name):
    _, r, c = land.shape
    tr = _row_tile(r, c, budget=1 << 20)

    def body(l_ref, o_ref):
        o_ref[...] = _sum_chips(l_ref)

    return pl.pallas_call(
        body,
        name=name,
        grid=(r // tr,),
        in_specs=[pl.BlockSpec((NCHIP, tr, c), lambda i: (0, i, 0))],
        out_specs=_rows(tr, c),
        out_shape=_sds((r, c)),
        compiler_params=_cparams(1),
    )(land)


def _mod_fwd(call, ada_w, ada_b_loc):
    def body(c_ref, w_ref, b_ref, o_ref):
        s = _silu(c_ref[...]).astype(BF16)
        o_ref[...] = _mm(s, w_ref[...].astype(BF16)) + b_ref[...]

    return pl.pallas_call(
        body, name="mod_fwd", out_shape=_sds((call.shape[0], ada_w.shape[1])), compiler_params=_cparams(0, 32 << 20)
    )(call, ada_w, ada_b_loc)


def _mod_bwd(call, dm_loc, dmc_loc, ada_w):
    d, n = ada_w.shape
    pad = call.shape[0] - NDEV - 1

    def body(c_ref, dm_ref, dmc_ref, w_ref, gw_ref, gb_ref, cp_ref):
        cv = c_ref[...]
        sg = _sigmoid(cv)
        dmc = _colsum(dmc_ref[...])
        dm = dm_ref[...]
        rows = jnp.concatenate([dm, dmc, jnp.zeros((pad, n), F32)], axis=0)
        gw_ref[...] = _mm_tn((cv * sg).astype(BF16), rows.astype(BF16))
        gb_ref[...] = _colsum(dm) + dmc
        back = _mm_nt(rows[NDEV:].astype(BF16), w_ref[...].astype(BF16))
        c8, s8 = cv[NDEV : NDEV + 1], sg[NDEV : NDEV + 1]
        cp_ref[...] = back[0:1] * (s8 * (1.0 + c8 * (1.0 - s8)))

    return pl.pallas_call(
        body, name="mod_bwd", out_shape=[_sds((d, n)), _sds((1, n)), _sds((1, d))], compiler_params=_cparams(0, 40 << 20)
    )(call, dm_loc, dmc_loc, ada_w)


def _riding(body, nin, nout, nr, grid, copies, nscratch=0):
    def wrapped(*refs):
        base = nin + 2 * nr + nout
        start, finish = copies(refs[nin : nin + nr], refs[nin + nr + nout : base], *refs[base + nscratch :])
        first, last = True, True
        for ax, size in enumerate(grid):
            first = jnp.logical_and(first, pl.program_id(ax) == 0)
            last = jnp.logical_and(last, pl.program_id(ax) == size - 1)
        pl.when(first)(start)
        body(*refs[:nin], *refs[nin + nr : nin + nr + nout], *refs[base : base + nscratch])
        pl.when(last)(finish)

    return wrapped if nr else body


def _in_proj(xr, norm_g, scale, shift, wg, jsel, tm, name, shards=()):
    lx, d = xr.shape
    cb = wg.shape[2]
    nr = len(shards)

    def body(x_ref, g_ref, sc_ref, sh_ref, w_ref, p_ref, h_ref):
        xv = x_ref[...]
        r = lax.rsqrt(_rowmean(xv * xv) + EPS)
        hb = ((xv * r) * g_ref[...] * (1.0 + sc_ref[...]) + sh_ref[...]).astype(BF16)
        h_ref[...] = hb
        for q, j in enumerate(jsel):
            p_ref[:, q * cb : (q + 1) * cb] = _mm(hb, w_ref[j])

    vec = _whole((1, d))
    out = pl.pallas_call(
        _riding(body, 5, 2, nr, (lx // tm,), _gather_copies),
        name=name,
        grid=(lx // tm,),
        in_specs=[_rows(tm, d), vec, vec, vec, _whole(wg.shape)] + [_HBM] * nr,
        out_specs=[_rows(tm, len(jsel) * cb), _rows(tm, d)] + [_HBM] * nr,
        out_shape=[_sds((lx, len(jsel) * cb)), _sds((lx, d), BF16)] + _gather_shapes(shards),
        scratch_shapes=_gather_sems(nr) if nr else [],
        compiler_params=_cparams(1, VMEM_BIG),
    )(xr, norm_g, scale, shift, wg, *shards)
    return out[:2], out[2:]


def _in_bwd(xr, dxo, dup, du5, dypre, dz, wg, norm_g, scale, shift, d_skip, dg_init, tm):
    lx, d = xr.shape
    cb = wg.shape[2]
    pw = dup.shape[1]
    sw = du5.shape[1]
    mix = dz.shape[1]
    ncol = NDEV * cb

    def body(x_ref, dxo_ref, up_ref, d5_ref, dy_ref, dz_ref, w_ref, g_ref, sc_ref, sh_ref, dk_ref, gi_ref,
             gx_ref, dp_ref, dsc_ref, dsh_ref, dg_ref):
        i = pl.program_id(0)

        @pl.when(i == 0)
        def _():
            dsc_ref[...] = jnp.zeros_like(dsc_ref)
            dsh_ref[...] = jnp.zeros_like(dsh_ref)
            dg_ref[...] = gi_ref[...]

        dp = jnp.concatenate(
            [up_ref[...], d5_ref[...] + dk_ref[...] * dy_ref[...], dz_ref[...]], axis=1
        ).astype(BF16)
        dp_ref[...] = dp
        dh = _mm_nt(dp[:, 0:cb], w_ref[0])
        for j in range(1, NDEV):
            dh = dh + _mm_nt(dp[:, j * cb : (j + 1) * cb], w_ref[j])
        xv = x_ref[...]
        r = lax.rsqrt(_rowmean(xv * xv) + EPS)
        xh = xv * r
        g = g_ref[...]
        one_sc = 1.0 + sc_ref[...]
        dsh_ref[...] += _colsum(dh)
        dsc_ref[...] += _colsum(dh * (xh * g))
        dg_ref[...] += _colsum(dh * one_sc * xh)
        dxh = dh * one_sc * g
        gx_ref[...] = r * (dxh - xh * _rowmean(dxh * xh)) + dxo_ref[...]

    vec = _whole((1, d))
    return pl.pallas_call(
        body,
        name="in_bwd",
        grid=(lx // tm,),
        in_specs=[_rows(tm, d), _rows(tm, d), _rows(tm, pw), _rows(tm, sw), _rows(tm, sw), _rows(tm, mix), _whole(wg.shape), vec, vec, vec, _whole((1, sw)), vec],
        out_specs=[_rows(tm, d), _rows(tm, ncol), _acc((1, d)), _acc((1, d)), _acc((1, d))],
        out_shape=[_sds((lx, d)), _sds((lx, ncol), BF16), _sds((1, d)), _sds((1, d)), _sds((1, d))],
        compiler_params=_cparams(1, VMEM_BIG),
    )(xr, dxo, dup, du5, dypre, dz, wg, norm_g, scale, shift, d_skip, dg_init)


def _in_bwd_ctx(xc, duc, wg, jsel, norm_g, scale, shift):
    lc, d = xc.shape
    cb = wg.shape[2]

    def body(x_ref, du_ref, w_ref, g_ref, sc_ref, sh_ref, dp_ref, dsc_ref, dsh_ref, dg_ref):
        dp = du_ref[...].astype(BF16)
        dp_ref[...] = dp
        dh = _mm_nt(dp[:, 0:cb], w_ref[jsel[0]])
        for q in range(1, len(jsel)):
            dh = dh + _mm_nt(dp[:, q * cb : (q + 1) * cb], w_ref[jsel[q]])
        xv = x_ref[...]
        xh = xv * lax.rsqrt(_rowmean(xv * xv) + EPS)
        dsh_ref[...] = _colsum(dh)
        dsc_ref[...] = _colsum(dh * (xh * g_ref[...]))
        dg_ref[...] = _colsum(dh * (1.0 + sc_ref[...]) * xh)

    return pl.pallas_call(
        body,
        name="in_bwd_ctx",
        out_shape=[_sds(duc.shape, BF16), _sds((1, d)), _sds((1, d)), _sds((1, d))],
        compiler_params=_cparams(0, VMEM_BIG),
    )(xc, duc, wg, norm_g, scale, shift)


def _in_w_grad(hb, dpb, hcb, dpcb, cb, tm, parts=()):
    lx, d = hb.shape
    lc = hcb.shape[0]
    nb = lx // tm
    cg = 2 * cb
    nr = len(parts)
    grid = (NDEV // 2, nb + 1)

    def body(h_ref, dp_ref, hc_ref, dpc_ref, o_ref, acc):
        j, i = pl.program_id(0), pl.program_id(1)

        @pl.when(i == 0)
        def _():
            acc[...] = jnp.zeros_like(acc)

        @pl.when(i < nb)
        def _():
            h = h_ref[...]
            acc[0] += _mm_tn(h, dp_ref[:, 0:cb])
            acc[1] += _mm_tn(h, dp_ref[:, cb:cg])

        @pl.when(jnp.logical_and(i == nb, j == 1))
        def _():
            h = hc_ref[...]
            acc[0] += _mm_tn(h, dpc_ref[:, 0:cb])
            acc[1] += _mm_tn(h, dpc_ref[:, cb:cg])

        @pl.when(i == nb)
        def _():
            o_ref[...] = acc[...].astype(o_ref.dtype)

    out = pl.pallas_call(
        _riding(body, 4, 1, nr, grid, _exchange_copies, nscratch=1),
        name="in_w_grad",
        grid=grid,
        in_specs=[
            pl.BlockSpec((tm, d), lambda j, i: (jnp.minimum(i, nb - 1), 0)),
            pl.BlockSpec((tm, cg), lambda j, i: (jnp.minimum(i, nb - 1), j)),
            pl.BlockSpec((lc, d), lambda j, i: (0, 0)),
            pl.BlockSpec((lc, cg), lambda j, i: (0, 0)),
        ]
        + [_HBM] * nr,
        out_specs=[pl.BlockSpec((2, d, cb), lambda j, i: (j, 0, 0))] + [_HBM] * nr,
        out_shape=[_sds((NDEV, d, cb), BF16)] + [_sds(p.shape, p.dtype) for p in parts],
        scratch_shapes=[pltpu.VMEM((2, d, cb), F32)] + (_exchange_sems(nr) if nr else []),
        compiler_params=_cparams(2, VMEM_BIG),
    )(hb, dpb, hcb, dpcb, *parts)
    return out[0], out[1:]


def _pool_tables(w, rows, rb, pgw, transpose):
    t = np.arange(GRID_W)
    lo, hi = np.clip(t - w // 2, 0, GRID_W), np.clip(t + w - w // 2, 0, GRID_W)
    band = ((t[None, :] >= lo[:, None]) & (t[None, :] < hi[:, None])).astype(np.float32)
    if transpose:
        band = band.T
    kc = np.kron(np.eye(rb, dtype=np.float32), band)
    inv_c = np.tile((1.0 / (hi - lo).astype(np.float32))[:, None], (rb, pgw)).astype(np.float32)
    r = np.arange(rows)
    cnt_r = np.clip(r + w - w // 2, 0, rows) - np.clip(r - w // 2, 0, rows)
    inv_r = (1.0 / cnt_r.astype(np.float32)).astype(np.float32)
    return jnp.asarray(kc, BF16), jnp.asarray(inv_r), jnp.asarray(inv_c)


def _pool_stack(rows, rb, pgw, transpose):
    tabs = [_pool_tables(w, rows, rb, pgw, transpose) for w in POOL_WINDOWS]
    return tuple(jnp.stack([t[i] for t in tabs]) for i in range(3))


def _window_sums(ref, r0, rb, w):
    slab = lambda off: ref[pl.ds(pl.multiple_of((r0 + off) * GRID_W, GRID_W), GRID_W), :]
    shared = range(rb - 1, w)
    common = None
    for off in shared:
        common = slab(off) if common is None else common + slab(off)
    sums = []
    for rr in range(rb):
        s = common
        for off in range(rr, rr + w):
            if off not in shared:
                s = slab(off) if s is None else s + slab(off)
        sums.append(s)
    return sums


def _per_window(group):
    for k, w in enumerate(POOL_WINDOWS):
        pl.when(pl.program_id(0) == k)(lambda k=k, w=w: group(k, w))


def _pool_fwd(proj, pool_wf):
    lx = proj.shape[0]
    ng, pgw, _ = pool_wf.shape
    rows = lx // GRID_W
    rb = min(4, rows)
    tok = rb * GRID_W
    kc, inv_r, inv_c = _pool_stack(rows, rb, pgw, False)

    def body(p_ref, w_ref, kc_ref, ir_ref, ic_ref, lin_ref, dm_ref, xp):
        def group(kk, w):
            lo = w // 2
            xp[pl.ds(0, lo * GRID_W), :] = jnp.zeros((lo * GRID_W, pgw), F32)
            xp[pl.ds((lo + rows) * GRID_W, (w - lo) * GRID_W), :] = jnp.zeros(((w - lo) * GRID_W, pgw), F32)
            xp[pl.ds(lo * GRID_W, lx), :] = p_ref[...]

            def blk(b, carry):
                r0 = b * rb
                parts = [s * ir_ref[kk, r0 + rr] for rr, s in enumerate(_window_sums(xp, r0, rb, w))]
                m = _mm_split(kc_ref[0], jnp.concatenate(parts, axis=0)) * ic_ref[0]
                u = xp[pl.ds(pl.multiple_of((r0 + lo) * GRID_W, GRID_W), tok), :]
                db = (m - u).astype(BF16)
                o0 = pl.multiple_of(r0 * GRID_W, GRID_W)
                dm_ref[pl.ds(o0, tok), :] = db
                lin_ref[pl.ds(o0, tok), :] = _mm(db, w_ref[0])
                return carry

            lax.fori_loop(0, rows // rb, blk, 0)

        _per_window(group)

    col = pl.BlockSpec((lx, pgw), lambda g: (0, g))
    return pl.pallas_call(
        body,
        name="pool_fwd",
        grid=(ng,),
        in_specs=[col, _gspec(1, pgw, pgw), _gspec(1, tok, tok), _SMEM, _gspec(1, tok, pgw)],
        out_specs=[col, col],
        out_shape=[_sds((lx, ng * pgw)), _sds((lx, ng * pgw), BF16)],
        scratch_shapes=[pltpu.VMEM(((rows + max(POOL_WINDOWS)) * GRID_W, pgw), F32)],
        compiler_params=_cparams(1, VMEM_BIG),
    )(proj, pool_wf, kc, inv_r, inv_c)


def _pool_bwd(dlin, pool_wf, dmat):
    lx = dlin.shape[0]
    ng, pgw, _ = pool_wf.shape
    rows = lx // GRID_W
    rb = min(4, rows)
    tok = rb * GRID_W
    kct, inv_r, inv_c = _pool_stack(rows, rb, pgw, True)

    def body(dl_ref, w_ref, kc_ref, ir_ref, ic_ref, dm_ref, du_ref, dw_ref, tp):
        def group(kk, w):
            front = w - w // 2 - 1
            if front:
                tp[pl.ds(0, front * GRID_W), :] = jnp.zeros((front * GRID_W, pgw), F32)
            tp[pl.ds((front + rows) * GRID_W, (w - front) * GRID_W), :] = jnp.zeros(((w - front) * GRID_W, pgw), F32)
            dw_ref[...] = jnp.zeros_like(dw_ref)

            def blk(b, carry):
                o0 = pl.multiple_of(b * tok, GRID_W)
                dlb = dl_ref[pl.ds(o0, tok), :].astype(BF16)
                dw_ref[0] += _mm_tn(dm_ref[pl.ds(o0, tok), :], dlb)
                dd = _mm_nt(dlb, w_ref[0])
                du_ref[pl.ds(o0, tok), :] = -dd
                t = _mm(kc_ref[0], (dd * ic_ref[0]).astype(BF16))
                for rr in range(rb):
                    dst = pl.multiple_of((b * rb + rr + front) * GRID_W, GRID_W)
                    tp[pl.ds(dst, GRID_W), :] = t[rr * GRID_W : (rr + 1) * GRID_W] * ir_ref[kk, b * rb + rr]
                return carry

            lax.fori_loop(0, rows // rb, blk, 0)

            def rowl(b, carry):
                for rr, s in enumerate(_window_sums(tp, b * rb, rb, w)):
                    du_ref[pl.ds(pl.multiple_of((b * rb + rr) * GRID_W, GRID_W), GRID_W), :] += s
                return carry

            lax.fori_loop(0, rows // rb, rowl, 0)

        _per_window(group)

    col = pl.BlockSpec((lx, pgw), lambda g: (0, g))
    return pl.pallas_call(
        body,
        name="pool_bwd",
        grid=(ng,),
        in_specs=[col, _gspec(1, pgw, pgw), _gspec(1, tok, tok), _SMEM, _gspec(1, tok, pgw), col],
        out_specs=[col, _gspec(1, pgw, pgw)],
        out_shape=[_sds((lx, ng * pgw)), _sds((ng, pgw, pgw))],
        scratch_shapes=[pltpu.VMEM(((rows + max(POOL_WINDOWS)) * GRID_W, pgw), F32)],
        compiler_params=_cparams(1, VMEM_BIG),
    )(dlin, pool_wf, kct, inv_r, inv_c, dmat)


def _s5_small(lam_re, lam_im, log_dt, b_re, b_im, c_re, c_im):
    t = CHUNK_T
    dt = jnp.exp(log_dt)[..., None]
    zr, zi = lam_re * dt, lam_im * dt
    tau = jnp.arange(t + 1, dtype=F32)
    mag = jnp.exp(zr[..., None] * tau)
    pr, pi = mag * jnp.cos(zi[..., None] * tau), mag * jnp.sin(zi[..., None] * tau)
    ar, ai = pr[..., 1], pi[..., 1]
    den = lam_re * lam_re + lam_im * lam_im
    qr = ((ar - 1.0) * lam_re + ai * lam_im) / den
    qi = (ai * lam_re - (ar - 1.0) * lam_im) / den
    bbr = qr[..., None] * b_re - qi[..., None] * b_im
    bbi = qr[..., None] * b_im + qi[..., None] * b_re
    ctr, cti = jnp.swapaxes(c_re, 2, 3), jnp.swapaxes(c_im, 2, 3)
    lane_pad = lambda v: jnp.pad(jnp.swapaxes(v, 0, 1), ((0, 0), (0, 0), (0, 0), (0, LANES - v.shape[-1])))
    tp = lambda v: jnp.swapaxes(v, 1, 2)
    flip = lambda v: v[..., ::-1]
    pfr, pfi = tp(flip(pr[0, ..., :t])), tp(flip(pi[0, ..., :t]))
    pbr, pbi = tp(pr[1, ..., :t]), tp(pi[1, ..., :t])
    bfr, bfi, bbr_t, bbi_t = tp(bbr[0]), tp(bbi[0]), tp(bbr[1]), tp(bbi[1])
    a1 = jnp.concatenate([pfr, pbr, pfr, pbr], axis=-1)
    a2 = jnp.concatenate([-pfi, -pbi, pfi, pbi], axis=-1)
    b1 = jnp.concatenate([bfr, bbr_t, bfi, bbi_t], axis=-1)
    b2 = jnp.concatenate([bfi, bbi_t, bfr, bbr_t], axis=-1)
    a16 = jnp.concatenate([pr[0, ..., t], pr[1, ..., t], pi[0, ..., t], pi[1, ..., t]], axis=1)
    return (lane_pad(pr), lane_pad(pi), lane_pad(bbr), lane_pad(bbi), lane_pad(ctr), lane_pad(cti), a1, a2, b1, b2), a16


def _split2(v):
    hi = v.astype(BF16)
    return hi, (v - hi.astype(F32)).astype(BF16)


def _dotx(a, b, mm=_mm):
    ah, al = (a, None) if a.dtype == BF16 else _split2(a)
    bh, bl = (b, None) if b.dtype == BF16 else _split2(b)
    out = mm(ah, bh)
    if bl is not None:
        out = out + mm(ah, bl)
    if al is not None:
        out = out + mm(al, bh)
    return out


def _shift_lanes(v, k):
    if k == 0:
        return v
    n = v.shape[-1]
    lane = lax.broadcasted_iota(jnp.int32, v.shape, v.ndim - 1)
    r = pltpu.roll(v, k % n, axis=v.ndim - 1)
    return jnp.where(lane >= k, r, 0.0) if k > 0 else jnp.where(lane < n + k, r, 0.0)


def _op_consts():
    t, h = CHUNK_T, SSM_H
    j, o = np.divmod(np.arange(t * h), h)
    p = np.arange(LANES)[:, None]
    sel = lambda col: (p == col[None, :]).astype(np.float32)
    pw = np.stack([np.concatenate([sel(j), sel(j + 1)], axis=1), np.concatenate([sel(t - 1 - j), sel(t - j)], axis=1)])
    return jnp.asarray(pw, BF16), jnp.asarray(sel(o), BF16), jnp.asarray(sel(j).T, BF16), jnp.asarray(sel(o).T, BF16)


def _op_factors(gi, pr_ref, pi_ref, br_ref, bi_ref, cr_ref, ci_ref, pw_ref, ch_ref):
    n = pr_ref.shape[2]
    c16 = _dotx(jnp.concatenate([cr_ref[gi, 0], ci_ref[gi, 0], cr_ref[gi, 1], ci_ref[gi, 1]], axis=0), ch_ref[...])
    out = []
    for d in range(2):
        e = _dotx(jnp.concatenate([pr_ref[gi, d], pi_ref[gi, d]], axis=0), pw_ref[d])
        c = c16[2 * d * n : (2 * d + 1) * n], c16[(2 * d + 1) * n : (2 * d + 2) * n]
        bst = jnp.concatenate([br_ref[gi, d], -bi_ref[gi, d]], axis=0)
        out.append(((e[:n, :CW], e[n:, :CW]), (e[:n, CW:], e[n:, CW:]), c, bst))
    return out


def _op_spread(m_ref, u_ref, v_ref, gi):
    return _dotx(m_ref[...], jnp.concatenate([_rows_pad(u_ref[gi]), _rows_pad(v_ref[gi])], axis=1))


def _cmul(x, y):
    return x[0] * y[0] - x[1] * y[1], x[0] * y[1] + x[1] * y[0]


def _cmul_conj(g, y):
    return g[0] * y[0] + g[1] * y[1], g[1] * y[0] - g[0] * y[1]


def _rows_pad(v):
    return jnp.concatenate([v, jnp.zeros((LANES - v.shape[0], v.shape[1]), F32)], axis=0)


def _s5_build(small, gb, shards):
    pr, pi, br, bi, cr, ci, a1, a2, b1, b2 = small
    g, _, n, _ = pr.shape
    t, h, ns = CHUNK_T, SSM_H, 4 * n
    pw, ch, rep_s, til_s = _op_consts()
    nr, nin, steps = len(shards), 14, g // gb

    def body(*refs):
        pr_ref, pi_ref, br_ref, bi_ref, cr_ref, ci_ref, a1_ref, a2_ref, b1_ref, b2_ref, pw_ref, ch_ref, rs_ref, ts_ref = refs[:nin]
        t_ref, w_ref, c_ref = refs[nin + nr : nin + nr + 3]
        start, finish = _gather_copies(refs[nin : nin + nr], refs[nin + nr + 3 : nin + 2 * nr + 3], *refs[nin + 2 * nr + 3 :])
        pl.when(pl.program_id(0) == 0)(start)

        def group(gi, carry):
            rows, outs = [], []
            for e16, e1, c16, bst in _op_factors(gi, pr_ref, pi_ref, br_ref, bi_ref, cr_ref, ci_ref, pw_ref, ch_ref):
                rows.append(_dotx(bst, jnp.concatenate(_cmul(c16, e16), axis=0), _mm_tn)[:h])
                outs.append(_cmul(c16, e1))
            c_ref[gi] = jnp.concatenate([outs[0][0], outs[1][0], -outs[0][1], -outs[1][1]], axis=0).astype(BF16)
            for s in range(t):
                t_ref[gi, s * h : (s + 1) * h, :] = (_shift_lanes(rows[0], h * s) + _shift_lanes(rows[1], -h * (t - 1 - s))).astype(BF16)
            ae, be = _op_spread(rs_ref, a1_ref, a2_ref, gi), _op_spread(ts_ref, b1_ref, b2_ref, gi)
            w_ref[gi] = (ae[:, :ns] * be[:, :ns] + ae[:, ns:] * be[:, ns:]).astype(BF16)
            return carry

        lax.fori_loop(0, gb, group, 0)
        pl.when(pl.program_id(0) == steps - 1)(finish)

    sm = pl.BlockSpec((gb, 2, n, LANES), lambda i: (i, 0, 0, 0))
    ab = lambda v: pl.BlockSpec((gb,) + v.shape[1:], lambda i: (i, 0, 0))
    out = pl.pallas_call(
        body,
        name="s5_build",
        grid=(steps,),
        in_specs=[sm] * 6 + [ab(a1), ab(a2), ab(b1), ab(b2), _whole(pw.shape), _whole(ch.shape), _whole(rep_s.shape), _whole(til_s.shape)] + [_HBM] * nr,
        out_specs=[_gspec(gb, CW, CW), _gspec(gb, CW, 4 * n), _gspec(gb, 4 * n, CW)] + [_HBM] * nr,
        out_shape=[_sds((g, CW, CW), BF16), _sds((g, CW, 4 * n), BF16), _sds((g, 4 * n, CW), BF16)] + _gather_shapes(shards),
        scratch_shapes=_gather_sems(nr),
        compiler_params=_cparams(1, VMEM_BIG),
    )(pr, pi, br, bi, cr, ci, a1, a2, b1, b2, pw, ch, rep_s, til_s, *shards)
    return out[:3], out[3:]


def _s5_build_bwd(small, dtsum, dwst, dccat, gb, parts):
    pr, pi, br, bi, cr, ci, a1, a2, b1, b2 = small
    g, _, n, _ = pr.shape
    t, h, ns = CHUNK_T, SSM_H, 4 * n
    pw, ch, rep_s, til_s = _op_consts()
    nr, nin, nout, steps = len(parts), 17, 10, g // gb

    def body(*refs):
        pr_ref, pi_ref, br_ref, bi_ref, cr_ref, ci_ref, a1_ref, a2_ref, b1_ref, b2_ref, pw_ref, ch_ref, rs_ref, ts_ref, dt_ref, dw_ref, dc_ref = refs[:nin]
        dpr_ref, dpi_ref, dbr_ref, dbi_ref, dcr_ref, dci_ref, da1_ref, da2_ref, db1_ref, db2_ref = refs[nin + nr : nin + nr + nout]
        start, finish = _exchange_copies(refs[nin : nin + nr], refs[nin + nr + nout : nin + 2 * nr + nout], *refs[nin + 2 * nr + nout :])
        pl.when(pl.program_id(0) == 0)(start)

        def group(gi, carry):
            drow = [jnp.zeros((h, CW), F32), jnp.zeros((h, CW), F32)]
            for s in range(t):
                blk = dt_ref[gi, s * h : (s + 1) * h, :]
                drow[0] = drow[0] + _shift_lanes(blk, -h * s)
                drow[1] = drow[1] + _shift_lanes(blk, h * (t - 1 - s))
            dcc = dc_ref[gi]
            dcs = []
            for d, (e16, e1, c16, bst) in enumerate(_op_factors(gi, pr_ref, pi_ref, br_ref, bi_ref, cr_ref, ci_ref, pw_ref, ch_ref)):
                dr = _rows_pad(drow[d])
                db = _dotx(jnp.concatenate(_cmul(c16, e16), axis=0), dr, _mm_nt)
                dbr_ref[gi, d] = db[:n]
                dbi_ref[gi, d] = -db[n:]
                dst = _dotx(bst, dr)
                dca = dst[:n], dst[n:]
                gx = dcc[d * n : (d + 1) * n], -dcc[(2 + d) * n : (3 + d) * n]
                dc16, dc16b = _cmul_conj(dca, e16), _cmul_conj(gx, e1)
                dcs += [dc16[0] + dc16b[0], dc16[1] + dc16b[1]]
                de = jnp.concatenate([jnp.concatenate(_cmul_conj(dca, c16), axis=0), jnp.concatenate(_cmul_conj(gx, c16), axis=0)], axis=1)
                dp = _dotx(de, pw_ref[d], _mm_nt)
                dpr_ref[gi, d] = dp[:n]
                dpi_ref[gi, d] = dp[n:]
            dct = _dotx(jnp.concatenate(dcs, axis=0), ch_ref[...], _mm_nt)
            for d in range(2):
                dcr_ref[gi, d] = dct[2 * d * n : (2 * d + 1) * n]
                dci_ref[gi, d] = dct[(2 * d + 1) * n : (2 * d + 2) * n]
            dw = dw_ref[gi]
            ae, be = _op_spread(rs_ref, a1_ref, a2_ref, gi), _op_spread(ts_ref, b1_ref, b2_ref, gi)
            da = _dotx(rs_ref[...], jnp.concatenate([dw * be[:, :ns], dw * be[:, ns:]], axis=1), _mm_tn)
            db = _dotx(ts_ref[...], jnp.concatenate([dw * ae[:, :ns], dw * ae[:, ns:]], axis=1), _mm_tn)
            da1_ref[gi] = da[:t, :ns]
            da2_ref[gi] = da[:t, ns:]
            db1_ref[gi] = db[:h, :ns]
            db2_ref[gi] = db[:h, ns:]
            return carry

        lax.fori_loop(0, gb, group, 0)
        pl.when(pl.program_id(0) == steps - 1)(finish)

    sm = pl.BlockSpec((gb, 2, n, LANES), lambda i: (i, 0, 0, 0))
    ab = lambda v: pl.BlockSpec((gb,) + v.shape[1:], lambda i: (i, 0, 0))
    ops = [_gspec(gb, CW, CW), _gspec(gb, CW, 4 * n), _gspec(gb, 4 * n, CW)]
    out = pl.pallas_call(
        body,
        name="s5_build_bwd",
        grid=(steps,),
        in_specs=[sm] * 6 + [ab(a1), ab(a2), ab(b1), ab(b2), _whole(pw.shape), _whole(ch.shape), _whole(rep_s.shape), _whole(til_s.shape)] + ops + [_HBM] * nr,
        out_specs=[sm] * 6 + [ab(a1), ab(a2), ab(b1), ab(b2)] + [_HBM] * nr,
        out_shape=[_sds(v.shape) for v in small] + [_sds(p.shape, p.dtype) for p in parts],
        scratch_shapes=_exchange_sems(nr),
        compiler_params=_cparams(1, VMEM_BIG),
    )(pr, pi, br, bi, cr, ci, a1, a2, b1, b2, pw, ch, rep_s, til_s, dtsum, dwst, dccat, *parts)
    return out[:nout], out[nout:]


SCAN_UNROLL = 4
GPL = LANES // SSM_H
CW = CHUNK_T * SSM_H
BW = CHUNK_T * LANES


def _chunk_perm():
    o = np.arange(BW)
    src = ((o % CW) // SSM_H) * LANES + (o // CW) * SSM_H + o % SSM_H
    return jnp.asarray(np.arange(BW)[:, None] == src[None, :], BF16)


def _gspec(gb, a, b):
    return pl.BlockSpec((gb, a, b), lambda i: (i, 0, 0))


def _chunk_rows(ref, n):
    return jnp.concatenate([ref[pl.ds(s, n, stride=CHUNK_T), :] for s in range(CHUNK_T)], axis=1)


def _s5_state(proj, col0, uc, perm, wst, ncc, ncl):
    nlb = uc.shape[1] // LANES
    g, _, ns = wst.shape
    nch = ncc + ncl
    half = ns // 2

    def body(p_ref, c_ref, pm_ref, w_ref, u_ref, sr_ref, si_ref):
        xrows = jnp.concatenate([_chunk_rows(c_ref, ncc), _chunk_rows(p_ref, ncl)], axis=0).astype(BF16)
        u = _mm(xrows, pm_ref[...]).astype(BF16)
        u_ref[0] = u
        for gi in range(GPL):
            s = _mm(u[:, gi * CW : (gi + 1) * CW], w_ref[gi].astype(BF16))
            sr_ref[gi] = s[:, :half]
            si_ref[gi] = s[:, half:]

    return pl.pallas_call(
        body,
        name="s5_state",
        grid=(nlb,),
        in_specs=[
            pl.BlockSpec((ncl * CHUNK_T, LANES), lambda i: (0, col0 + i)),
            pl.BlockSpec((ncc * CHUNK_T, LANES), lambda i: (0, i)),
            _whole(perm.shape),
            _gspec(GPL, CW, ns),
        ],
        out_specs=[pl.BlockSpec((1, nch, BW), lambda i: (i, 0, 0)), _gspec(GPL, nch, half), _gspec(GPL, nch, half)],
        out_shape=[_sds((nlb, nch, BW), BF16), _sds((g, nch, half)), _sds((g, nch, half))],
        compiler_params=_cparams(1, VMEM_BIG),
    )(proj, uc, perm, wst)


def _unrolled_loop(n, step, init):
    u = SCAN_UNROLL if n % SCAN_UNROLL == 0 else 1

    def trip(i, c):
        for k in range(u):
            c = step(i * u + k, c)
        return c

    return lax.fori_loop(0, n // u, trip, init)


def _lane_masks(gb, half):
    lane = lax.broadcasted_iota(jnp.int32, (gb, half), 1)
    return lane < (half // 2)


def _roll_half(v):
    return pltpu.roll(v, v.shape[-1] // 2, axis=v.ndim - 1)


def _pack_dirs(lo, hi, mf):
    return jnp.where(mf, lo, _roll_half(hi)), jnp.where(mf, _roll_half(lo), hi)


def _unpack_dirs(pf, pb, mf):
    return jnp.where(mf, pf, _roll_half(pb)), jnp.where(mf, _roll_half(pf), pb)


def _packed_spec(gb, nch, half, buffers=None):
    mode = {} if buffers is None else dict(pipeline_mode=pl.Buffered(buffers))
    if gb == 2 * GPL:
        return pl.BlockSpec((GPL, nch, half), lambda i: (i // 2, 0, 0), **mode)
    assert gb == GPL
    return pl.BlockSpec((gb // 2, nch, half), lambda i: (i, 0, 0), **mode)


def _pair_index(j, gb):
    if gb == 2 * GPL:
        return j, pl.program_id(0) % 2
    return j % (gb // 2), j // (gb // 2)


def _pick(which, lo, hi):
    return (hi if which else lo) if isinstance(which, int) else jnp.where(which == 0, lo, hi)


def _unpacked(pf_ref, pb_ref, j, gb, mf):
    jl, which = _pair_index(j, gb)
    return _pick(which, *_unpack_dirs(pf_ref[jl], pb_ref[jl], mf))


def _pair_tile(refs, jl, rows):
    return jnp.concatenate([r[jl, rows, :] for r in refs], axis=1)


def _pair_rows_of(op, which, q):
    z = jnp.zeros((q, op.shape[1]), op.dtype)
    blocks = [op[0:q], op[2 * q : 3 * q], op[q : 2 * q], op[3 * q : 4 * q]]
    lo = jnp.concatenate([x for b in blocks for x in (b, z)], axis=0)
    hi = jnp.concatenate([x for b in blocks for x in (z, b)], axis=0)
    return _pick(which, lo, hi)


def _group_rows_of(pair_rows, which, q):
    def take(off):
        blk = lambda k: pair_rows[2 * q * k + off : 2 * q * k + off + q]
        return jnp.concatenate([blk(0), blk(2), blk(1), blk(3)], axis=0)

    return _pick(which, take(0), take(q))


def _scan_rows(ref, row, hp, nch, mf):
    return _pack_dirs(ref[pl.ds(row, hp, stride=nch), :], ref[pl.ds(hp * nch + row, hp, stride=nch), :], mf)


def _s5_scan_fwd(s_re, s_im, a16, ncc, nch, gb):
    g, ns = a16.shape
    half = ns // 2
    hp = gb // 2

    def body(sr_ref, si_ref, a_ref, hfr_ref, hfi_ref, hbr_ref, hbi_ref):
        mf = _lane_masks(hp, half)
        afr, abr = _pack_dirs(a_ref[:hp, :half], a_ref[hp:, :half], mf)
        afi, abi = _pack_dirs(a_ref[:hp, half:], a_ref[hp:, half:], mf)
        zero = jnp.zeros((hp, half), F32)

        def step(t, c):
            fr, fi, br, bi = c
            rb = jnp.where(t < ncc, ncc - 1 - t, nch - 1 - (t - ncc))
            hfr_ref[pl.ds(t, hp, stride=nch), :] = fr
            hfi_ref[pl.ds(t, hp, stride=nch), :] = fi
            hbr_ref[pl.ds(rb, hp, stride=nch), :] = br
            hbi_ref[pl.ds(rb, hp, stride=nch), :] = bi
            sfr, sfi = _scan_rows(sr_ref, t, hp, nch, mf)[0], _scan_rows(si_ref, t, hp, nch, mf)[0]
            sbr, sbi = _scan_rows(sr_ref, rb, hp, nch, mf)[1], _scan_rows(si_ref, rb, hp, nch, mf)[1]
            return (afr * fr - afi * fi + sfr, afr * fi + afi * fr + sfi, abr * br - abi * bi + sbr, abr * bi + abi * br + sbi)

        _unrolled_loop(nch, step, (zero, zero, zero, zero))

    blk = pl.BlockSpec((gb * nch, half), lambda i: (i, 0))
    pblk = pl.BlockSpec((hp * nch, half), lambda i: (i, 0))
    return pl.pallas_call(
        body,
        name="s5_scan_fwd",
        grid=(g // gb,),
        in_specs=[blk, blk, pl.BlockSpec((gb, ns), lambda i: (i, 0))],
        out_specs=[pblk] * 4,
        out_shape=[_sds((g // 2 * nch, half))] * 4,
        compiler_params=_cparams(1, VMEM_BIG),
    )(s_re, s_im, a16)


def _token_rows_store(ref, val, n):
    for s in range(CHUNK_T):
        ref[pl.ds(s, n, stride=CHUNK_T), :] = val[:, s * LANES : (s + 1) * LANES]


def _s5_out(u_all, h4, tsum, ccat, perm, ncc, lx, gb):
    nlb, nch, _ = u_all.shape
    g, ns, _ = ccat.shape
    ncl = nch - ncc
    half = ns // 2

    def body(u_ref, hfr_ref, hfi_ref, hbr_ref, hbi_ref, t_ref, c_ref, pm_ref, y_ref):
        parts = []
        for gi in range(GPL):
            u = u_ref[0, ncc:, gi * CW : (gi + 1) * CW]
            jl, which = _pair_index(gi, gb)
            hs = _pair_tile((hfr_ref, hfi_ref, hbr_ref, hbi_ref), jl, slice(ncc, None)).astype(BF16)
            parts.append(_mm(u, t_ref[gi].astype(BF16)) + _mm(hs, _pair_rows_of(c_ref[gi].astype(BF16), which, ns // 4)))
        _token_rows_store(y_ref, _mm_nt(jnp.concatenate(parts, axis=1).astype(BF16), pm_ref[...]), ncl)

    return pl.pallas_call(
        body,
        name="s5_out",
        grid=(nlb,),
        in_specs=[pl.BlockSpec((1, nch, BW), lambda i: (i, 0, 0))] + [_packed_spec(gb, nch, half)] * 4 + [_gspec(GPL, CW, CW), _gspec(GPL, ns, CW), _whole(perm.shape)],
        out_specs=pl.BlockSpec((lx, LANES), lambda i: (0, i)),
        out_shape=_sds((lx, nlb * LANES)),
        compiler_params=_cparams(1, VMEM_BIG),
    )(u_all, *h4, tsum, ccat, perm)


def _s5_dstate(dypre, perm, ccat, ncc, swaps=()):
    lx, sw = dypre.shape
    nr = len(swaps)
    nlb = sw // LANES
    g, ns, _ = ccat.shape
    ncl = lx // CHUNK_T
    nch = ncl + ncc
    half = ns // 2

    def body(dy_ref, pm_ref, c_ref, dyr_ref, dr_ref, di_ref):
        dy = _mm(_chunk_rows(dy_ref, ncl).astype(BF16), pm_ref[...]).astype(BF16)
        dyr_ref[0] = dy
        for gi in range(GPL):
            dh = _mm_nt(dy[:, gi * CW : (gi + 1) * CW], c_ref[gi].astype(BF16))
            dr_ref[gi, :ncc, :] = jnp.zeros((ncc, half), F32)
            di_ref[gi, :ncc, :] = jnp.zeros((ncc, half), F32)
            dr_ref[gi, ncc:, :] = dh[:, :half]
            di_ref[gi, ncc:, :] = dh[:, half:]

    out = pl.pallas_call(
        _riding(body, 3, 3, nr, (nlb,), _swap_copies),
        name="s5_dstate",
        grid=(nlb,),
        in_specs=[pl.BlockSpec((lx, LANES), lambda i: (0, i)), _whole(perm.shape), _gspec(GPL, ns, CW)] + [_HBM] * nr,
        out_specs=[pl.BlockSpec((1, ncl, BW), lambda i: (i, 0, 0)), _gspec(GPL, nch, half), _gspec(GPL, nch, half)] + [_HBM] * nr,
        out_shape=[_sds((nlb, ncl, BW), BF16), _sds((g, nch, half)), _sds((g, nch, half))] + _swap_shapes(swaps),
        scratch_shapes=_swap_sems(nr) if nr else [],
        compiler_params=_cparams(1, VMEM_BIG),
    )(dypre, perm, ccat, *swaps)
    return out[:3], out[3:]


def _s5_scan_bwd(dh_re, dh_im, a16, ncc, nch, gb):
    g, ns = a16.shape
    half = ns // 2
    ncl = nch - ncc
    hp = gb // 2

    def body(dr_ref, di_ref, a_ref, sfr_ref, sfi_ref, sbr_ref, sbi_ref):
        mf = _lane_masks(hp, half)
        afr, abr = _pack_dirs(a_ref[:hp, :half], a_ref[hp:, :half], mf)
        afi, abi = _pack_dirs(a_ref[:hp, half:], a_ref[hp:, half:], mf)
        zero = jnp.zeros((hp, half), F32)

        def step(t, c):
            lfr, lfi, lbr, lbi = c
            pf = nch - 1 - t
            pb = jnp.where(t < ncl, ncc + t, t - ncl)
            sfr_ref[pl.ds(pf, hp, stride=nch), :] = lfr
            sfi_ref[pl.ds(pf, hp, stride=nch), :] = lfi
            sbr_ref[pl.ds(pb, hp, stride=nch), :] = lbr
            sbi_ref[pl.ds(pb, hp, stride=nch), :] = lbi
            dfr, dfi = _scan_rows(dr_ref, pf, hp, nch, mf)[0], _scan_rows(di_ref, pf, hp, nch, mf)[0]
            dbr, dbi = _scan_rows(dr_ref, pb, hp, nch, mf)[1], _scan_rows(di_ref, pb, hp, nch, mf)[1]
            return (dfr + afr * lfr + afi * lfi, dfi + afr * lfi - afi * lfr, dbr + abr * lbr + abi * lbi, dbi + abr * lbi - abi * lbr)

        _unrolled_loop(nch, step, (zero,) * 4)

    blk = pl.BlockSpec((gb * nch, half), lambda i: (i, 0))
    pblk = pl.BlockSpec((hp * nch, half), lambda i: (i, 0))
    return pl.pallas_call(
        body,
        name="s5_scan_bwd",
        grid=(g // gb,),
        in_specs=[blk, blk, pl.BlockSpec((gb, ns), lambda i: (i, 0))],
        out_specs=[pblk] * 4,
        out_shape=[_sds((g // 2 * nch, half))] * 4,
        compiler_params=_cparams(1, VMEM_BIG),
    )(dh_re, dh_im, a16)


def _s5_du(u_all, dy_all, ds4, tsum, wst, perm, ncc, lx, lc, gb):
    nlb, ncl, _ = dy_all.shape
    g, _, ns = wst.shape
    nch = ncc + ncl
    half = ns // 2

    def body(u_ref, dy_ref, sfr_ref, sfi_ref, sbr_ref, sbi_ref, t_ref, w_ref, pm_ref, du_ref, dc_ref, dw_ref):
        parts = []
        mf = _lane_masks(nch, half)
        for gi in range(GPL):
            ds = jnp.concatenate([_unpacked(sfr_ref, sbr_ref, gi, gb, mf), _unpacked(sfi_ref, sbi_ref, gi, gb, mf)], axis=1).astype(BF16)
            dw_ref[gi] = _mm_tn(u_ref[0, :, gi * CW : (gi + 1) * CW], ds)
            d_all = _mm_nt(ds, w_ref[gi].astype(BF16))
            d_lat = d_all[ncc:] + _mm_nt(dy_ref[0, :, gi * CW : (gi + 1) * CW], t_ref[gi].astype(BF16))
            parts.append(jnp.concatenate([d_all[:ncc], d_lat], axis=0))
        du = _mm_nt(jnp.concatenate(parts, axis=1).astype(BF16), pm_ref[...])
        _token_rows_store(dc_ref, du[:ncc], ncc)
        _token_rows_store(du_ref, du[ncc:], ncl)

    return pl.pallas_call(
        body,
        name="s5_du",
        grid=(nlb,),
        in_specs=[pl.BlockSpec((1, nch, BW), lambda i: (i, 0, 0)), pl.BlockSpec((1, ncl, BW), lambda i: (i, 0, 0))]
        + [_packed_spec(gb, nch, half)] * 4
        + [_gspec(GPL, CW, CW), _gspec(GPL, CW, ns), _whole(perm.shape)],
        out_specs=[pl.BlockSpec((lx, LANES), lambda i: (0, i)), pl.BlockSpec((lc, LANES), lambda i: (0, i)), _gspec(GPL, CW, ns)],
        out_shape=[_sds((lx, nlb * LANES)), _sds((lc, nlb * LANES)), _sds((g, CW, ns))],
        compiler_params=_cparams(1, VMEM_BIG),
    )(u_all, dy_all, *ds4, tsum, wst, perm)


def _s5_op_grads(u_all, dy_all, ds4, h4, ncc, gb):
    nlb, nch, _ = u_all.shape
    half = ds4[0].shape[2]
    g = 2 * ds4[0].shape[0]
    ns = 2 * half

    def body(u_ref, dy_ref, sfr_ref, sfi_ref, sbr_ref, sbi_ref, hfr_ref, hfi_ref, hbr_ref, hbi_ref, dt_ref, dc_ref, da_ref):
        das = []
        mf1 = _lane_masks(1, half)
        for gi in range(GPL):
            cols = slice(gi * CW, (gi + 1) * CW)
            jl, which = _pair_index(gi, gb)
            dy = dy_ref[0, :, cols]
            dt_ref[gi] = _mm_tn(u_ref[0, ncc:, cols], dy)
            hs = _pair_tile((hfr_ref, hfi_ref, hbr_ref, hbi_ref), jl, slice(ncc, None)).astype(BF16)
            dc_ref[gi] = _group_rows_of(_mm_tn(hs, dy), which, ns // 4)
            hfr, hfi, hbr, hbi = hfr_ref[jl], hfi_ref[jl], hbr_ref[jl], hbi_ref[jl]
            sfr, sfi, sbr, sbi = sfr_ref[jl], sfi_ref[jl], sbr_ref[jl], sbi_ref[jl]
            a_re = _pick(which, *_unpack_dirs(_colsum(hfr * sfr + hfi * sfi), _colsum(hbr * sbr + hbi * sbi), mf1))
            a_im = _pick(which, *_unpack_dirs(_colsum(hfr * sfi - hfi * sfr), _colsum(hbr * sbi - hbi * sbr), mf1))
            das.append(jnp.concatenate([a_re, a_im], axis=1))
        da_ref[...] = jnp.concatenate(das, axis=0)

    return pl.pallas_call(
        body,
        name="s5_op_grads",
        grid=(nlb,),
        in_specs=[pl.BlockSpec((1, nch, BW), lambda i: (i, 0, 0)), pl.BlockSpec((1, nch - ncc, BW), lambda i: (i, 0, 0))]
        + [_packed_spec(gb, nch, half)] * 4
        + [_packed_spec(gb, nch, half, buffers=1)] * 4,
        out_specs=[_gspec(GPL, CW, CW), _gspec(GPL, ns, CW), pl.BlockSpec((GPL, ns), lambda i: (i, 0))],
        out_shape=[_sds((g, CW, CW)), _sds((g, ns, CW)), _sds((g, ns))],
        compiler_params=_cparams(1, VMEM_BIG),
    )(u_all, dy_all, *ds4, *h4)


def _merge_fwd(y_ssm, proj, lin, x, tgt, glu_wg, out_w, d_skip, glu_b, pscale, gate, fin_g, tm):
    lx, d = x.shape
    sw = y_ssm.shape[1]
    mix = out_w.shape[0]
    pw = mix - sw
    cb2 = glu_wg.shape[2]
    nh = NDEV // 2

    def body(y_ref, u_ref, z_ref, l_ref, x_ref, t_ref, gw_ref, ow_ref, dk_ref, gb_ref, ps_ref, gt_ref, fg_ref,
             ypre_ref, y12_ref, br_ref, dxo_ref, loss_ref, dfg_ref, dgt_ref):
        @pl.when(pl.program_id(0) == 0)
        def _():
            loss_ref[...] = jnp.zeros_like(loss_ref)
            dfg_ref[...] = jnp.zeros_like(dfg_ref)
            dgt_ref[...] = jnp.zeros_like(dgt_ref)

        ypre = y_ref[...] + dk_ref[...] * u_ref[...]
        ypre_ref[...] = ypre
        yg = _gelu(ypre).astype(BF16)
        outs = []
        for j in range(nh):
            y1 = _mm(yg, gw_ref[j]) + gb_ref[:, j * cb2 : (j + 1) * cb2]
            y2 = _mm(yg, gw_ref[nh + j]) + gb_ref[:, (nh + j) * cb2 : (nh + j + 1) * cb2]
            y12_ref[:, j * cb2 : (j + 1) * cb2] = y1
            y12_ref[:, (nh + j) * cb2 : (nh + j + 1) * cb2] = y2
            outs.append(y1 * _sigmoid(y2))
        lin = l_ref[...] * ps_ref[...]
        brb = (jnp.concatenate([lin] + outs, axis=1) * _silu(z_ref[...])).astype(BF16)
        br_ref[...] = brb
        mixv = _mm(brb, ow_ref[...])
        xo = x_ref[...] + gt_ref[...] * mixv
        r2 = lax.rsqrt(_rowmean(xo * xo) + EPS)
        xh = xo * r2
        fg = fg_ref[...]
        e = xh * fg - t_ref[...]
        loss_ref[...] += 0.5 * _colsum(_rowmean(e * e))
        dy = e * (1.0 / d)
        dfg_ref[...] += _colsum(dy * xh)
        gy = dy * fg
        dxo = r2 * (gy - xh * _rowmean(gy * xh))
        dxo_ref[...] = dxo
        dgt_ref[...] += _colsum(dxo * mixv)

    vec = _whole((1, d))
    return pl.pallas_call(
        body,
        name="merge_fwd",
        grid=(lx // tm,),
        in_specs=[_rows(tm, sw), _rows(tm, sw, 1), _rows(tm, mix, 1), _rows(tm, pw), _rows(tm, d), _rows(tm, d), _whole(glu_wg.shape), _whole(out_w.shape), _whole((1, sw)), _whole((1, 2 * sw)), _whole((1, pw)), vec, vec],
        out_specs=[_rows(tm, sw), _rows(tm, 2 * sw), _rows(tm, mix), _rows(tm, d), _acc((1, 1)), _acc((1, d)), _acc((1, d))],
        out_shape=[_sds((lx, sw)), _sds((lx, 2 * sw)), _sds((lx, mix), BF16), _sds((lx, d)), _sds((1, 1)), _sds((1, d)), _sds((1, d))],
        compiler_params=_cparams(1, VMEM_BIG),
    )(y_ssm, proj, proj, lin, x, tgt, glu_wg, out_w, d_skip, glu_b, pscale, gate, fin_g)


def _out_bwd(dxo, gate, branch, out_w, lin, y12, proj, pscale, tm):
    lx, d = dxo.shape
    mix = out_w.shape[0]
    pw = lin.shape[1]
    sw = mix - pw
    nb = lx // tm

    def body(dxo_ref, gt_ref, br_ref, ow_ref, l_ref, y_ref, z_ref, ps_ref, dz_ref, dl_ref, dg_ref, dps_ref, dow_hbm, acc, sem):
        i = pl.program_id(0)

        @pl.when(i == 0)
        def _():
            acc[...] = jnp.zeros_like(acc)
            dps_ref[...] = jnp.zeros_like(dps_ref)

        dmix = (dxo_ref[...] * gt_ref[...]).astype(BF16)
        dbr = _mm_nt(dmix, ow_ref[...])
        acc[...] += _mm_tn(br_ref[...], dmix)
        z = z_ref[...]
        sg = _sigmoid(z)
        dbp = dbr * (z * sg)
        y1, y2 = y_ref[:, :sw], y_ref[:, sw:]
        s2 = _sigmoid(y2)
        ps = ps_ref[...]
        lin = l_ref[...]
        bp = jnp.concatenate([lin * ps, y1 * s2], axis=1)
        dz_ref[...] = dbr * bp * (sg * (1.0 + z * (1.0 - sg)))
        dlp = dbp[:, :pw]
        dl_ref[...] = dlp * ps
        dps_ref[...] += _colsum(dlp * lin)
        dss = dbp[:, pw:]
        dg_ref[...] = jnp.concatenate([dss * s2, dss * y1 * s2 * (1.0 - s2)], axis=1).astype(BF16)

        @pl.when(i == nb - 1)
        def _():
            cp = pltpu.make_async_copy(acc, dow_hbm, sem)
            cp.start()
            cp.wait()

    return pl.pallas_call(
        body,
        name="out_bwd",
        grid=(nb,),
        in_specs=[_rows(tm, d), _whole((1, d)), _rows(tm, mix), _whole(out_w.shape), _rows(tm, pw), _rows(tm, 2 * sw), _rows(tm, mix, 1), _whole((1, pw))],
        out_specs=[_rows(tm, mix), _rows(tm, pw), _rows(tm, 2 * sw), _acc((1, pw)), _ANY],
        out_shape=[_sds((lx, mix)), _sds((lx, pw)), _sds((lx, 2 * sw), BF16), _sds((1, pw)), _sds((mix, d))],
        scratch_shapes=[pltpu.VMEM((mix, d), F32), pltpu.SemaphoreType.DMA],
        compiler_params=_cparams(1, VMEM_BIG),
    )(dxo, gate, branch, out_w, lin, y12, proj, pscale)


def _glu_bwd(dg12, ypre, proj, glu_wg, d_skip, tm):
    lx, sw = ypre.shape
    cb2 = glu_wg.shape[2]
    nb = lx // tm

    def body(dg_ref, yp_ref, u_ref, gw_ref, dk_ref, dyp_ref, dgb_ref, dd_ref, dgw_hbm, acc, sem):
        i = pl.program_id(0)

        @pl.when(i == 0)
        def _():
            acc[...] = jnp.zeros_like(acc)
            dgb_ref[...] = jnp.zeros_like(dgb_ref)
            dd_ref[...] = jnp.zeros_like(dd_ref)

        ypre = yp_ref[...]
        ygb = _gelu(ypre).astype(BF16)
        dg = dg_ref[...]
        dyg = jnp.zeros((tm, sw), F32)
        for j in range(NDEV):
            dgj = dg[:, j * cb2 : (j + 1) * cb2]
            dyg = dyg + _mm_nt(dgj, gw_ref[j])
            acc[j] += _mm_tn(ygb, dgj)
        dgb_ref[...] += _colsum(dg.astype(F32))
        dyp = dyg * _gelu_grad(ypre)
        dyp_ref[...] = dyp
        dd_ref[...] += _colsum(dyp * u_ref[...])

        @pl.when(i == nb - 1)
        def _():
            cp = pltpu.make_async_copy(acc, dgw_hbm, sem)
            cp.start()
            cp.wait()

    return pl.pallas_call(
        body,
        name="glu_bwd",
        grid=(nb,),
        in_specs=[_rows(tm, 2 * sw), _rows(tm, sw), _rows(tm, sw, 1), _whole(glu_wg.shape), _whole((1, sw))],
        out_specs=[_rows(tm, sw), _acc((1, 2 * sw)), _acc((1, sw)), _ANY],
        out_shape=[_sds((lx, sw)), _sds((1, 2 * sw)), _sds((1, sw)), _sds(glu_wg.shape)],
        scratch_shapes=[pltpu.VMEM(glu_wg.shape, F32), pltpu.SemaphoreType.DMA],
        compiler_params=_cparams(1, VMEM_BIG),
    )(dg12, ypre, proj, glu_wg, d_skip)


def _pack(parts, total):
    flat = jnp.concatenate([p.reshape(-1).astype(F32) for p in parts])
    return jnp.pad(flat, (0, total - flat.shape[0]))


def _unpack(flat, shapes):
    out, off = [], 0
    for s in shapes:
        n = int(np.prod(s)) if len(s) else 1
        out.append(flat[off : off + n].reshape(s))
        off += n
    return out


def kernel(x, c, ctx, c_ctx, ada_w, ada_b, norm_g, in_w, pool_w, pool_scale, s5_lam_re, s5_lam_im, s5_log_dt, s5_b_re, s5_b_im, s5_c_re, s5_c_im, s5_d, glu_w, glu_b, out_w, final_g, loss_target, m_c_ctx, m_ada_w, m_ada_b, m_norm_g, m_in_w, m_pool_w, m_pool_scale, m_s5_lam_re, m_s5_lam_im, m_s5_log_dt, m_s5_b_re, m_s5_b_im, m_s5_c_re, m_s5_c_im, m_s5_d, m_glu_w, m_glu_b, m_out_w, m_final_g, v_c_ctx, v_ada_w, v_ada_b, v_norm_g, v_in_w, v_pool_w, v_pool_scale, v_s5_lam_re, v_s5_lam_im, v_s5_log_dt, v_s5_b_re, v_s5_b_im, v_s5_c_re, v_s5_c_im, v_s5_d, v_glu_w, v_glu_b, v_out_w, v_final_g):
    xr, tgt, xc = x[0], loss_target[0], ctx[0]
    lx, d = xr.shape
    lc = xc.shape[0]
    mix = out_w.shape[1] * NDEV
    sw = glu_w.shape[1]
    pw = mix - sw
    pgw = pw // len(POOL_WINDOWS)
    ngrp = sw // SSM_H
    cb = in_w.shape[2]
    cb2 = glu_w.shape[2]
    nmod = ada_w.shape[2]
    ncc, ncl = lc // CHUNK_T, lx // CHUNK_T
    nch = ncc + ncl
    tm = min(256, lx)
    tm_acc = min(512, lx)
    assert mix == d and pw == sw and 2 * cb == pw and NDEV * cb2 == 2 * sw and lx % GRID_W == 0
    mx, my, mc = _my_pos()
    me = 4 * mx + 2 * my + mc

    (c_all,) = _all_gather([c], "gather_c")
    call = jnp.concatenate([c_all.reshape(NDEV, d), c_ctx.reshape(1, d), jnp.zeros((NDEV - 1, d), F32)], axis=0)
    ada_w_l = ada_w[0]
    ada_b_l = lax.dynamic_slice_in_dim(ada_b, me * nmod, nmod, axis=1)
    m_loc = _mod_fwd(call, ada_w_l, ada_b_l)
    (m_all,) = _all_gather([m_loc], "gather_mod")
    mod = lax.dynamic_index_in_dim(m_all, me, axis=1, keepdims=False).reshape(1, NDEV * nmod)
    mod_c = m_all[:, NDEV, :].reshape(1, NDEV * nmod)
    shift, scale, gate = mod[:, :d], mod[:, d : 2 * d], mod[:, 2 * d :]
    shift_c, scale_c = mod_c[:, :d], mod_c[:, d : 2 * d]

    s5_params = tuple(p[0] for p in (s5_lam_re, s5_lam_im, s5_log_dt, s5_b_re, s5_b_im, s5_c_re, s5_c_im))
    (small, a16), small_vjp = jax.vjp(_s5_small, *s5_params)
    gb_ops = min(8, ngrp)
    (tsum, wst, ccat), (in_wg,) = _s5_build(small, gb_ops, [in_w[0].astype(BF16)])

    ssm_blocks = (2, 3)
    (proj, hb), (glu_wg, out_wg, pool_wg) = _in_proj(
        xr, norm_g, scale, shift, in_wg, tuple(range(NDEV)), tm, "in_proj",
        shards=[glu_w[0].astype(BF16), out_w[0].astype(BF16), pool_w[0].reshape(-1, pgw).astype(BF16)],
    )
    out_wf = out_wg.reshape(mix, d)
    pool_wf = pool_wg.reshape(NDEV, len(POOL_WINDOWS), pgw // NDEV, pgw).transpose(1, 0, 2, 3).reshape(len(POOL_WINDOWS), pgw, pgw)
    (uc, hcb), _ = _in_proj(xc, norm_g, scale_c, shift_c, in_wg, ssm_blocks, min(tm, lc), "in_proj_ctx")
    lin, dmat = _pool_fwd(proj, pool_wf)

    gb_scan = min(16, ngrp)
    half = wst.shape[2] // 2
    perm = _chunk_perm()
    flat = lambda a: a.reshape(ngrp * nch, half)
    pairs = lambda arrs: [a.reshape(ngrp // 2, nch, half) for a in arrs]
    u_all, s_re, s_im = _s5_state(proj, pw // LANES, uc, perm, wst, ncc, ncl)
    h4 = pairs(_s5_scan_fwd(flat(s_re), flat(s_im), a16, ncc, nch, gb_scan))
    y_ssm = _s5_out(u_all, h4, tsum, ccat, perm, ncc, lx, gb_scan)

    ypre, y12, branch, dxo, loss_l, dfg, dgate = _merge_fwd(
        y_ssm, proj, lin, xr, tgt, glu_wg, out_wf, s5_d, glu_b, pool_scale, gate, final_g.reshape(1, d), tm
    )

    dz, dlin, dg12, dps, d_out_w = _out_bwd(dxo, gate, branch, out_wf, lin, y12, proj, pool_scale, tm)
    dypre, dglu_b, dd_skip, d_glu_w = _glu_bwd(dg12, ypre, proj, glu_wg, s5_d, tm_acc)

    d_glu_out = [d_glu_w, d_out_w.reshape(NDEV, mix // NDEV, d)]
    (dy_all, dh_re, dh_im), got_glu_out = _s5_dstate(dypre, perm, ccat, ncc, swaps=[_by_chip_and_core(a) for a in d_glu_out])
    ds4 = pairs(_s5_scan_bwd(flat(dh_re), flat(dh_im), a16, ncc, nch, gb_scan))
    du5, duc, dwst = _s5_du(u_all, dy_all, ds4, tsum, wst, perm, ncc, lx, lc, gb_scan)
    dtsum, dccat, da16 = _s5_op_grads(u_all, dy_all, ds4, h4, ncc, gb_scan)
    dup, dpool_w = _pool_bwd(dlin, pool_wf, dmat)

    npw = len(POOL_WINDOWS)
    dpool_s = dpool_w.reshape(npw, NDEV, pgw // NDEV, pgw).transpose(1, 0, 2, 3).reshape(NDEV, npw * (pgw // NDEV), pgw)
    parts_a = _pair_partials(d_glu_out, [BF16] * 2, "grads_a", got=got_glu_out)
    dpcb, dsc_c, dsh_c, dg_c = _in_bwd_ctx(xc, duc, in_wg, ssm_blocks, norm_g, scale_c, shift_c)
    grad_x, dpb, dsc, dsh, dnorm_g = _in_bwd(xr, dxo, dup, du5, dypre, dz, in_wg, norm_g, scale, shift, s5_d, dg_c, tm)
    d_in_w, (land_glu, land_out) = _in_w_grad(hb, dpb, hcb, dpcb, cb, min(2 * tm_acc, lx), parts=parts_a)

    dmod = jnp.concatenate(
        [jnp.concatenate([dsh, dsc, dgate], axis=1), jnp.concatenate([dsh_c, dsc_c, jnp.zeros((1, d), F32)], axis=1)], axis=0
    )
    (dmod_all,) = _all_gather([dmod], "gather_dmod")
    dmod_l = lax.dynamic_slice_in_dim(dmod_all, me * nmod, nmod, axis=2)
    g_ada_w, g_ada_b_l, cctx_part = _mod_bwd(call, dmod_l[:, 0, :], dmod_l[:, 1, :], ada_w_l)

    parts_b = _pair_partials([d_in_w, dpool_s], [BF16] * 2, "grads_b")
    dsmall, (land_in, land_pool) = _s5_build_bwd(small, dtsum, dwst, dccat, gb_ops, parts_b)
    ds5 = small_vjp((tuple(dsmall), da16))

    big_names = ["s5_b_re", "s5_b_im", "s5_c_re", "s5_c_im"]
    small_names = ["norm_g", "pool_scale", "s5_lam_re", "s5_lam_im", "s5_log_dt", "s5_d", "glu_b", "final_g", "c_ctx"]
    small_w = dict(norm_g=norm_g, pool_scale=pool_scale, s5_lam_re=s5_lam_re, s5_lam_im=s5_lam_im, s5_log_dt=s5_log_dt, s5_b_re=s5_b_re, s5_b_im=s5_b_im, s5_c_re=s5_c_re, s5_c_im=s5_c_im, s5_d=s5_d, glu_b=glu_b, final_g=final_g, c_ctx=c_ctx)
    small_m = dict(norm_g=m_norm_g, pool_scale=m_pool_scale, s5_lam_re=m_s5_lam_re, s5_lam_im=m_s5_lam_im, s5_log_dt=m_s5_log_dt, s5_b_re=m_s5_b_re, s5_b_im=m_s5_b_im, s5_c_re=m_s5_c_re, s5_c_im=m_s5_c_im, s5_d=m_s5_d, glu_b=m_glu_b, final_g=m_final_g, c_ctx=m_c_ctx)
    small_v = dict(norm_g=v_norm_g, pool_scale=v_pool_scale, s5_lam_re=v_s5_lam_re, s5_lam_im=v_s5_lam_im, s5_log_dt=v_s5_log_dt, s5_b_re=v_s5_b_re, s5_b_im=v_s5_b_im, s5_c_re=v_s5_c_re, s5_c_im=v_s5_c_im, s5_d=v_s5_d, glu_b=v_glu_b, final_g=v_final_g, c_ctx=v_c_ctx)
    small_g = dict(norm_g=dnorm_g, pool_scale=dps, s5_lam_re=ds5[0], s5_lam_im=ds5[1], s5_log_dt=ds5[2], s5_b_re=ds5[3], s5_b_im=ds5[4], s5_c_re=ds5[5], s5_c_im=ds5[6], s5_d=dd_skip, glu_b=dglu_b, final_g=dfg, c_ctx=cctx_part)
    shapes = [small_w[k].shape for k in small_names]
    nsmall = sum(int(np.prod(s)) for s in shapes) + 1
    unit = NDEV * 8 * LANES
    tot_s = -(-nsmall // unit) * unit
    big_n = [int(np.prod(small_w[k].shape)) for k in big_names]
    assert all(n % unit == 0 for n in big_n)
    tot = tot_s + sum(big_n)
    rows = tot // LANES
    gpack = jnp.concatenate(
        [_pack([small_g[k] for k in small_names] + [loss_l], tot_s)] + [small_g[k].reshape(-1) for k in big_names]
    ).reshape(NDEV, rows // NDEV, LANES)
    (land_small,) = _reduce_scatter([gpack], [F32], "small")
    gsum = _sum_slots(land_small, "sum_small")
    gall, g_ada_b_all = _all_gather([gsum, g_ada_b_l], "gather_small")
    gflat = gall.reshape(tot)
    loss = gflat[nsmall - 1]

    two_d = lambda a: a.reshape(-1, a.shape[-1])
    snames = small_names + ["ada_b"]
    sw = [small_w[k] for k in small_names] + [ada_b]
    sg = _unpack(gflat, shapes) + [g_ada_b_all.reshape(ada_b.shape)]
    sd, sm, sv = _adam_many(
        [two_d(a) for a in sg],
        [two_d(a) for a in sw],
        [two_d(small_m[k]) for k in small_names] + [two_d(m_ada_b)],
        [two_d(small_v[k]) for k in small_names] + [two_d(v_ada_b)],
        "adam_small",
    )
    res = {key: {k: a.reshape(w.shape) for k, a, w in zip(snames, vals, sw)} for key, vals in (("g", sg), ("d", sd), ("m", sm), ("v", sv))}

    def shard(name, g, land, w, m, v):
        shp = w.shape
        w2, m2_, v2_ = (a.reshape(-1, shp[-1]) for a in (w, m, v))
        out = _adam(g, w2, m2_, v2_, "adam_" + name, land=land)
        for k, a in zip(("g", "d", "m", "v"), out):
            res[k][name] = a.reshape(shp)

    shard("ada_w", g_ada_w, None, ada_w, m_ada_w, v_ada_w)
    shard("in_w", None, land_in, in_w, m_in_w, v_in_w)
    shard("pool_w", None, land_pool, pool_w, m_pool_w, v_pool_w)
    shard("glu_w", None, land_glu, glu_w, m_glu_w, v_glu_w)
    shard("out_w", None, land_out, out_w, m_out_w, v_out_w)
    off = tot_s
    for k, n in zip(big_names, big_n):
        shard(k, gflat[off : off + n].reshape(-1, small_w[k].shape[-1]), None, small_w[k], small_m[k], small_v[k])
        off += n

    names = ["c_ctx", "ada_w", "ada_b", "norm_g", "in_w", "pool_w", "pool_scale", "s5_lam_re", "s5_lam_im", "s5_log_dt", "s5_b_re", "s5_b_im", "s5_c_re", "s5_c_im", "s5_d", "glu_w", "glu_b", "out_w", "final_g"]
    return (loss, grad_x[None], *[res["g"][n] for n in names], *[res["d"][n] for n in names], *[res["m"][n] for n in names], *[res["v"][n] for n in names])
```

```python
import numpy as np

import jax
import jax.numpy as jnp
from jax import lax
from jax.experimental import pallas as pl
from jax.experimental.pallas import tpu as pltpu

F32 = jnp.float32
BF16 = jnp.bfloat16
NDEV = 8
EPS = 1e-6
GRID_W = 64
POOL_WINDOWS = (2, 4, 8, 16)
SSM_H = 16
CHUNK_T = 16
LANES = 128
ADAM_LR, ADAM_B1, ADAM_B2, ADAM_EPS, ADAM_WD, ADAM_STEP = 0.001, 0.9, 0.999, 1e-08, 0.01, 10
VMEM_BIG = 56 << 20
MESH_ID = pl.DeviceIdType.MESH

_HBM = pl.BlockSpec(memory_space=pltpu.HBM)
_ANY = pl.BlockSpec(memory_space=pl.ANY)
_SMEM = pl.BlockSpec(memory_space=pltpu.SMEM)


def _sds(shape, dtype=F32):
    return jax.ShapeDtypeStruct(tuple(shape), dtype)


def _cparams(ngrid=0, vmem=None):
    return pltpu.CompilerParams(
        dimension_semantics=("arbitrary",) * ngrid if ngrid else None, vmem_limit_bytes=vmem
    )


def _rows(tm, c, col=0):
    return pl.BlockSpec((tm, c), lambda i: (i, col))


def _whole(shape):
    nd = len(shape)
    return pl.BlockSpec(tuple(shape), lambda *_: (0,) * nd, pipeline_mode=pl.Buffered(1))


def _acc(shape):
    nd = len(shape)
    return pl.BlockSpec(tuple(shape), lambda *_: (0,) * nd)


def _mm(a, b):
    return jnp.dot(a, b, preferred_element_type=F32)


def _mm_nt(a, b):
    return lax.dot_general(a, b, (((1,), (1,)), ((), ())), preferred_element_type=F32)


def _mm_tn(a, b):
    return lax.dot_general(a, b, (((0,), (0,)), ((), ())), preferred_element_type=F32)


def _mm_split(k01, s):
    hi = s.astype(BF16)
    lo = (s - hi.astype(F32)).astype(BF16)
    return _mm(k01, hi) + _mm(k01, lo)


def _sigmoid(v):
    return 0.5 * (jnp.tanh(0.5 * v) + 1.0)


def _silu(v):
    return v * _sigmoid(v)


_GELU_K = 0.7978845608028654
_GELU_C = 0.044715


def _gelu(v):
    return 0.5 * v * (1.0 + jnp.tanh(_GELU_K * (v + _GELU_C * v * v * v)))


def _gelu_grad(v):
    th = jnp.tanh(_GELU_K * (v + _GELU_C * v * v * v))
    return 0.5 * (1.0 + th) + 0.5 * v * (1.0 - th * th) * (_GELU_K * (1.0 + 3.0 * _GELU_C * v * v))


def _colsum(v):
    return jnp.sum(v, axis=0, keepdims=True)


def _rowmean(v):
    return jnp.mean(v, axis=-1, keepdims=True)


NCHIP = NDEV // 2


def _my_pos():
    return lax.axis_index("x"), lax.axis_index("y"), lax.axis_index("c")


def _other_chips():
    x, y, _ = _my_pos()
    return [(1 - x, y), (x, 1 - y), (1 - x, 1 - y)]


def _remote(src, dst, send, recv, to):
    return pltpu.make_async_remote_copy(src, dst, send, recv, device_id=to, device_id_type=MESH_ID)


def _all_gather(arrs, name):
    n = len(arrs)

    def body(*refs):
        start, finish = _gather_copies(refs[:n], refs[n : 2 * n], *refs[2 * n :])
        start()
        finish()

    return pl.pallas_call(
        body, name=name, out_shape=_gather_shapes(arrs), in_specs=[_HBM] * n, out_specs=[_HBM] * n, scratch_shapes=_gather_sems(n)
    )(*arrs)


def _gather_shapes(arrs):
    return [_sds((NDEV,) + a.shape, a.dtype) for a in arrs]


def _gather_sems(n):
    return [pltpu.SemaphoreType.DMA((n, NDEV - 1)), pltpu.SemaphoreType.DMA((n, NDEV - 1)), pltpu.SemaphoreType.DMA((n,))]


def _gather_copies(ins, outs, send, recv, loc):
    n = len(ins)
    x, y, c = _my_pos()
    me, sib = (x, y, c), (x, y, 1 - c)
    chips = _other_chips()

    def slot(a, p):
        return outs[a].at[4 * p[0] + 2 * p[1] + p[2]]

    def copy(a, k, block, to, own=False):
        return _remote(ins[a] if own else slot(a, block), slot(a, block), send.at[a, k], recv.at[a, k], to)

    def mine():
        local = [pltpu.make_async_copy(ins[a], slot(a, me), loc.at[a]) for a in range(n)]
        sent = []
        for a in range(n):
            sent.append(copy(a, 0, me, sib, own=True))
            sent += [copy(a, 1 + j, me, (*chip, c), own=True) for j, chip in enumerate(chips)]
        return local, sent

    def start():
        local, sent = mine()
        for cp in local + sent:
            cp.start()

    def finish():
        local, sent = mine()
        for j, chip in enumerate(chips):
            for a in range(n):
                copy(a, 1 + j, (*chip, c), me).wait_recv()
                fwd = copy(a, 4 + j, (*chip, c), sib)
                fwd.start()
                sent.append(fwd)
        for a in range(n):
            copy(a, 0, sib, me).wait_recv()
            for j, chip in enumerate(chips):
                copy(a, 4 + j, (*chip, 1 - c), me).wait_recv()
        for cp in sent:
            cp.wait_send()
        for cp in local:
            cp.wait()

    return start, finish


def _sibling_swap(arrs, name):
    n = len(arrs)

    def body(*refs):
        start, finish = _swap_copies(refs[:n], refs[n : 2 * n], *refs[2 * n :])
        start()
        finish()

    return pl.pallas_call(
        body, name=name, out_shape=_swap_shapes(arrs), in_specs=[_HBM] * n, out_specs=[_HBM] * n, scratch_shapes=_swap_sems(n)
    )(*arrs)


def _swap_shapes(arrs):
    return [_sds((a.shape[0],) + a.shape[2:], a.dtype) for a in arrs]


def _swap_sems(n):
    return [pltpu.SemaphoreType.DMA((n,)), pltpu.SemaphoreType.DMA((n,))]


def _swap_copies(ins, outs, send, recv):
    x, y, c = _my_pos()

    def copies():
        return [_remote(ins[a].at[:, 1 - c], outs[a], send.at[a], recv.at[a], (x, y, 1 - c)) for a in range(len(ins))]

    def start():
        for cp in copies():
            cp.start()

    def finish():
        for cp in copies():
            cp.wait()

    return start, finish


def _pair_sum(arr, got, wire, name):
    _, _, r, c = arr.shape
    tr = _row_tile(r, c, budget=1 << 20)

    def body(a_ref, g_ref, o_ref):
        o_ref[0] = (a_ref[0, lax.axis_index("c")].astype(F32) + g_ref[0].astype(F32)).astype(wire)

    return pl.pallas_call(
        body,
        name=name,
        grid=(NCHIP, r // tr),
        in_specs=[pl.BlockSpec((1, 2, tr, c), lambda q, i: (q, 0, i, 0)), pl.BlockSpec((1, tr, c), lambda q, i: (q, i, 0))],
        out_specs=pl.BlockSpec((1, tr, c), lambda q, i: (q, i, 0)),
        out_shape=_sds((NCHIP, r, c), wire),
        compiler_params=_cparams(2),
    )(arr, got)


def _chip_exchange(arrs, name):
    n = len(arrs)

    def body(*refs):
        start, finish = _exchange_copies(refs[:n], refs[n : 2 * n], *refs[2 * n :])
        start()
        finish()

    return pl.pallas_call(
        body, name=name, out_shape=[_sds(a.shape, a.dtype) for a in arrs], in_specs=[_HBM] * n, out_specs=[_HBM] * n, scratch_shapes=_exchange_sems(n)
    )(*arrs)


def _exchange_sems(n):
    return [pltpu.SemaphoreType.DMA((n, NCHIP - 1)), pltpu.SemaphoreType.DMA((n, NCHIP - 1)), pltpu.SemaphoreType.DMA((n,))]


def _exchange_copies(ins, outs, send, recv, loc):
    n = len(ins)
    x, y, c = _my_pos()
    mine = 2 * x + y
    chips = _other_chips()

    def copies():
        local = [pltpu.make_async_copy(ins[a].at[mine], outs[a].at[mine], loc.at[a]) for a in range(n)]
        sent = [
            _remote(ins[a].at[2 * px + py], outs[a].at[mine], send.at[a, j], recv.at[a, j], (px, py, c))
            for a in range(n)
            for j, (px, py) in enumerate(chips)
        ]
        return local, sent

    def start():
        local, sent = copies()
        for cp in local + sent:
            cp.start()

    def finish():
        local, sent = copies()
        for a in range(n):
            for j, (px, py) in enumerate(chips):
                _remote(ins[a].at[mine], outs[a].at[2 * px + py], send.at[a, j], recv.at[a, j], (px, py, c)).wait_recv()
        for cp in sent:
            cp.wait_send()
        for cp in local:
            cp.wait()

    return start, finish


def _by_chip_and_core(a):
    return a.reshape((NCHIP, 2) + a.shape[1:])


def _pair_partials(arrs, wires, tag, got=None):
    four = [_by_chip_and_core(a) for a in arrs]
    if got is None:
        got = _sibling_swap(four, "swap_" + tag)
    return [_pair_sum(a, g, w, f"pair_sum_{tag}{i}") for i, (a, g, w) in enumerate(zip(four, got, wires))]


def _reduce_scatter(arrs, wires, tag):
    return _chip_exchange(_pair_partials(arrs, wires, tag), "exchange_" + tag)


def _row_tile(r, c, budget=1 << 20):
    width = max(c, LANES) * 4
    best = r
    for t in range(8, r, 8):
        if r % t == 0 and t * width <= budget:
            best = t
    if r * width <= budget:
        best = r
    return best


def _sum_chips(land):
    tot = land[0].astype(F32)
    for q in range(1, NCHIP):
        tot = tot + land[q].astype(F32)
    return tot


def _adam_math(w, g, m, v):
    m2 = ADAM_B1 * m + (1.0 - ADAM_B1) * g
    v2 = ADAM_B2 * v + (1.0 - ADAM_B2) * (g * g)
    mh = m2 / (1.0 - ADAM_B1**ADAM_STEP)
    vh = v2 / (1.0 - ADAM_B2**ADAM_STEP)
    delta = -ADAM_LR * (mh / (jnp.sqrt(vh) + ADAM_EPS) + ADAM_WD * w)
    return delta, m2, v2


def _adam_many(gs, ws, ms, vs, name):
    n = len(gs)

    def body(*refs):
        for t in range(n):
            d, m2, v2 = _adam_math(refs[n + t][...], refs[t][...], refs[2 * n + t][...], refs[3 * n + t][...])
            refs[4 * n + t][...] = d
            refs[5 * n + t][...] = m2
            refs[6 * n + t][...] = v2

    out = pl.pallas_call(body, name=name, out_shape=[_sds(w.shape) for w in ws] * 3, compiler_params=_cparams(0, 32 << 20))(*gs, *ws, *ms, *vs)
    return out[:n], out[n : 2 * n], out[2 * n :]


def _adam(g, w, m, v, name, land=None):
    r, c = w.shape
    tr = _row_tile(r, c, budget=1 << 20)

    def body(*refs):
        if land is not None:
            l_ref, w_ref, m_ref, v_ref, g_ref, d_ref, m2_ref, v2_ref = refs
            gv = _sum_chips(l_ref)
            g_ref[...] = gv
        else:
            g_in, w_ref, m_ref, v_ref, d_ref, m2_ref, v2_ref = refs
            gv = g_in[...]
        d, m2, v2 = _adam_math(w_ref[...], gv, m_ref[...], v_ref[...])
        d_ref[...] = d
        m2_ref[...] = m2
        v2_ref[...] = v2

    blk = _rows(tr, c)
    if land is not None:
        in_specs = [pl.BlockSpec((NCHIP, tr, c), lambda i: (0, i, 0)), blk, blk, blk]
        out = pl.pallas_call(
            body, name=name, grid=(r // tr,), in_specs=in_specs, out_specs=[blk] * 4, out_shape=[_sds((r, c))] * 4, compiler_params=_cparams(1)
        )(land, w, m, v)
        return out
    out = pl.pallas_call(
        body, name=name, grid=(r // tr,), in_specs=[blk] * 4, out_specs=[blk] * 3, out_shape=[_sds((r, c))] * 3, compiler_params=_cparams(1)
    )(g, w, m, v)
    return (g,) + tuple(out)


def _sum_slots(land, name):
    _, r, c = land.shape
    tr = _row_tile(r, c, budget=1 << 20)

    def body(l_ref, o_ref):
        o_ref[...] = _sum_chips(l_ref)

    return pl.pallas_call(
        body,
        name=name,
        grid=(r // tr,),
        in_specs=[pl.BlockSpec((NCHIP, tr, c), lambda i: (0, i, 0))],
        out_specs=_rows(tr, c),
        out_shape=_sds((r, c)),
        compiler_params=_cparams(1),
    )(land)


def _mod_fwd(call, ada_w, ada_b_loc):
    def body(c_ref, w_ref, b_ref, o_ref):
        s = _silu(c_ref[...]).astype(BF16)
        o_ref[...] = _mm(s, w_ref[...].astype(BF16)) + b_ref[...]

    return pl.pallas_call(
        body, name="mod_fwd", out_shape=_sds((call.shape[0], ada_w.shape[1])), compiler_params=_cparams(0, 32 << 20)
    )(call, ada_w, ada_b_loc)


def _mod_bwd(call, dm_loc, dmc_loc, ada_w):
    d, n = ada_w.shape
    pad = call.shape[0] - NDEV - 1

    def body(c_ref, dm_ref, dmc_ref, w_ref, gw_ref, gb_ref, cp_ref):
        cv = c_ref[...]
        sg = _sigmoid(cv)
        dmc = _colsum(dmc_ref[...])
        dm = dm_ref[...]
        rows = jnp.concatenate([dm, dmc, jnp.zeros((pad, n), F32)], axis=0)
        gw_ref[...] = _mm_tn((cv * sg).astype(BF16), rows.astype(BF16))
        gb_ref[...] = _colsum(dm) + dmc
        back = _mm_nt(rows[NDEV:].astype(BF16), w_ref[...].astype(BF16))
        c8, s8 = cv[NDEV : NDEV + 1], sg[NDEV : NDEV + 1]
        cp_ref[...] = back[0:1] * (s8 * (1.0 + c8 * (1.0 - s8)))

    return pl.pallas_call(
        body, name="mod_bwd", out_shape=[_sds((d, n)), _sds((1, n)), _sds((1, d))], compiler_params=_cparams(0, 40 << 20)
    )(call, dm_loc, dmc_loc, ada_w)


def _riding(body, nin, nout, nr, grid, copies, nscratch=0):
    def wrapped(*refs):
        base = nin + 2 * nr + nout
        start, finish = copies(refs[nin : nin + nr], refs[nin + nr + nout : base], *refs[base + nscratch :])
        first, last = True, True
        for ax, size in enumerate(grid):
            first = jnp.logical_and(first, pl.program_id(ax) == 0)
            last = jnp.logical_and(last, pl.program_id(ax) == size - 1)
        pl.when(first)(start)
        body(*refs[:nin], *refs[nin + nr : nin + nr + nout], *refs[base : base + nscratch])
        pl.when(last)(finish)

    return wrapped if nr else body


def _in_proj(xr, norm_g, scale, shift, wg, jsel, tm, name, shards=()):
    lx, d = xr.shape
    cb = wg.shape[2]
    nr = len(shards)

    def body(x_ref, g_ref, sc_ref, sh_ref, w_ref, p_ref, h_ref):
        xv = x_ref[...]
        r = lax.rsqrt(_rowmean(xv * xv) + EPS)
        hb = ((xv * r) * g_ref[...] * (1.0 + sc_ref[...]) + sh_ref[...]).astype(BF16)
        h_ref[...] = hb
        for q, j in enumerate(jsel):
            p_ref[:, q * cb : (q + 1) * cb] = _mm(hb, w_ref[j])

    vec = _whole((1, d))
    out = pl.pallas_call(
        _riding(body, 5, 2, nr, (lx // tm,), _gather_copies),
        name=name,
        grid=(lx // tm,),
        in_specs=[_rows(tm, d), vec, vec, vec, _whole(wg.shape)] + [_HBM] * nr,
        out_specs=[_rows(tm, len(jsel) * cb), _rows(tm, d)] + [_HBM] * nr,
        out_shape=[_sds((lx, len(jsel) * cb)), _sds((lx, d), BF16)] + _gather_shapes(shards),
        scratch_shapes=_gather_sems(nr) if nr else [],
        compiler_params=_cparams(1, VMEM_BIG),
    )(xr, norm_g, scale, shift, wg, *shards)
    return out[:2], out[2:]


def _in_bwd(xr, dxo, dup, du5, dypre, dz, wg, norm_g, scale, shift, d_skip, dg_init, tm):
    lx, d = xr.shape
    cb = wg.shape[2]
    pw = dup.shape[1]
    sw = du5.shape[1]
    mix = dz.shape[1]
    ncol = NDEV * cb

    def body(x_ref, dxo_ref, up_ref, d5_ref, dy_ref, dz_ref, w_ref, g_ref, sc_ref, sh_ref, dk_ref, gi_ref,
             gx_ref, dp_ref, dsc_ref, dsh_ref, dg_ref):
        i = pl.program_id(0)

        @pl.when(i == 0)
        def _():
            dsc_ref[...] = jnp.zeros_like(dsc_ref)
            dsh_ref[...] = jnp.zeros_like(dsh_ref)
            dg_ref[...] = gi_ref[...]

        dp = jnp.concatenate(
            [up_ref[...], d5_ref[...] + dk_ref[...] * dy_ref[...], dz_ref[...]], axis=1
        ).astype(BF16)
        dp_ref[...] = dp
        dh = _mm_nt(dp[:, 0:cb], w_ref[0])
        for j in range(1, NDEV):
            dh = dh + _mm_nt(dp[:, j * cb : (j + 1) * cb], w_ref[j])
        xv = x_ref[...]
        r = lax.rsqrt(_rowmean(xv * xv) + EPS)
        xh = xv * r
        g = g_ref[...]
        one_sc = 1.0 + sc_ref[...]
        dsh_ref[...] += _colsum(dh)
        dsc_ref[...] += _colsum(dh * (xh * g))
        dg_ref[...] += _colsum(dh * one_sc * xh)
        dxh = dh * one_sc * g
        gx_ref[...] = r * (dxh - xh * _rowmean(dxh * xh)) + dxo_ref[...]

    vec = _whole((1, d))
    return pl.pallas_call(
        body,
        name="in_bwd",
        grid=(lx // tm,),
        in_specs=[_rows(tm, d), _rows(tm, d), _rows(tm, pw), _rows(tm, sw), _rows(tm, sw), _rows(tm, mix), _whole(wg.shape), vec, vec, vec, _whole((1, sw)), vec],
        out_specs=[_rows(tm, d), _rows(tm, ncol), _acc((1, d)), _acc((1, d)), _acc((1, d))],
        out_shape=[_sds((lx, d)), _sds((lx, ncol), BF16), _sds((1, d)), _sds((1, d)), _sds((1, d))],
        compiler_params=_cparams(1, VMEM_BIG),
    )(xr, dxo, dup, du5, dypre, dz, wg, norm_g, scale, shift, d_skip, dg_init)


def _in_bwd_ctx(xc, duc, wg, jsel, norm_g, scale, shift):
    lc, d = xc.shape
    cb = wg.shape[2]

    def body(x_ref, du_ref, w_ref, g_ref, sc_ref, sh_ref, dp_ref, dsc_ref, dsh_ref, dg_ref):
        dp = du_ref[...].astype(BF16)
        dp_ref[...] = dp
        dh = _mm_nt(dp[:, 0:cb], w_ref[jsel[0]])
        for q in range(1, len(jsel)):
            dh = dh + _mm_nt(dp[:, q * cb : (q + 1) * cb], w_ref[jsel[q]])
        xv = x_ref[...]
        xh = xv * lax.rsqrt(_rowmean(xv * xv) + EPS)
        dsh_ref[...] = _colsum(dh)
        dsc_ref[...] = _colsum(dh * (xh * g_ref[...]))
        dg_ref[...] = _colsum(dh * (1.0 + sc_ref[...]) * xh)

    return pl.pallas_call(
        body,
        name="in_bwd_ctx",
        out_shape=[_sds(duc.shape, BF16), _sds((1, d)), _sds((1, d)), _sds((1, d))],
        compiler_params=_cparams(0, VMEM_BIG),
    )(xc, duc, wg, norm_g, scale, shift)


def _exchange_and_gather(n_parts):
    def copies(ins, outs, *sems):
        sa, fa = _exchange_copies(ins[:n_parts], outs[:n_parts], *sems[:3])
        sb, fb = _gather_copies(ins[n_parts:], outs[n_parts:], *sems[3:])

        def start():
            sa()
            sb()

        def finish():
            fa()
            fb()

        return start, finish

    return copies


def _in_w_grad(hb, dpb, hcb, dpcb, cb, tm, parts, shards):
    lx, d = hb.shape
    lc = hcb.shape[0]
    nb = lx // tm
    cg = 2 * cb
    npart = len(parts)
    nr = npart + len(shards)
    grid = (NDEV // 2, nb + 1)

    def body(h_ref, dp_ref, hc_ref, dpc_ref, o_ref, acc):
        j, i = pl.program_id(0), pl.program_id(1)

        @pl.when(i == 0)
        def _():
            acc[...] = jnp.zeros_like(acc)

        @pl.when(i < nb)
        def _():
            h = h_ref[...]
            acc[0] += _mm_tn(h, dp_ref[:, 0:cb])
            acc[1] += _mm_tn(h, dp_ref[:, cb:cg])

        @pl.when(jnp.logical_and(i == nb, j == 1))
        def _():
            h = hc_ref[...]
            acc[0] += _mm_tn(h, dpc_ref[:, 0:cb])
            acc[1] += _mm_tn(h, dpc_ref[:, cb:cg])

        @pl.when(i == nb)
        def _():
            o_ref[...] = acc[...].astype(o_ref.dtype)

    out = pl.pallas_call(
        _riding(body, 4, 1, nr, grid, _exchange_and_gather(npart), nscratch=1),
        name="in_w_grad",
        grid=grid,
        in_specs=[
            pl.BlockSpec((tm, d), lambda j, i: (jnp.minimum(i, nb - 1), 0)),
            pl.BlockSpec((tm, cg), lambda j, i: (jnp.minimum(i, nb - 1), j)),
            pl.BlockSpec((lc, d), lambda j, i: (0, 0)),
            pl.BlockSpec((lc, cg), lambda j, i: (0, 0)),
        ]
        + [_HBM] * nr,
        out_specs=[pl.BlockSpec((2, d, cb), lambda j, i: (j, 0, 0))] + [_HBM] * nr,
        out_shape=[_sds((NDEV, d, cb), BF16)] + [_sds(p.shape, p.dtype) for p in parts] + _gather_shapes(shards),
        scratch_shapes=[pltpu.VMEM((2, d, cb), F32)] + _exchange_sems(npart) + _gather_sems(len(shards)),
        compiler_params=_cparams(2, VMEM_BIG),
    )(hb, dpb, hcb, dpcb, *parts, *shards)
    return out[0], out[1 : 1 + npart], out[1 + npart :]


def _pool_tables(w, rows, rb, pgw, transpose):
    t = np.arange(GRID_W)
    lo, hi = np.clip(t - w // 2, 0, GRID_W), np.clip(t + w - w // 2, 0, GRID_W)
    band = ((t[None, :] >= lo[:, None]) & (t[None, :] < hi[:, None])).astype(np.float32)
    if transpose:
        band = band.T
    kc = np.kron(np.eye(rb, dtype=np.float32), band)
    inv_c = np.tile((1.0 / (hi - lo).astype(np.float32))[:, None], (rb, pgw)).astype(np.float32)
    r = np.arange(rows)
    cnt_r = np.clip(r + w - w // 2, 0, rows) - np.clip(r - w // 2, 0, rows)
    inv_r = (1.0 / cnt_r.astype(np.float32)).astype(np.float32)
    return jnp.asarray(kc, BF16), jnp.asarray(inv_r), jnp.asarray(inv_c)


def _pool_stack(rows, rb, pgw, transpose):
    tabs = [_pool_tables(w, rows, rb, pgw, transpose) for w in POOL_WINDOWS]
    return tuple(jnp.stack([t[i] for t in tabs]) for i in range(3))


def _window_sums(ref, r0, rb, w):
    slab = lambda off: ref[pl.ds(pl.multiple_of((r0 + off) * GRID_W, GRID_W), GRID_W), :]
    shared = range(rb - 1, w)
    common = None
    for off in shared:
        common = slab(off) if common is None else common + slab(off)
    sums = []
    for rr in range(rb):
        s = common
        for off in range(rr, rr + w):
            if off not in shared:
                s = slab(off) if s is None else s + slab(off)
        sums.append(s)
    return sums


def _per_window(group):
    for k, w in enumerate(POOL_WINDOWS):
        pl.when(pl.program_id(0) == k)(lambda k=k, w=w: group(k, w))


def _pool_fwd(proj, pool_wf):
    lx = proj.shape[0]
    ng, pgw, _ = pool_wf.shape
    rows = lx // GRID_W
    rb = min(4, rows)
    tok = rb * GRID_W
    kc, inv_r, inv_c = _pool_stack(rows, rb, pgw, False)

    def body(p_ref, w_ref, kc_ref, ir_ref, ic_ref, lin_ref, dm_ref, xp):
        def group(kk, w):
            lo = w // 2
            xp[pl.ds(0, lo * GRID_W), :] = jnp.zeros((lo * GRID_W, pgw), F32)
            xp[pl.ds((lo + rows) * GRID_W, (w - lo) * GRID_W), :] = jnp.zeros(((w - lo) * GRID_W, pgw), F32)
            xp[pl.ds(lo * GRID_W, lx), :] = p_ref[...]

            def blk(b, carry):
                r0 = b * rb
                parts = [s * ir_ref[kk, r0 + rr] for rr, s in enumerate(_window_sums(xp, r0, rb, w))]
                m = _mm_split(kc_ref[0], jnp.concatenate(parts, axis=0)) * ic_ref[0]
                u = xp[pl.ds(pl.multiple_of((r0 + lo) * GRID_W, GRID_W), tok), :]
                db = (m - u).astype(BF16)
                o0 = pl.multiple_of(r0 * GRID_W, GRID_W)
                dm_ref[pl.ds(o0, tok), :] = db
                lin_ref[pl.ds(o0, tok), :] = _mm(db, w_ref[0])
                return carry

            lax.fori_loop(0, rows // rb, blk, 0)

        _per_window(group)

    col = pl.BlockSpec((lx, pgw), lambda g: (0, g))
    return pl.pallas_call(
        body,
        name="pool_fwd",
        grid=(ng,),
        in_specs=[col, _gspec(1, pgw, pgw), _gspec(1, tok, tok), _SMEM, _gspec(1, tok, pgw)],
        out_specs=[col, col],
        out_shape=[_sds((lx, ng * pgw)), _sds((lx, ng * pgw), BF16)],
        scratch_shapes=[pltpu.VMEM(((rows + max(POOL_WINDOWS)) * GRID_W, pgw), F32)],
        compiler_params=_cparams(1, VMEM_BIG),
    )(proj, pool_wf, kc, inv_r, inv_c)


def _pool_bwd(dlin, pool_wf, dmat):
    lx = dlin.shape[0]
    ng, pgw, _ = pool_wf.shape
    rows = lx // GRID_W
    rb = min(4, rows)
    tok = rb * GRID_W
    kct, inv_r, inv_c = _pool_stack(rows, rb, pgw, True)

    def body(dl_ref, w_ref, kc_ref, ir_ref, ic_ref, dm_ref, du_ref, dw_ref, tp):
        def group(kk, w):
            front = w - w // 2 - 1
            if front:
                tp[pl.ds(0, front * GRID_W), :] = jnp.zeros((front * GRID_W, pgw), F32)
            tp[pl.ds((front + rows) * GRID_W, (w - front) * GRID_W), :] = jnp.zeros(((w - front) * GRID_W, pgw), F32)
            dw_ref[...] = jnp.zeros_like(dw_ref)

            def blk(b, carry):
                o0 = pl.multiple_of(b * tok, GRID_W)
                dlb = dl_ref[pl.ds(o0, tok), :].astype(BF16)
                dw_ref[0] += _mm_tn(dm_ref[pl.ds(o0, tok), :], dlb)
                dd = _mm_nt(dlb, w_ref[0])
                du_ref[pl.ds(o0, tok), :] = -dd
                t = _mm(kc_ref[0], (dd * ic_ref[0]).astype(BF16))
                for rr in range(rb):
                    dst = pl.multiple_of((b * rb + rr + front) * GRID_W, GRID_W)
                    tp[pl.ds(dst, GRID_W), :] = t[rr * GRID_W : (rr + 1) * GRID_W] * ir_ref[kk, b * rb + rr]
                return carry

            lax.fori_loop(0, rows // rb, blk, 0)

            def rowl(b, carry):
                for rr, s in enumerate(_window_sums(tp, b * rb, rb, w)):
                    du_ref[pl.ds(pl.multiple_of((b * rb + rr) * GRID_W, GRID_W), GRID_W), :] += s
                return carry

            lax.fori_loop(0, rows // rb, rowl, 0)

        _per_window(group)

    col = pl.BlockSpec((lx, pgw), lambda g: (0, g))
    return pl.pallas_call(
        body,
        name="pool_bwd",
        grid=(ng,),
        in_specs=[col, _gspec(1, pgw, pgw), _gspec(1, tok, tok), _SMEM, _gspec(1, tok, pgw), col],
        out_specs=[col, _gspec(1, pgw, pgw)],
        out_shape=[_sds((lx, ng * pgw)), _sds((ng, pgw, pgw))],
        scratch_shapes=[pltpu.VMEM(((rows + max(POOL_WINDOWS)) * GRID_W, pgw), F32)],
        compiler_params=_cparams(1, VMEM_BIG),
    )(dlin, pool_wf, kct, inv_r, inv_c, dmat)


def _s5_small(lam_re, lam_im, log_dt, b_re, b_im, c_re, c_im):
    t = CHUNK_T
    dt = jnp.exp(log_dt)[..., None]
    zr, zi = lam_re * dt, lam_im * dt
    tau = jnp.arange(t + 1, dtype=F32)
    mag = jnp.exp(zr[..., None] * tau)
    pr, pi = mag * jnp.cos(zi[..., None] * tau), mag * jnp.sin(zi[..., None] * tau)
    ar, ai = pr[..., 1], pi[..., 1]
    den = lam_re * lam_re + lam_im * lam_im
    qr = ((ar - 1.0) * lam_re + ai * lam_im) / den
    qi = (ai * lam_re - (ar - 1.0) * lam_im) / den
    bbr = qr[..., None] * b_re - qi[..., None] * b_im
    bbi = qr[..., None] * b_im + qi[..., None] * b_re
    ctr, cti = jnp.swapaxes(c_re, 2, 3), jnp.swapaxes(c_im, 2, 3)
    lane_pad = lambda v: jnp.pad(jnp.swapaxes(v, 0, 1), ((0, 0), (0, 0), (0, 0), (0, LANES - v.shape[-1])))
    tp = lambda v: jnp.swapaxes(v, 1, 2)
    flip = lambda v: v[..., ::-1]
    pfr, pfi = tp(flip(pr[0, ..., :t])), tp(flip(pi[0, ..., :t]))
    pbr, pbi = tp(pr[1, ..., :t]), tp(pi[1, ..., :t])
    bfr, bfi, bbr_t, bbi_t = tp(bbr[0]), tp(bbi[0]), tp(bbr[1]), tp(bbi[1])
    a1 = jnp.concatenate([pfr, pbr, pfr, pbr], axis=-1)
    a2 = jnp.concatenate([-pfi, -pbi, pfi, pbi], axis=-1)
    b1 = jnp.concatenate([bfr, bbr_t, bfi, bbi_t], axis=-1)
    b2 = jnp.concatenate([bfi, bbi_t, bfr, bbr_t], axis=-1)
    a16 = jnp.concatenate([pr[0, ..., t], pr[1, ..., t], pi[0, ..., t], pi[1, ..., t]], axis=1)
    return (lane_pad(pr), lane_pad(pi), lane_pad(bbr), lane_pad(bbi), lane_pad(ctr), lane_pad(cti), a1, a2, b1, b2), a16


def _split2(v):
    hi = v.astype(BF16)
    return hi, (v - hi.astype(F32)).astype(BF16)


def _dotx(a, b, mm=_mm):
    ah, al = (a, None) if a.dtype == BF16 else _split2(a)
    bh, bl = (b, None) if b.dtype == BF16 else _split2(b)
    out = mm(ah, bh)
    if bl is not None:
        out = out + mm(ah, bl)
    if al is not None:
        out = out + mm(al, bh)
    return out


def _shift_lanes(v, k):
    if k == 0:
        return v
    n = v.shape[-1]
    lane = lax.broadcasted_iota(jnp.int32, v.shape, v.ndim - 1)
    r = pltpu.roll(v, k % n, axis=v.ndim - 1)
    return jnp.where(lane >= k, r, 0.0) if k > 0 else jnp.where(lane < n + k, r, 0.0)


def _op_consts():
    t, h = CHUNK_T, SSM_H
    j, o = np.divmod(np.arange(t * h), h)
    p = np.arange(LANES)[:, None]
    sel = lambda col: (p == col[None, :]).astype(np.float32)
    pw = np.stack([np.concatenate([sel(j), sel(j + 1)], axis=1), np.concatenate([sel(t - 1 - j), sel(t - j)], axis=1)])
    return jnp.asarray(pw, BF16), jnp.asarray(sel(o), BF16), jnp.asarray(sel(j).T, BF16), jnp.asarray(sel(o).T, BF16)


def _op_factors(gi, pr_ref, pi_ref, br_ref, bi_ref, cr_ref, ci_ref, pw_ref, ch_ref):
    n = pr_ref.shape[2]
    c16 = _dotx(jnp.concatenate([cr_ref[gi, 0], ci_ref[gi, 0], cr_ref[gi, 1], ci_ref[gi, 1]], axis=0), ch_ref[...])
    out = []
    for d in range(2):
        e = _dotx(jnp.concatenate([pr_ref[gi, d], pi_ref[gi, d]], axis=0), pw_ref[d])
        c = c16[2 * d * n : (2 * d + 1) * n], c16[(2 * d + 1) * n : (2 * d + 2) * n]
        bst = jnp.concatenate([br_ref[gi, d], -bi_ref[gi, d]], axis=0)
        out.append(((e[:n, :CW], e[n:, :CW]), (e[:n, CW:], e[n:, CW:]), c, bst))
    return out


def _op_spread(m_ref, u_ref, v_ref, gi):
    return _dotx(m_ref[...], jnp.concatenate([_rows_pad(u_ref[gi]), _rows_pad(v_ref[gi])], axis=1))


def _cmul(x, y):
    return x[0] * y[0] - x[1] * y[1], x[0] * y[1] + x[1] * y[0]


def _cmul_conj(g, y):
    return g[0] * y[0] + g[1] * y[1], g[1] * y[0] - g[0] * y[1]


def _rows_pad(v):
    return jnp.concatenate([v, jnp.zeros((LANES - v.shape[0], v.shape[1]), F32)], axis=0)


def _s5_build(small, gb, shards):
    pr, pi, br, bi, cr, ci, a1, a2, b1, b2 = small
    g, _, n, _ = pr.shape
    t, h, ns = CHUNK_T, SSM_H, 4 * n
    pw, ch, rep_s, til_s = _op_consts()
    nr, nin, steps = len(shards), 14, g // gb

    def body(*refs):
        pr_ref, pi_ref, br_ref, bi_ref, cr_ref, ci_ref, a1_ref, a2_ref, b1_ref, b2_ref, pw_ref, ch_ref, rs_ref, ts_ref = refs[:nin]
        t_ref, w_ref, c_ref = refs[nin + nr : nin + nr + 3]
        start, finish = _gather_copies(refs[nin : nin + nr], refs[nin + nr + 3 : nin + 2 * nr + 3], *refs[nin + 2 * nr + 3 :])
        pl.when(pl.program_id(0) == 0)(start)

        def group(gi, carry):
            rows, outs = [], []
            for e16, e1, c16, bst in _op_factors(gi, pr_ref, pi_ref, br_ref, bi_ref, cr_ref, ci_ref, pw_ref, ch_ref):
                rows.append(_dotx(bst, jnp.concatenate(_cmul(c16, e16), axis=0), _mm_tn)[:h])
                outs.append(_cmul(c16, e1))
            c_ref[gi] = jnp.concatenate([outs[0][0], outs[1][0], -outs[0][1], -outs[1][1]], axis=0).astype(BF16)
            for s in range(t):
                t_ref[gi, s * h : (s + 1) * h, :] = (_shift_lanes(rows[0], h * s) + _shift_lanes(rows[1], -h * (t - 1 - s))).astype(BF16)
            ae, be = _op_spread(rs_ref, a1_ref, a2_ref, gi), _op_spread(ts_ref, b1_ref, b2_ref, gi)
            w_ref[gi] = (ae[:, :ns] * be[:, :ns] + ae[:, ns:] * be[:, ns:]).astype(BF16)
            return carry

        lax.fori_loop(0, gb, group, 0)
        pl.when(pl.program_id(0) == steps - 1)(finish)

    sm = pl.BlockSpec((gb, 2, n, LANES), lambda i: (i, 0, 0, 0))
    ab = lambda v: pl.BlockSpec((gb,) + v.shape[1:], lambda i: (i, 0, 0))
    out = pl.pallas_call(
        body,
        name="s5_build",
        grid=(steps,),
        in_specs=[sm] * 6 + [ab(a1), ab(a2), ab(b1), ab(b2), _whole(pw.shape), _whole(ch.shape), _whole(rep_s.shape), _whole(til_s.shape)] + [_HBM] * nr,
        out_specs=[_gspec(gb, CW, CW), _gspec(gb, CW, 4 * n), _gspec(gb, 4 * n, CW)] + [_HBM] * nr,
        out_shape=[_sds((g, CW, CW), BF16), _sds((g, CW, 4 * n), BF16), _sds((g, 4 * n, CW), BF16)] + _gather_shapes(shards),
        scratch_shapes=_gather_sems(nr),
        compiler_params=_cparams(1, VMEM_BIG),
    )(pr, pi, br, bi, cr, ci, a1, a2, b1, b2, pw, ch, rep_s, til_s, *shards)
    return out[:3], out[3:]


def _s5_build_bwd(small, dtsum, dwst, dccat, gb, parts):
    pr, pi, br, bi, cr, ci, a1, a2, b1, b2 = small
    g, _, n, _ = pr.shape
    t, h, ns = CHUNK_T, SSM_H, 4 * n
    pw, ch, rep_s, til_s = _op_consts()
    nr, nin, nout, steps = len(parts), 17, 10, g // gb

    def body(*refs):
        pr_ref, pi_ref, br_ref, bi_ref, cr_ref, ci_ref, a1_ref, a2_ref, b1_ref, b2_ref, pw_ref, ch_ref, rs_ref, ts_ref, dt_ref, dw_ref, dc_ref = refs[:nin]
        dpr_ref, dpi_ref, dbr_ref, dbi_ref, dcr_ref, dci_ref, da1_ref, da2_ref, db1_ref, db2_ref = refs[nin + nr : nin + nr + nout]
        start, finish = _exchange_copies(refs[nin : nin + nr], refs[nin + nr + nout : nin + 2 * nr + nout], *refs[nin + 2 * nr + nout :])
        pl.when(pl.program_id(0) == 0)(start)

        def group(gi, carry):
            drow = [jnp.zeros((h, CW), F32), jnp.zeros((h, CW), F32)]
            for s in range(t):
                blk = dt_ref[gi, s * h : (s + 1) * h, :]
                drow[0] = drow[0] + _shift_lanes(blk, -h * s)
                drow[1] = drow[1] + _shift_lanes(blk, h * (t - 1 - s))
            dcc = dc_ref[gi]
            dcs = []
            for d, (e16, e1, c16, bst) in enumerate(_op_factors(gi, pr_ref, pi_ref, br_ref, bi_ref, cr_ref, ci_ref, pw_ref, ch_ref)):
                dr = _rows_pad(drow[d])
                db = _dotx(jnp.concatenate(_cmul(c16, e16), axis=0), dr, _mm_nt)
                dbr_ref[gi, d] = db[:n]
                dbi_ref[gi, d] = -db[n:]
                dst = _dotx(bst, dr)
                dca = dst[:n], dst[n:]
                gx = dcc[d * n : (d + 1) * n], -dcc[(2 + d) * n : (3 + d) * n]
                dc16, dc16b = _cmul_conj(dca, e16), _cmul_conj(gx, e1)
                dcs += [dc16[0] + dc16b[0], dc16[1] + dc16b[1]]
                de = jnp.concatenate([jnp.concatenate(_cmul_conj(dca, c16), axis=0), jnp.concatenate(_cmul_conj(gx, c16), axis=0)], axis=1)
                dp = _dotx(de, pw_ref[d], _mm_nt)
                dpr_ref[gi, d] = dp[:n]
                dpi_ref[gi, d] = dp[n:]
            dct = _dotx(jnp.concatenate(dcs, axis=0), ch_ref[...], _mm_nt)
            for d in range(2):
                dcr_ref[gi, d] = dct[2 * d * n : (2 * d + 1) * n]
                dci_ref[gi, d] = dct[(2 * d + 1) * n : (2 * d + 2) * n]
            dw = dw_ref[gi]
            ae, be = _op_spread(rs_ref, a1_ref, a2_ref, gi), _op_spread(ts_ref, b1_ref, b2_ref, gi)
            da = _dotx(rs_ref[...], jnp.concatenate([dw * be[:, :ns], dw * be[:, ns:]], axis=1), _mm_tn)
            db = _dotx(ts_ref[...], jnp.concatenate([dw * ae[:, :ns], dw * ae[:, ns:]], axis=1), _mm_tn)
            da1_ref[gi] = da[:t, :ns]
            da2_ref[gi] = da[:t, ns:]
            db1_ref[gi] = db[:h, :ns]
            db2_ref[gi] = db[:h, ns:]
            return carry

        lax.fori_loop(0, gb, group, 0)
        pl.when(pl.program_id(0) == steps - 1)(finish)

    sm = pl.BlockSpec((gb, 2, n, LANES), lambda i: (i, 0, 0, 0))
    ab = lambda v: pl.BlockSpec((gb,) + v.shape[1:], lambda i: (i, 0, 0))
    ops = [_gspec(gb, CW, CW), _gspec(gb, CW, 4 * n), _gspec(gb, 4 * n, CW)]
    out = pl.pallas_call(
        body,
        name="s5_build_bwd",
        grid=(steps,),
        in_specs=[sm] * 6 + [ab(a1), ab(a2), ab(b1), ab(b2), _whole(pw.shape), _whole(ch.shape), _whole(rep_s.shape), _whole(til_s.shape)] + ops + [_HBM] * nr,
        out_specs=[sm] * 6 + [ab(a1), ab(a2), ab(b1), ab(b2)] + [_HBM] * nr,
        out_shape=[_sds(v.shape) for v in small] + [_sds(p.shape, p.dtype) for p in parts],
        scratch_shapes=_exchange_sems(nr),
        compiler_params=_cparams(1, VMEM_BIG),
    )(pr, pi, br, bi, cr, ci, a1, a2, b1, b2, pw, ch, rep_s, til_s, dtsum, dwst, dccat, *parts)
    return out[:nout], out[nout:]


SCAN_UNROLL = 4
GPL = LANES // SSM_H
CW = CHUNK_T * SSM_H
BW = CHUNK_T * LANES


def _chunk_perm():
    o = np.arange(BW)
    src = ((o % CW) // SSM_H) * LANES + (o // CW) * SSM_H + o % SSM_H
    return jnp.asarray(np.arange(BW)[:, None] == src[None, :], BF16)


def _gspec(gb, a, b):
    return pl.BlockSpec((gb, a, b), lambda i: (i, 0, 0))


def _chunk_rows(ref, n):
    return jnp.concatenate([ref[pl.ds(s, n, stride=CHUNK_T), :] for s in range(CHUNK_T)], axis=1)


def _s5_state(proj, col0, uc, perm, wst, ncc, ncl):
    nlb = uc.shape[1] // LANES
    g, _, ns = wst.shape
    nch = ncc + ncl
    half = ns // 2

    def body(p_ref, c_ref, pm_ref, w_ref, u_ref, sr_ref, si_ref):
        xrows = jnp.concatenate([_chunk_rows(c_ref, ncc), _chunk_rows(p_ref, ncl)], axis=0).astype(BF16)
        u = _mm(xrows, pm_ref[...]).astype(BF16)
        u_ref[0] = u
        for gi in range(GPL):
            s = _mm(u[:, gi * CW : (gi + 1) * CW], w_ref[gi].astype(BF16))
            sr_ref[gi] = s[:, :half]
            si_ref[gi] = s[:, half:]

    return pl.pallas_call(
        body,
        name="s5_state",
        grid=(nlb,),
        in_specs=[
            pl.BlockSpec((ncl * CHUNK_T, LANES), lambda i: (0, col0 + i)),
            pl.BlockSpec((ncc * CHUNK_T, LANES), lambda i: (0, i)),
            _whole(perm.shape),
            _gspec(GPL, CW, ns),
        ],
        out_specs=[pl.BlockSpec((1, nch, BW), lambda i: (i, 0, 0)), _gspec(GPL, nch, half), _gspec(GPL, nch, half)],
        out_shape=[_sds((nlb, nch, BW), BF16), _sds((g, nch, half)), _sds((g, nch, half))],
        compiler_params=_cparams(1, VMEM_BIG),
    )(proj, uc, perm, wst)


def _unrolled_loop(n, step, init):
    u = SCAN_UNROLL if n % SCAN_UNROLL == 0 else 1

    def trip(i, c):
        for k in range(u):
            c = step(i * u + k, c)
        return c

    return lax.fori_loop(0, n // u, trip, init)


def _lane_masks(gb, half):
    lane = lax.broadcasted_iota(jnp.int32, (gb, half), 1)
    return lane < (half // 2)


def _roll_half(v):
    return pltpu.roll(v, v.shape[-1] // 2, axis=v.ndim - 1)


def _pack_dirs(lo, hi, mf):
    return jnp.where(mf, lo, _roll_half(hi)), jnp.where(mf, _roll_half(lo), hi)


def _unpack_dirs(pf, pb, mf):
    return jnp.where(mf, pf, _roll_half(pb)), jnp.where(mf, _roll_half(pf), pb)


def _packed_spec(gb, nch, half, buffers=None):
    mode = {} if buffers is None else dict(pipeline_mode=pl.Buffered(buffers))
    if gb == 2 * GPL:
        return pl.BlockSpec((GPL, nch, half), lambda i: (i // 2, 0, 0), **mode)
    assert gb == GPL
    return pl.BlockSpec((gb // 2, nch, half), lambda i: (i, 0, 0), **mode)


def _pair_index(j, gb):
    if gb == 2 * GPL:
        return j, pl.program_id(0) % 2
    return j % (gb // 2), j // (gb // 2)


def _pick(which, lo, hi):
    return (hi if which else lo) if isinstance(which, int) else jnp.where(which == 0, lo, hi)


def _unpacked(pf_ref, pb_ref, j, gb, mf):
    jl, which = _pair_index(j, gb)
    return _pick(which, *_unpack_dirs(pf_ref[jl], pb_ref[jl], mf))


def _pair_tile(refs, jl, rows):
    return jnp.concatenate([r[jl, rows, :] for r in refs], axis=1)


def _pair_rows_of(op, which, q):
    z = jnp.zeros((q, op.shape[1]), op.dtype)
    blocks = [op[0:q], op[2 * q : 3 * q], op[q : 2 * q], op[3 * q : 4 * q]]
    lo = jnp.concatenate([x for b in blocks for x in (b, z)], axis=0)
    hi = jnp.concatenate([x for b in blocks for x in (z, b)], axis=0)
    return _pick(which, lo, hi)


def _group_rows_of(pair_rows, which, q):
    def take(off):
        blk = lambda k: pair_rows[2 * q * k + off : 2 * q * k + off + q]
        return jnp.concatenate([blk(0), blk(2), blk(1), blk(3)], axis=0)

    return _pick(which, take(0), take(q))


def _scan_rows(ref, row, hp, nch, mf):
    return _pack_dirs(ref[pl.ds(row, hp, stride=nch), :], ref[pl.ds(hp * nch + row, hp, stride=nch), :], mf)


def _s5_scan_fwd(s_re, s_im, a16, ncc, nch, gb):
    g, ns = a16.shape
    half = ns // 2
    hp = gb // 2

    def body(sr_ref, si_ref, a_ref, hfr_ref, hfi_ref, hbr_ref, hbi_ref):
        mf = _lane_masks(hp, half)
        afr, abr = _pack_dirs(a_ref[:hp, :half], a_ref[hp:, :half], mf)
        afi, abi = _pack_dirs(a_ref[:hp, half:], a_ref[hp:, half:], mf)
        zero = jnp.zeros((hp, half), F32)

        def step(t, c):
            fr, fi, br, bi = c
            rb = jnp.where(t < ncc, ncc - 1 - t, nch - 1 - (t - ncc))
            hfr_ref[pl.ds(t, hp, stride=nch), :] = fr
            hfi_ref[pl.ds(t, hp, stride=nch), :] = fi
            hbr_ref[pl.ds(rb, hp, stride=nch), :] = br
            hbi_ref[pl.ds(rb, hp, stride=nch), :] = bi
            sfr, sfi = _scan_rows(sr_ref, t, hp, nch, mf)[0], _scan_rows(si_ref, t, hp, nch, mf)[0]
            sbr, sbi = _scan_rows(sr_ref, rb, hp, nch, mf)[1], _scan_rows(si_ref, rb, hp, nch, mf)[1]
            return (afr * fr - afi * fi + sfr, afr * fi + afi * fr + sfi, abr * br - abi * bi + sbr, abr * bi + abi * br + sbi)

        _unrolled_loop(nch, step, (zero, zero, zero, zero))

    blk = pl.BlockSpec((gb * nch, half), lambda i: (i, 0))
    pblk = pl.BlockSpec((hp * nch, half), lambda i: (i, 0))
    return pl.pallas_call(
        body,
        name="s5_scan_fwd",
        grid=(g // gb,),
        in_specs=[blk, blk, pl.BlockSpec((gb, ns), lambda i: (i, 0))],
        out_specs=[pblk] * 4,
        out_shape=[_sds((g // 2 * nch, half))] * 4,
        compiler_params=_cparams(1, VMEM_BIG),
    )(s_re, s_im, a16)


def _token_rows_store(ref, val, n):
    for s in range(CHUNK_T):
        ref[pl.ds(s, n, stride=CHUNK_T), :] = val[:, s * LANES : (s + 1) * LANES]


def _s5_out(u_all, h4, tsum, ccat, perm, ncc, lx, gb):
    nlb, nch, _ = u_all.shape
    g, ns, _ = ccat.shape
    ncl = nch - ncc
    half = ns // 2

    def body(u_ref, hfr_ref, hfi_ref, hbr_ref, hbi_ref, t_ref, c_ref, pm_ref, y_ref):
        parts = []
        for gi in range(GPL):
            u = u_ref[0, ncc:, gi * CW : (gi + 1) * CW]
            jl, which = _pair_index(gi, gb)
            hs = _pair_tile((hfr_ref, hfi_ref, hbr_ref, hbi_ref), jl, slice(ncc, None)).astype(BF16)
            parts.append(_mm(u, t_ref[gi].astype(BF16)) + _mm(hs, _pair_rows_of(c_ref[gi].astype(BF16), which, ns // 4)))
        _token_rows_store(y_ref, _mm_nt(jnp.concatenate(parts, axis=1).astype(BF16), pm_ref[...]), ncl)

    return pl.pallas_call(
        body,
        name="s5_out",
        grid=(nlb,),
        in_specs=[pl.BlockSpec((1, nch, BW), lambda i: (i, 0, 0))] + [_packed_spec(gb, nch, half)] * 4 + [_gspec(GPL, CW, CW), _gspec(GPL, ns, CW), _whole(perm.shape)],
        out_specs=pl.BlockSpec((lx, LANES), lambda i: (0, i)),
        out_shape=_sds((lx, nlb * LANES)),
        compiler_params=_cparams(1, VMEM_BIG),
    )(u_all, *h4, tsum, ccat, perm)


def _s5_dstate(dypre, perm, ccat, ncc, swaps=()):
    lx, sw = dypre.shape
    nr = len(swaps)
    nlb = sw // LANES
    g, ns, _ = ccat.shape
    ncl = lx // CHUNK_T
    nch = ncl + ncc
    half = ns // 2

    def body(dy_ref, pm_ref, c_ref, dyr_ref, dr_ref, di_ref):
        dy = _mm(_chunk_rows(dy_ref, ncl).astype(BF16), pm_ref[...]).astype(BF16)
        dyr_ref[0] = dy
        for gi in range(GPL):
            dh = _mm_nt(dy[:, gi * CW : (gi + 1) * CW], c_ref[gi].astype(BF16))
            dr_ref[gi, :ncc, :] = jnp.zeros((ncc, half), F32)
            di_ref[gi, :ncc, :] = jnp.zeros((ncc, half), F32)
            dr_ref[gi, ncc:, :] = dh[:, :half]
            di_ref[gi, ncc:, :] = dh[:, half:]

    out = pl.pallas_call(
        _riding(body, 3, 3, nr, (nlb,), _swap_copies),
        name="s5_dstate",
        grid=(nlb,),
        in_specs=[pl.BlockSpec((lx, LANES), lambda i: (0, i)), _whole(perm.shape), _gspec(GPL, ns, CW)] + [_HBM] * nr,
        out_specs=[pl.BlockSpec((1, ncl, BW), lambda i: (i, 0, 0)), _gspec(GPL, nch, half), _gspec(GPL, nch, half)] + [_HBM] * nr,
        out_shape=[_sds((nlb, ncl, BW), BF16), _sds((g, nch, half)), _sds((g, nch, half))] + _swap_shapes(swaps),
        scratch_shapes=_swap_sems(nr) if nr else [],
        compiler_params=_cparams(1, VMEM_BIG),
    )(dypre, perm, ccat, *swaps)
    return out[:3], out[3:]


def _s5_scan_bwd(dh_re, dh_im, a16, ncc, nch, gb):
    g, ns = a16.shape
    half = ns // 2
    ncl = nch - ncc
    hp = gb // 2

    def body(dr_ref, di_ref, a_ref, sfr_ref, sfi_ref, sbr_ref, sbi_ref):
        mf = _lane_masks(hp, half)
        afr, abr = _pack_dirs(a_ref[:hp, :half], a_ref[hp:, :half], mf)
        afi, abi = _pack_dirs(a_ref[:hp, half:], a_ref[hp:, half:], mf)
        zero = jnp.zeros((hp, half), F32)

        def step(t, c):
            lfr, lfi, lbr, lbi = c
            pf = nch - 1 - t
            pb = jnp.where(t < ncl, ncc + t, t - ncl)
            sfr_ref[pl.ds(pf, hp, stride=nch), :] = lfr
            sfi_ref[pl.ds(pf, hp, stride=nch), :] = lfi
            sbr_ref[pl.ds(pb, hp, stride=nch), :] = lbr
            sbi_ref[pl.ds(pb, hp, stride=nch), :] = lbi
            dfr, dfi = _scan_rows(dr_ref, pf, hp, nch, mf)[0], _scan_rows(di_ref, pf, hp, nch, mf)[0]
            dbr, dbi = _scan_rows(dr_ref, pb, hp, nch, mf)[1], _scan_rows(di_ref, pb, hp, nch, mf)[1]
            return (dfr + afr * lfr + afi * lfi, dfi + afr * lfi - afi * lfr, dbr + abr * lbr + abi * lbi, dbi + abr * lbi - abi * lbr)

        _unrolled_loop(nch, step, (zero,) * 4)

    blk = pl.BlockSpec((gb * nch, half), lambda i: (i, 0))
    pblk = pl.BlockSpec((hp * nch, half), lambda i: (i, 0))
    return pl.pallas_call(
        body,
        name="s5_scan_bwd",
        grid=(g // gb,),
        in_specs=[blk, blk, pl.BlockSpec((gb, ns), lambda i: (i, 0))],
        out_specs=[pblk] * 4,
        out_shape=[_sds((g // 2 * nch, half))] * 4,
        compiler_params=_cparams(1, VMEM_BIG),
    )(dh_re, dh_im, a16)


def _s5_du(u_all, dy_all, ds4, tsum, wst, perm, ncc, lx, lc, gb):
    nlb, ncl, _ = dy_all.shape
    g, _, ns = wst.shape
    nch = ncc + ncl
    half = ns // 2

    def body(u_ref, dy_ref, sfr_ref, sfi_ref, sbr_ref, sbi_ref, t_ref, w_ref, pm_ref, du_ref, dc_ref, dw_ref):
        parts = []
        mf = _lane_masks(nch, half)
        for gi in range(GPL):
            ds = jnp.concatenate([_unpacked(sfr_ref, sbr_ref, gi, gb, mf), _unpacked(sfi_ref, sbi_ref, gi, gb, mf)], axis=1).astype(BF16)
            dw_ref[gi] = _mm_tn(u_ref[0, :, gi * CW : (gi + 1) * CW], ds)
            d_all = _mm_nt(ds, w_ref[gi].astype(BF16))
            d_lat = d_all[ncc:] + _mm_nt(dy_ref[0, :, gi * CW : (gi + 1) * CW], t_ref[gi].astype(BF16))
            parts.append(jnp.concatenate([d_all[:ncc], d_lat], axis=0))
        du = _mm_nt(jnp.concatenate(parts, axis=1).astype(BF16), pm_ref[...])
        _token_rows_store(dc_ref, du[:ncc], ncc)
        _token_rows_store(du_ref, du[ncc:], ncl)

    return pl.pallas_call(
        body,
        name="s5_du",
        grid=(nlb,),
        in_specs=[pl.BlockSpec((1, nch, BW), lambda i: (i, 0, 0)), pl.BlockSpec((1, ncl, BW), lambda i: (i, 0, 0))]
        + [_packed_spec(gb, nch, half)] * 4
        + [_gspec(GPL, CW, CW), _gspec(GPL, CW, ns), _whole(perm.shape)],
        out_specs=[pl.BlockSpec((lx, LANES), lambda i: (0, i)), pl.BlockSpec((lc, LANES), lambda i: (0, i)), _gspec(GPL, CW, ns)],
        out_shape=[_sds((lx, nlb * LANES)), _sds((lc, nlb * LANES)), _sds((g, CW, ns))],
        compiler_params=_cparams(1, VMEM_BIG),
    )(u_all, dy_all, *ds4, tsum, wst, perm)


def _s5_op_grads(u_all, dy_all, ds4, h4, ncc, gb):
    nlb, nch, _ = u_all.shape
    half = ds4[0].shape[2]
    g = 2 * ds4[0].shape[0]
    ns = 2 * half

    def body(u_ref, dy_ref, sfr_ref, sfi_ref, sbr_ref, sbi_ref, hfr_ref, hfi_ref, hbr_ref, hbi_ref, dt_ref, dc_ref, da_ref):
        das = []
        mf1 = _lane_masks(1, half)
        for gi in range(GPL):
            cols = slice(gi * CW, (gi + 1) * CW)
            jl, which = _pair_index(gi, gb)
            dy = dy_ref[0, :, cols]
            dt_ref[gi] = _mm_tn(u_ref[0, ncc:, cols], dy)
            hs = _pair_tile((hfr_ref, hfi_ref, hbr_ref, hbi_ref), jl, slice(ncc, None)).astype(BF16)
            dc_ref[gi] = _group_rows_of(_mm_tn(hs, dy), which, ns // 4)
            hfr, hfi, hbr, hbi = hfr_ref[jl], hfi_ref[jl], hbr_ref[jl], hbi_ref[jl]
            sfr, sfi, sbr, sbi = sfr_ref[jl], sfi_ref[jl], sbr_ref[jl], sbi_ref[jl]
            a_re = _pick(which, *_unpack_dirs(_colsum(hfr * sfr + hfi * sfi), _colsum(hbr * sbr + hbi * sbi), mf1))
            a_im = _pick(which, *_unpack_dirs(_colsum(hfr * sfi - hfi * sfr), _colsum(hbr * sbi - hbi * sbr), mf1))
            das.append(jnp.concatenate([a_re, a_im], axis=1))
        da_ref[...] = jnp.concatenate(das, axis=0)

    return pl.pallas_call(
        body,
        name="s5_op_grads",
        grid=(nlb,),
        in_specs=[pl.BlockSpec((1, nch, BW), lambda i: (i, 0, 0)), pl.BlockSpec((1, nch - ncc, BW), lambda i: (i, 0, 0))]
        + [_packed_spec(gb, nch, half)] * 4
        + [_packed_spec(gb, nch, half, buffers=1)] * 4,
        out_specs=[_gspec(GPL, CW, CW), _gspec(GPL, ns, CW), pl.BlockSpec((GPL, ns), lambda i: (i, 0))],
        out_shape=[_sds((g, CW, CW)), _sds((g, ns, CW)), _sds((g, ns))],
        compiler_params=_cparams(1, VMEM_BIG),
    )(u_all, dy_all, *ds4, *h4)


def _merge_fwd(y_ssm, proj, lin, x, tgt, glu_wg, out_w, d_skip, glu_b, pscale, gate, fin_g, tm):
    lx, d = x.shape
    sw = y_ssm.shape[1]
    mix = out_w.shape[0]
    pw = mix - sw
    cb2 = glu_wg.shape[2]
    nh = NDEV // 2

    def body(y_ref, u_ref, z_ref, l_ref, x_ref, t_ref, gw_ref, ow_ref, dk_ref, gb_ref, ps_ref, gt_ref, fg_ref,
             ypre_ref, y12_ref, br_ref, dxo_ref, loss_ref, dfg_ref, dgt_ref):
        @pl.when(pl.program_id(0) == 0)
        def _():
            loss_ref[...] = jnp.zeros_like(loss_ref)
            dfg_ref[...] = jnp.zeros_like(dfg_ref)
            dgt_ref[...] = jnp.zeros_like(dgt_ref)

        ypre = y_ref[...] + dk_ref[...] * u_ref[...]
        ypre_ref[...] = ypre
        yg = _gelu(ypre).astype(BF16)
        outs = []
        for j in range(nh):
            y1 = _mm(yg, gw_ref[j]) + gb_ref[:, j * cb2 : (j + 1) * cb2]
            y2 = _mm(yg, gw_ref[nh + j]) + gb_ref[:, (nh + j) * cb2 : (nh + j + 1) * cb2]
            y12_ref[:, j * cb2 : (j + 1) * cb2] = y1
            y12_ref[:, (nh + j) * cb2 : (nh + j + 1) * cb2] = y2
            outs.append(y1 * _sigmoid(y2))
        lin = l_ref[...] * ps_ref[...]
        brb = (jnp.concatenate([lin] + outs, axis=1) * _silu(z_ref[...])).astype(BF16)
        br_ref[...] = brb
        mixv = _mm(brb, ow_ref[...])
        xo = x_ref[...] + gt_ref[...] * mixv
        r2 = lax.rsqrt(_rowmean(xo * xo) + EPS)
        xh = xo * r2
        fg = fg_ref[...]
        e = xh * fg - t_ref[...]
        loss_ref[...] += 0.5 * _colsum(_rowmean(e * e))
        dy = e * (1.0 / d)
        dfg_ref[...] += _colsum(dy * xh)
        gy = dy * fg
        dxo = r2 * (gy - xh * _rowmean(gy * xh))
        dxo_ref[...] = dxo
        dgt_ref[...] += _colsum(dxo * mixv)

    vec = _whole((1, d))
    return pl.pallas_call(
        body,
        name="merge_fwd",
        grid=(lx // tm,),
        in_specs=[_rows(tm, sw), _rows(tm, sw, 1), _rows(tm, mix, 1), _rows(tm, pw), _rows(tm, d), _rows(tm, d), _whole(glu_wg.shape), _whole(out_w.shape), _whole((1, sw)), _whole((1, 2 * sw)), _whole((1, pw)), vec, vec],
        out_specs=[_rows(tm, sw), _rows(tm, 2 * sw), _rows(tm, mix), _rows(tm, d), _acc((1, 1)), _acc((1, d)), _acc((1, d))],
        out_shape=[_sds((lx, sw)), _sds((lx, 2 * sw)), _sds((lx, mix), BF16), _sds((lx, d)), _sds((1, 1)), _sds((1, d)), _sds((1, d))],
        compiler_params=_cparams(1, VMEM_BIG),
    )(y_ssm, proj, proj, lin, x, tgt, glu_wg, out_w, d_skip, glu_b, pscale, gate, fin_g)


def _out_bwd(dxo, gate, branch, out_w, lin, y12, proj, pscale, tm):
    lx, d = dxo.shape
    mix = out_w.shape[0]
    pw = lin.shape[1]
    sw = mix - pw
    nb = lx // tm

    def body(dxo_ref, gt_ref, br_ref, ow_ref, l_ref, y_ref, z_ref, ps_ref, dz_ref, dl_ref, dg_ref, dps_ref, dow_hbm, acc, sem):
        i = pl.program_id(0)

        @pl.when(i == 0)
        def _():
            acc[...] = jnp.zeros_like(acc)
            dps_ref[...] = jnp.zeros_like(dps_ref)

        dmix = (dxo_ref[...] * gt_ref[...]).astype(BF16)
        dbr = _mm_nt(dmix, ow_ref[...])
        acc[...] += _mm_tn(br_ref[...], dmix)
        z = z_ref[...]
        sg = _sigmoid(z)
        dbp = dbr * (z * sg)
        y1, y2 = y_ref[:, :sw], y_ref[:, sw:]
        s2 = _sigmoid(y2)
        ps = ps_ref[...]
        lin = l_ref[...]
        bp = jnp.concatenate([lin * ps, y1 * s2], axis=1)
        dz_ref[...] = dbr * bp * (sg * (1.0 + z * (1.0 - sg)))
        dlp = dbp[:, :pw]
        dl_ref[...] = dlp * ps
        dps_ref[...] += _colsum(dlp * lin)
        dss = dbp[:, pw:]
        dg_ref[...] = jnp.concatenate([dss * s2, dss * y1 * s2 * (1.0 - s2)], axis=1).astype(BF16)

        @pl.when(i == nb - 1)
        def _():
            cp = pltpu.make_async_copy(acc, dow_hbm, sem)
            cp.start()
            cp.wait()

    return pl.pallas_call(
        body,
        name="out_bwd",
        grid=(nb,),
        in_specs=[_rows(tm, d), _whole((1, d)), _rows(tm, mix), _whole(out_w.shape), _rows(tm, pw), _rows(tm, 2 * sw), _rows(tm, mix, 1), _whole((1, pw))],
        out_specs=[_rows(tm, mix), _rows(tm, pw), _rows(tm, 2 * sw), _acc((1, pw)), _ANY],
        out_shape=[_sds((lx, mix)), _sds((lx, pw)), _sds((lx, 2 * sw), BF16), _sds((1, pw)), _sds((mix, d))],
        scratch_shapes=[pltpu.VMEM((mix, d), F32), pltpu.SemaphoreType.DMA],
        compiler_params=_cparams(1, VMEM_BIG),
    )(dxo, gate, branch, out_w, lin, y12, proj, pscale)


def _glu_bwd(dg12, ypre, proj, glu_wg, d_skip, tm):
    lx, sw = ypre.shape
    cb2 = glu_wg.shape[2]
    nb = lx // tm

    def body(dg_ref, yp_ref, u_ref, gw_ref, dk_ref, dyp_ref, dgb_ref, dd_ref, dgw_hbm, acc, sem):
        i = pl.program_id(0)

        @pl.when(i == 0)
        def _():
            acc[...] = jnp.zeros_like(acc)
            dgb_ref[...] = jnp.zeros_like(dgb_ref)
            dd_ref[...] = jnp.zeros_like(dd_ref)

        ypre = yp_ref[...]
        ygb = _gelu(ypre).astype(BF16)
        dg = dg_ref[...]
        dyg = jnp.zeros((tm, sw), F32)
        for j in range(NDEV):
            dgj = dg[:, j * cb2 : (j + 1) * cb2]
            dyg = dyg + _mm_nt(dgj, gw_ref[j])
            acc[j] += _mm_tn(ygb, dgj)
        dgb_ref[...] += _colsum(dg.astype(F32))
        dyp = dyg * _gelu_grad(ypre)
        dyp_ref[...] = dyp
        dd_ref[...] += _colsum(dyp * u_ref[...])

        @pl.when(i == nb - 1)
        def _():
            cp = pltpu.make_async_copy(acc, dgw_hbm, sem)
            cp.start()
            cp.wait()

    return pl.pallas_call(
        body,
        name="glu_bwd",
        grid=(nb,),
        in_specs=[_rows(tm, 2 * sw), _rows(tm, sw), _rows(tm, sw, 1), _whole(glu_wg.shape), _whole((1, sw))],
        out_specs=[_rows(tm, sw), _acc((1, 2 * sw)), _acc((1, sw)), _ANY],
        out_shape=[_sds((lx, sw)), _sds((1, 2 * sw)), _sds((1, sw)), _sds(glu_wg.shape)],
        scratch_shapes=[pltpu.VMEM(glu_wg.shape, F32), pltpu.SemaphoreType.DMA],
        compiler_params=_cparams(1, VMEM_BIG),
    )(dg12, ypre, proj, glu_wg, d_skip)


def _pack(parts, total):
    flat = jnp.concatenate([p.reshape(-1).astype(F32) for p in parts])
    return jnp.pad(flat, (0, total - flat.shape[0]))


def _unpack(flat, shapes):
    out, off = [], 0
    for s in shapes:
        n = int(np.prod(s)) if len(s) else 1
        out.append(flat[off : off + n].reshape(s))
        off += n
    return out


def kernel(x, c, ctx, c_ctx, ada_w, ada_b, norm_g, in_w, pool_w, pool_scale, s5_lam_re, s5_lam_im, s5_log_dt, s5_b_re, s5_b_im, s5_c_re, s5_c_im, s5_d, glu_w, glu_b, out_w, final_g, loss_target, m_c_ctx, m_ada_w, m_ada_b, m_norm_g, m_in_w, m_pool_w, m_pool_scale, m_s5_lam_re, m_s5_lam_im, m_s5_log_dt, m_s5_b_re, m_s5_b_im, m_s5_c_re, m_s5_c_im, m_s5_d, m_glu_w, m_glu_b, m_out_w, m_final_g, v_c_ctx, v_ada_w, v_ada_b, v_norm_g, v_in_w, v_pool_w, v_pool_scale, v_s5_lam_re, v_s5_lam_im, v_s5_log_dt, v_s5_b_re, v_s5_b_im, v_s5_c_re, v_s5_c_im, v_s5_d, v_glu_w, v_glu_b, v_out_w, v_final_g):
    xr, tgt, xc = x[0], loss_target[0], ctx[0]
    lx, d = xr.shape
    lc = xc.shape[0]
    mix = out_w.shape[1] * NDEV
    sw = glu_w.shape[1]
    pw = mix - sw
    pgw = pw // len(POOL_WINDOWS)
    ngrp = sw // SSM_H
    cb = in_w.shape[2]
    cb2 = glu_w.shape[2]
    nmod = ada_w.shape[2]
    ncc, ncl = lc // CHUNK_T, lx // CHUNK_T
    nch = ncc + ncl
    tm = min(256, lx)
    tm_acc = min(512, lx)
    assert mix == d and pw == sw and 2 * cb == pw and NDEV * cb2 == 2 * sw and lx % GRID_W == 0
    mx, my, mc = _my_pos()
    me = 4 * mx + 2 * my + mc

    (c_all,) = _all_gather([c], "gather_c")
    call = jnp.concatenate([c_all.reshape(NDEV, d), c_ctx.reshape(1, d), jnp.zeros((NDEV - 1, d), F32)], axis=0)
    ada_w_l = ada_w[0]
    ada_b_l = lax.dynamic_slice_in_dim(ada_b, me * nmod, nmod, axis=1)
    m_loc = _mod_fwd(call, ada_w_l, ada_b_l)
    (m_all,) = _all_gather([m_loc], "gather_mod")
    mod = lax.dynamic_index_in_dim(m_all, me, axis=1, keepdims=False).reshape(1, NDEV * nmod)
    mod_c = m_all[:, NDEV, :].reshape(1, NDEV * nmod)
    shift, scale, gate = mod[:, :d], mod[:, d : 2 * d], mod[:, 2 * d :]
    shift_c, scale_c = mod_c[:, :d], mod_c[:, d : 2 * d]

    s5_params = tuple(p[0] for p in (s5_lam_re, s5_lam_im, s5_log_dt, s5_b_re, s5_b_im, s5_c_re, s5_c_im))
    (small, a16), small_vjp = jax.vjp(_s5_small, *s5_params)
    gb_ops = min(8, ngrp)
    (tsum, wst, ccat), (in_wg,) = _s5_build(small, gb_ops, [in_w[0].astype(BF16)])

    ssm_blocks = (2, 3)
    (proj, hb), (glu_wg, out_wg, pool_wg) = _in_proj(
        xr, norm_g, scale, shift, in_wg, tuple(range(NDEV)), tm, "in_proj",
        shards=[glu_w[0].astype(BF16), out_w[0].astype(BF16), pool_w[0].reshape(-1, pgw).astype(BF16)],
    )
    out_wf = out_wg.reshape(mix, d)
    pool_wf = pool_wg.reshape(NDEV, len(POOL_WINDOWS), pgw // NDEV, pgw).transpose(1, 0, 2, 3).reshape(len(POOL_WINDOWS), pgw, pgw)
    (uc, hcb), _ = _in_proj(xc, norm_g, scale_c, shift_c, in_wg, ssm_blocks, min(tm, lc), "in_proj_ctx")
    lin, dmat = _pool_fwd(proj, pool_wf)

    gb_scan = min(16, ngrp)
    half = wst.shape[2] // 2
    perm = _chunk_perm()
    flat = lambda a: a.reshape(ngrp * nch, half)
    pairs = lambda arrs: [a.reshape(ngrp // 2, nch, half) for a in arrs]
    u_all, s_re, s_im = _s5_state(proj, pw // LANES, uc, perm, wst, ncc, ncl)
    h4 = pairs(_s5_scan_fwd(flat(s_re), flat(s_im), a16, ncc, nch, gb_scan))
    y_ssm = _s5_out(u_all, h4, tsum, ccat, perm, ncc, lx, gb_scan)

    ypre, y12, branch, dxo, loss_l, dfg, dgate = _merge_fwd(
        y_ssm, proj, lin, xr, tgt, glu_wg, out_wf, s5_d, glu_b, pool_scale, gate, final_g.reshape(1, d), tm
    )

    dz, dlin, dg12, dps, d_out_w = _out_bwd(dxo, gate, branch, out_wf, lin, y12, proj, pool_scale, tm)
    dypre, dglu_b, dd_skip, d_glu_w = _glu_bwd(dg12, ypre, proj, glu_wg, s5_d, tm_acc)

    d_glu_out = [d_glu_w, d_out_w.reshape(NDEV, mix // NDEV, d)]
    (dy_all, dh_re, dh_im), got_glu_out = _s5_dstate(dypre, perm, ccat, ncc, swaps=[_by_chip_and_core(a) for a in d_glu_out])
    ds4 = pairs(_s5_scan_bwd(flat(dh_re), flat(dh_im), a16, ncc, nch, gb_scan))
    du5, duc, dwst = _s5_du(u_all, dy_all, ds4, tsum, wst, perm, ncc, lx, lc, gb_scan)
    dtsum, dccat, da16 = _s5_op_grads(u_all, dy_all, ds4, h4, ncc, gb_scan)
    dup, dpool_w = _pool_bwd(dlin, pool_wf, dmat)

    npw = len(POOL_WINDOWS)
    dpool_s = dpool_w.reshape(npw, NDEV, pgw // NDEV, pgw).transpose(1, 0, 2, 3).reshape(NDEV, npw * (pgw // NDEV), pgw)
    parts_a = _pair_partials(d_glu_out, [BF16] * 2, "grads_a", got=got_glu_out)
    dpcb, dsc_c, dsh_c, dg_c = _in_bwd_ctx(xc, duc, in_wg, ssm_blocks, norm_g, scale_c, shift_c)
    grad_x, dpb, dsc, dsh, dnorm_g = _in_bwd(xr, dxo, dup, du5, dypre, dz, in_wg, norm_g, scale, shift, s5_d, dg_c, tm)
    dmod = jnp.concatenate(
        [jnp.concatenate([dsh, dsc, dgate], axis=1), jnp.concatenate([dsh_c, dsc_c, jnp.zeros((1, d), F32)], axis=1)], axis=0
    )
    d_in_w, (land_glu, land_out), (dmod_all,) = _in_w_grad(hb, dpb, hcb, dpcb, cb, min(2 * tm_acc, lx), parts_a, [dmod])

    dmod_l = lax.dynamic_slice_in_dim(dmod_all, me * nmod, nmod, axis=2)
    g_ada_w, g_ada_b_l, cctx_part = _mod_bwd(call, dmod_l[:, 0, :], dmod_l[:, 1, :], ada_w_l)

    parts_b = _pair_partials([d_in_w, dpool_s], [BF16] * 2, "grads_b")
    dsmall, (land_in, land_pool) = _s5_build_bwd(small, dtsum, dwst, dccat, gb_ops, parts_b)
    ds5 = small_vjp((tuple(dsmall), da16))

    big_names = ["s5_b_re", "s5_b_im", "s5_c_re", "s5_c_im"]
    small_names = ["norm_g", "pool_scale", "s5_lam_re", "s5_lam_im", "s5_log_dt", "s5_d", "glu_b", "final_g", "c_ctx"]
    small_w = dict(norm_g=norm_g, pool_scale=pool_scale, s5_lam_re=s5_lam_re, s5_lam_im=s5_lam_im, s5_log_dt=s5_log_dt, s5_b_re=s5_b_re, s5_b_im=s5_b_im, s5_c_re=s5_c_re, s5_c_im=s5_c_im, s5_d=s5_d, glu_b=glu_b, final_g=final_g, c_ctx=c_ctx)
    small_m = dict(norm_g=m_norm_g, pool_scale=m_pool_scale, s5_lam_re=m_s5_lam_re, s5_lam_im=m_s5_lam_im, s5_log_dt=m_s5_log_dt, s5_b_re=m_s5_b_re, s5_b_im=m_s5_b_im, s5_c_re=m_s5_c_re, s5_c_im=m_s5_c_im, s5_d=m_s5_d, glu_b=m_glu_b, final_g=m_final_g, c_ctx=m_c_ctx)
    small_v = dict(norm_g=v_norm_g, pool_scale=v_pool_scale, s5_lam_re=v_s5_lam_re, s5_lam_im=v_s5_lam_im, s5_log_dt=v_s5_log_dt, s5_b_re=v_s5_b_re, s5_b_im=v_s5_b_im, s5_c_re=v_s5_c_re, s5_c_im=v_s5_c_im, s5_d=v_s5_d, glu_b=v_glu_b, final_g=v_final_g, c_ctx=v_c_ctx)
    small_g = dict(norm_g=dnorm_g, pool_scale=dps, s5_lam_re=ds5[0], s5_lam_im=ds5[1], s5_log_dt=ds5[2], s5_b_re=ds5[3], s5_b_im=ds5[4], s5_c_re=ds5[5], s5_c_im=ds5[6], s5_d=dd_skip, glu_b=dglu_b, final_g=dfg, c_ctx=cctx_part)
    shapes = [small_w[k].shape for k in small_names]
    nsmall = sum(int(np.prod(s)) for s in shapes) + 1
    unit = NDEV * 8 * LANES
    tot_s = -(-nsmall // unit) * unit
    big_n = [int(np.prod(small_w[k].shape)) for k in big_names]
    assert all(n % unit == 0 for n in big_n)
    tot = tot_s + sum(big_n)
    rows = tot // LANES
    gpack = jnp.concatenate(
        [_pack([small_g[k] for k in small_names] + [loss_l], tot_s)] + [small_g[k].reshape(-1) for k in big_names]
    ).reshape(NDEV, rows // NDEV, LANES)
    (land_small,) = _reduce_scatter([gpack], [F32], "small")
    gsum = _sum_slots(land_small, "sum_small")
    gall, g_ada_b_all = _all_gather([gsum, g_ada_b_l], "gather_small")
    gflat = gall.reshape(tot)
    loss = gflat[nsmall - 1]

    two_d = lambda a: a.reshape(-1, a.shape[-1])
    snames = small_names + ["ada_b"]
    sw = [small_w[k] for k in small_names] + [ada_b]
    sg = _unpack(gflat, shapes) + [g_ada_b_all.reshape(ada_b.shape)]
    sd, sm, sv = _adam_many(
        [two_d(a) for a in sg],
        [two_d(a) for a in sw],
        [two_d(small_m[k]) for k in small_names] + [two_d(m_ada_b)],
        [two_d(small_v[k]) for k in small_names] + [two_d(v_ada_b)],
        "adam_small",
    )
    res = {key: {k: a.reshape(w.shape) for k, a, w in zip(snames, vals, sw)} for key, vals in (("g", sg), ("d", sd), ("m", sm), ("v", sv))}

    def shard(name, g, land, w, m, v):
        shp = w.shape
        w2, m2_, v2_ = (a.reshape(-1, shp[-1]) for a in (w, m, v))
        out = _adam(g, w2, m2_, v2_, "adam_" + name, land=land)
        for k, a in zip(("g", "d", "m", "v"), out):
            res[k][name] = a.reshape(shp)

    shard("ada_w", g_ada_w, None, ada_w, m_ada_w, v_ada_w)
    shard("in_w", None, land_in, in_w, m_in_w, v_in_w)
    shard("pool_w", None, land_pool, pool_w, m_pool_w, v_pool_w)
    shard("glu_w", None, land_glu, glu_w, m_glu_w, v_glu_w)
    shard("out_w", None, land_out, out_w, m_out_w, v_out_w)
    off = tot_s
    for k, n in zip(big_names, big_n):
        shard(k, gflat[off : off + n].reshape(-1, small_w[k].shape[-1]), None, small_w[k], small_m[k], small_v[k])
        off += n

    names = ["c_ctx", "ada_w", "ada_b", "norm_g", "in_w", "pool_w", "pool_scale", "s5_lam_re", "s5_lam_im", "s5_log_dt", "s5_b_re", "s5_b_im", "s5_c_re", "s5_c_im", "s5_d", "glu_w", "glu_b", "out_w", "final_g"]
    return (loss, grad_x[None], *[res["g"][n] for n in names], *[res["d"][n] for n in names], *[res["m"][n] for n in names], *[res["v"][n] for n in names])
```

```python
import numpy as np

import jax
import jax.numpy as jnp
from jax import lax
from jax.experimental import pallas as pl
from jax.experimental.pallas import tpu as pltpu

F32 = jnp.float32
BF16 = jnp.bfloat16
NDEV = 8
EPS = 1e-6
GRID_W = 64
POOL_WINDOWS = (2, 4, 8, 16)
SSM_H = 16
CHUNK_T = 16
LANES = 128
ADAM_LR, ADAM_B1, ADAM_B2, ADAM_EPS, ADAM_WD, ADAM_STEP = 0.001, 0.9, 0.999, 1e-08, 0.01, 10
VMEM_BIG = 56 << 20
MESH_ID = pl.DeviceIdType.MESH

_HBM = pl.BlockSpec(memory_space=pltpu.HBM)
_ANY = pl.BlockSpec(memory_space=pl.ANY)
_SMEM = pl.BlockSpec(memory_space=pltpu.SMEM)


def _sds(shape, dtype=F32):
    return jax.ShapeDtypeStruct(tuple(shape), dtype)


def _cparams(ngrid=0, vmem=None):
    return pltpu.CompilerParams(
        dimension_semantics=("arbitrary",) * ngrid if ngrid else None, vmem_limit_bytes=vmem
    )


def _rows(tm, c, col=0):
    return pl.BlockSpec((tm, c), lambda i: (i, col))


def _whole(shape):
    nd = len(shape)
    return pl.BlockSpec(tuple(shape), lambda *_: (0,) * nd, pipeline_mode=pl.Buffered(1))


def _acc(shape):
    nd = len(shape)
    return pl.BlockSpec(tuple(shape), lambda *_: (0,) * nd)


def _mm(a, b):
    return jnp.dot(a, b, preferred_element_type=F32)


def _mm_nt(a, b):
    return lax.dot_general(a, b, (((1,), (1,)), ((), ())), preferred_element_type=F32)


def _mm_tn(a, b):
    return lax.dot_general(a, b, (((0,), (0,)), ((), ())), preferred_element_type=F32)


def _mm_split(k01, s):
    hi = s.astype(BF16)
    lo = (s - hi.astype(F32)).astype(BF16)
    return _mm(k01, hi) + _mm(k01, lo)


def _sigmoid(v):
    return 0.5 * (jnp.tanh(0.5 * v) + 1.0)


def _silu(v):
    return v * _sigmoid(v)


_GELU_K = 0.7978845608028654
_GELU_C = 0.044715


def _gelu(v):
    return 0.5 * v * (1.0 + jnp.tanh(_GELU_K * (v + _GELU_C * v * v * v)))


def _gelu_grad(v):
    th = jnp.tanh(_GELU_K * (v + _GELU_C * v * v * v))
    return 0.5 * (1.0 + th) + 0.5 * v * (1.0 - th * th) * (_GELU_K * (1.0 + 3.0 * _GELU_C * v * v))


def _colsum(v):
    return jnp.sum(v, axis=0, keepdims=True)


def _rowmean(v):
    return jnp.mean(v, axis=-1, keepdims=True)


NCHIP = NDEV // 2


def _my_pos():
    return lax.axis_index("x"), lax.axis_index("y"), lax.axis_index("c")


def _other_chips():
    x, y, _ = _my_pos()
    return [(1 - x, y), (x, 1 - y), (1 - x, 1 - y)]


def _remote(src, dst, send, recv, to):
    return pltpu.make_async_remote_copy(src, dst, send, recv, device_id=to, device_id_type=MESH_ID)


def _all_gather(arrs, name):
    n = len(arrs)

    def body(*refs):
        start, finish = _gather_copies(refs[:n], refs[n : 2 * n], *refs[2 * n :])
        start()
        finish()

    return pl.pallas_call(
        body, name=name, out_shape=_gather_shapes(arrs), in_specs=[_HBM] * n, out_specs=[_HBM] * n, scratch_shapes=_gather_sems(n)
    )(*arrs)


def _gather_shapes(arrs):
    return [_sds((NDEV,) + a.shape, a.dtype) for a in arrs]


def _gather_sems(n):
    return [pltpu.SemaphoreType.DMA((n, NDEV - 1)), pltpu.SemaphoreType.DMA((n, NDEV - 1)), pltpu.SemaphoreType.DMA((n,))]


def _gather_copies(ins, outs, send, recv, loc):
    n = len(ins)
    x, y, c = _my_pos()
    me, sib = (x, y, c), (x, y, 1 - c)
    chips = _other_chips()

    def slot(a, p):
        return outs[a].at[4 * p[0] + 2 * p[1] + p[2]]

    def copy(a, k, block, to, own=False):
        return _remote(ins[a] if own else slot(a, block), slot(a, block), send.at[a, k], recv.at[a, k], to)

    def mine():
        local = [pltpu.make_async_copy(ins[a], slot(a, me), loc.at[a]) for a in range(n)]
        sent = []
        for a in range(n):
            sent.append(copy(a, 0, me, sib, own=True))
            sent += [copy(a, 1 + j, me, (*chip, c), own=True) for j, chip in enumerate(chips)]
        return local, sent

    def start():
        local, sent = mine()
        for cp in local + sent:
            cp.start()

    def finish():
        local, sent = mine()
        for j, chip in enumerate(chips):
            for a in range(n):
                copy(a, 1 + j, (*chip, c), me).wait_recv()
                fwd = copy(a, 4 + j, (*chip, c), sib)
                fwd.start()
                sent.append(fwd)
        for a in range(n):
            copy(a, 0, sib, me).wait_recv()
            for j, chip in enumerate(chips):
                copy(a, 4 + j, (*chip, 1 - c), me).wait_recv()
        for cp in sent:
            cp.wait_send()
        for cp in local:
            cp.wait()

    return start, finish


def _sibling_swap(arrs, name):
    n = len(arrs)

    def body(*refs):
        start, finish = _swap_copies(refs[:n], refs[n : 2 * n], *refs[2 * n :])
        start()
        finish()

    return pl.pallas_call(
        body, name=name, out_shape=_swap_shapes(arrs), in_specs=[_HBM] * n, out_specs=[_HBM] * n, scratch_shapes=_swap_sems(n)
    )(*arrs)


def _swap_shapes(arrs):
    return [_sds((a.shape[0],) + a.shape[2:], a.dtype) for a in arrs]


def _swap_sems(n):
    return [pltpu.SemaphoreType.DMA((n,)), pltpu.SemaphoreType.DMA((n,))]


def _swap_copies(ins, outs, send, recv):
    x, y, c = _my_pos()

    def copies():
        return [_remote(ins[a].at[:, 1 - c], outs[a], send.at[a], recv.at[a], (x, y, 1 - c)) for a in range(len(ins))]

    def start():
        for cp in copies():
            cp.start()

    def finish():
        for cp in copies():
            cp.wait()

    return start, finish


def _pair_sum(arr, got, wire, name):
    _, _, r, c = arr.shape
    tr = _row_tile(r, c, budget=1 << 20)

    def body(a_ref, g_ref, o_ref):
        o_ref[0] = (a_ref[0, lax.axis_index("c")].astype(F32) + g_ref[0].astype(F32)).astype(wire)

    return pl.pallas_call(
        body,
        name=name,
        grid=(NCHIP, r // tr),
        in_specs=[pl.BlockSpec((1, 2, tr, c), lambda q, i: (q, 0, i, 0)), pl.BlockSpec((1, tr, c), lambda q, i: (q, i, 0))],
        out_specs=pl.BlockSpec((1, tr, c), lambda q, i: (q, i, 0)),
        out_shape=_sds((NCHIP, r, c), wire),
        compiler_params=_cparams(2),
    )(arr, got)


def _chip_exchange(arrs, name):
    n = len(arrs)

    def body(*refs):
        start, finish = _exchange_copies(refs[:n], refs[n : 2 * n], *refs[2 * n :])
        start()
        finish()

    return pl.pallas_call(
        body, name=name, out_shape=[_sds(a.shape, a.dtype) for a in arrs], in_specs=[_HBM] * n, out_specs=[_HBM] * n, scratch_shapes=_exchange_sems(n)
    )(*arrs)


def _exchange_sems(n):
    return [pltpu.SemaphoreType.DMA((n, NCHIP - 1)), pltpu.SemaphoreType.DMA((n, NCHIP - 1)), pltpu.SemaphoreType.DMA((n,))]


def _exchange_copies(ins, outs, send, recv, loc):
    n = len(ins)
    x, y, c = _my_pos()
    mine = 2 * x + y
    chips = _other_chips()

    def copies():
        local = [pltpu.make_async_copy(ins[a].at[mine], outs[a].at[mine], loc.at[a]) for a in range(n)]
        sent = [
            _remote(ins[a].at[2 * px + py], outs[a].at[mine], send.at[a, j], recv.at[a, j], (px, py, c))
            for a in range(n)
            for j, (px, py) in enumerate(chips)
        ]
        return local, sent

    def start():
        local, sent = copies()
        for cp in local + sent:
            cp.start()

    def finish():
        local, sent = copies()
        for a in range(n):
            for j, (px, py) in enumerate(chips):
                _remote(ins[a].at[mine], outs[a].at[2 * px + py], send.at[a, j], recv.at[a, j], (px, py, c)).wait_recv()
        for cp in sent:
            cp.wait_send()
        for cp in local:
            cp.wait()

    return start, finish


def _by_chip_and_core(a):
    return a.reshape((NCHIP, 2) + a.shape[1:])


def _pair_partials(arrs, wires, tag, got=None):
    four = [_by_chip_and_core(a) for a in arrs]
    if got is None:
        got = _sibling_swap(four, "swap_" + tag)
    return [_pair_sum(a, g, w, f"pair_sum_{tag}{i}") for i, (a, g, w) in enumerate(zip(four, got, wires))]


def _reduce_scatter(arrs, wires, tag):
    return _chip_exchange(_pair_partials(arrs, wires, tag), "exchange_" + tag)


def _row_tile(r, c, budget=1 << 20):
    width = max(c, LANES) * 4
    best = r
    for t in range(8, r, 8):
        if r % t == 0 and t * width <= budget:
            best = t
    if r * width <= budget:
        best = r
    return best


def _sum_chips(land):
    tot = land[0].astype(F32)
    for q in range(1, NCHIP):
        tot = tot + land[q].astype(F32)
    return tot


def _adam_math(w, g, m, v):
    m2 = ADAM_B1 * m + (1.0 - ADAM_B1) * g
    v2 = ADAM_B2 * v + (1.0 - ADAM_B2) * (g * g)
    mh = m2 / (1.0 - ADAM_B1**ADAM_STEP)
    vh = v2 / (1.0 - ADAM_B2**ADAM_STEP)
    delta = -ADAM_LR * (mh / (jnp.sqrt(vh) + ADAM_EPS) + ADAM_WD * w)
    return delta, m2, v2


def _adam_many(gs, ws, ms, vs, name):
    n = len(gs)

    def body(*refs):
        for t in range(n):
            d, m2, v2 = _adam_math(refs[n + t][...], refs[t][...], refs[2 * n + t][...], refs[3 * n + t][...])
            refs[4 * n + t][...] = d
            refs[5 * n + t][...] = m2
            refs[6 * n + t][...] = v2

    out = pl.pallas_call(body, name=name, out_shape=[_sds(w.shape) for w in ws] * 3, compiler_params=_cparams(0, 32 << 20))(*gs, *ws, *ms, *vs)
    return out[:n], out[n : 2 * n], out[2 * n :]


def _adam(g, w, m, v, name, land=None):
    r, c = w.shape
    tr = _row_tile(r, c, budget=1 << 20)

    def body(*refs):
        if land is not None:
            l_ref, w_ref, m_ref, v_ref, g_ref, d_ref, m2_ref, v2_ref = refs
            gv = _sum_chips(l_ref)
            g_ref[...] = gv
        else:
            g_in, w_ref, m_ref, v_ref, d_ref, m2_ref, v2_ref = refs
            gv = g_in[...]
        d, m2, v2 = _adam_math(w_ref[...], gv, m_ref[...], v_ref[...])
        d_ref[...] = d
        m2_ref[...] = m2
        v2_ref[...] = v2

    blk = _rows(tr, c)
    if land is not None:
        in_specs = [pl.BlockSpec((NCHIP, tr, c), lambda i: (0, i, 0)), blk, blk, blk]
        out = pl.pallas_call(
            body, name=name, grid=(r // tr,), in_specs=in_specs, out_specs=[blk] * 4, out_shape=[_sds((r, c))] * 4, compiler_params=_cparams(1)
        )(land, w, m, v)
        return out
    out = pl.pallas_call(
        body, name=name, grid=(r // tr,), in_specs=[blk] * 4, out_specs=[blk] * 3, out_shape=[_sds((r, c))] * 3, compiler_params=_cparams(1)
    )(g, w, m, v)
    return (g,) + tuple(out)


def _sum_slots(land, name):
    _, r, c = land.shape
    tr = _row_tile(r, c, budget=1 << 20)

    def body(l_ref, o_ref):
        o_ref[...] = _sum_chips(l_ref)

    return pl.pallas_call(
        body,
        name=name,
        grid=(r // tr,),
        in_specs=[pl.BlockSpec((NCHIP, tr, c), lambda i: (0, i, 0))],
        out_specs=_rows(tr, c),
        out_shape=_sds((r, c)),
        compiler_params=_cparams(1),
    )(land)


def _mod_fwd(call, ada_w, ada_b_loc):
    def body(c_ref, w_ref, b_ref, o_ref):
        s = _silu(c_ref[...]).astype(BF16)
        o_ref[...] = _mm(s, w_ref[...].astype(BF16)) + b_ref[...]

    return pl.pallas_call(
        body, name="mod_fwd", out_shape=_sds((call.shape[0], ada_w.shape[1])), compiler_params=_cparams(0, 32 << 20)
    )(call, ada_w, ada_b_loc)


def _mod_bwd(call, dm_loc, dmc_loc, ada_w):
    d, n = ada_w.shape
    pad = call.shape[0] - NDEV - 1

    def body(c_ref, dm_ref, dmc_ref, w_ref, gw_ref, gb_ref, cp_ref):
        cv = c_ref[...]
        sg = _sigmoid(cv)
        dmc = _colsum(dmc_ref[...])
        dm = dm_ref[...]
        rows = jnp.concatenate([dm, dmc, jnp.zeros((pad, n), F32)], axis=0)
        gw_ref[...] = _mm_tn((cv * sg).astype(BF16), rows.astype(BF16))
        gb_ref[...] = _colsum(dm) + dmc
        back = _mm_nt(rows[NDEV:].astype(BF16), w_ref[...].astype(BF16))
        c8, s8 = cv[NDEV : NDEV + 1], sg[NDEV : NDEV + 1]
        cp_ref[...] = back[0:1] * (s8 * (1.0 + c8 * (1.0 - s8)))

    return pl.pallas_call(
        body, name="mod_bwd", out_shape=[_sds((d, n)), _sds((1, n)), _sds((1, d))], compiler_params=_cparams(0, 40 << 20)
    )(call, dm_loc, dmc_loc, ada_w)


def _riding(body, nin, nout, nr, grid, copies, nscratch=0):
    def wrapped(*refs):
        base = nin + 2 * nr + nout
        start, finish = copies(refs[nin : nin + nr], refs[nin + nr + nout : base], *refs[base + nscratch :])
        first, last = True, True
        for ax, size in enumerate(grid):
            first = jnp.logical_and(first, pl.program_id(ax) == 0)
            last = jnp.logical_and(last, pl.program_id(ax) == size - 1)
        pl.when(first)(start)
        body(*refs[:nin], *refs[nin + nr : nin + nr + nout], *refs[base : base + nscratch])
        pl.when(last)(finish)

    return wrapped if nr else body


def _in_proj(xr, norm_g, scale, shift, wg, jsel, tm, name, shards=()):
    lx, d = xr.shape
    cb = wg.shape[2]
    nr = len(shards)

    def body(x_ref, g_ref, sc_ref, sh_ref, w_ref, p_ref, h_ref):
        xv = x_ref[...]
        r = lax.rsqrt(_rowmean(xv * xv) + EPS)
        hb = ((xv * r) * g_ref[...] * (1.0 + sc_ref[...]) + sh_ref[...]).astype(BF16)
        h_ref[...] = hb
        for q, j in enumerate(jsel):
            p_ref[:, q * cb : (q + 1) * cb] = _mm(hb, w_ref[j])

    vec = _whole((1, d))
    out = pl.pallas_call(
        _riding(body, 5, 2, nr, (lx // tm,), _gather_copies),
        name=name,
        grid=(lx // tm,),
        in_specs=[_rows(tm, d), vec, vec, vec, _whole(wg.shape)] + [_HBM] * nr,
        out_specs=[_rows(tm, len(jsel) * cb), _rows(tm, d)] + [_HBM] * nr,
        out_shape=[_sds((lx, len(jsel) * cb)), _sds((lx, d), BF16)] + _gather_shapes(shards),
        scratch_shapes=_gather_sems(nr) if nr else [],
        compiler_params=_cparams(1, VMEM_BIG),
    )(xr, norm_g, scale, shift, wg, *shards)
    return out[:2], out[2:]


def _in_bwd(xr, dxo, dup, du5, dypre, dz, wg, norm_g, scale, shift, d_skip, dg_init, tm):
    lx, d = xr.shape
    cb = wg.shape[2]
    pw = dup.shape[1]
    sw = du5.shape[1]
    mix = dz.shape[1]
    ncol = NDEV * cb

    def body(x_ref, dxo_ref, up_ref, d5_ref, dy_ref, dz_ref, w_ref, g_ref, sc_ref, sh_ref, dk_ref, gi_ref,
             gx_ref, dp_ref, dsc_ref, dsh_ref, dg_ref):
        i = pl.program_id(0)

        @pl.when(i == 0)
        def _():
            dsc_ref[...] = jnp.zeros_like(dsc_ref)
            dsh_ref[...] = jnp.zeros_like(dsh_ref)
            dg_ref[...] = gi_ref[...]

        dp = jnp.concatenate(
            [up_ref[...], d5_ref[...] + dk_ref[...] * dy_ref[...], dz_ref[...]], axis=1
        ).astype(BF16)
        dp_ref[...] = dp
        dh = _mm_nt(dp[:, 0:cb], w_ref[0])
        for j in range(1, NDEV):
            dh = dh + _mm_nt(dp[:, j * cb : (j + 1) * cb], w_ref[j])
        xv = x_ref[...]
        r = lax.rsqrt(_rowmean(xv * xv) + EPS)
        xh = xv * r
        g = g_ref[...]
        one_sc = 1.0 + sc_ref[...]
        dsh_ref[...] += _colsum(dh)
        dsc_ref[...] += _colsum(dh * (xh * g))
        dg_ref[...] += _colsum(dh * one_sc * xh)
        dxh = dh * one_sc * g
        gx_ref[...] = r * (dxh - xh * _rowmean(dxh * xh)) + dxo_ref[...]

    vec = _whole((1, d))
    return pl.pallas_call(
        body,
        name="in_bwd",
        grid=(lx // tm,),
        in_specs=[_rows(tm, d), _rows(tm, d), _rows(tm, pw), _rows(tm, sw), _rows(tm, sw), _rows(tm, mix), _whole(wg.shape), vec, vec, vec, _whole((1, sw)), vec],
        out_specs=[_rows(tm, d), _rows(tm, ncol), _acc((1, d)), _acc((1, d)), _acc((1, d))],
        out_shape=[_sds((lx, d)), _sds((lx, ncol), BF16), _sds((1, d)), _sds((1, d)), _sds((1, d))],
        compiler_params=_cparams(1, VMEM_BIG),
    )(xr, dxo, dup, du5, dypre, dz, wg, norm_g, scale, shift, d_skip, dg_init)


def _in_bwd_ctx(xc, duc, wg, jsel, norm_g, scale, shift):
    lc, d = xc.shape
    cb = wg.shape[2]

    def body(x_ref, du_ref, w_ref, g_ref, sc_ref, sh_ref, dp_ref, dsc_ref, dsh_ref, dg_ref):
        dp = du_ref[...].astype(BF16)
        dp_ref[...] = dp
        dh = _mm_nt(dp[:, 0:cb], w_ref[jsel[0]])
        for q in range(1, len(jsel)):
            dh = dh + _mm_nt(dp[:, q * cb : (q + 1) * cb], w_ref[jsel[q]])
        xv = x_ref[...]
        xh = xv * lax.rsqrt(_rowmean(xv * xv) + EPS)
        dsh_ref[...] = _colsum(dh)
        dsc_ref[...] = _colsum(dh * (xh * g_ref[...]))
        dg_ref[...] = _colsum(dh * (1.0 + sc_ref[...]) * xh)

    return pl.pallas_call(
        body,
        name="in_bwd_ctx",
        out_shape=[_sds(duc.shape, BF16), _sds((1, d)), _sds((1, d)), _sds((1, d))],
        compiler_params=_cparams(0, VMEM_BIG),
    )(xc, duc, wg, norm_g, scale, shift)


def _exchange_and_gather(n_parts):
    def copies(ins, outs, *sems):
        sa, fa = _exchange_copies(ins[:n_parts], outs[:n_parts], *sems[:3])
        sb, fb = _gather_copies(ins[n_parts:], outs[n_parts:], *sems[3:])

        def start():
            sa()
            sb()

        def finish():
            fa()
            fb()

        return start, finish

    return copies


def _in_w_grad(hb, dpb, hcb, dpcb, cb, tm, parts, shards):
    lx, d = hb.shape
    lc = hcb.shape[0]
    nb = lx // tm
    cg = 2 * cb
    npart = len(parts)
    nr = npart + len(shards)
    grid = (NDEV // 2, nb + 1)

    def body(h_ref, dp_ref, hc_ref, dpc_ref, o_ref, acc):
        j, i = pl.program_id(0), pl.program_id(1)

        @pl.when(i == 0)
        def _():
            acc[...] = jnp.zeros_like(acc)

        @pl.when(i < nb)
        def _():
            h = h_ref[...]
            acc[0] += _mm_tn(h, dp_ref[:, 0:cb])
            acc[1] += _mm_tn(h, dp_ref[:, cb:cg])

        @pl.when(jnp.logical_and(i == nb, j == 1))
        def _():
            h = hc_ref[...]
            acc[0] += _mm_tn(h, dpc_ref[:, 0:cb])
            acc[1] += _mm_tn(h, dpc_ref[:, cb:cg])

        @pl.when(i == nb)
        def _():
            o_ref[...] = acc[...].astype(o_ref.dtype)

    out = pl.pallas_call(
        _riding(body, 4, 1, nr, grid, _exchange_and_gather(npart), nscratch=1),
        name="in_w_grad",
        grid=grid,
        in_specs=[
            pl.BlockSpec((tm, d), lambda j, i: (jnp.minimum(i, nb - 1), 0)),
            pl.BlockSpec((tm, cg), lambda j, i: (jnp.minimum(i, nb - 1), j)),
            pl.BlockSpec((lc, d), lambda j, i: (0, 0)),
            pl.BlockSpec((lc, cg), lambda j, i: (0, 0)),
        ]
        + [_HBM] * nr,
        out_specs=[pl.BlockSpec((2, d, cb), lambda j, i: (j, 0, 0))] + [_HBM] * nr,
        out_shape=[_sds((NDEV, d, cb), BF16)] + [_sds(p.shape, p.dtype) for p in parts] + _gather_shapes(shards),
        scratch_shapes=[pltpu.VMEM((2, d, cb), F32)] + _exchange_sems(npart) + _gather_sems(len(shards)),
        compiler_params=_cparams(2, VMEM_BIG),
    )(hb, dpb, hcb, dpcb, *parts, *shards)
    return out[0], out[1 : 1 + npart], out[1 + npart :]


def _pool_tables(w, rows, rb, pgw, transpose):
    t = np.arange(GRID_W)
    lo, hi = np.clip(t - w // 2, 0, GRID_W), np.clip(t + w - w // 2, 0, GRID_W)
    band = ((t[None, :] >= lo[:, None]) & (t[None, :] < hi[:, None])).astype(np.float32)
    if transpose:
        band = band.T
    kc = np.kron(np.eye(rb, dtype=np.float32), band)
    inv_c = np.tile((1.0 / (hi - lo).astype(np.float32))[:, None], (rb, pgw)).astype(np.float32)
    r = np.arange(rows)
    cnt_r = np.clip(r + w - w // 2, 0, rows) - np.clip(r - w // 2, 0, rows)
    inv_r = (1.0 / cnt_r.astype(np.float32)).astype(np.float32)
    return jnp.asarray(kc, BF16), jnp.asarray(inv_r), jnp.asarray(inv_c)


def _pool_stack(rows, rb, pgw, transpose):
    tabs = [_pool_tables(w, rows, rb, pgw, transpose) for w in POOL_WINDOWS]
    return tuple(jnp.stack([t[i] for t in tabs]) for i in range(3))


def _window_sums(ref, r0, rb, w):
    slab = lambda off: ref[pl.ds(pl.multiple_of((r0 + off) * GRID_W, GRID_W), GRID_W), :]
    shared = range(rb - 1, w)
    common = None
    for off in shared:
        common = slab(off) if common is None else common + slab(off)
    sums = []
    for rr in range(rb):
        s = common
        for off in range(rr, rr + w):
            if off not in shared:
                s = slab(off) if s is None else s + slab(off)
        sums.append(s)
    return sums


def _per_window(group):
    for k, w in enumerate(POOL_WINDOWS):
        pl.when(pl.program_id(0) == k)(lambda k=k, w=w: group(k, w))


def _pool_fwd(proj, pool_wf):
    lx = proj.shape[0]
    ng, pgw, _ = pool_wf.shape
    rows = lx // GRID_W
    rb = min(4, rows)
    tok = rb * GRID_W
    kc, inv_r, inv_c = _pool_stack(rows, rb, pgw, False)

    def body(p_ref, w_ref, kc_ref, ir_ref, ic_ref, lin_ref, dm_ref, xp):
        def group(kk, w):
            lo = w // 2
            xp[pl.ds(0, lo * GRID_W), :] = jnp.zeros((lo * GRID_W, pgw), F32)
            xp[pl.ds((lo + rows) * GRID_W, (w - lo) * GRID_W), :] = jnp.zeros(((w - lo) * GRID_W, pgw), F32)
            xp[pl.ds(lo * GRID_W, lx), :] = p_ref[...]

            def blk(b, carry):
                r0 = b * rb
                parts = [s * ir_ref[kk, r0 + rr] for rr, s in enumerate(_window_sums(xp, r0, rb, w))]
                m = _mm_split(kc_ref[0], jnp.concatenate(parts, axis=0)) * ic_ref[0]
                u = xp[pl.ds(pl.multiple_of((r0 + lo) * GRID_W, GRID_W), tok), :]
                db = (m - u).astype(BF16)
                o0 = pl.multiple_of(r0 * GRID_W, GRID_W)
                dm_ref[pl.ds(o0, tok), :] = db
                lin_ref[pl.ds(o0, tok), :] = _mm(db, w_ref[0])
                return carry

            lax.fori_loop(0, rows // rb, blk, 0)

        _per_window(group)

    col = pl.BlockSpec((lx, pgw), lambda g: (0, g))
    return pl.pallas_call(
        body,
        name="pool_fwd",
        grid=(ng,),
        in_specs=[col, _gspec(1, pgw, pgw), _gspec(1, tok, tok), _SMEM, _gspec(1, tok, pgw)],
        out_specs=[col, col],
        out_shape=[_sds((lx, ng * pgw)), _sds((lx, ng * pgw), BF16)],
        scratch_shapes=[pltpu.VMEM(((rows + max(POOL_WINDOWS)) * GRID_W, pgw), F32)],
        compiler_params=_cparams(1, VMEM_BIG),
    )(proj, pool_wf, kc, inv_r, inv_c)


def _pool_bwd(dlin, pool_wf, dmat):
    lx = dlin.shape[0]
    ng, pgw, _ = pool_wf.shape
    rows = lx // GRID_W
    rb = min(4, rows)
    tok = rb * GRID_W
    kct, inv_r, inv_c = _pool_stack(rows, rb, pgw, True)

    def body(dl_ref, w_ref, kc_ref, ir_ref, ic_ref, dm_ref, du_ref, dw_ref, tp):
        def group(kk, w):
            front = w - w // 2 - 1
            if front:
                tp[pl.ds(0, front * GRID_W), :] = jnp.zeros((front * GRID_W, pgw), F32)
            tp[pl.ds((front + rows) * GRID_W, (w - front) * GRID_W), :] = jnp.zeros(((w - front) * GRID_W, pgw), F32)
            dw_ref[...] = jnp.zeros_like(dw_ref)

            def blk(b, carry):
                o0 = pl.multiple_of(b * tok, GRID_W)
                dlb = dl_ref[pl.ds(o0, tok), :].astype(BF16)
                dw_ref[0] += _mm_tn(dm_ref[pl.ds(o0, tok), :], dlb)
                dd = _mm_nt(dlb, w_ref[0])
                du_ref[pl.ds(o0, tok), :] = -dd
                t = _mm(kc_ref[0], (dd * ic_ref[0]).astype(BF16))
                for rr in range(rb):
                    dst = pl.multiple_of((b * rb + rr + front) * GRID_W, GRID_W)
                    tp[pl.ds(dst, GRID_W), :] = t[rr * GRID_W : (rr + 1) * GRID_W] * ir_ref[kk, b * rb + rr]
                return carry

            lax.fori_loop(0, rows // rb, blk, 0)

            def rowl(b, carry):
                for rr, s in enumerate(_window_sums(tp, b * rb, rb, w)):
                    du_ref[pl.ds(pl.multiple_of((b * rb + rr) * GRID_W, GRID_W), GRID_W), :] += s
                return carry

            lax.fori_loop(0, rows // rb, rowl, 0)

        _per_window(group)

    col = pl.BlockSpec((lx, pgw), lambda g: (0, g))
    return pl.pallas_call(
        body,
        name="pool_bwd",
        grid=(ng,),
        in_specs=[col, _gspec(1, pgw, pgw), _gspec(1, tok, tok), _SMEM, _gspec(1, tok, pgw), col],
        out_specs=[col, _gspec(1, pgw, pgw)],
        out_shape=[_sds((lx, ng * pgw)), _sds((ng, pgw, pgw))],
        scratch_shapes=[pltpu.VMEM(((rows + max(POOL_WINDOWS)) * GRID_W, pgw), F32)],
        compiler_params=_cparams(1, VMEM_BIG),
    )(dlin, pool_wf, kct, inv_r, inv_c, dmat)


def _s5_small(lam_re, lam_im, log_dt, b_re, b_im, c_re, c_im):
    t = CHUNK_T
    dt = jnp.exp(log_dt)[..., None]
    zr, zi = lam_re * dt, lam_im * dt
    tau = jnp.arange(t + 1, dtype=F32)
    mag = jnp.exp(zr[..., None] * tau)
    pr, pi = mag * jnp.cos(zi[..., None] * tau), mag * jnp.sin(zi[..., None] * tau)
    ar, ai = pr[..., 1], pi[..., 1]
    den = lam_re * lam_re + lam_im * lam_im
    qr = ((ar - 1.0) * lam_re + ai * lam_im) / den
    qi = (ai * lam_re - (ar - 1.0) * lam_im) / den
    bbr = qr[..., None] * b_re - qi[..., None] * b_im
    bbi = qr[..., None] * b_im + qi[..., None] * b_re
    ctr, cti = jnp.swapaxes(c_re, 2, 3), jnp.swapaxes(c_im, 2, 3)
    lane_pad = lambda v: jnp.pad(jnp.swapaxes(v, 0, 1), ((0, 0), (0, 0), (0, 0), (0, LANES - v.shape[-1])))
    tp = lambda v: jnp.swapaxes(v, 1, 2)
    flip = lambda v: v[..., ::-1]
    pfr, pfi = tp(flip(pr[0, ..., :t])), tp(flip(pi[0, ..., :t]))
    pbr, pbi = tp(pr[1, ..., :t]), tp(pi[1, ..., :t])
    bfr, bfi, bbr_t, bbi_t = tp(bbr[0]), tp(bbi[0]), tp(bbr[1]), tp(bbi[1])
    a1 = jnp.concatenate([pfr, pbr, pfr, pbr], axis=-1)
    a2 = jnp.concatenate([-pfi, -pbi, pfi, pbi], axis=-1)
    b1 = jnp.concatenate([bfr, bbr_t, bfi, bbi_t], axis=-1)
    b2 = jnp.concatenate([bfi, bbi_t, bfr, bbr_t], axis=-1)
    a16 = jnp.concatenate([pr[0, ..., t], pr[1, ..., t], pi[0, ..., t], pi[1, ..., t]], axis=1)
    return (lane_pad(pr), lane_pad(pi), lane_pad(bbr), lane_pad(bbi), lane_pad(ctr), lane_pad(cti), a1, a2, b1, b2), a16


def _split2(v):
    hi = v.astype(BF16)
    return hi, (v - hi.astype(F32)).astype(BF16)


def _dotx(a, b, mm=_mm):
    ah, al = (a, None) if a.dtype == BF16 else _split2(a)
    bh, bl = (b, None) if b.dtype == BF16 else _split2(b)
    out = mm(ah, bh)
    if bl is not None:
        out = out + mm(ah, bl)
    if al is not None:
        out = out + mm(al, bh)
    return out


def _shift_lanes(v, k):
    if k == 0:
        return v
    n = v.shape[-1]
    lane = lax.broadcasted_iota(jnp.int32, v.shape, v.ndim - 1)
    r = pltpu.roll(v, k % n, axis=v.ndim - 1)
    return jnp.where(lane >= k, r, 0.0) if k > 0 else jnp.where(lane < n + k, r, 0.0)


def _op_consts():
    t, h = CHUNK_T, SSM_H
    j, o = np.divmod(np.arange(t * h), h)
    p = np.arange(LANES)[:, None]
    sel = lambda col: (p == col[None, :]).astype(np.float32)
    pw = np.stack([np.concatenate([sel(j), sel(j + 1)], axis=1), np.concatenate([sel(t - 1 - j), sel(t - j)], axis=1)])
    return jnp.asarray(pw, BF16), jnp.asarray(sel(o), BF16), jnp.asarray(sel(j).T, BF16), jnp.asarray(sel(o).T, BF16)


def _op_factors(gi, pr_ref, pi_ref, br_ref, bi_ref, cr_ref, ci_ref, pw_ref, ch_ref):
    n = pr_ref.shape[2]
    c16 = _dotx(jnp.concatenate([cr_ref[gi, 0], ci_ref[gi, 0], cr_ref[gi, 1], ci_ref[gi, 1]], axis=0), ch_ref[...])
    out = []
    for d in range(2):
        e = _dotx(jnp.concatenate([pr_ref[gi, d], pi_ref[gi, d]], axis=0), pw_ref[d])
        c = c16[2 * d * n : (2 * d + 1) * n], c16[(2 * d + 1) * n : (2 * d + 2) * n]
        bst = jnp.concatenate([br_ref[gi, d], -bi_ref[gi, d]], axis=0)
        out.append(((e[:n, :CW], e[n:, :CW]), (e[:n, CW:], e[n:, CW:]), c, bst))
    return out


def _op_spread(m_ref, u_ref, v_ref, gi):
    return _dotx(m_ref[...], jnp.concatenate([_rows_pad(u_ref[gi]), _rows_pad(v_ref[gi])], axis=1))


def _cmul(x, y):
    return x[0] * y[0] - x[1] * y[1], x[0] * y[1] + x[1] * y[0]


def _cmul_conj(g, y):
    return g[0] * y[0] + g[1] * y[1], g[1] * y[0] - g[0] * y[1]


def _rows_pad(v):
    return jnp.concatenate([v, jnp.zeros((LANES - v.shape[0], v.shape[1]), F32)], axis=0)


def _s5_build(small, gb, shards):
    pr, pi, br, bi, cr, ci, a1, a2, b1, b2 = small
    g, _, n, _ = pr.shape
    t, h, ns = CHUNK_T, SSM_H, 4 * n
    pw, ch, rep_s, til_s = _op_consts()
    nr, nin, steps = len(shards), 14, g // gb

    def body(*refs):
        pr_ref, pi_ref, br_ref, bi_ref, cr_ref, ci_ref, a1_ref, a2_ref, b1_ref, b2_ref, pw_ref, ch_ref, rs_ref, ts_ref = refs[:nin]
        t_ref, w_ref, c_ref = refs[nin + nr : nin + nr + 3]
        start, finish = _gather_copies(refs[nin : nin + nr], refs[nin + nr + 3 : nin + 2 * nr + 3], *refs[nin + 2 * nr + 3 :])
        pl.when(pl.program_id(0) == 0)(start)

        def group(gi, carry):
            rows, outs = [], []
            for e16, e1, c16, bst in _op_factors(gi, pr_ref, pi_ref, br_ref, bi_ref, cr_ref, ci_ref, pw_ref, ch_ref):
                rows.append(_dotx(bst, jnp.concatenate(_cmul(c16, e16), axis=0), _mm_tn)[:h])
                outs.append(_cmul(c16, e1))
            c_ref[gi] = jnp.concatenate([outs[0][0], outs[1][0], -outs[0][1], -outs[1][1]], axis=0).astype(BF16)
            for s in range(t):
                t_ref[gi, s * h : (s + 1) * h, :] = (_shift_lanes(rows[0], h * s) + _shift_lanes(rows[1], -h * (t - 1 - s))).astype(BF16)
            ae, be = _op_spread(rs_ref, a1_ref, a2_ref, gi), _op_spread(ts_ref, b1_ref, b2_ref, gi)
            w_ref[gi] = (ae[:, :ns] * be[:, :ns] + ae[:, ns:] * be[:, ns:]).astype(BF16)
            return carry

        lax.fori_loop(0, gb, group, 0)
        pl.when(pl.program_id(0) == steps - 1)(finish)

    sm = pl.BlockSpec((gb, 2, n, LANES), lambda i: (i, 0, 0, 0))
    ab = lambda v: pl.BlockSpec((gb,) + v.shape[1:], lambda i: (i, 0, 0))
    out = pl.pallas_call(
        body,
        name="s5_build",
        grid=(steps,),
        in_specs=[sm] * 6 + [ab(a1), ab(a2), ab(b1), ab(b2), _whole(pw.shape), _whole(ch.shape), _whole(rep_s.shape), _whole(til_s.shape)] + [_HBM] * nr,
        out_specs=[_gspec(gb, CW, CW), _gspec(gb, CW, 4 * n), _gspec(gb, 4 * n, CW)] + [_HBM] * nr,
        out_shape=[_sds((g, CW, CW), BF16), _sds((g, CW, 4 * n), BF16), _sds((g, 4 * n, CW), BF16)] + _gather_shapes(shards),
        scratch_shapes=_gather_sems(nr),
        compiler_params=_cparams(1, VMEM_BIG),
    )(pr, pi, br, bi, cr, ci, a1, a2, b1, b2, pw, ch, rep_s, til_s, *shards)
    return out[:3], out[3:]


def _s5_build_bwd(small, dtsum, dwst, dccat, gb, parts):
    pr, pi, br, bi, cr, ci, a1, a2, b1, b2 = small
    g, _, n, _ = pr.shape
    t, h, ns = CHUNK_T, SSM_H, 4 * n
    pw, ch, rep_s, til_s = _op_consts()
    nr, nin, nout, steps = len(parts), 17, 10, g // gb

    def body(*refs):
        pr_ref, pi_ref, br_ref, bi_ref, cr_ref, ci_ref, a1_ref, a2_ref, b1_ref, b2_ref, pw_ref, ch_ref, rs_ref, ts_ref, dt_ref, dw_ref, dc_ref = refs[:nin]
        dpr_ref, dpi_ref, dbr_ref, dbi_ref, dcr_ref, dci_ref, da1_ref, da2_ref, db1_ref, db2_ref = refs[nin + nr : nin + nr + nout]
        start, finish = _exchange_copies(refs[nin : nin + nr], refs[nin + nr + nout : nin + 2 * nr + nout], *refs[nin + 2 * nr + nout :])
        pl.when(pl.program_id(0) == 0)(start)

        def group(gi, carry):
            drow = [jnp.zeros((h, CW), F32), jnp.zeros((h, CW), F32)]
            for s in range(t):
                blk = dt_ref[gi, s * h : (s + 1) * h, :]
                drow[0] = drow[0] + _shift_lanes(blk, -h * s)
                drow[1] = drow[1] + _shift_lanes(blk, h * (t - 1 - s))
            dcc = dc_ref[gi]
            dcs = []
            for d, (e16, e1, c16, bst) in enumerate(_op_factors(gi, pr_ref, pi_ref, br_ref, bi_ref, cr_ref, ci_ref, pw_ref, ch_ref)):
                dr = _rows_pad(drow[d])
                db = _dotx(jnp.concatenate(_cmul(c16, e16), axis=0), dr, _mm_nt)
                dbr_ref[gi, d] = db[:n]
                dbi_ref[gi, d] = -db[n:]
                dst = _dotx(bst, dr)
                dca = dst[:n], dst[n:]
                gx = dcc[d * n : (d + 1) * n], -dcc[(2 + d) * n : (3 + d) * n]
                dc16, dc16b = _cmul_conj(dca, e16), _cmul_conj(gx, e1)
                dcs += [dc16[0] + dc16b[0], dc16[1] + dc16b[1]]
                de = jnp.concatenate([jnp.concatenate(_cmul_conj(dca, c16), axis=0), jnp.concatenate(_cmul_conj(gx, c16), axis=0)], axis=1)
                dp = _dotx(de, pw_ref[d], _mm_nt)
                dpr_ref[gi, d] = dp[:n]
                dpi_ref[gi, d] = dp[n:]
            dct = _dotx(jnp.concatenate(dcs, axis=0), ch_ref[...], _mm_nt)
            for d in range(2):
                dcr_ref[gi, d] = dct[2 * d * n : (2 * d + 1) * n]
                dci_ref[gi, d] = dct[(2 * d + 1) * n : (2 * d + 2) * n]
            dw = dw_ref[gi]
            ae, be = _op_spread(rs_ref, a1_ref, a2_ref, gi), _op_spread(ts_ref, b1_ref, b2_ref, gi)
            da = _dotx(rs_ref[...], jnp.concatenate([dw * be[:, :ns], dw * be[:, ns:]], axis=1), _mm_tn)
            db = _dotx(ts_ref[...], jnp.concatenate([dw * ae[:, :ns], dw * ae[:, ns:]], axis=1), _mm_tn)
            da1_ref[gi] = da[:t, :ns]
            da2_ref[gi] = da[:t, ns:]
            db1_ref[gi] = db[:h, :ns]
            db2_ref[gi] = db[:h, ns:]
            return carry

        lax.fori_loop(0, gb, group, 0)
        pl.when(pl.program_id(0) == steps - 1)(finish)

    sm = pl.BlockSpec((gb, 2, n, LANES), lambda i: (i, 0, 0, 0))
    ab = lambda v: pl.BlockSpec((gb,) + v.shape[1:], lambda i: (i, 0, 0))
    ops = [_gspec(gb, CW, CW), _gspec(gb, CW, 4 * n), _gspec(gb, 4 * n, CW)]
    out = pl.pallas_call(
        body,
        name="s5_build_bwd",
        grid=(steps,),
        in_specs=[sm] * 6 + [ab(a1), ab(a2), ab(b1), ab(b2), _whole(pw.shape), _whole(ch.shape), _whole(rep_s.shape), _whole(til_s.shape)] + ops + [_HBM] * nr,
        out_specs=[sm] * 6 + [ab(a1), ab(a2), ab(b1), ab(b2)] + [_HBM] * nr,
        out_shape=[_sds(v.shape) for v in small] + [_sds(p.shape, p.dtype) for p in parts],
        scratch_shapes=_exchange_sems(nr),
        compiler_params=_cparams(1, VMEM_BIG),
    )(pr, pi, br, bi, cr, ci, a1, a2, b1, b2, pw, ch, rep_s, til_s, dtsum, dwst, dccat, *parts)
    return out[:nout], out[nout:]


SCAN_UNROLL = 4
GPL = LANES // SSM_H
CW = CHUNK_T * SSM_H
BW = CHUNK_T * LANES


def _chunk_perm():
    o = np.arange(BW)
    src = ((o % CW) // SSM_H) * LANES + (o // CW) * SSM_H + o % SSM_H
    return jnp.asarray(np.arange(BW)[:, None] == src[None, :], BF16)


def _gspec(gb, a, b):
    return pl.BlockSpec((gb, a, b), lambda i: (i, 0, 0))


def _chunk_rows(ref, n):
    return jnp.concatenate([ref[pl.ds(s, n, stride=CHUNK_T), :] for s in range(CHUNK_T)], axis=1)


def _s5_state(proj, col0, uc, perm, wst, ncc, ncl):
    nlb = uc.shape[1] // LANES
    g, _, ns = wst.shape
    nch = ncc + ncl
    half = ns // 2

    def body(p_ref, c_ref, pm_ref, w_ref, u_ref, sr_ref, si_ref):
        xrows = jnp.concatenate([_chunk_rows(c_ref, ncc), _chunk_rows(p_ref, ncl)], axis=0).astype(BF16)
        u = _mm(xrows, pm_ref[...]).astype(BF16)
        u_ref[0] = u
        for gi in range(GPL):
            s = _mm(u[:, gi * CW : (gi + 1) * CW], w_ref[gi].astype(BF16))
            sr_ref[gi] = s[:, :half]
            si_ref[gi] = s[:, half:]

    return pl.pallas_call(
        body,
        name="s5_state",
        grid=(nlb,),
        in_specs=[
            pl.BlockSpec((ncl * CHUNK_T, LANES), lambda i: (0, col0 + i)),
            pl.BlockSpec((ncc * CHUNK_T, LANES), lambda i: (0, i)),
            _whole(perm.shape),
            _gspec(GPL, CW, ns),
        ],
        out_specs=[pl.BlockSpec((1, nch, BW), lambda i: (i, 0, 0)), _gspec(GPL, nch, half), _gspec(GPL, nch, half)],
        out_shape=[_sds((nlb, nch, BW), BF16), _sds((g, nch, half)), _sds((g, nch, half))],
        compiler_params=_cparams(1, VMEM_BIG),
    )(proj, uc, perm, wst)


def _unrolled_loop(n, step, init):
    u = SCAN_UNROLL if n % SCAN_UNROLL == 0 else 1

    def trip(i, c):
        for k in range(u):
            c = step(i * u + k, c)
        return c

    return lax.fori_loop(0, n // u, trip, init)


def _lane_masks(gb, half):
    lane = lax.broadcasted_iota(jnp.int32, (gb, half), 1)
    return lane < (half // 2)


def _roll_half(v):
    return pltpu.roll(v, v.shape[-1] // 2, axis=v.ndim - 1)


def _pack_dirs(lo, hi, mf):
    return jnp.where(mf, lo, _roll_half(hi)), jnp.where(mf, _roll_half(lo), hi)


def _unpack_dirs(pf, pb, mf):
    return jnp.where(mf, pf, _roll_half(pb)), jnp.where(mf, _roll_half(pf), pb)


def _packed_spec(gb, nch, half, buffers=None):
    mode = {} if buffers is None else dict(pipeline_mode=pl.Buffered(buffers))
    if gb == 2 * GPL:
        return pl.BlockSpec((GPL, nch, half), lambda i: (i // 2, 0, 0), **mode)
    assert gb == GPL
    return pl.BlockSpec((gb // 2, nch, half), lambda i: (i, 0, 0), **mode)


def _pair_index(j, gb):
    if gb == 2 * GPL:
        return j, pl.program_id(0) % 2
    return j % (gb // 2), j // (gb // 2)


def _pick(which, lo, hi):
    return (hi if which else lo) if isinstance(which, int) else jnp.where(which == 0, lo, hi)


def _unpacked(pf_ref, pb_ref, j, gb, mf):
    jl, which = _pair_index(j, gb)
    return _pick(which, *_unpack_dirs(pf_ref[jl], pb_ref[jl], mf))


def _pair_tile(refs, jl, rows):
    return jnp.concatenate([r[jl, rows, :] for r in refs], axis=1)


def _pair_rows_of(op, which, q):
    z = jnp.zeros((q, op.shape[1]), op.dtype)
    blocks = [op[0:q], op[2 * q : 3 * q], op[q : 2 * q], op[3 * q : 4 * q]]
    lo = jnp.concatenate([x for b in blocks for x in (b, z)], axis=0)
    hi = jnp.concatenate([x for b in blocks for x in (z, b)], axis=0)
    return _pick(which, lo, hi)


def _group_rows_of(pair_rows, which, q):
    def take(off):
        blk = lambda k: pair_rows[2 * q * k + off : 2 * q * k + off + q]
        return jnp.concatenate([blk(0), blk(2), blk(1), blk(3)], axis=0)

    return _pick(which, take(0), take(q))


def _scan_rows(ref, row, hp, nch, mf):
    return _pack_dirs(ref[pl.ds(row, hp, stride=nch), :], ref[pl.ds(hp * nch + row, hp, stride=nch), :], mf)


def _s5_scan_fwd(s_re, s_im, a16, ncc, nch, gb):
    g, ns = a16.shape
    half = ns // 2
    hp = gb // 2

    def body(sr_ref, si_ref, a_ref, hfr_ref, hfi_ref, hbr_ref, hbi_ref):
        mf = _lane_masks(hp, half)
        afr, abr = _pack_dirs(a_ref[:hp, :half], a_ref[hp:, :half], mf)
        afi, abi = _pack_dirs(a_ref[:hp, half:], a_ref[hp:, half:], mf)
        zero = jnp.zeros((hp, half), F32)

        def step(t, c):
            fr, fi, br, bi = c
            rb = jnp.where(t < ncc, ncc - 1 - t, nch - 1 - (t - ncc))
            hfr_ref[pl.ds(t, hp, stride=nch), :] = fr
            hfi_ref[pl.ds(t, hp, stride=nch), :] = fi
            hbr_ref[pl.ds(rb, hp, stride=nch), :] = br
            hbi_ref[pl.ds(rb, hp, stride=nch), :] = bi
            sfr, sfi = _scan_rows(sr_ref, t, hp, nch, mf)[0], _scan_rows(si_ref, t, hp, nch, mf)[0]
            sbr, sbi = _scan_rows(sr_ref, rb, hp, nch, mf)[1], _scan_rows(si_ref, rb, hp, nch, mf)[1]
            return (afr * fr - afi * fi + sfr, afr * fi + afi * fr + sfi, abr * br - abi * bi + sbr, abr * bi + abi * br + sbi)

        _unrolled_loop(nch, step, (zero, zero, zero, zero))

    blk = pl.BlockSpec((gb * nch, half), lambda i: (i, 0))
    pblk = pl.BlockSpec((hp * nch, half), lambda i: (i, 0))
    return pl.pallas_call(
        body,
        name="s5_scan_fwd",
        grid=(g // gb,),
        in_specs=[blk, blk, pl.BlockSpec((gb, ns), lambda i: (i, 0))],
        out_specs=[pblk] * 4,
        out_shape=[_sds((g // 2 * nch, half))] * 4,
        compiler_params=_cparams(1, VMEM_BIG),
    )(s_re, s_im, a16)


def _token_rows_store(ref, val, n):
    for s in range(CHUNK_T):
        ref[pl.ds(s, n, stride=CHUNK_T), :] = val[:, s * LANES : (s + 1) * LANES]


def _s5_out(u_all, h4, tsum, ccat, perm, ncc, lx, gb):
    nlb, nch, _ = u_all.shape
    g, ns, _ = ccat.shape
    ncl = nch - ncc
    half = ns // 2

    def body(u_ref, hfr_ref, hfi_ref, hbr_ref, hbi_ref, t_ref, c_ref, pm_ref, y_ref):
        parts = []
        for gi in range(GPL):
            u = u_ref[0, ncc:, gi * CW : (gi + 1) * CW]
            jl, which = _pair_index(gi, gb)
            hs = _pair_tile((hfr_ref, hfi_ref, hbr_ref, hbi_ref), jl, slice(ncc, None)).astype(BF16)
            parts.append(_mm(u, t_ref[gi].astype(BF16)) + _mm(hs, _pair_rows_of(c_ref[gi].astype(BF16), which, ns // 4)))
        _token_rows_store(y_ref, _mm_nt(jnp.concatenate(parts, axis=1).astype(BF16), pm_ref[...]), ncl)

    return pl.pallas_call(
        body,
        name="s5_out",
        grid=(nlb,),
        in_specs=[pl.BlockSpec((1, nch, BW), lambda i: (i, 0, 0))] + [_packed_spec(gb, nch, half)] * 4 + [_gspec(GPL, CW, CW), _gspec(GPL, ns, CW), _whole(perm.shape)],
        out_specs=pl.BlockSpec((lx, LANES), lambda i: (0, i)),
        out_shape=_sds((lx, nlb * LANES)),
        compiler_params=_cparams(1, VMEM_BIG),
    )(u_all, *h4, tsum, ccat, perm)


def _s5_dstate(dypre, perm, ccat, ncc, swaps=()):
    lx, sw = dypre.shape
    nr = len(swaps)
    nlb = sw // LANES
    g, ns, _ = ccat.shape
    ncl = lx // CHUNK_T
    nch = ncl + ncc
    half = ns // 2

    def body(dy_ref, pm_ref, c_ref, dyr_ref, dr_ref, di_ref):
        dy = _mm(_chunk_rows(dy_ref, ncl).astype(BF16), pm_ref[...]).astype(BF16)
        dyr_ref[0] = dy
        for gi in range(GPL):
            dh = _mm_nt(dy[:, gi * CW : (gi + 1) * CW], c_ref[gi].astype(BF16))
            dr_ref[gi, :ncc, :] = jnp.zeros((ncc, half), F32)
            di_ref[gi, :ncc, :] = jnp.zeros((ncc, half), F32)
            dr_ref[gi, ncc:, :] = dh[:, :half]
            di_ref[gi, ncc:, :] = dh[:, half:]

    out = pl.pallas_call(
        _riding(body, 3, 3, nr, (nlb,), _swap_copies),
        name="s5_dstate",
        grid=(nlb,),
        in_specs=[pl.BlockSpec((lx, LANES), lambda i: (0, i)), _whole(perm.shape), _gspec(GPL, ns, CW)] + [_HBM] * nr,
        out_specs=[pl.BlockSpec((1, ncl, BW), lambda i: (i, 0, 0)), _gspec(GPL, nch, half), _gspec(GPL, nch, half)] + [_HBM] * nr,
        out_shape=[_sds((nlb, ncl, BW), BF16), _sds((g, nch, half)), _sds((g, nch, half))] + _swap_shapes(swaps),
        scratch_shapes=_swap_sems(nr) if nr else [],
        compiler_params=_cparams(1, VMEM_BIG),
    )(dypre, perm, ccat, *swaps)
    return out[:3], out[3:]


def _s5_scan_bwd(dh_re, dh_im, a16, ncc, nch, gb):
    g, ns = a16.shape
    half = ns // 2
    ncl = nch - ncc
    hp = gb // 2

    def body(dr_ref, di_ref, a_ref, sfr_ref, sfi_ref, sbr_ref, sbi_ref):
        mf = _lane_masks(hp, half)
        afr, abr = _pack_dirs(a_ref[:hp, :half], a_ref[hp:, :half], mf)
        afi, abi = _pack_dirs(a_ref[:hp, half:], a_ref[hp:, half:], mf)
        zero = jnp.zeros((hp, half), F32)

        def step(t, c):
            lfr, lfi, lbr, lbi = c
            pf = nch - 1 - t
            pb = jnp.where(t < ncl, ncc + t, t - ncl)
            sfr_ref[pl.ds(pf, hp, stride=nch), :] = lfr
            sfi_ref[pl.ds(pf, hp, stride=nch), :] = lfi
            sbr_ref[pl.ds(pb, hp, stride=nch), :] = lbr
            sbi_ref[pl.ds(pb, hp, stride=nch), :] = lbi
            dfr, dfi = _scan_rows(dr_ref, pf, hp, nch, mf)[0], _scan_rows(di_ref, pf, hp, nch, mf)[0]
            dbr, dbi = _scan_rows(dr_ref, pb, hp, nch, mf)[1], _scan_rows(di_ref, pb, hp, nch, mf)[1]
            return (dfr + afr * lfr + afi * lfi, dfi + afr * lfi - afi * lfr, dbr + abr * lbr + abi * lbi, dbi + abr * lbi - abi * lbr)

        _unrolled_loop(nch, step, (zero,) * 4)

    blk = pl.BlockSpec((gb * nch, half), lambda i: (i, 0))
    pblk = pl.BlockSpec((hp * nch, half), lambda i: (i, 0))
    return pl.pallas_call(
        body,
        name="s5_scan_bwd",
        grid=(g // gb,),
        in_specs=[blk, blk, pl.BlockSpec((gb, ns), lambda i: (i, 0))],
        out_specs=[pblk] * 4,
        out_shape=[_sds((g // 2 * nch, half))] * 4,
        compiler_params=_cparams(1, VMEM_BIG),
    )(dh_re, dh_im, a16)


def _s5_du(u_all, dy_all, ds4, tsum, wst, perm, ncc, lx, lc, gb):
    nlb, ncl, _ = dy_all.shape
    g, _, ns = wst.shape
    nch = ncc + ncl
    half = ns // 2

    def body(u_ref, dy_ref, sfr_ref, sfi_ref, sbr_ref, sbi_ref, t_ref, w_ref, pm_ref, du_ref, dc_ref, dw_ref):
        parts = []
        mf = _lane_masks(nch, half)
        for gi in range(GPL):
            ds = jnp.concatenate([_unpacked(sfr_ref, sbr_ref, gi, gb, mf), _unpacked(sfi_ref, sbi_ref, gi, gb, mf)], axis=1).astype(BF16)
            dw_ref[gi] = _mm_tn(u_ref[0, :, gi * CW : (gi + 1) * CW], ds)
            d_all = _mm_nt(ds, w_ref[gi].astype(BF16))
            d_lat = d_all[ncc:] + _mm_nt(dy_ref[0, :, gi * CW : (gi + 1) * CW], t_ref[gi].astype(BF16))
            parts.append(jnp.concatenate([d_all[:ncc], d_lat], axis=0))
        du = _mm_nt(jnp.concatenate(parts, axis=1).astype(BF16), pm_ref[...])
        _token_rows_store(dc_ref, du[:ncc], ncc)
        _token_rows_store(du_ref, du[ncc:], ncl)

    return pl.pallas_call(
        body,
        name="s5_du",
        grid=(nlb,),
        in_specs=[pl.BlockSpec((1, nch, BW), lambda i: (i, 0, 0)), pl.BlockSpec((1, ncl, BW), lambda i: (i, 0, 0))]
        + [_packed_spec(gb, nch, half)] * 4
        + [_gspec(GPL, CW, CW), _gspec(GPL, CW, ns), _whole(perm.shape)],
        out_specs=[pl.BlockSpec((lx, LANES), lambda i: (0, i)), pl.BlockSpec((lc, LANES), lambda i: (0, i)), _gspec(GPL, CW, ns)],
        out_shape=[_sds((lx, nlb * LANES)), _sds((lc, nlb * LANES)), _sds((g, CW, ns))],
        compiler_params=_cparams(1, VMEM_BIG),
    )(u_all, dy_all, *ds4, tsum, wst, perm)


def _s5_op_grads(u_all, dy_all, ds4, h4, ncc, gb):
    nlb, nch, _ = u_all.shape
    half = ds4[0].shape[2]
    g = 2 * ds4[0].shape[0]
    ns = 2 * half

    def body(u_ref, dy_ref, sfr_ref, sfi_ref, sbr_ref, sbi_ref, hfr_ref, hfi_ref, hbr_ref, hbi_ref, dt_ref, dc_ref, da_ref):
        das = []
        mf1 = _lane_masks(1, half)
        for gi in range(GPL):
            cols = slice(gi * CW, (gi + 1) * CW)
            jl, which = _pair_index(gi, gb)
            dy = dy_ref[0, :, cols]
            dt_ref[gi] = _mm_tn(u_ref[0, ncc:, cols], dy)
            hs = _pair_tile((hfr_ref, hfi_ref, hbr_ref, hbi_ref), jl, slice(ncc, None)).astype(BF16)
            dc_ref[gi] = _group_rows_of(_mm_tn(hs, dy), which, ns // 4)
            hfr, hfi, hbr, hbi = hfr_ref[jl], hfi_ref[jl], hbr_ref[jl], hbi_ref[jl]
            sfr, sfi, sbr, sbi = sfr_ref[jl], sfi_ref[jl], sbr_ref[jl], sbi_ref[jl]
            a_re = _pick(which, *_unpack_dirs(_colsum(hfr * sfr + hfi * sfi), _colsum(hbr * sbr + hbi * sbi), mf1))
            a_im = _pick(which, *_unpack_dirs(_colsum(hfr * sfi - hfi * sfr), _colsum(hbr * sbi - hbi * sbr), mf1))
            das.append(jnp.concatenate([a_re, a_im], axis=1))
        da_ref[...] = jnp.concatenate(das, axis=0)

    return pl.pallas_call(
        body,
        name="s5_op_grads",
        grid=(nlb,),
        in_specs=[pl.BlockSpec((1, nch, BW), lambda i: (i, 0, 0)), pl.BlockSpec((1, nch - ncc, BW), lambda i: (i, 0, 0))]
        + [_packed_spec(gb, nch, half)] * 4
        + [_packed_spec(gb, nch, half, buffers=1)] * 4,
        out_specs=[_gspec(GPL, CW, CW), _gspec(GPL, ns, CW), pl.BlockSpec((GPL, ns), lambda i: (i, 0))],
        out_shape=[_sds((g, CW, CW)), _sds((g, ns, CW)), _sds((g, ns))],
        compiler_params=_cparams(1, VMEM_BIG),
    )(u_all, dy_all, *ds4, *h4)


def _merge_fwd(y_ssm, proj, lin, x, tgt, glu_wg, out_w, d_skip, glu_b, pscale, gate, fin_g, tm):
    lx, d = x.shape
    sw = y_ssm.shape[1]
    mix = out_w.shape[0]
    pw = mix - sw
    cb2 = glu_wg.shape[2]
    nh = NDEV // 2

    nb = lx // tm

    def body(y_ref, u_ref, z_ref, l_ref, x_ref, t_ref, gw_ref, ow_ref, dk_ref, gb_ref, ps_ref, gt_ref, fg_ref,
             ypre_ref, y12_ref, br_ref, dxo_ref, loss_ref, dfg_ref, dgt_ref, mix_ref):
        @pl.when(pl.program_id(0) == 0)
        def _():
            loss_ref[...] = jnp.zeros_like(loss_ref)
            dfg_ref[...] = jnp.zeros_like(dfg_ref)
            dgt_ref[...] = jnp.zeros_like(dgt_ref)
            mix_ref[...] = jnp.zeros_like(mix_ref)

        live = pl.program_id(0) > 0
        mixv = mix_ref[...]
        xo = x_ref[...] + gt_ref[...] * mixv
        r2 = lax.rsqrt(_rowmean(xo * xo) + EPS)
        xh = xo * r2
        fg = fg_ref[...]
        e = xh * fg - t_ref[...]
        loss_ref[...] += jnp.where(live, 0.5 * _colsum(_rowmean(e * e)), 0.0)
        dy = e * (1.0 / d)
        dfg_ref[...] += jnp.where(live, _colsum(dy * xh), 0.0)
        gy = dy * fg
        dxo = r2 * (gy - xh * _rowmean(gy * xh))
        dxo_ref[...] = dxo
        dgt_ref[...] += jnp.where(live, _colsum(dxo * mixv), 0.0)

        ypre = y_ref[...] + dk_ref[...] * u_ref[...]
        ypre_ref[...] = ypre
        yg = _gelu(ypre).astype(BF16)
        outs = []
        for j in range(nh):
            y1 = _mm(yg, gw_ref[j]) + gb_ref[:, j * cb2 : (j + 1) * cb2]
            y2 = _mm(yg, gw_ref[nh + j]) + gb_ref[:, (nh + j) * cb2 : (nh + j + 1) * cb2]
            y12_ref[:, j * cb2 : (j + 1) * cb2] = y1
            y12_ref[:, (nh + j) * cb2 : (nh + j + 1) * cb2] = y2
            outs.append(y1 * _sigmoid(y2))
        lin = l_ref[...] * ps_ref[...]
        brb = (jnp.concatenate([lin] + outs, axis=1) * _silu(z_ref[...])).astype(BF16)
        br_ref[...] = brb
        mix_ref[...] = _mm(brb, ow_ref[...])

    vec = _whole((1, d))
    cur = lambda c, col=0: pl.BlockSpec((tm, c), lambda i: (jnp.minimum(i, nb - 1), col))
    prev = lambda c: pl.BlockSpec((tm, c), lambda i: (jnp.maximum(i - 1, 0), 0))
    return pl.pallas_call(
        body,
        name="merge_fwd",
        grid=(nb + 1,),
        in_specs=[cur(sw), cur(sw, 1), cur(mix, 1), cur(pw), prev(d), prev(d), _whole(glu_wg.shape), _whole(out_w.shape), _whole((1, sw)), _whole((1, 2 * sw)), _whole((1, pw)), vec, vec],
        out_specs=[cur(sw), cur(2 * sw), cur(mix), prev(d), _acc((1, 1)), _acc((1, d)), _acc((1, d))],
        out_shape=[_sds((lx, sw)), _sds((lx, 2 * sw)), _sds((lx, mix), BF16), _sds((lx, d)), _sds((1, 1)), _sds((1, d)), _sds((1, d))],
        scratch_shapes=[pltpu.VMEM((tm, d), F32)],
        compiler_params=_cparams(1, VMEM_BIG),
    )(y_ssm, proj, proj, lin, x, tgt, glu_wg, out_w, d_skip, glu_b, pscale, gate, fin_g)


def _out_bwd(dxo, gate, branch, out_w, lin, y12, proj, pscale, tm):
    lx, d = dxo.shape
    mix = out_w.shape[0]
    pw = lin.shape[1]
    sw = mix - pw
    nb = lx // tm

    def body(dxo_ref, gt_ref, br_ref, ow_ref, l_ref, y_ref, z_ref, ps_ref, dz_ref, dl_ref, dg_ref, dps_ref, dow_hbm, acc, sem):
        i = pl.program_id(0)

        @pl.when(i == 0)
        def _():
            acc[...] = jnp.zeros_like(acc)
            dps_ref[...] = jnp.zeros_like(dps_ref)

        dmix = (dxo_ref[...] * gt_ref[...]).astype(BF16)
        dbr = _mm_nt(dmix, ow_ref[...])
        acc[...] += _mm_tn(br_ref[...], dmix)
        z = z_ref[...]
        sg = _sigmoid(z)
        dbp = dbr * (z * sg)
        y1, y2 = y_ref[:, :sw], y_ref[:, sw:]
        s2 = _sigmoid(y2)
        ps = ps_ref[...]
        lin = l_ref[...]
        bp = jnp.concatenate([lin * ps, y1 * s2], axis=1)
        dz_ref[...] = dbr * bp * (sg * (1.0 + z * (1.0 - sg)))
        dlp = dbp[:, :pw]
        dl_ref[...] = dlp * ps
        dps_ref[...] += _colsum(dlp * lin)
        dss = dbp[:, pw:]
        dg_ref[...] = jnp.concatenate([dss * s2, dss * y1 * s2 * (1.0 - s2)], axis=1).astype(BF16)

        @pl.when(i == nb - 1)
        def _():
            cp = pltpu.make_async_copy(acc, dow_hbm, sem)
            cp.start()
            cp.wait()

    return pl.pallas_call(
        body,
        name="out_bwd",
        grid=(nb,),
        in_specs=[_rows(tm, d), _whole((1, d)), _rows(tm, mix), _whole(out_w.shape), _rows(tm, pw), _rows(tm, 2 * sw), _rows(tm, mix, 1), _whole((1, pw))],
        out_specs=[_rows(tm, mix), _rows(tm, pw), _rows(tm, 2 * sw), _acc((1, pw)), _ANY],
        out_shape=[_sds((lx, mix)), _sds((lx, pw)), _sds((lx, 2 * sw), BF16), _sds((1, pw)), _sds((mix, d))],
        scratch_shapes=[pltpu.VMEM((mix, d), F32), pltpu.SemaphoreType.DMA],
        compiler_params=_cparams(1, VMEM_BIG),
    )(dxo, gate, branch, out_w, lin, y12, proj, pscale)


def _glu_bwd(dg12, ypre, proj, glu_wg, d_skip, tm):
    lx, sw = ypre.shape
    cb2 = glu_wg.shape[2]
    nb = lx // tm

    def body(dg_ref, yp_ref, u_ref, gw_ref, dk_ref, dyp_ref, dgb_ref, dd_ref, dgw_hbm, acc, sem):
        i = pl.program_id(0)

        @pl.when(i == 0)
        def _():
            acc[...] = jnp.zeros_like(acc)
            dgb_ref[...] = jnp.zeros_like(dgb_ref)
            dd_ref[...] = jnp.zeros_like(dd_ref)

        ypre = yp_ref[...]
        ygb = _gelu(ypre).astype(BF16)
        dg = dg_ref[...]
        dyg = jnp.zeros((tm, sw), F32)
        for j in range(NDEV):
            dgj = dg[:, j * cb2 : (j + 1) * cb2]
            dyg = dyg + _mm_nt(dgj, gw_ref[j])
            acc[j] += _mm_tn(ygb, dgj)
        dgb_ref[...] += _colsum(dg.astype(F32))
        dyp = dyg * _gelu_grad(ypre)
        dyp_ref[...] = dyp
        dd_ref[...] += _colsum(dyp * u_ref[...])

        @pl.when(i == nb - 1)
        def _():
            cp = pltpu.make_async_copy(acc, dgw_hbm, sem)
            cp.start()
            cp.wait()

    return pl.pallas_call(
        body,
        name="glu_bwd",
        grid=(nb,),
        in_specs=[_rows(tm, 2 * sw), _rows(tm, sw), _rows(tm, sw, 1), _whole(glu_wg.shape), _whole((1, sw))],
        out_specs=[_rows(tm, sw), _acc((1, 2 * sw)), _acc((1, sw)), _ANY],
        out_shape=[_sds((lx, sw)), _sds((1, 2 * sw)), _sds((1, sw)), _sds(glu_wg.shape)],
        scratch_shapes=[pltpu.VMEM(glu_wg.shape, F32), pltpu.SemaphoreType.DMA],
        compiler_params=_cparams(1, VMEM_BIG),
    )(dg12, ypre, proj, glu_wg, d_skip)


def _pack(parts, total):
    flat = jnp.concatenate([p.reshape(-1).astype(F32) for p in parts])
    return jnp.pad(flat, (0, total - flat.shape[0]))


def _unpack(flat, shapes):
    out, off = [], 0
    for s in shapes:
        n = int(np.prod(s)) if len(s) else 1
        out.append(flat[off : off + n].reshape(s))
        off += n
    return out


def kernel(x, c, ctx, c_ctx, ada_w, ada_b, norm_g, in_w, pool_w, pool_scale, s5_lam_re, s5_lam_im, s5_log_dt, s5_b_re, s5_b_im, s5_c_re, s5_c_im, s5_d, glu_w, glu_b, out_w, final_g, loss_target, m_c_ctx, m_ada_w, m_ada_b, m_norm_g, m_in_w, m_pool_w, m_pool_scale, m_s5_lam_re, m_s5_lam_im, m_s5_log_dt, m_s5_b_re, m_s5_b_im, m_s5_c_re, m_s5_c_im, m_s5_d, m_glu_w, m_glu_b, m_out_w, m_final_g, v_c_ctx, v_ada_w, v_ada_b, v_norm_g, v_in_w, v_pool_w, v_pool_scale, v_s5_lam_re, v_s5_lam_im, v_s5_log_dt, v_s5_b_re, v_s5_b_im, v_s5_c_re, v_s5_c_im, v_s5_d, v_glu_w, v_glu_b, v_out_w, v_final_g):
    xr, tgt, xc = x[0], loss_target[0], ctx[0]
    lx, d = xr.shape
    lc = xc.shape[0]
    mix = out_w.shape[1] * NDEV
    sw = glu_w.shape[1]
    pw = mix - sw
    pgw = pw // len(POOL_WINDOWS)
    ngrp = sw // SSM_H
    cb = in_w.shape[2]
    cb2 = glu_w.shape[2]
    nmod = ada_w.shape[2]
    ncc, ncl = lc // CHUNK_T, lx // CHUNK_T
    nch = ncc + ncl
    tm = min(256, lx)
    tm_acc = min(512, lx)
    assert mix == d and pw == sw and 2 * cb == pw and NDEV * cb2 == 2 * sw and lx % GRID_W == 0
    mx, my, mc = _my_pos()
    me = 4 * mx + 2 * my + mc

    (c_all,) = _all_gather([c], "gather_c")
    call = jnp.concatenate([c_all.reshape(NDEV, d), c_ctx.reshape(1, d), jnp.zeros((NDEV - 1, d), F32)], axis=0)
    ada_w_l = ada_w[0]
    ada_b_l = lax.dynamic_slice_in_dim(ada_b, me * nmod, nmod, axis=1)
    m_loc = _mod_fwd(call, ada_w_l, ada_b_l)
    (m_all,) = _all_gather([m_loc], "gather_mod")
    mod = lax.dynamic_index_in_dim(m_all, me, axis=1, keepdims=False).reshape(1, NDEV * nmod)
    mod_c = m_all[:, NDEV, :].reshape(1, NDEV * nmod)
    shift, scale, gate = mod[:, :d], mod[:, d : 2 * d], mod[:, 2 * d :]
    shift_c, scale_c = mod_c[:, :d], mod_c[:, d : 2 * d]

    s5_params = tuple(p[0] for p in (s5_lam_re, s5_lam_im, s5_log_dt, s5_b_re, s5_b_im, s5_c_re, s5_c_im))
    (small, a16), small_vjp = jax.vjp(_s5_small, *s5_params)
    gb_ops = min(8, ngrp)
    (tsum, wst, ccat), (in_wg,) = _s5_build(small, gb_ops, [in_w[0].astype(BF16)])

    ssm_blocks = (2, 3)
    (proj, hb), (glu_wg, out_wg, pool_wg) = _in_proj(
        xr, norm_g, scale, shift, in_wg, tuple(range(NDEV)), tm, "in_proj",
        shards=[glu_w[0].astype(BF16), out_w[0].astype(BF16), pool_w[0].reshape(-1, pgw).astype(BF16)],
    )
    out_wf = out_wg.reshape(mix, d)
    pool_wf = pool_wg.reshape(NDEV, len(POOL_WINDOWS), pgw // NDEV, pgw).transpose(1, 0, 2, 3).reshape(len(POOL_WINDOWS), pgw, pgw)
    (uc, hcb), _ = _in_proj(xc, norm_g, scale_c, shift_c, in_wg, ssm_blocks, min(tm, lc), "in_proj_ctx")
    lin, dmat = _pool_fwd(proj, pool_wf)

    gb_scan = min(16, ngrp)
    half = wst.shape[2] // 2
    perm = _chunk_perm()
    flat = lambda a: a.reshape(ngrp * nch, half)
    pairs = lambda arrs: [a.reshape(ngrp // 2, nch, half) for a in arrs]
    u_all, s_re, s_im = _s5_state(proj, pw // LANES, uc, perm, wst, ncc, ncl)
    h4 = pairs(_s5_scan_fwd(flat(s_re), flat(s_im), a16, ncc, nch, gb_scan))
    y_ssm = _s5_out(u_all, h4, tsum, ccat, perm, ncc, lx, gb_scan)

    ypre, y12, branch, dxo, loss_l, dfg, dgate = _merge_fwd(
        y_ssm, proj, lin, xr, tgt, glu_wg, out_wf, s5_d, glu_b, pool_scale, gate, final_g.reshape(1, d), tm
    )

    dz, dlin, dg12, dps, d_out_w = _out_bwd(dxo, gate, branch, out_wf, lin, y12, proj, pool_scale, tm)
    dypre, dglu_b, dd_skip, d_glu_w = _glu_bwd(dg12, ypre, proj, glu_wg, s5_d, tm_acc)

    d_glu_out = [d_glu_w, d_out_w.reshape(NDEV, mix // NDEV, d)]
    (dy_all, dh_re, dh_im), got_glu_out = _s5_dstate(dypre, perm, ccat, ncc, swaps=[_by_chip_and_core(a) for a in d_glu_out])
    ds4 = pairs(_s5_scan_bwd(flat(dh_re), flat(dh_im), a16, ncc, nch, gb_scan))
    du5, duc, dwst = _s5_du(u_all, dy_all, ds4, tsum, wst, perm, ncc, lx, lc, gb_scan)
    dtsum, dccat, da16 = _s5_op_grads(u_all, dy_all, ds4, h4, ncc, gb_scan)
    dup, dpool_w = _pool_bwd(dlin, pool_wf, dmat)

    npw = len(POOL_WINDOWS)
    dpool_s = dpool_w.reshape(npw, NDEV, pgw // NDEV, pgw).transpose(1, 0, 2, 3).reshape(NDEV, npw * (pgw // NDEV), pgw)
    parts_a = _pair_partials(d_glu_out, [BF16] * 2, "grads_a", got=got_glu_out)
    dpcb, dsc_c, dsh_c, dg_c = _in_bwd_ctx(xc, duc, in_wg, ssm_blocks, norm_g, scale_c, shift_c)
    grad_x, dpb, dsc, dsh, dnorm_g = _in_bwd(xr, dxo, dup, du5, dypre, dz, in_wg, norm_g, scale, shift, s5_d, dg_c, tm)
    dmod = jnp.concatenate(
        [jnp.concatenate([dsh, dsc, dgate], axis=1), jnp.concatenate([dsh_c, dsc_c, jnp.zeros((1, d), F32)], axis=1)], axis=0
    )
    d_in_w, (land_glu, land_out), (dmod_all,) = _in_w_grad(hb, dpb, hcb, dpcb, cb, min(2 * tm_acc, lx), parts_a, [dmod])

    dmod_l = lax.dynamic_slice_in_dim(dmod_all, me * nmod, nmod, axis=2)
    g_ada_w, g_ada_b_l, cctx_part = _mod_bwd(call, dmod_l[:, 0, :], dmod_l[:, 1, :], ada_w_l)

    parts_b = _pair_partials([d_in_w, dpool_s], [BF16] * 2, "grads_b")
    dsmall, (land_in, land_pool) = _s5_build_bwd(small, dtsum, dwst, dccat, gb_ops, parts_b)
    ds5 = small_vjp((tuple(dsmall), da16))

    big_names = ["s5_b_re", "s5_b_im", "s5_c_re", "s5_c_im"]
    small_names = ["norm_g", "pool_scale", "s5_lam_re", "s5_lam_im", "s5_log_dt", "s5_d", "glu_b", "final_g", "c_ctx"]
    small_w = dict(norm_g=norm_g, pool_scale=pool_scale, s5_lam_re=s5_lam_re, s5_lam_im=s5_lam_im, s5_log_dt=s5_log_dt, s5_b_re=s5_b_re, s5_b_im=s5_b_im, s5_c_re=s5_c_re, s5_c_im=s5_c_im, s5_d=s5_d, glu_b=glu_b, final_g=final_g, c_ctx=c_ctx)
    small_m = dict(norm_g=m_norm_g, pool_scale=m_pool_scale, s5_lam_re=m_s5_lam_re, s5_lam_im=m_s5_lam_im, s5_log_dt=m_s5_log_dt, s5_b_re=m_s5_b_re, s5_b_im=m_s5_b_im, s5_c_re=m_s5_c_re, s5_c_im=m_s5_c_im, s5_d=m_s5_d, glu_b=m_glu_b, final_g=m_final_g, c_ctx=m_c_ctx)
    small_v = dict(norm_g=v_norm_g, pool_scale=v_pool_scale, s5_lam_re=v_s5_lam_re, s5_lam_im=v_s5_lam_im, s5_log_dt=v_s5_log_dt, s5_b_re=v_s5_b_re, s5_b_im=v_s5_b_im, s5_c_re=v_s5_c_re, s5_c_im=v_s5_c_im, s5_d=v_s5_d, glu_b=v_glu_b, final_g=v_final_g, c_ctx=v_c_ctx)
    small_g = dict(norm_g=dnorm_g, pool_scale=dps, s5_lam_re=ds5[0], s5_lam_im=ds5[1], s5_log_dt=ds5[2], s5_b_re=ds5[3], s5_b_im=ds5[4], s5_c_re=ds5[5], s5_c_im=ds5[6], s5_d=dd_skip, glu_b=dglu_b, final_g=dfg, c_ctx=cctx_part)
    shapes = [small_w[k].shape for k in small_names]
    nsmall = sum(int(np.prod(s)) for s in shapes) + 1
    unit = NDEV * 8 * LANES
    tot_s = -(-nsmall // unit) * unit
    big_n = [int(np.prod(small_w[k].shape)) for k in big_names]
    assert all(n % unit == 0 for n in big_n)
    tot = tot_s + sum(big_n)
    rows = tot // LANES
    gpack = jnp.concatenate(
        [_pack([small_g[k] for k in small_names] + [loss_l], tot_s)] + [small_g[k].reshape(-1) for k in big_names]
    ).reshape(NDEV, rows // NDEV, LANES)
    (land_small,) = _reduce_scatter([gpack], [F32], "small")
    gsum = _sum_slots(land_small, "sum_small")
    gall, g_ada_b_all = _all_gather([gsum, g_ada_b_l], "gather_small")
    gflat = gall.reshape(tot)
    loss = gflat[nsmall - 1]

    two_d = lambda a: a.reshape(-1, a.shape[-1])
    snames = small_names + ["ada_b"]
    sw = [small_w[k] for k in small_names] + [ada_b]
    sg = _unpack(gflat, shapes) + [g_ada_b_all.reshape(ada_b.shape)]
    sd, sm, sv = _adam_many(
        [two_d(a) for a in sg],
        [two_d(a) for a in sw],
        [two_d(small_m[k]) for k in small_names] + [two_d(m_ada_b)],
        [two_d(small_v[k]) for k in small_names] + [two_d(v_ada_b)],
        "adam_small",
    )
    res = {key: {k: a.reshape(w.shape) for k, a, w in zip(snames, vals, sw)} for key, vals in (("g", sg), ("d", sd), ("m", sm), ("v", sv))}

    def shard(name, g, land, w, m, v):
        shp = w.shape
        w2, m2_, v2_ = (a.reshape(-1, shp[-1]) for a in (w, m, v))
        out = _adam(g, w2, m2_, v2_, "adam_" + name, land=land)
        for k, a in zip(("g", "d", "m", "v"), out):
            res[k][name] = a.reshape(shp)

    shard("ada_w", g_ada_w, None, ada_w, m_ada_w, v_ada_w)
    shard("in_w", None, land_in, in_w, m_in_w, v_in_w)
    shard("pool_w", None, land_pool, pool_w, m_pool_w, v_pool_w)
    shard("glu_w", None, land_glu, glu_w, m_glu_w, v_glu_w)
    shard("out_w", None, land_out, out_w, m_out_w, v_out_w)
    off = tot_s
    for k, n in zip(big_names, big_n):
        shard(k, gflat[off : off + n].reshape(-1, small_w[k].shape[-1]), None, small_w[k], small_m[k], small_v[k])
        off += n

    names = ["c_ctx", "ada_w", "ada_b", "norm_g", "in_w", "pool_w", "pool_scale", "s5_lam_re", "s5_lam_im", "s5_log_dt", "s5_b_re", "s5_b_im", "s5_c_re", "s5_c_im", "s5_d", "glu_w", "glu_b", "out_w", "final_g"]
    return (loss, grad_x[None], *[res["g"][n] for n in names], *[res["d"][n] for n in names], *[res["m"][n] for n in names], *[res["v"][n] for n in names])
```
